```python
import math
import jax, jax.numpy as jnp
from jax import lax
import numpy as np

D_MODEL = 1024
BATCH = 16
SEQ = 2048
DEPTH = 1

ATTN_HEADS = 8
KV_HEADS = 2
Q_PER_KV = ATTN_HEADS // KV_HEADS
HEAD_DIM = 64
D_ATTN = ATTN_HEADS * HEAD_DIM
D_KV = KV_HEADS * HEAD_DIM
WINDOW = 128
BLOCK = 128
NUM_BUCKETS = 32
MAX_DISTANCE = 128
D_SSM = 512
SSM_GROUP = 16
N_SSM_GROUPS = D_SSM // SSM_GROUP
SSM_STATE = 64
N_DIRS = 2
N_BRANCHES = 2
EPS = 1e-6
NEG_INF = -1e30
D_IN = D_ATTN + 2 * D_KV + D_ATTN + 2 * D_SSM + N_BRANCHES * D_MODEL
SPLITS = (D_ATTN,
          D_ATTN + D_KV,
          D_ATTN + 2 * D_KV,
          2 * D_ATTN + 2 * D_KV,
          2 * D_ATTN + 2 * D_KV + D_SSM,
          2 * D_ATTN + 2 * D_KV + 2 * D_SSM)

kernel_name = "hybrid_gated_swa_s5_block"


def rms_norm(x, g):
    x32 = x.astype(jnp.float32)
    y = x32 * lax.rsqrt(jnp.mean(x32 * x32, axis=-1, keepdims=True) + EPS)
    return (y * g.astype(jnp.float32)).astype(x.dtype)


def t5_bucket(rel):
    half = NUM_BUCKETS // 2
    ret = (rel > 0).astype(jnp.int32) * half
    n = jnp.abs(rel)
    max_exact = half // 2
    nf = jnp.maximum(n, 1).astype(jnp.float32)
    large = max_exact + (jnp.log(nf / max_exact) / math.log(MAX_DISTANCE / max_exact)
                         * (half - max_exact)).astype(jnp.int32)
    large = jnp.minimum(large, half - 1)
    return ret + jnp.where(n < max_exact, n, large)


def band_windows(t, nb):
    b = t.shape[0]
    tp = jnp.pad(t, ((0, 0), (BLOCK, BLOCK), (0, 0), (0, 0)))
    tb = tp.reshape(b, nb + 2, BLOCK, KV_HEADS, HEAD_DIM)
    return jnp.concatenate([tb[:, :-2], tb[:, 1:-1], tb[:, 2:]], axis=2)


def windowed_gqa(q, k, v, sink, rel_table):
    b, s, _ = q.shape
    nb = s // BLOCK
    q = q.reshape(b, nb, BLOCK, KV_HEADS, Q_PER_KV, HEAD_DIM)
    kw = band_windows(k.reshape(b, s, KV_HEADS, HEAD_DIM), nb)
    vw = band_windows(v.reshape(b, s, KV_HEADS, HEAD_DIM), nb)
    scores = jnp.einsum('bnqhgd,bnkhd->bnhgqk', q, kw).astype(jnp.float32) * (HEAD_DIM ** -0.5)
    rel = (jnp.arange(3 * BLOCK)[None, :] - BLOCK) - jnp.arange(BLOCK)[:, None]
    bias = rel_table.astype(jnp.float32)[t5_bucket(rel)]
    bias = jnp.transpose(bias, (2, 0, 1)).reshape(KV_HEADS, Q_PER_KV, BLOCK, 3 * BLOCK)
    kpos = jnp.arange(nb)[:, None] * BLOCK - BLOCK + jnp.arange(3 * BLOCK)[None, :]
    valid = (jnp.abs(rel) <= WINDOW)[None] & ((kpos >= 0) & (kpos < s))[:, None, :]
    scores = jnp.where(valid[None, :, None, None], scores + bias, NEG_INF)
    sk = sink.astype(jnp.float32).reshape(KV_HEADS, Q_PER_KV)[None, None, :, :, None, None]
    m = jnp.maximum(scores.max(axis=-1, keepdims=True), sk)
    p = jnp.exp(scores - m)
    probs = p / (p.sum(axis=-1, keepdims=True) + jnp.exp(sk - m))
    o = jnp.einsum('bnhgqk,bnkhd->bnqhgd', probs.astype(v.dtype), vw)
    return o.reshape(b, s, D_ATTN)


def _scan_op(e1, e2):
    a1, b1 = e1
    a2, b2 = e2
    return a1 * a2, a2 * b1 + b2


def s5_bidirectional(u, a_re, a_im, log_dt, b_re, b_im, c_re, c_im, d_skip):
    bsz, s, _ = u.shape
    f32 = jnp.float32
    ut = jnp.swapaxes(u.astype(f32).reshape(bsz, s, N_SSM_GROUPS, SSM_GROUP), 0, 1)
    y = ut * d_skip.astype(f32).reshape(N_SSM_GROUPS, SSM_GROUP)
    for d in range(N_DIRS):
        lam = lax.complex(a_re[d].astype(f32), a_im[d].astype(f32))
        dt = jnp.exp(log_dt[d].astype(f32))[:, None]
        a_bar = jnp.exp(lam * dt)
        coef = (a_bar - 1.0) / lam
        b_bar = lax.complex(b_re[d].astype(f32), b_im[d].astype(f32)) * coef[..., None]
        bu = lax.complex(jnp.einsum('sbgc,gpc->sbgp', ut, jnp.real(b_bar)),
                         jnp.einsum('sbgc,gpc->sbgp', ut, jnp.imag(b_bar)))
        a_el = jnp.broadcast_to(a_bar[None, None], (s, 1, N_SSM_GROUPS, SSM_STATE))
        _, xs = lax.associative_scan(_scan_op, (a_el, bu), reverse=(d == 1), axis=0)
        y = y + jnp.einsum('sbgp,gcp->sbgc', jnp.real(xs), c_re[d].astype(f32)) \
              - jnp.einsum('sbgp,gcp->sbgc', jnp.imag(xs), c_im[d].astype(f32))
    return jnp.swapaxes(y, 0, 1).reshape(bsz, s, D_SSM)


def setup_inputs(seed: int = 0) -> dict:
    key = jax.random.key(seed)
    ks = jax.random.split(key, 24)
    nrm = lambda k, shape, scale: jax.random.normal(k, shape, jnp.float32) * scale
    L, G, P, C = DEPTH, N_SSM_GROUPS, SSM_STATE, SSM_GROUP
    a_im_init = math.pi * jnp.arange(P, dtype=jnp.float32)
    return {
        "x": nrm(ks[0], (BATCH, SEQ, D_MODEL), 1.0),
        "norm_gain": 1.0 + nrm(ks[1], (L, D_MODEL), 0.02),
        "w_in": nrm(ks[2], (L, D_MODEL, D_IN), D_MODEL ** -0.5),
        "b_gate": nrm(ks[3], (L, N_BRANCHES * D_MODEL), 0.02),
        "attn_sink": nrm(ks[4], (L, ATTN_HEADS), 0.5),
        "rel_bias_table": nrm(ks[5], (NUM_BUCKETS, ATTN_HEADS), 0.5),
        "ssm_a_re": -0.5 + nrm(ks[6], (L, N_DIRS, G, P), 0.01),
        "ssm_a_im": a_im_init + nrm(ks[7], (L, N_DIRS, G, P), 0.01),
        "ssm_log_dt": jax.random.uniform(ks[8], (L, N_DIRS, G), jnp.float32,
                                         math.log(1e-3), math.log(1e-1)),
        "ssm_b_re": nrm(ks[9], (L, N_DIRS, G, P, C), (2 * C) ** -0.5),
        "ssm_b_im": nrm(ks[10], (L, N_DIRS, G, P, C), (2 * C) ** -0.5),
        "ssm_c_re": nrm(ks[11], (L, N_DIRS, G, C, P), (2 * P) ** -0.5),
        "ssm_c_im": nrm(ks[12], (L, N_DIRS, G, C, P), (2 * P) ** -0.5),
        "ssm_d": nrm(ks[13], (L, D_SSM), 1.0),
        "w_glu": nrm(ks[14], (L, D_SSM, D_SSM), D_SSM ** -0.5),
        "b_glu": nrm(ks[15], (L, D_SSM), 0.02),
        "w_branch_attn": nrm(ks[16], (L, D_ATTN, D_MODEL), D_ATTN ** -0.5),
        "w_branch_ssm": nrm(ks[17], (L, D_SSM, D_MODEL), D_SSM ** -0.5),
        "w_out": nrm(ks[18], (L, D_MODEL, D_MODEL), D_MODEL ** -0.5),
        "final_norm_gain": 1.0 + nrm(ks[19], (D_MODEL,), 0.02),
    }


def reference(x, norm_gain, w_in, b_gate, attn_sink, rel_bias_table, ssm_a_re, ssm_a_im,
              ssm_log_dt, ssm_b_re, ssm_b_im, ssm_c_re, ssm_c_im, ssm_d, w_glu, b_glu,
              w_branch_attn, w_branch_ssm, w_out, final_norm_gain):
    for l in range(DEPTH):
        h = rms_norm(x, norm_gain[l])
        proj = jnp.einsum('bsd,de->bse', h, w_in[l])
        q, k, v, z_attn, u_ssm, z_ssm, g = jnp.split(proj, SPLITS, axis=-1)
        g_attn, g_ssm = jnp.split(g + b_gate[l], N_BRANCHES, axis=-1)
        attn = windowed_gqa(q, k, v, attn_sink[l], rel_bias_table) * jax.nn.silu(z_attn)
        y = s5_bidirectional(u_ssm, ssm_a_re[l], ssm_a_im[l], ssm_log_dt[l], ssm_b_re[l],
                             ssm_b_im[l], ssm_c_re[l], ssm_c_im[l], ssm_d[l]).astype(x.dtype)
        y = jax.nn.gelu(y)
        y = y * jax.nn.sigmoid(jnp.einsum('bsc,ce->bse', y, w_glu[l]) + b_glu[l])
        ssm = y * jax.nn.silu(z_ssm)
        merged = jax.nn.sigmoid(g_attn) * jnp.einsum('bsc,cd->bsd', attn, w_branch_attn[l]) \
               + jax.nn.sigmoid(g_ssm) * jnp.einsum('bsc,cd->bsd', ssm, w_branch_ssm[l])
        x = x + jnp.einsum('bsd,de->bse', merged, w_out[l])
    return rms_norm(x, final_norm_gain)
```

```python
import functools
import math

import jax
import jax.numpy as jnp
import numpy as np
from jax import lax
from jax.experimental import pallas as pl
from jax.experimental.pallas import tpu as pltpu

D_MODEL = 1024
ATTN_HEADS = 8
KV_HEADS = 2
Q_PER_KV = ATTN_HEADS // KV_HEADS
HEAD_DIM = 64
D_ATTN = ATTN_HEADS * HEAD_DIM
D_KV = KV_HEADS * HEAD_DIM
WINDOW = 128
BLOCK = 128
NUM_BUCKETS = 32
MAX_DISTANCE = 128
D_SSM = 512
SSM_GROUP = 16
N_SSM_GROUPS = D_SSM // SSM_GROUP
SSM_STATE = 64
EPS = 1e-6
NEG_INF = -1e30

CHUNK = 16
CHUNK_W = CHUNK * SSM_GROUP
STATE_W = 4 * SSM_STATE

D_PROJ1 = 2 * D_ATTN + 2 * D_KV + 2 * D_SSM

VMEM_LIMIT_BYTES = 48 * 1024 * 1024

_F32 = jnp.float32
_BF16 = jnp.bfloat16
_HI = lax.Precision.HIGHEST


def _sigmoid(x):
    return 0.5 * jnp.tanh(0.5 * x) + 0.5


def _silu(x):
    return x * _sigmoid(x)


def _gelu_tanh(x):
    c = math.sqrt(2.0 / math.pi)
    return 0.5 * x * (1.0 + jnp.tanh(c * (x + 0.044715 * (x * x * x))))


def _rms_normalize(x, gain):
    ms = jnp.mean(x * x, axis=-1, keepdims=True)
    return x * lax.rsqrt(ms + EPS) * gain


def _inproj_kernel(x_ref, gain_ref, w_ref, q_ref, k_ref, v_ref, za_ref, u_ref, zs_ref):
    h = _rms_normalize(x_ref[...], gain_ref[...]).astype(_BF16)
    proj = jnp.dot(h, w_ref[...], preferred_element_type=_F32)
    o = 0
    q_ref[...] = (proj[:, o:o + D_ATTN] * (HEAD_DIM ** -0.5)).astype(_BF16)
    o += D_ATTN
    k_ref[...] = proj[:, o:o + D_KV].astype(_BF16)
    o += D_KV
    v_ref[...] = proj[:, o:o + D_KV].astype(_BF16)
    o += D_KV
    za_ref[...] = _silu(proj[:, o:o + D_ATTN]).astype(_BF16)
    o += D_ATTN
    u_ref[...] = proj[:, o:o + D_SSM].astype(_BF16)
    o += D_SSM
    zs_ref[...] = _silu(proj[:, o:o + D_SSM]).astype(_BF16)


def _inproj(x2, gain, w1, tm):
    t = x2.shape[0]
    row = lambda i: (i, 0)
    fixed = lambda i: (0, 0)
    outs = [D_ATTN, D_KV, D_KV, D_ATTN, D_SSM, D_SSM]
    return pl.pallas_call(
        _inproj_kernel,
        grid=(t // tm,),
        in_specs=[pl.BlockSpec((tm, D_MODEL), row),
                  pl.BlockSpec((1, D_MODEL), fixed),
                  pl.BlockSpec((D_MODEL, D_PROJ1), fixed)],
        out_specs=[pl.BlockSpec((tm, n), row) for n in outs],
        out_shape=[jax.ShapeDtypeStruct((t, n), _BF16) for n in outs],
        compiler_params=pltpu.CompilerParams(
            dimension_semantics=("arbitrary",), vmem_limit_bytes=VMEM_LIMIT_BYTES),
        name="inproj",
    )(x2, gain, w1)


def _ssm_kernel(n_chunks, rows_per_chunk, u_ref, t_ref, s_ref, c_ref, ar_ref, ai_ref,
                y_ref, s_scr, x_scr):
    rb = rows_per_chunk
    u = u_ref[0]
    s_scr[...] = jnp.dot(u, s_ref[0], preferred_element_type=_F32)
    ar = jnp.broadcast_to(ar_ref[0], (rb, 2 * SSM_STATE))
    ai = jnp.broadcast_to(ai_ref[0], (rb, 2 * SSM_STATE))
    is_fwd = lax.broadcasted_iota(jnp.int32, (rb, 2 * SSM_STATE), 1) < SSM_STATE
    is_bwd = jnp.logical_not(is_fwd)
    re_cols = pl.ds(0, 2 * SSM_STATE)
    im_cols = pl.ds(2 * SSM_STATE, 2 * SSM_STATE)

    def step(i, carry):
        st_re, st_im = carry
        rows_f = pl.ds(pl.multiple_of(i * rb, rb), rb)
        rows_b = pl.ds(pl.multiple_of((n_chunks - 1 - i) * rb, rb), rb)
        pltpu.store(x_scr.at[rows_f, re_cols], st_re, mask=is_fwd)
        pltpu.store(x_scr.at[rows_f, im_cols], st_im, mask=is_fwd)
        pltpu.store(x_scr.at[rows_b, re_cols], st_re, mask=is_bwd)
        pltpu.store(x_scr.at[rows_b, im_cols], st_im, mask=is_bwd)
        in_re = jnp.where(is_fwd, s_scr[rows_f, re_cols], s_scr[rows_b, re_cols])
        in_im = jnp.where(is_fwd, s_scr[rows_f, im_cols], s_scr[rows_b, im_cols])
        new_re = ar * st_re - ai * st_im + in_re
        new_im = ar * st_im + ai * st_re + in_im
        return new_re, new_im

    zero = jnp.zeros((rb, 2 * SSM_STATE), _F32)
    lax.fori_loop(0, n_chunks, step, (zero, zero), unroll=4)

    y = jnp.dot(u, t_ref[0], preferred_element_type=_F32)
    y = y + jnp.dot(x_scr[...].astype(_BF16), c_ref[0], preferred_element_type=_F32)
    y_ref[0] = y.astype(y_ref.dtype)


def _ssm(ug, tmat, smat, cmat, ar, ai, n_chunks, rows_per_chunk):
    g, rows, _ = ug.shape
    per_group = lambda i: (i, 0, 0)
    return pl.pallas_call(
        functools.partial(_ssm_kernel, n_chunks, rows_per_chunk),
        grid=(g,),
        in_specs=[pl.BlockSpec((1, rows, CHUNK_W), per_group),
                  pl.BlockSpec((1, CHUNK_W, CHUNK_W), per_group),
                  pl.BlockSpec((1, CHUNK_W, STATE_W), per_group),
                  pl.BlockSpec((1, STATE_W, CHUNK_W), per_group),
                  pl.BlockSpec((1, 1, 2 * SSM_STATE), per_group),
                  pl.BlockSpec((1, 1, 2 * SSM_STATE), per_group)],
        out_specs=pl.BlockSpec((1, rows, CHUNK_W), per_group),
        out_shape=jax.ShapeDtypeStruct((g, rows, CHUNK_W), _BF16),
        scratch_shapes=[pltpu.VMEM((rows, STATE_W), _F32),
                        pltpu.VMEM((rows, STATE_W), _F32)],
        compiler_params=pltpu.CompilerParams(
            dimension_semantics=("arbitrary",), vmem_limit_bytes=VMEM_LIMIT_BYTES),
        name="ssm",
    )(ug, tmat, smat, cmat, ar, ai)


def _ssm_matrices(a_re, a_im, log_dt, b_re, b_im, c_re, c_im, d_skip):
    L, G, P, C = CHUNK, N_SSM_GROUPS, SSM_STATE, SSM_GROUP
    dt = jnp.exp(log_dt)[..., None]
    taus = jnp.arange(L + 1, dtype=_F32)
    mag = jnp.exp((a_re * dt)[..., None] * taus)
    ang = (a_im * dt)[..., None] * taus
    pw_re, pw_im = mag * jnp.cos(ang), mag * jnp.sin(ang)
    ab_re, ab_im = pw_re[..., 1], pw_im[..., 1]
    den = a_re * a_re + a_im * a_im
    co_re = ((ab_re - 1.0) * a_re + ab_im * a_im) / den
    co_im = (ab_im * a_re - (ab_re - 1.0) * a_im) / den
    bb_re = b_re * co_re[..., None] - b_im * co_im[..., None]
    bb_im = b_re * co_im[..., None] + b_im * co_re[..., None]
    m_re = pw_re[..., None] * bb_re[..., None, :] - pw_im[..., None] * bb_im[..., None, :]
    m_im = pw_re[..., None] * bb_im[..., None, :] + pw_im[..., None] * bb_re[..., None, :]
    kk = (jnp.einsum('dgcp,dgpte->dgtce', c_re, m_re, precision=_HI)
          - jnp.einsum('dgcp,dgpte->dgtce', c_im, m_im, precision=_HI))
    jj = np.arange(L)[:, None]
    tt = np.arange(L)[None, :]
    lag = np.abs(tt - jj)
    kf = kk[0][:, lag]
    kb = kk[1][:, lag]
    eye_d = jnp.eye(C, dtype=_F32) * d_skip.reshape(G, 1, 1, C, 1)
    sel = lambda cond: jnp.asarray(cond, _F32)[None, :, :, None, None]
    tmat = (sel(tt > jj) * kf + sel(jj > tt) * kb
            + sel(tt == jj) * (kf + kb + eye_d))
    tmat = tmat.transpose(0, 1, 4, 2, 3).reshape(G, L * C, L * C)
    rev = np.arange(L)[::-1]
    sf_re, sf_im = m_re[0][:, :, rev], m_im[0][:, :, rev]
    sb_re, sb_im = m_re[1][:, :, :L], m_im[1][:, :, :L]
    smat = jnp.stack([sf_re, sb_re, sf_im, sb_im], axis=1)
    smat = smat.transpose(0, 3, 4, 1, 2).reshape(G, L * C, 4 * P)
    tf = np.arange(1, L + 1)
    tb = np.arange(L, 0, -1)
    cp = lambda cr, ci, pr, pi: (cr[..., None] * pr[:, None] - ci[..., None] * pi[:, None],
                                 cr[..., None] * pi[:, None] + ci[..., None] * pr[:, None])
    wf_re, wf_im = cp(c_re[0], c_im[0], pw_re[0][..., tf], pw_im[0][..., tf])
    wb_re, wb_im = cp(c_re[1], c_im[1], pw_re[1][..., tb], pw_im[1][..., tb])
    cmat = jnp.stack([wf_re, wb_re, -wf_im, -wb_im], axis=1)
    cmat = cmat.transpose(0, 1, 3, 4, 2).reshape(G, 4 * P, L * C)
    ar = jnp.concatenate([pw_re[0][..., L], pw_re[1][..., L]], axis=-1)[:, None, :]
    ai = jnp.concatenate([pw_im[0][..., L], pw_im[1][..., L]], axis=-1)[:, None, :]
    return tmat.astype(_BF16), smat.astype(_BF16), cmat.astype(_BF16), ar, ai


def _attn_kernel(sink_ref, q_ref, k0_ref, k1_ref, k2_ref, v0_ref, v1_ref, v2_ref,
                 bias_ref, za_ref, o_ref):
    q = q_ref[0]
    kw = jnp.concatenate([k0_ref[0], k1_ref[0], k2_ref[0]], axis=0)
    vw = jnp.concatenate([v0_ref[0], v1_ref[0], v2_ref[0]], axis=0)
    outs = []
    for kvh in range(KV_HEADS):
        kh = kw[:, kvh * HEAD_DIM:(kvh + 1) * HEAD_DIM]
        vh = vw[:, kvh * HEAD_DIM:(kvh + 1) * HEAD_DIM]
        for g in range(Q_PER_KV):
            h = kvh * Q_PER_KV + g
            qh = q[:, h * HEAD_DIM:(h + 1) * HEAD_DIM]
            s = lax.dot_general(qh, kh, (((1,), (1,)), ((), ())),
                                preferred_element_type=_F32)
            s = s + bias_ref[0, h]
            sk = sink_ref[h]
            m = jnp.maximum(jnp.max(s, axis=-1, keepdims=True), sk)
            p = jnp.exp(s - m)
            den = jnp.sum(p, axis=-1, keepdims=True) + jnp.exp(sk - m)
            o = jnp.dot(p.astype(_BF16), vh, preferred_element_type=_F32)
            outs.append(o / den)
    o_all = jnp.concatenate(outs, axis=1)
    o_ref[0] = (o_all * za_ref[0].astype(_F32)).astype(o_ref.dtype)


def _attention(sink, q, k, v, bias, za):
    b, s, _ = q.shape
    nb = s // BLOCK
    cur = lambda i, n: (i, n, 0)
    prev = lambda i, n: (i, jnp.maximum(n - 1, 0), 0)
    nxt = lambda i, n: (i, jnp.minimum(n + 1, nb - 1), 0)
    edge = lambda i, n: (jnp.where(n == 0, 0, jnp.where(n == nb - 1, 2, 1)), 0, 0, 0)
    kv_spec = lambda f: pl.BlockSpec((1, BLOCK, D_KV), f)
    return pl.pallas_call(
        _attn_kernel,
        grid=(b, nb),
        in_specs=[pl.BlockSpec(memory_space=pltpu.SMEM),
                  pl.BlockSpec((1, BLOCK, D_ATTN), cur),
                  kv_spec(prev), kv_spec(cur), kv_spec(nxt),
                  kv_spec(prev), kv_spec(cur), kv_spec(nxt),
                  pl.BlockSpec((1, ATTN_HEADS, BLOCK, 3 * BLOCK), edge),
                  pl.BlockSpec((1, BLOCK, D_ATTN), cur)],
        out_specs=pl.BlockSpec((1, BLOCK, D_ATTN), cur),
        out_shape=jax.ShapeDtypeStruct((b, s, D_ATTN), _BF16),
        compiler_params=pltpu.CompilerParams(
            dimension_semantics=("arbitrary", "arbitrary"),
            vmem_limit_bytes=VMEM_LIMIT_BYTES),
        name="attn",
    )(sink, q, k, k, k, v, v, v, bias, za)


def _t5_bucket_np(rel):
    half = NUM_BUCKETS // 2
    ret = (rel > 0).astype(np.int64) * half
    n = np.abs(rel)
    max_exact = half // 2
    nf = np.maximum(n, 1).astype(np.float64)
    large = max_exact + (np.log(nf / max_exact) / math.log(MAX_DISTANCE / max_exact)
                         * (half - max_exact)).astype(np.int64)
    large = np.minimum(large, half - 1)
    return ret + np.where(n < max_exact, n, large)


def _attn_bias(rel_table):
    rel = (np.arange(3 * BLOCK)[None, :] - BLOCK) - np.arange(BLOCK)[:, None]
    bucket = _t5_bucket_np(rel)
    bias = rel_table.astype(_F32)[bucket]
    bias = jnp.transpose(bias, (2, 0, 1))
    band = np.abs(rel) <= WINDOW
    slot = np.arange(3 * BLOCK)[None, :] // BLOCK
    valid = np.stack([band & (slot != 0), band, band & (slot != 2)])
    return jnp.where(jnp.asarray(valid)[:, None], bias[None], NEG_INF)


def _merge_kernel(x_ref, gain_ref, wg_ref, bg_ref, ao_ref, ys_ref, zs_ref, wglu_ref,
                  bglu_ref, wba_ref, wbs_ref, wout_ref, fgain_ref, o_ref):
    x = x_ref[...]
    h = _rms_normalize(x, gain_ref[...]).astype(_BF16)
    gates = _sigmoid(jnp.dot(h, wg_ref[...], preferred_element_type=_F32) + bg_ref[...])
    y = _gelu_tanh(ys_ref[...].astype(_F32))
    glu = jnp.dot(y.astype(_BF16), wglu_ref[...], preferred_element_type=_F32) + bglu_ref[...]
    ssm = y * _sigmoid(glu) * zs_ref[...].astype(_F32)
    pa = jnp.dot(ao_ref[...], wba_ref[...], preferred_element_type=_F32)
    ps = jnp.dot(ssm.astype(_BF16), wbs_ref[...], preferred_element_type=_F32)
    merged = gates[:, :D_MODEL] * pa + gates[:, D_MODEL:] * ps
    xn = x + jnp.dot(merged.astype(_BF16), wout_ref[...], preferred_element_type=_F32)
    o_ref[...] = _rms_normalize(xn, fgain_ref[...])


def _merge(x2, gain, wg, bg, ao, ys, zs, wglu, bglu, wba, wbs, wout, fgain, tm):
    t = x2.shape[0]
    row = lambda i: (i, 0)
    fixed = lambda i: (0, 0)
    full = lambda a: pl.BlockSpec(a.shape, fixed)
    return pl.pallas_call(
        _merge_kernel,
        grid=(t // tm,),
        in_specs=[pl.BlockSpec((tm, D_MODEL), row), full(gain), full(wg), full(bg),
                  pl.BlockSpec((tm, D_ATTN), row), pl.BlockSpec((tm, D_SSM), row),
                  pl.BlockSpec((tm, D_SSM), row), full(wglu), full(bglu), full(wba),
                  full(wbs), full(wout), full(fgain)],
        out_specs=pl.BlockSpec((tm, D_MODEL), row),
        out_shape=jax.ShapeDtypeStruct((t, D_MODEL), _F32),
        compiler_params=pltpu.CompilerParams(
            dimension_semantics=("arbitrary",), vmem_limit_bytes=VMEM_LIMIT_BYTES),
        name="merge",
    )(x2, gain, wg, bg, ao, ys, zs, wglu, bglu, wba, wbs, wout, fgain)


def _layer(x, norm_gain, w_in, b_gate, attn_sink, rel_bias_table, a_re, a_im, log_dt,
           b_re, b_im, c_re, c_im, d_skip, w_glu, b_glu, w_ba, w_bs, w_out, out_gain,
           bias):
    bsz, s, _ = x.shape
    t = bsz * s
    n_chunks = s // CHUNK
    x2 = x.reshape(t, D_MODEL)
    gain = norm_gain.reshape(1, D_MODEL).astype(_F32)
    w_bf = w_in.astype(_BF16)
    q, k, v, za, u, zs = _inproj(x2, gain, w_bf[:, :D_PROJ1], 512)

    ug = u.reshape(bsz, n_chunks, CHUNK, N_SSM_GROUPS, SSM_GROUP)
    ug = ug.transpose(3, 1, 0, 2, 4).reshape(N_SSM_GROUPS, n_chunks * bsz, CHUNK_W)
    tmat, smat, cmat, ar, ai = _ssm_matrices(a_re, a_im, log_dt, b_re, b_im, c_re, c_im,
                                             d_skip)
    yg = _ssm(ug, tmat, smat, cmat, ar, ai, n_chunks, bsz)
    ys = yg.reshape(N_SSM_GROUPS, n_chunks, bsz, CHUNK, SSM_GROUP)
    ys = ys.transpose(2, 1, 3, 0, 4).reshape(t, D_SSM)

    shp = lambda a, n: a.reshape(bsz, s, n)
    ao = _attention(attn_sink.astype(_F32), shp(q, D_ATTN), shp(k, D_KV), shp(v, D_KV),
                    bias, shp(za, D_ATTN)).reshape(t, D_ATTN)

    out = _merge(x2, gain, w_bf[:, D_PROJ1:], b_gate.reshape(1, -1).astype(_F32), ao, ys,
                 zs, w_glu.astype(_BF16), b_glu.reshape(1, -1).astype(_F32),
                 w_ba.astype(_BF16), w_bs.astype(_BF16), w_out.astype(_BF16),
                 out_gain.reshape(1, D_MODEL).astype(_F32), 512)
    return out.reshape(bsz, s, D_MODEL)


def kernel(x, norm_gain, w_in, b_gate, attn_sink, rel_bias_table, ssm_a_re, ssm_a_im,
           ssm_log_dt, ssm_b_re, ssm_b_im, ssm_c_re, ssm_c_im, ssm_d, w_glu, b_glu,
           w_branch_attn, w_branch_ssm, w_out, final_norm_gain):
    depth = norm_gain.shape[0]
    assert depth == 1, "final norm is fused into the single layer's epilogue"
    bias = _attn_bias(rel_bias_table)
    l = 0
    return _layer(x, norm_gain[l], w_in[l], b_gate[l], attn_sink[l], rel_bias_table,
                  ssm_a_re[l], ssm_a_im[l], ssm_log_dt[l], ssm_b_re[l], ssm_b_im[l],
                  ssm_c_re[l], ssm_c_im[l], ssm_d[l], w_glu[l], b_glu[l],
                  w_branch_attn[l], w_branch_ssm[l], w_out[l], final_norm_gain, bias)
```

```python
import functools
import math

import jax
import jax.numpy as jnp
import numpy as np
from jax import lax
from jax.experimental import pallas as pl
from jax.experimental.pallas import tpu as pltpu

D_MODEL = 1024
ATTN_HEADS = 8
KV_HEADS = 2
Q_PER_KV = ATTN_HEADS // KV_HEADS
HEAD_DIM = 64
D_ATTN = ATTN_HEADS * HEAD_DIM
D_KV = KV_HEADS * HEAD_DIM
WINDOW = 128
BLOCK = 128
NUM_BUCKETS = 32
MAX_DISTANCE = 128
D_SSM = 512
SSM_GROUP = 16
N_SSM_GROUPS = D_SSM // SSM_GROUP
SSM_STATE = 64
EPS = 1e-6
NEG_INF = -1e30

LANES = 128
CHUNK = 16
CHUNK_W = CHUNK * SSM_GROUP
STATE_W = 4 * SSM_STATE

HEAD_PAIR = LANES // HEAD_DIM
PAIRS_PER_KV = Q_PER_KV // HEAD_PAIR
D_KVX = 2 * D_KV

D_PROJ1 = 2 * D_ATTN + 2 * D_KVX + 2 * D_SSM

TOKEN_TILE = 512
VMEM_LIMIT_BYTES = 48 * 1024 * 1024

_F32 = jnp.float32
_BF16 = jnp.bfloat16
_HI = lax.Precision.HIGHEST


def _sigmoid(x):
    return 0.5 * jnp.tanh(0.5 * x) + 0.5


def _silu(x):
    return x * _sigmoid(x)


def _gelu_tanh(x):
    c = math.sqrt(2.0 / math.pi)
    return 0.5 * x * (1.0 + jnp.tanh(c * (x + 0.044715 * (x * x * x))))


def _rms_normalize(x, gain):
    ms = jnp.mean(x * x, axis=-1, keepdims=True)
    return x * lax.rsqrt(ms + EPS) * gain


def _inproj_kernel(x_ref, gain_ref, w_ref, q_ref, k_ref, v_ref, za_ref, u_ref, zs_ref):
    h = _rms_normalize(x_ref[...], gain_ref[...]).astype(_BF16)
    proj = jnp.dot(h, w_ref[...], preferred_element_type=_F32)
    o = 0
    q_ref[...] = (proj[:, o:o + D_ATTN] * (HEAD_DIM ** -0.5)).astype(_BF16)
    o += D_ATTN
    k_ref[...] = proj[:, o:o + D_KVX].astype(_BF16)
    o += D_KVX
    v_ref[...] = proj[:, o:o + D_KVX].astype(_BF16)
    o += D_KVX
    za_ref[...] = _silu(proj[:, o:o + D_ATTN]).astype(_BF16)
    o += D_ATTN
    u_ref[...] = proj[:, o:o + D_SSM].astype(_BF16)
    o += D_SSM
    zs_ref[...] = _silu(proj[:, o:o + D_SSM]).astype(_BF16)


def _inproj(x2, gain, w1):
    t = x2.shape[0]
    tm = TOKEN_TILE
    row = lambda i: (i, 0)
    fixed = lambda i: (0, 0)
    outs = [D_ATTN, D_KVX, D_KVX, D_ATTN, D_SSM, D_SSM]
    return pl.pallas_call(
        _inproj_kernel,
        grid=(t // tm,),
        in_specs=[pl.BlockSpec((tm, D_MODEL), row),
                  pl.BlockSpec((1, D_MODEL), fixed),
                  pl.BlockSpec((D_MODEL, D_PROJ1), fixed)],
        out_specs=[pl.BlockSpec((tm, n), row) for n in outs],
        out_shape=[jax.ShapeDtypeStruct((t, n), _BF16) for n in outs],
        compiler_params=pltpu.CompilerParams(
            dimension_semantics=("arbitrary",), vmem_limit_bytes=VMEM_LIMIT_BYTES),
        name="inproj",
    )(x2, gain, w1)


def _inproj_weight(w_in):
    o_k, o_v, o_rest = D_ATTN, D_ATTN + D_KV, D_ATTN + 2 * D_KV
    head = lambda o, j: w_in[:, o + j * HEAD_DIM:o + (j + 1) * HEAD_DIM]
    swapped = lambda o: [head(o, 0), head(o, 1), head(o, 1), head(o, 0)]
    cols = ([w_in[:, :D_ATTN]] + swapped(o_k) + swapped(o_v)
            + [w_in[:, o_rest:o_rest + D_ATTN + 2 * D_SSM]])
    return jnp.concatenate(cols, axis=1).astype(_BF16)


def _ssm_kernel(n_chunks, rows_per_chunk, u_ref, t_ref, s_ref, c_ref, ar_ref, ai_ref,
                y_ref, s_scr, x_scr):
    rb = rows_per_chunk
    u = u_ref[0]
    s_scr[...] = jnp.dot(u, s_ref[0], preferred_element_type=_F32)
    ar = jnp.broadcast_to(ar_ref[0], (rb, 2 * SSM_STATE))
    ai = jnp.broadcast_to(ai_ref[0], (rb, 2 * SSM_STATE))
    is_fwd = lax.broadcasted_iota(jnp.int32, (rb, 2 * SSM_STATE), 1) < SSM_STATE
    is_bwd = jnp.logical_not(is_fwd)
    re_cols = pl.ds(0, 2 * SSM_STATE)
    im_cols = pl.ds(2 * SSM_STATE, 2 * SSM_STATE)

    def step(i, carry):
        st_re, st_im = carry
        rows_f = pl.ds(pl.multiple_of(i * rb, rb), rb)
        rows_b = pl.ds(pl.multiple_of((n_chunks - 1 - i) * rb, rb), rb)
        pltpu.store(x_scr.at[rows_f, re_cols], st_re, mask=is_fwd)
        pltpu.store(x_scr.at[rows_f, im_cols], st_im, mask=is_fwd)
        pltpu.store(x_scr.at[rows_b, re_cols], st_re, mask=is_bwd)
        pltpu.store(x_scr.at[rows_b, im_cols], st_im, mask=is_bwd)
        in_re = jnp.where(is_fwd, s_scr[rows_f, re_cols], s_scr[rows_b, re_cols])
        in_im = jnp.where(is_fwd, s_scr[rows_f, im_cols], s_scr[rows_b, im_cols])
        new_re = ar * st_re - ai * st_im + in_re
        new_im = ar * st_im + ai * st_re + in_im
        return new_re, new_im

    zero = jnp.zeros((rb, 2 * SSM_STATE), _F32)
    lax.fori_loop(0, n_chunks, step, (zero, zero), unroll=4)

    y = jnp.dot(u, t_ref[0], preferred_element_type=_F32)
    y = y + jnp.dot(x_scr[...].astype(_BF16), c_ref[0], preferred_element_type=_F32)
    y_ref[0] = y.astype(y_ref.dtype)


def _ssm(ug, tmat, smat, cmat, ar, ai, n_chunks, rows_per_chunk):
    g, rows, _ = ug.shape
    per_group = lambda i: (i, 0, 0)
    return pl.pallas_call(
        functools.partial(_ssm_kernel, n_chunks, rows_per_chunk),
        grid=(g,),
        in_specs=[pl.BlockSpec((1, rows, CHUNK_W), per_group),
                  pl.BlockSpec((1, CHUNK_W, CHUNK_W), per_group),
                  pl.BlockSpec((1, CHUNK_W, STATE_W), per_group),
                  pl.BlockSpec((1, STATE_W, CHUNK_W), per_group),
                  pl.BlockSpec((1, 1, 2 * SSM_STATE), per_group),
                  pl.BlockSpec((1, 1, 2 * SSM_STATE), per_group)],
        out_specs=pl.BlockSpec((1, rows, CHUNK_W), per_group),
        out_shape=jax.ShapeDtypeStruct((g, rows, CHUNK_W), _BF16),
        scratch_shapes=[pltpu.VMEM((rows, STATE_W), _F32),
                        pltpu.VMEM((rows, STATE_W), _F32)],
        compiler_params=pltpu.CompilerParams(
            dimension_semantics=("arbitrary",), vmem_limit_bytes=VMEM_LIMIT_BYTES),
        name="ssm",
    )(ug, tmat, smat, cmat, ar, ai)


def _ssm_matrices(a_re, a_im, log_dt, b_re, b_im, c_re, c_im, d_skip):
    L, G, P, C = CHUNK, N_SSM_GROUPS, SSM_STATE, SSM_GROUP
    dt = jnp.exp(log_dt)[..., None]
    lr, li = a_re * dt, a_im * dt

    def cpow(d, tau, p_last):
        t = jnp.asarray(np.asarray(tau, np.float32))
        if p_last:
            e, a = lr[d][:, None, :] * t[:, None], li[d][:, None, :] * t[:, None]
        else:
            e, a = lr[d][:, :, None] * t, li[d][:, :, None] * t
        mag = jnp.exp(e)
        return mag * jnp.cos(a), mag * jnp.sin(a)

    def cmul(x_re, x_im, y_re, y_im):
        return x_re * y_re - x_im * y_im, x_re * y_im + x_im * y_re

    ab_re, ab_im = jnp.exp(lr) * jnp.cos(li), jnp.exp(lr) * jnp.sin(li)
    den = a_re * a_re + a_im * a_im
    co_re = ((ab_re - 1.0) * a_re + ab_im * a_im) / den
    co_im = (ab_im * a_re - (ab_re - 1.0) * a_im) / den
    bt_re, bt_im = cmul(jnp.swapaxes(b_re, 2, 3), jnp.swapaxes(b_im, 2, 3),
                        co_re[:, :, None, :], co_im[:, :, None, :])
    ct_re, ct_im = jnp.swapaxes(c_re, 2, 3), jnp.swapaxes(c_im, 2, 3)

    def rows(d, tau):
        p_re, p_im = cpow(d, tau, True)
        return cmul(p_re[:, :, None, :], p_im[:, :, None, :],
                    bt_re[d][:, None], bt_im[d][:, None])

    def cols(d, tau):
        p_re, p_im = cpow(d, tau, False)
        return cmul(p_re[..., None], p_im[..., None],
                    ct_re[d][:, :, None, :], ct_im[d][:, :, None, :])

    j = np.arange(L)
    sf_re, sf_im = rows(0, L - 1 - j)
    sb_re, sb_im = rows(1, j)
    smat = jnp.stack([sf_re, sb_re, sf_im, sb_im], axis=3).reshape(G, L * C, 4 * P)
    wf_re, wf_im = cols(0, j + 1)
    wb_re, wb_im = cols(1, L - j)
    cmat = jnp.stack([wf_re, wb_re, -wf_im, -wb_im], axis=1).reshape(G, 4 * P, L * C)
    lf_re, lf_im = rows(0, -1 - j)
    lb_re, lb_im = rows(1, j - L)
    lhs_f = jnp.concatenate([lf_re, lf_im], axis=-1).reshape(G, L * C, 2 * P)
    lhs_b = jnp.concatenate([lb_re, lb_im], axis=-1).reshape(G, L * C, 2 * P)
    rhs_f = jnp.concatenate([wf_re, -wf_im], axis=1).reshape(G, 2 * P, L * C)
    rhs_b = jnp.concatenate([wb_re, -wb_im], axis=1).reshape(G, 2 * P, L * C)
    mf = jnp.einsum('gik,gkn->gin', lhs_f, rhs_f, precision=_HI)
    mb = jnp.einsum('gik,gkn->gin', lhs_b, rhs_b, precision=_HI)
    jj = (np.arange(L * C) // C)[:, None]
    tt = (np.arange(L * C) // C)[None, :]
    d_diag = jnp.asarray(np.eye(L * C, dtype=np.float32)) * jnp.tile(
        d_skip.reshape(G, 1, C), (1, 1, L))
    tmat = (jnp.where(jnp.asarray(tt >= jj), mf, 0.0)
            + jnp.where(jnp.asarray(tt <= jj), mb, 0.0) + d_diag)
    al_f, al_b = cpow(0, [L], True), cpow(1, [L], True)
    al_re = jnp.concatenate([al_f[0], al_b[0]], axis=-1)
    al_im = jnp.concatenate([al_f[1], al_b[1]], axis=-1)
    return tmat.astype(_BF16), smat.astype(_BF16), cmat.astype(_BF16), al_re, al_im


def _attn_kernel(sink_ref, q_ref, k0_ref, k1_ref, k2_ref, v0_ref, v1_ref, v2_ref,
                 bias_ref, za_ref, o_ref):
    q = q_ref[0]
    k_blocks = [k0_ref[0], k1_ref[0], k2_ref[0]]
    v_blocks = [v0_ref[0], v1_ref[0], v2_ref[0]]
    lane = lax.broadcasted_iota(jnp.int32, (BLOCK, LANES), 1)
    halves = [lane < HEAD_DIM, lane >= HEAD_DIM]
    keep = [h.astype(_F32).astype(_BF16) for h in halves]
    nk = 3 * BLOCK
    slabs_out = []
    for j in range(KV_HEADS):
        src = lambda blocks, e: [b[:, (j ^ e) * LANES:((j ^ e) + 1) * LANES] for b in blocks]
        kcat = jnp.concatenate(
            [kb * keep[e] for e in range(HEAD_PAIR) for kb in src(k_blocks, e)],
            axis=0)
        vcat = jnp.concatenate(
            [jnp.concatenate([vb * keep[e], keep[e]], axis=1)
             for e in range(HEAD_PAIR) for vb in src(v_blocks, e)], axis=0)
        qs = jnp.concatenate(
            [q[:, (j * PAIRS_PER_KV + i) * LANES:(j * PAIRS_PER_KV + i + 1) * LANES]
             for i in range(PAIRS_PER_KV)], axis=0)
        s = lax.dot_general(qs, kcat, (((1,), (1,)), ((), ())),
                            preferred_element_type=_F32)
        s = s + bias_ref[0, j]
        p_rows, e_rows = [], []
        for i in range(PAIRS_PER_KV):
            p_lanes, e_sink = [], []
            for e in range(HEAD_PAIR):
                sg = s[i * BLOCK:(i + 1) * BLOCK, e * nk:(e + 1) * nk]
                sk = sink_ref[j * Q_PER_KV + i * HEAD_PAIR + e]
                m = jnp.maximum(jnp.max(sg, axis=-1, keepdims=True), sk)
                p_lanes.append(jnp.exp(sg - m).astype(_BF16))
                e_sink.append(jnp.broadcast_to(jnp.exp(sk - m), (BLOCK, LANES)))
            p_rows.append(jnp.concatenate(p_lanes, axis=1))
            e_rows.append(jnp.where(halves[0], e_sink[0], e_sink[1]))
        p = jnp.concatenate(p_rows, axis=0)
        o = jnp.dot(p, vcat, preferred_element_type=_F32)
        den = o[:, LANES:] + jnp.concatenate(e_rows, axis=0)
        on = o[:, :LANES] / den
        slabs_out += [on[i * BLOCK:(i + 1) * BLOCK] for i in range(PAIRS_PER_KV)]
    o_all = jnp.concatenate(slabs_out, axis=1)
    o_ref[0] = (o_all * za_ref[0].astype(_F32)).astype(o_ref.dtype)


def _attention(sink, q, k, v, bias, za):
    b, s, _ = q.shape
    nb = s // BLOCK
    cur = lambda i, n: (i, n, 0)
    prev = lambda i, n: (i, jnp.maximum(n - 1, 0), 0)
    nxt = lambda i, n: (i, jnp.minimum(n + 1, nb - 1), 0)
    edge = lambda i, n: (jnp.where(n == 0, 0, jnp.where(n == nb - 1, 2, 1)), 0, 0, 0)
    kv_spec = lambda f: pl.BlockSpec((1, BLOCK, D_KVX), f)
    return pl.pallas_call(
        _attn_kernel,
        grid=(b, nb),
        in_specs=[pl.BlockSpec(memory_space=pltpu.SMEM),
                  pl.BlockSpec((1, BLOCK, D_ATTN), cur),
                  kv_spec(prev), kv_spec(cur), kv_spec(nxt),
                  kv_spec(prev), kv_spec(cur), kv_spec(nxt),
                  pl.BlockSpec((1,) + bias.shape[1:], edge),
                  pl.BlockSpec((1, BLOCK, D_ATTN), cur)],
        out_specs=pl.BlockSpec((1, BLOCK, D_ATTN), cur),
        out_shape=jax.ShapeDtypeStruct((b, s, D_ATTN), _BF16),
        compiler_params=pltpu.CompilerParams(
            dimension_semantics=("arbitrary", "arbitrary"),
            vmem_limit_bytes=VMEM_LIMIT_BYTES),
        name="attn",
    )(sink, q, k, k, k, v, v, v, bias, za)


def _t5_bucket_np(rel):
    half = NUM_BUCKETS // 2
    ret = (rel > 0).astype(np.int64) * half
    n = np.abs(rel)
    max_exact = half // 2
    nf = np.maximum(n, 1).astype(np.float64)
    large = max_exact + (np.log(nf / max_exact) / math.log(MAX_DISTANCE / max_exact)
                         * (half - max_exact)).astype(np.int64)
    large = np.minimum(large, half - 1)
    return ret + np.where(n < max_exact, n, large)


def _attn_bias(rel_table):
    rel = (np.arange(3 * BLOCK)[None, :] - BLOCK) - np.arange(BLOCK)[:, None]
    onehot = (_t5_bucket_np(rel)[None] == np.arange(NUM_BUCKETS)[:, None, None])
    bias = jnp.einsum('bh,bqk->hqk', rel_table.astype(_F32),
                      jnp.asarray(onehot, _F32), precision=_HI)
    band = np.abs(rel) <= WINDOW
    slot = np.arange(3 * BLOCK)[None, :] // BLOCK
    valid = np.stack([band & (slot != 0), band, band & (slot != 2)])
    bias = jnp.where(jnp.asarray(valid)[:, None], bias[None], NEG_INF)
    bias = bias.reshape(3, KV_HEADS, PAIRS_PER_KV, HEAD_PAIR, BLOCK, 3 * BLOCK)
    bias = bias.transpose(0, 1, 2, 4, 3, 5)
    return bias.reshape(3, KV_HEADS, PAIRS_PER_KV * BLOCK, HEAD_PAIR * 3 * BLOCK)


def _merge_kernel(x_ref, gain_ref, wg_ref, bg_ref, ao_ref, ys_ref, zs_ref, wglu_ref,
                  bglu_ref, wba_ref, wbs_ref, wout_ref, fgain_ref, o_ref):
    x = x_ref[...]
    h = _rms_normalize(x, gain_ref[...]).astype(_BF16)
    gates = _sigmoid(jnp.dot(h, wg_ref[...], preferred_element_type=_F32) + bg_ref[...])
    y = _gelu_tanh(ys_ref[...].astype(_F32))
    glu = jnp.dot(y.astype(_BF16), wglu_ref[...], preferred_element_type=_F32) + bglu_ref[...]
    ssm = y * _sigmoid(glu) * zs_ref[...].astype(_F32)
    pa = jnp.dot(ao_ref[...], wba_ref[...], preferred_element_type=_F32)
    ps = jnp.dot(ssm.astype(_BF16), wbs_ref[...], preferred_element_type=_F32)
    merged = gates[:, :D_MODEL] * pa + gates[:, D_MODEL:] * ps
    xn = x + jnp.dot(merged.astype(_BF16), wout_ref[...], preferred_element_type=_F32)
    o_ref[...] = _rms_normalize(xn, fgain_ref[...])


def _merge(x2, gain, wg, bg, ao, ys, zs, wglu, bglu, wba, wbs, wout, fgain):
    t = x2.shape[0]
    tm = TOKEN_TILE
    row = lambda i: (i, 0)
    fixed = lambda i: (0, 0)
    full = lambda a: pl.BlockSpec(a.shape, fixed)
    return pl.pallas_call(
        _merge_kernel,
        grid=(t // tm,),
        in_specs=[pl.BlockSpec((tm, D_MODEL), row), full(gain), full(wg), full(bg),
                  pl.BlockSpec((tm, D_ATTN), row), pl.BlockSpec((tm, D_SSM), row),
                  pl.BlockSpec((tm, D_SSM), row), full(wglu), full(bglu), full(wba),
                  full(wbs), full(wout), full(fgain)],
        out_specs=pl.BlockSpec((tm, D_MODEL), row),
        out_shape=jax.ShapeDtypeStruct((t, D_MODEL), _F32),
        compiler_params=pltpu.CompilerParams(
            dimension_semantics=("arbitrary",), vmem_limit_bytes=VMEM_LIMIT_BYTES),
        name="merge",
    )(x2, gain, wg, bg, ao, ys, zs, wglu, bglu, wba, wbs, wout, fgain)


def _layer(x, norm_gain, w_in, b_gate, attn_sink, a_re, a_im, log_dt, b_re, b_im, c_re, c_im,
           d_skip, w_glu, b_glu, w_ba, w_bs, w_out, out_gain, bias):
    bsz, s, _ = x.shape
    t = bsz * s
    n_chunks = s // CHUNK
    x2 = x.reshape(t, D_MODEL)
    gain = norm_gain.reshape(1, D_MODEL).astype(_F32)
    q, k, v, za, u, zs = _inproj(x2, gain, _inproj_weight(w_in))

    ug = u.reshape(bsz, n_chunks, CHUNK, N_SSM_GROUPS, SSM_GROUP)
    ug = ug.transpose(3, 1, 0, 2, 4).reshape(N_SSM_GROUPS, n_chunks * bsz, CHUNK_W)
    tmat, smat, cmat, ar, ai = _ssm_matrices(a_re, a_im, log_dt, b_re, b_im, c_re, c_im,
                                             d_skip)
    yg = _ssm(ug, tmat, smat, cmat, ar, ai, n_chunks, bsz)
    ys = yg.reshape(N_SSM_GROUPS, n_chunks, bsz, CHUNK, SSM_GROUP)
    ys = ys.transpose(2, 1, 3, 0, 4).reshape(t, D_SSM)

    shp = lambda a: a.reshape(bsz, s, a.shape[-1])
    ao = _attention(attn_sink.astype(_F32), shp(q), shp(k), shp(v), bias,
                    shp(za)).reshape(t, D_ATTN)

    w_gate = w_in[:, 2 * D_ATTN + 2 * D_KV + 2 * D_SSM:].astype(_BF16)
    out = _merge(x2, gain, w_gate, b_gate.reshape(1, -1).astype(_F32), ao, ys, zs,
                 w_glu.astype(_BF16), b_glu.reshape(1, -1).astype(_F32),
                 w_ba.astype(_BF16), w_bs.astype(_BF16), w_out.astype(_BF16),
                 out_gain.reshape(1, D_MODEL).astype(_F32))
    return out.reshape(bsz, s, D_MODEL)


def kernel(x, norm_gain, w_in, b_gate, attn_sink, rel_bias_table, ssm_a_re, ssm_a_im,
           ssm_log_dt, ssm_b_re, ssm_b_im, ssm_c_re, ssm_c_im, ssm_d, w_glu, b_glu,
           w_branch_attn, w_branch_ssm, w_out, final_norm_gain):
    depth = norm_gain.shape[0]
    assert depth == 1, "final norm is fused into the single layer's epilogue"
    bias = _attn_bias(rel_bias_table)
    l = 0
    return _layer(x, norm_gain[l], w_in[l], b_gate[l], attn_sink[l], ssm_a_re[l],
                  ssm_a_im[l], ssm_log_dt[l], ssm_b_re[l], ssm_b_im[l], ssm_c_re[l],
                  ssm_c_im[l], ssm_d[l], w_glu[l], b_glu[l], w_branch_attn[l],
                  w_branch_ssm[l], w_out[l], final_norm_gain, bias)
```

```python
import functools
import math

import jax
import jax.numpy as jnp
import numpy as np
from jax import lax
from jax.experimental import pallas as pl
from jax.experimental.pallas import tpu as pltpu

D_MODEL = 1024
ATTN_HEADS = 8
KV_HEADS = 2
Q_PER_KV = ATTN_HEADS // KV_HEADS
HEAD_DIM = 64
D_ATTN = ATTN_HEADS * HEAD_DIM
D_KV = KV_HEADS * HEAD_DIM
WINDOW = 128
BLOCK = 128
NUM_BUCKETS = 32
MAX_DISTANCE = 128
D_SSM = 512
SSM_GROUP = 16
N_SSM_GROUPS = D_SSM // SSM_GROUP
SSM_STATE = 64
EPS = 1e-6
NEG_INF = -1e30

LANES = 128
CHUNK = 16
CHUNK_W = CHUNK * SSM_GROUP
STATE_W = 4 * SSM_STATE
SLOTS_PER_VREG = LANES // SSM_GROUP

HEAD_PAIR = LANES // HEAD_DIM
PAIRS_PER_KV = Q_PER_KV // HEAD_PAIR
D_KVX = 2 * D_KV

D_PROJ1 = 2 * D_ATTN + 2 * D_KVX + 2 * D_SSM

TOKEN_TILE = 512
VMEM_LIMIT_BYTES = 48 * 1024 * 1024

_F32 = jnp.float32
_BF16 = jnp.bfloat16
_HI = lax.Precision.HIGHEST


def _sigmoid(x):
    return 0.5 * jnp.tanh(0.5 * x) + 0.5


def _silu(x):
    return x * _sigmoid(x)


def _gelu_tanh(x):
    c = math.sqrt(2.0 / math.pi)
    return 0.5 * x * (1.0 + jnp.tanh(c * (x + 0.044715 * (x * x * x))))


def _rms_normalize(x, gain):
    ms = jnp.mean(x * x, axis=-1, keepdims=True)
    return x * lax.rsqrt(ms + EPS) * gain


def _slot_masks(rows):
    lane = lax.broadcasted_iota(jnp.int32, (rows, LANES), 1)
    return [(lane >= p * SSM_GROUP) & (lane < (p + 1) * SSM_GROUP)
            for p in range(SLOTS_PER_VREG)]


def _slot_of(group, time):
    half, t8 = divmod(time, SLOTS_PER_VREG)
    return half * SLOTS_PER_VREG + (group + t8) % SLOTS_PER_VREG


def _to_group_major(u_scr, ug_ref, nb, nt):
    masks = _slot_masks(nb)
    for kk in range(nt // CHUNK):
        for v in range(D_SSM // LANES):
            for half in range(CHUNK // SLOTS_PER_VREG):
                rolled = []
                for t8 in range(SLOTS_PER_VREG):
                    r = kk * CHUNK + half * SLOTS_PER_VREG + t8
                    piece = u_scr[v, pl.ds(r, nb, stride=nt), :]
                    rolled.append(pltpu.roll(piece, t8 * SSM_GROUP, 1) if t8 else piece)
                for p0 in range(SLOTS_PER_VREG):
                    acc = rolled[0]
                    for t8 in range(1, SLOTS_PER_VREG):
                        acc = jnp.where(masks[(p0 + t8) % SLOTS_PER_VREG], rolled[t8], acc)
                    g = v * SLOTS_PER_VREG + p0
                    ug_ref[g, kk, :, half * LANES:(half + 1) * LANES] = acc.astype(_BF16)


def _from_group_major(yg_ref, y_scr, nb, nt):
    masks = _slot_masks(nb)
    for kk in range(nt // CHUNK):
        for v in range(D_SSM // LANES):
            for half in range(CHUNK // SLOTS_PER_VREG):
                src = [yg_ref[v * SLOTS_PER_VREG + p0, kk, :,
                              half * LANES:(half + 1) * LANES].astype(_F32)
                       for p0 in range(SLOTS_PER_VREG)]
                for t8 in range(SLOTS_PER_VREG):
                    acc = src[0]
                    for p0 in range(1, SLOTS_PER_VREG):
                        acc = jnp.where(masks[(p0 + t8) % SLOTS_PER_VREG], src[p0], acc)
                    if t8:
                        acc = pltpu.roll(acc, LANES - t8 * SSM_GROUP, 1)
                    r = kk * CHUNK + half * SLOTS_PER_VREG + t8
                    y_scr[v, pl.ds(r, nb, stride=nt), :] = acc


def _inproj_kernel(x_ref, gain_ref, w_ref, q_ref, k_ref, v_ref, za_ref, ug_ref, zs_ref,
                   u_scr):
    nb, nt = x_ref.shape[0], x_ref.shape[1]
    x = x_ref[...].reshape(nb * nt, D_MODEL)
    h = _rms_normalize(x, gain_ref[...]).astype(_BF16)
    proj = jnp.dot(h, w_ref[...], preferred_element_type=_F32)

    def put(ref, val):
        ref[...] = val.astype(_BF16).reshape(nb, nt, val.shape[-1])

    o = 0
    put(q_ref, proj[:, o:o + D_ATTN] * (HEAD_DIM ** -0.5))
    o += D_ATTN
    put(k_ref, proj[:, o:o + D_KVX])
    o += D_KVX
    put(v_ref, proj[:, o:o + D_KVX])
    o += D_KVX
    put(za_ref, _silu(proj[:, o:o + D_ATTN]))
    o += D_ATTN
    for v in range(D_SSM // LANES):
        u_scr[v] = proj[:, o + v * LANES:o + (v + 1) * LANES]
    o += D_SSM
    put(zs_ref, _silu(proj[:, o:o + D_SSM]))
    _to_group_major(u_scr, ug_ref, nb, nt)


def _inproj(x, gain, w1):
    bsz, s, _ = x.shape
    nt = TOKEN_TILE // bsz
    tok = lambda i: (0, i, 0)
    fixed = lambda i: (0, 0)
    outs = [D_ATTN, D_KVX, D_KVX, D_ATTN, None, D_SSM]
    act = lambda n: (pl.BlockSpec((bsz, nt, n), tok), jax.ShapeDtypeStruct((bsz, s, n), _BF16))
    ug = (pl.BlockSpec((N_SSM_GROUPS, nt // CHUNK, bsz, CHUNK_W), lambda i: (0, i, 0, 0)),
          jax.ShapeDtypeStruct((N_SSM_GROUPS, s // CHUNK, bsz, CHUNK_W), _BF16))
    specs, shapes = zip(*[ug if n is None else act(n) for n in outs])
    return pl.pallas_call(
        _inproj_kernel,
        grid=(s // nt,),
        in_specs=[pl.BlockSpec((bsz, nt, D_MODEL), tok),
                  pl.BlockSpec((1, D_MODEL), fixed),
                  pl.BlockSpec((D_MODEL, D_PROJ1), fixed)],
        out_specs=list(specs),
        out_shape=list(shapes),
        scratch_shapes=[pltpu.VMEM((D_SSM // LANES, bsz * nt, LANES), _F32)],
        compiler_params=pltpu.CompilerParams(
            dimension_semantics=("arbitrary",), vmem_limit_bytes=VMEM_LIMIT_BYTES),
        name="inproj",
    )(x, gain, w1)


def _inproj_weight(w_in):
    o_k, o_v, o_rest = D_ATTN, D_ATTN + D_KV, D_ATTN + 2 * D_KV
    head = lambda o, j: w_in[:, o + j * HEAD_DIM:o + (j + 1) * HEAD_DIM]
    swapped = lambda o: [head(o, 0), head(o, 1), head(o, 1), head(o, 0)]
    cols = ([w_in[:, :D_ATTN]] + swapped(o_k) + swapped(o_v)
            + [w_in[:, o_rest:o_rest + D_ATTN + 2 * D_SSM]])
    return jnp.concatenate(cols, axis=1).astype(_BF16)


def _ssm_kernel(n_chunks, rows_per_chunk, u_ref, t_ref, s_ref, c_ref, ar_ref, ai_ref,
                y_ref, s_scr, x_scr):
    rb = rows_per_chunk
    u = u_ref[0]
    s_scr[...] = jnp.dot(u, s_ref[0], preferred_element_type=_F32)
    ar = jnp.broadcast_to(ar_ref[0], (rb, 2 * SSM_STATE))
    ai = jnp.broadcast_to(ai_ref[0], (rb, 2 * SSM_STATE))
    is_fwd = lax.broadcasted_iota(jnp.int32, (rb, 2 * SSM_STATE), 1) < SSM_STATE
    is_bwd = jnp.logical_not(is_fwd)
    re_cols = pl.ds(0, 2 * SSM_STATE)
    im_cols = pl.ds(2 * SSM_STATE, 2 * SSM_STATE)

    def step(i, carry):
        st_re, st_im = carry
        rows_f = pl.ds(pl.multiple_of(i * rb, rb), rb)
        rows_b = pl.ds(pl.multiple_of((n_chunks - 1 - i) * rb, rb), rb)
        pltpu.store(x_scr.at[rows_f, re_cols], st_re, mask=is_fwd)
        pltpu.store(x_scr.at[rows_f, im_cols], st_im, mask=is_fwd)
        pltpu.store(x_scr.at[rows_b, re_cols], st_re, mask=is_bwd)
        pltpu.store(x_scr.at[rows_b, im_cols], st_im, mask=is_bwd)
        in_re = jnp.where(is_fwd, s_scr[rows_f, re_cols], s_scr[rows_b, re_cols])
        in_im = jnp.where(is_fwd, s_scr[rows_f, im_cols], s_scr[rows_b, im_cols])
        new_re = ar * st_re - ai * st_im + in_re
        new_im = ar * st_im + ai * st_re + in_im
        return new_re, new_im

    zero = jnp.zeros((rb, 2 * SSM_STATE), _F32)
    lax.fori_loop(0, n_chunks, step, (zero, zero), unroll=4)

    y = jnp.dot(u, t_ref[0], preferred_element_type=_F32)
    y = y + jnp.dot(x_scr[...].astype(_BF16), c_ref[0], preferred_element_type=_F32)
    y_ref[0] = y.astype(y_ref.dtype)


def _ssm(ug, tmat, smat, cmat, ar, ai, n_chunks, rows_per_chunk):
    g, rows, _ = ug.shape
    per_group = lambda i: (i, 0, 0)
    return pl.pallas_call(
        functools.partial(_ssm_kernel, n_chunks, rows_per_chunk),
        grid=(g,),
        in_specs=[pl.BlockSpec((1, rows, CHUNK_W), per_group),
                  pl.BlockSpec((1, CHUNK_W, CHUNK_W), per_group),
                  pl.BlockSpec((1, CHUNK_W, STATE_W), per_group),
                  pl.BlockSpec((1, STATE_W, CHUNK_W), per_group),
                  pl.BlockSpec((1, 1, 2 * SSM_STATE), per_group),
                  pl.BlockSpec((1, 1, 2 * SSM_STATE), per_group)],
        out_specs=pl.BlockSpec((1, rows, CHUNK_W), per_group),
        out_shape=jax.ShapeDtypeStruct((g, rows, CHUNK_W), _BF16),
        scratch_shapes=[pltpu.VMEM((rows, STATE_W), _F32),
                        pltpu.VMEM((rows, STATE_W), _F32)],
        compiler_params=pltpu.CompilerParams(
            dimension_semantics=("arbitrary",), vmem_limit_bytes=VMEM_LIMIT_BYTES),
        name="ssm",
    )(ug, tmat, smat, cmat, ar, ai)


def _ssm_matrices(a_re, a_im, log_dt, b_re, b_im, c_re, c_im, d_skip):
    L, G, P, C = CHUNK, N_SSM_GROUPS, SSM_STATE, SSM_GROUP
    dt = jnp.exp(log_dt)[..., None]
    lr, li = a_re * dt, a_im * dt

    def cpow(d, tau, p_last):
        t = jnp.asarray(np.broadcast_to(np.asarray(tau, np.float32), (G, np.shape(tau)[-1])))
        if p_last:
            e, a = lr[d][:, None, :] * t[:, :, None], li[d][:, None, :] * t[:, :, None]
        else:
            e, a = lr[d][:, :, None] * t[:, None, :], li[d][:, :, None] * t[:, None, :]
        mag = jnp.exp(e)
        return mag * jnp.cos(a), mag * jnp.sin(a)

    def cmul(x_re, x_im, y_re, y_im):
        return x_re * y_re - x_im * y_im, x_re * y_im + x_im * y_re

    ab_re, ab_im = jnp.exp(lr) * jnp.cos(li), jnp.exp(lr) * jnp.sin(li)
    den = a_re * a_re + a_im * a_im
    co_re = ((ab_re - 1.0) * a_re + ab_im * a_im) / den
    co_im = (ab_im * a_re - (ab_re - 1.0) * a_im) / den
    bt_re, bt_im = cmul(jnp.swapaxes(b_re, 2, 3), jnp.swapaxes(b_im, 2, 3),
                        co_re[:, :, None, :], co_im[:, :, None, :])
    ct_re, ct_im = jnp.swapaxes(c_re, 2, 3), jnp.swapaxes(c_im, 2, 3)

    def rows(d, tau):
        p_re, p_im = cpow(d, tau, True)
        return cmul(p_re[:, :, None, :], p_im[:, :, None, :],
                    bt_re[d][:, None], bt_im[d][:, None])

    def cols(d, tau):
        p_re, p_im = cpow(d, tau, False)
        return cmul(p_re[..., None], p_im[..., None],
                    ct_re[d][:, :, None, :], ct_im[d][:, :, None, :])

    j = np.zeros((G, L), np.int64)
    for g in range(G):
        for t in range(L):
            j[g, _slot_of(g, t)] = t
    sf_re, sf_im = rows(0, L - 1 - j)
    sb_re, sb_im = rows(1, j)
    smat = jnp.stack([sf_re, sb_re, sf_im, sb_im], axis=3).reshape(G, L * C, 4 * P)
    wf_re, wf_im = cols(0, j + 1)
    wb_re, wb_im = cols(1, L - j)
    cmat = jnp.stack([wf_re, wb_re, -wf_im, -wb_im], axis=1).reshape(G, 4 * P, L * C)
    lf_re, lf_im = rows(0, -1 - j)
    lb_re, lb_im = rows(1, j - L)
    lhs_f = jnp.concatenate([lf_re, lf_im], axis=-1).reshape(G, L * C, 2 * P)
    lhs_b = jnp.concatenate([lb_re, lb_im], axis=-1).reshape(G, L * C, 2 * P)
    rhs_f = jnp.concatenate([wf_re, -wf_im], axis=1).reshape(G, 2 * P, L * C)
    rhs_b = jnp.concatenate([wb_re, -wb_im], axis=1).reshape(G, 2 * P, L * C)
    mf = jnp.einsum('gik,gkn->gin', lhs_f, rhs_f, precision=_HI)
    mb = jnp.einsum('gik,gkn->gin', lhs_b, rhs_b, precision=_HI)
    time_of_lane = jnp.asarray(np.repeat(j, C, axis=1), jnp.int32)
    jj, tt = time_of_lane[:, :, None], time_of_lane[:, None, :]
    d_diag = jnp.asarray(np.eye(L * C, dtype=np.float32)) * jnp.tile(
        d_skip.reshape(G, 1, C), (1, 1, L))
    tmat = jnp.where(tt >= jj, mf, 0.0) + jnp.where(tt <= jj, mb, 0.0) + d_diag
    al_f, al_b = cpow(0, [L], True), cpow(1, [L], True)
    al_re = jnp.concatenate([al_f[0], al_b[0]], axis=-1)
    al_im = jnp.concatenate([al_f[1], al_b[1]], axis=-1)
    return tmat.astype(_BF16), smat.astype(_BF16), cmat.astype(_BF16), al_re, al_im


def _attn_kernel(sink_ref, q_ref, k0_ref, k1_ref, k2_ref, v0_ref, v1_ref, v2_ref,
                 bias_ref, za_ref, o_ref):
    q = q_ref[0]
    k_blocks = [k0_ref[0], k1_ref[0], k2_ref[0]]
    v_blocks = [v0_ref[0], v1_ref[0], v2_ref[0]]
    lane = lax.broadcasted_iota(jnp.int32, (BLOCK, LANES), 1)
    halves = [lane < HEAD_DIM, lane >= HEAD_DIM]
    keep = [h.astype(_F32).astype(_BF16) for h in halves]
    nk = 3 * BLOCK
    slabs_out = []
    for j in range(KV_HEADS):
        src = lambda blocks, e: [b[:, (j ^ e) * LANES:((j ^ e) + 1) * LANES] for b in blocks]
        kcat = jnp.concatenate(
            [kb * keep[e] for e in range(HEAD_PAIR) for kb in src(k_blocks, e)],
            axis=0)
        vcat = jnp.concatenate(
            [jnp.concatenate([vb * keep[e], keep[e]], axis=1)
             for e in range(HEAD_PAIR) for vb in src(v_blocks, e)], axis=0)
        qs = jnp.concatenate(
            [q[:, (j * PAIRS_PER_KV + i) * LANES:(j * PAIRS_PER_KV + i + 1) * LANES]
             for i in range(PAIRS_PER_KV)], axis=0)
        s = lax.dot_general(qs, kcat, (((1,), (1,)), ((), ())),
                            preferred_element_type=_F32)
        s = s + bias_ref[0, j]
        p_rows, e_rows = [], []
        for i in range(PAIRS_PER_KV):
            p_lanes, e_sink = [], []
            for e in range(HEAD_PAIR):
                sg = s[i * BLOCK:(i + 1) * BLOCK, e * nk:(e + 1) * nk]
                sk = sink_ref[j * Q_PER_KV + i * HEAD_PAIR + e]
                m = jnp.maximum(jnp.max(sg, axis=-1, keepdims=True), sk)
                p_lanes.append(jnp.exp(sg - m).astype(_BF16))
                e_sink.append(jnp.broadcast_to(jnp.exp(sk - m), (BLOCK, LANES)))
            p_rows.append(jnp.concatenate(p_lanes, axis=1))
            e_rows.append(jnp.where(halves[0], e_sink[0], e_sink[1]))
        p = jnp.concatenate(p_rows, axis=0)
        o = jnp.dot(p, vcat, preferred_element_type=_F32)
        den = o[:, LANES:] + jnp.concatenate(e_rows, axis=0)
        on = o[:, :LANES] / den
        slabs_out += [on[i * BLOCK:(i + 1) * BLOCK] for i in range(PAIRS_PER_KV)]
    o_all = jnp.concatenate(slabs_out, axis=1)
    o_ref[0] = (o_all * za_ref[0].astype(_F32)).astype(o_ref.dtype)


def _attention(sink, q, k, v, bias, za):
    b, s, _ = q.shape
    nb = s // BLOCK
    cur = lambda i, n: (i, n, 0)
    prev = lambda i, n: (i, jnp.maximum(n - 1, 0), 0)
    nxt = lambda i, n: (i, jnp.minimum(n + 1, nb - 1), 0)
    edge = lambda i, n: (jnp.where(n == 0, 0, jnp.where(n == nb - 1, 2, 1)), 0, 0, 0)
    kv_spec = lambda f: pl.BlockSpec((1, BLOCK, D_KVX), f)
    return pl.pallas_call(
        _attn_kernel,
        grid=(b, nb),
        in_specs=[pl.BlockSpec(memory_space=pltpu.SMEM),
                  pl.BlockSpec((1, BLOCK, D_ATTN), cur),
                  kv_spec(prev), kv_spec(cur), kv_spec(nxt),
                  kv_spec(prev), kv_spec(cur), kv_spec(nxt),
                  pl.BlockSpec((1,) + bias.shape[1:], edge),
                  pl.BlockSpec((1, BLOCK, D_ATTN), cur)],
        out_specs=pl.BlockSpec((1, BLOCK, D_ATTN), cur),
        out_shape=jax.ShapeDtypeStruct((b, s, D_ATTN), _BF16),
        compiler_params=pltpu.CompilerParams(
            dimension_semantics=("arbitrary", "arbitrary"),
            vmem_limit_bytes=VMEM_LIMIT_BYTES),
        name="attn",
    )(sink, q, k, k, k, v, v, v, bias, za)


def _t5_bucket_np(rel):
    half = NUM_BUCKETS // 2
    ret = (rel > 0).astype(np.int64) * half
    n = np.abs(rel)
    max_exact = half // 2
    nf = np.maximum(n, 1).astype(np.float64)
    large = max_exact + (np.log(nf / max_exact) / math.log(MAX_DISTANCE / max_exact)
                         * (half - max_exact)).astype(np.int64)
    large = np.minimum(large, half - 1)
    return ret + np.where(n < max_exact, n, large)


def _attn_bias(rel_table):
    rel = (np.arange(3 * BLOCK)[None, :] - BLOCK) - np.arange(BLOCK)[:, None]
    onehot = (_t5_bucket_np(rel)[None] == np.arange(NUM_BUCKETS)[:, None, None])
    bias = jnp.einsum('bh,bqk->hqk', rel_table.astype(_F32),
                      jnp.asarray(onehot, _F32), precision=_HI)
    band = np.abs(rel) <= WINDOW
    slot = np.arange(3 * BLOCK)[None, :] // BLOCK
    valid = np.stack([band & (slot != 0), band, band & (slot != 2)])
    bias = jnp.where(jnp.asarray(valid)[:, None], bias[None], NEG_INF)
    bias = bias.reshape(3, KV_HEADS, PAIRS_PER_KV, HEAD_PAIR, BLOCK, 3 * BLOCK)
    bias = bias.transpose(0, 1, 2, 4, 3, 5)
    return bias.reshape(3, KV_HEADS, PAIRS_PER_KV * BLOCK, HEAD_PAIR * 3 * BLOCK)


def _merge_kernel(x_ref, gain_ref, wg_ref, bg_ref, ao_ref, yg_ref, zs_ref, wglu_ref,
                  bglu_ref, wba_ref, wbs_ref, wout_ref, fgain_ref, o_ref, y_scr):
    nb, nt = x_ref.shape[0], x_ref.shape[1]
    rows = nb * nt
    x = x_ref[...].reshape(rows, D_MODEL)
    h = _rms_normalize(x, gain_ref[...]).astype(_BF16)
    gates = _sigmoid(jnp.dot(h, wg_ref[...], preferred_element_type=_F32) + bg_ref[...])
    _from_group_major(yg_ref, y_scr, nb, nt)
    y = jnp.concatenate([y_scr[v] for v in range(D_SSM // LANES)], axis=1)
    y = _gelu_tanh(y)
    glu = jnp.dot(y.astype(_BF16), wglu_ref[...], preferred_element_type=_F32) + bglu_ref[...]
    ssm = y * _sigmoid(glu) * zs_ref[...].reshape(rows, D_SSM).astype(_F32)
    pa = jnp.dot(ao_ref[...].reshape(rows, D_ATTN), wba_ref[...],
                 preferred_element_type=_F32)
    ps = jnp.dot(ssm.astype(_BF16), wbs_ref[...], preferred_element_type=_F32)
    merged = gates[:, :D_MODEL] * pa + gates[:, D_MODEL:] * ps
    xn = x + jnp.dot(merged.astype(_BF16), wout_ref[...], preferred_element_type=_F32)
    o_ref[...] = _rms_normalize(xn, fgain_ref[...]).reshape(nb, nt, D_MODEL)


def _merge(x, gain, wg, bg, ao, yg, zs, wglu, bglu, wba, wbs, wout, fgain):
    bsz, s, _ = x.shape
    nt = TOKEN_TILE // bsz
    tok = lambda i: (0, i, 0)
    fixed = lambda i: (0, 0)
    full = lambda a: pl.BlockSpec(a.shape, fixed)
    act = lambda n: pl.BlockSpec((bsz, nt, n), tok)
    return pl.pallas_call(
        _merge_kernel,
        grid=(s // nt,),
        in_specs=[act(D_MODEL), full(gain), full(wg), full(bg), act(D_ATTN),
                  pl.BlockSpec((N_SSM_GROUPS, nt // CHUNK, bsz, CHUNK_W),
                               lambda i: (0, i, 0, 0)),
                  act(D_SSM), full(wglu), full(bglu), full(wba), full(wbs), full(wout),
                  full(fgain)],
        out_specs=act(D_MODEL),
        out_shape=jax.ShapeDtypeStruct((bsz, s, D_MODEL), _F32),
        scratch_shapes=[pltpu.VMEM((D_SSM // LANES, bsz * nt, LANES), _F32)],
        compiler_params=pltpu.CompilerParams(
            dimension_semantics=("arbitrary",), vmem_limit_bytes=VMEM_LIMIT_BYTES),
        name="merge",
    )(x, gain, wg, bg, ao, yg, zs, wglu, bglu, wba, wbs, wout, fgain)


def _layer(x, norm_gain, w_in, b_gate, attn_sink, a_re, a_im, log_dt, b_re, b_im, c_re, c_im,
           d_skip, w_glu, b_glu, w_ba, w_bs, w_out, out_gain, bias):
    bsz, s, _ = x.shape
    n_chunks = s // CHUNK
    gain = norm_gain.reshape(1, D_MODEL).astype(_F32)
    q, k, v, za, ug, zs = _inproj(x, gain, _inproj_weight(w_in))

    tmat, smat, cmat, ar, ai = _ssm_matrices(a_re, a_im, log_dt, b_re, b_im, c_re, c_im,
                                             d_skip)
    yg = _ssm(ug.reshape(N_SSM_GROUPS, n_chunks * bsz, CHUNK_W), tmat, smat, cmat, ar, ai,
              n_chunks, bsz)
    yg = yg.reshape(N_SSM_GROUPS, n_chunks, bsz, CHUNK_W)

    ao = _attention(attn_sink.astype(_F32), q, k, v, bias, za)

    w_gate = w_in[:, 2 * D_ATTN + 2 * D_KV + 2 * D_SSM:].astype(_BF16)
    return _merge(x, gain, w_gate, b_gate.reshape(1, -1).astype(_F32), ao, yg, zs,
                  w_glu.astype(_BF16), b_glu.reshape(1, -1).astype(_F32),
                  w_ba.astype(_BF16), w_bs.astype(_BF16), w_out.astype(_BF16),
                  out_gain.reshape(1, D_MODEL).astype(_F32))


def kernel(x, norm_gain, w_in, b_gate, attn_sink, rel_bias_table, ssm_a_re, ssm_a_im,
           ssm_log_dt, ssm_b_re, ssm_b_im, ssm_c_re, ssm_c_im, ssm_d, w_glu, b_glu,
           w_branch_attn, w_branch_ssm, w_out, final_norm_gain):
    depth = norm_gain.shape[0]
    assert depth == 1, "final norm is fused into the single layer's epilogue"
    bias = _attn_bias(rel_bias_table)
    l = 0
    return _layer(x, norm_gain[l], w_in[l], b_gate[l], attn_sink[l], ssm_a_re[l],
                  ssm_a_im[l], ssm_log_dt[l], ssm_b_re[l], ssm_b_im[l], ssm_c_re[l],
                  ssm_c_im[l], ssm_d[l], w_glu[l], b_glu[l], w_branch_attn[l],
                  w_branch_ssm[l], w_out[l], final_norm_gain, bias)
```

```python
import functools
import math

import jax
import jax.numpy as jnp
import numpy as np
from jax import lax
from jax.experimental import pallas as pl
from jax.experimental.pallas import tpu as pltpu

D_MODEL = 1024
ATTN_HEADS = 8
KV_HEADS = 2
Q_PER_KV = ATTN_HEADS // KV_HEADS
HEAD_DIM = 64
D_ATTN = ATTN_HEADS * HEAD_DIM
D_KV = KV_HEADS * HEAD_DIM
WINDOW = 128
BLOCK = 128
NUM_BUCKETS = 32
MAX_DISTANCE = 128
D_SSM = 512
SSM_GROUP = 16
N_SSM_GROUPS = D_SSM // SSM_GROUP
SSM_STATE = 64
EPS = 1e-6
NEG_INF = -1e30

LANES = 128
CHUNK = 16
CHUNK_W = CHUNK * SSM_GROUP
STATE_W = 4 * SSM_STATE
SLOTS_PER_VREG = LANES // SSM_GROUP

HEAD_PAIR = LANES // HEAD_DIM
PAIRS_PER_KV = Q_PER_KV // HEAD_PAIR
D_KVX = 2 * D_KV

D_PROJ1 = 2 * D_ATTN + 2 * D_KVX + 2 * D_SSM

TOKEN_TILE = 512
VMEM_LIMIT_BYTES = 48 * 1024 * 1024

_F32 = jnp.float32
_BF16 = jnp.bfloat16
_HI = lax.Precision.HIGHEST


def _sigmoid(x):
    return 0.5 * jnp.tanh(0.5 * x) + 0.5


def _silu(x):
    return x * _sigmoid(x)


def _gelu_tanh(x):
    c = math.sqrt(2.0 / math.pi)
    return 0.5 * x * (1.0 + jnp.tanh(c * (x + 0.044715 * (x * x * x))))


def _rms_normalize(x, gain):
    ms = jnp.mean(x * x, axis=-1, keepdims=True)
    return x * lax.rsqrt(ms + EPS) * gain


def _slot_masks(rows):
    lane = lax.broadcasted_iota(jnp.int32, (rows, LANES), 1)
    return [(lane >= p * SSM_GROUP) & (lane < (p + 1) * SSM_GROUP)
            for p in range(SLOTS_PER_VREG)]


def _to_group_major(u_scr, ug_ref, nb, nt):
    masks = _slot_masks(nb)
    for kk in range(nt // CHUNK):
        for v in range(D_SSM // LANES):
            for half in range(CHUNK // SLOTS_PER_VREG):
                rolled = []
                for t8 in range(SLOTS_PER_VREG):
                    r = kk * CHUNK + half * SLOTS_PER_VREG + t8
                    piece = u_scr[v, pl.ds(r, nb, stride=nt), :]
                    rolled.append(pltpu.roll(piece, t8 * SSM_GROUP, 1) if t8 else piece)
                for p0 in range(SLOTS_PER_VREG):
                    acc = rolled[0]
                    for t8 in range(1, SLOTS_PER_VREG):
                        acc = jnp.where(masks[(p0 + t8) % SLOTS_PER_VREG], rolled[t8], acc)
                    g = v * SLOTS_PER_VREG + p0
                    ug_ref[g, kk, :, half * LANES:(half + 1) * LANES] = acc.astype(_BF16)


def _from_group_major(yg_ref, y_scr, nb, nt):
    masks = _slot_masks(nb)
    for kk in range(nt // CHUNK):
        for v in range(D_SSM // LANES):
            for half in range(CHUNK // SLOTS_PER_VREG):
                src = [yg_ref[v * SLOTS_PER_VREG + p0, kk, :,
                              half * LANES:(half + 1) * LANES].astype(_F32)
                       for p0 in range(SLOTS_PER_VREG)]
                for t8 in range(SLOTS_PER_VREG):
                    acc = src[0]
                    for p0 in range(1, SLOTS_PER_VREG):
                        acc = jnp.where(masks[(p0 + t8) % SLOTS_PER_VREG], src[p0], acc)
                    if t8:
                        acc = pltpu.roll(acc, LANES - t8 * SSM_GROUP, 1)
                    r = kk * CHUNK + half * SLOTS_PER_VREG + t8
                    y_scr[v, pl.ds(r, nb, stride=nt), :] = acc


def _inproj_kernel(x_ref, gain_ref, w_ref, q_ref, k_ref, v_ref, za_ref, ug_ref, zs_ref,
                   u_scr):
    nb, nt = x_ref.shape[0], x_ref.shape[1]
    x = x_ref[...].reshape(nb * nt, D_MODEL)
    h = _rms_normalize(x, gain_ref[...]).astype(_BF16)
    proj = jnp.dot(h, w_ref[...], preferred_element_type=_F32)

    def put(ref, val):
        ref[...] = val.astype(_BF16).reshape(nb, nt, val.shape[-1])

    o = 0
    put(q_ref, proj[:, o:o + D_ATTN] * (HEAD_DIM ** -0.5))
    o += D_ATTN
    put(k_ref, proj[:, o:o + D_KVX])
    o += D_KVX
    put(v_ref, proj[:, o:o + D_KVX])
    o += D_KVX
    put(za_ref, _silu(proj[:, o:o + D_ATTN]))
    o += D_ATTN
    for v in range(D_SSM // LANES):
        u_scr[v] = proj[:, o + v * LANES:o + (v + 1) * LANES]
    o += D_SSM
    put(zs_ref, _silu(proj[:, o:o + D_SSM]))
    _to_group_major(u_scr, ug_ref, nb, nt)


def _inproj(x, gain, w1):
    bsz, s, _ = x.shape
    nt = TOKEN_TILE // bsz
    tok = lambda i: (0, i, 0)
    fixed = lambda i: (0, 0)
    outs = [D_ATTN, D_KVX, D_KVX, D_ATTN, None, D_SSM]
    act = lambda n: (pl.BlockSpec((bsz, nt, n), tok), jax.ShapeDtypeStruct((bsz, s, n), _BF16))
    ug = (pl.BlockSpec((N_SSM_GROUPS, nt // CHUNK, bsz, CHUNK_W), lambda i: (0, i, 0, 0)),
          jax.ShapeDtypeStruct((N_SSM_GROUPS, s // CHUNK, bsz, CHUNK_W), _BF16))
    specs, shapes = zip(*[ug if n is None else act(n) for n in outs])
    return pl.pallas_call(
        _inproj_kernel,
        grid=(s // nt,),
        in_specs=[pl.BlockSpec((bsz, nt, D_MODEL), tok),
                  pl.BlockSpec((1, D_MODEL), fixed),
                  pl.BlockSpec((D_MODEL, D_PROJ1), fixed)],
        out_specs=list(specs),
        out_shape=list(shapes),
        scratch_shapes=[pltpu.VMEM((D_SSM // LANES, bsz * nt, LANES), _F32)],
        compiler_params=pltpu.CompilerParams(
            dimension_semantics=("arbitrary",), vmem_limit_bytes=VMEM_LIMIT_BYTES),
        name="inproj",
    )(x, gain, w1)


def _inproj_weight(w_in):
    o_k, o_v, o_rest = D_ATTN, D_ATTN + D_KV, D_ATTN + 2 * D_KV
    head = lambda o, j: w_in[:, o + j * HEAD_DIM:o + (j + 1) * HEAD_DIM]
    swapped = lambda o: [head(o, 0), head(o, 1), head(o, 1), head(o, 0)]
    cols = ([w_in[:, :D_ATTN]] + swapped(o_k) + swapped(o_v)
            + [w_in[:, o_rest:o_rest + D_ATTN + 2 * D_SSM]])
    return jnp.concatenate(cols, axis=1).astype(_BF16)


def _ssm_kernel(n_chunks, rows_per_chunk, u_ref, t_ref, s_ref, c_ref, ar_ref, ai_ref,
                y_ref, s_scr, x_scr):
    rb = rows_per_chunk
    u = u_ref[0]
    s_scr[...] = jnp.dot(u, s_ref[0], preferred_element_type=_F32)
    ar = jnp.broadcast_to(ar_ref[0], (rb, 2 * SSM_STATE))
    ai = jnp.broadcast_to(ai_ref[0], (rb, 2 * SSM_STATE))
    is_fwd = lax.broadcasted_iota(jnp.int32, (rb, 2 * SSM_STATE), 1) < SSM_STATE
    is_bwd = jnp.logical_not(is_fwd)
    re_cols = pl.ds(0, 2 * SSM_STATE)
    im_cols = pl.ds(2 * SSM_STATE, 2 * SSM_STATE)

    def step(i, carry):
        st_re, st_im = carry
        rows_f = pl.ds(pl.multiple_of(i * rb, rb), rb)
        rows_b = pl.ds(pl.multiple_of((n_chunks - 1 - i) * rb, rb), rb)
        pltpu.store(x_scr.at[rows_f, re_cols], st_re, mask=is_fwd)
        pltpu.store(x_scr.at[rows_f, im_cols], st_im, mask=is_fwd)
        pltpu.store(x_scr.at[rows_b, re_cols], st_re, mask=is_bwd)
        pltpu.store(x_scr.at[rows_b, im_cols], st_im, mask=is_bwd)
        in_re = jnp.where(is_fwd, s_scr[rows_f, re_cols], s_scr[rows_b, re_cols])
        in_im = jnp.where(is_fwd, s_scr[rows_f, im_cols], s_scr[rows_b, im_cols])
        new_re = ar * st_re - ai * st_im + in_re
        new_im = ar * st_im + ai * st_re + in_im
        return new_re, new_im

    zero = jnp.zeros((rb, 2 * SSM_STATE), _F32)
    lax.fori_loop(0, n_chunks, step, (zero, zero), unroll=4)

    y = jnp.dot(u, t_ref[0], preferred_element_type=_F32)
    y = y + lax.dot_general(x_scr[...].astype(_BF16), c_ref[0], (((1,), (1,)), ((), ())),
                            preferred_element_type=_F32)
    y_ref[0] = y.astype(y_ref.dtype)


def _ssm(ug, tmat, smat, cmat, ar, ai, n_chunks, rows_per_chunk):
    g, rows, _ = ug.shape
    per_group = lambda i: (i, 0, 0)
    return pl.pallas_call(
        functools.partial(_ssm_kernel, n_chunks, rows_per_chunk),
        grid=(g,),
        in_specs=[pl.BlockSpec((1, rows, CHUNK_W), per_group),
                  pl.BlockSpec((1, CHUNK_W, CHUNK_W), per_group),
                  pl.BlockSpec((1, CHUNK_W, STATE_W), per_group),
                  pl.BlockSpec((1, CHUNK_W, STATE_W), per_group),
                  pl.BlockSpec((1, 1, 2 * SSM_STATE), per_group),
                  pl.BlockSpec((1, 1, 2 * SSM_STATE), per_group)],
        out_specs=pl.BlockSpec((1, rows, CHUNK_W), per_group),
        out_shape=jax.ShapeDtypeStruct((g, rows, CHUNK_W), _BF16),
        scratch_shapes=[pltpu.VMEM((rows, STATE_W), _F32),
                        pltpu.VMEM((rows, STATE_W), _F32)],
        compiler_params=pltpu.CompilerParams(
            dimension_semantics=("arbitrary",), vmem_limit_bytes=VMEM_LIMIT_BYTES),
        name="ssm",
    )(ug, tmat, smat, cmat, ar, ai)


def _ssm_matrices(a_re, a_im, log_dt, b_re, b_im, c_re, c_im, d_skip):
    L, G, P, C = CHUNK, N_SSM_GROUPS, SSM_STATE, SSM_GROUP
    both = lambda a: jnp.concatenate([a[0], a[1]], axis=-1)
    dt = jnp.broadcast_to(jnp.exp(log_dt)[..., None], a_re.shape)
    lam = jnp.stack([both(a_re), both(a_im), both(dt)], axis=1)
    bt = jnp.stack([b_re, b_im], axis=0).transpose(2, 0, 4, 1, 3).reshape(G, 2, C, 2 * P)
    ct = jnp.stack([c_re, c_im], axis=0).transpose(2, 0, 3, 1, 4).reshape(G, 2, C, 2 * P)
    dvec = jnp.tile(d_skip.reshape(G, 1, C), (1, 1, L))
    per_group = lambda *blk: pl.BlockSpec((1,) + blk, lambda i: (i,) + (0,) * len(blk))
    mat = jax.ShapeDtypeStruct((G, L * C, L * C), _BF16)
    vec = jax.ShapeDtypeStruct((G, 1, 2 * P), _F32)
    return pl.pallas_call(
        _ssm_prep_kernel,
        grid=(G,),
        in_specs=[per_group(3, 2 * P), per_group(2, C, 2 * P), per_group(2, C, 2 * P),
                  per_group(1, L * C)],
        out_specs=[per_group(L * C, L * C), per_group(L * C, STATE_W),
                   per_group(L * C, STATE_W), per_group(1, 2 * P), per_group(1, 2 * P)],
        out_shape=[mat, mat, mat, vec, vec],
        compiler_params=pltpu.CompilerParams(dimension_semantics=("arbitrary",)),
        name="ssm_prep",
    )(lam, bt, ct, dvec)


def _ssm_prep_kernel(lam_ref, bt_ref, ct_ref, dvec_ref, t_ref, s_ref, c_ref, ar_ref, ai_ref):
    L, P, C = CHUNK, SSM_STATE, SSM_GROUP
    g8 = jnp.bitwise_and(pl.program_id(0), SLOTS_PER_VREG - 1)

    def time_of_slot(slot):
        return (jnp.bitwise_and(slot, -SLOTS_PER_VREG)
                + jnp.bitwise_and(slot - g8, SLOTS_PER_VREG - 1))

    def cmul(x_re, x_im, y_re, y_im):
        return x_re * y_re - x_im * y_im, x_re * y_im + x_im * y_re

    a_re, a_im, dt = lam_ref[0, 0:1], lam_ref[0, 1:2], lam_ref[0, 2:3]
    lr, li = a_re * dt, a_im * dt

    def cpow(tau):
        mag, ang = jnp.exp(lr * tau), li * tau
        return mag * jnp.cos(ang), mag * jnp.sin(ang)

    ab_re, ab_im = cpow(jnp.ones_like(lr))
    den = a_re * a_re + a_im * a_im
    co_re = ((ab_re - 1.0) * a_re + ab_im * a_im) / den
    co_im = (ab_im * a_re - (ab_re - 1.0) * a_im) / den
    bb_re, bb_im = cmul(bt_ref[0, 0], bt_ref[0, 1], co_re, co_im)
    cc_re, cc_im = ct_ref[0, 0], ct_ref[0, 1]

    def outer(p, m):
        (p_re, p_im), (m_re, m_im) = p, m
        blocks = [cmul(p_re[s:s + 1], p_im[s:s + 1], m_re, m_im) for s in range(L)]
        return (jnp.concatenate([b[0] for b in blocks], axis=0),
                jnp.concatenate([b[1] for b in blocks], axis=0))

    j = time_of_slot(lax.broadcasted_iota(jnp.int32, (L, 2 * P), 0)).astype(_F32)
    fwd = lax.broadcasted_iota(jnp.int32, (L, 2 * P), 1) < P
    s_re, s_im = outer(cpow(jnp.where(fwd, L - 1 - j, j)), (bb_re, bb_im))
    s_ref[0] = jnp.concatenate([s_re, s_im], axis=1).astype(_BF16)
    w_re, w_im = outer(cpow(jnp.where(fwd, j + 1, L - j)), (cc_re, cc_im))
    wcat = jnp.concatenate([w_re, -w_im], axis=1)
    c_ref[0] = wcat.astype(_BF16)
    l_re, l_im = outer(cpow(jnp.where(fwd, -1 - j, j - L)), (bb_re, bb_im))
    lcat = jnp.concatenate([l_re, l_im], axis=1)
    row = lax.broadcasted_iota(jnp.int32, (L * C, L * C), 0)
    col = lax.broadcasted_iota(jnp.int32, (L * C, L * C), 1)
    is_fwd_state = jnp.bitwise_and(col, P) == 0
    contract_state = (((1,), (1,)), ((), ()))
    mf = lax.dot_general(jnp.where(is_fwd_state, lcat, 0.0), wcat, contract_state,
                         precision=_HI, preferred_element_type=_F32)
    mb = lax.dot_general(jnp.where(is_fwd_state, 0.0, lcat), wcat, contract_state,
                         precision=_HI, preferred_element_type=_F32)
    lane_to_slot = C.bit_length() - 1
    jj = time_of_slot(jnp.right_shift(row, lane_to_slot))
    tt = time_of_slot(jnp.right_shift(col, lane_to_slot))
    tm = (jnp.where(tt >= jj, mf, 0.0) + jnp.where(tt <= jj, mb, 0.0)
          + jnp.where(row == col, dvec_ref[0], 0.0))
    t_ref[0] = tm.astype(_BF16)
    al_re, al_im = cpow(jnp.full_like(lr, float(L)))
    ar_ref[0] = al_re
    ai_ref[0] = al_im


def _attn_kernel(sink_ref, q_ref, k0_ref, k1_ref, k2_ref, v0_ref, v1_ref, v2_ref,
                 bias_ref, za_ref, o_ref):
    q = q_ref[0]
    k_blocks = [k0_ref[0], k1_ref[0], k2_ref[0]]
    v_blocks = [v0_ref[0], v1_ref[0], v2_ref[0]]
    lane = lax.broadcasted_iota(jnp.int32, (BLOCK, LANES), 1)
    halves = [lane < HEAD_DIM, lane >= HEAD_DIM]
    keep = [h.astype(_F32).astype(_BF16) for h in halves]
    nk = 3 * BLOCK
    slabs_out = []
    for j in range(KV_HEADS):
        src = lambda blocks, e: [b[:, (j ^ e) * LANES:((j ^ e) + 1) * LANES] for b in blocks]
        kcat = jnp.concatenate(
            [kb * keep[e] for e in range(HEAD_PAIR) for kb in src(k_blocks, e)],
            axis=0)
        vcat = jnp.concatenate(
            [jnp.concatenate([vb * keep[e], keep[e]], axis=1)
             for e in range(HEAD_PAIR) for vb in src(v_blocks, e)], axis=0)
        qs = jnp.concatenate(
            [q[:, (j * PAIRS_PER_KV + i) * LANES:(j * PAIRS_PER_KV + i + 1) * LANES]
             for i in range(PAIRS_PER_KV)], axis=0)
        s = lax.dot_general(qs, kcat, (((1,), (1,)), ((), ())),
                            preferred_element_type=_F32)
        s = s + bias_ref[0, j]
        p_rows, e_rows = [], []
        for i in range(PAIRS_PER_KV):
            p_lanes, e_sink = [], []
            for e in range(HEAD_PAIR):
                sg = s[i * BLOCK:(i + 1) * BLOCK, e * nk:(e + 1) * nk]
                sk = sink_ref[j * Q_PER_KV + i * HEAD_PAIR + e]
                m = jnp.maximum(jnp.max(sg, axis=-1, keepdims=True), sk)
                p_lanes.append(jnp.exp(sg - m).astype(_BF16))
                e_sink.append(jnp.broadcast_to(jnp.exp(sk - m), (BLOCK, LANES)))
            p_rows.append(jnp.concatenate(p_lanes, axis=1))
            e_rows.append(jnp.where(halves[0], e_sink[0], e_sink[1]))
        p = jnp.concatenate(p_rows, axis=0)
        o = jnp.dot(p, vcat, preferred_element_type=_F32)
        den = o[:, LANES:] + jnp.concatenate(e_rows, axis=0)
        on = o[:, :LANES] / den
        slabs_out += [on[i * BLOCK:(i + 1) * BLOCK] for i in range(PAIRS_PER_KV)]
    o_all = jnp.concatenate(slabs_out, axis=1)
    o_ref[0] = (o_all * za_ref[0].astype(_F32)).astype(o_ref.dtype)


def _attention(sink, q, k, v, bias, za):
    b, s, _ = q.shape
    nb = s // BLOCK
    cur = lambda i, n: (i, n, 0)
    prev = lambda i, n: (i, jnp.maximum(n - 1, 0), 0)
    nxt = lambda i, n: (i, jnp.minimum(n + 1, nb - 1), 0)
    edge = lambda i, n: (jnp.where(n == 0, 0, jnp.where(n == nb - 1, 2, 1)), 0, 0, 0)
    kv_spec = lambda f: pl.BlockSpec((1, BLOCK, D_KVX), f)
    return pl.pallas_call(
        _attn_kernel,
        grid=(b, nb),
        in_specs=[pl.BlockSpec(memory_space=pltpu.SMEM),
                  pl.BlockSpec((1, BLOCK, D_ATTN), cur),
                  kv_spec(prev), kv_spec(cur), kv_spec(nxt),
                  kv_spec(prev), kv_spec(cur), kv_spec(nxt),
                  pl.BlockSpec((1,) + bias.shape[1:], edge),
                  pl.BlockSpec((1, BLOCK, D_ATTN), cur)],
        out_specs=pl.BlockSpec((1, BLOCK, D_ATTN), cur),
        out_shape=jax.ShapeDtypeStruct((b, s, D_ATTN), _BF16),
        compiler_params=pltpu.CompilerParams(
            dimension_semantics=("arbitrary", "arbitrary"),
            vmem_limit_bytes=VMEM_LIMIT_BYTES),
        name="attn",
    )(sink, q, k, k, k, v, v, v, bias, za)


def _t5_bucket_np(rel):
    half = NUM_BUCKETS // 2
    ret = (rel > 0).astype(np.int64) * half
    n = np.abs(rel)
    max_exact = half // 2
    nf = np.maximum(n, 1).astype(np.float64)
    large = max_exact + (np.log(nf / max_exact) / math.log(MAX_DISTANCE / max_exact)
                         * (half - max_exact)).astype(np.int64)
    large = np.minimum(large, half - 1)
    return ret + np.where(n < max_exact, n, large)


def _attn_bias(rel_table):
    rel = (np.arange(3 * BLOCK)[None, :] - BLOCK) - np.arange(BLOCK)[:, None]
    onehot = (_t5_bucket_np(rel)[None] == np.arange(NUM_BUCKETS)[:, None, None])
    bias = jnp.einsum('bh,bqk->hqk', rel_table.astype(_F32),
                      jnp.asarray(onehot, _F32), precision=_HI)
    band = np.abs(rel) <= WINDOW
    slot = np.arange(3 * BLOCK)[None, :] // BLOCK
    valid = np.stack([band & (slot != 0), band, band & (slot != 2)])
    bias = jnp.where(jnp.asarray(valid)[:, None], bias[None], NEG_INF)
    bias = bias.reshape(3, KV_HEADS, PAIRS_PER_KV, HEAD_PAIR, BLOCK, 3 * BLOCK)
    bias = bias.transpose(0, 1, 2, 4, 3, 5)
    return bias.reshape(3, KV_HEADS, PAIRS_PER_KV * BLOCK, HEAD_PAIR * 3 * BLOCK)


def _merge_kernel(x_ref, gain_ref, wg_ref, bg_ref, ao_ref, yg_ref, zs_ref, wglu_ref,
                  bglu_ref, wba_ref, wbs_ref, wout_ref, fgain_ref, o_ref, y_scr):
    nb, nt = x_ref.shape[0], x_ref.shape[1]
    rows = nb * nt
    x = x_ref[...].reshape(rows, D_MODEL)
    h = _rms_normalize(x, gain_ref[...]).astype(_BF16)
    gates = _sigmoid(jnp.dot(h, wg_ref[...], preferred_element_type=_F32) + bg_ref[...])
    _from_group_major(yg_ref, y_scr, nb, nt)
    y = jnp.concatenate([y_scr[v] for v in range(D_SSM // LANES)], axis=1)
    y = _gelu_tanh(y)
    glu = jnp.dot(y.astype(_BF16), wglu_ref[...], preferred_element_type=_F32) + bglu_ref[...]
    ssm = y * _sigmoid(glu) * zs_ref[...].reshape(rows, D_SSM).astype(_F32)
    pa = jnp.dot(ao_ref[...].reshape(rows, D_ATTN), wba_ref[...],
                 preferred_element_type=_F32)
    ps = jnp.dot(ssm.astype(_BF16), wbs_ref[...], preferred_element_type=_F32)
    merged = gates[:, :D_MODEL] * pa + gates[:, D_MODEL:] * ps
    xn = x + jnp.dot(merged.astype(_BF16), wout_ref[...], preferred_element_type=_F32)
    o_ref[...] = _rms_normalize(xn, fgain_ref[...]).reshape(nb, nt, D_MODEL)


def _merge(x, gain, wg, bg, ao, yg, zs, wglu, bglu, wba, wbs, wout, fgain):
    bsz, s, _ = x.shape
    nt = TOKEN_TILE // bsz
    tok = lambda i: (0, i, 0)
    fixed = lambda i: (0, 0)
    full = lambda a: pl.BlockSpec(a.shape, fixed)
    act = lambda n: pl.BlockSpec((bsz, nt, n), tok)
    return pl.pallas_call(
        _merge_kernel,
        grid=(s // nt,),
        in_specs=[act(D_MODEL), full(gain), full(wg), full(bg), act(D_ATTN),
                  pl.BlockSpec((N_SSM_GROUPS, nt // CHUNK, bsz, CHUNK_W),
                               lambda i: (0, i, 0, 0)),
                  act(D_SSM), full(wglu), full(bglu), full(wba), full(wbs), full(wout),
                  full(fgain)],
        out_specs=act(D_MODEL),
        out_shape=jax.ShapeDtypeStruct((bsz, s, D_MODEL), _F32),
        scratch_shapes=[pltpu.VMEM((D_SSM // LANES, bsz * nt, LANES), _F32)],
        compiler_params=pltpu.CompilerParams(
            dimension_semantics=("arbitrary",), vmem_limit_bytes=VMEM_LIMIT_BYTES),
        name="merge",
    )(x, gain, wg, bg, ao, yg, zs, wglu, bglu, wba, wbs, wout, fgain)


def _layer(x, norm_gain, w_in, b_gate, attn_sink, a_re, a_im, log_dt, b_re, b_im, c_re, c_im,
           d_skip, w_glu, b_glu, w_ba, w_bs, w_out, out_gain, bias):
    bsz, s, _ = x.shape
    n_chunks = s // CHUNK
    gain = norm_gain.reshape(1, D_MODEL).astype(_F32)
    q, k, v, za, ug, zs = _inproj(x, gain, _inproj_weight(w_in))

    tmat, smat, cmat, ar, ai = _ssm_matrices(a_re, a_im, log_dt, b_re, b_im, c_re, c_im,
                                             d_skip)
    yg = _ssm(ug.reshape(N_SSM_GROUPS, n_chunks * bsz, CHUNK_W), tmat, smat, cmat, ar, ai,
              n_chunks, bsz)
    yg = yg.reshape(N_SSM_GROUPS, n_chunks, bsz, CHUNK_W)

    ao = _attention(attn_sink.astype(_F32), q, k, v, bias, za)

    w_gate = w_in[:, 2 * D_ATTN + 2 * D_KV + 2 * D_SSM:].astype(_BF16)
    return _merge(x, gain, w_gate, b_gate.reshape(1, -1).astype(_F32), ao, yg, zs,
                  w_glu.astype(_BF16), b_glu.reshape(1, -1).astype(_F32),
                  w_ba.astype(_BF16), w_bs.astype(_BF16), w_out.astype(_BF16),
                  out_gain.reshape(1, D_MODEL).astype(_F32))


def kernel(x, norm_gain, w_in, b_gate, attn_sink, rel_bias_table, ssm_a_re, ssm_a_im,
           ssm_log_dt, ssm_b_re, ssm_b_im, ssm_c_re, ssm_c_im, ssm_d, w_glu, b_glu,
           w_branch_attn, w_branch_ssm, w_out, final_norm_gain):
    depth = norm_gain.shape[0]
    assert depth == 1, "final norm is fused into the single layer's epilogue"
    bias = _attn_bias(rel_bias_table)
    l = 0
    return _layer(x, norm_gain[l], w_in[l], b_gate[l], attn_sink[l], ssm_a_re[l],
                  ssm_a_im[l], ssm_log_dt[l], ssm_b_re[l], ssm_b_im[l], ssm_c_re[l],
                  ssm_c_im[l], ssm_d[l], w_glu[l], b_glu[l], w_branch_attn[l],
                  w_branch_ssm[l], w_out[l], final_norm_gain, bias)
```

```python
import functools
import math

import jax
import jax.numpy as jnp
import numpy as np
from jax import lax
from jax.experimental import pallas as pl
from jax.experimental.pallas import tpu as pltpu

D_MODEL = 1024
ATTN_HEADS = 8
KV_HEADS = 2
Q_PER_KV = ATTN_HEADS // KV_HEADS
HEAD_DIM = 64
D_ATTN = ATTN_HEADS * HEAD_DIM
D_KV = KV_HEADS * HEAD_DIM
WINDOW = 128
BLOCK = 128
NUM_BUCKETS = 32
MAX_DISTANCE = 128
D_SSM = 512
SSM_GROUP = 16
N_SSM_GROUPS = D_SSM // SSM_GROUP
SSM_STATE = 64
EPS = 1e-6
NEG_INF = -1e30

LANES = 128
CHUNK = 16
CHUNK_W = CHUNK * SSM_GROUP
STATE_W = 4 * SSM_STATE
SLOTS_PER_VREG = LANES // SSM_GROUP

HEAD_PAIR = LANES // HEAD_DIM
PAIRS_PER_KV = Q_PER_KV // HEAD_PAIR
D_KVX = 2 * D_KV

D_PROJ1 = 2 * D_ATTN + 2 * D_KVX + 2 * D_SSM

TOKEN_TILE = 512
VMEM_LIMIT_BYTES = 48 * 1024 * 1024

_F32 = jnp.float32
_BF16 = jnp.bfloat16
_HI = lax.Precision.HIGHEST


def _sigmoid(x):
    return 0.5 * jnp.tanh(0.5 * x) + 0.5


def _silu(x):
    return x * _sigmoid(x)


def _gelu_tanh(x):
    c = math.sqrt(2.0 / math.pi)
    return 0.5 * x * (1.0 + jnp.tanh(c * (x + 0.044715 * (x * x * x))))


def _rms_normalize(x, gain):
    ms = jnp.mean(x * x, axis=-1, keepdims=True)
    return x * lax.rsqrt(ms + EPS) * gain


def _slot_masks(rows):
    lane = lax.broadcasted_iota(jnp.int32, (rows, LANES), 1)
    return [(lane >= p * SSM_GROUP) & (lane < (p + 1) * SSM_GROUP)
            for p in range(SLOTS_PER_VREG)]


def _to_group_major(u_scr, ug_ref, kk, nb):
    masks = _slot_masks(nb)
    for v in range(D_SSM // LANES):
        for half in range(CHUNK // SLOTS_PER_VREG):
            rolled = []
            for t8 in range(SLOTS_PER_VREG):
                r = half * SLOTS_PER_VREG + t8
                piece = u_scr[v, pl.ds(r, nb, stride=CHUNK), :]
                rolled.append(pltpu.roll(piece, t8 * SSM_GROUP, 1) if t8 else piece)
            for p0 in range(SLOTS_PER_VREG):
                acc = rolled[0]
                for t8 in range(1, SLOTS_PER_VREG):
                    acc = jnp.where(masks[(p0 + t8) % SLOTS_PER_VREG], rolled[t8], acc)
                g = v * SLOTS_PER_VREG + p0
                ug_ref[g, kk, :, half * LANES:(half + 1) * LANES] = acc.astype(_BF16)


def _from_group_major(yg_ref, y_scr, kk, nb):
    masks = _slot_masks(nb)
    for v in range(D_SSM // LANES):
        for half in range(CHUNK // SLOTS_PER_VREG):
            src = [yg_ref[v * SLOTS_PER_VREG + p0, kk, :,
                          half * LANES:(half + 1) * LANES].astype(_F32)
                   for p0 in range(SLOTS_PER_VREG)]
            for t8 in range(SLOTS_PER_VREG):
                acc = src[0]
                for p0 in range(1, SLOTS_PER_VREG):
                    acc = jnp.where(masks[(p0 + t8) % SLOTS_PER_VREG], src[p0], acc)
                if t8:
                    acc = pltpu.roll(acc, LANES - t8 * SSM_GROUP, 1)
                r = half * SLOTS_PER_VREG + t8
                y_scr[v, pl.ds(r, nb, stride=CHUNK), :] = acc


def _inproj_kernel(x_ref, gain_ref, w_ref, q_ref, k_ref, v_ref, za_ref, ug_ref, zs_ref,
                   *u_scrs):
    nb, nt = x_ref.shape[0], x_ref.shape[1]
    rows = nb * CHUNK
    for kk in range(nt // CHUNK):
        steps = slice(kk * CHUNK, (kk + 1) * CHUNK)
        x = x_ref[:, steps, :].reshape(rows, D_MODEL)
        h = _rms_normalize(x, gain_ref[...]).astype(_BF16)
        proj = jnp.dot(h, w_ref[...], preferred_element_type=_F32)

        def put(ref, val):
            ref[:, steps, :] = val.astype(_BF16).reshape(nb, CHUNK, val.shape[-1])

        o = 0
        for v in range(D_SSM // LANES):
            u_scrs[kk][v] = proj[:, o + v * LANES:o + (v + 1) * LANES]
        o += D_SSM
        _to_group_major(u_scrs[kk], ug_ref, kk, nb)
        put(zs_ref, _silu(proj[:, o:o + D_SSM]))
        o += D_SSM
        put(za_ref, _silu(proj[:, o:o + D_ATTN]))
        o += D_ATTN
        put(q_ref, proj[:, o:o + D_ATTN] * (HEAD_DIM ** -0.5))
        o += D_ATTN
        put(k_ref, proj[:, o:o + D_KVX])
        o += D_KVX
        put(v_ref, proj[:, o:o + D_KVX])


def _inproj(x, gain, w1):
    bsz, s, _ = x.shape
    nt = TOKEN_TILE // bsz
    tok = lambda i: (0, i, 0)
    fixed = lambda i: (0, 0)
    outs = [D_ATTN, D_KVX, D_KVX, D_ATTN, None, D_SSM]
    act = lambda n: (pl.BlockSpec((bsz, nt, n), tok), jax.ShapeDtypeStruct((bsz, s, n), _BF16))
    ug = (pl.BlockSpec((N_SSM_GROUPS, nt // CHUNK, bsz, CHUNK_W), lambda i: (0, i, 0, 0)),
          jax.ShapeDtypeStruct((N_SSM_GROUPS, s // CHUNK, bsz, CHUNK_W), _BF16))
    specs, shapes = zip(*[ug if n is None else act(n) for n in outs])
    return pl.pallas_call(
        _inproj_kernel,
        grid=(s // nt,),
        in_specs=[pl.BlockSpec((bsz, nt, D_MODEL), tok),
                  pl.BlockSpec((1, D_MODEL), fixed),
                  pl.BlockSpec((D_MODEL, D_PROJ1), fixed)],
        out_specs=list(specs),
        out_shape=list(shapes),
        scratch_shapes=[pltpu.VMEM((D_SSM // LANES, bsz * CHUNK, LANES), _F32)
                        for _ in range(nt // CHUNK)],
        compiler_params=pltpu.CompilerParams(
            dimension_semantics=("arbitrary",), vmem_limit_bytes=VMEM_LIMIT_BYTES),
        name="inproj",
    )(x, gain, w1)


def _inproj_weight(w_in):
    o_k, o_v, o_za = D_ATTN, D_ATTN + D_KV, D_ATTN + 2 * D_KV
    o_u, o_zs = o_za + D_ATTN, o_za + D_ATTN + D_SSM
    head = lambda o, j: w_in[:, o + j * HEAD_DIM:o + (j + 1) * HEAD_DIM]
    swapped = lambda o: [head(o, 0), head(o, 1), head(o, 1), head(o, 0)]
    cols = ([w_in[:, o_u:o_u + D_SSM], w_in[:, o_zs:o_zs + D_SSM],
             w_in[:, o_za:o_za + D_ATTN], w_in[:, :D_ATTN]] + swapped(o_k) + swapped(o_v))
    return jnp.concatenate(cols, axis=1).astype(_BF16)


def _ssm_kernel(n_chunks, rows_per_chunk, u_ref, t_ref, s_ref, c_ref, ar_ref, ai_ref,
                y_ref, s_scr, x_scr):
    rb = rows_per_chunk
    u = u_ref[0]
    s_scr[...] = jnp.dot(u, s_ref[0], preferred_element_type=_F32)
    ar = jnp.broadcast_to(ar_ref[0], (rb, 2 * SSM_STATE))
    ai = jnp.broadcast_to(ai_ref[0], (rb, 2 * SSM_STATE))
    is_fwd = lax.broadcasted_iota(jnp.int32, (rb, 2 * SSM_STATE), 1) < SSM_STATE
    is_bwd = jnp.logical_not(is_fwd)
    re_cols = pl.ds(0, 2 * SSM_STATE)
    im_cols = pl.ds(2 * SSM_STATE, 2 * SSM_STATE)

    def step(i, carry):
        st_re, st_im = carry
        rows_f = pl.ds(pl.multiple_of(i * rb, rb), rb)
        rows_b = pl.ds(pl.multiple_of((n_chunks - 1 - i) * rb, rb), rb)
        pltpu.store(x_scr.at[rows_f, re_cols], st_re, mask=is_fwd)
        pltpu.store(x_scr.at[rows_f, im_cols], st_im, mask=is_fwd)
        pltpu.store(x_scr.at[rows_b, re_cols], st_re, mask=is_bwd)
        pltpu.store(x_scr.at[rows_b, im_cols], st_im, mask=is_bwd)
        in_re = jnp.where(is_fwd, s_scr[rows_f, re_cols], s_scr[rows_b, re_cols])
        in_im = jnp.where(is_fwd, s_scr[rows_f, im_cols], s_scr[rows_b, im_cols])
        new_re = ar * st_re - ai * st_im + in_re
        new_im = ar * st_im + ai * st_re + in_im
        return new_re, new_im

    zero = jnp.zeros((rb, 2 * SSM_STATE), _F32)
    lax.fori_loop(0, n_chunks, step, (zero, zero), unroll=4)

    y = jnp.dot(u, t_ref[0], preferred_element_type=_F32)
    y = y + lax.dot_general(x_scr[...].astype(_BF16), c_ref[0], (((1,), (1,)), ((), ())),
                            preferred_element_type=_F32)
    y_ref[0] = y.astype(y_ref.dtype)


def _ssm(ug, tmat, smat, cmat, ar, ai, n_chunks, rows_per_chunk):
    g, rows, _ = ug.shape
    per_group = lambda i: (i, 0, 0)
    return pl.pallas_call(
        functools.partial(_ssm_kernel, n_chunks, rows_per_chunk),
        grid=(g,),
        in_specs=[pl.BlockSpec((1, rows, CHUNK_W), per_group),
                  pl.BlockSpec((1, CHUNK_W, CHUNK_W), per_group),
                  pl.BlockSpec((1, CHUNK_W, STATE_W), per_group),
                  pl.BlockSpec((1, CHUNK_W, STATE_W), per_group),
                  pl.BlockSpec((1, 1, 2 * SSM_STATE), per_group),
                  pl.BlockSpec((1, 1, 2 * SSM_STATE), per_group)],
        out_specs=pl.BlockSpec((1, rows, CHUNK_W), per_group),
        out_shape=jax.ShapeDtypeStruct((g, rows, CHUNK_W), _BF16),
        scratch_shapes=[pltpu.VMEM((rows, STATE_W), _F32),
                        pltpu.VMEM((rows, STATE_W), _F32)],
        compiler_params=pltpu.CompilerParams(
            dimension_semantics=("arbitrary",), vmem_limit_bytes=VMEM_LIMIT_BYTES),
        name="ssm",
    )(ug, tmat, smat, cmat, ar, ai)


def _ssm_matrices(a_re, a_im, log_dt, b_re, b_im, c_re, c_im, d_skip):
    L, G, P, C = CHUNK, N_SSM_GROUPS, SSM_STATE, SSM_GROUP
    both = lambda a: jnp.concatenate([a[0], a[1]], axis=-1)
    dt = jnp.broadcast_to(jnp.exp(log_dt)[..., None], a_re.shape)
    lam = jnp.stack([both(a_re), both(a_im), both(dt)], axis=1)
    bt = jnp.stack([b_re, b_im], axis=0).transpose(2, 0, 4, 1, 3).reshape(G, 2, C, 2 * P)
    ct = jnp.stack([c_re, c_im], axis=0).transpose(2, 0, 3, 1, 4).reshape(G, 2, C, 2 * P)
    dvec = jnp.tile(d_skip.reshape(G, 1, C), (1, 1, L))
    per_group = lambda *blk: pl.BlockSpec((1,) + blk, lambda i: (i,) + (0,) * len(blk))
    mat = jax.ShapeDtypeStruct((G, L * C, L * C), _BF16)
    vec = jax.ShapeDtypeStruct((G, 1, 2 * P), _F32)
    return pl.pallas_call(
        _ssm_prep_kernel,
        grid=(G,),
        in_specs=[per_group(3, 2 * P), per_group(2, C, 2 * P), per_group(2, C, 2 * P),
                  per_group(1, L * C)],
        out_specs=[per_group(L * C, L * C), per_group(L * C, STATE_W),
                   per_group(L * C, STATE_W), per_group(1, 2 * P), per_group(1, 2 * P)],
        out_shape=[mat, mat, mat, vec, vec],
        compiler_params=pltpu.CompilerParams(dimension_semantics=("arbitrary",)),
        name="ssm_prep",
    )(lam, bt, ct, dvec)


def _ssm_prep_kernel(lam_ref, bt_ref, ct_ref, dvec_ref, t_ref, s_ref, c_ref, ar_ref, ai_ref):
    L, P, C = CHUNK, SSM_STATE, SSM_GROUP
    g8 = jnp.bitwise_and(pl.program_id(0), SLOTS_PER_VREG - 1)

    def time_of_slot(slot):
        return (jnp.bitwise_and(slot, -SLOTS_PER_VREG)
                + jnp.bitwise_and(slot - g8, SLOTS_PER_VREG - 1))

    def cmul(x_re, x_im, y_re, y_im):
        return x_re * y_re - x_im * y_im, x_re * y_im + x_im * y_re

    a_re, a_im, dt = lam_ref[0, 0:1], lam_ref[0, 1:2], lam_ref[0, 2:3]
    lr, li = a_re * dt, a_im * dt

    def cpow(tau):
        mag, ang = jnp.exp(lr * tau), li * tau
        return mag * jnp.cos(ang), mag * jnp.sin(ang)

    ab_re, ab_im = cpow(jnp.ones_like(lr))
    den = a_re * a_re + a_im * a_im
    co_re = ((ab_re - 1.0) * a_re + ab_im * a_im) / den
    co_im = (ab_im * a_re - (ab_re - 1.0) * a_im) / den
    bb_re, bb_im = cmul(bt_ref[0, 0], bt_ref[0, 1], co_re, co_im)
    cc_re, cc_im = ct_ref[0, 0], ct_ref[0, 1]

    def outer(p, m):
        (p_re, p_im), (m_re, m_im) = p, m
        blocks = [cmul(p_re[s:s + 1], p_im[s:s + 1], m_re, m_im) for s in range(L)]
        return (jnp.concatenate([b[0] for b in blocks], axis=0),
                jnp.concatenate([b[1] for b in blocks], axis=0))

    j = time_of_slot(lax.broadcasted_iota(jnp.int32, (L, 2 * P), 0)).astype(_F32)
    fwd = lax.broadcasted_iota(jnp.int32, (L, 2 * P), 1) < P
    s_re, s_im = outer(cpow(jnp.where(fwd, L - 1 - j, j)), (bb_re, bb_im))
    s_ref[0] = jnp.concatenate([s_re, s_im], axis=1).astype(_BF16)
    w_re, w_im = outer(cpow(jnp.where(fwd, j + 1, L - j)), (cc_re, cc_im))
    wcat = jnp.concatenate([w_re, -w_im], axis=1)
    c_ref[0] = wcat.astype(_BF16)
    l_re, l_im = outer(cpow(jnp.where(fwd, -1 - j, j - L)), (bb_re, bb_im))
    lcat = jnp.concatenate([l_re, l_im], axis=1)
    row = lax.broadcasted_iota(jnp.int32, (L * C, L * C), 0)
    col = lax.broadcasted_iota(jnp.int32, (L * C, L * C), 1)
    is_fwd_state = jnp.bitwise_and(col, P) == 0
    contract_state = (((1,), (1,)), ((), ()))
    mf = lax.dot_general(jnp.where(is_fwd_state, lcat, 0.0), wcat, contract_state,
                         precision=_HI, preferred_element_type=_F32)
    mb = lax.dot_general(jnp.where(is_fwd_state, 0.0, lcat), wcat, contract_state,
                         precision=_HI, preferred_element_type=_F32)
    lane_to_slot = C.bit_length() - 1
    jj = time_of_slot(jnp.right_shift(row, lane_to_slot))
    tt = time_of_slot(jnp.right_shift(col, lane_to_slot))
    tm = (jnp.where(tt >= jj, mf, 0.0) + jnp.where(tt <= jj, mb, 0.0)
          + jnp.where(row == col, dvec_ref[0], 0.0))
    t_ref[0] = tm.astype(_BF16)
    al_re, al_im = cpow(jnp.full_like(lr, float(L)))
    ar_ref[0] = al_re
    ai_ref[0] = al_im


def _attn_kernel(sink_ref, q_ref, k0_ref, k1_ref, k2_ref, v0_ref, v1_ref, v2_ref,
                 bias_ref, za_ref, o_ref):
    q = q_ref[0]
    k_blocks = [k0_ref[0], k1_ref[0], k2_ref[0]]
    v_blocks = [v0_ref[0], v1_ref[0], v2_ref[0]]
    lane = lax.broadcasted_iota(jnp.int32, (BLOCK, LANES), 1)
    halves = [lane < HEAD_DIM, lane >= HEAD_DIM]
    keep = [h.astype(_F32).astype(_BF16) for h in halves]
    nk = 3 * BLOCK
    slabs_out = []
    for j in range(KV_HEADS):
        src = lambda blocks, e: [b[:, (j ^ e) * LANES:((j ^ e) + 1) * LANES] for b in blocks]
        kcat = jnp.concatenate(
            [kb * keep[e] for e in range(HEAD_PAIR) for kb in src(k_blocks, e)],
            axis=0)
        vcat = jnp.concatenate(
            [jnp.concatenate([vb * keep[e], keep[e]], axis=1)
             for e in range(HEAD_PAIR) for vb in src(v_blocks, e)], axis=0)
        qs = jnp.concatenate(
            [q[:, (j * PAIRS_PER_KV + i) * LANES:(j * PAIRS_PER_KV + i + 1) * LANES]
             for i in range(PAIRS_PER_KV)], axis=0)
        s = lax.dot_general(qs, kcat, (((1,), (1,)), ((), ())),
                            preferred_element_type=_F32)
        s = s + bias_ref[0, j]
        p_rows, e_rows = [], []
        for i in range(PAIRS_PER_KV):
            p_lanes, e_sink = [], []
            for e in range(HEAD_PAIR):
                sg = s[i * BLOCK:(i + 1) * BLOCK, e * nk:(e + 1) * nk]
                sk = sink_ref[j * Q_PER_KV + i * HEAD_PAIR + e]
                m = jnp.maximum(jnp.max(sg, axis=-1, keepdims=True), sk)
                p_lanes.append(jnp.exp(sg - m).astype(_BF16))
                e_sink.append(jnp.broadcast_to(jnp.exp(sk - m), (BLOCK, LANES)))
            p_rows.append(jnp.concatenate(p_lanes, axis=1))
            e_rows.append(jnp.where(halves[0], e_sink[0], e_sink[1]))
        p = jnp.concatenate(p_rows, axis=0)
        o = jnp.dot(p, vcat, preferred_element_type=_F32)
        den = o[:, LANES:] + jnp.concatenate(e_rows, axis=0)
        on = o[:, :LANES] / den
        slabs_out += [on[i * BLOCK:(i + 1) * BLOCK] for i in range(PAIRS_PER_KV)]
    o_all = jnp.concatenate(slabs_out, axis=1)
    o_ref[0] = (o_all * za_ref[0].astype(_F32)).astype(o_ref.dtype)


def _attention(sink, q, k, v, bias, za):
    b, s, _ = q.shape
    nb = s // BLOCK
    cur = lambda i, n: (i, n, 0)
    prev = lambda i, n: (i, jnp.maximum(n - 1, 0), 0)
    nxt = lambda i, n: (i, jnp.minimum(n + 1, nb - 1), 0)
    edge = lambda i, n: (jnp.where(n == 0, 0, jnp.where(n == nb - 1, 2, 1)), 0, 0, 0)
    kv_spec = lambda f: pl.BlockSpec((1, BLOCK, D_KVX), f)
    return pl.pallas_call(
        _attn_kernel,
        grid=(b, nb),
        in_specs=[pl.BlockSpec(memory_space=pltpu.SMEM),
                  pl.BlockSpec((1, BLOCK, D_ATTN), cur),
                  kv_spec(prev), kv_spec(cur), kv_spec(nxt),
                  kv_spec(prev), kv_spec(cur), kv_spec(nxt),
                  pl.BlockSpec((1,) + bias.shape[1:], edge),
                  pl.BlockSpec((1, BLOCK, D_ATTN), cur)],
        out_specs=pl.BlockSpec((1, BLOCK, D_ATTN), cur),
        out_shape=jax.ShapeDtypeStruct((b, s, D_ATTN), _BF16),
        compiler_params=pltpu.CompilerParams(
            dimension_semantics=("arbitrary", "arbitrary"),
            vmem_limit_bytes=VMEM_LIMIT_BYTES),
        name="attn",
    )(sink, q, k, k, k, v, v, v, bias, za)


def _t5_bucket_np(rel):
    half = NUM_BUCKETS // 2
    ret = (rel > 0).astype(np.int64) * half
    n = np.abs(rel)
    max_exact = half // 2
    nf = np.maximum(n, 1).astype(np.float64)
    large = max_exact + (np.log(nf / max_exact) / math.log(MAX_DISTANCE / max_exact)
                         * (half - max_exact)).astype(np.int64)
    large = np.minimum(large, half - 1)
    return ret + np.where(n < max_exact, n, large)


def _attn_bias(rel_table):
    rel = (np.arange(3 * BLOCK)[None, :] - BLOCK) - np.arange(BLOCK)[:, None]
    onehot = (_t5_bucket_np(rel)[None] == np.arange(NUM_BUCKETS)[:, None, None])
    bias = jnp.einsum('bh,bqk->hqk', rel_table.astype(_F32),
                      jnp.asarray(onehot, _F32), precision=_HI)
    band = np.abs(rel) <= WINDOW
    slot = np.arange(3 * BLOCK)[None, :] // BLOCK
    valid = np.stack([band & (slot != 0), band, band & (slot != 2)])
    bias = jnp.where(jnp.asarray(valid)[:, None], bias[None], NEG_INF)
    bias = bias.reshape(3, KV_HEADS, PAIRS_PER_KV, HEAD_PAIR, BLOCK, 3 * BLOCK)
    bias = bias.transpose(0, 1, 2, 4, 3, 5)
    return bias.reshape(3, KV_HEADS, PAIRS_PER_KV * BLOCK, HEAD_PAIR * 3 * BLOCK)


def _merge_kernel(x_ref, gain_ref, wg_ref, bg_ref, ao_ref, yg_ref, zs_ref, wglu_ref,
                  bglu_ref, wba_ref, wbs_ref, wout_ref, fgain_ref, o_ref, *y_scrs):
    nb, nt = x_ref.shape[0], x_ref.shape[1]
    rows = nb * CHUNK
    for kk in range(nt // CHUNK):
        steps = slice(kk * CHUNK, (kk + 1) * CHUNK)
        x = x_ref[:, steps, :].reshape(rows, D_MODEL)
        h = _rms_normalize(x, gain_ref[...]).astype(_BF16)
        gates = _sigmoid(jnp.dot(h, wg_ref[...], preferred_element_type=_F32) + bg_ref[...])
        _from_group_major(yg_ref, y_scrs[kk], kk, nb)
        y = jnp.concatenate([y_scrs[kk][v] for v in range(D_SSM // LANES)], axis=1)
        y = _gelu_tanh(y)
        glu = (jnp.dot(y.astype(_BF16), wglu_ref[...], preferred_element_type=_F32)
               + bglu_ref[...])
        ssm = y * _sigmoid(glu) * zs_ref[:, steps, :].reshape(rows, D_SSM).astype(_F32)
        pa = jnp.dot(ao_ref[:, steps, :].reshape(rows, D_ATTN), wba_ref[...],
                     preferred_element_type=_F32)
        ps = jnp.dot(ssm.astype(_BF16), wbs_ref[...], preferred_element_type=_F32)
        merged = gates[:, :D_MODEL] * pa + gates[:, D_MODEL:] * ps
        xn = x + jnp.dot(merged.astype(_BF16), wout_ref[...], preferred_element_type=_F32)
        o_ref[:, steps, :] = _rms_normalize(xn, fgain_ref[...]).reshape(nb, CHUNK, D_MODEL)


def _merge(x, gain, wg, bg, ao, yg, zs, wglu, bglu, wba, wbs, wout, fgain):
    bsz, s, _ = x.shape
    nt = TOKEN_TILE // bsz
    tok = lambda i: (0, i, 0)
    fixed = lambda i: (0, 0)
    full = lambda a: pl.BlockSpec(a.shape, fixed)
    act = lambda n: pl.BlockSpec((bsz, nt, n), tok)
    return pl.pallas_call(
        _merge_kernel,
        grid=(s // nt,),
        in_specs=[act(D_MODEL), full(gain), full(wg), full(bg), act(D_ATTN),
                  pl.BlockSpec((N_SSM_GROUPS, nt // CHUNK, bsz, CHUNK_W),
                               lambda i: (0, i, 0, 0)),
                  act(D_SSM), full(wglu), full(bglu), full(wba), full(wbs), full(wout),
                  full(fgain)],
        out_specs=act(D_MODEL),
        out_shape=jax.ShapeDtypeStruct((bsz, s, D_MODEL), _F32),
        scratch_shapes=[pltpu.VMEM((D_SSM // LANES, bsz * CHUNK, LANES), _F32)
                        for _ in range(nt // CHUNK)],
        compiler_params=pltpu.CompilerParams(
            dimension_semantics=("arbitrary",), vmem_limit_bytes=VMEM_LIMIT_BYTES),
        name="merge",
    )(x, gain, wg, bg, ao, yg, zs, wglu, bglu, wba, wbs, wout, fgain)


def _layer(x, norm_gain, w_in, b_gate, attn_sink, a_re, a_im, log_dt, b_re, b_im, c_re, c_im,
           d_skip, w_glu, b_glu, w_ba, w_bs, w_out, out_gain, bias):
    bsz, s, _ = x.shape
    n_chunks = s // CHUNK
    gain = norm_gain.reshape(1, D_MODEL).astype(_F32)
    q, k, v, za, ug, zs = _inproj(x, gain, _inproj_weight(w_in))

    tmat, smat, cmat, ar, ai = _ssm_matrices(a_re, a_im, log_dt, b_re, b_im, c_re, c_im,
                                             d_skip)
    yg = _ssm(ug.reshape(N_SSM_GROUPS, n_chunks * bsz, CHUNK_W), tmat, smat, cmat, ar, ai,
              n_chunks, bsz)
    yg = yg.reshape(N_SSM_GROUPS, n_chunks, bsz, CHUNK_W)

    ao = _attention(attn_sink.astype(_F32), q, k, v, bias, za)

    w_gate = w_in[:, 2 * D_ATTN + 2 * D_KV + 2 * D_SSM:].astype(_BF16)
    return _merge(x, gain, w_gate, b_gate.reshape(1, -1).astype(_F32), ao, yg, zs,
                  w_glu.astype(_BF16), b_glu.reshape(1, -1).astype(_F32),
                  w_ba.astype(_BF16), w_bs.astype(_BF16), w_out.astype(_BF16),
                  out_gain.reshape(1, D_MODEL).astype(_F32))


def kernel(x, norm_gain, w_in, b_gate, attn_sink, rel_bias_table, ssm_a_re, ssm_a_im,
           ssm_log_dt, ssm_b_re, ssm_b_im, ssm_c_re, ssm_c_im, ssm_d, w_glu, b_glu,
           w_branch_attn, w_branch_ssm, w_out, final_norm_gain):
    depth = norm_gain.shape[0]
    assert depth == 1, "final norm is fused into the single layer's epilogue"
    bias = _attn_bias(rel_bias_table)
    l = 0
    return _layer(x, norm_gain[l], w_in[l], b_gate[l], attn_sink[l], ssm_a_re[l],
                  ssm_a_im[l], ssm_log_dt[l], ssm_b_re[l], ssm_b_im[l], ssm_c_re[l],
                  ssm_c_im[l], ssm_d[l], w_glu[l], b_glu[l], w_branch_attn[l],
                  w_branch_ssm[l], w_out[l], final_norm_gain, bias)
```

```python
import functools
import math

import jax
import jax.numpy as jnp
import numpy as np
from jax import lax
from jax.experimental import pallas as pl
from jax.experimental.pallas import tpu as pltpu

D_MODEL = 1024
ATTN_HEADS = 8
KV_HEADS = 2
Q_PER_KV = ATTN_HEADS // KV_HEADS
HEAD_DIM = 64
D_ATTN = ATTN_HEADS * HEAD_DIM
D_KV = KV_HEADS * HEAD_DIM
WINDOW = 128
BLOCK = 128
NUM_BUCKETS = 32
MAX_DISTANCE = 128
D_SSM = 512
SSM_GROUP = 16
N_SSM_GROUPS = D_SSM // SSM_GROUP
SSM_STATE = 64
EPS = 1e-6
NEG_INF = -1e30

LANES = 128
CHUNK = 16
CHUNK_W = CHUNK * SSM_GROUP
STATE_W = 4 * SSM_STATE
SLOTS_PER_VREG = LANES // SSM_GROUP

HEAD_PAIR = LANES // HEAD_DIM
PAIRS_PER_KV = Q_PER_KV // HEAD_PAIR
D_KVX = 2 * D_KV
ATTN_Q_BLOCKS = 4

D_PROJ1 = 2 * D_ATTN + 2 * D_KVX + 2 * D_SSM

TOKEN_TILE = 512
VMEM_LIMIT_BYTES = 48 * 1024 * 1024

_F32 = jnp.float32
_BF16 = jnp.bfloat16
_HI = lax.Precision.HIGHEST


def _sigmoid(x):
    return 0.5 * jnp.tanh(0.5 * x) + 0.5


def _silu(x):
    return x * _sigmoid(x)


def _gelu_tanh(x):
    c = math.sqrt(2.0 / math.pi)
    return 0.5 * x * (1.0 + jnp.tanh(c * (x + 0.044715 * (x * x * x))))


def _rms_normalize(x, gain):
    ms = jnp.mean(x * x, axis=-1, keepdims=True)
    return x * lax.rsqrt(ms + EPS) * gain


def _slot_masks(rows):
    lane = lax.broadcasted_iota(jnp.int32, (rows, LANES), 1)
    return [(lane >= p * SSM_GROUP) & (lane < (p + 1) * SSM_GROUP)
            for p in range(SLOTS_PER_VREG)]


def _to_group_major(u_scr, ug_ref, kk, nb):
    masks = _slot_masks(nb)
    for v in range(D_SSM // LANES):
        for half in range(CHUNK // SLOTS_PER_VREG):
            rolled = []
            for t8 in range(SLOTS_PER_VREG):
                r = half * SLOTS_PER_VREG + t8
                piece = u_scr[v, pl.ds(r, nb, stride=CHUNK), :]
                rolled.append(pltpu.roll(piece, t8 * SSM_GROUP, 1) if t8 else piece)
            for p0 in range(SLOTS_PER_VREG):
                acc = rolled[0]
                for t8 in range(1, SLOTS_PER_VREG):
                    acc = jnp.where(masks[(p0 + t8) % SLOTS_PER_VREG], rolled[t8], acc)
                g = v * SLOTS_PER_VREG + p0
                ug_ref[g, kk, :, half * LANES:(half + 1) * LANES] = acc.astype(_BF16)


def _from_group_major(yg_ref, y_scr, kk, nb):
    masks = _slot_masks(nb)
    for v in range(D_SSM // LANES):
        for half in range(CHUNK // SLOTS_PER_VREG):
            src = [yg_ref[v * SLOTS_PER_VREG + p0, kk, :,
                          half * LANES:(half + 1) * LANES].astype(_F32)
                   for p0 in range(SLOTS_PER_VREG)]
            for t8 in range(SLOTS_PER_VREG):
                acc = src[0]
                for p0 in range(1, SLOTS_PER_VREG):
                    acc = jnp.where(masks[(p0 + t8) % SLOTS_PER_VREG], src[p0], acc)
                if t8:
                    acc = pltpu.roll(acc, LANES - t8 * SSM_GROUP, 1)
                r = half * SLOTS_PER_VREG + t8
                y_scr[v, pl.ds(r, nb, stride=CHUNK), :] = acc


def _inproj_kernel(x_ref, gain_ref, w_ref, q_ref, k_ref, v_ref, za_ref, ug_ref, zs_ref,
                   *u_scrs):
    nb, nt = x_ref.shape[0], x_ref.shape[1]
    rows = nb * CHUNK
    for kk in range(nt // CHUNK):
        steps = slice(kk * CHUNK, (kk + 1) * CHUNK)
        x = x_ref[:, steps, :].reshape(rows, D_MODEL)
        h = _rms_normalize(x, gain_ref[...]).astype(_BF16)
        proj = jnp.dot(h, w_ref[...], preferred_element_type=_F32)

        def put(ref, val):
            ref[:, steps, :] = val.astype(_BF16).reshape(nb, CHUNK, val.shape[-1])

        o = 0
        for v in range(D_SSM // LANES):
            u_scrs[kk][v] = proj[:, o + v * LANES:o + (v + 1) * LANES]
        o += D_SSM
        _to_group_major(u_scrs[kk], ug_ref, kk, nb)
        put(zs_ref, _silu(proj[:, o:o + D_SSM]))
        o += D_SSM
        put(za_ref, _silu(proj[:, o:o + D_ATTN]))
        o += D_ATTN
        put(q_ref, proj[:, o:o + D_ATTN] * (HEAD_DIM ** -0.5))
        o += D_ATTN
        put(k_ref, proj[:, o:o + D_KVX])
        o += D_KVX
        put(v_ref, proj[:, o:o + D_KVX])


def _inproj(x, gain, w1):
    bsz, s, _ = x.shape
    nt = TOKEN_TILE // bsz
    tok = lambda i: (0, i, 0)
    fixed = lambda i: (0, 0)
    outs = [D_ATTN, D_KVX, D_KVX, D_ATTN, None, D_SSM]
    act = lambda n: (pl.BlockSpec((bsz, nt, n), tok), jax.ShapeDtypeStruct((bsz, s, n), _BF16))
    ug = (pl.BlockSpec((N_SSM_GROUPS, nt // CHUNK, bsz, CHUNK_W), lambda i: (0, i, 0, 0)),
          jax.ShapeDtypeStruct((N_SSM_GROUPS, s // CHUNK, bsz, CHUNK_W), _BF16))
    specs, shapes = zip(*[ug if n is None else act(n) for n in outs])
    return pl.pallas_call(
        _inproj_kernel,
        grid=(s // nt,),
        in_specs=[pl.BlockSpec((bsz, nt, D_MODEL), tok),
                  pl.BlockSpec((1, D_MODEL), fixed),
                  pl.BlockSpec((D_MODEL, D_PROJ1), fixed)],
        out_specs=list(specs),
        out_shape=list(shapes),
        scratch_shapes=[pltpu.VMEM((D_SSM // LANES, bsz * CHUNK, LANES), _F32)
                        for _ in range(nt // CHUNK)],
        compiler_params=pltpu.CompilerParams(
            dimension_semantics=("arbitrary",), vmem_limit_bytes=VMEM_LIMIT_BYTES),
        name="inproj",
    )(x, gain, w1)


def _inproj_weight(w_in):
    o_k, o_v, o_za = D_ATTN, D_ATTN + D_KV, D_ATTN + 2 * D_KV
    o_u, o_zs = o_za + D_ATTN, o_za + D_ATTN + D_SSM
    head = lambda o, j: w_in[:, o + j * HEAD_DIM:o + (j + 1) * HEAD_DIM]
    swapped = lambda o: [head(o, 0), head(o, 1), head(o, 1), head(o, 0)]
    cols = ([w_in[:, o_u:o_u + D_SSM], w_in[:, o_zs:o_zs + D_SSM],
             w_in[:, o_za:o_za + D_ATTN], w_in[:, :D_ATTN]] + swapped(o_k) + swapped(o_v))
    return jnp.concatenate(cols, axis=1).astype(_BF16)


def _ssm_kernel(n_chunks, rows_per_chunk, u_ref, t_ref, s_ref, c_ref, ar_ref, ai_ref,
                y_ref, s_scr, x_scr):
    rb = rows_per_chunk
    u = u_ref[0]
    s_scr[...] = jnp.dot(u, s_ref[0], preferred_element_type=_F32)
    ar = jnp.broadcast_to(ar_ref[0], (rb, 2 * SSM_STATE))
    ai = jnp.broadcast_to(ai_ref[0], (rb, 2 * SSM_STATE))
    is_fwd = lax.broadcasted_iota(jnp.int32, (rb, 2 * SSM_STATE), 1) < SSM_STATE
    is_bwd = jnp.logical_not(is_fwd)
    re_cols = pl.ds(0, 2 * SSM_STATE)
    im_cols = pl.ds(2 * SSM_STATE, 2 * SSM_STATE)

    def step(i, carry):
        st_re, st_im = carry
        rows_f = pl.ds(pl.multiple_of(i * rb, rb), rb)
        rows_b = pl.ds(pl.multiple_of((n_chunks - 1 - i) * rb, rb), rb)
        pltpu.store(x_scr.at[rows_f, re_cols], st_re, mask=is_fwd)
        pltpu.store(x_scr.at[rows_f, im_cols], st_im, mask=is_fwd)
        pltpu.store(x_scr.at[rows_b, re_cols], st_re, mask=is_bwd)
        pltpu.store(x_scr.at[rows_b, im_cols], st_im, mask=is_bwd)
        in_re = jnp.where(is_fwd, s_scr[rows_f, re_cols], s_scr[rows_b, re_cols])
        in_im = jnp.where(is_fwd, s_scr[rows_f, im_cols], s_scr[rows_b, im_cols])
        new_re = ar * st_re - ai * st_im + in_re
        new_im = ar * st_im + ai * st_re + in_im
        return new_re, new_im

    zero = jnp.zeros((rb, 2 * SSM_STATE), _F32)
    lax.fori_loop(0, n_chunks, step, (zero, zero), unroll=4)

    y = jnp.dot(u, t_ref[0], preferred_element_type=_F32)
    y = y + lax.dot_general(x_scr[...].astype(_BF16), c_ref[0], (((1,), (1,)), ((), ())),
                            preferred_element_type=_F32)
    y_ref[0] = y.astype(y_ref.dtype)


def _ssm(ug, tmat, smat, cmat, ar, ai, n_chunks, rows_per_chunk):
    g, rows, _ = ug.shape
    per_group = lambda i: (i, 0, 0)
    return pl.pallas_call(
        functools.partial(_ssm_kernel, n_chunks, rows_per_chunk),
        grid=(g,),
        in_specs=[pl.BlockSpec((1, rows, CHUNK_W), per_group),
                  pl.BlockSpec((1, CHUNK_W, CHUNK_W), per_group),
                  pl.BlockSpec((1, CHUNK_W, STATE_W), per_group),
                  pl.BlockSpec((1, CHUNK_W, STATE_W), per_group),
                  pl.BlockSpec((1, 1, 2 * SSM_STATE), per_group),
                  pl.BlockSpec((1, 1, 2 * SSM_STATE), per_group)],
        out_specs=pl.BlockSpec((1, rows, CHUNK_W), per_group),
        out_shape=jax.ShapeDtypeStruct((g, rows, CHUNK_W), _BF16),
        scratch_shapes=[pltpu.VMEM((rows, STATE_W), _F32),
                        pltpu.VMEM((rows, STATE_W), _F32)],
        compiler_params=pltpu.CompilerParams(
            dimension_semantics=("arbitrary",), vmem_limit_bytes=VMEM_LIMIT_BYTES),
        name="ssm",
    )(ug, tmat, smat, cmat, ar, ai)


def _ssm_matrices(a_re, a_im, log_dt, b_re, b_im, c_re, c_im, d_skip):
    L, G, P, C = CHUNK, N_SSM_GROUPS, SSM_STATE, SSM_GROUP
    both = lambda a: jnp.concatenate([a[0], a[1]], axis=-1)
    dt = jnp.broadcast_to(jnp.exp(log_dt)[..., None], a_re.shape)
    lam = jnp.stack([both(a_re), both(a_im), both(dt)], axis=1)
    bt = jnp.stack([b_re, b_im], axis=0).transpose(2, 0, 4, 1, 3).reshape(G, 2, C, 2 * P)
    ct = jnp.stack([c_re, c_im], axis=0).transpose(2, 0, 3, 1, 4).reshape(G, 2, C, 2 * P)
    dvec = jnp.tile(d_skip.reshape(G, 1, C), (1, 1, L))
    per_group = lambda *blk: pl.BlockSpec((1,) + blk, lambda i: (i,) + (0,) * len(blk))
    mat = jax.ShapeDtypeStruct((G, L * C, L * C), _BF16)
    vec = jax.ShapeDtypeStruct((G, 1, 2 * P), _F32)
    return pl.pallas_call(
        _ssm_prep_kernel,
        grid=(G,),
        in_specs=[per_group(3, 2 * P), per_group(2, C, 2 * P), per_group(2, C, 2 * P),
                  per_group(1, L * C)],
        out_specs=[per_group(L * C, L * C), per_group(L * C, STATE_W),
                   per_group(L * C, STATE_W), per_group(1, 2 * P), per_group(1, 2 * P)],
        out_shape=[mat, mat, mat, vec, vec],
        compiler_params=pltpu.CompilerParams(dimension_semantics=("arbitrary",)),
        name="ssm_prep",
    )(lam, bt, ct, dvec)


def _ssm_prep_kernel(lam_ref, bt_ref, ct_ref, dvec_ref, t_ref, s_ref, c_ref, ar_ref, ai_ref):
    L, P, C = CHUNK, SSM_STATE, SSM_GROUP
    g8 = jnp.bitwise_and(pl.program_id(0), SLOTS_PER_VREG - 1)

    def time_of_slot(slot):
        return (jnp.bitwise_and(slot, -SLOTS_PER_VREG)
                + jnp.bitwise_and(slot - g8, SLOTS_PER_VREG - 1))

    def cmul(x_re, x_im, y_re, y_im):
        return x_re * y_re - x_im * y_im, x_re * y_im + x_im * y_re

    a_re, a_im, dt = lam_ref[0, 0:1], lam_ref[0, 1:2], lam_ref[0, 2:3]
    lr, li = a_re * dt, a_im * dt

    def cpow(tau):
        mag, ang = jnp.exp(lr * tau), li * tau
        return mag * jnp.cos(ang), mag * jnp.sin(ang)

    ab_re, ab_im = cpow(jnp.ones_like(lr))
    den = a_re * a_re + a_im * a_im
    co_re = ((ab_re - 1.0) * a_re + ab_im * a_im) / den
    co_im = (ab_im * a_re - (ab_re - 1.0) * a_im) / den
    bb_re, bb_im = cmul(bt_ref[0, 0], bt_ref[0, 1], co_re, co_im)
    cc_re, cc_im = ct_ref[0, 0], ct_ref[0, 1]

    def outer(p, m):
        (p_re, p_im), (m_re, m_im) = p, m
        blocks = [cmul(p_re[s:s + 1], p_im[s:s + 1], m_re, m_im) for s in range(L)]
        return (jnp.concatenate([b[0] for b in blocks], axis=0),
                jnp.concatenate([b[1] for b in blocks], axis=0))

    j = time_of_slot(lax.broadcasted_iota(jnp.int32, (L, 2 * P), 0)).astype(_F32)
    fwd = lax.broadcasted_iota(jnp.int32, (L, 2 * P), 1) < P
    s_re, s_im = outer(cpow(jnp.where(fwd, L - 1 - j, j)), (bb_re, bb_im))
    s_ref[0] = jnp.concatenate([s_re, s_im], axis=1).astype(_BF16)
    w_re, w_im = outer(cpow(jnp.where(fwd, j + 1, L - j)), (cc_re, cc_im))
    wcat = jnp.concatenate([w_re, -w_im], axis=1)
    c_ref[0] = wcat.astype(_BF16)
    l_re, l_im = outer(cpow(jnp.where(fwd, -1 - j, j - L)), (bb_re, bb_im))
    lcat = jnp.concatenate([l_re, l_im], axis=1)
    row = lax.broadcasted_iota(jnp.int32, (L * C, L * C), 0)
    col = lax.broadcasted_iota(jnp.int32, (L * C, L * C), 1)
    is_fwd_state = jnp.bitwise_and(col, P) == 0
    contract_state = (((1,), (1,)), ((), ()))
    mf = lax.dot_general(jnp.where(is_fwd_state, lcat, 0.0), wcat, contract_state,
                         precision=_HI, preferred_element_type=_F32)
    mb = lax.dot_general(jnp.where(is_fwd_state, 0.0, lcat), wcat, contract_state,
                         precision=_HI, preferred_element_type=_F32)
    lane_to_slot = C.bit_length() - 1
    jj = time_of_slot(jnp.right_shift(row, lane_to_slot))
    tt = time_of_slot(jnp.right_shift(col, lane_to_slot))
    tm = (jnp.where(tt >= jj, mf, 0.0) + jnp.where(tt <= jj, mb, 0.0)
          + jnp.where(row == col, dvec_ref[0], 0.0))
    t_ref[0] = tm.astype(_BF16)
    al_re, al_im = cpow(jnp.full_like(lr, float(L)))
    ar_ref[0] = al_re
    ai_ref[0] = al_im


def _attn_kernel(sink_ref, q_ref, kp_ref, kc_ref, kn_ref, vp_ref, vc_ref, vn_ref,
                 bias_first_ref, bias_mid_ref, bias_last_ref, za_ref, o_ref):
    nq = q_ref.shape[1] // BLOCK
    rows = lambda ref, c: ref[0, c * BLOCK:(c + 1) * BLOCK]
    k_blocks = [kp_ref[0]] + [rows(kc_ref, c) for c in range(nq)] + [kn_ref[0]]
    v_blocks = [vp_ref[0]] + [rows(vc_ref, c) for c in range(nq)] + [vn_ref[0]]
    lane = lax.broadcasted_iota(jnp.int32, (BLOCK, LANES), 1)
    halves = [lane < HEAD_DIM, lane >= HEAD_DIM]
    keep = [h.astype(_F32).astype(_BF16) for h in halves]
    nk = 3 * BLOCK
    slab = lambda blk, j, e: blk[:, (j ^ e) * LANES:((j ^ e) + 1) * LANES]
    k_half = {(j, e): [slab(kb, j, e) * keep[e] for kb in k_blocks]
              for j in range(KV_HEADS) for e in range(HEAD_PAIR)}
    v_half = {(j, e): [jnp.concatenate([slab(vb, j, e) * keep[e], keep[e]], axis=1)
                       for vb in v_blocks]
              for j in range(KV_HEADS) for e in range(HEAD_PAIR)}
    for c in range(nq):
        bias_ref = bias_first_ref if c == 0 else bias_last_ref if c == nq - 1 else bias_mid_ref
        q = rows(q_ref, c)
        slabs_out = []
        for j in range(KV_HEADS):
            window = lambda halves_of: jnp.concatenate(
                [halves_of[j, e][c + d] for e in range(HEAD_PAIR) for d in range(3)], axis=0)
            kcat = window(k_half)
            vcat = window(v_half)
            qs = jnp.concatenate(
                [q[:, (j * PAIRS_PER_KV + i) * LANES:(j * PAIRS_PER_KV + i + 1) * LANES]
                 for i in range(PAIRS_PER_KV)], axis=0)
            s = lax.dot_general(qs, kcat, (((1,), (1,)), ((), ())),
                                preferred_element_type=_F32)
            s = s + bias_ref[0, j]
            p_rows, e_rows = [], []
            for i in range(PAIRS_PER_KV):
                p_lanes, e_sink = [], []
                for e in range(HEAD_PAIR):
                    sg = s[i * BLOCK:(i + 1) * BLOCK, e * nk:(e + 1) * nk]
                    sk = sink_ref[j * Q_PER_KV + i * HEAD_PAIR + e]
                    m = jnp.maximum(jnp.max(sg, axis=-1, keepdims=True), sk)
                    p_lanes.append(jnp.exp(sg - m).astype(_BF16))
                    e_sink.append(jnp.broadcast_to(jnp.exp(sk - m), (BLOCK, LANES)))
                p_rows.append(jnp.concatenate(p_lanes, axis=1))
                e_rows.append(jnp.where(halves[0], e_sink[0], e_sink[1]))
            p = jnp.concatenate(p_rows, axis=0)
            o = jnp.dot(p, vcat, preferred_element_type=_F32)
            den = o[:, LANES:] + jnp.concatenate(e_rows, axis=0)
            on = o[:, :LANES] / den
            slabs_out += [on[i * BLOCK:(i + 1) * BLOCK] for i in range(PAIRS_PER_KV)]
        o_all = jnp.concatenate(slabs_out, axis=1)
        o_ref[0, c * BLOCK:(c + 1) * BLOCK, :] = (
            o_all * rows(za_ref, c).astype(_F32)).astype(o_ref.dtype)


def _attention(sink, q, k, v, bias, za):
    b, s, _ = q.shape
    nq = ATTN_Q_BLOCKS
    nb, steps = s // BLOCK, s // (nq * BLOCK)
    cur = lambda i, n: (i, n, 0)
    prev = lambda i, n: (i, jnp.maximum(n * nq - 1, 0), 0)
    nxt = lambda i, n: (i, jnp.minimum((n + 1) * nq, nb - 1), 0)
    first = lambda i, n: (jnp.where(n == 0, 0, 1), 0, 0, 0)
    mid = lambda i, n: (1, 0, 0, 0)
    last = lambda i, n: (jnp.where(n == steps - 1, 2, 1), 0, 0, 0)
    halo = lambda f: pl.BlockSpec((1, BLOCK, D_KVX), f)
    main = lambda w: pl.BlockSpec((1, nq * BLOCK, w), cur)
    bias_spec = lambda f: pl.BlockSpec((1,) + bias.shape[1:], f)
    return pl.pallas_call(
        _attn_kernel,
        grid=(b, steps),
        in_specs=[pl.BlockSpec(memory_space=pltpu.SMEM), main(D_ATTN),
                  halo(prev), main(D_KVX), halo(nxt),
                  halo(prev), main(D_KVX), halo(nxt),
                  bias_spec(first), bias_spec(mid), bias_spec(last), main(D_ATTN)],
        out_specs=main(D_ATTN),
        out_shape=jax.ShapeDtypeStruct((b, s, D_ATTN), _BF16),
        compiler_params=pltpu.CompilerParams(
            dimension_semantics=("arbitrary", "arbitrary"),
            vmem_limit_bytes=VMEM_LIMIT_BYTES),
        name="attn",
    )(sink, q, k, k, k, v, v, v, bias, bias, bias, za)


def _t5_bucket_np(rel):
    half = NUM_BUCKETS // 2
    ret = (rel > 0).astype(np.int64) * half
    n = np.abs(rel)
    max_exact = half // 2
    nf = np.maximum(n, 1).astype(np.float64)
    large = max_exact + (np.log(nf / max_exact) / math.log(MAX_DISTANCE / max_exact)
                         * (half - max_exact)).astype(np.int64)
    large = np.minimum(large, half - 1)
    return ret + np.where(n < max_exact, n, large)


def _attn_bias(rel_table):
    rel = (np.arange(3 * BLOCK)[None, :] - BLOCK) - np.arange(BLOCK)[:, None]
    onehot = (_t5_bucket_np(rel)[None] == np.arange(NUM_BUCKETS)[:, None, None])
    bias = jnp.einsum('bh,bqk->hqk', rel_table.astype(_F32),
                      jnp.asarray(onehot, _F32), precision=_HI)
    band = np.abs(rel) <= WINDOW
    slot = np.arange(3 * BLOCK)[None, :] // BLOCK
    valid = np.stack([band & (slot != 0), band, band & (slot != 2)])
    bias = jnp.where(jnp.asarray(valid)[:, None], bias[None], NEG_INF)
    bias = bias.reshape(3, KV_HEADS, PAIRS_PER_KV, HEAD_PAIR, BLOCK, 3 * BLOCK)
    bias = bias.transpose(0, 1, 2, 4, 3, 5)
    return bias.reshape(3, KV_HEADS, PAIRS_PER_KV * BLOCK, HEAD_PAIR * 3 * BLOCK)


def _merge_kernel(x_ref, gain_ref, wg_ref, bg_ref, ao_ref, yg_ref, zs_ref, wglu_ref,
                  bglu_ref, wba_ref, wbs_ref, wout_ref, fgain_ref, o_ref, *y_scrs):
    nb, nt = x_ref.shape[0], x_ref.shape[1]
    rows = nb * CHUNK
    for kk in range(nt // CHUNK):
        steps = slice(kk * CHUNK, (kk + 1) * CHUNK)
        x = x_ref[:, steps, :].reshape(rows, D_MODEL)
        h = _rms_normalize(x, gain_ref[...]).astype(_BF16)
        gates = _sigmoid(jnp.dot(h, wg_ref[...], preferred_element_type=_F32) + bg_ref[...])
        _from_group_major(yg_ref, y_scrs[kk], kk, nb)
        y = jnp.concatenate([y_scrs[kk][v] for v in range(D_SSM // LANES)], axis=1)
        y = _gelu_tanh(y)
        glu = (jnp.dot(y.astype(_BF16), wglu_ref[...], preferred_element_type=_F32)
               + bglu_ref[...])
        ssm = y * _sigmoid(glu) * zs_ref[:, steps, :].reshape(rows, D_SSM).astype(_F32)
        pa = jnp.dot(ao_ref[:, steps, :].reshape(rows, D_ATTN), wba_ref[...],
                     preferred_element_type=_F32)
        ps = jnp.dot(ssm.astype(_BF16), wbs_ref[...], preferred_element_type=_F32)
        merged = gates[:, :D_MODEL] * pa + gates[:, D_MODEL:] * ps
        xn = x + jnp.dot(merged.astype(_BF16), wout_ref[...], preferred_element_type=_F32)
        o_ref[:, steps, :] = _rms_normalize(xn, fgain_ref[...]).reshape(nb, CHUNK, D_MODEL)


def _merge(x, gain, wg, bg, ao, yg, zs, wglu, bglu, wba, wbs, wout, fgain):
    bsz, s, _ = x.shape
    nt = TOKEN_TILE // bsz
    tok = lambda i: (0, i, 0)
    fixed = lambda i: (0, 0)
    full = lambda a: pl.BlockSpec(a.shape, fixed)
    act = lambda n: pl.BlockSpec((bsz, nt, n), tok)
    return pl.pallas_call(
        _merge_kernel,
        grid=(s // nt,),
        in_specs=[act(D_MODEL), full(gain), full(wg), full(bg), act(D_ATTN),
                  pl.BlockSpec((N_SSM_GROUPS, nt // CHUNK, bsz, CHUNK_W),
                               lambda i: (0, i, 0, 0)),
                  act(D_SSM), full(wglu), full(bglu), full(wba), full(wbs), full(wout),
                  full(fgain)],
        out_specs=act(D_MODEL),
        out_shape=jax.ShapeDtypeStruct((bsz, s, D_MODEL), _F32),
        scratch_shapes=[pltpu.VMEM((D_SSM // LANES, bsz * CHUNK, LANES), _F32)
                        for _ in range(nt // CHUNK)],
        compiler_params=pltpu.CompilerParams(
            dimension_semantics=("arbitrary",), vmem_limit_bytes=VMEM_LIMIT_BYTES),
        name="merge",
    )(x, gain, wg, bg, ao, yg, zs, wglu, bglu, wba, wbs, wout, fgain)


def _layer(x, norm_gain, w_in, b_gate, attn_sink, a_re, a_im, log_dt, b_re, b_im, c_re, c_im,
           d_skip, w_glu, b_glu, w_ba, w_bs, w_out, out_gain, bias):
    bsz, s, _ = x.shape
    n_chunks = s // CHUNK
    gain = norm_gain.reshape(1, D_MODEL).astype(_F32)
    q, k, v, za, ug, zs = _inproj(x, gain, _inproj_weight(w_in))

    tmat, smat, cmat, ar, ai = _ssm_matrices(a_re, a_im, log_dt, b_re, b_im, c_re, c_im,
                                             d_skip)
    yg = _ssm(ug.reshape(N_SSM_GROUPS, n_chunks * bsz, CHUNK_W), tmat, smat, cmat, ar, ai,
              n_chunks, bsz)
    yg = yg.reshape(N_SSM_GROUPS, n_chunks, bsz, CHUNK_W)

    ao = _attention(attn_sink.astype(_F32), q, k, v, bias, za)

    w_gate = w_in[:, 2 * D_ATTN + 2 * D_KV + 2 * D_SSM:].astype(_BF16)
    return _merge(x, gain, w_gate, b_gate.reshape(1, -1).astype(_F32), ao, yg, zs,
                  w_glu.astype(_BF16), b_glu.reshape(1, -1).astype(_F32),
                  w_ba.astype(_BF16), w_bs.astype(_BF16), w_out.astype(_BF16),
                  out_gain.reshape(1, D_MODEL).astype(_F32))


def kernel(x, norm_gain, w_in, b_gate, attn_sink, rel_bias_table, ssm_a_re, ssm_a_im,
           ssm_log_dt, ssm_b_re, ssm_b_im, ssm_c_re, ssm_c_im, ssm_d, w_glu, b_glu,
           w_branch_attn, w_branch_ssm, w_out, final_norm_gain):
    depth = norm_gain.shape[0]
    assert depth == 1, "final norm is fused into the single layer's epilogue"
    bias = _attn_bias(rel_bias_table)
    l = 0
    return _layer(x, norm_gain[l], w_in[l], b_gate[l], attn_sink[l], ssm_a_re[l],
                  ssm_a_im[l], ssm_log_dt[l], ssm_b_re[l], ssm_b_im[l], ssm_c_re[l],
                  ssm_c_im[l], ssm_d[l], w_glu[l], b_glu[l], w_branch_attn[l],
                  w_branch_ssm[l], w_out[l], final_norm_gain, bias)
```

```python
import functools
import math

import jax
import jax.numpy as jnp
import numpy as np
from jax import lax
from jax.experimental import pallas as pl
from jax.experimental.pallas import tpu as pltpu

D_MODEL = 1024
ATTN_HEADS = 8
KV_HEADS = 2
Q_PER_KV = ATTN_HEADS // KV_HEADS
HEAD_DIM = 64
D_ATTN = ATTN_HEADS * HEAD_DIM
D_KV = KV_HEADS * HEAD_DIM
WINDOW = 128
BLOCK = 128
NUM_BUCKETS = 32
MAX_DISTANCE = 128
D_SSM = 512
SSM_GROUP = 16
N_SSM_GROUPS = D_SSM // SSM_GROUP
SSM_STATE = 64
EPS = 1e-6
NEG_INF = -1e30

LANES = 128
CHUNK = 16
CHUNK_W = CHUNK * SSM_GROUP
STATE_W = 4 * SSM_STATE
SLOTS_PER_VREG = LANES // SSM_GROUP
SSM_GROUPS_PER_STEP = 4

HEAD_PAIR = LANES // HEAD_DIM
PAIRS_PER_KV = Q_PER_KV // HEAD_PAIR
D_KVX = 2 * D_KV
ATTN_Q_BLOCKS = 4

D_PROJ1 = 2 * D_ATTN + 2 * D_KVX + 2 * D_SSM

TOKEN_TILE = 1024
VMEM_LIMIT_BYTES = 56 * 1024 * 1024

_F32 = jnp.float32
_BF16 = jnp.bfloat16
_HI = lax.Precision.HIGHEST


def _sigmoid(x):
    return 0.5 * jnp.tanh(0.5 * x) + 0.5


def _silu(x):
    return x * _sigmoid(x)


def _gelu_tanh(x):
    c = math.sqrt(2.0 / math.pi)
    return 0.5 * x * (1.0 + jnp.tanh(c * (x + 0.044715 * (x * x * x))))


def _rms_normalize(x, gain):
    ms = jnp.mean(x * x, axis=-1, keepdims=True)
    return x * lax.rsqrt(ms + EPS) * gain


def _slot_masks(rows):
    lane = lax.broadcasted_iota(jnp.int32, (rows, LANES), 1)
    return [(lane >= p * SSM_GROUP) & (lane < (p + 1) * SSM_GROUP)
            for p in range(SLOTS_PER_VREG)]


def _to_group_major(u_scr, ug_ref, kk, nb):
    masks = _slot_masks(nb)
    for v in range(D_SSM // LANES):
        for half in range(CHUNK // SLOTS_PER_VREG):
            rolled = []
            for t8 in range(SLOTS_PER_VREG):
                r = half * SLOTS_PER_VREG + t8
                piece = u_scr[v, pl.ds(r, nb, stride=CHUNK), :]
                rolled.append(pltpu.roll(piece, t8 * SSM_GROUP, 1) if t8 else piece)
            for p0 in range(SLOTS_PER_VREG):
                acc = rolled[0]
                for t8 in range(1, SLOTS_PER_VREG):
                    acc = jnp.where(masks[(p0 + t8) % SLOTS_PER_VREG], rolled[t8], acc)
                g = v * SLOTS_PER_VREG + p0
                ug_ref[g, kk, :, half * LANES:(half + 1) * LANES] = acc.astype(_BF16)


def _from_group_major(yg_ref, y_scr, kk, nb):
    masks = _slot_masks(nb)
    for v in range(D_SSM // LANES):
        for half in range(CHUNK // SLOTS_PER_VREG):
            src = [yg_ref[v * SLOTS_PER_VREG + p0, kk, :,
                          half * LANES:(half + 1) * LANES].astype(_F32)
                   for p0 in range(SLOTS_PER_VREG)]
            for t8 in range(SLOTS_PER_VREG):
                acc = src[0]
                for p0 in range(1, SLOTS_PER_VREG):
                    acc = jnp.where(masks[(p0 + t8) % SLOTS_PER_VREG], src[p0], acc)
                if t8:
                    acc = pltpu.roll(acc, LANES - t8 * SSM_GROUP, 1)
                r = half * SLOTS_PER_VREG + t8
                y_scr[v, pl.ds(r, nb, stride=CHUNK), :] = acc


def _inproj_kernel(x_ref, gain_ref, w_ref, q_ref, k_ref, v_ref, za_ref, ug_ref, zs_ref,
                   *u_scrs):
    nb, nt = x_ref.shape[0], x_ref.shape[1]
    rows = nb * CHUNK
    for kk in range(nt // CHUNK):
        steps = slice(kk * CHUNK, (kk + 1) * CHUNK)
        x = x_ref[:, steps, :].reshape(rows, D_MODEL)
        h = _rms_normalize(x, gain_ref[...]).astype(_BF16)
        proj = jnp.dot(h, w_ref[...], preferred_element_type=_F32)

        def put(ref, val):
            ref[:, steps, :] = val.astype(_BF16).reshape(nb, CHUNK, val.shape[-1])

        o = 0
        for v in range(D_SSM // LANES):
            u_scrs[kk][v] = proj[:, o + v * LANES:o + (v + 1) * LANES]
        o += D_SSM
        _to_group_major(u_scrs[kk], ug_ref, kk, nb)
        put(zs_ref, _silu(proj[:, o:o + D_SSM]))
        o += D_SSM
        put(za_ref, _silu(proj[:, o:o + D_ATTN]))
        o += D_ATTN
        put(q_ref, proj[:, o:o + D_ATTN] * (HEAD_DIM ** -0.5))
        o += D_ATTN
        put(k_ref, proj[:, o:o + D_KVX])
        o += D_KVX
        put(v_ref, proj[:, o:o + D_KVX])


def _inproj(x, gain, w1):
    bsz, s, _ = x.shape
    nt = TOKEN_TILE // bsz
    tok = lambda i: (0, i, 0)
    fixed = lambda i: (0, 0)
    outs = [D_ATTN, D_KVX, D_KVX, D_ATTN, None, D_SSM]
    act = lambda n: (pl.BlockSpec((bsz, nt, n), tok), jax.ShapeDtypeStruct((bsz, s, n), _BF16))
    ug = (pl.BlockSpec((N_SSM_GROUPS, nt // CHUNK, bsz, CHUNK_W), lambda i: (0, i, 0, 0)),
          jax.ShapeDtypeStruct((N_SSM_GROUPS, s // CHUNK, bsz, CHUNK_W), _BF16))
    specs, shapes = zip(*[ug if n is None else act(n) for n in outs])
    return pl.pallas_call(
        _inproj_kernel,
        grid=(s // nt,),
        in_specs=[pl.BlockSpec((bsz, nt, D_MODEL), tok),
                  pl.BlockSpec((1, D_MODEL), fixed),
                  pl.BlockSpec((D_MODEL, D_PROJ1), fixed)],
        out_specs=list(specs),
        out_shape=list(shapes),
        scratch_shapes=[pltpu.VMEM((D_SSM // LANES, bsz * CHUNK, LANES), _F32)
                        for _ in range(nt // CHUNK)],
        compiler_params=pltpu.CompilerParams(
            dimension_semantics=("arbitrary",), vmem_limit_bytes=VMEM_LIMIT_BYTES),
        name="inproj",
    )(x, gain, w1)


def _inproj_weight(w_in):
    o_k, o_v, o_za = D_ATTN, D_ATTN + D_KV, D_ATTN + 2 * D_KV
    o_u, o_zs = o_za + D_ATTN, o_za + D_ATTN + D_SSM
    head = lambda o, j: w_in[:, o + j * HEAD_DIM:o + (j + 1) * HEAD_DIM]
    swapped = lambda o: [head(o, 0), head(o, 1), head(o, 1), head(o, 0)]
    cols = ([w_in[:, o_u:o_u + D_SSM], w_in[:, o_zs:o_zs + D_SSM],
             w_in[:, o_za:o_za + D_ATTN], w_in[:, :D_ATTN]] + swapped(o_k) + swapped(o_v))
    return jnp.concatenate(cols, axis=1).astype(_BF16)


def _ssm_kernel(n_chunks, rows_per_chunk, u_ref, t_ref, s_ref, c_ref, ar_ref, ai_ref,
                y_ref, s_scr, x_scr):
    rb = rows_per_chunk
    groups = range(u_ref.shape[0])
    for gi in groups:
        s_scr[gi] = jnp.dot(u_ref[gi], s_ref[gi], preferred_element_type=_F32)
    ar = [jnp.broadcast_to(ar_ref[gi], (rb, 2 * SSM_STATE)) for gi in groups]
    ai = [jnp.broadcast_to(ai_ref[gi], (rb, 2 * SSM_STATE)) for gi in groups]
    is_fwd = lax.broadcasted_iota(jnp.int32, (rb, 2 * SSM_STATE), 1) < SSM_STATE
    is_bwd = jnp.logical_not(is_fwd)
    re_cols = pl.ds(0, 2 * SSM_STATE)
    im_cols = pl.ds(2 * SSM_STATE, 2 * SSM_STATE)

    def step(i, carry):
        rows_f = pl.ds(pl.multiple_of(i * rb, rb), rb)
        rows_b = pl.ds(pl.multiple_of((n_chunks - 1 - i) * rb, rb), rb)
        new = []
        for gi in groups:
            st_re, st_im = carry[2 * gi], carry[2 * gi + 1]
            pltpu.store(x_scr.at[gi, rows_f, re_cols], st_re, mask=is_fwd)
            pltpu.store(x_scr.at[gi, rows_f, im_cols], st_im, mask=is_fwd)
            pltpu.store(x_scr.at[gi, rows_b, re_cols], st_re, mask=is_bwd)
            pltpu.store(x_scr.at[gi, rows_b, im_cols], st_im, mask=is_bwd)
            in_re = jnp.where(is_fwd, s_scr[gi, rows_f, re_cols], s_scr[gi, rows_b, re_cols])
            in_im = jnp.where(is_fwd, s_scr[gi, rows_f, im_cols], s_scr[gi, rows_b, im_cols])
            new.append(ar[gi] * st_re - ai[gi] * st_im + in_re)
            new.append(ar[gi] * st_im + ai[gi] * st_re + in_im)
        return tuple(new)

    zero = jnp.zeros((rb, 2 * SSM_STATE), _F32)
    lax.fori_loop(0, n_chunks, step, (zero,) * (2 * len(groups)), unroll=4)

    for gi in groups:
        y = jnp.dot(u_ref[gi], t_ref[gi], preferred_element_type=_F32)
        y = y + lax.dot_general(x_scr[gi].astype(_BF16), c_ref[gi], (((1,), (1,)), ((), ())),
                                preferred_element_type=_F32)
        y_ref[gi] = y.astype(y_ref.dtype)


def _ssm(ug, tmat, smat, cmat, ar, ai, n_chunks, rows_per_chunk):
    g, rows, _ = ug.shape
    gb = SSM_GROUPS_PER_STEP
    per_group = lambda i: (i, 0, 0)
    return pl.pallas_call(
        functools.partial(_ssm_kernel, n_chunks, rows_per_chunk),
        grid=(g // gb,),
        in_specs=[pl.BlockSpec((gb, rows, CHUNK_W), per_group),
                  pl.BlockSpec((gb, CHUNK_W, CHUNK_W), per_group),
                  pl.BlockSpec((gb, CHUNK_W, STATE_W), per_group),
                  pl.BlockSpec((gb, CHUNK_W, STATE_W), per_group),
                  pl.BlockSpec((gb, 1, 2 * SSM_STATE), per_group),
                  pl.BlockSpec((gb, 1, 2 * SSM_STATE), per_group)],
        out_specs=pl.BlockSpec((gb, rows, CHUNK_W), per_group),
        out_shape=jax.ShapeDtypeStruct((g, rows, CHUNK_W), _BF16),
        scratch_shapes=[pltpu.VMEM((gb, rows, STATE_W), _F32),
                        pltpu.VMEM((gb, rows, STATE_W), _F32)],
        compiler_params=pltpu.CompilerParams(
            dimension_semantics=("arbitrary",), vmem_limit_bytes=VMEM_LIMIT_BYTES),
        name="ssm",
    )(ug, tmat, smat, cmat, ar, ai)


def _ssm_matrices(a_re, a_im, log_dt, b_re, b_im, c_re, c_im, d_skip):
    L, G, P, C = CHUNK, N_SSM_GROUPS, SSM_STATE, SSM_GROUP
    both = lambda a: jnp.concatenate([a[0], a[1]], axis=-1)
    dt = jnp.broadcast_to(jnp.exp(log_dt)[..., None], a_re.shape)
    lam = jnp.stack([both(a_re), both(a_im), both(dt)], axis=1)
    bt = jnp.stack([b_re, b_im], axis=0).transpose(2, 0, 4, 1, 3).reshape(G, 2, C, 2 * P)
    ct = jnp.stack([c_re, c_im], axis=0).transpose(2, 0, 3, 1, 4).reshape(G, 2, C, 2 * P)
    dvec = jnp.tile(d_skip.reshape(G, 1, C), (1, 1, L))
    per_group = lambda *blk: pl.BlockSpec((1,) + blk, lambda i: (i,) + (0,) * len(blk))
    mat = jax.ShapeDtypeStruct((G, L * C, L * C), _BF16)
    vec = jax.ShapeDtypeStruct((G, 1, 2 * P), _F32)
    return pl.pallas_call(
        _ssm_prep_kernel,
        grid=(G,),
        in_specs=[per_group(3, 2 * P), per_group(2, C, 2 * P), per_group(2, C, 2 * P),
                  per_group(1, L * C)],
        out_specs=[per_group(L * C, L * C), per_group(L * C, STATE_W),
                   per_group(L * C, STATE_W), per_group(1, 2 * P), per_group(1, 2 * P)],
        out_shape=[mat, mat, mat, vec, vec],
        compiler_params=pltpu.CompilerParams(dimension_semantics=("arbitrary",)),
        name="ssm_prep",
    )(lam, bt, ct, dvec)


def _ssm_prep_kernel(lam_ref, bt_ref, ct_ref, dvec_ref, t_ref, s_ref, c_ref, ar_ref, ai_ref):
    L, P, C = CHUNK, SSM_STATE, SSM_GROUP
    g8 = jnp.bitwise_and(pl.program_id(0), SLOTS_PER_VREG - 1)

    def time_of_slot(slot):
        return (jnp.bitwise_and(slot, -SLOTS_PER_VREG)
                + jnp.bitwise_and(slot - g8, SLOTS_PER_VREG - 1))

    def cmul(x_re, x_im, y_re, y_im):
        return x_re * y_re - x_im * y_im, x_re * y_im + x_im * y_re

    a_re, a_im, dt = lam_ref[0, 0:1], lam_ref[0, 1:2], lam_ref[0, 2:3]
    lr, li = a_re * dt, a_im * dt

    def cpow(tau):
        mag, ang = jnp.exp(lr * tau), li * tau
        return mag * jnp.cos(ang), mag * jnp.sin(ang)

    ab_re, ab_im = cpow(jnp.ones_like(lr))
    den = a_re * a_re + a_im * a_im
    co_re = ((ab_re - 1.0) * a_re + ab_im * a_im) / den
    co_im = (ab_im * a_re - (ab_re - 1.0) * a_im) / den
    bb_re, bb_im = cmul(bt_ref[0, 0], bt_ref[0, 1], co_re, co_im)
    cc_re, cc_im = ct_ref[0, 0], ct_ref[0, 1]

    def outer(p, m):
        (p_re, p_im), (m_re, m_im) = p, m
        blocks = [cmul(p_re[s:s + 1], p_im[s:s + 1], m_re, m_im) for s in range(L)]
        return (jnp.concatenate([b[0] for b in blocks], axis=0),
                jnp.concatenate([b[1] for b in blocks], axis=0))

    j = time_of_slot(lax.broadcasted_iota(jnp.int32, (L, 2 * P), 0)).astype(_F32)
    fwd = lax.broadcasted_iota(jnp.int32, (L, 2 * P), 1) < P
    s_re, s_im = outer(cpow(jnp.where(fwd, L - 1 - j, j)), (bb_re, bb_im))
    s_ref[0] = jnp.concatenate([s_re, s_im], axis=1).astype(_BF16)
    w_re, w_im = outer(cpow(jnp.where(fwd, j + 1, L - j)), (cc_re, cc_im))
    wcat = jnp.concatenate([w_re, -w_im], axis=1)
    c_ref[0] = wcat.astype(_BF16)
    l_re, l_im = outer(cpow(jnp.where(fwd, -1 - j, j - L)), (bb_re, bb_im))
    lcat = jnp.concatenate([l_re, l_im], axis=1)
    row = lax.broadcasted_iota(jnp.int32, (L * C, L * C), 0)
    col = lax.broadcasted_iota(jnp.int32, (L * C, L * C), 1)
    is_fwd_state = jnp.bitwise_and(col, P) == 0
    contract_state = (((1,), (1,)), ((), ()))
    mf = lax.dot_general(jnp.where(is_fwd_state, lcat, 0.0), wcat, contract_state,
                         precision=_HI, preferred_element_type=_F32)
    mb = lax.dot_general(jnp.where(is_fwd_state, 0.0, lcat), wcat, contract_state,
                         precision=_HI, preferred_element_type=_F32)
    lane_to_slot = C.bit_length() - 1
    jj = time_of_slot(jnp.right_shift(row, lane_to_slot))
    tt = time_of_slot(jnp.right_shift(col, lane_to_slot))
    tm = (jnp.where(tt >= jj, mf, 0.0) + jnp.where(tt <= jj, mb, 0.0)
          + jnp.where(row == col, dvec_ref[0], 0.0))
    t_ref[0] = tm.astype(_BF16)
    al_re, al_im = cpow(jnp.full_like(lr, float(L)))
    ar_ref[0] = al_re
    ai_ref[0] = al_im


def _attn_kernel(sink_ref, q_ref, kp_ref, kc_ref, kn_ref, vp_ref, vc_ref, vn_ref,
                 bias_first_ref, bias_mid_ref, bias_last_ref, za_ref, o_ref):
    nq = q_ref.shape[1] // BLOCK
    rows = lambda ref, c: ref[0, c * BLOCK:(c + 1) * BLOCK]
    k_blocks = [kp_ref[0]] + [rows(kc_ref, c) for c in range(nq)] + [kn_ref[0]]
    v_blocks = [vp_ref[0]] + [rows(vc_ref, c) for c in range(nq)] + [vn_ref[0]]
    lane = lax.broadcasted_iota(jnp.int32, (BLOCK, LANES), 1)
    halves = [lane < HEAD_DIM, lane >= HEAD_DIM]
    keep = [h.astype(_F32).astype(_BF16) for h in halves]
    nk = 3 * BLOCK
    slab = lambda blk, j, e: blk[:, (j ^ e) * LANES:((j ^ e) + 1) * LANES]
    k_half = {(j, e): [slab(kb, j, e) * keep[e] for kb in k_blocks]
              for j in range(KV_HEADS) for e in range(HEAD_PAIR)}
    v_half = {(j, e): [jnp.concatenate([slab(vb, j, e) * keep[e], keep[e]], axis=1)
                       for vb in v_blocks]
              for j in range(KV_HEADS) for e in range(HEAD_PAIR)}
    for c in range(nq):
        bias_ref = bias_first_ref if c == 0 else bias_last_ref if c == nq - 1 else bias_mid_ref
        q = rows(q_ref, c)
        slabs_out = []
        for j in range(KV_HEADS):
            window = lambda halves_of: jnp.concatenate(
                [halves_of[j, e][c + d] for e in range(HEAD_PAIR) for d in range(3)], axis=0)
            kcat = window(k_half)
            vcat = window(v_half)
            qs = jnp.concatenate(
                [q[:, (j * PAIRS_PER_KV + i) * LANES:(j * PAIRS_PER_KV + i + 1) * LANES]
                 for i in range(PAIRS_PER_KV)], axis=0)
            s = lax.dot_general(qs, kcat, (((1,), (1,)), ((), ())),
                                preferred_element_type=_F32)
            s = s + bias_ref[0, j]
            p_rows, e_rows = [], []
            for i in range(PAIRS_PER_KV):
                p_lanes, e_sink = [], []
                for e in range(HEAD_PAIR):
                    sg = s[i * BLOCK:(i + 1) * BLOCK, e * nk:(e + 1) * nk]
                    sk = sink_ref[j * Q_PER_KV + i * HEAD_PAIR + e]
                    m = jnp.maximum(jnp.max(sg, axis=-1, keepdims=True), sk)
                    p_lanes.append(jnp.exp(sg - m).astype(_BF16))
                    e_sink.append(jnp.broadcast_to(jnp.exp(sk - m), (BLOCK, LANES)))
                p_rows.append(jnp.concatenate(p_lanes, axis=1))
                e_rows.append(jnp.where(halves[0], e_sink[0], e_sink[1]))
            p = jnp.concatenate(p_rows, axis=0)
            o = jnp.dot(p, vcat, preferred_element_type=_F32)
            den = o[:, LANES:] + jnp.concatenate(e_rows, axis=0)
            on = o[:, :LANES] / den
            slabs_out += [on[i * BLOCK:(i + 1) * BLOCK] for i in range(PAIRS_PER_KV)]
        o_all = jnp.concatenate(slabs_out, axis=1)
        o_ref[0, c * BLOCK:(c + 1) * BLOCK, :] = (
            o_all * rows(za_ref, c).astype(_F32)).astype(o_ref.dtype)


def _attention(sink, q, k, v, bias, za):
    b, s, _ = q.shape
    nq = ATTN_Q_BLOCKS
    nb, steps = s // BLOCK, s // (nq * BLOCK)
    cur = lambda i, n: (i, n, 0)
    prev = lambda i, n: (i, jnp.maximum(n * nq - 1, 0), 0)
    nxt = lambda i, n: (i, jnp.minimum((n + 1) * nq, nb - 1), 0)
    first = lambda i, n: (jnp.where(n == 0, 0, 1), 0, 0, 0)
    mid = lambda i, n: (1, 0, 0, 0)
    last = lambda i, n: (jnp.where(n == steps - 1, 2, 1), 0, 0, 0)
    halo = lambda f: pl.BlockSpec((1, BLOCK, D_KVX), f)
    main = lambda w: pl.BlockSpec((1, nq * BLOCK, w), cur)
    bias_spec = lambda f: pl.BlockSpec((1,) + bias.shape[1:], f)
    return pl.pallas_call(
        _attn_kernel,
        grid=(b, steps),
        in_specs=[pl.BlockSpec(memory_space=pltpu.SMEM), main(D_ATTN),
                  halo(prev), main(D_KVX), halo(nxt),
                  halo(prev), main(D_KVX), halo(nxt),
                  bias_spec(first), bias_spec(mid), bias_spec(last), main(D_ATTN)],
        out_specs=main(D_ATTN),
        out_shape=jax.ShapeDtypeStruct((b, s, D_ATTN), _BF16),
        compiler_params=pltpu.CompilerParams(
            dimension_semantics=("arbitrary", "arbitrary"),
            vmem_limit_bytes=VMEM_LIMIT_BYTES),
        name="attn",
    )(sink, q, k, k, k, v, v, v, bias, bias, bias, za)


def _t5_bucket_np(rel):
    half = NUM_BUCKETS // 2
    ret = (rel > 0).astype(np.int64) * half
    n = np.abs(rel)
    max_exact = half // 2
    nf = np.maximum(n, 1).astype(np.float64)
    large = max_exact + (np.log(nf / max_exact) / math.log(MAX_DISTANCE / max_exact)
                         * (half - max_exact)).astype(np.int64)
    large = np.minimum(large, half - 1)
    return ret + np.where(n < max_exact, n, large)


def _attn_bias(rel_table):
    rel = (np.arange(3 * BLOCK)[None, :] - BLOCK) - np.arange(BLOCK)[:, None]
    onehot = (_t5_bucket_np(rel)[None] == np.arange(NUM_BUCKETS)[:, None, None])
    bias = jnp.einsum('bh,bqk->hqk', rel_table.astype(_F32),
                      jnp.asarray(onehot, _F32), precision=_HI)
    band = np.abs(rel) <= WINDOW
    slot = np.arange(3 * BLOCK)[None, :] // BLOCK
    valid = np.stack([band & (slot != 0), band, band & (slot != 2)])
    bias = jnp.where(jnp.asarray(valid)[:, None], bias[None], NEG_INF)
    bias = bias.reshape(3, KV_HEADS, PAIRS_PER_KV, HEAD_PAIR, BLOCK, 3 * BLOCK)
    bias = bias.transpose(0, 1, 2, 4, 3, 5)
    return bias.reshape(3, KV_HEADS, PAIRS_PER_KV * BLOCK, HEAD_PAIR * 3 * BLOCK)


def _merge_kernel(x_ref, gain_ref, wg_ref, bg_ref, ao_ref, yg_ref, zs_ref, wglu_ref,
                  bglu_ref, wba_ref, wbs_ref, wout_ref, fgain_ref, o_ref, *y_scrs):
    nb, nt = x_ref.shape[0], x_ref.shape[1]
    rows = nb * CHUNK
    for kk in range(nt // CHUNK):
        steps = slice(kk * CHUNK, (kk + 1) * CHUNK)
        x = x_ref[:, steps, :].reshape(rows, D_MODEL)
        h = _rms_normalize(x, gain_ref[...]).astype(_BF16)
        gates = _sigmoid(jnp.dot(h, wg_ref[...], preferred_element_type=_F32) + bg_ref[...])
        _from_group_major(yg_ref, y_scrs[kk], kk, nb)
        y = jnp.concatenate([y_scrs[kk][v] for v in range(D_SSM // LANES)], axis=1)
        y = _gelu_tanh(y)
        glu = (jnp.dot(y.astype(_BF16), wglu_ref[...], preferred_element_type=_F32)
               + bglu_ref[...])
        ssm = y * _sigmoid(glu) * zs_ref[:, steps, :].reshape(rows, D_SSM).astype(_F32)
        pa = jnp.dot(ao_ref[:, steps, :].reshape(rows, D_ATTN), wba_ref[...],
                     preferred_element_type=_F32)
        ps = jnp.dot(ssm.astype(_BF16), wbs_ref[...], preferred_element_type=_F32)
        merged = gates[:, :D_MODEL] * pa + gates[:, D_MODEL:] * ps
        xn = x + jnp.dot(merged.astype(_BF16), wout_ref[...], preferred_element_type=_F32)
        o_ref[:, steps, :] = _rms_normalize(xn, fgain_ref[...]).reshape(nb, CHUNK, D_MODEL)


def _merge(x, gain, wg, bg, ao, yg, zs, wglu, bglu, wba, wbs, wout, fgain):
    bsz, s, _ = x.shape
    nt = TOKEN_TILE // bsz
    tok = lambda i: (0, i, 0)
    fixed = lambda i: (0, 0)
    full = lambda a: pl.BlockSpec(a.shape, fixed)
    act = lambda n: pl.BlockSpec((bsz, nt, n), tok)
    return pl.pallas_call(
        _merge_kernel,
        grid=(s // nt,),
        in_specs=[act(D_MODEL), full(gain), full(wg), full(bg), act(D_ATTN),
                  pl.BlockSpec((N_SSM_GROUPS, nt // CHUNK, bsz, CHUNK_W),
                               lambda i: (0, i, 0, 0)),
                  act(D_SSM), full(wglu), full(bglu), full(wba), full(wbs), full(wout),
                  full(fgain)],
        out_specs=act(D_MODEL),
        out_shape=jax.ShapeDtypeStruct((bsz, s, D_MODEL), _F32),
        scratch_shapes=[pltpu.VMEM((D_SSM // LANES, bsz * CHUNK, LANES), _F32)
                        for _ in range(nt // CHUNK)],
        compiler_params=pltpu.CompilerParams(
            dimension_semantics=("arbitrary",), vmem_limit_bytes=VMEM_LIMIT_BYTES),
        name="merge",
    )(x, gain, wg, bg, ao, yg, zs, wglu, bglu, wba, wbs, wout, fgain)


def _layer(x, norm_gain, w_in, b_gate, attn_sink, a_re, a_im, log_dt, b_re, b_im, c_re, c_im,
           d_skip, w_glu, b_glu, w_ba, w_bs, w_out, out_gain, bias):
    bsz, s, _ = x.shape
    n_chunks = s // CHUNK
    gain = norm_gain.reshape(1, D_MODEL).astype(_F32)
    q, k, v, za, ug, zs = _inproj(x, gain, _inproj_weight(w_in))

    tmat, smat, cmat, ar, ai = _ssm_matrices(a_re, a_im, log_dt, b_re, b_im, c_re, c_im,
                                             d_skip)
    yg = _ssm(ug.reshape(N_SSM_GROUPS, n_chunks * bsz, CHUNK_W), tmat, smat, cmat, ar, ai,
              n_chunks, bsz)
    yg = yg.reshape(N_SSM_GROUPS, n_chunks, bsz, CHUNK_W)

    ao = _attention(attn_sink.astype(_F32), q, k, v, bias, za)

    w_gate = w_in[:, 2 * D_ATTN + 2 * D_KV + 2 * D_SSM:].astype(_BF16)
    return _merge(x, gain, w_gate, b_gate.reshape(1, -1).astype(_F32), ao, yg, zs,
                  w_glu.astype(_BF16), b_glu.reshape(1, -1).astype(_F32),
                  w_ba.astype(_BF16), w_bs.astype(_BF16), w_out.astype(_BF16),
                  out_gain.reshape(1, D_MODEL).astype(_F32))


def kernel(x, norm_gain, w_in, b_gate, attn_sink, rel_bias_table, ssm_a_re, ssm_a_im,
           ssm_log_dt, ssm_b_re, ssm_b_im, ssm_c_re, ssm_c_im, ssm_d, w_glu, b_glu,
           w_branch_attn, w_branch_ssm, w_out, final_norm_gain):
    depth = norm_gain.shape[0]
    assert depth == 1, "final norm is fused into the single layer's epilogue"
    bias = _attn_bias(rel_bias_table)
    l = 0
    return _layer(x, norm_gain[l], w_in[l], b_gate[l], attn_sink[l], ssm_a_re[l],
                  ssm_a_im[l], ssm_log_dt[l], ssm_b_re[l], ssm_b_im[l], ssm_c_re[l],
                  ssm_c_im[l], ssm_d[l], w_glu[l], b_glu[l], w_branch_attn[l],
                  w_branch_ssm[l], w_out[l], final_norm_gain, bias)
```

```python
import functools
import math

import jax
import jax.numpy as jnp
import numpy as np
from jax import lax
from jax.experimental import pallas as pl
from jax.experimental.pallas import tpu as pltpu

D_MODEL = 1024
ATTN_HEADS = 8
KV_HEADS = 2
Q_PER_KV = ATTN_HEADS // KV_HEADS
HEAD_DIM = 64
D_ATTN = ATTN_HEADS * HEAD_DIM
D_KV = KV_HEADS * HEAD_DIM
WINDOW = 128
BLOCK = 128
NUM_BUCKETS = 32
MAX_DISTANCE = 128
D_SSM = 512
SSM_GROUP = 16
N_SSM_GROUPS = D_SSM // SSM_GROUP
SSM_STATE = 64
EPS = 1e-6
NEG_INF = -1e30

LANES = 128
CHUNK = 16
CHUNK_W = CHUNK * SSM_GROUP
STATE_W = 4 * SSM_STATE
SLOTS_PER_VREG = LANES // SSM_GROUP
SSM_GROUPS_PER_STEP = 4

HEAD_PAIR = LANES // HEAD_DIM
PAIRS_PER_KV = Q_PER_KV // HEAD_PAIR
assert D_KV == LANES and KV_HEADS == HEAD_PAIR
ATTN_Q_BLOCKS = 4
LOG2_E = math.log2(math.e)
Q_SCALE = HEAD_DIM ** -0.5 * LOG2_E

TOKEN_TILE = 1024
VMEM_LIMIT_BYTES = 56 * 1024 * 1024

_F32 = jnp.float32
_BF16 = jnp.bfloat16
_HI = lax.Precision.HIGHEST


def _sigmoid(x):
    return 0.5 * jnp.tanh(0.5 * x) + 0.5


def _silu(x):
    return x * _sigmoid(x)


def _gelu_tanh(x):
    c = math.sqrt(2.0 / math.pi)
    return 0.5 * x * (1.0 + jnp.tanh(c * (x + 0.044715 * (x * x * x))))


def _rms_normalize(x, gain):
    ms = jnp.mean(x * x, axis=-1, keepdims=True)
    return x * lax.rsqrt(ms + EPS) * gain


def _slot_masks(rows):
    lane = lax.broadcasted_iota(jnp.int32, (rows, LANES), 1)
    return [(lane >= p * SSM_GROUP) & (lane < (p + 1) * SSM_GROUP)
            for p in range(SLOTS_PER_VREG)]


def _to_group_major(u_scr, ug_ref, kk, nb):
    masks = _slot_masks(nb)
    for v in range(D_SSM // LANES):
        for half in range(CHUNK // SLOTS_PER_VREG):
            rolled = []
            for t8 in range(SLOTS_PER_VREG):
                r = half * SLOTS_PER_VREG + t8
                piece = u_scr[v, pl.ds(r, nb, stride=CHUNK), :]
                rolled.append(pltpu.roll(piece, t8 * SSM_GROUP, 1) if t8 else piece)
            for p0 in range(SLOTS_PER_VREG):
                acc = rolled[0]
                for t8 in range(1, SLOTS_PER_VREG):
                    acc = jnp.where(masks[(p0 + t8) % SLOTS_PER_VREG], rolled[t8], acc)
                g = v * SLOTS_PER_VREG + p0
                ug_ref[g, kk, :, half * LANES:(half + 1) * LANES] = acc.astype(_BF16)


def _from_group_major(yg_ref, y_scr, kk, nb):
    masks = _slot_masks(nb)
    for v in range(D_SSM // LANES):
        for half in range(CHUNK // SLOTS_PER_VREG):
            src = [yg_ref[v * SLOTS_PER_VREG + p0, kk, :,
                          half * LANES:(half + 1) * LANES].astype(_F32)
                   for p0 in range(SLOTS_PER_VREG)]
            for t8 in range(SLOTS_PER_VREG):
                acc = src[0]
                for p0 in range(1, SLOTS_PER_VREG):
                    acc = jnp.where(masks[(p0 + t8) % SLOTS_PER_VREG], src[p0], acc)
                if t8:
                    acc = pltpu.roll(acc, LANES - t8 * SSM_GROUP, 1)
                r = half * SLOTS_PER_VREG + t8
                y_scr[v, pl.ds(r, nb, stride=CHUNK), :] = acc


def _inproj_kernel(x_ref, gain_ref, w_ssm_ref, w_za_ref, w_qkv_ref, q_ref, k_ref, v_ref,
                   za_ref, ug_ref, zs_ref, *u_scrs):
    nb, nt = x_ref.shape[0], x_ref.shape[1]
    rows = nb * CHUNK
    for kk in range(nt // CHUNK):
        steps = slice(kk * CHUNK, (kk + 1) * CHUNK)
        x = x_ref[:, steps, :].reshape(rows, D_MODEL)
        h = _rms_normalize(x, gain_ref[...]).astype(_BF16)

        def put(ref, val):
            ref[:, steps, :] = val.astype(_BF16).reshape(nb, CHUNK, val.shape[-1])

        u_zs = jnp.dot(h, w_ssm_ref[...], preferred_element_type=_F32)
        for v in range(D_SSM // LANES):
            u_scrs[kk][v] = u_zs[:, v * LANES:(v + 1) * LANES]
        _to_group_major(u_scrs[kk], ug_ref, kk, nb)
        put(zs_ref, _silu(u_zs[:, D_SSM:]))
        put(za_ref, _silu(jnp.dot(h, w_za_ref[...], preferred_element_type=_F32)))
        qkv = jnp.dot(h, w_qkv_ref[...], preferred_element_type=_F32)
        put(q_ref, qkv[:, :D_ATTN] * Q_SCALE)
        put(k_ref, qkv[:, D_ATTN:D_ATTN + D_KV])
        put(v_ref, qkv[:, D_ATTN + D_KV:])


def _inproj(x, gain, w_in):
    bsz, s, _ = x.shape
    nt = TOKEN_TILE // bsz
    tok = lambda i: (0, i, 0)
    fixed = lambda i: (0, 0)
    o_za = D_ATTN + 2 * D_KV
    o_u = o_za + D_ATTN
    weights = [w_in[:, o_u:o_u + 2 * D_SSM].astype(_BF16),
               w_in[:, o_za:o_u].astype(_BF16), w_in[:, :o_za].astype(_BF16)]
    outs = [D_ATTN, D_KV, D_KV, D_ATTN, None, D_SSM]
    act = lambda n: (pl.BlockSpec((bsz, nt, n), tok), jax.ShapeDtypeStruct((bsz, s, n), _BF16))
    ug = (pl.BlockSpec((N_SSM_GROUPS, nt // CHUNK, bsz, CHUNK_W), lambda i: (0, i, 0, 0)),
          jax.ShapeDtypeStruct((N_SSM_GROUPS, s // CHUNK, bsz, CHUNK_W), _BF16))
    specs, shapes = zip(*[ug if n is None else act(n) for n in outs])
    return pl.pallas_call(
        _inproj_kernel,
        grid=(s // nt,),
        in_specs=[pl.BlockSpec((bsz, nt, D_MODEL), tok),
                  pl.BlockSpec((1, D_MODEL), fixed)]
                 + [pl.BlockSpec(w.shape, fixed) for w in weights],
        out_specs=list(specs),
        out_shape=list(shapes),
        scratch_shapes=[pltpu.VMEM((D_SSM // LANES, bsz * CHUNK, LANES), _F32)
                        for _ in range(nt // CHUNK)],
        compiler_params=pltpu.CompilerParams(
            dimension_semantics=("arbitrary",), vmem_limit_bytes=VMEM_LIMIT_BYTES),
        name="inproj",
    )(x, gain, *weights)


def _ssm_kernel(n_chunks, rows_per_chunk, u_ref, t_ref, s_ref, c_ref, ar_ref, ai_ref,
                y_ref, s_scr, x_scr):
    rb = rows_per_chunk
    groups = range(u_ref.shape[0])
    for gi in groups:
        s_scr[gi] = jnp.dot(u_ref[gi], s_ref[gi], preferred_element_type=_F32)
    ar = [jnp.broadcast_to(ar_ref[gi], (rb, 2 * SSM_STATE)) for gi in groups]
    ai = [jnp.broadcast_to(ai_ref[gi], (rb, 2 * SSM_STATE)) for gi in groups]
    is_fwd = lax.broadcasted_iota(jnp.int32, (rb, 2 * SSM_STATE), 1) < SSM_STATE
    is_bwd = jnp.logical_not(is_fwd)
    re_cols = pl.ds(0, 2 * SSM_STATE)
    im_cols = pl.ds(2 * SSM_STATE, 2 * SSM_STATE)

    def step(i, carry):
        rows_f = pl.ds(pl.multiple_of(i * rb, rb), rb)
        rows_b = pl.ds(pl.multiple_of((n_chunks - 1 - i) * rb, rb), rb)
        new = []
        for gi in groups:
            st_re, st_im = carry[2 * gi], carry[2 * gi + 1]
            pltpu.store(x_scr.at[gi, rows_f, re_cols], st_re, mask=is_fwd)
            pltpu.store(x_scr.at[gi, rows_f, im_cols], st_im, mask=is_fwd)
            pltpu.store(x_scr.at[gi, rows_b, re_cols], st_re, mask=is_bwd)
            pltpu.store(x_scr.at[gi, rows_b, im_cols], st_im, mask=is_bwd)
            in_re = jnp.where(is_fwd, s_scr[gi, rows_f, re_cols], s_scr[gi, rows_b, re_cols])
            in_im = jnp.where(is_fwd, s_scr[gi, rows_f, im_cols], s_scr[gi, rows_b, im_cols])
            new.append(ar[gi] * st_re - ai[gi] * st_im + in_re)
            new.append(ar[gi] * st_im + ai[gi] * st_re + in_im)
        return tuple(new)

    zero = jnp.zeros((rb, 2 * SSM_STATE), _F32)
    lax.fori_loop(0, n_chunks, step, (zero,) * (2 * len(groups)), unroll=4)

    for gi in groups:
        y = jnp.dot(u_ref[gi], t_ref[gi], preferred_element_type=_F32)
        y = y + lax.dot_general(x_scr[gi].astype(_BF16), c_ref[gi], (((1,), (1,)), ((), ())),
                                preferred_element_type=_F32)
        y_ref[gi] = y.astype(y_ref.dtype)


def _ssm(ug, tmat, smat, cmat, ar, ai, n_chunks, rows_per_chunk):
    g, rows, _ = ug.shape
    gb = SSM_GROUPS_PER_STEP
    per_group = lambda i: (i, 0, 0)
    return pl.pallas_call(
        functools.partial(_ssm_kernel, n_chunks, rows_per_chunk),
        grid=(g // gb,),
        in_specs=[pl.BlockSpec((gb, rows, CHUNK_W), per_group),
                  pl.BlockSpec((gb, CHUNK_W, CHUNK_W), per_group),
                  pl.BlockSpec((gb, CHUNK_W, STATE_W), per_group),
                  pl.BlockSpec((gb, CHUNK_W, STATE_W), per_group),
                  pl.BlockSpec((gb, 1, 2 * SSM_STATE), per_group),
                  pl.BlockSpec((gb, 1, 2 * SSM_STATE), per_group)],
        out_specs=pl.BlockSpec((gb, rows, CHUNK_W), per_group),
        out_shape=jax.ShapeDtypeStruct((g, rows, CHUNK_W), _BF16),
        scratch_shapes=[pltpu.VMEM((gb, rows, STATE_W), _F32),
                        pltpu.VMEM((gb, rows, STATE_W), _F32)],
        compiler_params=pltpu.CompilerParams(
            dimension_semantics=("arbitrary",), vmem_limit_bytes=VMEM_LIMIT_BYTES),
        name="ssm",
    )(ug, tmat, smat, cmat, ar, ai)


def _ssm_matrices(a_re, a_im, log_dt, b_re, b_im, c_re, c_im, d_skip):
    L, G, P, C = CHUNK, N_SSM_GROUPS, SSM_STATE, SSM_GROUP
    both = lambda a: jnp.concatenate([a[0], a[1]], axis=-1)
    dt = jnp.broadcast_to(jnp.exp(log_dt)[..., None], a_re.shape)
    lam = jnp.stack([both(a_re), both(a_im), both(dt)], axis=1)
    bt = jnp.stack([b_re, b_im], axis=0).transpose(2, 0, 4, 1, 3).reshape(G, 2, C, 2 * P)
    ct = jnp.stack([c_re, c_im], axis=0).transpose(2, 0, 3, 1, 4).reshape(G, 2, C, 2 * P)
    dvec = jnp.tile(d_skip.reshape(G, 1, C), (1, 1, L))
    gb = SLOTS_PER_VREG
    per_group = lambda *blk: pl.BlockSpec((gb,) + blk, lambda i: (i,) + (0,) * len(blk))
    mat = jax.ShapeDtypeStruct((G, L * C, L * C), _BF16)
    vec = jax.ShapeDtypeStruct((G, 1, 2 * P), _F32)
    return pl.pallas_call(
        _ssm_prep_kernel,
        grid=(G // gb,),
        in_specs=[per_group(3, 2 * P), per_group(2, C, 2 * P), per_group(2, C, 2 * P),
                  per_group(1, L * C)],
        out_specs=[per_group(L * C, L * C), per_group(L * C, STATE_W),
                   per_group(L * C, STATE_W), per_group(1, 2 * P), per_group(1, 2 * P)],
        out_shape=[mat, mat, mat, vec, vec],
        compiler_params=pltpu.CompilerParams(dimension_semantics=("arbitrary",)),
        name="ssm_prep",
    )(lam, bt, ct, dvec)


def _ssm_prep_kernel(lam_ref, bt_ref, ct_ref, dvec_ref, t_ref, s_ref, c_ref, ar_ref, ai_ref):
    for gi in range(lam_ref.shape[0]):
        _ssm_prep_group(gi, lam_ref, bt_ref, ct_ref, dvec_ref, t_ref, s_ref, c_ref, ar_ref,
                        ai_ref)


def _ssm_prep_group(gi, lam_ref, bt_ref, ct_ref, dvec_ref, t_ref, s_ref, c_ref, ar_ref,
                    ai_ref):
    L, P, C = CHUNK, SSM_STATE, SSM_GROUP
    g8 = gi % SLOTS_PER_VREG

    def time_of_slot(slot):
        return (jnp.bitwise_and(slot, -SLOTS_PER_VREG)
                + jnp.bitwise_and(slot - g8, SLOTS_PER_VREG - 1))

    def cmul(x_re, x_im, y_re, y_im):
        return x_re * y_re - x_im * y_im, x_re * y_im + x_im * y_re

    a_re, a_im, dt = lam_ref[gi, 0:1], lam_ref[gi, 1:2], lam_ref[gi, 2:3]
    lr, li = a_re * dt, a_im * dt

    def cpow(tau):
        mag, ang = jnp.exp(lr * tau), li * tau
        return mag * jnp.cos(ang), mag * jnp.sin(ang)

    ab_re, ab_im = cpow(jnp.ones_like(lr))
    den = a_re * a_re + a_im * a_im
    co_re = ((ab_re - 1.0) * a_re + ab_im * a_im) / den
    co_im = (ab_im * a_re - (ab_re - 1.0) * a_im) / den
    bb_re, bb_im = cmul(bt_ref[gi, 0], bt_ref[gi, 1], co_re, co_im)
    cc_re, cc_im = ct_ref[gi, 0], ct_ref[gi, 1]

    def outer(p, m):
        (p_re, p_im), (m_re, m_im) = p, m
        blocks = [cmul(p_re[s:s + 1], p_im[s:s + 1], m_re, m_im) for s in range(L)]
        return (jnp.concatenate([b[0] for b in blocks], axis=0),
                jnp.concatenate([b[1] for b in blocks], axis=0))

    j = time_of_slot(lax.broadcasted_iota(jnp.int32, (L, 2 * P), 0)).astype(_F32)
    fwd = lax.broadcasted_iota(jnp.int32, (L, 2 * P), 1) < P
    s_re, s_im = outer(cpow(jnp.where(fwd, L - 1 - j, j)), (bb_re, bb_im))
    s_ref[gi] = jnp.concatenate([s_re, s_im], axis=1).astype(_BF16)
    w_re, w_im = outer(cpow(jnp.where(fwd, j + 1, L - j)), (cc_re, cc_im))
    wcat = jnp.concatenate([w_re, -w_im], axis=1)
    c_ref[gi] = wcat.astype(_BF16)
    l_re, l_im = outer(cpow(jnp.where(fwd, -1 - j, j - L)), (bb_re, bb_im))
    lcat = jnp.concatenate([l_re, l_im], axis=1)
    row = lax.broadcasted_iota(jnp.int32, (L * C, L * C), 0)
    col = lax.broadcasted_iota(jnp.int32, (L * C, L * C), 1)
    is_fwd_state = jnp.bitwise_and(col, P) == 0
    contract_state = (((1,), (1,)), ((), ()))
    mf = lax.dot_general(jnp.where(is_fwd_state, lcat, 0.0), wcat, contract_state,
                         precision=_HI, preferred_element_type=_F32)
    mb = lax.dot_general(jnp.where(is_fwd_state, 0.0, lcat), wcat, contract_state,
                         precision=_HI, preferred_element_type=_F32)
    lane_to_slot = C.bit_length() - 1
    jj = time_of_slot(jnp.right_shift(row, lane_to_slot))
    tt = time_of_slot(jnp.right_shift(col, lane_to_slot))
    tm = (jnp.where(tt >= jj, mf, 0.0) + jnp.where(tt <= jj, mb, 0.0)
          + jnp.where(row == col, dvec_ref[gi], 0.0))
    t_ref[gi] = tm.astype(_BF16)
    al_re, al_im = cpow(jnp.full_like(lr, float(L)))
    ar_ref[gi] = al_re
    ai_ref[gi] = al_im


def _attn_kernel(sink_ref, q_ref, kp_ref, kc_ref, kn_ref, vp_ref, vc_ref, vn_ref,
                 bias_first_ref, bias_mid_ref, bias_last_ref, za_ref, o_ref):
    nq = q_ref.shape[1] // BLOCK
    rows = lambda ref, c: ref[0, c * BLOCK:(c + 1) * BLOCK]
    k_blocks = [kp_ref[0]] + [rows(kc_ref, c) for c in range(nq)] + [kn_ref[0]]
    v_blocks = [vp_ref[0]] + [rows(vc_ref, c) for c in range(nq)] + [vn_ref[0]]
    lane = lax.broadcasted_iota(jnp.int32, (BLOCK, LANES), 1)
    halves = [lane < HEAD_DIM, lane >= HEAD_DIM]
    keep = [h.astype(_F32).astype(_BF16) for h in halves]
    nk = 3 * BLOCK
    def swap_halves(blk):
        return pltpu.bitcast(pltpu.roll(pltpu.bitcast(blk, jnp.uint32), HEAD_DIM, 1), _BF16)

    k_blocks = [(kb, swap_halves(kb)) for kb in k_blocks]
    v_blocks = [(vb, swap_halves(vb)) for vb in v_blocks]
    slab = lambda blk, j, e: blk[j ^ e]
    k_half = {(j, e): [slab(kb, j, e) * keep[e] for kb in k_blocks]
              for j in range(KV_HEADS) for e in range(HEAD_PAIR)}
    v_half = {(j, e): [jnp.concatenate([slab(vb, j, e) * keep[e], keep[e]], axis=1)
                       for vb in v_blocks]
              for j in range(KV_HEADS) for e in range(HEAD_PAIR)}
    for c in range(nq):
        bias_ref = bias_first_ref if c == 0 else bias_last_ref if c == nq - 1 else bias_mid_ref
        q = rows(q_ref, c)
        slabs_out = []
        for j in range(KV_HEADS):
            window = lambda halves_of: jnp.concatenate(
                [halves_of[j, e][c + d] for e in range(HEAD_PAIR) for d in range(3)], axis=0)
            kcat = window(k_half)
            vcat = window(v_half)
            qs = jnp.concatenate(
                [q[:, (j * PAIRS_PER_KV + i) * LANES:(j * PAIRS_PER_KV + i + 1) * LANES]
                 for i in range(PAIRS_PER_KV)], axis=0)
            s = lax.dot_general(qs, kcat, (((1,), (1,)), ((), ())),
                                preferred_element_type=_F32)
            s = s + bias_ref[0, j]
            p_rows, e_rows = [], []
            for i in range(PAIRS_PER_KV):
                p_lanes, e_sink = [], []
                for e in range(HEAD_PAIR):
                    sg = s[i * BLOCK:(i + 1) * BLOCK, e * nk:(e + 1) * nk]
                    sk = sink_ref[j * Q_PER_KV + i * HEAD_PAIR + e]
                    m = jnp.maximum(jnp.max(sg, axis=-1, keepdims=True), sk)
                    p_lanes.append(jnp.exp2(sg - m).astype(_BF16))
                    e_sink.append(jnp.broadcast_to(jnp.exp2(sk - m), (BLOCK, LANES)))
                p_rows.append(jnp.concatenate(p_lanes, axis=1))
                e_rows.append(jnp.where(halves[0], e_sink[0], e_sink[1]))
            p = jnp.concatenate(p_rows, axis=0)
            o = jnp.dot(p, vcat, preferred_element_type=_F32)
            den = o[:, LANES:] + jnp.concatenate(e_rows, axis=0)
            on = o[:, :LANES] / den
            slabs_out += [on[i * BLOCK:(i + 1) * BLOCK] for i in range(PAIRS_PER_KV)]
        o_all = jnp.concatenate(slabs_out, axis=1)
        o_ref[0, c * BLOCK:(c + 1) * BLOCK, :] = (
            o_all * rows(za_ref, c).astype(_F32)).astype(o_ref.dtype)


def _attention(sink, q, k, v, bias, za):
    b, s, _ = q.shape
    nq = ATTN_Q_BLOCKS
    nb, steps = s // BLOCK, s // (nq * BLOCK)
    cur = lambda i, n: (i, n, 0)
    prev = lambda i, n: (i, jnp.maximum(n * nq - 1, 0), 0)
    nxt = lambda i, n: (i, jnp.minimum((n + 1) * nq, nb - 1), 0)
    first = lambda i, n: (jnp.where(n == 0, 0, 1), 0, 0, 0)
    mid = lambda i, n: (1, 0, 0, 0)
    last = lambda i, n: (jnp.where(n == steps - 1, 2, 1), 0, 0, 0)
    halo = lambda f: pl.BlockSpec((1, BLOCK, D_KV), f)
    main = lambda w: pl.BlockSpec((1, nq * BLOCK, w), cur)
    bias_spec = lambda f: pl.BlockSpec((1,) + bias.shape[1:], f)
    return pl.pallas_call(
        _attn_kernel,
        grid=(b, steps),
        in_specs=[pl.BlockSpec(memory_space=pltpu.SMEM), main(D_ATTN),
                  halo(prev), main(D_KV), halo(nxt),
                  halo(prev), main(D_KV), halo(nxt),
                  bias_spec(first), bias_spec(mid), bias_spec(last), main(D_ATTN)],
        out_specs=main(D_ATTN),
        out_shape=jax.ShapeDtypeStruct((b, s, D_ATTN), _BF16),
        compiler_params=pltpu.CompilerParams(
            dimension_semantics=("arbitrary", "arbitrary"),
            vmem_limit_bytes=VMEM_LIMIT_BYTES),
        name="attn",
    )(sink, q, k, k, k, v, v, v, bias, bias, bias, za)


def _t5_bucket_np(rel):
    half = NUM_BUCKETS // 2
    ret = (rel > 0).astype(np.int64) * half
    n = np.abs(rel)
    max_exact = half // 2
    nf = np.maximum(n, 1).astype(np.float64)
    large = max_exact + (np.log(nf / max_exact) / math.log(MAX_DISTANCE / max_exact)
                         * (half - max_exact)).astype(np.int64)
    large = np.minimum(large, half - 1)
    return ret + np.where(n < max_exact, n, large)


def _attn_bias(rel_table):
    rel = (np.arange(3 * BLOCK)[None, :] - BLOCK) - np.arange(BLOCK)[:, None]
    onehot = (_t5_bucket_np(rel)[None] == np.arange(NUM_BUCKETS)[:, None, None])
    bias = jnp.einsum('bh,bqk->hqk', rel_table.astype(_F32),
                      jnp.asarray(onehot, _F32), precision=_HI)
    band = np.abs(rel) <= WINDOW
    slot = np.arange(3 * BLOCK)[None, :] // BLOCK
    valid = np.stack([band & (slot != 0), band, band & (slot != 2)])
    bias = jnp.where(jnp.asarray(valid)[:, None], LOG2_E * bias[None], NEG_INF)
    bias = bias.reshape(3, KV_HEADS, PAIRS_PER_KV, HEAD_PAIR, BLOCK, 3 * BLOCK)
    bias = bias.transpose(0, 1, 2, 4, 3, 5)
    return bias.reshape(3, KV_HEADS, PAIRS_PER_KV * BLOCK, HEAD_PAIR * 3 * BLOCK)


def _merge_kernel(x_ref, gain_ref, wg_ref, bg_ref, ao_ref, yg_ref, zs_ref, wglu_ref,
                  bglu_ref, wba_ref, wbs_ref, wout_ref, fgain_ref, o_ref, *y_scrs):
    nb, nt = x_ref.shape[0], x_ref.shape[1]
    rows = nb * CHUNK
    for kk in range(nt // CHUNK):
        steps = slice(kk * CHUNK, (kk + 1) * CHUNK)
        x = x_ref[:, steps, :].reshape(rows, D_MODEL)
        h = _rms_normalize(x, gain_ref[...]).astype(_BF16)
        gates = _sigmoid(jnp.dot(h, wg_ref[...], preferred_element_type=_F32) + bg_ref[...])
        _from_group_major(yg_ref, y_scrs[kk], kk, nb)
        y = jnp.concatenate([y_scrs[kk][v] for v in range(D_SSM // LANES)], axis=1)
        y = _gelu_tanh(y)
        glu = (jnp.dot(y.astype(_BF16), wglu_ref[...], preferred_element_type=_F32)
               + bglu_ref[...])
        ssm = y * _sigmoid(glu) * zs_ref[:, steps, :].reshape(rows, D_SSM).astype(_F32)
        pa = jnp.dot(ao_ref[:, steps, :].reshape(rows, D_ATTN), wba_ref[...],
                     preferred_element_type=_F32)
        ps = jnp.dot(ssm.astype(_BF16), wbs_ref[...], preferred_element_type=_F32)
        merged = gates[:, :D_MODEL] * pa + gates[:, D_MODEL:] * ps
        xn = x + jnp.dot(merged.astype(_BF16), wout_ref[...], preferred_element_type=_F32)
        o_ref[:, steps, :] = _rms_normalize(xn, fgain_ref[...]).reshape(nb, CHUNK, D_MODEL)


def _merge(x, gain, wg, bg, ao, yg, zs, wglu, bglu, wba, wbs, wout, fgain):
    bsz, s, _ = x.shape
    nt = TOKEN_TILE // bsz
    tok = lambda i: (0, i, 0)
    fixed = lambda i: (0, 0)
    full = lambda a: pl.BlockSpec(a.shape, fixed)
    act = lambda n: pl.BlockSpec((bsz, nt, n), tok)
    return pl.pallas_call(
        _merge_kernel,
        grid=(s // nt,),
        in_specs=[act(D_MODEL), full(gain), full(wg), full(bg), act(D_ATTN),
                  pl.BlockSpec((N_SSM_GROUPS, nt // CHUNK, bsz, CHUNK_W),
                               lambda i: (0, i, 0, 0)),
                  act(D_SSM), full(wglu), full(bglu), full(wba), full(wbs), full(wout),
                  full(fgain)],
        out_specs=act(D_MODEL),
        out_shape=jax.ShapeDtypeStruct((bsz, s, D_MODEL), _F32),
        scratch_shapes=[pltpu.VMEM((D_SSM // LANES, bsz * CHUNK, LANES), _F32)
                        for _ in range(nt // CHUNK)],
        compiler_params=pltpu.CompilerParams(
            dimension_semantics=("arbitrary",), vmem_limit_bytes=VMEM_LIMIT_BYTES),
        name="merge",
    )(x, gain, wg, bg, ao, yg, zs, wglu, bglu, wba, wbs, wout, fgain)


def _layer(x, norm_gain, w_in, b_gate, attn_sink, a_re, a_im, log_dt, b_re, b_im, c_re, c_im,
           d_skip, w_glu, b_glu, w_ba, w_bs, w_out, out_gain, bias):
    bsz, s, _ = x.shape
    n_chunks = s // CHUNK
    gain = norm_gain.reshape(1, D_MODEL).astype(_F32)
    q, k, v, za, ug, zs = _inproj(x, gain, w_in)

    tmat, smat, cmat, ar, ai = _ssm_matrices(a_re, a_im, log_dt, b_re, b_im, c_re, c_im,
                                             d_skip)
    yg = _ssm(ug.reshape(N_SSM_GROUPS, n_chunks * bsz, CHUNK_W), tmat, smat, cmat, ar, ai,
              n_chunks, bsz)
    yg = yg.reshape(N_SSM_GROUPS, n_chunks, bsz, CHUNK_W)

    ao = _attention(LOG2_E * attn_sink.astype(_F32), q, k, v, bias, za)

    w_gate = w_in[:, 2 * D_ATTN + 2 * D_KV + 2 * D_SSM:].astype(_BF16)
    return _merge(x, gain, w_gate, b_gate.reshape(1, -1).astype(_F32), ao, yg, zs,
                  w_glu.astype(_BF16), b_glu.reshape(1, -1).astype(_F32),
                  w_ba.astype(_BF16), w_bs.astype(_BF16), w_out.astype(_BF16),
                  out_gain.reshape(1, D_MODEL).astype(_F32))


def kernel(x, norm_gain, w_in, b_gate, attn_sink, rel_bias_table, ssm_a_re, ssm_a_im,
           ssm_log_dt, ssm_b_re, ssm_b_im, ssm_c_re, ssm_c_im, ssm_d, w_glu, b_glu,
           w_branch_attn, w_branch_ssm, w_out, final_norm_gain):
    depth = norm_gain.shape[0]
    assert depth == 1, "final norm is fused into the single layer's epilogue"
    bias = _attn_bias(rel_bias_table)
    l = 0
    return _layer(x, norm_gain[l], w_in[l], b_gate[l], attn_sink[l], ssm_a_re[l],
                  ssm_a_im[l], ssm_log_dt[l], ssm_b_re[l], ssm_b_im[l], ssm_c_re[l],
                  ssm_c_im[l], ssm_d[l], w_glu[l], b_glu[l], w_branch_attn[l],
                  w_branch_ssm[l], w_out[l], final_norm_gain, bias)
```

```python
import functools
import math

import jax
import jax.numpy as jnp
import numpy as np
from jax import lax
from jax.experimental import pallas as pl
from jax.experimental.pallas import tpu as pltpu

D_MODEL = 1024
ATTN_HEADS = 8
KV_HEADS = 2
Q_PER_KV = ATTN_HEADS // KV_HEADS
HEAD_DIM = 64
D_ATTN = ATTN_HEADS * HEAD_DIM
D_KV = KV_HEADS * HEAD_DIM
WINDOW = 128
BLOCK = 128
NUM_BUCKETS = 32
MAX_DISTANCE = 128
D_SSM = 512
SSM_GROUP = 16
N_SSM_GROUPS = D_SSM // SSM_GROUP
SSM_STATE = 64
EPS = 1e-6
NEG_INF = -1e30

LANES = 128
CHUNK = 16
CHUNK_W = CHUNK * SSM_GROUP
STATE_W = 4 * SSM_STATE
SLOTS_PER_VREG = LANES // SSM_GROUP
SSM_GROUPS_PER_STEP = 4

HEAD_PAIR = LANES // HEAD_DIM
PAIRS_PER_KV = Q_PER_KV // HEAD_PAIR
assert D_KV == LANES and KV_HEADS == HEAD_PAIR
D_KVX = 2 * D_KV
ATTN_Q_BLOCKS = 4
LOG2_E = math.log2(math.e)
Q_SCALE = HEAD_DIM ** -0.5 * LOG2_E

TOKEN_TILE = 1024
VMEM_LIMIT_BYTES = 56 * 1024 * 1024

_F32 = jnp.float32
_BF16 = jnp.bfloat16
_HI = lax.Precision.HIGHEST


def _sigmoid(x):
    return 0.5 * jnp.tanh(0.5 * x) + 0.5


def _silu(x):
    return x * _sigmoid(x)


def _gelu_tanh(x):
    c = math.sqrt(2.0 / math.pi)
    return 0.5 * x * (1.0 + jnp.tanh(c * (x + 0.044715 * (x * x * x))))


def _rms_normalize(x, gain):
    ms = jnp.mean(x * x, axis=-1, keepdims=True)
    return x * lax.rsqrt(ms + EPS) * gain


def _slot_masks(rows):
    lane = lax.broadcasted_iota(jnp.int32, (rows, LANES), 1)
    return [(lane >= p * SSM_GROUP) & (lane < (p + 1) * SSM_GROUP)
            for p in range(SLOTS_PER_VREG)]


def _to_group_major(u_scr, ug_ref, kk, nb):
    masks = _slot_masks(nb)
    for v in range(D_SSM // LANES):
        for half in range(CHUNK // SLOTS_PER_VREG):
            rolled = []
            for t8 in range(SLOTS_PER_VREG):
                r = half * SLOTS_PER_VREG + t8
                piece = u_scr[v, pl.ds(r, nb, stride=CHUNK), :]
                rolled.append(pltpu.roll(piece, t8 * SSM_GROUP, 1) if t8 else piece)
            for p0 in range(SLOTS_PER_VREG):
                acc = rolled[0]
                for t8 in range(1, SLOTS_PER_VREG):
                    acc = jnp.where(masks[(p0 + t8) % SLOTS_PER_VREG], rolled[t8], acc)
                g = v * SLOTS_PER_VREG + p0
                ug_ref[g, kk, :, half * LANES:(half + 1) * LANES] = acc.astype(_BF16)


def _from_group_major(yg_ref, y_scr, kk, nb):
    masks = _slot_masks(nb)
    for v in range(D_SSM // LANES):
        for half in range(CHUNK // SLOTS_PER_VREG):
            src = [yg_ref[v * SLOTS_PER_VREG + p0, kk, :,
                          half * LANES:(half + 1) * LANES].astype(_F32)
                   for p0 in range(SLOTS_PER_VREG)]
            for t8 in range(SLOTS_PER_VREG):
                acc = src[0]
                for p0 in range(1, SLOTS_PER_VREG):
                    acc = jnp.where(masks[(p0 + t8) % SLOTS_PER_VREG], src[p0], acc)
                if t8:
                    acc = pltpu.roll(acc, LANES - t8 * SSM_GROUP, 1)
                r = half * SLOTS_PER_VREG + t8
                y_scr[v, pl.ds(r, nb, stride=CHUNK), :] = acc


def _inproj_kernel(x_ref, gain_ref, w_ssm_ref, w_za_ref, w_qkv_ref, q_ref, k_ref, v_ref,
                   za_ref, ug_ref, zs_ref, *u_scrs):
    nb, nt = x_ref.shape[0], x_ref.shape[1]
    rows = nb * CHUNK
    for kk in range(nt // CHUNK):
        steps = slice(kk * CHUNK, (kk + 1) * CHUNK)
        x = x_ref[:, steps, :].reshape(rows, D_MODEL)
        h = _rms_normalize(x, gain_ref[...]).astype(_BF16)

        def put(ref, val):
            ref[:, steps, :] = val.astype(_BF16).reshape(nb, CHUNK, val.shape[-1])

        u_zs = jnp.dot(h, w_ssm_ref[...], preferred_element_type=_F32)
        for v in range(D_SSM // LANES):
            u_scrs[kk][v] = u_zs[:, v * LANES:(v + 1) * LANES]
        _to_group_major(u_scrs[kk], ug_ref, kk, nb)
        put(zs_ref, _silu(u_zs[:, D_SSM:]))
        put(za_ref, _silu(jnp.dot(h, w_za_ref[...], preferred_element_type=_F32)))
        qkv = jnp.dot(h, w_qkv_ref[...], preferred_element_type=_F32)
        put(q_ref, qkv[:, :D_ATTN] * Q_SCALE)
        with_swap = lambda a: jnp.concatenate([a, pltpu.roll(a, HEAD_DIM, 1)], axis=1)
        put(k_ref, with_swap(qkv[:, D_ATTN:D_ATTN + D_KV]))
        put(v_ref, with_swap(qkv[:, D_ATTN + D_KV:]))


def _inproj(x, gain, w_in):
    bsz, s, _ = x.shape
    nt = TOKEN_TILE // bsz
    tok = lambda i: (0, i, 0)
    fixed = lambda i: (0, 0)
    o_za = D_ATTN + 2 * D_KV
    o_u = o_za + D_ATTN
    weights = [w_in[:, o_u:o_u + 2 * D_SSM].astype(_BF16),
               w_in[:, o_za:o_u].astype(_BF16), w_in[:, :o_za].astype(_BF16)]
    outs = [D_ATTN, D_KVX, D_KVX, D_ATTN, None, D_SSM]
    act = lambda n: (pl.BlockSpec((bsz, nt, n), tok), jax.ShapeDtypeStruct((bsz, s, n), _BF16))
    ug = (pl.BlockSpec((N_SSM_GROUPS, nt // CHUNK, bsz, CHUNK_W), lambda i: (0, i, 0, 0)),
          jax.ShapeDtypeStruct((N_SSM_GROUPS, s // CHUNK, bsz, CHUNK_W), _BF16))
    specs, shapes = zip(*[ug if n is None else act(n) for n in outs])
    return pl.pallas_call(
        _inproj_kernel,
        grid=(s // nt,),
        in_specs=[pl.BlockSpec((bsz, nt, D_MODEL), tok),
                  pl.BlockSpec((1, D_MODEL), fixed)]
                 + [pl.BlockSpec(w.shape, fixed) for w in weights],
        out_specs=list(specs),
        out_shape=list(shapes),
        scratch_shapes=[pltpu.VMEM((D_SSM // LANES, bsz * CHUNK, LANES), _F32)
                        for _ in range(nt // CHUNK)],
        compiler_params=pltpu.CompilerParams(
            dimension_semantics=("arbitrary",), vmem_limit_bytes=VMEM_LIMIT_BYTES),
        name="inproj",
    )(x, gain, *weights)


def _ssm_kernel(n_chunks, rows_per_chunk, u_ref, t_ref, s_ref, c_ref, ar_ref, ai_ref,
                y_ref, s_scr, x_scr):
    rb = rows_per_chunk
    groups = range(u_ref.shape[0])
    for gi in groups:
        s_scr[gi] = jnp.dot(u_ref[gi], s_ref[gi], preferred_element_type=_F32)
    ar = [jnp.broadcast_to(ar_ref[gi], (rb, 2 * SSM_STATE)) for gi in groups]
    ai = [jnp.broadcast_to(ai_ref[gi], (rb, 2 * SSM_STATE)) for gi in groups]
    is_fwd = lax.broadcasted_iota(jnp.int32, (rb, 2 * SSM_STATE), 1) < SSM_STATE
    is_bwd = jnp.logical_not(is_fwd)
    re_cols = pl.ds(0, 2 * SSM_STATE)
    im_cols = pl.ds(2 * SSM_STATE, 2 * SSM_STATE)

    def step(i, carry):
        rows_f = pl.ds(pl.multiple_of(i * rb, rb), rb)
        rows_b = pl.ds(pl.multiple_of((n_chunks - 1 - i) * rb, rb), rb)
        new = []
        for gi in groups:
            st_re, st_im = carry[2 * gi], carry[2 * gi + 1]
            pltpu.store(x_scr.at[gi, rows_f, re_cols], st_re, mask=is_fwd)
            pltpu.store(x_scr.at[gi, rows_f, im_cols], st_im, mask=is_fwd)
            pltpu.store(x_scr.at[gi, rows_b, re_cols], st_re, mask=is_bwd)
            pltpu.store(x_scr.at[gi, rows_b, im_cols], st_im, mask=is_bwd)
            in_re = jnp.where(is_fwd, s_scr[gi, rows_f, re_cols], s_scr[gi, rows_b, re_cols])
            in_im = jnp.where(is_fwd, s_scr[gi, rows_f, im_cols], s_scr[gi, rows_b, im_cols])
            new.append(ar[gi] * st_re - ai[gi] * st_im + in_re)
            new.append(ar[gi] * st_im + ai[gi] * st_re + in_im)
        return tuple(new)

    zero = jnp.zeros((rb, 2 * SSM_STATE), _F32)
    lax.fori_loop(0, n_chunks, step, (zero,) * (2 * len(groups)), unroll=4)

    for gi in groups:
        y = jnp.dot(u_ref[gi], t_ref[gi], preferred_element_type=_F32)
        y = y + lax.dot_general(x_scr[gi].astype(_BF16), c_ref[gi], (((1,), (1,)), ((), ())),
                                preferred_element_type=_F32)
        y_ref[gi] = y.astype(y_ref.dtype)


def _ssm(ug, tmat, smat, cmat, ar, ai, n_chunks, rows_per_chunk):
    g, rows, _ = ug.shape
    gb = SSM_GROUPS_PER_STEP
    per_group = lambda i: (i, 0, 0)
    return pl.pallas_call(
        functools.partial(_ssm_kernel, n_chunks, rows_per_chunk),
        grid=(g // gb,),
        in_specs=[pl.BlockSpec((gb, rows, CHUNK_W), per_group),
                  pl.BlockSpec((gb, CHUNK_W, CHUNK_W), per_group),
                  pl.BlockSpec((gb, CHUNK_W, STATE_W), per_group),
                  pl.BlockSpec((gb, CHUNK_W, STATE_W), per_group),
                  pl.BlockSpec((gb, 1, 2 * SSM_STATE), per_group),
                  pl.BlockSpec((gb, 1, 2 * SSM_STATE), per_group)],
        out_specs=pl.BlockSpec((gb, rows, CHUNK_W), per_group),
        out_shape=jax.ShapeDtypeStruct((g, rows, CHUNK_W), _BF16),
        scratch_shapes=[pltpu.VMEM((gb, rows, STATE_W), _F32),
                        pltpu.VMEM((gb, rows, STATE_W), _F32)],
        compiler_params=pltpu.CompilerParams(
            dimension_semantics=("arbitrary",), vmem_limit_bytes=VMEM_LIMIT_BYTES),
        name="ssm",
    )(ug, tmat, smat, cmat, ar, ai)


def _ssm_matrices(a_re, a_im, log_dt, b_re, b_im, c_re, c_im, d_skip):
    L, G, P, C = CHUNK, N_SSM_GROUPS, SSM_STATE, SSM_GROUP
    both = lambda a: jnp.concatenate([a[0], a[1]], axis=-1)
    dt = jnp.broadcast_to(jnp.exp(log_dt)[..., None], a_re.shape)
    lam = jnp.stack([both(a_re), both(a_im), both(dt)], axis=1)
    bt = jnp.stack([b_re, b_im], axis=0).transpose(2, 0, 4, 1, 3).reshape(G, 2, C, 2 * P)
    ct = jnp.stack([c_re, c_im], axis=0).transpose(2, 0, 3, 1, 4).reshape(G, 2, C, 2 * P)
    dvec = jnp.tile(d_skip.reshape(G, 1, C), (1, 1, L))
    gb = SLOTS_PER_VREG
    per_group = lambda *blk: pl.BlockSpec((gb,) + blk, lambda i: (i,) + (0,) * len(blk))
    mat = jax.ShapeDtypeStruct((G, L * C, L * C), _BF16)
    vec = jax.ShapeDtypeStruct((G, 1, 2 * P), _F32)
    return pl.pallas_call(
        _ssm_prep_kernel,
        grid=(G // gb,),
        in_specs=[per_group(3, 2 * P), per_group(2, C, 2 * P), per_group(2, C, 2 * P),
                  per_group(1, L * C)],
        out_specs=[per_group(L * C, L * C), per_group(L * C, STATE_W),
                   per_group(L * C, STATE_W), per_group(1, 2 * P), per_group(1, 2 * P)],
        out_shape=[mat, mat, mat, vec, vec],
        compiler_params=pltpu.CompilerParams(dimension_semantics=("arbitrary",)),
        name="ssm_prep",
    )(lam, bt, ct, dvec)


def _ssm_prep_kernel(lam_ref, bt_ref, ct_ref, dvec_ref, t_ref, s_ref, c_ref, ar_ref, ai_ref):
    for gi in range(lam_ref.shape[0]):
        _ssm_prep_group(gi, lam_ref, bt_ref, ct_ref, dvec_ref, t_ref, s_ref, c_ref, ar_ref,
                        ai_ref)


def _ssm_prep_group(gi, lam_ref, bt_ref, ct_ref, dvec_ref, t_ref, s_ref, c_ref, ar_ref,
                    ai_ref):
    L, P, C = CHUNK, SSM_STATE, SSM_GROUP
    g8 = gi % SLOTS_PER_VREG

    def time_of_slot(slot):
        return (jnp.bitwise_and(slot, -SLOTS_PER_VREG)
                + jnp.bitwise_and(slot - g8, SLOTS_PER_VREG - 1))

    def cmul(x_re, x_im, y_re, y_im):
        return x_re * y_re - x_im * y_im, x_re * y_im + x_im * y_re

    a_re, a_im, dt = lam_ref[gi, 0:1], lam_ref[gi, 1:2], lam_ref[gi, 2:3]
    lr, li = a_re * dt, a_im * dt

    def cpow(tau):
        mag, ang = jnp.exp(lr * tau), li * tau
        return mag * jnp.cos(ang), mag * jnp.sin(ang)

    ab_re, ab_im = cpow(jnp.ones_like(lr))
    den = a_re * a_re + a_im * a_im
    co_re = ((ab_re - 1.0) * a_re + ab_im * a_im) / den
    co_im = (ab_im * a_re - (ab_re - 1.0) * a_im) / den
    bb_re, bb_im = cmul(bt_ref[gi, 0], bt_ref[gi, 1], co_re, co_im)
    cc_re, cc_im = ct_ref[gi, 0], ct_ref[gi, 1]

    def outer(p, m):
        (p_re, p_im), (m_re, m_im) = p, m
        blocks = [cmul(p_re[s:s + 1], p_im[s:s + 1], m_re, m_im) for s in range(L)]
        return (jnp.concatenate([b[0] for b in blocks], axis=0),
                jnp.concatenate([b[1] for b in blocks], axis=0))

    j = time_of_slot(lax.broadcasted_iota(jnp.int32, (L, 2 * P), 0)).astype(_F32)
    fwd = lax.broadcasted_iota(jnp.int32, (L, 2 * P), 1) < P
    s_re, s_im = outer(cpow(jnp.where(fwd, L - 1 - j, j)), (bb_re, bb_im))
    s_ref[gi] = jnp.concatenate([s_re, s_im], axis=1).astype(_BF16)
    w_re, w_im = outer(cpow(jnp.where(fwd, j + 1, L - j)), (cc_re, cc_im))
    wcat = jnp.concatenate([w_re, -w_im], axis=1)
    c_ref[gi] = wcat.astype(_BF16)
    l_re, l_im = outer(cpow(jnp.where(fwd, -1 - j, j - L)), (bb_re, bb_im))
    lcat = jnp.concatenate([l_re, l_im], axis=1)
    row = lax.broadcasted_iota(jnp.int32, (L * C, L * C), 0)
    col = lax.broadcasted_iota(jnp.int32, (L * C, L * C), 1)
    is_fwd_state = jnp.bitwise_and(col, P) == 0
    contract_state = (((1,), (1,)), ((), ()))
    mf = lax.dot_general(jnp.where(is_fwd_state, lcat, 0.0), wcat, contract_state,
                         precision=_HI, preferred_element_type=_F32)
    mb = lax.dot_general(jnp.where(is_fwd_state, 0.0, lcat), wcat, contract_state,
                         precision=_HI, preferred_element_type=_F32)
    lane_to_slot = C.bit_length() - 1
    jj = time_of_slot(jnp.right_shift(row, lane_to_slot))
    tt = time_of_slot(jnp.right_shift(col, lane_to_slot))
    tm = (jnp.where(tt >= jj, mf, 0.0) + jnp.where(tt <= jj, mb, 0.0)
          + jnp.where(row == col, dvec_ref[gi], 0.0))
    t_ref[gi] = tm.astype(_BF16)
    al_re, al_im = cpow(jnp.full_like(lr, float(L)))
    ar_ref[gi] = al_re
    ai_ref[gi] = al_im


def _attn_kernel(sink_ref, q_ref, kp_ref, kc_ref, kn_ref, vp_ref, vc_ref, vn_ref,
                 bias_first_ref, bias_mid_ref, bias_last_ref, za_ref, o_ref):
    nq = q_ref.shape[1] // BLOCK
    rows = lambda ref, c: ref[0, c * BLOCK:(c + 1) * BLOCK]
    k_blocks = [kp_ref[0]] + [rows(kc_ref, c) for c in range(nq)] + [kn_ref[0]]
    v_blocks = [vp_ref[0]] + [rows(vc_ref, c) for c in range(nq)] + [vn_ref[0]]
    lane = lax.broadcasted_iota(jnp.int32, (BLOCK, LANES), 1)
    halves = [lane < HEAD_DIM, lane >= HEAD_DIM]
    keep = [h.astype(_F32).astype(_BF16) for h in halves]
    nk = 3 * BLOCK
    slab = lambda blk, j, e: blk[:, (j ^ e) * LANES:((j ^ e) + 1) * LANES]
    k_half = {(j, e): [slab(kb, j, e) * keep[e] for kb in k_blocks]
              for j in range(KV_HEADS) for e in range(HEAD_PAIR)}
    v_half = {(j, e): [jnp.concatenate([slab(vb, j, e) * keep[e], keep[e]], axis=1)
                       for vb in v_blocks]
              for j in range(KV_HEADS) for e in range(HEAD_PAIR)}
    for c in range(nq):
        bias_ref = bias_first_ref if c == 0 else bias_last_ref if c == nq - 1 else bias_mid_ref
        q = rows(q_ref, c)
        slabs_out = []
        for j in range(KV_HEADS):
            window = lambda halves_of: jnp.concatenate(
                [halves_of[j, e][c + d] for e in range(HEAD_PAIR) for d in range(3)], axis=0)
            kcat = window(k_half)
            vcat = window(v_half)
            qs = jnp.concatenate(
                [q[:, (j * PAIRS_PER_KV + i) * LANES:(j * PAIRS_PER_KV + i + 1) * LANES]
                 for i in range(PAIRS_PER_KV)], axis=0)
            s = lax.dot_general(qs, kcat, (((1,), (1,)), ((), ())),
                                preferred_element_type=_F32)
            s = s + bias_ref[0, j]
            p_rows, e_rows = [], []
            for i in range(PAIRS_PER_KV):
                p_lanes, e_sink = [], []
                for e in range(HEAD_PAIR):
                    sg = s[i * BLOCK:(i + 1) * BLOCK, e * nk:(e + 1) * nk]
                    sk = sink_ref[j * Q_PER_KV + i * HEAD_PAIR + e]
                    m = jnp.maximum(jnp.max(sg, axis=-1, keepdims=True), sk)
                    p_lanes.append(jnp.exp2(sg - m).astype(_BF16))
                    e_sink.append(jnp.broadcast_to(jnp.exp2(sk - m), (BLOCK, LANES)))
                p_rows.append(jnp.concatenate(p_lanes, axis=1))
                e_rows.append(jnp.where(halves[0], e_sink[0], e_sink[1]))
            p = jnp.concatenate(p_rows, axis=0)
            o = jnp.dot(p, vcat, preferred_element_type=_F32)
            den = o[:, LANES:] + jnp.concatenate(e_rows, axis=0)
            on = o[:, :LANES] / den
            slabs_out += [on[i * BLOCK:(i + 1) * BLOCK] for i in range(PAIRS_PER_KV)]
        o_all = jnp.concatenate(slabs_out, axis=1)
        o_ref[0, c * BLOCK:(c + 1) * BLOCK, :] = (
            o_all * rows(za_ref, c).astype(_F32)).astype(o_ref.dtype)


def _attention(sink, q, k, v, bias, za):
    b, s, _ = q.shape
    nq = ATTN_Q_BLOCKS
    nb, steps = s // BLOCK, s // (nq * BLOCK)
    cur = lambda i, n: (i, n, 0)
    prev = lambda i, n: (i, jnp.maximum(n * nq - 1, 0), 0)
    nxt = lambda i, n: (i, jnp.minimum((n + 1) * nq, nb - 1), 0)
    first = lambda i, n: (jnp.where(n == 0, 0, 1), 0, 0, 0)
    mid = lambda i, n: (1, 0, 0, 0)
    last = lambda i, n: (jnp.where(n == steps - 1, 2, 1), 0, 0, 0)
    halo = lambda f: pl.BlockSpec((1, BLOCK, D_KVX), f)
    main = lambda w: pl.BlockSpec((1, nq * BLOCK, w), cur)
    bias_spec = lambda f: pl.BlockSpec((1,) + bias.shape[1:], f)
    return pl.pallas_call(
        _attn_kernel,
        grid=(b, steps),
        in_specs=[pl.BlockSpec(memory_space=pltpu.SMEM), main(D_ATTN),
                  halo(prev), main(D_KVX), halo(nxt),
                  halo(prev), main(D_KVX), halo(nxt),
                  bias_spec(first), bias_spec(mid), bias_spec(last), main(D_ATTN)],
        out_specs=main(D_ATTN),
        out_shape=jax.ShapeDtypeStruct((b, s, D_ATTN), _BF16),
        compiler_params=pltpu.CompilerParams(
            dimension_semantics=("arbitrary", "arbitrary"),
            vmem_limit_bytes=VMEM_LIMIT_BYTES),
        name="attn",
    )(sink, q, k, k, k, v, v, v, bias, bias, bias, za)


def _t5_bucket_np(rel):
    half = NUM_BUCKETS // 2
    ret = (rel > 0).astype(np.int64) * half
    n = np.abs(rel)
    max_exact = half // 2
    nf = np.maximum(n, 1).astype(np.float64)
    large = max_exact + (np.log(nf / max_exact) / math.log(MAX_DISTANCE / max_exact)
                         * (half - max_exact)).astype(np.int64)
    large = np.minimum(large, half - 1)
    return ret + np.where(n < max_exact, n, large)


def _attn_bias(rel_table):
    rel = (np.arange(3 * BLOCK)[None, :] - BLOCK) - np.arange(BLOCK)[:, None]
    onehot = (_t5_bucket_np(rel)[None] == np.arange(NUM_BUCKETS)[:, None, None])
    bias = jnp.einsum('bh,bqk->hqk', rel_table.astype(_F32),
                      jnp.asarray(onehot, _F32), precision=_HI)
    band = np.abs(rel) <= WINDOW
    slot = np.arange(3 * BLOCK)[None, :] // BLOCK
    valid = np.stack([band & (slot != 0), band, band & (slot != 2)])
    bias = jnp.where(jnp.asarray(valid)[:, None], LOG2_E * bias[None], NEG_INF)
    bias = bias.reshape(3, KV_HEADS, PAIRS_PER_KV, HEAD_PAIR, BLOCK, 3 * BLOCK)
    bias = bias.transpose(0, 1, 2, 4, 3, 5)
    return bias.reshape(3, KV_HEADS, PAIRS_PER_KV * BLOCK, HEAD_PAIR * 3 * BLOCK)


def _merge_kernel(x_ref, gain_ref, wg_ref, bg_ref, ao_ref, yg_ref, zs_ref, wglu_ref,
                  bglu_ref, wba_ref, wbs_ref, wout_ref, fgain_ref, o_ref, *y_scrs):
    nb, nt = x_ref.shape[0], x_ref.shape[1]
    rows = nb * CHUNK
    for kk in range(nt // CHUNK):
        steps = slice(kk * CHUNK, (kk + 1) * CHUNK)
        x = x_ref[:, steps, :].reshape(rows, D_MODEL)
        h = _rms_normalize(x, gain_ref[...]).astype(_BF16)
        gates = _sigmoid(jnp.dot(h, wg_ref[...], preferred_element_type=_F32) + bg_ref[...])
        _from_group_major(yg_ref, y_scrs[kk], kk, nb)
        y = jnp.concatenate([y_scrs[kk][v] for v in range(D_SSM // LANES)], axis=1)
        y = _gelu_tanh(y)
        glu = (jnp.dot(y.astype(_BF16), wglu_ref[...], preferred_element_type=_F32)
               + bglu_ref[...])
        ssm = y * _sigmoid(glu) * zs_ref[:, steps, :].reshape(rows, D_SSM).astype(_F32)
        pa = jnp.dot(ao_ref[:, steps, :].reshape(rows, D_ATTN), wba_ref[...],
                     preferred_element_type=_F32)
        ps = jnp.dot(ssm.astype(_BF16), wbs_ref[...], preferred_element_type=_F32)
        merged = gates[:, :D_MODEL] * pa + gates[:, D_MODEL:] * ps
        xn = x + jnp.dot(merged.astype(_BF16), wout_ref[...], preferred_element_type=_F32)
        o_ref[:, steps, :] = _rms_normalize(xn, fgain_ref[...]).reshape(nb, CHUNK, D_MODEL)


def _merge(x, gain, wg, bg, ao, yg, zs, wglu, bglu, wba, wbs, wout, fgain):
    bsz, s, _ = x.shape
    nt = TOKEN_TILE // bsz
    tok = lambda i: (0, i, 0)
    fixed = lambda i: (0, 0)
    full = lambda a: pl.BlockSpec(a.shape, fixed)
    act = lambda n: pl.BlockSpec((bsz, nt, n), tok)
    return pl.pallas_call(
        _merge_kernel,
        grid=(s // nt,),
        in_specs=[act(D_MODEL), full(gain), full(wg), full(bg), act(D_ATTN),
                  pl.BlockSpec((N_SSM_GROUPS, nt // CHUNK, bsz, CHUNK_W),
                               lambda i: (0, i, 0, 0)),
                  act(D_SSM), full(wglu), full(bglu), full(wba), full(wbs), full(wout),
                  full(fgain)],
        out_specs=act(D_MODEL),
        out_shape=jax.ShapeDtypeStruct((bsz, s, D_MODEL), _F32),
        scratch_shapes=[pltpu.VMEM((D_SSM // LANES, bsz * CHUNK, LANES), _F32)
                        for _ in range(nt // CHUNK)],
        compiler_params=pltpu.CompilerParams(
            dimension_semantics=("arbitrary",), vmem_limit_bytes=VMEM_LIMIT_BYTES),
        name="merge",
    )(x, gain, wg, bg, ao, yg, zs, wglu, bglu, wba, wbs, wout, fgain)


def _layer(x, norm_gain, w_in, b_gate, attn_sink, a_re, a_im, log_dt, b_re, b_im, c_re, c_im,
           d_skip, w_glu, b_glu, w_ba, w_bs, w_out, out_gain, bias):
    bsz, s, _ = x.shape
    n_chunks = s // CHUNK
    gain = norm_gain.reshape(1, D_MODEL).astype(_F32)
    q, k, v, za, ug, zs = _inproj(x, gain, w_in)

    tmat, smat, cmat, ar, ai = _ssm_matrices(a_re, a_im, log_dt, b_re, b_im, c_re, c_im,
                                             d_skip)
    yg = _ssm(ug.reshape(N_SSM_GROUPS, n_chunks * bsz, CHUNK_W), tmat, smat, cmat, ar, ai,
              n_chunks, bsz)
    yg = yg.reshape(N_SSM_GROUPS, n_chunks, bsz, CHUNK_W)

    ao = _attention(LOG2_E * attn_sink.astype(_F32), q, k, v, bias, za)

    w_gate = w_in[:, 2 * D_ATTN + 2 * D_KV + 2 * D_SSM:].astype(_BF16)
    return _merge(x, gain, w_gate, b_gate.reshape(1, -1).astype(_F32), ao, yg, zs,
                  w_glu.astype(_BF16), b_glu.reshape(1, -1).astype(_F32),
                  w_ba.astype(_BF16), w_bs.astype(_BF16), w_out.astype(_BF16),
                  out_gain.reshape(1, D_MODEL).astype(_F32))


def kernel(x, norm_gain, w_in, b_gate, attn_sink, rel_bias_table, ssm_a_re, ssm_a_im,
           ssm_log_dt, ssm_b_re, ssm_b_im, ssm_c_re, ssm_c_im, ssm_d, w_glu, b_glu,
           w_branch_attn, w_branch_ssm, w_out, final_norm_gain):
    depth = norm_gain.shape[0]
    assert depth == 1, "final norm is fused into the single layer's epilogue"
    bias = _attn_bias(rel_bias_table)
    l = 0
    return _layer(x, norm_gain[l], w_in[l], b_gate[l], attn_sink[l], ssm_a_re[l],
                  ssm_a_im[l], ssm_log_dt[l], ssm_b_re[l], ssm_b_im[l], ssm_c_re[l],
                  ssm_c_im[l], ssm_d[l], w_glu[l], b_glu[l], w_branch_attn[l],
                  w_branch_ssm[l], w_out[l], final_norm_gain, bias)
```

```python
import functools
import math

import jax
import jax.numpy as jnp
import numpy as np
from jax import lax
from jax.experimental import pallas as pl
from jax.experimental.pallas import tpu as pltpu

D_MODEL = 1024
ATTN_HEADS = 8
KV_HEADS = 2
Q_PER_KV = ATTN_HEADS // KV_HEADS
HEAD_DIM = 64
D_ATTN = ATTN_HEADS * HEAD_DIM
D_KV = KV_HEADS * HEAD_DIM
WINDOW = 128
BLOCK = 128
NUM_BUCKETS = 32
MAX_DISTANCE = 128
D_SSM = 512
SSM_GROUP = 16
N_SSM_GROUPS = D_SSM // SSM_GROUP
SSM_STATE = 64
EPS = 1e-6
NEG_INF = -1e30

LANES = 128
CHUNK = 16
CHUNK_W = CHUNK * SSM_GROUP
STATE_W = 4 * SSM_STATE
SLOTS_PER_VREG = LANES // SSM_GROUP
SSM_GROUPS_PER_STEP = 4

HEAD_PAIR = LANES // HEAD_DIM
PAIRS_PER_KV = Q_PER_KV // HEAD_PAIR
assert D_KV == LANES and KV_HEADS == HEAD_PAIR
D_KVX = 2 * D_KV
ATTN_Q_BLOCKS = 16
LOG2_E = math.log2(math.e)
Q_SCALE = HEAD_DIM ** -0.5 * LOG2_E

TOKEN_TILE = 1024
INPROJ_TOKEN_TILE = 2048
SUB_STEPS = CHUNK
VMEM_LIMIT_BYTES = 56 * 1024 * 1024

_F32 = jnp.float32
_BF16 = jnp.bfloat16
_HI = lax.Precision.HIGHEST


def _sigmoid(x):
    return 0.5 * jnp.tanh(0.5 * x) + 0.5


def _silu(x):
    return x * _sigmoid(x)


def _gelu_tanh(x):
    c = math.sqrt(2.0 / math.pi)
    return 0.5 * x * (1.0 + jnp.tanh(c * (x + 0.044715 * (x * x * x))))


def _rms_normalize(x, gain):
    ms = jnp.mean(x * x, axis=-1, keepdims=True)
    return x * lax.rsqrt(ms + EPS) * gain


def _slot_masks(rows):
    lane = lax.broadcasted_iota(jnp.int32, (rows, LANES), 1)
    return [(lane >= p * SSM_GROUP) & (lane < (p + 1) * SSM_GROUP)
            for p in range(SLOTS_PER_VREG)]


def _to_group_major(u_scr, ug_ref, kk, nb, t0, steps):
    masks = _slot_masks(nb)
    for v in range(D_SSM // LANES):
        for half in range(CHUNK // SLOTS_PER_VREG):
            rolled = []
            for t8 in range(SLOTS_PER_VREG):
                r = t0 + half * SLOTS_PER_VREG + t8
                piece = u_scr[v, pl.ds(r, nb, stride=steps), :]
                rolled.append(pltpu.roll(piece, t8 * SSM_GROUP, 1) if t8 else piece)
            for p0 in range(SLOTS_PER_VREG):
                acc = rolled[0]
                for t8 in range(1, SLOTS_PER_VREG):
                    acc = jnp.where(masks[(p0 + t8) % SLOTS_PER_VREG], rolled[t8], acc)
                g = v * SLOTS_PER_VREG + p0
                ug_ref[g, kk, :, half * LANES:(half + 1) * LANES] = acc.astype(_BF16)


def _from_group_major(yg_ref, y_scr, kk, nb, t0, steps):
    masks = _slot_masks(nb)
    for v in range(D_SSM // LANES):
        for half in range(CHUNK // SLOTS_PER_VREG):
            src = [yg_ref[v * SLOTS_PER_VREG + p0, kk, :,
                          half * LANES:(half + 1) * LANES].astype(_F32)
                   for p0 in range(SLOTS_PER_VREG)]
            for t8 in range(SLOTS_PER_VREG):
                acc = src[0]
                for p0 in range(1, SLOTS_PER_VREG):
                    acc = jnp.where(masks[(p0 + t8) % SLOTS_PER_VREG], src[p0], acc)
                if t8:
                    acc = pltpu.roll(acc, LANES - t8 * SSM_GROUP, 1)
                r = t0 + half * SLOTS_PER_VREG + t8
                y_scr[v, pl.ds(r, nb, stride=steps), :] = acc


def _inproj_kernel(x_ref, gain_ref, w_ssm_ref, w_za_ref, w_qkv_ref, q_ref, k_ref, v_ref,
                   za_ref, ug_ref, zs_ref, *u_scrs):
    nb, nt = x_ref.shape[0], x_ref.shape[1]
    rows = nb * SUB_STEPS
    for st in range(nt // SUB_STEPS):
        steps = slice(st * SUB_STEPS, (st + 1) * SUB_STEPS)
        x = x_ref[:, steps, :].reshape(rows, D_MODEL)
        h = _rms_normalize(x, gain_ref[...]).astype(_BF16)

        def put(ref, val):
            ref[:, steps, :] = val.astype(_BF16).reshape(nb, SUB_STEPS, val.shape[-1])

        u_zs = jnp.dot(h, w_ssm_ref[...], preferred_element_type=_F32)
        for v in range(D_SSM // LANES):
            u_scrs[st][v] = u_zs[:, v * LANES:(v + 1) * LANES]
        for c in range(SUB_STEPS // CHUNK):
            _to_group_major(u_scrs[st], ug_ref, st * (SUB_STEPS // CHUNK) + c, nb, c * CHUNK,
                            SUB_STEPS)
        put(zs_ref, _silu(u_zs[:, D_SSM:]))
        put(za_ref, _silu(jnp.dot(h, w_za_ref[...], preferred_element_type=_F32)))
        qkv = jnp.dot(h, w_qkv_ref[...], preferred_element_type=_F32)
        put(q_ref, qkv[:, :D_ATTN] * Q_SCALE)
        with_swap = lambda a: jnp.concatenate([a, pltpu.roll(a, HEAD_DIM, 1)], axis=1)
        put(k_ref, with_swap(qkv[:, D_ATTN:D_ATTN + D_KV]))
        put(v_ref, with_swap(qkv[:, D_ATTN + D_KV:]))


def _inproj(x, gain, w_in):
    bsz, s, _ = x.shape
    nt = INPROJ_TOKEN_TILE // bsz
    tok = lambda i: (0, i, 0)
    fixed = lambda i: (0, 0)
    o_za = D_ATTN + 2 * D_KV
    o_u = o_za + D_ATTN
    weights = [w_in[:, o_u:o_u + 2 * D_SSM].astype(_BF16),
               w_in[:, o_za:o_u].astype(_BF16), w_in[:, :o_za].astype(_BF16)]
    outs = [D_ATTN, D_KVX, D_KVX, D_ATTN, None, D_SSM]
    act = lambda n: (pl.BlockSpec((bsz, nt, n), tok), jax.ShapeDtypeStruct((bsz, s, n), _BF16))
    ug = (pl.BlockSpec((N_SSM_GROUPS, nt // CHUNK, bsz, CHUNK_W), lambda i: (0, i, 0, 0)),
          jax.ShapeDtypeStruct((N_SSM_GROUPS, s // CHUNK, bsz, CHUNK_W), _BF16))
    specs, shapes = zip(*[ug if n is None else act(n) for n in outs])
    return pl.pallas_call(
        _inproj_kernel,
        grid=(s // nt,),
        in_specs=[pl.BlockSpec((bsz, nt, D_MODEL), tok),
                  pl.BlockSpec((1, D_MODEL), fixed)]
                 + [pl.BlockSpec(w.shape, fixed) for w in weights],
        out_specs=list(specs),
        out_shape=list(shapes),
        scratch_shapes=[pltpu.VMEM((D_SSM // LANES, bsz * SUB_STEPS, LANES), _F32)
                        for _ in range(nt // SUB_STEPS)],
        compiler_params=pltpu.CompilerParams(
            dimension_semantics=("arbitrary",), vmem_limit_bytes=VMEM_LIMIT_BYTES),
        name="inproj",
    )(x, gain, *weights)


def _ssm_kernel(n_chunks, rows_per_chunk, u_ref, t_ref, s_ref, c_ref, ar_ref, ai_ref,
                y_ref, s_scr, x_scr):
    rb = rows_per_chunk
    groups = range(u_ref.shape[0])
    for gi in groups:
        s_scr[gi] = jnp.dot(u_ref[gi], s_ref[gi], preferred_element_type=_F32)
    ar = [jnp.broadcast_to(ar_ref[gi], (rb, 2 * SSM_STATE)) for gi in groups]
    ai = [jnp.broadcast_to(ai_ref[gi], (rb, 2 * SSM_STATE)) for gi in groups]
    is_fwd = lax.broadcasted_iota(jnp.int32, (rb, 2 * SSM_STATE), 1) < SSM_STATE
    is_bwd = jnp.logical_not(is_fwd)
    re_cols = pl.ds(0, 2 * SSM_STATE)
    im_cols = pl.ds(2 * SSM_STATE, 2 * SSM_STATE)

    def step(i, carry):
        rows_f = pl.ds(pl.multiple_of(i * rb, rb), rb)
        rows_b = pl.ds(pl.multiple_of((n_chunks - 1 - i) * rb, rb), rb)
        new = []
        for gi in groups:
            st_re, st_im = carry[2 * gi], carry[2 * gi + 1]
            pltpu.store(x_scr.at[gi, rows_f, re_cols], st_re, mask=is_fwd)
            pltpu.store(x_scr.at[gi, rows_f, im_cols], st_im, mask=is_fwd)
            pltpu.store(x_scr.at[gi, rows_b, re_cols], st_re, mask=is_bwd)
            pltpu.store(x_scr.at[gi, rows_b, im_cols], st_im, mask=is_bwd)
            in_re = jnp.where(is_fwd, s_scr[gi, rows_f, re_cols], s_scr[gi, rows_b, re_cols])
            in_im = jnp.where(is_fwd, s_scr[gi, rows_f, im_cols], s_scr[gi, rows_b, im_cols])
            new.append(ar[gi] * st_re - ai[gi] * st_im + in_re)
            new.append(ar[gi] * st_im + ai[gi] * st_re + in_im)
        return tuple(new)

    zero = jnp.zeros((rb, 2 * SSM_STATE), _F32)
    lax.fori_loop(0, n_chunks, step, (zero,) * (2 * len(groups)), unroll=4)

    for gi in groups:
        y = jnp.dot(u_ref[gi], t_ref[gi], preferred_element_type=_F32)
        y = y + lax.dot_general(x_scr[gi].astype(_BF16), c_ref[gi], (((1,), (1,)), ((), ())),
                                preferred_element_type=_F32)
        y_ref[gi] = y.astype(y_ref.dtype)


def _ssm(ug, tmat, smat, cmat, ar, ai, n_chunks, rows_per_chunk):
    g, rows, _ = ug.shape
    gb = SSM_GROUPS_PER_STEP
    per_group = lambda i: (i, 0, 0)
    return pl.pallas_call(
        functools.partial(_ssm_kernel, n_chunks, rows_per_chunk),
        grid=(g // gb,),
        in_specs=[pl.BlockSpec((gb, rows, CHUNK_W), per_group),
                  pl.BlockSpec((gb, CHUNK_W, CHUNK_W), per_group),
                  pl.BlockSpec((gb, CHUNK_W, STATE_W), per_group),
                  pl.BlockSpec((gb, CHUNK_W, STATE_W), per_group),
                  pl.BlockSpec((gb, 1, 2 * SSM_STATE), per_group),
                  pl.BlockSpec((gb, 1, 2 * SSM_STATE), per_group)],
        out_specs=pl.BlockSpec((gb, rows, CHUNK_W), per_group),
        out_shape=jax.ShapeDtypeStruct((g, rows, CHUNK_W), _BF16),
        scratch_shapes=[pltpu.VMEM((gb, rows, STATE_W), _F32),
                        pltpu.VMEM((gb, rows, STATE_W), _F32)],
        compiler_params=pltpu.CompilerParams(
            dimension_semantics=("arbitrary",), vmem_limit_bytes=VMEM_LIMIT_BYTES),
        name="ssm",
    )(ug, tmat, smat, cmat, ar, ai)


def _ssm_matrices(a_re, a_im, log_dt, b_re, b_im, c_re, c_im, d_skip):
    L, G, P, C = CHUNK, N_SSM_GROUPS, SSM_STATE, SSM_GROUP
    both = lambda a: jnp.concatenate([a[0], a[1]], axis=-1)
    dt = jnp.broadcast_to(jnp.exp(log_dt)[..., None], a_re.shape)
    lam = jnp.stack([both(a_re), both(a_im), both(dt)], axis=1)
    bt = jnp.stack([b_re, b_im], axis=0).transpose(2, 0, 4, 1, 3).reshape(G, 2, C, 2 * P)
    ct = jnp.stack([c_re, c_im], axis=0).transpose(2, 0, 3, 1, 4).reshape(G, 2, C, 2 * P)
    dvec = jnp.tile(d_skip.reshape(G, 1, C), (1, 1, L))
    gb = SLOTS_PER_VREG
    per_group = lambda *blk: pl.BlockSpec((gb,) + blk, lambda i: (i,) + (0,) * len(blk))
    mat = jax.ShapeDtypeStruct((G, L * C, L * C), _BF16)
    vec = jax.ShapeDtypeStruct((G, 1, 2 * P), _F32)
    return pl.pallas_call(
        _ssm_prep_kernel,
        grid=(G // gb,),
        in_specs=[per_group(3, 2 * P), per_group(2, C, 2 * P), per_group(2, C, 2 * P),
                  per_group(1, L * C)],
        out_specs=[per_group(L * C, L * C), per_group(L * C, STATE_W),
                   per_group(L * C, STATE_W), per_group(1, 2 * P), per_group(1, 2 * P)],
        out_shape=[mat, mat, mat, vec, vec],
        compiler_params=pltpu.CompilerParams(dimension_semantics=("arbitrary",)),
        name="ssm_prep",
    )(lam, bt, ct, dvec)


def _ssm_prep_kernel(lam_ref, bt_ref, ct_ref, dvec_ref, t_ref, s_ref, c_ref, ar_ref, ai_ref):
    for gi in range(lam_ref.shape[0]):
        _ssm_prep_group(gi, lam_ref, bt_ref, ct_ref, dvec_ref, t_ref, s_ref, c_ref, ar_ref,
                        ai_ref)


def _ssm_prep_group(gi, lam_ref, bt_ref, ct_ref, dvec_ref, t_ref, s_ref, c_ref, ar_ref,
                    ai_ref):
    L, P, C = CHUNK, SSM_STATE, SSM_GROUP
    g8 = gi % SLOTS_PER_VREG

    def time_of_slot(slot):
        return (jnp.bitwise_and(slot, -SLOTS_PER_VREG)
                + jnp.bitwise_and(slot - g8, SLOTS_PER_VREG - 1))

    def cmul(x_re, x_im, y_re, y_im):
        return x_re * y_re - x_im * y_im, x_re * y_im + x_im * y_re

    a_re, a_im, dt = lam_ref[gi, 0:1], lam_ref[gi, 1:2], lam_ref[gi, 2:3]
    lr, li = a_re * dt, a_im * dt

    def cpow(tau):
        mag, ang = jnp.exp(lr * tau), li * tau
        return mag * jnp.cos(ang), mag * jnp.sin(ang)

    ab_re, ab_im = cpow(jnp.ones_like(lr))
    den = a_re * a_re + a_im * a_im
    co_re = ((ab_re - 1.0) * a_re + ab_im * a_im) / den
    co_im = (ab_im * a_re - (ab_re - 1.0) * a_im) / den
    bb_re, bb_im = cmul(bt_ref[gi, 0], bt_ref[gi, 1], co_re, co_im)
    cc_re, cc_im = ct_ref[gi, 0], ct_ref[gi, 1]

    def outer(p, m):
        (p_re, p_im), (m_re, m_im) = p, m
        blocks = [cmul(p_re[s:s + 1], p_im[s:s + 1], m_re, m_im) for s in range(L)]
        return (jnp.concatenate([b[0] for b in blocks], axis=0),
                jnp.concatenate([b[1] for b in blocks], axis=0))

    j = time_of_slot(lax.broadcasted_iota(jnp.int32, (L, 2 * P), 0)).astype(_F32)
    fwd = lax.broadcasted_iota(jnp.int32, (L, 2 * P), 1) < P
    s_re, s_im = outer(cpow(jnp.where(fwd, L - 1 - j, j)), (bb_re, bb_im))
    s_ref[gi] = jnp.concatenate([s_re, s_im], axis=1).astype(_BF16)
    w_re, w_im = outer(cpow(jnp.where(fwd, j + 1, L - j)), (cc_re, cc_im))
    wcat = jnp.concatenate([w_re, -w_im], axis=1)
    c_ref[gi] = wcat.astype(_BF16)
    l_re, l_im = outer(cpow(jnp.where(fwd, -1 - j, j - L)), (bb_re, bb_im))
    lcat = jnp.concatenate([l_re, l_im], axis=1)
    row = lax.broadcasted_iota(jnp.int32, (L * C, L * C), 0)
    col = lax.broadcasted_iota(jnp.int32, (L * C, L * C), 1)
    is_fwd_state = jnp.bitwise_and(col, P) == 0
    contract_state = (((1,), (1,)), ((), ()))
    mf = lax.dot_general(jnp.where(is_fwd_state, lcat, 0.0), wcat, contract_state,
                         precision=_HI, preferred_element_type=_F32)
    mb = lax.dot_general(jnp.where(is_fwd_state, 0.0, lcat), wcat, contract_state,
                         precision=_HI, preferred_element_type=_F32)
    lane_to_slot = C.bit_length() - 1
    jj = time_of_slot(jnp.right_shift(row, lane_to_slot))
    tt = time_of_slot(jnp.right_shift(col, lane_to_slot))
    tm = (jnp.where(tt >= jj, mf, 0.0) + jnp.where(tt <= jj, mb, 0.0)
          + jnp.where(row == col, dvec_ref[gi], 0.0))
    t_ref[gi] = tm.astype(_BF16)
    al_re, al_im = cpow(jnp.full_like(lr, float(L)))
    ar_ref[gi] = al_re
    ai_ref[gi] = al_im


def _attn_kernel(sink_ref, q_ref, kp_ref, kc_ref, kn_ref, vp_ref, vc_ref, vn_ref,
                 bias_first_ref, bias_mid_ref, bias_last_ref, za_ref, o_ref):
    nq = q_ref.shape[1] // BLOCK
    rows = lambda ref, c: ref[0, c * BLOCK:(c + 1) * BLOCK]
    k_blocks = [kp_ref[0]] + [rows(kc_ref, c) for c in range(nq)] + [kn_ref[0]]
    v_blocks = [vp_ref[0]] + [rows(vc_ref, c) for c in range(nq)] + [vn_ref[0]]
    lane = lax.broadcasted_iota(jnp.int32, (BLOCK, LANES), 1)
    halves = [lane < HEAD_DIM, lane >= HEAD_DIM]
    keep = [h.astype(_F32).astype(_BF16) for h in halves]
    nk = 3 * BLOCK
    slab = lambda blk, j, e: blk[:, (j ^ e) * LANES:((j ^ e) + 1) * LANES]
    k_half = {(j, e): [slab(kb, j, e) * keep[e] for kb in k_blocks]
              for j in range(KV_HEADS) for e in range(HEAD_PAIR)}
    v_half = {(j, e): [jnp.concatenate([slab(vb, j, e) * keep[e], keep[e]], axis=1)
                       for vb in v_blocks]
              for j in range(KV_HEADS) for e in range(HEAD_PAIR)}
    for c in range(nq):
        bias_ref = bias_first_ref if c == 0 else bias_last_ref if c == nq - 1 else bias_mid_ref
        q = rows(q_ref, c)
        slabs_out = []
        for j in range(KV_HEADS):
            window = lambda halves_of: jnp.concatenate(
                [halves_of[j, e][c + d] for e in range(HEAD_PAIR) for d in range(3)], axis=0)
            kcat = window(k_half)
            vcat = window(v_half)
            qs = jnp.concatenate(
                [q[:, (j * PAIRS_PER_KV + i) * LANES:(j * PAIRS_PER_KV + i + 1) * LANES]
                 for i in range(PAIRS_PER_KV)], axis=0)
            s = lax.dot_general(qs, kcat, (((1,), (1,)), ((), ())),
                                preferred_element_type=_F32)
            s = s + bias_ref[0, j]
            p_rows, e_rows = [], []
            for i in range(PAIRS_PER_KV):
                p_lanes, e_sink = [], []
                for e in range(HEAD_PAIR):
                    sg = s[i * BLOCK:(i + 1) * BLOCK, e * nk:(e + 1) * nk]
                    sk = sink_ref[j * Q_PER_KV + i * HEAD_PAIR + e]
                    m = jnp.maximum(jnp.max(sg, axis=-1, keepdims=True), sk)
                    p_lanes.append(jnp.exp2(sg - m).astype(_BF16))
                    e_sink.append(jnp.broadcast_to(jnp.exp2(sk - m), (BLOCK, LANES)))
                p_rows.append(jnp.concatenate(p_lanes, axis=1))
                e_rows.append(jnp.where(halves[0], e_sink[0], e_sink[1]))
            p = jnp.concatenate(p_rows, axis=0)
            o = jnp.dot(p, vcat, preferred_element_type=_F32)
            den = o[:, LANES:] + jnp.concatenate(e_rows, axis=0)
            on = o[:, :LANES] / den
            slabs_out += [on[i * BLOCK:(i + 1) * BLOCK] for i in range(PAIRS_PER_KV)]
        o_all = jnp.concatenate(slabs_out, axis=1)
        o_ref[0, c * BLOCK:(c + 1) * BLOCK, :] = (
            o_all * rows(za_ref, c).astype(_F32)).astype(o_ref.dtype)


def _attention(sink, q, k, v, bias, za):
    b, s, _ = q.shape
    nq = ATTN_Q_BLOCKS
    nb, steps = s // BLOCK, s // (nq * BLOCK)
    cur = lambda i, n: (i, n, 0)
    prev = lambda i, n: (i, jnp.maximum(n * nq - 1, 0), 0)
    nxt = lambda i, n: (i, jnp.minimum((n + 1) * nq, nb - 1), 0)
    first = lambda i, n: (jnp.where(n == 0, 0, 1), 0, 0, 0)
    mid = lambda i, n: (1, 0, 0, 0)
    last = lambda i, n: (jnp.where(n == steps - 1, 2, 1), 0, 0, 0)
    halo = lambda f: pl.BlockSpec((1, BLOCK, D_KVX), f)
    main = lambda w: pl.BlockSpec((1, nq * BLOCK, w), cur)
    bias_spec = lambda f: pl.BlockSpec((1,) + bias.shape[1:], f)
    return pl.pallas_call(
        _attn_kernel,
        grid=(b, steps),
        in_specs=[pl.BlockSpec(memory_space=pltpu.SMEM), main(D_ATTN),
                  halo(prev), main(D_KVX), halo(nxt),
                  halo(prev), main(D_KVX), halo(nxt),
                  bias_spec(first), bias_spec(mid), bias_spec(last), main(D_ATTN)],
        out_specs=main(D_ATTN),
        out_shape=jax.ShapeDtypeStruct((b, s, D_ATTN), _BF16),
        compiler_params=pltpu.CompilerParams(
            dimension_semantics=("arbitrary", "arbitrary"),
            vmem_limit_bytes=VMEM_LIMIT_BYTES),
        name="attn",
    )(sink, q, k, k, k, v, v, v, bias, bias, bias, za)


def _t5_bucket_np(rel):
    half = NUM_BUCKETS // 2
    ret = (rel > 0).astype(np.int64) * half
    n = np.abs(rel)
    max_exact = half // 2
    nf = np.maximum(n, 1).astype(np.float64)
    large = max_exact + (np.log(nf / max_exact) / math.log(MAX_DISTANCE / max_exact)
                         * (half - max_exact)).astype(np.int64)
    large = np.minimum(large, half - 1)
    return ret + np.where(n < max_exact, n, large)


def _attn_bias(rel_table):
    rel = (np.arange(3 * BLOCK)[None, :] - BLOCK) - np.arange(BLOCK)[:, None]
    onehot = (_t5_bucket_np(rel)[None] == np.arange(NUM_BUCKETS)[:, None, None])
    bias = jnp.einsum('bh,bqk->hqk', rel_table.astype(_F32),
                      jnp.asarray(onehot, _F32), precision=_HI)
    band = np.abs(rel) <= WINDOW
    slot = np.arange(3 * BLOCK)[None, :] // BLOCK
    valid = np.stack([band & (slot != 0), band, band & (slot != 2)])
    bias = jnp.where(jnp.asarray(valid)[:, None], LOG2_E * bias[None], NEG_INF)
    bias = bias.reshape(3, KV_HEADS, PAIRS_PER_KV, HEAD_PAIR, BLOCK, 3 * BLOCK)
    bias = bias.transpose(0, 1, 2, 4, 3, 5)
    return bias.reshape(3, KV_HEADS, PAIRS_PER_KV * BLOCK, HEAD_PAIR * 3 * BLOCK)


def _merge_kernel(x_ref, gain_ref, wg_ref, bg_ref, ao_ref, yg_ref, zs_ref, wglu_ref,
                  bglu_ref, wba_ref, wbs_ref, wout_ref, fgain_ref, o_ref, *y_scrs):
    nb, nt = x_ref.shape[0], x_ref.shape[1]
    rows = nb * SUB_STEPS
    for st in range(nt // SUB_STEPS):
        steps = slice(st * SUB_STEPS, (st + 1) * SUB_STEPS)
        x = x_ref[:, steps, :].reshape(rows, D_MODEL)
        h = _rms_normalize(x, gain_ref[...]).astype(_BF16)
        gates = _sigmoid(jnp.dot(h, wg_ref[...], preferred_element_type=_F32) + bg_ref[...])
        for c in range(SUB_STEPS // CHUNK):
            _from_group_major(yg_ref, y_scrs[st], st * (SUB_STEPS // CHUNK) + c, nb, c * CHUNK,
                              SUB_STEPS)
        y = jnp.concatenate([y_scrs[st][v] for v in range(D_SSM // LANES)], axis=1)
        y = _gelu_tanh(y)
        glu = (jnp.dot(y.astype(_BF16), wglu_ref[...], preferred_element_type=_F32)
               + bglu_ref[...])
        ssm = y * _sigmoid(glu) * zs_ref[:, steps, :].reshape(rows, D_SSM).astype(_F32)
        pa = jnp.dot(ao_ref[:, steps, :].reshape(rows, D_ATTN), wba_ref[...],
                     preferred_element_type=_F32)
        ps = jnp.dot(ssm.astype(_BF16), wbs_ref[...], preferred_element_type=_F32)
        merged = gates[:, :D_MODEL] * pa + gates[:, D_MODEL:] * ps
        xn = x + jnp.dot(merged.astype(_BF16), wout_ref[...], preferred_element_type=_F32)
        o_ref[:, steps, :] = _rms_normalize(xn, fgain_ref[...]).reshape(nb, SUB_STEPS, D_MODEL)


def _merge(x, gain, wg, bg, ao, yg, zs, wglu, bglu, wba, wbs, wout, fgain):
    bsz, s, _ = x.shape
    nt = TOKEN_TILE // bsz
    tok = lambda i: (0, i, 0)
    fixed = lambda i: (0, 0)
    full = lambda a: pl.BlockSpec(a.shape, fixed)
    act = lambda n: pl.BlockSpec((bsz, nt, n), tok)
    return pl.pallas_call(
        _merge_kernel,
        grid=(s // nt,),
        in_specs=[act(D_MODEL), full(gain), full(wg), full(bg), act(D_ATTN),
                  pl.BlockSpec((N_SSM_GROUPS, nt // CHUNK, bsz, CHUNK_W),
                               lambda i: (0, i, 0, 0)),
                  act(D_SSM), full(wglu), full(bglu), full(wba), full(wbs), full(wout),
                  full(fgain)],
        out_specs=act(D_MODEL),
        out_shape=jax.ShapeDtypeStruct((bsz, s, D_MODEL), _F32),
        scratch_shapes=[pltpu.VMEM((D_SSM // LANES, bsz * SUB_STEPS, LANES), _F32)
                        for _ in range(nt // SUB_STEPS)],
        compiler_params=pltpu.CompilerParams(
            dimension_semantics=("arbitrary",), vmem_limit_bytes=VMEM_LIMIT_BYTES),
        name="merge",
    )(x, gain, wg, bg, ao, yg, zs, wglu, bglu, wba, wbs, wout, fgain)


def _layer(x, norm_gain, w_in, b_gate, attn_sink, a_re, a_im, log_dt, b_re, b_im, c_re, c_im,
           d_skip, w_glu, b_glu, w_ba, w_bs, w_out, out_gain, bias):
    bsz, s, _ = x.shape
    n_chunks = s // CHUNK
    gain = norm_gain.reshape(1, D_MODEL).astype(_F32)
    q, k, v, za, ug, zs = _inproj(x, gain, w_in)

    tmat, smat, cmat, ar, ai = _ssm_matrices(a_re, a_im, log_dt, b_re, b_im, c_re, c_im,
                                             d_skip)
    yg = _ssm(ug.reshape(N_SSM_GROUPS, n_chunks * bsz, CHUNK_W), tmat, smat, cmat, ar, ai,
              n_chunks, bsz)
    yg = yg.reshape(N_SSM_GROUPS, n_chunks, bsz, CHUNK_W)

    ao = _attention(LOG2_E * attn_sink.astype(_F32), q, k, v, bias, za)

    w_gate = w_in[:, 2 * D_ATTN + 2 * D_KV + 2 * D_SSM:].astype(_BF16)
    return _merge(x, gain, w_gate, b_gate.reshape(1, -1).astype(_F32), ao, yg, zs,
                  w_glu.astype(_BF16), b_glu.reshape(1, -1).astype(_F32),
                  w_ba.astype(_BF16), w_bs.astype(_BF16), w_out.astype(_BF16),
                  out_gain.reshape(1, D_MODEL).astype(_F32))


def kernel(x, norm_gain, w_in, b_gate, attn_sink, rel_bias_table, ssm_a_re, ssm_a_im,
           ssm_log_dt, ssm_b_re, ssm_b_im, ssm_c_re, ssm_c_im, ssm_d, w_glu, b_glu,
           w_branch_attn, w_branch_ssm, w_out, final_norm_gain):
    depth = norm_gain.shape[0]
    assert depth == 1, "final norm is fused into the single layer's epilogue"
    bias = _attn_bias(rel_bias_table)
    l = 0
    return _layer(x, norm_gain[l], w_in[l], b_gate[l], attn_sink[l], ssm_a_re[l],
                  ssm_a_im[l], ssm_log_dt[l], ssm_b_re[l], ssm_b_im[l], ssm_c_re[l],
                  ssm_c_im[l], ssm_d[l], w_glu[l], b_glu[l], w_branch_attn[l],
                  w_branch_ssm[l], w_out[l], final_norm_gain, bias)
```

```python
import functools
import math

import jax
import jax.numpy as jnp
import numpy as np
from jax import lax
from jax.experimental import pallas as pl
from jax.experimental.pallas import tpu as pltpu

D_MODEL = 1024
ATTN_HEADS = 8
KV_HEADS = 2
Q_PER_KV = ATTN_HEADS // KV_HEADS
HEAD_DIM = 64
D_ATTN = ATTN_HEADS * HEAD_DIM
D_KV = KV_HEADS * HEAD_DIM
WINDOW = 128
BLOCK = 128
NUM_BUCKETS = 32
MAX_DISTANCE = 128
D_SSM = 512
SSM_GROUP = 16
N_SSM_GROUPS = D_SSM // SSM_GROUP
SSM_STATE = 64
EPS = 1e-6
NEG_INF = -1e30

LANES = 128
CHUNK = 16
CHUNK_W = CHUNK * SSM_GROUP
STATE_W = 4 * SSM_STATE
SLOTS_PER_VREG = LANES // SSM_GROUP
SSM_GROUPS_PER_STEP = 4

HEAD_PAIR = LANES // HEAD_DIM
PAIRS_PER_KV = Q_PER_KV // HEAD_PAIR
assert D_KV == LANES and KV_HEADS == HEAD_PAIR
D_KVX = 2 * D_KV
LOG2_E = math.log2(math.e)
Q_SCALE = HEAD_DIM ** -0.5 * LOG2_E

TOKEN_TILE = 1024
INPROJ_TOKEN_TILE = 2048
SUB_STEPS = CHUNK
VMEM_LIMIT_BYTES = 56 * 1024 * 1024

_F32 = jnp.float32
_BF16 = jnp.bfloat16
_HI = lax.Precision.HIGHEST


def _sigmoid(x):
    return 0.5 * jnp.tanh(0.5 * x) + 0.5


def _silu(x):
    return x * _sigmoid(x)


def _gelu_tanh(x):
    c = math.sqrt(2.0 / math.pi)
    return 0.5 * x * (1.0 + jnp.tanh(c * (x + 0.044715 * (x * x * x))))


def _rms_normalize(x, gain):
    ms = jnp.mean(x * x, axis=-1, keepdims=True)
    return x * lax.rsqrt(ms + EPS) * gain


def _slot_masks(rows):
    lane = lax.broadcasted_iota(jnp.int32, (rows, LANES), 1)
    return [(lane >= p * SSM_GROUP) & (lane < (p + 1) * SSM_GROUP)
            for p in range(SLOTS_PER_VREG)]


def _to_group_major(u_scr, ug_ref, kk, nb, t0, steps):
    masks = _slot_masks(nb)
    for v in range(D_SSM // LANES):
        for half in range(CHUNK // SLOTS_PER_VREG):
            rolled = []
            for t8 in range(SLOTS_PER_VREG):
                r = t0 + half * SLOTS_PER_VREG + t8
                piece = u_scr[v, pl.ds(r, nb, stride=steps), :]
                rolled.append(pltpu.roll(piece, t8 * SSM_GROUP, 1) if t8 else piece)
            for p0 in range(SLOTS_PER_VREG):
                acc = rolled[0]
                for t8 in range(1, SLOTS_PER_VREG):
                    acc = jnp.where(masks[(p0 + t8) % SLOTS_PER_VREG], rolled[t8], acc)
                g = v * SLOTS_PER_VREG + p0
                ug_ref[g, kk, :, half * LANES:(half + 1) * LANES] = acc.astype(_BF16)


def _from_group_major(yg_ref, y_scr, kk, nb, t0, steps):
    masks = _slot_masks(nb)
    for v in range(D_SSM // LANES):
        for half in range(CHUNK // SLOTS_PER_VREG):
            src = [yg_ref[v * SLOTS_PER_VREG + p0, kk, :,
                          half * LANES:(half + 1) * LANES].astype(_F32)
                   for p0 in range(SLOTS_PER_VREG)]
            for t8 in range(SLOTS_PER_VREG):
                acc = src[0]
                for p0 in range(1, SLOTS_PER_VREG):
                    acc = jnp.where(masks[(p0 + t8) % SLOTS_PER_VREG], src[p0], acc)
                if t8:
                    acc = pltpu.roll(acc, LANES - t8 * SSM_GROUP, 1)
                r = t0 + half * SLOTS_PER_VREG + t8
                y_scr[v, pl.ds(r, nb, stride=steps), :] = acc


def _inproj_kernel(x_ref, gain_ref, w_ssm_ref, w_za_ref, w_qkv_ref, q_ref, k_ref, v_ref,
                   za_ref, ug_ref, zs_ref, *u_scrs):
    nb, nt = x_ref.shape[0], x_ref.shape[1]
    rows = nb * SUB_STEPS
    for st in range(nt // SUB_STEPS):
        steps = slice(st * SUB_STEPS, (st + 1) * SUB_STEPS)
        x = x_ref[:, steps, :].reshape(rows, D_MODEL)
        h = _rms_normalize(x, gain_ref[...]).astype(_BF16)

        def put(ref, val):
            ref[:, steps, :] = val.astype(_BF16).reshape(nb, SUB_STEPS, val.shape[-1])

        u_zs = jnp.dot(h, w_ssm_ref[...], preferred_element_type=_F32)
        for v in range(D_SSM // LANES):
            u_scrs[st][v] = u_zs[:, v * LANES:(v + 1) * LANES]
        for c in range(SUB_STEPS // CHUNK):
            _to_group_major(u_scrs[st], ug_ref, st * (SUB_STEPS // CHUNK) + c, nb, c * CHUNK,
                            SUB_STEPS)
        put(zs_ref, _silu(u_zs[:, D_SSM:]))
        put(za_ref, _silu(jnp.dot(h, w_za_ref[...], preferred_element_type=_F32)))
        qkv = jnp.dot(h, w_qkv_ref[...], preferred_element_type=_F32)
        put(q_ref, qkv[:, :D_ATTN] * Q_SCALE)
        with_swap = lambda a: jnp.concatenate([a, pltpu.roll(a, HEAD_DIM, 1)], axis=1)
        put(k_ref, with_swap(qkv[:, D_ATTN:D_ATTN + D_KV]))
        put(v_ref, with_swap(qkv[:, D_ATTN + D_KV:]))


def _inproj(x, gain, w_in):
    bsz, s, _ = x.shape
    nt = INPROJ_TOKEN_TILE // bsz
    tok = lambda i: (0, i, 0)
    fixed = lambda i: (0, 0)
    o_za = D_ATTN + 2 * D_KV
    o_u = o_za + D_ATTN
    weights = [w_in[:, o_u:o_u + 2 * D_SSM].astype(_BF16),
               w_in[:, o_za:o_u].astype(_BF16), w_in[:, :o_za].astype(_BF16)]
    outs = [D_ATTN, D_KVX, D_KVX, D_ATTN, None, D_SSM]
    act = lambda n: (pl.BlockSpec((bsz, nt, n), tok), jax.ShapeDtypeStruct((bsz, s, n), _BF16))
    ug = (pl.BlockSpec((N_SSM_GROUPS, nt // CHUNK, bsz, CHUNK_W), lambda i: (0, i, 0, 0)),
          jax.ShapeDtypeStruct((N_SSM_GROUPS, s // CHUNK, bsz, CHUNK_W), _BF16))
    specs, shapes = zip(*[ug if n is None else act(n) for n in outs])
    return pl.pallas_call(
        _inproj_kernel,
        grid=(s // nt,),
        in_specs=[pl.BlockSpec((bsz, nt, D_MODEL), tok),
                  pl.BlockSpec((1, D_MODEL), fixed)]
                 + [pl.BlockSpec(w.shape, fixed) for w in weights],
        out_specs=list(specs),
        out_shape=list(shapes),
        scratch_shapes=[pltpu.VMEM((D_SSM // LANES, bsz * SUB_STEPS, LANES), _F32)
                        for _ in range(nt // SUB_STEPS)],
        compiler_params=pltpu.CompilerParams(
            dimension_semantics=("arbitrary",), vmem_limit_bytes=VMEM_LIMIT_BYTES),
        name="inproj",
    )(x, gain, *weights)


def _ssm_kernel(n_chunks, rows_per_chunk, u_ref, t_ref, s_ref, c_ref, ar_ref, ai_ref,
                y_ref, s_scr, x_scr):
    rb = rows_per_chunk
    groups = range(u_ref.shape[0])
    for gi in groups:
        s_scr[gi] = jnp.dot(u_ref[gi], s_ref[gi], preferred_element_type=_F32)
    ar = [jnp.broadcast_to(ar_ref[gi], (rb, 2 * SSM_STATE)) for gi in groups]
    ai = [jnp.broadcast_to(ai_ref[gi], (rb, 2 * SSM_STATE)) for gi in groups]
    is_fwd = lax.broadcasted_iota(jnp.int32, (rb, 2 * SSM_STATE), 1) < SSM_STATE
    is_bwd = jnp.logical_not(is_fwd)
    re_cols = pl.ds(0, 2 * SSM_STATE)
    im_cols = pl.ds(2 * SSM_STATE, 2 * SSM_STATE)

    def step(i, carry):
        rows_f = pl.ds(pl.multiple_of(i * rb, rb), rb)
        rows_b = pl.ds(pl.multiple_of((n_chunks - 1 - i) * rb, rb), rb)
        new = []
        for gi in groups:
            st_re, st_im = carry[2 * gi], carry[2 * gi + 1]
            pltpu.store(x_scr.at[gi, rows_f, re_cols], st_re, mask=is_fwd)
            pltpu.store(x_scr.at[gi, rows_f, im_cols], st_im, mask=is_fwd)
            pltpu.store(x_scr.at[gi, rows_b, re_cols], st_re, mask=is_bwd)
            pltpu.store(x_scr.at[gi, rows_b, im_cols], st_im, mask=is_bwd)
            in_re = jnp.where(is_fwd, s_scr[gi, rows_f, re_cols], s_scr[gi, rows_b, re_cols])
            in_im = jnp.where(is_fwd, s_scr[gi, rows_f, im_cols], s_scr[gi, rows_b, im_cols])
            new.append(ar[gi] * st_re - ai[gi] * st_im + in_re)
            new.append(ar[gi] * st_im + ai[gi] * st_re + in_im)
        return tuple(new)

    zero = jnp.zeros((rb, 2 * SSM_STATE), _F32)
    lax.fori_loop(0, n_chunks, step, (zero,) * (2 * len(groups)), unroll=4)

    for gi in groups:
        y = jnp.dot(u_ref[gi], t_ref[gi], preferred_element_type=_F32)
        y = y + lax.dot_general(x_scr[gi].astype(_BF16), c_ref[gi], (((1,), (1,)), ((), ())),
                                preferred_element_type=_F32)
        y_ref[gi] = y.astype(y_ref.dtype)


def _ssm(ug, tmat, smat, cmat, ar, ai, n_chunks, rows_per_chunk):
    g, rows, _ = ug.shape
    gb = SSM_GROUPS_PER_STEP
    per_group = lambda i: (i, 0, 0)
    return pl.pallas_call(
        functools.partial(_ssm_kernel, n_chunks, rows_per_chunk),
        grid=(g // gb,),
        in_specs=[pl.BlockSpec((gb, rows, CHUNK_W), per_group),
                  pl.BlockSpec((gb, CHUNK_W, CHUNK_W), per_group),
                  pl.BlockSpec((gb, CHUNK_W, STATE_W), per_group),
                  pl.BlockSpec((gb, CHUNK_W, STATE_W), per_group),
                  pl.BlockSpec((gb, 1, 2 * SSM_STATE), per_group),
                  pl.BlockSpec((gb, 1, 2 * SSM_STATE), per_group)],
        out_specs=pl.BlockSpec((gb, rows, CHUNK_W), per_group),
        out_shape=jax.ShapeDtypeStruct((g, rows, CHUNK_W), _BF16),
        scratch_shapes=[pltpu.VMEM((gb, rows, STATE_W), _F32),
                        pltpu.VMEM((gb, rows, STATE_W), _F32)],
        compiler_params=pltpu.CompilerParams(
            dimension_semantics=("arbitrary",), vmem_limit_bytes=VMEM_LIMIT_BYTES),
        name="ssm",
    )(ug, tmat, smat, cmat, ar, ai)


def _ssm_matrices(a_re, a_im, log_dt, b_re, b_im, c_re, c_im, d_skip):
    L, G, P, C = CHUNK, N_SSM_GROUPS, SSM_STATE, SSM_GROUP
    both = lambda a: jnp.concatenate([a[0], a[1]], axis=-1)
    dt = jnp.broadcast_to(jnp.exp(log_dt)[..., None], a_re.shape)
    lam = jnp.stack([both(a_re), both(a_im), both(dt)], axis=1)
    bt = jnp.stack([b_re, b_im], axis=0).transpose(2, 0, 4, 1, 3).reshape(G, 2, C, 2 * P)
    ct = jnp.stack([c_re, c_im], axis=0).transpose(2, 0, 3, 1, 4).reshape(G, 2, C, 2 * P)
    dvec = jnp.tile(d_skip.reshape(G, 1, C), (1, 1, L))
    gb = SLOTS_PER_VREG
    per_group = lambda *blk: pl.BlockSpec((gb,) + blk, lambda i: (i,) + (0,) * len(blk))
    mat = jax.ShapeDtypeStruct((G, L * C, L * C), _BF16)
    vec = jax.ShapeDtypeStruct((G, 1, 2 * P), _F32)
    return pl.pallas_call(
        _ssm_prep_kernel,
        grid=(G // gb,),
        in_specs=[per_group(3, 2 * P), per_group(2, C, 2 * P), per_group(2, C, 2 * P),
                  per_group(1, L * C)],
        out_specs=[per_group(L * C, L * C), per_group(L * C, STATE_W),
                   per_group(L * C, STATE_W), per_group(1, 2 * P), per_group(1, 2 * P)],
        out_shape=[mat, mat, mat, vec, vec],
        compiler_params=pltpu.CompilerParams(dimension_semantics=("arbitrary",)),
        name="ssm_prep",
    )(lam, bt, ct, dvec)


def _ssm_prep_kernel(lam_ref, bt_ref, ct_ref, dvec_ref, t_ref, s_ref, c_ref, ar_ref, ai_ref):
    for gi in range(lam_ref.shape[0]):
        _ssm_prep_group(gi, lam_ref, bt_ref, ct_ref, dvec_ref, t_ref, s_ref, c_ref, ar_ref,
                        ai_ref)


def _ssm_prep_group(gi, lam_ref, bt_ref, ct_ref, dvec_ref, t_ref, s_ref, c_ref, ar_ref,
                    ai_ref):
    L, P, C = CHUNK, SSM_STATE, SSM_GROUP
    g8 = gi % SLOTS_PER_VREG

    def time_of_slot(slot):
        return (jnp.bitwise_and(slot, -SLOTS_PER_VREG)
                + jnp.bitwise_and(slot - g8, SLOTS_PER_VREG - 1))

    def cmul(x_re, x_im, y_re, y_im):
        return x_re * y_re - x_im * y_im, x_re * y_im + x_im * y_re

    a_re, a_im, dt = lam_ref[gi, 0:1], lam_ref[gi, 1:2], lam_ref[gi, 2:3]
    lr, li = a_re * dt, a_im * dt

    def cpow(tau):
        mag, ang = jnp.exp(lr * tau), li * tau
        return mag * jnp.cos(ang), mag * jnp.sin(ang)

    ab_re, ab_im = cpow(jnp.ones_like(lr))
    den = a_re * a_re + a_im * a_im
    co_re = ((ab_re - 1.0) * a_re + ab_im * a_im) / den
    co_im = (ab_im * a_re - (ab_re - 1.0) * a_im) / den
    bb_re, bb_im = cmul(bt_ref[gi, 0], bt_ref[gi, 1], co_re, co_im)
    cc_re, cc_im = ct_ref[gi, 0], ct_ref[gi, 1]

    def outer(p, m):
        (p_re, p_im), (m_re, m_im) = p, m
        blocks = [cmul(p_re[s:s + 1], p_im[s:s + 1], m_re, m_im) for s in range(L)]
        return (jnp.concatenate([b[0] for b in blocks], axis=0),
                jnp.concatenate([b[1] for b in blocks], axis=0))

    j = time_of_slot(lax.broadcasted_iota(jnp.int32, (L, 2 * P), 0)).astype(_F32)
    fwd = lax.broadcasted_iota(jnp.int32, (L, 2 * P), 1) < P
    s_re, s_im = outer(cpow(jnp.where(fwd, L - 1 - j, j)), (bb_re, bb_im))
    s_ref[gi] = jnp.concatenate([s_re, s_im], axis=1).astype(_BF16)
    w_re, w_im = outer(cpow(jnp.where(fwd, j + 1, L - j)), (cc_re, cc_im))
    wcat = jnp.concatenate([w_re, -w_im], axis=1)
    c_ref[gi] = wcat.astype(_BF16)
    l_re, l_im = outer(cpow(jnp.where(fwd, -1 - j, j - L)), (bb_re, bb_im))
    lcat = jnp.concatenate([l_re, l_im], axis=1)
    row = lax.broadcasted_iota(jnp.int32, (L * C, L * C), 0)
    col = lax.broadcasted_iota(jnp.int32, (L * C, L * C), 1)
    is_fwd_state = jnp.bitwise_and(col, P) == 0
    contract_state = (((1,), (1,)), ((), ()))
    mf = lax.dot_general(jnp.where(is_fwd_state, lcat, 0.0), wcat, contract_state,
                         precision=_HI, preferred_element_type=_F32)
    mb = lax.dot_general(jnp.where(is_fwd_state, 0.0, lcat), wcat, contract_state,
                         precision=_HI, preferred_element_type=_F32)
    lane_to_slot = C.bit_length() - 1
    jj = time_of_slot(jnp.right_shift(row, lane_to_slot))
    tt = time_of_slot(jnp.right_shift(col, lane_to_slot))
    tm = (jnp.where(tt >= jj, mf, 0.0) + jnp.where(tt <= jj, mb, 0.0)
          + jnp.where(row == col, dvec_ref[gi], 0.0))
    t_ref[gi] = tm.astype(_BF16)
    al_re, al_im = cpow(jnp.full_like(lr, float(L)))
    ar_ref[gi] = al_re
    ai_ref[gi] = al_im


def _attn_kernel(sink_ref, q_ref, k_ref, v_ref, bias_ref, za_ref, o_ref):
    nq = q_ref.shape[1] // BLOCK
    rows = lambda ref, c: ref[0, c * BLOCK:(c + 1) * BLOCK]
    k_blocks = [rows(k_ref, c) for c in range(nq)]
    v_blocks = [rows(v_ref, c) for c in range(nq)]
    lane = lax.broadcasted_iota(jnp.int32, (BLOCK, LANES), 1)
    halves = [lane < HEAD_DIM, lane >= HEAD_DIM]
    keep = [h.astype(_F32).astype(_BF16) for h in halves]
    slab = lambda blk, j, e: blk[:, (j ^ e) * LANES:((j ^ e) + 1) * LANES]
    k_half = {(j, e): [slab(kb, j, e) * keep[e] for kb in k_blocks]
              for j in range(KV_HEADS) for e in range(HEAD_PAIR)}
    v_half = {(j, e): [jnp.concatenate([slab(vb, j, e) * keep[e], keep[e]], axis=1)
                       for vb in v_blocks]
              for j in range(KV_HEADS) for e in range(HEAD_PAIR)}
    for c in range(nq):
        blocks = [b for b in (c - 1, c, c + 1) if 0 <= b < nq]
        slot0 = blocks[0] - (c - 1)
        nk = len(blocks) * BLOCK
        q = rows(q_ref, c)
        slabs_out = []
        for j in range(KV_HEADS):
            window = lambda halves_of: jnp.concatenate(
                [halves_of[j, e][b] for e in range(HEAD_PAIR) for b in blocks], axis=0)
            kcat = window(k_half)
            vcat = window(v_half)
            qs = jnp.concatenate(
                [q[:, (j * PAIRS_PER_KV + i) * LANES:(j * PAIRS_PER_KV + i + 1) * LANES]
                 for i in range(PAIRS_PER_KV)], axis=0)
            s = lax.dot_general(qs, kcat, (((1,), (1,)), ((), ())),
                                preferred_element_type=_F32)
            p_rows, e_rows = [], []
            for i in range(PAIRS_PER_KV):
                p_lanes, e_sink = [], []
                for e in range(HEAD_PAIR):
                    first_key = (e * 3 + slot0) * BLOCK
                    sg = (s[i * BLOCK:(i + 1) * BLOCK, e * nk:(e + 1) * nk]
                          + bias_ref[j, i * BLOCK:(i + 1) * BLOCK, first_key:first_key + nk])
                    sk = sink_ref[j * Q_PER_KV + i * HEAD_PAIR + e]
                    m = jnp.maximum(jnp.max(sg, axis=-1, keepdims=True), sk)
                    p_lanes.append(jnp.exp2(sg - m).astype(_BF16))
                    e_sink.append(jnp.broadcast_to(jnp.exp2(sk - m), (BLOCK, LANES)))
                p_rows.append(jnp.concatenate(p_lanes, axis=1))
                e_rows.append(jnp.where(halves[0], e_sink[0], e_sink[1]))
            p = jnp.concatenate(p_rows, axis=0)
            o = jnp.dot(p, vcat, preferred_element_type=_F32)
            den = o[:, LANES:] + jnp.concatenate(e_rows, axis=0)
            on = o[:, :LANES] / den
            slabs_out += [on[i * BLOCK:(i + 1) * BLOCK] for i in range(PAIRS_PER_KV)]
        o_all = jnp.concatenate(slabs_out, axis=1)
        o_ref[0, c * BLOCK:(c + 1) * BLOCK, :] = (
            o_all * rows(za_ref, c).astype(_F32)).astype(o_ref.dtype)


def _attention(sink, q, k, v, bias, za):
    b, s, _ = q.shape
    seq = lambda w: pl.BlockSpec((1, s, w), lambda i: (i, 0, 0))
    return pl.pallas_call(
        _attn_kernel,
        grid=(b,),
        in_specs=[pl.BlockSpec(memory_space=pltpu.SMEM), seq(D_ATTN), seq(D_KVX), seq(D_KVX),
                  pl.BlockSpec(bias.shape, lambda i: (0, 0, 0)), seq(D_ATTN)],
        out_specs=seq(D_ATTN),
        out_shape=jax.ShapeDtypeStruct((b, s, D_ATTN), _BF16),
        compiler_params=pltpu.CompilerParams(
            dimension_semantics=("arbitrary",), vmem_limit_bytes=VMEM_LIMIT_BYTES),
        name="attn",
    )(sink, q, k, v, bias, za)


def _t5_bucket_np(rel):
    half = NUM_BUCKETS // 2
    ret = (rel > 0).astype(np.int64) * half
    n = np.abs(rel)
    max_exact = half // 2
    nf = np.maximum(n, 1).astype(np.float64)
    large = max_exact + (np.log(nf / max_exact) / math.log(MAX_DISTANCE / max_exact)
                         * (half - max_exact)).astype(np.int64)
    large = np.minimum(large, half - 1)
    return ret + np.where(n < max_exact, n, large)


def _attn_bias(rel_table):
    rel = (np.arange(3 * BLOCK)[None, :] - BLOCK) - np.arange(BLOCK)[:, None]
    onehot = (_t5_bucket_np(rel)[None] == np.arange(NUM_BUCKETS)[:, None, None])
    bias = jnp.einsum('bh,bqk->hqk', LOG2_E * rel_table.astype(_F32),
                      jnp.asarray(onehot, _F32), precision=_HI)
    bias = jnp.where(jnp.asarray(np.abs(rel) <= WINDOW), bias, NEG_INF)
    bias = bias.reshape(KV_HEADS, PAIRS_PER_KV, HEAD_PAIR, BLOCK, 3 * BLOCK)
    bias = bias.transpose(0, 1, 3, 2, 4)
    return bias.reshape(KV_HEADS, PAIRS_PER_KV * BLOCK, HEAD_PAIR * 3 * BLOCK)


def _merge_kernel(x_ref, gain_ref, wg_ref, bg_ref, ao_ref, yg_ref, zs_ref, wglu_ref,
                  bglu_ref, wba_ref, wbs_ref, wout_ref, fgain_ref, o_ref, *y_scrs):
    nb, nt = x_ref.shape[0], x_ref.shape[1]
    rows = nb * SUB_STEPS
    for st in range(nt // SUB_STEPS):
        steps = slice(st * SUB_STEPS, (st + 1) * SUB_STEPS)
        x = x_ref[:, steps, :].reshape(rows, D_MODEL)
        h = _rms_normalize(x, gain_ref[...]).astype(_BF16)
        gates = _sigmoid(jnp.dot(h, wg_ref[...], preferred_element_type=_F32) + bg_ref[...])
        for c in range(SUB_STEPS // CHUNK):
            _from_group_major(yg_ref, y_scrs[st], st * (SUB_STEPS // CHUNK) + c, nb, c * CHUNK,
                              SUB_STEPS)
        y = jnp.concatenate([y_scrs[st][v] for v in range(D_SSM // LANES)], axis=1)
        y = _gelu_tanh(y)
        glu = (jnp.dot(y.astype(_BF16), wglu_ref[...], preferred_element_type=_F32)
               + bglu_ref[...])
        ssm = y * _sigmoid(glu) * zs_ref[:, steps, :].reshape(rows, D_SSM).astype(_F32)
        pa = jnp.dot(ao_ref[:, steps, :].reshape(rows, D_ATTN), wba_ref[...],
                     preferred_element_type=_F32)
        ps = jnp.dot(ssm.astype(_BF16), wbs_ref[...], preferred_element_type=_F32)
        merged = gates[:, :D_MODEL] * pa + gates[:, D_MODEL:] * ps
        xn = x + jnp.dot(merged.astype(_BF16), wout_ref[...], preferred_element_type=_F32)
        o_ref[:, steps, :] = _rms_normalize(xn, fgain_ref[...]).reshape(nb, SUB_STEPS, D_MODEL)


def _merge(x, gain, wg, bg, ao, yg, zs, wglu, bglu, wba, wbs, wout, fgain):
    bsz, s, _ = x.shape
    nt = TOKEN_TILE // bsz
    tok = lambda i: (0, i, 0)
    fixed = lambda i: (0, 0)
    full = lambda a: pl.BlockSpec(a.shape, fixed)
    act = lambda n: pl.BlockSpec((bsz, nt, n), tok)
    return pl.pallas_call(
        _merge_kernel,
        grid=(s // nt,),
        in_specs=[act(D_MODEL), full(gain), full(wg), full(bg), act(D_ATTN),
                  pl.BlockSpec((N_SSM_GROUPS, nt // CHUNK, bsz, CHUNK_W),
                               lambda i: (0, i, 0, 0)),
                  act(D_SSM), full(wglu), full(bglu), full(wba), full(wbs), full(wout),
                  full(fgain)],
        out_specs=act(D_MODEL),
        out_shape=jax.ShapeDtypeStruct((bsz, s, D_MODEL), _F32),
        scratch_shapes=[pltpu.VMEM((D_SSM // LANES, bsz * SUB_STEPS, LANES), _F32)
                        for _ in range(nt // SUB_STEPS)],
        compiler_params=pltpu.CompilerParams(
            dimension_semantics=("arbitrary",), vmem_limit_bytes=VMEM_LIMIT_BYTES),
        name="merge",
    )(x, gain, wg, bg, ao, yg, zs, wglu, bglu, wba, wbs, wout, fgain)


def _layer(x, norm_gain, w_in, b_gate, attn_sink, a_re, a_im, log_dt, b_re, b_im, c_re, c_im,
           d_skip, w_glu, b_glu, w_ba, w_bs, w_out, out_gain, bias):
    bsz, s, _ = x.shape
    n_chunks = s // CHUNK
    gain = norm_gain.reshape(1, D_MODEL).astype(_F32)
    q, k, v, za, ug, zs = _inproj(x, gain, w_in)

    tmat, smat, cmat, ar, ai = _ssm_matrices(a_re, a_im, log_dt, b_re, b_im, c_re, c_im,
                                             d_skip)
    yg = _ssm(ug.reshape(N_SSM_GROUPS, n_chunks * bsz, CHUNK_W), tmat, smat, cmat, ar, ai,
              n_chunks, bsz)
    yg = yg.reshape(N_SSM_GROUPS, n_chunks, bsz, CHUNK_W)

    ao = _attention(LOG2_E * attn_sink.astype(_F32), q, k, v, bias, za)

    w_gate = w_in[:, 2 * D_ATTN + 2 * D_KV + 2 * D_SSM:].astype(_BF16)
    return _merge(x, gain, w_gate, b_gate.reshape(1, -1).astype(_F32), ao, yg, zs,
                  w_glu.astype(_BF16), b_glu.reshape(1, -1).astype(_F32),
                  w_ba.astype(_BF16), w_bs.astype(_BF16), w_out.astype(_BF16),
                  out_gain.reshape(1, D_MODEL).astype(_F32))


def kernel(x, norm_gain, w_in, b_gate, attn_sink, rel_bias_table, ssm_a_re, ssm_a_im,
           ssm_log_dt, ssm_b_re, ssm_b_im, ssm_c_re, ssm_c_im, ssm_d, w_glu, b_glu,
           w_branch_attn, w_branch_ssm, w_out, final_norm_gain):
    depth = norm_gain.shape[0]
    assert depth == 1, "final norm is fused into the single layer's epilogue"
    bias = _attn_bias(rel_bias_table)
    l = 0
    return _layer(x, norm_gain[l], w_in[l], b_gate[l], attn_sink[l], ssm_a_re[l],
                  ssm_a_im[l], ssm_log_dt[l], ssm_b_re[l], ssm_b_im[l], ssm_c_re[l],
                  ssm_c_im[l], ssm_d[l], w_glu[l], b_glu[l], w_branch_attn[l],
                  w_branch_ssm[l], w_out[l], final_norm_gain, bias)
```

```python
import functools
import math

import jax
import jax.numpy as jnp
import numpy as np
from jax import lax
from jax.experimental import pallas as pl
from jax.experimental.pallas import tpu as pltpu

D_MODEL = 1024
ATTN_HEADS = 8
KV_HEADS = 2
Q_PER_KV = ATTN_HEADS // KV_HEADS
HEAD_DIM = 64
D_ATTN = ATTN_HEADS * HEAD_DIM
D_KV = KV_HEADS * HEAD_DIM
WINDOW = 128
BLOCK = 128
NUM_BUCKETS = 32
MAX_DISTANCE = 128
D_SSM = 512
SSM_GROUP = 16
N_SSM_GROUPS = D_SSM // SSM_GROUP
SSM_STATE = 64
EPS = 1e-6
NEG_INF = -1e30

LANES = 128
CHUNK = 16
CHUNK_W = CHUNK * SSM_GROUP
STATE_W = 4 * SSM_STATE
SLOTS_PER_VREG = LANES // SSM_GROUP
SSM_GROUPS_PER_STEP = 4

HEAD_PAIR = LANES // HEAD_DIM
PAIRS_PER_KV = Q_PER_KV // HEAD_PAIR
assert D_KV == LANES and KV_HEADS == HEAD_PAIR
D_KVX = 2 * D_KV
LOG2_E = math.log2(math.e)
Q_SCALE = HEAD_DIM ** -0.5 * LOG2_E

W_OFF_ZA = D_ATTN + 2 * D_KV
W_OFF_U = W_OFF_ZA + D_ATTN
W_OFF_GATES = W_OFF_U + 2 * D_SSM

TOKEN_TILE = 1024
INPROJ_TOKEN_TILE = 2048
SUB_STEPS = CHUNK
VMEM_LIMIT_BYTES = 56 * 1024 * 1024

_F32 = jnp.float32
_BF16 = jnp.bfloat16
_HI = lax.Precision.HIGHEST


def _sigmoid(x):
    return 0.5 * jnp.tanh(0.5 * x) + 0.5


def _silu(x):
    return x * _sigmoid(x)


def _gelu_tanh(x):
    c = math.sqrt(2.0 / math.pi)
    return 0.5 * x * (1.0 + jnp.tanh(c * (x + 0.044715 * (x * x * x))))


def _rms_normalize(x, gain):
    ms = jnp.mean(x * x, axis=-1, keepdims=True)
    return x * lax.rsqrt(ms + EPS) * gain


def _slot_masks(rows):
    lane = lax.broadcasted_iota(jnp.int32, (rows, LANES), 1)
    return [(lane >= p * SSM_GROUP) & (lane < (p + 1) * SSM_GROUP)
            for p in range(SLOTS_PER_VREG)]


def _to_group_major(u_scr, ug_ref, kk, nb, t0, steps):
    masks = _slot_masks(nb)
    for v in range(D_SSM // LANES):
        for half in range(CHUNK // SLOTS_PER_VREG):
            rolled = []
            for t8 in range(SLOTS_PER_VREG):
                r = t0 + half * SLOTS_PER_VREG + t8
                piece = u_scr[v, pl.ds(r, nb, stride=steps), :]
                rolled.append(pltpu.roll(piece, t8 * SSM_GROUP, 1) if t8 else piece)
            for p0 in range(SLOTS_PER_VREG):
                acc = rolled[0]
                for t8 in range(1, SLOTS_PER_VREG):
                    acc = jnp.where(masks[(p0 + t8) % SLOTS_PER_VREG], rolled[t8], acc)
                g = v * SLOTS_PER_VREG + p0
                ug_ref[g, kk, :, half * LANES:(half + 1) * LANES] = acc.astype(_BF16)


def _from_group_major(yg_ref, y_scr, kk, nb, t0, steps):
    masks = _slot_masks(nb)
    for v in range(D_SSM // LANES):
        for half in range(CHUNK // SLOTS_PER_VREG):
            src = [yg_ref[v * SLOTS_PER_VREG + p0, kk, :,
                          half * LANES:(half + 1) * LANES].astype(_F32)
                   for p0 in range(SLOTS_PER_VREG)]
            for t8 in range(SLOTS_PER_VREG):
                acc = src[0]
                for p0 in range(1, SLOTS_PER_VREG):
                    acc = jnp.where(masks[(p0 + t8) % SLOTS_PER_VREG], src[p0], acc)
                if t8:
                    acc = pltpu.roll(acc, LANES - t8 * SSM_GROUP, 1)
                r = t0 + half * SLOTS_PER_VREG + t8
                y_scr[v, pl.ds(r, nb, stride=steps), :] = acc


def _inproj_kernel(x_ref, gain_ref, w_ref, q_ref, k_ref, v_ref, za_ref, ug_ref, zs_ref,
                   *u_scrs):
    nb, nt = x_ref.shape[0], x_ref.shape[1]
    rows = nb * SUB_STEPS
    for st in range(nt // SUB_STEPS):
        steps = slice(st * SUB_STEPS, (st + 1) * SUB_STEPS)
        x = x_ref[:, steps, :].reshape(rows, D_MODEL)
        h = _rms_normalize(x, gain_ref[...]).astype(_BF16)

        def put(ref, val):
            ref[:, steps, :] = val.astype(_BF16).reshape(nb, SUB_STEPS, val.shape[-1])

        proj = lambda lo, n: jnp.dot(h, w_ref[:, lo:lo + n], preferred_element_type=_F32)
        u_zs = proj(W_OFF_U, 2 * D_SSM)
        for v in range(D_SSM // LANES):
            u_scrs[st][v] = u_zs[:, v * LANES:(v + 1) * LANES]
        for c in range(SUB_STEPS // CHUNK):
            _to_group_major(u_scrs[st], ug_ref, st * (SUB_STEPS // CHUNK) + c, nb, c * CHUNK,
                            SUB_STEPS)
        put(zs_ref, _silu(u_zs[:, D_SSM:]))
        put(za_ref, _silu(proj(W_OFF_ZA, D_ATTN)))
        qkv = proj(0, W_OFF_ZA)
        put(q_ref, qkv[:, :D_ATTN] * Q_SCALE)
        with_swap = lambda a: jnp.concatenate([a, pltpu.roll(a, HEAD_DIM, 1)], axis=1)
        put(k_ref, with_swap(qkv[:, D_ATTN:D_ATTN + D_KV]))
        put(v_ref, with_swap(qkv[:, D_ATTN + D_KV:]))


def _inproj(x, gain, w_in):
    bsz, s, _ = x.shape
    nt = INPROJ_TOKEN_TILE // bsz
    tok = lambda i: (0, i, 0)
    fixed = lambda i: (0, 0)
    w_proj = w_in[:, :W_OFF_GATES].astype(_BF16)
    outs = [D_ATTN, D_KVX, D_KVX, D_ATTN, None, D_SSM]
    act = lambda n: (pl.BlockSpec((bsz, nt, n), tok), jax.ShapeDtypeStruct((bsz, s, n), _BF16))
    ug = (pl.BlockSpec((N_SSM_GROUPS, nt // CHUNK, bsz, CHUNK_W), lambda i: (0, i, 0, 0)),
          jax.ShapeDtypeStruct((N_SSM_GROUPS, s // CHUNK, bsz, CHUNK_W), _BF16))
    specs, shapes = zip(*[ug if n is None else act(n) for n in outs])
    return pl.pallas_call(
        _inproj_kernel,
        grid=(s // nt,),
        in_specs=[pl.BlockSpec((bsz, nt, D_MODEL), tok),
                  pl.BlockSpec((1, D_MODEL), fixed)]
                 + [pl.BlockSpec(w_proj.shape, fixed)],
        out_specs=list(specs),
        out_shape=list(shapes),
        scratch_shapes=[pltpu.VMEM((D_SSM // LANES, bsz * SUB_STEPS, LANES), _F32)
                        for _ in range(nt // SUB_STEPS)],
        compiler_params=pltpu.CompilerParams(
            dimension_semantics=("arbitrary",), vmem_limit_bytes=VMEM_LIMIT_BYTES),
        name="inproj",
    )(x, gain, w_proj)


def _ssm_kernel(n_chunks, rows_per_chunk, u_ref, t_ref, s_ref, c_ref, ar_ref, ai_ref,
                y_ref, s_scr, x_scr):
    rb = rows_per_chunk
    groups = range(u_ref.shape[0])
    for gi in groups:
        s_scr[gi] = jnp.dot(u_ref[gi], s_ref[gi], preferred_element_type=_F32)
    ar = [jnp.broadcast_to(ar_ref[gi], (rb, 2 * SSM_STATE)) for gi in groups]
    ai = [jnp.broadcast_to(ai_ref[gi], (rb, 2 * SSM_STATE)) for gi in groups]
    is_fwd = lax.broadcasted_iota(jnp.int32, (rb, 2 * SSM_STATE), 1) < SSM_STATE
    is_bwd = jnp.logical_not(is_fwd)
    re_cols = pl.ds(0, 2 * SSM_STATE)
    im_cols = pl.ds(2 * SSM_STATE, 2 * SSM_STATE)

    def step(i, carry):
        rows_f = pl.ds(pl.multiple_of(i * rb, rb), rb)
        rows_b = pl.ds(pl.multiple_of((n_chunks - 1 - i) * rb, rb), rb)
        new = []
        for gi in groups:
            st_re, st_im = carry[2 * gi], carry[2 * gi + 1]
            pltpu.store(x_scr.at[gi, rows_f, re_cols], st_re, mask=is_fwd)
            pltpu.store(x_scr.at[gi, rows_f, im_cols], st_im, mask=is_fwd)
            pltpu.store(x_scr.at[gi, rows_b, re_cols], st_re, mask=is_bwd)
            pltpu.store(x_scr.at[gi, rows_b, im_cols], st_im, mask=is_bwd)
            in_re = jnp.where(is_fwd, s_scr[gi, rows_f, re_cols], s_scr[gi, rows_b, re_cols])
            in_im = jnp.where(is_fwd, s_scr[gi, rows_f, im_cols], s_scr[gi, rows_b, im_cols])
            new.append(ar[gi] * st_re - ai[gi] * st_im + in_re)
            new.append(ar[gi] * st_im + ai[gi] * st_re + in_im)
        return tuple(new)

    zero = jnp.zeros((rb, 2 * SSM_STATE), _F32)
    lax.fori_loop(0, n_chunks, step, (zero,) * (2 * len(groups)), unroll=True)

    for gi in groups:
        y = jnp.dot(u_ref[gi], t_ref[gi], preferred_element_type=_F32)
        y = y + lax.dot_general(x_scr[gi].astype(_BF16), c_ref[gi], (((1,), (1,)), ((), ())),
                                preferred_element_type=_F32)
        y_ref[gi] = y.astype(y_ref.dtype)


def _ssm(ug, tmat, smat, cmat, ar, ai, n_chunks, rows_per_chunk):
    g, rows, _ = ug.shape
    gb = SSM_GROUPS_PER_STEP
    per_group = lambda i: (i, 0, 0)
    return pl.pallas_call(
        functools.partial(_ssm_kernel, n_chunks, rows_per_chunk),
        grid=(g // gb,),
        in_specs=[pl.BlockSpec((gb, rows, CHUNK_W), per_group),
                  pl.BlockSpec((gb, CHUNK_W, CHUNK_W), per_group),
                  pl.BlockSpec((gb, CHUNK_W, STATE_W), per_group),
                  pl.BlockSpec((gb, CHUNK_W, STATE_W), per_group),
                  pl.BlockSpec((gb, 1, 2 * SSM_STATE), per_group),
                  pl.BlockSpec((gb, 1, 2 * SSM_STATE), per_group)],
        out_specs=pl.BlockSpec((gb, rows, CHUNK_W), per_group),
        out_shape=jax.ShapeDtypeStruct((g, rows, CHUNK_W), _BF16),
        scratch_shapes=[pltpu.VMEM((gb, rows, STATE_W), _F32),
                        pltpu.VMEM((gb, rows, STATE_W), _F32)],
        compiler_params=pltpu.CompilerParams(
            dimension_semantics=("arbitrary",), vmem_limit_bytes=VMEM_LIMIT_BYTES),
        name="ssm",
    )(ug, tmat, smat, cmat, ar, ai)


def _ssm_matrices(a_re, a_im, log_dt, b_re, b_im, c_re, c_im, d_skip):
    L, G, P, C = CHUNK, N_SSM_GROUPS, SSM_STATE, SSM_GROUP
    both = lambda a: jnp.concatenate([a[0], a[1]], axis=-1)
    dt = jnp.broadcast_to(jnp.exp(log_dt)[..., None], a_re.shape)
    lam = jnp.stack([both(a_re), both(a_im), both(dt)], axis=1)
    bt = jnp.stack([b_re, b_im], axis=0).transpose(2, 0, 4, 1, 3).reshape(G, 2, C, 2 * P)
    ct = jnp.stack([c_re, c_im], axis=0).transpose(2, 0, 3, 1, 4).reshape(G, 2, C, 2 * P)
    dvec = jnp.tile(d_skip.reshape(G, 1, C), (1, 1, L))
    gb = SLOTS_PER_VREG
    per_group = lambda *blk: pl.BlockSpec((gb,) + blk, lambda i: (i,) + (0,) * len(blk))
    mat = jax.ShapeDtypeStruct((G, L * C, L * C), _BF16)
    vec = jax.ShapeDtypeStruct((G, 1, 2 * P), _F32)
    return pl.pallas_call(
        _ssm_prep_kernel,
        grid=(G // gb,),
        in_specs=[per_group(3, 2 * P), per_group(2, C, 2 * P), per_group(2, C, 2 * P),
                  per_group(1, L * C)],
        out_specs=[per_group(L * C, L * C), per_group(L * C, STATE_W),
                   per_group(L * C, STATE_W), per_group(1, 2 * P), per_group(1, 2 * P)],
        out_shape=[mat, mat, mat, vec, vec],
        compiler_params=pltpu.CompilerParams(dimension_semantics=("arbitrary",)),
        name="ssm_prep",
    )(lam, bt, ct, dvec)


def _ssm_prep_kernel(lam_ref, bt_ref, ct_ref, dvec_ref, t_ref, s_ref, c_ref, ar_ref, ai_ref):
    for gi in range(lam_ref.shape[0]):
        _ssm_prep_group(gi, lam_ref, bt_ref, ct_ref, dvec_ref, t_ref, s_ref, c_ref, ar_ref,
                        ai_ref)


def _ssm_prep_group(gi, lam_ref, bt_ref, ct_ref, dvec_ref, t_ref, s_ref, c_ref, ar_ref,
                    ai_ref):
    L, P, C = CHUNK, SSM_STATE, SSM_GROUP
    g8 = gi % SLOTS_PER_VREG

    def time_of_slot(slot):
        return (jnp.bitwise_and(slot, -SLOTS_PER_VREG)
                + jnp.bitwise_and(slot - g8, SLOTS_PER_VREG - 1))

    def cmul(x_re, x_im, y_re, y_im):
        return x_re * y_re - x_im * y_im, x_re * y_im + x_im * y_re

    a_re, a_im, dt = lam_ref[gi, 0:1], lam_ref[gi, 1:2], lam_ref[gi, 2:3]
    lr, li = a_re * dt, a_im * dt

    def cpow(tau):
        mag, ang = jnp.exp(lr * tau), li * tau
        return mag * jnp.cos(ang), mag * jnp.sin(ang)

    ab_re, ab_im = cpow(jnp.ones_like(lr))
    den = a_re * a_re + a_im * a_im
    co_re = ((ab_re - 1.0) * a_re + ab_im * a_im) / den
    co_im = (ab_im * a_re - (ab_re - 1.0) * a_im) / den
    bb_re, bb_im = cmul(bt_ref[gi, 0], bt_ref[gi, 1], co_re, co_im)
    cc_re, cc_im = ct_ref[gi, 0], ct_ref[gi, 1]

    def outer(p, m):
        (p_re, p_im), (m_re, m_im) = p, m
        blocks = [cmul(p_re[s:s + 1], p_im[s:s + 1], m_re, m_im) for s in range(L)]
        return (jnp.concatenate([b[0] for b in blocks], axis=0),
                jnp.concatenate([b[1] for b in blocks], axis=0))

    j = time_of_slot(lax.broadcasted_iota(jnp.int32, (L, 2 * P), 0)).astype(_F32)
    fwd = lax.broadcasted_iota(jnp.int32, (L, 2 * P), 1) < P
    s_re, s_im = outer(cpow(jnp.where(fwd, L - 1 - j, j)), (bb_re, bb_im))
    s_ref[gi] = jnp.concatenate([s_re, s_im], axis=1).astype(_BF16)
    w_re, w_im = outer(cpow(jnp.where(fwd, j + 1, L - j)), (cc_re, cc_im))
    wcat = jnp.concatenate([w_re, -w_im], axis=1)
    c_ref[gi] = wcat.astype(_BF16)
    l_re, l_im = outer(cpow(jnp.where(fwd, -1 - j, j - L)), (bb_re, bb_im))
    lcat = jnp.concatenate([l_re, l_im], axis=1)
    row = lax.broadcasted_iota(jnp.int32, (L * C, L * C), 0)
    col = lax.broadcasted_iota(jnp.int32, (L * C, L * C), 1)
    is_fwd_state = jnp.bitwise_and(col, P) == 0
    contract_state = (((1,), (1,)), ((), ()))
    mf = lax.dot_general(jnp.where(is_fwd_state, lcat, 0.0), wcat, contract_state,
                         precision=_HI, preferred_element_type=_F32)
    mb = lax.dot_general(jnp.where(is_fwd_state, 0.0, lcat), wcat, contract_state,
                         precision=_HI, preferred_element_type=_F32)
    lane_to_slot = C.bit_length() - 1
    jj = time_of_slot(jnp.right_shift(row, lane_to_slot))
    tt = time_of_slot(jnp.right_shift(col, lane_to_slot))
    tm = (jnp.where(tt >= jj, mf, 0.0) + jnp.where(tt <= jj, mb, 0.0)
          + jnp.where(row == col, dvec_ref[gi], 0.0))
    t_ref[gi] = tm.astype(_BF16)
    al_re, al_im = cpow(jnp.full_like(lr, float(L)))
    ar_ref[gi] = al_re
    ai_ref[gi] = al_im


def _attn_kernel(sink_ref, q_ref, k_ref, v_ref, bias_ref, za_ref, o_ref):
    nq = q_ref.shape[1] // BLOCK
    rows = lambda ref, c: ref[0, c * BLOCK:(c + 1) * BLOCK]
    k_blocks = [rows(k_ref, c) for c in range(nq)]
    v_blocks = [rows(v_ref, c) for c in range(nq)]
    lane = lax.broadcasted_iota(jnp.int32, (BLOCK, LANES), 1)
    halves = [lane < HEAD_DIM, lane >= HEAD_DIM]
    keep = [h.astype(_F32).astype(_BF16) for h in halves]
    slab = lambda blk, j, e: blk[:, (j ^ e) * LANES:((j ^ e) + 1) * LANES]
    k_half = {(j, e): [slab(kb, j, e) * keep[e] for kb in k_blocks]
              for j in range(KV_HEADS) for e in range(HEAD_PAIR)}
    v_half = {(j, e): [jnp.concatenate([slab(vb, j, e) * keep[e], keep[e]], axis=1)
                       for vb in v_blocks]
              for j in range(KV_HEADS) for e in range(HEAD_PAIR)}
    for c in range(nq):
        blocks = [b for b in (c - 1, c, c + 1) if 0 <= b < nq]
        slot0 = blocks[0] - (c - 1)
        nk = len(blocks) * BLOCK
        q = rows(q_ref, c)
        slabs_out = []
        for j in range(KV_HEADS):
            window = lambda halves_of: jnp.concatenate(
                [halves_of[j, e][b] for e in range(HEAD_PAIR) for b in blocks], axis=0)
            kcat = window(k_half)
            vcat = window(v_half)
            qs = jnp.concatenate(
                [q[:, (j * PAIRS_PER_KV + i) * LANES:(j * PAIRS_PER_KV + i + 1) * LANES]
                 for i in range(PAIRS_PER_KV)], axis=0)
            s = lax.dot_general(qs, kcat, (((1,), (1,)), ((), ())),
                                preferred_element_type=_F32)
            p_rows, e_rows = [], []
            for i in range(PAIRS_PER_KV):
                p_lanes, e_sink = [], []
                for e in range(HEAD_PAIR):
                    first_key = (e * 3 + slot0) * BLOCK
                    sg = (s[i * BLOCK:(i + 1) * BLOCK, e * nk:(e + 1) * nk]
                          + bias_ref[j, i * BLOCK:(i + 1) * BLOCK, first_key:first_key + nk])
                    sk = sink_ref[j * Q_PER_KV + i * HEAD_PAIR + e]
                    m = jnp.maximum(jnp.max(sg, axis=-1, keepdims=True), sk)
                    p_lanes.append(jnp.exp2(sg - m).astype(_BF16))
                    e_sink.append(jnp.broadcast_to(jnp.exp2(sk - m), (BLOCK, LANES)))
                p_rows.append(jnp.concatenate(p_lanes, axis=1))
                e_rows.append(jnp.where(halves[0], e_sink[0], e_sink[1]))
            p = jnp.concatenate(p_rows, axis=0)
            o = jnp.dot(p, vcat, preferred_element_type=_F32)
            den = o[:, LANES:] + jnp.concatenate(e_rows, axis=0)
            on = o[:, :LANES] / den
            slabs_out += [on[i * BLOCK:(i + 1) * BLOCK] for i in range(PAIRS_PER_KV)]
        o_all = jnp.concatenate(slabs_out, axis=1)
        o_ref[0, c * BLOCK:(c + 1) * BLOCK, :] = (
            o_all * rows(za_ref, c).astype(_F32)).astype(o_ref.dtype)


def _attention(sink, q, k, v, bias, za):
    b, s, _ = q.shape
    seq = lambda w: pl.BlockSpec((1, s, w), lambda i: (i, 0, 0))
    return pl.pallas_call(
        _attn_kernel,
        grid=(b,),
        in_specs=[pl.BlockSpec(memory_space=pltpu.SMEM), seq(D_ATTN), seq(D_KVX), seq(D_KVX),
                  pl.BlockSpec(bias.shape, lambda i: (0, 0, 0)), seq(D_ATTN)],
        out_specs=seq(D_ATTN),
        out_shape=jax.ShapeDtypeStruct((b, s, D_ATTN), _BF16),
        compiler_params=pltpu.CompilerParams(
            dimension_semantics=("arbitrary",), vmem_limit_bytes=VMEM_LIMIT_BYTES),
        name="attn",
    )(sink, q, k, v, bias, za)


def _t5_bucket_np(rel):
    half = NUM_BUCKETS // 2
    ret = (rel > 0).astype(np.int64) * half
    n = np.abs(rel)
    max_exact = half // 2
    nf = np.maximum(n, 1).astype(np.float64)
    large = max_exact + (np.log(nf / max_exact) / math.log(MAX_DISTANCE / max_exact)
                         * (half - max_exact)).astype(np.int64)
    large = np.minimum(large, half - 1)
    return ret + np.where(n < max_exact, n, large)


def _attn_bias(rel_table):
    rel = (np.arange(3 * BLOCK)[None, :] - BLOCK) - np.arange(BLOCK)[:, None]
    onehot = (_t5_bucket_np(rel)[None] == np.arange(NUM_BUCKETS)[:, None, None])
    bias = jnp.einsum('bh,bqk->hqk', LOG2_E * rel_table.astype(_F32),
                      jnp.asarray(onehot, _F32), precision=_HI)
    bias = jnp.where(jnp.asarray(np.abs(rel) <= WINDOW), bias, NEG_INF)
    bias = bias.reshape(KV_HEADS, PAIRS_PER_KV, HEAD_PAIR, BLOCK, 3 * BLOCK)
    bias = bias.transpose(0, 1, 3, 2, 4)
    return bias.reshape(KV_HEADS, PAIRS_PER_KV * BLOCK, HEAD_PAIR * 3 * BLOCK)


def _merge_kernel(x_ref, gain_ref, wg_ref, bg_ref, ao_ref, yg_ref, zs_ref, wglu_ref,
                  bglu_ref, wba_ref, wbs_ref, wout_ref, fgain_ref, o_ref, *y_scrs):
    nb, nt = x_ref.shape[0], x_ref.shape[1]
    rows = nb * SUB_STEPS
    for st in range(nt // SUB_STEPS):
        steps = slice(st * SUB_STEPS, (st + 1) * SUB_STEPS)
        x = x_ref[:, steps, :].reshape(rows, D_MODEL)
        h = _rms_normalize(x, gain_ref[...]).astype(_BF16)
        gates = _sigmoid(jnp.dot(h, wg_ref[...], preferred_element_type=_F32) + bg_ref[...])
        for c in range(SUB_STEPS // CHUNK):
            _from_group_major(yg_ref, y_scrs[st], st * (SUB_STEPS // CHUNK) + c, nb, c * CHUNK,
                              SUB_STEPS)
        y = jnp.concatenate([y_scrs[st][v] for v in range(D_SSM // LANES)], axis=1)
        y = _gelu_tanh(y)
        glu = (jnp.dot(y.astype(_BF16), wglu_ref[...], preferred_element_type=_F32)
               + bglu_ref[...])
        ssm = y * _sigmoid(glu) * zs_ref[:, steps, :].reshape(rows, D_SSM).astype(_F32)
        pa = jnp.dot(ao_ref[:, steps, :].reshape(rows, D_ATTN), wba_ref[...],
                     preferred_element_type=_F32)
        ps = jnp.dot(ssm.astype(_BF16), wbs_ref[...], preferred_element_type=_F32)
        merged = gates[:, :D_MODEL] * pa + gates[:, D_MODEL:] * ps
        xn = x + jnp.dot(merged.astype(_BF16), wout_ref[...], preferred_element_type=_F32)
        o_ref[:, steps, :] = _rms_normalize(xn, fgain_ref[...]).reshape(nb, SUB_STEPS, D_MODEL)


def _merge(x, gain, wg, bg, ao, yg, zs, wglu, bglu, wba, wbs, wout, fgain):
    bsz, s, _ = x.shape
    nt = TOKEN_TILE // bsz
    tok = lambda i: (0, i, 0)
    fixed = lambda i: (0, 0)
    full = lambda a: pl.BlockSpec(a.shape, fixed)
    act = lambda n: pl.BlockSpec((bsz, nt, n), tok)
    return pl.pallas_call(
        _merge_kernel,
        grid=(s // nt,),
        in_specs=[act(D_MODEL), full(gain), full(wg), full(bg), act(D_ATTN),
                  pl.BlockSpec((N_SSM_GROUPS, nt // CHUNK, bsz, CHUNK_W),
                               lambda i: (0, i, 0, 0)),
                  act(D_SSM), full(wglu), full(bglu), full(wba), full(wbs), full(wout),
                  full(fgain)],
        out_specs=act(D_MODEL),
        out_shape=jax.ShapeDtypeStruct((bsz, s, D_MODEL), _F32),
        scratch_shapes=[pltpu.VMEM((D_SSM // LANES, bsz * SUB_STEPS, LANES), _F32)
                        for _ in range(nt // SUB_STEPS)],
        compiler_params=pltpu.CompilerParams(
            dimension_semantics=("arbitrary",), vmem_limit_bytes=VMEM_LIMIT_BYTES),
        name="merge",
    )(x, gain, wg, bg, ao, yg, zs, wglu, bglu, wba, wbs, wout, fgain)


def _layer(x, norm_gain, w_in, b_gate, attn_sink, a_re, a_im, log_dt, b_re, b_im, c_re, c_im,
           d_skip, w_glu, b_glu, w_ba, w_bs, w_out, out_gain, bias):
    bsz, s, _ = x.shape
    n_chunks = s // CHUNK
    gain = norm_gain.reshape(1, D_MODEL).astype(_F32)
    q, k, v, za, ug, zs = _inproj(x, gain, w_in)

    tmat, smat, cmat, ar, ai = _ssm_matrices(a_re, a_im, log_dt, b_re, b_im, c_re, c_im,
                                             d_skip)
    yg = _ssm(ug.reshape(N_SSM_GROUPS, n_chunks * bsz, CHUNK_W), tmat, smat, cmat, ar, ai,
              n_chunks, bsz)
    yg = yg.reshape(N_SSM_GROUPS, n_chunks, bsz, CHUNK_W)

    ao = _attention(LOG2_E * attn_sink.astype(_F32), q, k, v, bias, za)

    w_gate = w_in[:, W_OFF_GATES:].astype(_BF16)
    return _merge(x, gain, w_gate, b_gate.reshape(1, -1).astype(_F32), ao, yg, zs,
                  w_glu.astype(_BF16), b_glu.reshape(1, -1).astype(_F32),
                  w_ba.astype(_BF16), w_bs.astype(_BF16), w_out.astype(_BF16),
                  out_gain.reshape(1, D_MODEL).astype(_F32))


def kernel(x, norm_gain, w_in, b_gate, attn_sink, rel_bias_table, ssm_a_re, ssm_a_im,
           ssm_log_dt, ssm_b_re, ssm_b_im, ssm_c_re, ssm_c_im, ssm_d, w_glu, b_glu,
           w_branch_attn, w_branch_ssm, w_out, final_norm_gain):
    depth = norm_gain.shape[0]
    assert depth == 1, "final norm is fused into the single layer's epilogue"
    bias = _attn_bias(rel_bias_table)
    l = 0
    return _layer(x, norm_gain[l], w_in[l], b_gate[l], attn_sink[l], ssm_a_re[l],
                  ssm_a_im[l], ssm_log_dt[l], ssm_b_re[l], ssm_b_im[l], ssm_c_re[l],
                  ssm_c_im[l], ssm_d[l], w_glu[l], b_glu[l], w_branch_attn[l],
                  w_branch_ssm[l], w_out[l], final_norm_gain, bias)
```

```python
import functools
import math

import jax
import jax.numpy as jnp
import numpy as np
from jax import lax
from jax.experimental import pallas as pl
from jax.experimental.pallas import tpu as pltpu

D_MODEL = 1024
ATTN_HEADS = 8
KV_HEADS = 2
Q_PER_KV = ATTN_HEADS // KV_HEADS
HEAD_DIM = 64
D_ATTN = ATTN_HEADS * HEAD_DIM
D_KV = KV_HEADS * HEAD_DIM
WINDOW = 128
BLOCK = 128
NUM_BUCKETS = 32
MAX_DISTANCE = 128
D_SSM = 512
SSM_GROUP = 16
N_SSM_GROUPS = D_SSM // SSM_GROUP
SSM_STATE = 64
EPS = 1e-6
NEG_INF = -1e30

LANES = 128
CHUNK = 16
CHUNK_W = CHUNK * SSM_GROUP
STATE_W = 4 * SSM_STATE
SLOTS_PER_VREG = LANES // SSM_GROUP
SSM_GROUPS_PER_STEP = 4

HEAD_PAIR = LANES // HEAD_DIM
PAIRS_PER_KV = Q_PER_KV // HEAD_PAIR
assert D_KV == LANES and KV_HEADS == HEAD_PAIR
D_KVX = 2 * D_KV
LOG2_E = math.log2(math.e)
Q_SCALE = HEAD_DIM ** -0.5 * LOG2_E

W_OFF_ZA = D_ATTN + 2 * D_KV
W_OFF_U = W_OFF_ZA + D_ATTN
W_OFF_GATES = W_OFF_U + 2 * D_SSM

TOKEN_TILE = 1024
INPROJ_TOKEN_TILE = 2048
SUB_STEPS = CHUNK
VMEM_LIMIT_BYTES = 56 * 1024 * 1024

_F32 = jnp.float32
_BF16 = jnp.bfloat16
_HI = lax.Precision.HIGHEST


def _twice_sigmoid_of_double(xh):
    return 1.0 + jnp.tanh(xh)


def _silu_of_double(xh):
    return xh * _twice_sigmoid_of_double(xh)


def _twice_gelu(x):
    c = math.sqrt(2.0 / math.pi)
    return x * (1.0 + jnp.tanh(c * (x + 0.044715 * (x * x * x))))


def _rms_normalize(x, gain):
    ms = jnp.mean(x * x, axis=-1, keepdims=True)
    return x * lax.rsqrt(ms + EPS) * gain


def _slot_masks(rows):
    lane = lax.broadcasted_iota(jnp.int32, (rows, LANES), 1)
    return [(lane >= p * SSM_GROUP) & (lane < (p + 1) * SSM_GROUP)
            for p in range(SLOTS_PER_VREG)]


def _to_group_major(u_scr, ug_ref, kk, nb, t0, steps):
    masks = _slot_masks(nb)
    for v in range(D_SSM // LANES):
        for half in range(CHUNK // SLOTS_PER_VREG):
            rolled = []
            for t8 in range(SLOTS_PER_VREG):
                r = t0 + half * SLOTS_PER_VREG + t8
                piece = u_scr[v, pl.ds(r, nb, stride=steps), :]
                rolled.append(pltpu.roll(piece, t8 * SSM_GROUP, 1) if t8 else piece)
            for p0 in range(SLOTS_PER_VREG):
                acc = rolled[0]
                for t8 in range(1, SLOTS_PER_VREG):
                    acc = jnp.where(masks[(p0 + t8) % SLOTS_PER_VREG], rolled[t8], acc)
                g = v * SLOTS_PER_VREG + p0
                ug_ref[g, kk, :, half * LANES:(half + 1) * LANES] = acc.astype(_BF16)


def _from_group_major(yg_ref, y_scr, kk, nb, t0, steps):
    masks = _slot_masks(nb)
    for v in range(D_SSM // LANES):
        for half in range(CHUNK // SLOTS_PER_VREG):
            src = [yg_ref[v * SLOTS_PER_VREG + p0, kk, :,
                          half * LANES:(half + 1) * LANES].astype(_F32)
                   for p0 in range(SLOTS_PER_VREG)]
            for t8 in range(SLOTS_PER_VREG):
                acc = src[0]
                for p0 in range(1, SLOTS_PER_VREG):
                    acc = jnp.where(masks[(p0 + t8) % SLOTS_PER_VREG], src[p0], acc)
                if t8:
                    acc = pltpu.roll(acc, LANES - t8 * SSM_GROUP, 1)
                r = t0 + half * SLOTS_PER_VREG + t8
                y_scr[v, pl.ds(r, nb, stride=steps), :] = acc


def _inproj_kernel(x_ref, gain_ref, w_ref, q_ref, k_ref, v_ref, za_ref, ug_ref, zs_ref,
                   *u_scrs):
    nb, nt = x_ref.shape[0], x_ref.shape[1]
    rows = nb * SUB_STEPS
    for st in range(nt // SUB_STEPS):
        steps = slice(st * SUB_STEPS, (st + 1) * SUB_STEPS)
        x = x_ref[:, steps, :].reshape(rows, D_MODEL)
        h = _rms_normalize(x, gain_ref[...]).astype(_BF16)

        def put(ref, val):
            ref[:, steps, :] = val.astype(_BF16).reshape(nb, SUB_STEPS, val.shape[-1])

        proj = lambda lo, n: jnp.dot(h, w_ref[:, lo:lo + n], preferred_element_type=_F32)
        u_zs = proj(W_OFF_U, 2 * D_SSM)
        for v in range(D_SSM // LANES):
            u_scrs[st][v] = u_zs[:, v * LANES:(v + 1) * LANES]
        for c in range(SUB_STEPS // CHUNK):
            _to_group_major(u_scrs[st], ug_ref, st * (SUB_STEPS // CHUNK) + c, nb, c * CHUNK,
                            SUB_STEPS)
        put(zs_ref, _silu_of_double(u_zs[:, D_SSM:]))
        put(za_ref, _silu_of_double(proj(W_OFF_ZA, D_ATTN)))
        qkv = proj(0, W_OFF_ZA)
        put(q_ref, qkv[:, :D_ATTN] * Q_SCALE)
        with_swap = lambda a: jnp.concatenate([a, pltpu.roll(a, HEAD_DIM, 1)], axis=1)
        put(k_ref, with_swap(qkv[:, D_ATTN:D_ATTN + D_KV]))
        put(v_ref, with_swap(qkv[:, D_ATTN + D_KV:]))


def _inproj(x, gain, w_in):
    bsz, s, _ = x.shape
    nt = INPROJ_TOKEN_TILE // bsz
    tok = lambda i: (0, i, 0)
    fixed = lambda i: (0, 0)
    col_scale = np.ones((W_OFF_GATES,), np.float32)
    col_scale[W_OFF_ZA:W_OFF_U] = 0.5
    col_scale[W_OFF_U + D_SSM:] = 0.5
    w_proj = (w_in[:, :W_OFF_GATES] * col_scale).astype(_BF16)
    outs = [D_ATTN, D_KVX, D_KVX, D_ATTN, None, D_SSM]
    act = lambda n: (pl.BlockSpec((bsz, nt, n), tok), jax.ShapeDtypeStruct((bsz, s, n), _BF16))
    ug = (pl.BlockSpec((N_SSM_GROUPS, nt // CHUNK, bsz, CHUNK_W), lambda i: (0, i, 0, 0)),
          jax.ShapeDtypeStruct((N_SSM_GROUPS, s // CHUNK, bsz, CHUNK_W), _BF16))
    specs, shapes = zip(*[ug if n is None else act(n) for n in outs])
    return pl.pallas_call(
        _inproj_kernel,
        grid=(s // nt,),
        in_specs=[pl.BlockSpec((bsz, nt, D_MODEL), tok),
                  pl.BlockSpec((1, D_MODEL), fixed)]
                 + [pl.BlockSpec(w_proj.shape, fixed)],
        out_specs=list(specs),
        out_shape=list(shapes),
        scratch_shapes=[pltpu.VMEM((D_SSM // LANES, bsz * SUB_STEPS, LANES), _F32)
                        for _ in range(nt // SUB_STEPS)],
        compiler_params=pltpu.CompilerParams(
            dimension_semantics=("arbitrary",), vmem_limit_bytes=VMEM_LIMIT_BYTES),
        name="inproj",
    )(x, gain, w_proj)


def _ssm_kernel(n_chunks, rows_per_chunk, u_ref, t_ref, s_ref, c_ref, ar_ref, ai_ref,
                y_ref, s_scr, x_scr):
    rb = rows_per_chunk
    groups = range(u_ref.shape[0])
    for gi in groups:
        s_scr[gi] = jnp.dot(u_ref[gi], s_ref[gi], preferred_element_type=_F32)
    ar = [jnp.broadcast_to(ar_ref[gi], (rb, 2 * SSM_STATE)) for gi in groups]
    ai = [jnp.broadcast_to(ai_ref[gi], (rb, 2 * SSM_STATE)) for gi in groups]
    is_fwd = lax.broadcasted_iota(jnp.int32, (rb, 2 * SSM_STATE), 1) < SSM_STATE
    is_bwd = jnp.logical_not(is_fwd)
    re_cols = pl.ds(0, 2 * SSM_STATE)
    im_cols = pl.ds(2 * SSM_STATE, 2 * SSM_STATE)

    def step(i, carry):
        rows_f = pl.ds(pl.multiple_of(i * rb, rb), rb)
        rows_b = pl.ds(pl.multiple_of((n_chunks - 1 - i) * rb, rb), rb)
        new = []
        for gi in groups:
            st_re, st_im = carry[2 * gi], carry[2 * gi + 1]
            pltpu.store(x_scr.at[gi, rows_f, re_cols], st_re, mask=is_fwd)
            pltpu.store(x_scr.at[gi, rows_f, im_cols], st_im, mask=is_fwd)
            pltpu.store(x_scr.at[gi, rows_b, re_cols], st_re, mask=is_bwd)
            pltpu.store(x_scr.at[gi, rows_b, im_cols], st_im, mask=is_bwd)
            in_re = jnp.where(is_fwd, s_scr[gi, rows_f, re_cols], s_scr[gi, rows_b, re_cols])
            in_im = jnp.where(is_fwd, s_scr[gi, rows_f, im_cols], s_scr[gi, rows_b, im_cols])
            new.append(ar[gi] * st_re - ai[gi] * st_im + in_re)
            new.append(ar[gi] * st_im + ai[gi] * st_re + in_im)
        return tuple(new)

    zero = jnp.zeros((rb, 2 * SSM_STATE), _F32)
    lax.fori_loop(0, n_chunks, step, (zero,) * (2 * len(groups)), unroll=True)

    for gi in groups:
        y = jnp.dot(u_ref[gi], t_ref[gi], preferred_element_type=_F32)
        y = y + lax.dot_general(x_scr[gi].astype(_BF16), c_ref[gi], (((1,), (1,)), ((), ())),
                                preferred_element_type=_F32)
        y_ref[gi] = y.astype(y_ref.dtype)


def _ssm(ug, tmat, smat, cmat, ar, ai, n_chunks, rows_per_chunk):
    g, rows, _ = ug.shape
    gb = SSM_GROUPS_PER_STEP
    per_group = lambda i: (i, 0, 0)
    return pl.pallas_call(
        functools.partial(_ssm_kernel, n_chunks, rows_per_chunk),
        grid=(g // gb,),
        in_specs=[pl.BlockSpec((gb, rows, CHUNK_W), per_group),
                  pl.BlockSpec((gb, CHUNK_W, CHUNK_W), per_group),
                  pl.BlockSpec((gb, CHUNK_W, STATE_W), per_group),
                  pl.BlockSpec((gb, CHUNK_W, STATE_W), per_group),
                  pl.BlockSpec((gb, 1, 2 * SSM_STATE), per_group),
                  pl.BlockSpec((gb, 1, 2 * SSM_STATE), per_group)],
        out_specs=pl.BlockSpec((gb, rows, CHUNK_W), per_group),
        out_shape=jax.ShapeDtypeStruct((g, rows, CHUNK_W), _BF16),
        scratch_shapes=[pltpu.VMEM((gb, rows, STATE_W), _F32),
                        pltpu.VMEM((gb, rows, STATE_W), _F32)],
        compiler_params=pltpu.CompilerParams(
            dimension_semantics=("arbitrary",), vmem_limit_bytes=VMEM_LIMIT_BYTES),
        name="ssm",
    )(ug, tmat, smat, cmat, ar, ai)


def _ssm_matrices(a_re, a_im, log_dt, b_re, b_im, c_re, c_im, d_skip):
    L, G, P, C = CHUNK, N_SSM_GROUPS, SSM_STATE, SSM_GROUP
    both = lambda a: jnp.concatenate([a[0], a[1]], axis=-1)
    dt = jnp.broadcast_to(jnp.exp(log_dt)[..., None], a_re.shape)
    lam = jnp.stack([both(a_re), both(a_im), both(dt)], axis=1)
    bt = jnp.stack([b_re, b_im], axis=0).transpose(2, 0, 4, 1, 3).reshape(G, 2, C, 2 * P)
    ct = jnp.stack([c_re, c_im], axis=0).transpose(2, 0, 3, 1, 4).reshape(G, 2, C, 2 * P)
    dvec = jnp.tile(d_skip.reshape(G, 1, C), (1, 1, L))
    gb = SLOTS_PER_VREG
    per_group = lambda *blk: pl.BlockSpec((gb,) + blk, lambda i: (i,) + (0,) * len(blk))
    mat = jax.ShapeDtypeStruct((G, L * C, L * C), _BF16)
    vec = jax.ShapeDtypeStruct((G, 1, 2 * P), _F32)
    return pl.pallas_call(
        _ssm_prep_kernel,
        grid=(G // gb,),
        in_specs=[per_group(3, 2 * P), per_group(2, C, 2 * P), per_group(2, C, 2 * P),
                  per_group(1, L * C)],
        out_specs=[per_group(L * C, L * C), per_group(L * C, STATE_W),
                   per_group(L * C, STATE_W), per_group(1, 2 * P), per_group(1, 2 * P)],
        out_shape=[mat, mat, mat, vec, vec],
        compiler_params=pltpu.CompilerParams(dimension_semantics=("arbitrary",)),
        name="ssm_prep",
    )(lam, bt, ct, dvec)


def _ssm_prep_kernel(lam_ref, bt_ref, ct_ref, dvec_ref, t_ref, s_ref, c_ref, ar_ref, ai_ref):
    for gi in range(lam_ref.shape[0]):
        _ssm_prep_group(gi, lam_ref, bt_ref, ct_ref, dvec_ref, t_ref, s_ref, c_ref, ar_ref,
                        ai_ref)


def _ssm_prep_group(gi, lam_ref, bt_ref, ct_ref, dvec_ref, t_ref, s_ref, c_ref, ar_ref,
                    ai_ref):
    L, P, C = CHUNK, SSM_STATE, SSM_GROUP
    g8 = gi % SLOTS_PER_VREG

    def time_of_slot(slot):
        return (jnp.bitwise_and(slot, -SLOTS_PER_VREG)
                + jnp.bitwise_and(slot - g8, SLOTS_PER_VREG - 1))

    def cmul(x_re, x_im, y_re, y_im):
        return x_re * y_re - x_im * y_im, x_re * y_im + x_im * y_re

    a_re, a_im, dt = lam_ref[gi, 0:1], lam_ref[gi, 1:2], lam_ref[gi, 2:3]
    lr, li = a_re * dt, a_im * dt

    def cpow(tau):
        mag, ang = jnp.exp(lr * tau), li * tau
        return mag * jnp.cos(ang), mag * jnp.sin(ang)

    ab_re, ab_im = cpow(jnp.ones_like(lr))
    den = a_re * a_re + a_im * a_im
    co_re = ((ab_re - 1.0) * a_re + ab_im * a_im) / den
    co_im = (ab_im * a_re - (ab_re - 1.0) * a_im) / den
    bb_re, bb_im = cmul(bt_ref[gi, 0], bt_ref[gi, 1], co_re, co_im)
    cc_re, cc_im = ct_ref[gi, 0], ct_ref[gi, 1]

    def outer(p, m):
        (p_re, p_im), (m_re, m_im) = p, m
        blocks = [cmul(p_re[s:s + 1], p_im[s:s + 1], m_re, m_im) for s in range(L)]
        return (jnp.concatenate([b[0] for b in blocks], axis=0),
                jnp.concatenate([b[1] for b in blocks], axis=0))

    j = time_of_slot(lax.broadcasted_iota(jnp.int32, (L, 2 * P), 0)).astype(_F32)
    fwd = lax.broadcasted_iota(jnp.int32, (L, 2 * P), 1) < P
    s_re, s_im = outer(cpow(jnp.where(fwd, L - 1 - j, j)), (bb_re, bb_im))
    s_ref[gi] = jnp.concatenate([s_re, s_im], axis=1).astype(_BF16)
    w_re, w_im = outer(cpow(jnp.where(fwd, j + 1, L - j)), (cc_re, cc_im))
    wcat = jnp.concatenate([w_re, -w_im], axis=1)
    c_ref[gi] = wcat.astype(_BF16)
    l_re, l_im = outer(cpow(jnp.where(fwd, -1 - j, j - L)), (bb_re, bb_im))
    lcat = jnp.concatenate([l_re, l_im], axis=1)
    row = lax.broadcasted_iota(jnp.int32, (L * C, L * C), 0)
    col = lax.broadcasted_iota(jnp.int32, (L * C, L * C), 1)
    is_fwd_state = jnp.bitwise_and(col, P) == 0
    contract_state = (((1,), (1,)), ((), ()))
    mf = lax.dot_general(jnp.where(is_fwd_state, lcat, 0.0), wcat, contract_state,
                         precision=_HI, preferred_element_type=_F32)
    mb = lax.dot_general(jnp.where(is_fwd_state, 0.0, lcat), wcat, contract_state,
                         precision=_HI, preferred_element_type=_F32)
    lane_to_slot = C.bit_length() - 1
    jj = time_of_slot(jnp.right_shift(row, lane_to_slot))
    tt = time_of_slot(jnp.right_shift(col, lane_to_slot))
    tm = (jnp.where(tt >= jj, mf, 0.0) + jnp.where(tt <= jj, mb, 0.0)
          + jnp.where(row == col, dvec_ref[gi], 0.0))
    t_ref[gi] = tm.astype(_BF16)
    al_re, al_im = cpow(jnp.full_like(lr, float(L)))
    ar_ref[gi] = al_re
    ai_ref[gi] = al_im


def _attn_kernel(sink_ref, q_ref, k_ref, v_ref, bias_ref, za_ref, o_ref):
    nq = q_ref.shape[1] // BLOCK
    rows = lambda ref, c: ref[0, c * BLOCK:(c + 1) * BLOCK]
    k_blocks = [rows(k_ref, c) for c in range(nq)]
    v_blocks = [rows(v_ref, c) for c in range(nq)]
    lane = lax.broadcasted_iota(jnp.int32, (BLOCK, LANES), 1)
    halves = [lane < HEAD_DIM, lane >= HEAD_DIM]
    keep = [h.astype(_F32).astype(_BF16) for h in halves]
    slab = lambda blk, j, e: blk[:, (j ^ e) * LANES:((j ^ e) + 1) * LANES]
    k_half = {(j, e): [slab(kb, j, e) * keep[e] for kb in k_blocks]
              for j in range(KV_HEADS) for e in range(HEAD_PAIR)}
    v_half = {(j, e): [jnp.concatenate([slab(vb, j, e) * keep[e], keep[e]], axis=1)
                       for vb in v_blocks]
              for j in range(KV_HEADS) for e in range(HEAD_PAIR)}
    for c in range(nq):
        blocks = [b for b in (c - 1, c, c + 1) if 0 <= b < nq]
        slot0 = blocks[0] - (c - 1)
        nk = len(blocks) * BLOCK
        q = rows(q_ref, c)
        slabs_out = []
        for j in range(KV_HEADS):
            window = lambda halves_of: jnp.concatenate(
                [halves_of[j, e][b] for e in range(HEAD_PAIR) for b in blocks], axis=0)
            kcat = window(k_half)
            vcat = window(v_half)
            qs = jnp.concatenate(
                [q[:, (j * PAIRS_PER_KV + i) * LANES:(j * PAIRS_PER_KV + i + 1) * LANES]
                 for i in range(PAIRS_PER_KV)], axis=0)
            s = lax.dot_general(qs, kcat, (((1,), (1,)), ((), ())),
                                preferred_element_type=_F32)
            p_rows, e_rows = [], []
            for i in range(PAIRS_PER_KV):
                p_lanes, e_sink = [], []
                for e in range(HEAD_PAIR):
                    first_key = (e * 3 + slot0) * BLOCK
                    sg = (s[i * BLOCK:(i + 1) * BLOCK, e * nk:(e + 1) * nk]
                          + bias_ref[j, i * BLOCK:(i + 1) * BLOCK, first_key:first_key + nk])
                    sk = sink_ref[j * Q_PER_KV + i * HEAD_PAIR + e]
                    m = jnp.maximum(jnp.max(sg, axis=-1, keepdims=True), sk)
                    p_lanes.append(jnp.exp2(sg - m).astype(_BF16))
                    e_sink.append(jnp.broadcast_to(jnp.exp2(sk - m), (BLOCK, LANES)))
                p_rows.append(jnp.concatenate(p_lanes, axis=1))
                e_rows.append(jnp.where(halves[0], e_sink[0], e_sink[1]))
            p = jnp.concatenate(p_rows, axis=0)
            o = jnp.dot(p, vcat, preferred_element_type=_F32)
            den = o[:, LANES:] + jnp.concatenate(e_rows, axis=0)
            on = o[:, :LANES] / den
            slabs_out += [on[i * BLOCK:(i + 1) * BLOCK] for i in range(PAIRS_PER_KV)]
        o_all = jnp.concatenate(slabs_out, axis=1)
        o_ref[0, c * BLOCK:(c + 1) * BLOCK, :] = (
            o_all * rows(za_ref, c).astype(_F32)).astype(o_ref.dtype)


def _attention(sink, q, k, v, bias, za):
    b, s, _ = q.shape
    seq = lambda w: pl.BlockSpec((1, s, w), lambda i: (i, 0, 0))
    return pl.pallas_call(
        _attn_kernel,
        grid=(b,),
        in_specs=[pl.BlockSpec(memory_space=pltpu.SMEM), seq(D_ATTN), seq(D_KVX), seq(D_KVX),
                  pl.BlockSpec(bias.shape, lambda i: (0, 0, 0)), seq(D_ATTN)],
        out_specs=seq(D_ATTN),
        out_shape=jax.ShapeDtypeStruct((b, s, D_ATTN), _BF16),
        compiler_params=pltpu.CompilerParams(
            dimension_semantics=("arbitrary",), vmem_limit_bytes=VMEM_LIMIT_BYTES),
        name="attn",
    )(sink, q, k, v, bias, za)


def _t5_bucket_np(rel):
    half = NUM_BUCKETS // 2
    ret = (rel > 0).astype(np.int64) * half
    n = np.abs(rel)
    max_exact = half // 2
    nf = np.maximum(n, 1).astype(np.float64)
    large = max_exact + (np.log(nf / max_exact) / math.log(MAX_DISTANCE / max_exact)
                         * (half - max_exact)).astype(np.int64)
    large = np.minimum(large, half - 1)
    return ret + np.where(n < max_exact, n, large)


def _attn_bias(rel_table):
    rel = (np.arange(3 * BLOCK)[None, :] - BLOCK) - np.arange(BLOCK)[:, None]
    onehot = (_t5_bucket_np(rel)[None] == np.arange(NUM_BUCKETS)[:, None, None])
    bias = jnp.einsum('bh,bqk->hqk', LOG2_E * rel_table.astype(_F32),
                      jnp.asarray(onehot, _F32), precision=_HI)
    bias = jnp.where(jnp.asarray(np.abs(rel) <= WINDOW), bias, NEG_INF)
    bias = bias.reshape(KV_HEADS, PAIRS_PER_KV, HEAD_PAIR, BLOCK, 3 * BLOCK)
    bias = bias.transpose(0, 1, 3, 2, 4)
    return bias.reshape(KV_HEADS, PAIRS_PER_KV * BLOCK, HEAD_PAIR * 3 * BLOCK)


def _merge_kernel(x_ref, gain_ref, wg_ref, bg_ref, ao_ref, yg_ref, zs_ref, wglu_ref,
                  bglu_ref, wba_ref, wbs_ref, wout_ref, fgain_ref, o_ref, *y_scrs):
    nb, nt = x_ref.shape[0], x_ref.shape[1]
    rows = nb * SUB_STEPS
    for st in range(nt // SUB_STEPS):
        steps = slice(st * SUB_STEPS, (st + 1) * SUB_STEPS)
        x = x_ref[:, steps, :].reshape(rows, D_MODEL)
        h = _rms_normalize(x, gain_ref[...]).astype(_BF16)
        gates2 = _twice_sigmoid_of_double(
            jnp.dot(h, wg_ref[...], preferred_element_type=_F32) + bg_ref[...])
        for c in range(SUB_STEPS // CHUNK):
            _from_group_major(yg_ref, y_scrs[st], st * (SUB_STEPS // CHUNK) + c, nb, c * CHUNK,
                              SUB_STEPS)
        y = jnp.concatenate([y_scrs[st][v] for v in range(D_SSM // LANES)], axis=1)
        y2 = _twice_gelu(y)
        glu_half = (jnp.dot(y2.astype(_BF16), wglu_ref[...], preferred_element_type=_F32)
                    + bglu_ref[...])
        ssm4 = (y2 * _twice_sigmoid_of_double(glu_half)
                * zs_ref[:, steps, :].reshape(rows, D_SSM).astype(_F32))
        pa = jnp.dot(ao_ref[:, steps, :].reshape(rows, D_ATTN), wba_ref[...],
                     preferred_element_type=_F32)
        ps = jnp.dot(ssm4.astype(_BF16), wbs_ref[...], preferred_element_type=_F32)
        merged2 = gates2[:, :D_MODEL] * pa + gates2[:, D_MODEL:] * ps
        xn = x + jnp.dot(merged2.astype(_BF16), wout_ref[...], preferred_element_type=_F32)
        o_ref[:, steps, :] = _rms_normalize(xn, fgain_ref[...]).reshape(nb, SUB_STEPS, D_MODEL)


def _merge(x, gain, wg, bg, ao, yg, zs, wglu, bglu, wba, wbs, wout, fgain):
    bsz, s, _ = x.shape
    nt = TOKEN_TILE // bsz
    tok = lambda i: (0, i, 0)
    fixed = lambda i: (0, 0)
    full = lambda a: pl.BlockSpec(a.shape, fixed)
    act = lambda n: pl.BlockSpec((bsz, nt, n), tok)
    return pl.pallas_call(
        _merge_kernel,
        grid=(s // nt,),
        in_specs=[act(D_MODEL), full(gain), full(wg), full(bg), act(D_ATTN),
                  pl.BlockSpec((N_SSM_GROUPS, nt // CHUNK, bsz, CHUNK_W),
                               lambda i: (0, i, 0, 0)),
                  act(D_SSM), full(wglu), full(bglu), full(wba), full(wbs), full(wout),
                  full(fgain)],
        out_specs=act(D_MODEL),
        out_shape=jax.ShapeDtypeStruct((bsz, s, D_MODEL), _F32),
        scratch_shapes=[pltpu.VMEM((D_SSM // LANES, bsz * SUB_STEPS, LANES), _F32)
                        for _ in range(nt // SUB_STEPS)],
        compiler_params=pltpu.CompilerParams(
            dimension_semantics=("arbitrary",), vmem_limit_bytes=VMEM_LIMIT_BYTES),
        name="merge",
    )(x, gain, wg, bg, ao, yg, zs, wglu, bglu, wba, wbs, wout, fgain)


def _layer(x, norm_gain, w_in, b_gate, attn_sink, a_re, a_im, log_dt, b_re, b_im, c_re, c_im,
           d_skip, w_glu, b_glu, w_ba, w_bs, w_out, out_gain, bias):
    bsz, s, d = x.shape
    assert d == D_MODEL and x.dtype == _F32 and w_in.shape == (D_MODEL, W_OFF_GATES + 2 * D_MODEL)
    assert bsz % 8 == 0 and s % BLOCK == 0
    assert INPROJ_TOKEN_TILE % (bsz * SUB_STEPS) == 0 and s % (INPROJ_TOKEN_TILE // bsz) == 0
    assert TOKEN_TILE % (bsz * SUB_STEPS) == 0 and s % (TOKEN_TILE // bsz) == 0
    n_chunks = s // CHUNK
    gain = norm_gain.reshape(1, D_MODEL).astype(_F32)
    q, k, v, za, ug, zs = _inproj(x, gain, w_in)

    tmat, smat, cmat, ar, ai = _ssm_matrices(a_re, a_im, log_dt, b_re, b_im, c_re, c_im,
                                             d_skip)
    yg = _ssm(ug.reshape(N_SSM_GROUPS, n_chunks * bsz, CHUNK_W), tmat, smat, cmat, ar, ai,
              n_chunks, bsz)
    yg = yg.reshape(N_SSM_GROUPS, n_chunks, bsz, CHUNK_W)

    ao = _attention(LOG2_E * attn_sink.astype(_F32), q, k, v, bias, za)

    return _merge(x, gain,
                  (0.5 * w_in[:, W_OFF_GATES:]).astype(_BF16),
                  (0.5 * b_gate).reshape(1, -1).astype(_F32),
                  ao, yg, zs,
                  (0.25 * w_glu).astype(_BF16),
                  (0.5 * b_glu).reshape(1, -1).astype(_F32),
                  w_ba.astype(_BF16),
                  (0.25 * w_bs).astype(_BF16),
                  (0.5 * w_out).astype(_BF16),
                  out_gain.reshape(1, D_MODEL).astype(_F32))


def kernel(x, norm_gain, w_in, b_gate, attn_sink, rel_bias_table, ssm_a_re, ssm_a_im,
           ssm_log_dt, ssm_b_re, ssm_b_im, ssm_c_re, ssm_c_im, ssm_d, w_glu, b_glu,
           w_branch_attn, w_branch_ssm, w_out, final_norm_gain):
    depth = norm_gain.shape[0]
    assert depth == 1, "final norm is fused into the single layer's epilogue"
    bias = _attn_bias(rel_bias_table)
    l = 0
    return _layer(x, norm_gain[l], w_in[l], b_gate[l], attn_sink[l], ssm_a_re[l],
                  ssm_a_im[l], ssm_log_dt[l], ssm_b_re[l], ssm_b_im[l], ssm_c_re[l],
                  ssm_c_im[l], ssm_d[l], w_glu[l], b_glu[l], w_branch_attn[l],
                  w_branch_ssm[l], w_out[l], final_norm_gain, bias)
```

```python
import functools
import math

import jax
import jax.numpy as jnp
import numpy as np
from jax import lax
from jax.experimental import pallas as pl
from jax.experimental.pallas import tpu as pltpu

D_MODEL = 1024
ATTN_HEADS = 8
KV_HEADS = 2
Q_PER_KV = ATTN_HEADS // KV_HEADS
HEAD_DIM = 64
D_ATTN = ATTN_HEADS * HEAD_DIM
D_KV = KV_HEADS * HEAD_DIM
WINDOW = 128
BLOCK = 128
NUM_BUCKETS = 32
MAX_DISTANCE = 128
D_SSM = 512
SSM_GROUP = 16
N_SSM_GROUPS = D_SSM // SSM_GROUP
SSM_STATE = 64
EPS = 1e-6
NEG_INF = -1e30

LANES = 128
CHUNK = 16
CHUNK_W = CHUNK * SSM_GROUP
STATE_W = 4 * SSM_STATE
SLOTS_PER_VREG = LANES // SSM_GROUP
SSM_GROUPS_PER_STEP = 4

HEAD_PAIR = LANES // HEAD_DIM
PAIRS_PER_KV = Q_PER_KV // HEAD_PAIR
assert D_KV == LANES and KV_HEADS == HEAD_PAIR
D_KVX = 2 * D_KV
LOG2_E = math.log2(math.e)
Q_SCALE = HEAD_DIM ** -0.5 * LOG2_E

W_OFF_ZA = D_ATTN + 2 * D_KV
W_OFF_U = W_OFF_ZA + D_ATTN
W_OFF_GATES = W_OFF_U + 2 * D_SSM

TOKEN_TILE = 1024
INPROJ_TOKEN_TILE = 2048
SUB_STEPS = CHUNK
VMEM_LIMIT_BYTES = 56 * 1024 * 1024

_F32 = jnp.float32
_BF16 = jnp.bfloat16
_HI = lax.Precision.HIGHEST


def _silu(x):
    return x * (0.5 * jnp.tanh(0.5 * x) + 0.5)


def _twice_sigmoid_of_double(xh):
    return 1.0 + jnp.tanh(xh)


def _twice_gelu(x):
    c = math.sqrt(2.0 / math.pi)
    return x * (1.0 + jnp.tanh(c * (x + 0.044715 * (x * x * x))))


def _rms_normalize(x, gain):
    ms = jnp.mean(x * x, axis=-1, keepdims=True)
    return x * lax.rsqrt(ms + EPS) * gain


def _slot_masks(rows):
    lane = lax.broadcasted_iota(jnp.int32, (rows, LANES), 1)
    return [(lane >= p * SSM_GROUP) & (lane < (p + 1) * SSM_GROUP)
            for p in range(SLOTS_PER_VREG)]


def _to_group_major(u_scr, ug_ref, kk, nb, t0, steps):
    masks = _slot_masks(nb)
    for v in range(D_SSM // LANES):
        for half in range(CHUNK // SLOTS_PER_VREG):
            rolled = []
            for t8 in range(SLOTS_PER_VREG):
                r = t0 + half * SLOTS_PER_VREG + t8
                piece = u_scr[v, pl.ds(r, nb, stride=steps), :]
                rolled.append(pltpu.roll(piece, t8 * SSM_GROUP, 1) if t8 else piece)
            for p0 in range(SLOTS_PER_VREG):
                acc = rolled[0]
                for t8 in range(1, SLOTS_PER_VREG):
                    acc = jnp.where(masks[(p0 + t8) % SLOTS_PER_VREG], rolled[t8], acc)
                g = v * SLOTS_PER_VREG + p0
                ug_ref[g, kk, :, half * LANES:(half + 1) * LANES] = acc.astype(_BF16)


def _from_group_major(yg_ref, y_scr, kk, nb, t0, steps):
    masks = _slot_masks(nb)
    for v in range(D_SSM // LANES):
        for half in range(CHUNK // SLOTS_PER_VREG):
            src = [yg_ref[v * SLOTS_PER_VREG + p0, kk, :,
                          half * LANES:(half + 1) * LANES].astype(_F32)
                   for p0 in range(SLOTS_PER_VREG)]
            for t8 in range(SLOTS_PER_VREG):
                acc = src[0]
                for p0 in range(1, SLOTS_PER_VREG):
                    acc = jnp.where(masks[(p0 + t8) % SLOTS_PER_VREG], src[p0], acc)
                if t8:
                    acc = pltpu.roll(acc, LANES - t8 * SSM_GROUP, 1)
                r = t0 + half * SLOTS_PER_VREG + t8
                y_scr[v, pl.ds(r, nb, stride=steps), :] = acc


def _inproj_kernel(x_ref, gain_ref, w_ref, q_ref, k_ref, v_ref, za_ref, ug_ref, zs_ref,
                   *u_scrs):
    nb, nt = x_ref.shape[0], x_ref.shape[1]
    rows = nb * SUB_STEPS
    for st in range(nt // SUB_STEPS):
        steps = slice(st * SUB_STEPS, (st + 1) * SUB_STEPS)
        x = x_ref[:, steps, :].reshape(rows, D_MODEL)
        h = _rms_normalize(x, gain_ref[...]).astype(_BF16)

        def put(ref, val):
            ref[:, steps, :] = val.astype(_BF16).reshape(nb, SUB_STEPS, val.shape[-1])

        proj = lambda lo, n: jnp.dot(h, w_ref[:, lo:lo + n], preferred_element_type=_F32)
        u_zs = proj(W_OFF_U, 2 * D_SSM)
        for v in range(D_SSM // LANES):
            u_scrs[st][v] = u_zs[:, v * LANES:(v + 1) * LANES]
        for c in range(SUB_STEPS // CHUNK):
            _to_group_major(u_scrs[st], ug_ref, st * (SUB_STEPS // CHUNK) + c, nb, c * CHUNK,
                            SUB_STEPS)
        put(zs_ref, _silu(u_zs[:, D_SSM:]))
        put(za_ref, _silu(proj(W_OFF_ZA, D_ATTN)))
        qkv = proj(0, W_OFF_ZA)
        put(q_ref, qkv[:, :D_ATTN] * Q_SCALE)
        with_swap = lambda a: jnp.concatenate([a, pltpu.roll(a, HEAD_DIM, 1)], axis=1)
        put(k_ref, with_swap(qkv[:, D_ATTN:D_ATTN + D_KV]))
        put(v_ref, with_swap(qkv[:, D_ATTN + D_KV:]))


def _inproj(x, gain, w_in):
    bsz, s, _ = x.shape
    nt = INPROJ_TOKEN_TILE // bsz
    tok = lambda i: (0, i, 0)
    fixed = lambda i: (0, 0)
    w_proj = w_in[:, :W_OFF_GATES].astype(_BF16)
    outs = [D_ATTN, D_KVX, D_KVX, D_ATTN, None, D_SSM]
    act = lambda n: (pl.BlockSpec((bsz, nt, n), tok), jax.ShapeDtypeStruct((bsz, s, n), _BF16))
    ug = (pl.BlockSpec((N_SSM_GROUPS, nt // CHUNK, bsz, CHUNK_W), lambda i: (0, i, 0, 0)),
          jax.ShapeDtypeStruct((N_SSM_GROUPS, s // CHUNK, bsz, CHUNK_W), _BF16))
    specs, shapes = zip(*[ug if n is None else act(n) for n in outs])
    return pl.pallas_call(
        _inproj_kernel,
        grid=(s // nt,),
        in_specs=[pl.BlockSpec((bsz, nt, D_MODEL), tok),
                  pl.BlockSpec((1, D_MODEL), fixed)]
                 + [pl.BlockSpec(w_proj.shape, fixed)],
        out_specs=list(specs),
        out_shape=list(shapes),
        scratch_shapes=[pltpu.VMEM((D_SSM // LANES, bsz * SUB_STEPS, LANES), _F32)
                        for _ in range(nt // SUB_STEPS)],
        compiler_params=pltpu.CompilerParams(
            dimension_semantics=("arbitrary",), vmem_limit_bytes=VMEM_LIMIT_BYTES),
        name="inproj",
    )(x, gain, w_proj)


def _ssm_kernel(n_chunks, rows_per_chunk, u_ref, t_ref, s_ref, c_ref, ar_ref, ai_ref,
                y_ref, s_scr, x_scr):
    rb = rows_per_chunk
    groups = range(u_ref.shape[0])
    for gi in groups:
        s_scr[gi] = jnp.dot(u_ref[gi], s_ref[gi], preferred_element_type=_F32)
    ar = [jnp.broadcast_to(ar_ref[gi], (rb, 2 * SSM_STATE)) for gi in groups]
    ai = [jnp.broadcast_to(ai_ref[gi], (rb, 2 * SSM_STATE)) for gi in groups]
    is_fwd = lax.broadcasted_iota(jnp.int32, (rb, 2 * SSM_STATE), 1) < SSM_STATE
    is_bwd = jnp.logical_not(is_fwd)
    re_cols = pl.ds(0, 2 * SSM_STATE)
    im_cols = pl.ds(2 * SSM_STATE, 2 * SSM_STATE)

    def step(i, carry):
        rows_f = pl.ds(pl.multiple_of(i * rb, rb), rb)
        rows_b = pl.ds(pl.multiple_of((n_chunks - 1 - i) * rb, rb), rb)
        new = []
        for gi in groups:
            st_re, st_im = carry[2 * gi], carry[2 * gi + 1]
            pltpu.store(x_scr.at[gi, rows_f, re_cols], st_re, mask=is_fwd)
            pltpu.store(x_scr.at[gi, rows_f, im_cols], st_im, mask=is_fwd)
            pltpu.store(x_scr.at[gi, rows_b, re_cols], st_re, mask=is_bwd)
            pltpu.store(x_scr.at[gi, rows_b, im_cols], st_im, mask=is_bwd)
            in_re = jnp.where(is_fwd, s_scr[gi, rows_f, re_cols], s_scr[gi, rows_b, re_cols])
            in_im = jnp.where(is_fwd, s_scr[gi, rows_f, im_cols], s_scr[gi, rows_b, im_cols])
            new.append(ar[gi] * st_re - ai[gi] * st_im + in_re)
            new.append(ar[gi] * st_im + ai[gi] * st_re + in_im)
        return tuple(new)

    zero = jnp.zeros((rb, 2 * SSM_STATE), _F32)
    lax.fori_loop(0, n_chunks, step, (zero,) * (2 * len(groups)), unroll=True)

    for gi in groups:
        y = jnp.dot(u_ref[gi], t_ref[gi], preferred_element_type=_F32)
        y = y + lax.dot_general(x_scr[gi].astype(_BF16), c_ref[gi], (((1,), (1,)), ((), ())),
                                preferred_element_type=_F32)
        y_ref[gi] = y.astype(y_ref.dtype)


def _ssm(ug, tmat, smat, cmat, ar, ai, n_chunks, rows_per_chunk):
    g, rows, _ = ug.shape
    gb = SSM_GROUPS_PER_STEP
    per_group = lambda i: (i, 0, 0)
    return pl.pallas_call(
        functools.partial(_ssm_kernel, n_chunks, rows_per_chunk),
        grid=(g // gb,),
        in_specs=[pl.BlockSpec((gb, rows, CHUNK_W), per_group),
                  pl.BlockSpec((gb, CHUNK_W, CHUNK_W), per_group),
                  pl.BlockSpec((gb, CHUNK_W, STATE_W), per_group),
                  pl.BlockSpec((gb, CHUNK_W, STATE_W), per_group),
                  pl.BlockSpec((gb, 1, 2 * SSM_STATE), per_group),
                  pl.BlockSpec((gb, 1, 2 * SSM_STATE), per_group)],
        out_specs=pl.BlockSpec((gb, rows, CHUNK_W), per_group),
        out_shape=jax.ShapeDtypeStruct((g, rows, CHUNK_W), _BF16),
        scratch_shapes=[pltpu.VMEM((gb, rows, STATE_W), _F32),
                        pltpu.VMEM((gb, rows, STATE_W), _F32)],
        compiler_params=pltpu.CompilerParams(
            dimension_semantics=("arbitrary",), vmem_limit_bytes=VMEM_LIMIT_BYTES),
        name="ssm",
    )(ug, tmat, smat, cmat, ar, ai)


def _ssm_matrices(a_re, a_im, log_dt, b_re, b_im, c_re, c_im, d_skip):
    L, G, P, C = CHUNK, N_SSM_GROUPS, SSM_STATE, SSM_GROUP
    both = lambda a: jnp.concatenate([a[0], a[1]], axis=-1)
    dt = jnp.broadcast_to(jnp.exp(log_dt)[..., None], a_re.shape)
    lam = jnp.stack([both(a_re), both(a_im), both(dt)], axis=1)
    bt = jnp.stack([b_re, b_im], axis=0).transpose(2, 0, 4, 1, 3).reshape(G, 2, C, 2 * P)
    ct = jnp.stack([c_re, c_im], axis=0).transpose(2, 0, 3, 1, 4).reshape(G, 2, C, 2 * P)
    dvec = jnp.tile(d_skip.reshape(G, 1, C), (1, 1, L))
    gb = SLOTS_PER_VREG
    per_group = lambda *blk: pl.BlockSpec((gb,) + blk, lambda i: (i,) + (0,) * len(blk))
    mat = jax.ShapeDtypeStruct((G, L * C, L * C), _BF16)
    vec = jax.ShapeDtypeStruct((G, 1, 2 * P), _F32)
    return pl.pallas_call(
        _ssm_prep_kernel,
        grid=(G // gb,),
        in_specs=[per_group(3, 2 * P), per_group(2, C, 2 * P), per_group(2, C, 2 * P),
                  per_group(1, L * C)],
        out_specs=[per_group(L * C, L * C), per_group(L * C, STATE_W),
                   per_group(L * C, STATE_W), per_group(1, 2 * P), per_group(1, 2 * P)],
        out_shape=[mat, mat, mat, vec, vec],
        compiler_params=pltpu.CompilerParams(dimension_semantics=("arbitrary",)),
        name="ssm_prep",
    )(lam, bt, ct, dvec)


def _ssm_prep_kernel(lam_ref, bt_ref, ct_ref, dvec_ref, t_ref, s_ref, c_ref, ar_ref, ai_ref):
    for gi in range(lam_ref.shape[0]):
        _ssm_prep_group(gi, lam_ref, bt_ref, ct_ref, dvec_ref, t_ref, s_ref, c_ref, ar_ref,
                        ai_ref)


def _ssm_prep_group(gi, lam_ref, bt_ref, ct_ref, dvec_ref, t_ref, s_ref, c_ref, ar_ref,
                    ai_ref):
    L, P, C = CHUNK, SSM_STATE, SSM_GROUP
    g8 = gi % SLOTS_PER_VREG

    def time_of_slot(slot):
        return (jnp.bitwise_and(slot, -SLOTS_PER_VREG)
                + jnp.bitwise_and(slot - g8, SLOTS_PER_VREG - 1))

    def cmul(x_re, x_im, y_re, y_im):
        return x_re * y_re - x_im * y_im, x_re * y_im + x_im * y_re

    a_re, a_im, dt = lam_ref[gi, 0:1], lam_ref[gi, 1:2], lam_ref[gi, 2:3]
    lr, li = a_re * dt, a_im * dt

    def cpow(tau):
        mag, ang = jnp.exp(lr * tau), li * tau
        return mag * jnp.cos(ang), mag * jnp.sin(ang)

    ab_re, ab_im = cpow(jnp.ones_like(lr))
    den = a_re * a_re + a_im * a_im
    co_re = ((ab_re - 1.0) * a_re + ab_im * a_im) / den
    co_im = (ab_im * a_re - (ab_re - 1.0) * a_im) / den
    bb_re, bb_im = cmul(bt_ref[gi, 0], bt_ref[gi, 1], co_re, co_im)
    cc_re, cc_im = ct_ref[gi, 0], ct_ref[gi, 1]

    def outer(p, m):
        (p_re, p_im), (m_re, m_im) = p, m
        blocks = [cmul(p_re[s:s + 1], p_im[s:s + 1], m_re, m_im) for s in range(L)]
        return (jnp.concatenate([b[0] for b in blocks], axis=0),
                jnp.concatenate([b[1] for b in blocks], axis=0))

    j = time_of_slot(lax.broadcasted_iota(jnp.int32, (L, 2 * P), 0)).astype(_F32)
    fwd = lax.broadcasted_iota(jnp.int32, (L, 2 * P), 1) < P
    s_re, s_im = outer(cpow(jnp.where(fwd, L - 1 - j, j)), (bb_re, bb_im))
    s_ref[gi] = jnp.concatenate([s_re, s_im], axis=1).astype(_BF16)
    w_re, w_im = outer(cpow(jnp.where(fwd, j + 1, L - j)), (cc_re, cc_im))
    wcat = jnp.concatenate([w_re, -w_im], axis=1)
    c_ref[gi] = wcat.astype(_BF16)
    l_re, l_im = outer(cpow(jnp.where(fwd, -1 - j, j - L)), (bb_re, bb_im))
    lcat = jnp.concatenate([l_re, l_im], axis=1)
    row = lax.broadcasted_iota(jnp.int32, (L * C, L * C), 0)
    col = lax.broadcasted_iota(jnp.int32, (L * C, L * C), 1)
    is_fwd_state = jnp.bitwise_and(col, P) == 0

    def split(a):
        hi = a.astype(_BF16).astype(_F32)
        return hi, a - hi

    def dot_state(a_parts, b_parts):
        (a_hi, a_lo), (b_hi, b_lo) = [[p.astype(_BF16) for p in parts]
                                      for parts in (a_parts, b_parts)]
        d = lambda a, b: lax.dot_general(a, b, (((1,), (1,)), ((), ())),
                                         preferred_element_type=_F32)
        return d(a_hi, b_hi) + d(a_hi, b_lo) + d(a_lo, b_hi)

    l_parts, w_parts = split(lcat), split(wcat)
    mf = dot_state([jnp.where(is_fwd_state, p, 0.0) for p in l_parts], w_parts)
    mb = dot_state([jnp.where(is_fwd_state, 0.0, p) for p in l_parts], w_parts)
    lane_to_slot = C.bit_length() - 1
    jj = time_of_slot(jnp.right_shift(row, lane_to_slot))
    tt = time_of_slot(jnp.right_shift(col, lane_to_slot))
    tm = (jnp.where(tt >= jj, mf, 0.0) + jnp.where(tt <= jj, mb, 0.0)
          + jnp.where(row == col, dvec_ref[gi], 0.0))
    t_ref[gi] = tm.astype(_BF16)
    al_re, al_im = cpow(jnp.full_like(lr, float(L)))
    ar_ref[gi] = al_re
    ai_ref[gi] = al_im


def _attn_kernel(sink_ref, q_ref, k_ref, v_ref, bias_ref, za_ref, o_ref):
    nq = q_ref.shape[1] // BLOCK
    rows = lambda ref, c: ref[0, c * BLOCK:(c + 1) * BLOCK]
    k_blocks = [rows(k_ref, c) for c in range(nq)]
    v_blocks = [rows(v_ref, c) for c in range(nq)]
    lane = lax.broadcasted_iota(jnp.int32, (BLOCK, LANES), 1)
    halves = [lane < HEAD_DIM, lane >= HEAD_DIM]
    keep = [h.astype(_F32).astype(_BF16) for h in halves]
    slab = lambda blk, j, e: blk[:, (j ^ e) * LANES:((j ^ e) + 1) * LANES]
    k_half = {(j, e): [slab(kb, j, e) * keep[e] for kb in k_blocks]
              for j in range(KV_HEADS) for e in range(HEAD_PAIR)}
    v_half = {(j, e): [jnp.concatenate([slab(vb, j, e) * keep[e], keep[e]], axis=1)
                       for vb in v_blocks]
              for j in range(KV_HEADS) for e in range(HEAD_PAIR)}
    for c in range(nq):
        blocks = [b for b in (c - 1, c, c + 1) if 0 <= b < nq]
        slot0 = blocks[0] - (c - 1)
        nk = len(blocks) * BLOCK
        q = rows(q_ref, c)
        slabs_out = []
        for j in range(KV_HEADS):
            window = lambda halves_of: jnp.concatenate(
                [halves_of[j, e][b] for e in range(HEAD_PAIR) for b in blocks], axis=0)
            kcat = window(k_half)
            vcat = window(v_half)
            qs = jnp.concatenate(
                [q[:, (j * PAIRS_PER_KV + i) * LANES:(j * PAIRS_PER_KV + i + 1) * LANES]
                 for i in range(PAIRS_PER_KV)], axis=0)
            s = lax.dot_general(qs, kcat, (((1,), (1,)), ((), ())),
                                preferred_element_type=_F32)
            p_rows, e_rows = [], []
            for i in range(PAIRS_PER_KV):
                p_lanes, e_sink = [], []
                for e in range(HEAD_PAIR):
                    first_key = (e * 3 + slot0) * BLOCK
                    sg = (s[i * BLOCK:(i + 1) * BLOCK, e * nk:(e + 1) * nk]
                          + bias_ref[j, i * BLOCK:(i + 1) * BLOCK, first_key:first_key + nk])
                    sk = sink_ref[j * Q_PER_KV + i * HEAD_PAIR + e]
                    m = jnp.maximum(jnp.max(sg, axis=-1, keepdims=True), sk)
                    p_lanes.append(jnp.exp2(sg - m).astype(_BF16))
                    e_sink.append(jnp.broadcast_to(jnp.exp2(sk - m), (BLOCK, LANES)))
                p_rows.append(jnp.concatenate(p_lanes, axis=1))
                e_rows.append(jnp.where(halves[0], e_sink[0], e_sink[1]))
            p = jnp.concatenate(p_rows, axis=0)
            o = jnp.dot(p, vcat, preferred_element_type=_F32)
            den = o[:, LANES:] + jnp.concatenate(e_rows, axis=0)
            on = o[:, :LANES] / den
            slabs_out += [on[i * BLOCK:(i + 1) * BLOCK] for i in range(PAIRS_PER_KV)]
        o_all = jnp.concatenate(slabs_out, axis=1)
        o_ref[0, c * BLOCK:(c + 1) * BLOCK, :] = (
            o_all * rows(za_ref, c).astype(_F32)).astype(o_ref.dtype)


def _attention(sink, q, k, v, bias, za):
    b, s, _ = q.shape
    seq = lambda w: pl.BlockSpec((1, s, w), lambda i: (i, 0, 0))
    return pl.pallas_call(
        _attn_kernel,
        grid=(b,),
        in_specs=[pl.BlockSpec(memory_space=pltpu.SMEM), seq(D_ATTN), seq(D_KVX), seq(D_KVX),
                  pl.BlockSpec(bias.shape, lambda i: (0, 0, 0)), seq(D_ATTN)],
        out_specs=seq(D_ATTN),
        out_shape=jax.ShapeDtypeStruct((b, s, D_ATTN), _BF16),
        compiler_params=pltpu.CompilerParams(
            dimension_semantics=("arbitrary",), vmem_limit_bytes=VMEM_LIMIT_BYTES),
        name="attn",
    )(sink, q, k, v, bias, za)


def _t5_bucket_np(rel):
    half = NUM_BUCKETS // 2
    ret = (rel > 0).astype(np.int64) * half
    n = np.abs(rel)
    max_exact = half // 2
    nf = np.maximum(n, 1).astype(np.float64)
    large = max_exact + (np.log(nf / max_exact) / math.log(MAX_DISTANCE / max_exact)
                         * (half - max_exact)).astype(np.int64)
    large = np.minimum(large, half - 1)
    return ret + np.where(n < max_exact, n, large)


def _attn_bias(rel_table):
    rel = (np.arange(3 * BLOCK)[None, :] - BLOCK) - np.arange(BLOCK)[:, None]
    onehot = (_t5_bucket_np(rel)[None] == np.arange(NUM_BUCKETS)[:, None, None])
    bias = jnp.einsum('bh,bqk->hqk', LOG2_E * rel_table.astype(_F32),
                      jnp.asarray(onehot, _F32), precision=_HI)
    bias = jnp.where(jnp.asarray(np.abs(rel) <= WINDOW), bias, NEG_INF)
    bias = bias.reshape(KV_HEADS, PAIRS_PER_KV, HEAD_PAIR, BLOCK, 3 * BLOCK)
    bias = bias.transpose(0, 1, 3, 2, 4)
    return bias.reshape(KV_HEADS, PAIRS_PER_KV * BLOCK, HEAD_PAIR * 3 * BLOCK)


def _merge_kernel(x_ref, gain_ref, wg_ref, bg_ref, ao_ref, yg_ref, zs_ref, wglu_ref,
                  bglu_ref, wba_ref, wbs_ref, wout_ref, fgain_ref, o_ref, *y_scrs):
    nb, nt = x_ref.shape[0], x_ref.shape[1]
    rows = nb * SUB_STEPS
    for st in range(nt // SUB_STEPS):
        steps = slice(st * SUB_STEPS, (st + 1) * SUB_STEPS)
        x = x_ref[:, steps, :].reshape(rows, D_MODEL)
        h = _rms_normalize(x, gain_ref[...]).astype(_BF16)
        gates2 = _twice_sigmoid_of_double(
            jnp.dot(h, wg_ref[...], preferred_element_type=_F32) + bg_ref[...])
        for c in range(SUB_STEPS // CHUNK):
            _from_group_major(yg_ref, y_scrs[st], st * (SUB_STEPS // CHUNK) + c, nb, c * CHUNK,
                              SUB_STEPS)
        y = jnp.concatenate([y_scrs[st][v] for v in range(D_SSM // LANES)], axis=1)
        y2 = _twice_gelu(y)
        glu_half = (jnp.dot(y2.astype(_BF16), wglu_ref[...], preferred_element_type=_F32)
                    + bglu_ref[...])
        ssm4 = (y2 * _twice_sigmoid_of_double(glu_half)
                * zs_ref[:, steps, :].reshape(rows, D_SSM).astype(_F32))
        pa = jnp.dot(ao_ref[:, steps, :].reshape(rows, D_ATTN), wba_ref[...],
                     preferred_element_type=_F32)
        ps = jnp.dot(ssm4.astype(_BF16), wbs_ref[...], preferred_element_type=_F32)
        merged2 = gates2[:, :D_MODEL] * pa + gates2[:, D_MODEL:] * ps
        xn = x + jnp.dot(merged2.astype(_BF16), wout_ref[...], preferred_element_type=_F32)
        o_ref[:, steps, :] = _rms_normalize(xn, fgain_ref[...]).reshape(nb, SUB_STEPS, D_MODEL)


def _merge(x, gain, wg, bg, ao, yg, zs, wglu, bglu, wba, wbs, wout, fgain):
    bsz, s, _ = x.shape
    nt = TOKEN_TILE // bsz
    tok = lambda i: (0, i, 0)
    fixed = lambda i: (0, 0)
    full = lambda a: pl.BlockSpec(a.shape, fixed)
    act = lambda n: pl.BlockSpec((bsz, nt, n), tok)
    return pl.pallas_call(
        _merge_kernel,
        grid=(s // nt,),
        in_specs=[act(D_MODEL), full(gain), full(wg), full(bg), act(D_ATTN),
                  pl.BlockSpec((N_SSM_GROUPS, nt // CHUNK, bsz, CHUNK_W),
                               lambda i: (0, i, 0, 0)),
                  act(D_SSM), full(wglu), full(bglu), full(wba), full(wbs), full(wout),
                  full(fgain)],
        out_specs=act(D_MODEL),
        out_shape=jax.ShapeDtypeStruct((bsz, s, D_MODEL), _F32),
        scratch_shapes=[pltpu.VMEM((D_SSM // LANES, bsz * SUB_STEPS, LANES), _F32)
                        for _ in range(nt // SUB_STEPS)],
        compiler_params=pltpu.CompilerParams(
            dimension_semantics=("arbitrary",), vmem_limit_bytes=VMEM_LIMIT_BYTES),
        name="merge",
    )(x, gain, wg, bg, ao, yg, zs, wglu, bglu, wba, wbs, wout, fgain)


def _layer(x, norm_gain, w_in, b_gate, attn_sink, a_re, a_im, log_dt, b_re, b_im, c_re, c_im,
           d_skip, w_glu, b_glu, w_ba, w_bs, w_out, out_gain, bias):
    bsz, s, d = x.shape
    assert d == D_MODEL and x.dtype == _F32 and w_in.shape == (D_MODEL, W_OFF_GATES + 2 * D_MODEL)
    assert bsz % 8 == 0 and s % BLOCK == 0
    assert INPROJ_TOKEN_TILE % (bsz * SUB_STEPS) == 0 and s % (INPROJ_TOKEN_TILE // bsz) == 0
    assert TOKEN_TILE % (bsz * SUB_STEPS) == 0 and s % (TOKEN_TILE // bsz) == 0
    n_chunks = s // CHUNK
    gain = norm_gain.reshape(1, D_MODEL).astype(_F32)
    q, k, v, za, ug, zs = _inproj(x, gain, w_in)

    tmat, smat, cmat, ar, ai = _ssm_matrices(a_re, a_im, log_dt, b_re, b_im, c_re, c_im,
                                             d_skip)
    yg = _ssm(ug.reshape(N_SSM_GROUPS, n_chunks * bsz, CHUNK_W), tmat, smat, cmat, ar, ai,
              n_chunks, bsz)
    yg = yg.reshape(N_SSM_GROUPS, n_chunks, bsz, CHUNK_W)

    ao = _attention(LOG2_E * attn_sink.astype(_F32), q, k, v, bias, za)

    return _merge(x, gain,
                  (0.5 * w_in[:, W_OFF_GATES:]).astype(_BF16),
                  (0.5 * b_gate).reshape(1, -1).astype(_F32),
                  ao, yg, zs,
                  (0.25 * w_glu).astype(_BF16),
                  (0.5 * b_glu).reshape(1, -1).astype(_F32),
                  w_ba.astype(_BF16),
                  (0.25 * w_bs).astype(_BF16),
                  (0.5 * w_out).astype(_BF16),
                  out_gain.reshape(1, D_MODEL).astype(_F32))


def kernel(x, norm_gain, w_in, b_gate, attn_sink, rel_bias_table, ssm_a_re, ssm_a_im,
           ssm_log_dt, ssm_b_re, ssm_b_im, ssm_c_re, ssm_c_im, ssm_d, w_glu, b_glu,
           w_branch_attn, w_branch_ssm, w_out, final_norm_gain):
    depth = norm_gain.shape[0]
    assert depth == 1, "final norm is fused into the single layer's epilogue"
    bias = _attn_bias(rel_bias_table)
    l = 0
    return _layer(x, norm_gain[l], w_in[l], b_gate[l], attn_sink[l], ssm_a_re[l],
                  ssm_a_im[l], ssm_log_dt[l], ssm_b_re[l], ssm_b_im[l], ssm_c_re[l],
                  ssm_c_im[l], ssm_d[l], w_glu[l], b_glu[l], w_branch_attn[l],
                  w_branch_ssm[l], w_out[l], final_norm_gain, bias)
```

```python
import functools
import math

import jax
import jax.numpy as jnp
import numpy as np
from jax import lax
from jax.experimental import pallas as pl
from jax.experimental.pallas import tpu as pltpu

D_MODEL = 1024
ATTN_HEADS = 8
KV_HEADS = 2
Q_PER_KV = ATTN_HEADS // KV_HEADS
HEAD_DIM = 64
D_ATTN = ATTN_HEADS * HEAD_DIM
D_KV = KV_HEADS * HEAD_DIM
WINDOW = 128
BLOCK = 128
NUM_BUCKETS = 32
MAX_DISTANCE = 128
D_SSM = 512
SSM_GROUP = 16
N_SSM_GROUPS = D_SSM // SSM_GROUP
SSM_STATE = 64
EPS = 1e-6
NEG_INF = -1e30

LANES = 128
CHUNK = 16
CHUNK_W = CHUNK * SSM_GROUP
STATE_W = 4 * SSM_STATE
SLOTS_PER_VREG = LANES // SSM_GROUP
SSM_GROUPS_PER_STEP = 4

HEAD_PAIR = LANES // HEAD_DIM
PAIRS_PER_KV = Q_PER_KV // HEAD_PAIR
assert D_KV == LANES and KV_HEADS == HEAD_PAIR
D_KVX = 2 * D_KV
LOG2_E = math.log2(math.e)
Q_SCALE = HEAD_DIM ** -0.5 * LOG2_E

W_OFF_ZA = D_ATTN + 2 * D_KV
W_OFF_U = W_OFF_ZA + D_ATTN
W_OFF_GATES = W_OFF_U + 2 * D_SSM

TOKEN_TILE = 1024
INPROJ_TOKEN_TILE = 2048
SUB_STEPS = CHUNK
VMEM_LIMIT_BYTES = 56 * 1024 * 1024

_F32 = jnp.float32
_BF16 = jnp.bfloat16
_HI = lax.Precision.HIGHEST


def _silu(x):
    return x * (0.5 * jnp.tanh(0.5 * x) + 0.5)


def _twice_sigmoid_of_double(xh):
    return 1.0 + jnp.tanh(xh)


def _twice_gelu(x):
    c = math.sqrt(2.0 / math.pi)
    return x * (1.0 + jnp.tanh(c * (x + 0.044715 * (x * x * x))))


def _rms_normalize(x, gain):
    ms = jnp.mean(x * x, axis=-1, keepdims=True)
    return x * lax.rsqrt(ms + EPS) * gain


def _slot_masks(rows):
    lane = lax.broadcasted_iota(jnp.int32, (rows, LANES), 1)
    return [(lane >= p * SSM_GROUP) & (lane < (p + 1) * SSM_GROUP)
            for p in range(SLOTS_PER_VREG)]


def _to_group_major(u_scr, ug_ref, kk, nb, t0, steps):
    masks = _slot_masks(nb)
    for v in range(D_SSM // LANES):
        for half in range(CHUNK // SLOTS_PER_VREG):
            rolled = []
            for t8 in range(SLOTS_PER_VREG):
                r = t0 + half * SLOTS_PER_VREG + t8
                piece = u_scr[v, pl.ds(r, nb, stride=steps), :]
                rolled.append(pltpu.roll(piece, t8 * SSM_GROUP, 1) if t8 else piece)
            for p0 in range(SLOTS_PER_VREG):
                acc = rolled[0]
                for t8 in range(1, SLOTS_PER_VREG):
                    acc = jnp.where(masks[(p0 + t8) % SLOTS_PER_VREG], rolled[t8], acc)
                g = v * SLOTS_PER_VREG + p0
                ug_ref[g, kk, :, half * LANES:(half + 1) * LANES] = acc.astype(_BF16)


def _from_group_major(yg_ref, y_scr, kk, nb, t0, steps):
    masks = _slot_masks(nb)
    for v in range(D_SSM // LANES):
        for half in range(CHUNK // SLOTS_PER_VREG):
            src = [yg_ref[v * SLOTS_PER_VREG + p0, kk, :,
                          half * LANES:(half + 1) * LANES].astype(_F32)
                   for p0 in range(SLOTS_PER_VREG)]
            for t8 in range(SLOTS_PER_VREG):
                acc = src[0]
                for p0 in range(1, SLOTS_PER_VREG):
                    acc = jnp.where(masks[(p0 + t8) % SLOTS_PER_VREG], src[p0], acc)
                if t8:
                    acc = pltpu.roll(acc, LANES - t8 * SSM_GROUP, 1)
                r = t0 + half * SLOTS_PER_VREG + t8
                y_scr[v, pl.ds(r, nb, stride=steps), :] = acc


def _inproj_kernel(x_ref, gain_ref, w_ref, q_ref, k_ref, v_ref, za_ref, ug_ref, zs_ref,
                   *u_scrs):
    nb, nt = x_ref.shape[0], x_ref.shape[1]
    rows = nb * SUB_STEPS
    for st in range(nt // SUB_STEPS):
        steps = slice(st * SUB_STEPS, (st + 1) * SUB_STEPS)
        x = x_ref[:, steps, :].reshape(rows, D_MODEL)
        h = _rms_normalize(x, gain_ref[...]).astype(_BF16)

        def put(ref, val):
            ref[:, steps, :] = val.astype(_BF16).reshape(nb, SUB_STEPS, val.shape[-1])

        proj = lambda lo, n: jnp.dot(h, w_ref[:, lo:lo + n], preferred_element_type=_F32)
        u_zs = proj(W_OFF_U, 2 * D_SSM)
        for v in range(D_SSM // LANES):
            u_scrs[st][v] = u_zs[:, v * LANES:(v + 1) * LANES]
        for c in range(SUB_STEPS // CHUNK):
            _to_group_major(u_scrs[st], ug_ref, st * (SUB_STEPS // CHUNK) + c, nb, c * CHUNK,
                            SUB_STEPS)
        put(zs_ref, _silu(u_zs[:, D_SSM:]))
        put(za_ref, _silu(proj(W_OFF_ZA, D_ATTN)))
        qkv = proj(0, W_OFF_ZA)
        put(q_ref, qkv[:, :D_ATTN] * Q_SCALE)
        with_swap = lambda a: jnp.concatenate([a, pltpu.roll(a, HEAD_DIM, 1)], axis=1)
        put(k_ref, with_swap(qkv[:, D_ATTN:D_ATTN + D_KV]))
        put(v_ref, with_swap(qkv[:, D_ATTN + D_KV:]))


def _inproj(x, gain, w_in):
    bsz, s, _ = x.shape
    nt = INPROJ_TOKEN_TILE // bsz
    tok = lambda i: (0, i, 0)
    fixed = lambda i: (0, 0)
    w_proj = w_in[:, :W_OFF_GATES].astype(_BF16)
    outs = [D_ATTN, D_KVX, D_KVX, D_ATTN, None, D_SSM]
    act = lambda n: (pl.BlockSpec((bsz, nt, n), tok), jax.ShapeDtypeStruct((bsz, s, n), _BF16))
    ug = (pl.BlockSpec((N_SSM_GROUPS, nt // CHUNK, bsz, CHUNK_W), lambda i: (0, i, 0, 0)),
          jax.ShapeDtypeStruct((N_SSM_GROUPS, s // CHUNK, bsz, CHUNK_W), _BF16))
    specs, shapes = zip(*[ug if n is None else act(n) for n in outs])
    return pl.pallas_call(
        _inproj_kernel,
        grid=(s // nt,),
        in_specs=[pl.BlockSpec((bsz, nt, D_MODEL), tok),
                  pl.BlockSpec((1, D_MODEL), fixed)]
                 + [pl.BlockSpec(w_proj.shape, fixed)],
        out_specs=list(specs),
        out_shape=list(shapes),
        scratch_shapes=[pltpu.VMEM((D_SSM // LANES, bsz * SUB_STEPS, LANES), _F32)
                        for _ in range(nt // SUB_STEPS)],
        compiler_params=pltpu.CompilerParams(
            dimension_semantics=("arbitrary",), vmem_limit_bytes=VMEM_LIMIT_BYTES),
        name="inproj",
    )(x, gain, w_proj)


def _ssm_kernel(n_chunks, rows_per_chunk, u_ref, t_ref, s_ref, c_ref, ar_ref, ai_ref,
                y_ref, s_scr, x_scr):
    rb = rows_per_chunk
    groups = range(u_ref.shape[0])
    for gi in groups:
        s_scr[gi] = jnp.dot(u_ref[gi], s_ref[gi], preferred_element_type=_F32)
    ar = [jnp.broadcast_to(ar_ref[gi], (rb, 2 * SSM_STATE)) for gi in groups]
    ai = [jnp.broadcast_to(ai_ref[gi], (rb, 2 * SSM_STATE)) for gi in groups]
    is_fwd = lax.broadcasted_iota(jnp.int32, (rb, 2 * SSM_STATE), 1) < SSM_STATE
    is_bwd = jnp.logical_not(is_fwd)
    re_cols = pl.ds(0, 2 * SSM_STATE)
    im_cols = pl.ds(2 * SSM_STATE, 2 * SSM_STATE)

    def step(i, carry):
        rows_f = pl.ds(pl.multiple_of(i * rb, rb), rb)
        rows_b = pl.ds(pl.multiple_of((n_chunks - 1 - i) * rb, rb), rb)
        new = []
        for gi in groups:
            st_re, st_im = carry[2 * gi], carry[2 * gi + 1]
            pltpu.store(x_scr.at[gi, rows_f, re_cols], st_re, mask=is_fwd)
            pltpu.store(x_scr.at[gi, rows_f, im_cols], st_im, mask=is_fwd)
            pltpu.store(x_scr.at[gi, rows_b, re_cols], st_re, mask=is_bwd)
            pltpu.store(x_scr.at[gi, rows_b, im_cols], st_im, mask=is_bwd)
            in_re = jnp.where(is_fwd, s_scr[gi, rows_f, re_cols], s_scr[gi, rows_b, re_cols])
            in_im = jnp.where(is_fwd, s_scr[gi, rows_f, im_cols], s_scr[gi, rows_b, im_cols])
            new.append(ar[gi] * st_re - ai[gi] * st_im + in_re)
            new.append(ar[gi] * st_im + ai[gi] * st_re + in_im)
        return tuple(new)

    zero = jnp.zeros((rb, 2 * SSM_STATE), _F32)
    lax.fori_loop(0, n_chunks, step, (zero,) * (2 * len(groups)), unroll=True)

    for gi in groups:
        y = jnp.dot(u_ref[gi], t_ref[gi], preferred_element_type=_F32)
        y = y + lax.dot_general(x_scr[gi].astype(_BF16), c_ref[gi], (((1,), (1,)), ((), ())),
                                preferred_element_type=_F32)
        y_ref[gi] = y.astype(y_ref.dtype)


def _ssm(ug, tmat, smat, cmat, ar, ai, n_chunks, rows_per_chunk):
    g, rows, _ = ug.shape
    gb = SSM_GROUPS_PER_STEP
    per_group = lambda i: (i, 0, 0)
    return pl.pallas_call(
        functools.partial(_ssm_kernel, n_chunks, rows_per_chunk),
        grid=(g // gb,),
        in_specs=[pl.BlockSpec((gb, rows, CHUNK_W), per_group),
                  pl.BlockSpec((gb, CHUNK_W, CHUNK_W), per_group),
                  pl.BlockSpec((gb, CHUNK_W, STATE_W), per_group),
                  pl.BlockSpec((gb, CHUNK_W, STATE_W), per_group),
                  pl.BlockSpec((gb, 1, 2 * SSM_STATE), per_group),
                  pl.BlockSpec((gb, 1, 2 * SSM_STATE), per_group)],
        out_specs=pl.BlockSpec((gb, rows, CHUNK_W), per_group),
        out_shape=jax.ShapeDtypeStruct((g, rows, CHUNK_W), _BF16),
        scratch_shapes=[pltpu.VMEM((gb, rows, STATE_W), _F32),
                        pltpu.VMEM((gb, rows, STATE_W), _F32)],
        compiler_params=pltpu.CompilerParams(
            dimension_semantics=("arbitrary",), vmem_limit_bytes=VMEM_LIMIT_BYTES),
        name="ssm",
    )(ug, tmat, smat, cmat, ar, ai)


def _ssm_matrices(a_re, a_im, log_dt, b_re, b_im, c_re, c_im, d_skip):
    L, G, P, C = CHUNK, N_SSM_GROUPS, SSM_STATE, SSM_GROUP
    both = lambda a: jnp.concatenate([a[0], a[1]], axis=-1)
    dt = jnp.broadcast_to(jnp.exp(log_dt)[..., None], a_re.shape)
    lam = jnp.stack([both(a_re), both(a_im), both(dt)], axis=1)
    bt = jnp.stack([b_re, b_im], axis=0).transpose(2, 0, 4, 1, 3).reshape(G, 2, C, 2 * P)
    ct = jnp.stack([c_re, c_im], axis=0).transpose(2, 0, 3, 1, 4).reshape(G, 2, C, 2 * P)
    dvec = jnp.tile(d_skip.reshape(G, 1, C), (1, 1, L))
    gb = SLOTS_PER_VREG
    per_group = lambda *blk: pl.BlockSpec((gb,) + blk, lambda i: (i,) + (0,) * len(blk))
    mat = jax.ShapeDtypeStruct((G, L * C, L * C), _BF16)
    vec = jax.ShapeDtypeStruct((G, 1, 2 * P), _F32)
    return pl.pallas_call(
        _ssm_prep_kernel,
        grid=(G // gb,),
        in_specs=[per_group(3, 2 * P), per_group(2, C, 2 * P), per_group(2, C, 2 * P),
                  per_group(1, L * C)],
        out_specs=[per_group(L * C, L * C), per_group(L * C, STATE_W),
                   per_group(L * C, STATE_W), per_group(1, 2 * P), per_group(1, 2 * P)],
        out_shape=[mat, mat, mat, vec, vec],
        compiler_params=pltpu.CompilerParams(dimension_semantics=("arbitrary",)),
        name="ssm_prep",
    )(lam, bt, ct, dvec)


def _ssm_prep_kernel(lam_ref, bt_ref, ct_ref, dvec_ref, t_ref, s_ref, c_ref, ar_ref, ai_ref):
    for gi in range(lam_ref.shape[0]):
        _ssm_prep_group(gi, lam_ref, bt_ref, ct_ref, dvec_ref, t_ref, s_ref, c_ref, ar_ref,
                        ai_ref)


def _ssm_prep_group(gi, lam_ref, bt_ref, ct_ref, dvec_ref, t_ref, s_ref, c_ref, ar_ref,
                    ai_ref):
    L, P, C = CHUNK, SSM_STATE, SSM_GROUP
    g8 = gi % SLOTS_PER_VREG

    def time_of_slot(slot):
        return (jnp.bitwise_and(slot, -SLOTS_PER_VREG)
                + jnp.bitwise_and(slot - g8, SLOTS_PER_VREG - 1))

    def cmul(x_re, x_im, y_re, y_im):
        return x_re * y_re - x_im * y_im, x_re * y_im + x_im * y_re

    a_re, a_im, dt = lam_ref[gi, 0:1], lam_ref[gi, 1:2], lam_ref[gi, 2:3]
    lr, li = a_re * dt, a_im * dt

    def cpow(tau):
        mag, ang = jnp.exp(lr * tau), li * tau
        return mag * jnp.cos(ang), mag * jnp.sin(ang)

    ab_re, ab_im = cpow(jnp.ones_like(lr))
    den = a_re * a_re + a_im * a_im
    co_re = ((ab_re - 1.0) * a_re + ab_im * a_im) / den
    co_im = (ab_im * a_re - (ab_re - 1.0) * a_im) / den
    bb_re, bb_im = cmul(bt_ref[gi, 0], bt_ref[gi, 1], co_re, co_im)
    cc_re, cc_im = ct_ref[gi, 0], ct_ref[gi, 1]

    def outer(p, m):
        (p_re, p_im), (m_re, m_im) = p, m
        blocks = [cmul(p_re[s:s + 1], p_im[s:s + 1], m_re, m_im) for s in range(L)]
        return (jnp.concatenate([b[0] for b in blocks], axis=0),
                jnp.concatenate([b[1] for b in blocks], axis=0))

    j = time_of_slot(lax.broadcasted_iota(jnp.int32, (L, 2 * P), 0)).astype(_F32)
    fwd = lax.broadcasted_iota(jnp.int32, (L, 2 * P), 1) < P
    s_re, s_im = outer(cpow(jnp.where(fwd, L - 1 - j, j)), (bb_re, bb_im))
    s_ref[gi] = jnp.concatenate([s_re, s_im], axis=1).astype(_BF16)
    w_re, w_im = outer(cpow(jnp.where(fwd, j + 1, L - j)), (cc_re, cc_im))
    wcat = jnp.concatenate([w_re, -w_im], axis=1)
    c_ref[gi] = wcat.astype(_BF16)
    l_re, l_im = outer(cpow(jnp.where(fwd, -1 - j, j - L)), (bb_re, bb_im))
    lcat = jnp.concatenate([l_re, l_im], axis=1)
    row = lax.broadcasted_iota(jnp.int32, (L * C, L * C), 0)
    col = lax.broadcasted_iota(jnp.int32, (L * C, L * C), 1)
    is_fwd_state = jnp.bitwise_and(col, P) == 0

    def split(a):
        hi = a.astype(_BF16).astype(_F32)
        return hi, a - hi

    def dot_state(a_parts, b_parts):
        (a_hi, a_lo), (b_hi, b_lo) = [[p.astype(_BF16) for p in parts]
                                      for parts in (a_parts, b_parts)]
        d = lambda a, b: lax.dot_general(a, b, (((1,), (1,)), ((), ())),
                                         preferred_element_type=_F32)
        return d(a_hi, b_hi) + d(a_hi, b_lo) + d(a_lo, b_hi)

    l_parts, w_parts = split(lcat), split(wcat)
    mf = dot_state([jnp.where(is_fwd_state, p, 0.0) for p in l_parts], w_parts)
    mb = dot_state([jnp.where(is_fwd_state, 0.0, p) for p in l_parts], w_parts)
    lane_to_slot = C.bit_length() - 1
    jj = time_of_slot(jnp.right_shift(row, lane_to_slot))
    tt = time_of_slot(jnp.right_shift(col, lane_to_slot))
    tm = (jnp.where(tt >= jj, mf, 0.0) + jnp.where(tt <= jj, mb, 0.0)
          + jnp.where(row == col, dvec_ref[gi], 0.0))
    t_ref[gi] = tm.astype(_BF16)
    al_re, al_im = cpow(jnp.full_like(lr, float(L)))
    ar_ref[gi] = al_re
    ai_ref[gi] = al_im


def _attn_kernel(sink_ref, q_ref, k_ref, v_ref, bias_ref, za_ref, o_ref):
    nq = q_ref.shape[1] // BLOCK
    rows = lambda ref, c: ref[0, c * BLOCK:(c + 1) * BLOCK]
    k_blocks = [rows(k_ref, c) for c in range(nq)]
    v_blocks = [rows(v_ref, c) for c in range(nq)]
    lane = lax.broadcasted_iota(jnp.int32, (BLOCK, LANES), 1)
    halves = [lane < HEAD_DIM, lane >= HEAD_DIM]
    keep = [h.astype(_F32).astype(_BF16) for h in halves]
    slab = lambda blk, j, e: blk[:, (j ^ e) * LANES:((j ^ e) + 1) * LANES]
    k_half = {(j, e): [slab(kb, j, e) * keep[e] for kb in k_blocks]
              for j in range(KV_HEADS) for e in range(HEAD_PAIR)}
    v_half = {(j, e): [jnp.concatenate([slab(vb, j, e) * keep[e], keep[e]], axis=1)
                       for vb in v_blocks]
              for j in range(KV_HEADS) for e in range(HEAD_PAIR)}
    for c in range(nq):
        blocks = [b for b in (c - 1, c, c + 1) if 0 <= b < nq]
        slot0 = blocks[0] - (c - 1)
        nk = len(blocks) * BLOCK
        q = rows(q_ref, c)
        slabs_out = []
        for j in range(KV_HEADS):
            window = lambda halves_of: jnp.concatenate(
                [halves_of[j, e][b] for e in range(HEAD_PAIR) for b in blocks], axis=0)
            kcat = window(k_half)
            vcat = window(v_half)
            qs = jnp.concatenate(
                [q[:, (j * PAIRS_PER_KV + i) * LANES:(j * PAIRS_PER_KV + i + 1) * LANES]
                 for i in range(PAIRS_PER_KV)], axis=0)
            s = lax.dot_general(qs, kcat, (((1,), (1,)), ((), ())),
                                preferred_element_type=_F32)
            p_rows, e_rows = [], []
            for i in range(PAIRS_PER_KV):
                p_lanes, e_sink = [], []
                for e in range(HEAD_PAIR):
                    head = j * Q_PER_KV + i * HEAD_PAIR + e
                    sg = (s[i * BLOCK:(i + 1) * BLOCK, e * nk:(e + 1) * nk]
                          + bias_ref[head, :, slot0 * BLOCK:slot0 * BLOCK + nk])
                    sk = sink_ref[head]
                    m = jnp.maximum(jnp.max(sg, axis=-1, keepdims=True), sk)
                    p_lanes.append(jnp.exp2(sg - m).astype(_BF16))
                    e_sink.append(jnp.broadcast_to(jnp.exp2(sk - m), (BLOCK, LANES)))
                p_rows.append(jnp.concatenate(p_lanes, axis=1))
                e_rows.append(jnp.where(halves[0], e_sink[0], e_sink[1]))
            p = jnp.concatenate(p_rows, axis=0)
            o = jnp.dot(p, vcat, preferred_element_type=_F32)
            den = o[:, LANES:] + jnp.concatenate(e_rows, axis=0)
            on = o[:, :LANES] / den
            slabs_out += [on[i * BLOCK:(i + 1) * BLOCK] for i in range(PAIRS_PER_KV)]
        o_all = jnp.concatenate(slabs_out, axis=1)
        o_ref[0, c * BLOCK:(c + 1) * BLOCK, :] = (
            o_all * rows(za_ref, c).astype(_F32)).astype(o_ref.dtype)


def _attention(sink, q, k, v, bias, za):
    b, s, _ = q.shape
    seq = lambda w: pl.BlockSpec((1, s, w), lambda i: (i, 0, 0))
    return pl.pallas_call(
        _attn_kernel,
        grid=(b,),
        in_specs=[pl.BlockSpec(memory_space=pltpu.SMEM), seq(D_ATTN), seq(D_KVX), seq(D_KVX),
                  pl.BlockSpec(bias.shape, lambda i: (0, 0, 0)), seq(D_ATTN)],
        out_specs=seq(D_ATTN),
        out_shape=jax.ShapeDtypeStruct((b, s, D_ATTN), _BF16),
        compiler_params=pltpu.CompilerParams(
            dimension_semantics=("arbitrary",), vmem_limit_bytes=VMEM_LIMIT_BYTES),
        name="attn",
    )(sink, q, k, v, bias, za)


def _t5_bucket_np(rel):
    half = NUM_BUCKETS // 2
    ret = (rel > 0).astype(np.int64) * half
    n = np.abs(rel)
    max_exact = half // 2
    nf = np.maximum(n, 1).astype(np.float64)
    large = max_exact + (np.log(nf / max_exact) / math.log(MAX_DISTANCE / max_exact)
                         * (half - max_exact)).astype(np.int64)
    large = np.minimum(large, half - 1)
    return ret + np.where(n < max_exact, n, large)


def _attn_bias(rel_table):
    rel = (np.arange(3 * BLOCK)[None, :] - BLOCK) - np.arange(BLOCK)[:, None]
    onehot = (_t5_bucket_np(rel)[None] == np.arange(NUM_BUCKETS)[:, None, None])
    bias = jnp.einsum('bh,bqk->hqk', LOG2_E * rel_table.astype(_F32),
                      jnp.asarray(onehot, _F32), precision=_HI)
    return jnp.where(jnp.asarray(np.abs(rel) <= WINDOW), bias, NEG_INF)


def _merge_kernel(x_ref, gain_ref, wg_ref, bg_ref, ao_ref, yg_ref, zs_ref, wglu_ref,
                  bglu_ref, wba_ref, wbs_ref, wout_ref, fgain_ref, o_ref, *y_scrs):
    nb, nt = x_ref.shape[0], x_ref.shape[1]
    rows = nb * SUB_STEPS
    for st in range(nt // SUB_STEPS):
        steps = slice(st * SUB_STEPS, (st + 1) * SUB_STEPS)
        x = x_ref[:, steps, :].reshape(rows, D_MODEL)
        h = _rms_normalize(x, gain_ref[...]).astype(_BF16)
        gates2 = _twice_sigmoid_of_double(
            jnp.dot(h, wg_ref[...], preferred_element_type=_F32) + bg_ref[...])
        for c in range(SUB_STEPS // CHUNK):
            _from_group_major(yg_ref, y_scrs[st], st * (SUB_STEPS // CHUNK) + c, nb, c * CHUNK,
                              SUB_STEPS)
        y = jnp.concatenate([y_scrs[st][v] for v in range(D_SSM // LANES)], axis=1)
        y2 = _twice_gelu(y)
        glu_half = (jnp.dot(y2.astype(_BF16), wglu_ref[...], preferred_element_type=_F32)
                    + bglu_ref[...])
        ssm4 = (y2 * _twice_sigmoid_of_double(glu_half)
                * zs_ref[:, steps, :].reshape(rows, D_SSM).astype(_F32))
        pa = jnp.dot(ao_ref[:, steps, :].reshape(rows, D_ATTN), wba_ref[...],
                     preferred_element_type=_F32)
        ps = jnp.dot(ssm4.astype(_BF16), wbs_ref[...], preferred_element_type=_F32)
        merged2 = gates2[:, :D_MODEL] * pa + gates2[:, D_MODEL:] * ps
        xn = x + jnp.dot(merged2.astype(_BF16), wout_ref[...], preferred_element_type=_F32)
        o_ref[:, steps, :] = _rms_normalize(xn, fgain_ref[...]).reshape(nb, SUB_STEPS, D_MODEL)


def _merge(x, gain, wg, bg, ao, yg, zs, wglu, bglu, wba, wbs, wout, fgain):
    bsz, s, _ = x.shape
    nt = TOKEN_TILE // bsz
    tok = lambda i: (0, i, 0)
    fixed = lambda i: (0, 0)
    full = lambda a: pl.BlockSpec(a.shape, fixed)
    act = lambda n: pl.BlockSpec((bsz, nt, n), tok)
    return pl.pallas_call(
        _merge_kernel,
        grid=(s // nt,),
        in_specs=[act(D_MODEL), full(gain), full(wg), full(bg), act(D_ATTN),
                  pl.BlockSpec((N_SSM_GROUPS, nt // CHUNK, bsz, CHUNK_W),
                               lambda i: (0, i, 0, 0)),
                  act(D_SSM), full(wglu), full(bglu), full(wba), full(wbs), full(wout),
                  full(fgain)],
        out_specs=act(D_MODEL),
        out_shape=jax.ShapeDtypeStruct((bsz, s, D_MODEL), _F32),
        scratch_shapes=[pltpu.VMEM((D_SSM // LANES, bsz * SUB_STEPS, LANES), _F32)
                        for _ in range(nt // SUB_STEPS)],
        compiler_params=pltpu.CompilerParams(
            dimension_semantics=("arbitrary",), vmem_limit_bytes=VMEM_LIMIT_BYTES),
        name="merge",
    )(x, gain, wg, bg, ao, yg, zs, wglu, bglu, wba, wbs, wout, fgain)


def _layer(x, norm_gain, w_in, b_gate, attn_sink, a_re, a_im, log_dt, b_re, b_im, c_re, c_im,
           d_skip, w_glu, b_glu, w_ba, w_bs, w_out, out_gain, bias):
    bsz, s, d = x.shape
    assert d == D_MODEL and x.dtype == _F32 and w_in.shape == (D_MODEL, W_OFF_GATES + 2 * D_MODEL)
    assert bsz % 8 == 0 and s % BLOCK == 0
    assert INPROJ_TOKEN_TILE % (bsz * SUB_STEPS) == 0 and s % (INPROJ_TOKEN_TILE // bsz) == 0
    assert TOKEN_TILE % (bsz * SUB_STEPS) == 0 and s % (TOKEN_TILE // bsz) == 0
    n_chunks = s // CHUNK
    gain = norm_gain.reshape(1, D_MODEL).astype(_F32)
    q, k, v, za, ug, zs = _inproj(x, gain, w_in)

    tmat, smat, cmat, ar, ai = _ssm_matrices(a_re, a_im, log_dt, b_re, b_im, c_re, c_im,
                                             d_skip)
    yg = _ssm(ug.reshape(N_SSM_GROUPS, n_chunks * bsz, CHUNK_W), tmat, smat, cmat, ar, ai,
              n_chunks, bsz)
    yg = yg.reshape(N_SSM_GROUPS, n_chunks, bsz, CHUNK_W)

    ao = _attention(LOG2_E * attn_sink.astype(_F32), q, k, v, bias, za)

    return _merge(x, 0.5 * gain,
                  w_in[:, W_OFF_GATES:].astype(_BF16),
                  (0.5 * b_gate).reshape(1, -1).astype(_F32),
                  ao, yg, zs,
                  (0.25 * w_glu).astype(_BF16),
                  (0.5 * b_glu).reshape(1, -1).astype(_F32),
                  w_ba.astype(_BF16),
                  (0.25 * w_bs).astype(_BF16),
                  (0.5 * w_out).astype(_BF16),
                  out_gain.reshape(1, D_MODEL).astype(_F32))


def kernel(x, norm_gain, w_in, b_gate, attn_sink, rel_bias_table, ssm_a_re, ssm_a_im,
           ssm_log_dt, ssm_b_re, ssm_b_im, ssm_c_re, ssm_c_im, ssm_d, w_glu, b_glu,
           w_branch_attn, w_branch_ssm, w_out, final_norm_gain):
    depth = norm_gain.shape[0]
    assert depth == 1, "final norm is fused into the single layer's epilogue"
    bias = _attn_bias(rel_bias_table)
    l = 0
    return _layer(x, norm_gain[l], w_in[l], b_gate[l], attn_sink[l], ssm_a_re[l],
                  ssm_a_im[l], ssm_log_dt[l], ssm_b_re[l], ssm_b_im[l], ssm_c_re[l],
                  ssm_c_im[l], ssm_d[l], w_glu[l], b_glu[l], w_branch_attn[l],
                  w_branch_ssm[l], w_out[l], final_norm_gain, bias)
```

```python
import functools
import math

import jax
import jax.numpy as jnp
import numpy as np
from jax import lax
from jax.experimental import pallas as pl
from jax.experimental.pallas import tpu as pltpu

D_MODEL = 1024
ATTN_HEADS = 8
KV_HEADS = 2
Q_PER_KV = ATTN_HEADS // KV_HEADS
HEAD_DIM = 64
D_ATTN = ATTN_HEADS * HEAD_DIM
D_KV = KV_HEADS * HEAD_DIM
WINDOW = 128
BLOCK = 128
NUM_BUCKETS = 32
MAX_DISTANCE = 128
D_SSM = 512
SSM_GROUP = 16
N_SSM_GROUPS = D_SSM // SSM_GROUP
SSM_STATE = 64
EPS = 1e-6
NEG_INF = -1e30

LANES = 128
CHUNK = 16
CHUNK_W = CHUNK * SSM_GROUP
STATE_W = 4 * SSM_STATE
SLOTS_PER_VREG = LANES // SSM_GROUP
SSM_GROUPS_PER_STEP = 4

HEAD_PAIR = LANES // HEAD_DIM
PAIRS_PER_KV = Q_PER_KV // HEAD_PAIR
assert D_KV == LANES and KV_HEADS == HEAD_PAIR
D_KVX = 2 * D_KV
LOG2_E = math.log2(math.e)
Q_SCALE = HEAD_DIM ** -0.5 * LOG2_E

W_OFF_ZA = D_ATTN + 2 * D_KV
W_OFF_U = W_OFF_ZA + D_ATTN
W_OFF_GATES = W_OFF_U + 2 * D_SSM

TOKEN_TILE = 1024
INPROJ_TOKEN_TILE = 2048
SUB_STEPS = CHUNK
VMEM_LIMIT_BYTES = 56 * 1024 * 1024

_F32 = jnp.float32
_BF16 = jnp.bfloat16
_HI = lax.Precision.HIGHEST


def _silu(x):
    return x * (0.5 * jnp.tanh(0.5 * x) + 0.5)


def _twice_sigmoid_of_double(xh):
    return 1.0 + jnp.tanh(xh)


def _twice_gelu(x):
    c = math.sqrt(2.0 / math.pi)
    return x * (1.0 + jnp.tanh(c * (x + 0.044715 * (x * x * x))))


def _rms_normalize(x, gain):
    ms = jnp.mean(x * x, axis=-1, keepdims=True)
    return x * lax.rsqrt(ms + EPS) * gain


def _slot_masks(rows):
    lane = lax.broadcasted_iota(jnp.int32, (rows, LANES), 1)
    return [(lane >= p * SSM_GROUP) & (lane < (p + 1) * SSM_GROUP)
            for p in range(SLOTS_PER_VREG)]


def _to_group_major(u_scr, ug_ref, kk, nb, t0, steps):
    masks = _slot_masks(nb)
    for v in range(D_SSM // LANES):
        for half in range(CHUNK // SLOTS_PER_VREG):
            rolled = []
            for t8 in range(SLOTS_PER_VREG):
                r = t0 + half * SLOTS_PER_VREG + t8
                piece = u_scr[v, pl.ds(r, nb, stride=steps), :]
                rolled.append(pltpu.roll(piece, t8 * SSM_GROUP, 1) if t8 else piece)
            for p0 in range(SLOTS_PER_VREG):
                acc = rolled[0]
                for t8 in range(1, SLOTS_PER_VREG):
                    acc = jnp.where(masks[(p0 + t8) % SLOTS_PER_VREG], rolled[t8], acc)
                g = v * SLOTS_PER_VREG + p0
                ug_ref[g, kk, :, half * LANES:(half + 1) * LANES] = acc.astype(_BF16)


def _from_group_major(yg_ref, y_scr, kk, nb, t0, steps):
    masks = _slot_masks(nb)
    for v in range(D_SSM // LANES):
        for half in range(CHUNK // SLOTS_PER_VREG):
            src = [yg_ref[v * SLOTS_PER_VREG + p0, kk, :,
                          half * LANES:(half + 1) * LANES].astype(_F32)
                   for p0 in range(SLOTS_PER_VREG)]
            for t8 in range(SLOTS_PER_VREG):
                acc = src[0]
                for p0 in range(1, SLOTS_PER_VREG):
                    acc = jnp.where(masks[(p0 + t8) % SLOTS_PER_VREG], src[p0], acc)
                if t8:
                    acc = pltpu.roll(acc, LANES - t8 * SSM_GROUP, 1)
                r = t0 + half * SLOTS_PER_VREG + t8
                y_scr[v, pl.ds(r, nb, stride=steps), :] = acc


def _inproj_kernel(x_ref, gain_ref, w_ref, q_ref, k_ref, v_ref, za_ref, ug_ref, zs_ref,
                   *u_scrs):
    nb, nt = x_ref.shape[0], x_ref.shape[1]
    rows = nb * SUB_STEPS
    for st in range(nt // SUB_STEPS):
        steps = slice(st * SUB_STEPS, (st + 1) * SUB_STEPS)
        x = x_ref[:, steps, :].reshape(rows, D_MODEL)
        h = _rms_normalize(x, gain_ref[...]).astype(_BF16)

        def put(ref, val):
            ref[:, steps, :] = val.astype(_BF16).reshape(nb, SUB_STEPS, val.shape[-1])

        proj = lambda lo, n: jnp.dot(h, w_ref[:, lo:lo + n], preferred_element_type=_F32)
        u_zs = proj(W_OFF_U, 2 * D_SSM)
        for v in range(D_SSM // LANES):
            u_scrs[st][v] = u_zs[:, v * LANES:(v + 1) * LANES]
        for c in range(SUB_STEPS // CHUNK):
            _to_group_major(u_scrs[st], ug_ref, st * (SUB_STEPS // CHUNK) + c, nb, c * CHUNK,
                            SUB_STEPS)
        put(zs_ref, _silu(u_zs[:, D_SSM:]))
        put(za_ref, _silu(proj(W_OFF_ZA, D_ATTN)))
        qkv = proj(0, W_OFF_ZA)
        put(q_ref, qkv[:, :D_ATTN] * Q_SCALE)
        with_swap = lambda a: jnp.concatenate([a, pltpu.roll(a, HEAD_DIM, 1)], axis=1)
        put(k_ref, with_swap(qkv[:, D_ATTN:D_ATTN + D_KV]))
        put(v_ref, with_swap(qkv[:, D_ATTN + D_KV:]))


def _inproj(x, gain, w_in):
    bsz, s, _ = x.shape
    nt = INPROJ_TOKEN_TILE // bsz
    tok = lambda i: (0, i, 0)
    fixed = lambda i: (0, 0)
    w_proj = w_in[:, :W_OFF_GATES].astype(_BF16)
    outs = [D_ATTN, D_KVX, D_KVX, D_ATTN, None, D_SSM]
    act = lambda n: (pl.BlockSpec((bsz, nt, n), tok), jax.ShapeDtypeStruct((bsz, s, n), _BF16))
    ug = (pl.BlockSpec((N_SSM_GROUPS, nt // CHUNK, bsz, CHUNK_W), lambda i: (0, i, 0, 0)),
          jax.ShapeDtypeStruct((N_SSM_GROUPS, s // CHUNK, bsz, CHUNK_W), _BF16))
    specs, shapes = zip(*[ug if n is None else act(n) for n in outs])
    return pl.pallas_call(
        _inproj_kernel,
        grid=(s // nt,),
        in_specs=[pl.BlockSpec((bsz, nt, D_MODEL), tok),
                  pl.BlockSpec((1, D_MODEL), fixed)]
                 + [pl.BlockSpec(w_proj.shape, fixed)],
        out_specs=list(specs),
        out_shape=list(shapes),
        scratch_shapes=[pltpu.VMEM((D_SSM // LANES, bsz * SUB_STEPS, LANES), _F32)
                        for _ in range(nt // SUB_STEPS)],
        compiler_params=pltpu.CompilerParams(
            dimension_semantics=("arbitrary",), vmem_limit_bytes=VMEM_LIMIT_BYTES),
        name="inproj",
    )(x, gain, w_proj)


def _ssm_kernel(n_chunks, rows_per_chunk, u_ref, t_ref, s_ref, c_ref, ar_ref, ai_ref,
                y_ref, s_scr, x_scr):
    rb = rows_per_chunk
    groups = range(u_ref.shape[0])
    for gi in groups:
        s_scr[gi] = jnp.dot(u_ref[gi], s_ref[gi], preferred_element_type=_F32)
    ar = [jnp.broadcast_to(ar_ref[gi], (rb, 2 * SSM_STATE)) for gi in groups]
    ai = [jnp.broadcast_to(ai_ref[gi], (rb, 2 * SSM_STATE)) for gi in groups]
    is_fwd = lax.broadcasted_iota(jnp.int32, (rb, 2 * SSM_STATE), 1) < SSM_STATE
    is_bwd = jnp.logical_not(is_fwd)
    re_cols = pl.ds(0, 2 * SSM_STATE)
    im_cols = pl.ds(2 * SSM_STATE, 2 * SSM_STATE)

    def step(i, carry):
        rows_f = pl.ds(pl.multiple_of(i * rb, rb), rb)
        rows_b = pl.ds(pl.multiple_of((n_chunks - 1 - i) * rb, rb), rb)
        new = []
        for gi in groups:
            st_re, st_im = carry[2 * gi], carry[2 * gi + 1]
            pltpu.store(x_scr.at[gi, rows_f, re_cols], st_re, mask=is_fwd)
            pltpu.store(x_scr.at[gi, rows_f, im_cols], st_im, mask=is_fwd)
            pltpu.store(x_scr.at[gi, rows_b, re_cols], st_re, mask=is_bwd)
            pltpu.store(x_scr.at[gi, rows_b, im_cols], st_im, mask=is_bwd)
            in_re = jnp.where(is_fwd, s_scr[gi, rows_f, re_cols], s_scr[gi, rows_b, re_cols])
            in_im = jnp.where(is_fwd, s_scr[gi, rows_f, im_cols], s_scr[gi, rows_b, im_cols])
            new.append(ar[gi] * st_re - ai[gi] * st_im + in_re)
            new.append(ar[gi] * st_im + ai[gi] * st_re + in_im)
        return tuple(new)

    zero = jnp.zeros((rb, 2 * SSM_STATE), _F32)
    lax.fori_loop(0, n_chunks, step, (zero,) * (2 * len(groups)), unroll=True)

    for gi in groups:
        y = jnp.dot(u_ref[gi], t_ref[gi], preferred_element_type=_F32)
        y = y + lax.dot_general(x_scr[gi].astype(_BF16), c_ref[gi], (((1,), (1,)), ((), ())),
                                preferred_element_type=_F32)
        y_ref[gi] = y.astype(y_ref.dtype)


def _ssm(ug, tmat, smat, cmat, ar, ai, n_chunks, rows_per_chunk):
    g, rows, _ = ug.shape
    gb = SSM_GROUPS_PER_STEP
    per_group = lambda i: (i, 0, 0)
    return pl.pallas_call(
        functools.partial(_ssm_kernel, n_chunks, rows_per_chunk),
        grid=(g // gb,),
        in_specs=[pl.BlockSpec((gb, rows, CHUNK_W), per_group),
                  pl.BlockSpec((gb, CHUNK_W, CHUNK_W), per_group),
                  pl.BlockSpec((gb, CHUNK_W, STATE_W), per_group),
                  pl.BlockSpec((gb, CHUNK_W, STATE_W), per_group),
                  pl.BlockSpec((gb, 1, 2 * SSM_STATE), per_group),
                  pl.BlockSpec((gb, 1, 2 * SSM_STATE), per_group)],
        out_specs=pl.BlockSpec((gb, rows, CHUNK_W), per_group),
        out_shape=jax.ShapeDtypeStruct((g, rows, CHUNK_W), _BF16),
        scratch_shapes=[pltpu.VMEM((gb, rows, STATE_W), _F32),
                        pltpu.VMEM((gb, rows, STATE_W), _F32)],
        compiler_params=pltpu.CompilerParams(
            dimension_semantics=("arbitrary",), vmem_limit_bytes=VMEM_LIMIT_BYTES),
        name="ssm",
    )(ug, tmat, smat, cmat, ar, ai)


def _ssm_matrices(a_re, a_im, log_dt, b_re, b_im, c_re, c_im, d_skip):
    L, G, P, C = CHUNK, N_SSM_GROUPS, SSM_STATE, SSM_GROUP
    both = lambda a: jnp.concatenate([a[0], a[1]], axis=-1)
    dt = jnp.broadcast_to(jnp.exp(log_dt)[..., None], a_re.shape)
    lam = jnp.stack([both(a_re), both(a_im), both(dt)], axis=1)
    bt = jnp.stack([b_re, b_im], axis=0).transpose(2, 0, 4, 1, 3).reshape(G, 2, C, 2 * P)
    ct = jnp.stack([c_re, c_im], axis=0).transpose(2, 0, 3, 1, 4).reshape(G, 2, C, 2 * P)
    dvec = jnp.tile(d_skip.reshape(G, 1, C), (1, 1, L))
    gb = SLOTS_PER_VREG
    per_group = lambda *blk: pl.BlockSpec((gb,) + blk, lambda i: (i,) + (0,) * len(blk))
    mat = jax.ShapeDtypeStruct((G, L * C, L * C), _BF16)
    vec = jax.ShapeDtypeStruct((G, 1, 2 * P), _F32)
    return pl.pallas_call(
        _ssm_prep_kernel,
        grid=(G // gb,),
        in_specs=[per_group(3, 2 * P), per_group(2, C, 2 * P), per_group(2, C, 2 * P),
                  per_group(1, L * C)],
        out_specs=[per_group(L * C, L * C), per_group(L * C, STATE_W),
                   per_group(L * C, STATE_W), per_group(1, 2 * P), per_group(1, 2 * P)],
        out_shape=[mat, mat, mat, vec, vec],
        compiler_params=pltpu.CompilerParams(dimension_semantics=("arbitrary",)),
        name="ssm_prep",
    )(lam, bt, ct, dvec)


def _ssm_prep_kernel(lam_ref, bt_ref, ct_ref, dvec_ref, t_ref, s_ref, c_ref, ar_ref, ai_ref):
    for gi in range(lam_ref.shape[0]):
        _ssm_prep_group(gi, lam_ref, bt_ref, ct_ref, dvec_ref, t_ref, s_ref, c_ref, ar_ref,
                        ai_ref)


def _ssm_prep_group(gi, lam_ref, bt_ref, ct_ref, dvec_ref, t_ref, s_ref, c_ref, ar_ref,
                    ai_ref):
    L, P, C = CHUNK, SSM_STATE, SSM_GROUP
    g8 = gi % SLOTS_PER_VREG

    def time_of_slot(slot):
        return (jnp.bitwise_and(slot, -SLOTS_PER_VREG)
                + jnp.bitwise_and(slot - g8, SLOTS_PER_VREG - 1))

    def cmul(x_re, x_im, y_re, y_im):
        return x_re * y_re - x_im * y_im, x_re * y_im + x_im * y_re

    a_re, a_im, dt = lam_ref[gi, 0:1], lam_ref[gi, 1:2], lam_ref[gi, 2:3]
    lr, li = a_re * dt, a_im * dt

    def cpow(tau):
        mag, ang = jnp.exp(lr * tau), li * tau
        return mag * jnp.cos(ang), mag * jnp.sin(ang)

    ab_re, ab_im = cpow(jnp.ones_like(lr))
    den = a_re * a_re + a_im * a_im
    co_re = ((ab_re - 1.0) * a_re + ab_im * a_im) / den
    co_im = (ab_im * a_re - (ab_re - 1.0) * a_im) / den
    bb_re, bb_im = cmul(bt_ref[gi, 0], bt_ref[gi, 1], co_re, co_im)
    cc_re, cc_im = ct_ref[gi, 0], ct_ref[gi, 1]

    def outer(p, m):
        (p_re, p_im), (m_re, m_im) = p, m
        blocks = [cmul(p_re[s:s + 1], p_im[s:s + 1], m_re, m_im) for s in range(L)]
        return (jnp.concatenate([b[0] for b in blocks], axis=0),
                jnp.concatenate([b[1] for b in blocks], axis=0))

    j = time_of_slot(lax.broadcasted_iota(jnp.int32, (L, 2 * P), 0)).astype(_F32)
    fwd = lax.broadcasted_iota(jnp.int32, (L, 2 * P), 1) < P
    s_re, s_im = outer(cpow(jnp.where(fwd, L - 1 - j, j)), (bb_re, bb_im))
    s_ref[gi] = jnp.concatenate([s_re, s_im], axis=1).astype(_BF16)
    w_re, w_im = outer(cpow(jnp.where(fwd, j + 1, L - j)), (cc_re, cc_im))
    wcat = jnp.concatenate([w_re, -w_im], axis=1)
    c_ref[gi] = wcat.astype(_BF16)
    l_re, l_im = outer(cpow(jnp.where(fwd, -1 - j, j - L)), (bb_re, bb_im))
    lcat = jnp.concatenate([l_re, l_im], axis=1)
    row = lax.broadcasted_iota(jnp.int32, (L * C, L * C), 0)
    col = lax.broadcasted_iota(jnp.int32, (L * C, L * C), 1)
    is_fwd_state = jnp.bitwise_and(col, P) == 0

    def split(a):
        hi = a.astype(_BF16).astype(_F32)
        return hi, a - hi

    def dot_state(a_parts, b_parts):
        (a_hi, a_lo), (b_hi, b_lo) = [[p.astype(_BF16) for p in parts]
                                      for parts in (a_parts, b_parts)]
        d = lambda a, b: lax.dot_general(a, b, (((1,), (1,)), ((), ())),
                                         preferred_element_type=_F32)
        return d(a_hi, b_hi) + d(a_hi, b_lo) + d(a_lo, b_hi)

    l_parts, w_parts = split(lcat), split(wcat)
    mf = dot_state([jnp.where(is_fwd_state, p, 0.0) for p in l_parts], w_parts)
    mb = dot_state([jnp.where(is_fwd_state, 0.0, p) for p in l_parts], w_parts)
    lane_to_slot = C.bit_length() - 1
    jj = time_of_slot(jnp.right_shift(row, lane_to_slot))
    tt = time_of_slot(jnp.right_shift(col, lane_to_slot))
    tm = (jnp.where(tt >= jj, mf, 0.0) + jnp.where(tt <= jj, mb, 0.0)
          + jnp.where(row == col, dvec_ref[gi], 0.0))
    t_ref[gi] = tm.astype(_BF16)
    al_re, al_im = cpow(jnp.full_like(lr, float(L)))
    ar_ref[gi] = al_re
    ai_ref[gi] = al_im


def _attn_kernel(sink_ref, q_ref, k_ref, v_ref, bias_ref, za_ref, o_ref):
    nq = q_ref.shape[1] // BLOCK
    rows = lambda ref, c: ref[0, c * BLOCK:(c + 1) * BLOCK]
    k_blocks = [rows(k_ref, c) for c in range(nq)]
    v_blocks = [rows(v_ref, c) for c in range(nq)]
    lane = lax.broadcasted_iota(jnp.int32, (BLOCK, LANES), 1)
    halves = [lane < HEAD_DIM, lane >= HEAD_DIM]
    keep = [h.astype(_F32).astype(_BF16) for h in halves]
    slab = lambda blk, j, e: blk[:, (j ^ e) * LANES:((j ^ e) + 1) * LANES]
    k_half = {(j, e): [slab(kb, j, e) * keep[e] for kb in k_blocks]
              for j in range(KV_HEADS) for e in range(HEAD_PAIR)}
    v_half = {(j, e): [jnp.concatenate([slab(vb, j, e) * keep[e], keep[e]], axis=1)
                       for vb in v_blocks]
              for j in range(KV_HEADS) for e in range(HEAD_PAIR)}
    for c in range(nq):
        blocks = [b for b in (c - 1, c, c + 1) if 0 <= b < nq]
        slot0 = blocks[0] - (c - 1)
        nk = len(blocks) * BLOCK
        q = rows(q_ref, c)
        slabs_out = []
        for j in range(KV_HEADS):
            window = lambda halves_of: jnp.concatenate(
                [halves_of[j, e][b] for e in range(HEAD_PAIR) for b in blocks], axis=0)
            kcat = window(k_half)
            vcat = window(v_half)
            qs = jnp.concatenate(
                [q[:, (j * PAIRS_PER_KV + i) * LANES:(j * PAIRS_PER_KV + i + 1) * LANES]
                 for i in range(PAIRS_PER_KV)], axis=0)
            s = lax.dot_general(qs, kcat, (((1,), (1,)), ((), ())),
                                preferred_element_type=_F32)
            p_rows, e_rows = [], []
            for i in range(PAIRS_PER_KV):
                p_lanes, e_sink = [], []
                for e in range(HEAD_PAIR):
                    head = j * Q_PER_KV + i * HEAD_PAIR + e
                    sg = (s[i * BLOCK:(i + 1) * BLOCK, e * nk:(e + 1) * nk]
                          + bias_ref[head, :, slot0 * BLOCK:slot0 * BLOCK + nk])
                    sk = sink_ref[head]
                    m = jnp.maximum(jnp.max(sg, axis=-1, keepdims=True), sk)
                    p_lanes.append(jnp.exp2((sg - m).astype(_BF16)))
                    e_sink.append(jnp.broadcast_to(jnp.exp2(sk - m), (BLOCK, LANES)))
                p_rows.append(jnp.concatenate(p_lanes, axis=1))
                e_rows.append(jnp.where(halves[0], e_sink[0], e_sink[1]))
            p = jnp.concatenate(p_rows, axis=0)
            o = jnp.dot(p, vcat, preferred_element_type=_F32)
            den = o[:, LANES:] + jnp.concatenate(e_rows, axis=0)
            on = o[:, :LANES] / den
            slabs_out += [on[i * BLOCK:(i + 1) * BLOCK] for i in range(PAIRS_PER_KV)]
        o_all = jnp.concatenate(slabs_out, axis=1)
        o_ref[0, c * BLOCK:(c + 1) * BLOCK, :] = (
            o_all * rows(za_ref, c).astype(_F32)).astype(o_ref.dtype)


def _attention(sink, q, k, v, bias, za):
    b, s, _ = q.shape
    seq = lambda w: pl.BlockSpec((1, s, w), lambda i: (i, 0, 0))
    return pl.pallas_call(
        _attn_kernel,
        grid=(b,),
        in_specs=[pl.BlockSpec(memory_space=pltpu.SMEM), seq(D_ATTN), seq(D_KVX), seq(D_KVX),
                  pl.BlockSpec(bias.shape, lambda i: (0, 0, 0)), seq(D_ATTN)],
        out_specs=seq(D_ATTN),
        out_shape=jax.ShapeDtypeStruct((b, s, D_ATTN), _BF16),
        compiler_params=pltpu.CompilerParams(
            dimension_semantics=("arbitrary",), vmem_limit_bytes=VMEM_LIMIT_BYTES),
        name="attn",
    )(sink, q, k, v, bias, za)


def _t5_bucket_np(rel):
    half = NUM_BUCKETS // 2
    ret = (rel > 0).astype(np.int64) * half
    n = np.abs(rel)
    max_exact = half // 2
    nf = np.maximum(n, 1).astype(np.float64)
    large = max_exact + (np.log(nf / max_exact) / math.log(MAX_DISTANCE / max_exact)
                         * (half - max_exact)).astype(np.int64)
    large = np.minimum(large, half - 1)
    return ret + np.where(n < max_exact, n, large)


def _attn_bias(rel_table):
    rel = (np.arange(3 * BLOCK)[None, :] - BLOCK) - np.arange(BLOCK)[:, None]
    onehot = (_t5_bucket_np(rel)[None] == np.arange(NUM_BUCKETS)[:, None, None])
    bias = jnp.einsum('bh,bqk->hqk', LOG2_E * rel_table.astype(_F32),
                      jnp.asarray(onehot, _F32), precision=_HI)
    return jnp.where(jnp.asarray(np.abs(rel) <= WINDOW), bias, NEG_INF)


def _merge_kernel(x_ref, gain_ref, wg_ref, bg_ref, ao_ref, yg_ref, zs_ref, wglu_ref,
                  bglu_ref, wba_ref, wbs_ref, wout_ref, fgain_ref, o_ref, *y_scrs):
    nb, nt = x_ref.shape[0], x_ref.shape[1]
    rows = nb * SUB_STEPS
    for st in range(nt // SUB_STEPS):
        steps = slice(st * SUB_STEPS, (st + 1) * SUB_STEPS)
        x = x_ref[:, steps, :].reshape(rows, D_MODEL)
        h = _rms_normalize(x, gain_ref[...]).astype(_BF16)
        gates2 = _twice_sigmoid_of_double(
            jnp.dot(h, wg_ref[...], preferred_element_type=_F32) + bg_ref[...])
        for c in range(SUB_STEPS // CHUNK):
            _from_group_major(yg_ref, y_scrs[st], st * (SUB_STEPS // CHUNK) + c, nb, c * CHUNK,
                              SUB_STEPS)
        y = jnp.concatenate([y_scrs[st][v] for v in range(D_SSM // LANES)], axis=1)
        y2 = _twice_gelu(y)
        glu_half = (jnp.dot(y2.astype(_BF16), wglu_ref[...], preferred_element_type=_F32)
                    + bglu_ref[...])
        ssm4 = (y2 * _twice_sigmoid_of_double(glu_half)
                * zs_ref[:, steps, :].reshape(rows, D_SSM).astype(_F32))
        pa = jnp.dot(ao_ref[:, steps, :].reshape(rows, D_ATTN), wba_ref[...],
                     preferred_element_type=_F32)
        ps = jnp.dot(ssm4.astype(_BF16), wbs_ref[...], preferred_element_type=_F32)
        merged2 = gates2[:, :D_MODEL] * pa + gates2[:, D_MODEL:] * ps
        xn = x + jnp.dot(merged2.astype(_BF16), wout_ref[...], preferred_element_type=_F32)
        o_ref[:, steps, :] = _rms_normalize(xn, fgain_ref[...]).reshape(nb, SUB_STEPS, D_MODEL)


def _merge(x, gain, wg, bg, ao, yg, zs, wglu, bglu, wba, wbs, wout, fgain):
    bsz, s, _ = x.shape
    nt = TOKEN_TILE // bsz
    tok = lambda i: (0, i, 0)
    fixed = lambda i: (0, 0)
    full = lambda a: pl.BlockSpec(a.shape, fixed)
    act = lambda n: pl.BlockSpec((bsz, nt, n), tok)
    return pl.pallas_call(
        _merge_kernel,
        grid=(s // nt,),
        in_specs=[act(D_MODEL), full(gain), full(wg), full(bg), act(D_ATTN),
                  pl.BlockSpec((N_SSM_GROUPS, nt // CHUNK, bsz, CHUNK_W),
                               lambda i: (0, i, 0, 0)),
                  act(D_SSM), full(wglu), full(bglu), full(wba), full(wbs), full(wout),
                  full(fgain)],
        out_specs=act(D_MODEL),
        out_shape=jax.ShapeDtypeStruct((bsz, s, D_MODEL), _F32),
        scratch_shapes=[pltpu.VMEM((D_SSM // LANES, bsz * SUB_STEPS, LANES), _F32)
                        for _ in range(nt // SUB_STEPS)],
        compiler_params=pltpu.CompilerParams(
            dimension_semantics=("arbitrary",), vmem_limit_bytes=VMEM_LIMIT_BYTES),
        name="merge",
    )(x, gain, wg, bg, ao, yg, zs, wglu, bglu, wba, wbs, wout, fgain)


def _layer(x, norm_gain, w_in, b_gate, attn_sink, a_re, a_im, log_dt, b_re, b_im, c_re, c_im,
           d_skip, w_glu, b_glu, w_ba, w_bs, w_out, out_gain, bias):
    bsz, s, d = x.shape
    assert d == D_MODEL and x.dtype == _F32 and w_in.shape == (D_MODEL, W_OFF_GATES + 2 * D_MODEL)
    assert bsz % 8 == 0 and s % BLOCK == 0
    assert INPROJ_TOKEN_TILE % (bsz * SUB_STEPS) == 0 and s % (INPROJ_TOKEN_TILE // bsz) == 0
    assert TOKEN_TILE % (bsz * SUB_STEPS) == 0 and s % (TOKEN_TILE // bsz) == 0
    n_chunks = s // CHUNK
    gain = norm_gain.reshape(1, D_MODEL).astype(_F32)
    q, k, v, za, ug, zs = _inproj(x, gain, w_in)

    tmat, smat, cmat, ar, ai = _ssm_matrices(a_re, a_im, log_dt, b_re, b_im, c_re, c_im,
                                             d_skip)
    yg = _ssm(ug.reshape(N_SSM_GROUPS, n_chunks * bsz, CHUNK_W), tmat, smat, cmat, ar, ai,
              n_chunks, bsz)
    yg = yg.reshape(N_SSM_GROUPS, n_chunks, bsz, CHUNK_W)

    ao = _attention(LOG2_E * attn_sink.astype(_F32), q, k, v, bias, za)

    return _merge(x, 0.5 * gain,
                  w_in[:, W_OFF_GATES:].astype(_BF16),
                  (0.5 * b_gate).reshape(1, -1).astype(_F32),
                  ao, yg, zs,
                  (0.25 * w_glu).astype(_BF16),
                  (0.5 * b_glu).reshape(1, -1).astype(_F32),
                  w_ba.astype(_BF16),
                  (0.25 * w_bs).astype(_BF16),
                  (0.5 * w_out).astype(_BF16),
                  out_gain.reshape(1, D_MODEL).astype(_F32))


def kernel(x, norm_gain, w_in, b_gate, attn_sink, rel_bias_table, ssm_a_re, ssm_a_im,
           ssm_log_dt, ssm_b_re, ssm_b_im, ssm_c_re, ssm_c_im, ssm_d, w_glu, b_glu,
           w_branch_attn, w_branch_ssm, w_out, final_norm_gain):
    depth = norm_gain.shape[0]
    assert depth == 1, "final norm is fused into the single layer's epilogue"
    bias = _attn_bias(rel_bias_table)
    l = 0
    return _layer(x, norm_gain[l], w_in[l], b_gate[l], attn_sink[l], ssm_a_re[l],
                  ssm_a_im[l], ssm_log_dt[l], ssm_b_re[l], ssm_b_im[l], ssm_c_re[l],
                  ssm_c_im[l], ssm_d[l], w_glu[l], b_glu[l], w_branch_attn[l],
                  w_branch_ssm[l], w_out[l], final_norm_gain, bias)
```

```python
import functools
import math

import jax
import jax.numpy as jnp
import numpy as np
from jax import lax
from jax.experimental import pallas as pl
from jax.experimental.pallas import tpu as pltpu

D_MODEL = 1024
ATTN_HEADS = 8
KV_HEADS = 2
Q_PER_KV = ATTN_HEADS // KV_HEADS
HEAD_DIM = 64
D_ATTN = ATTN_HEADS * HEAD_DIM
D_KV = KV_HEADS * HEAD_DIM
WINDOW = 128
BLOCK = 128
NUM_BUCKETS = 32
MAX_DISTANCE = 128
D_SSM = 512
SSM_GROUP = 16
N_SSM_GROUPS = D_SSM // SSM_GROUP
SSM_STATE = 64
EPS = 1e-6
NEG_INF = -1e30

LANES = 128
CHUNK = 16
CHUNK_W = CHUNK * SSM_GROUP
STATE_W = 4 * SSM_STATE
SLOTS_PER_VREG = LANES // SSM_GROUP
SSM_GROUPS_PER_STEP = 4

HEAD_PAIR = LANES // HEAD_DIM
PAIRS_PER_KV = Q_PER_KV // HEAD_PAIR
assert D_KV == LANES and KV_HEADS == HEAD_PAIR
D_KVX = 2 * D_KV
LOG2_E = math.log2(math.e)
Q_SCALE = HEAD_DIM ** -0.5 * LOG2_E

W_OFF_ZA = D_ATTN + 2 * D_KV
W_OFF_U = W_OFF_ZA + D_ATTN
W_OFF_GATES = W_OFF_U + 2 * D_SSM
GATE_BLOCK = (W_OFF_GATES + 2 * D_MODEL) // 2
assert GATE_BLOCK % LANES == 0 and GATE_BLOCK <= W_OFF_GATES

TOKEN_TILE = 1024
INPROJ_TOKEN_TILE = 2048
SUB_STEPS = CHUNK
VMEM_LIMIT_BYTES = 56 * 1024 * 1024

_F32 = jnp.float32
_BF16 = jnp.bfloat16
_HI = lax.Precision.HIGHEST


def _silu(x):
    return x * (0.5 * jnp.tanh(0.5 * x) + 0.5)


def _twice_sigmoid_of_double(xh):
    return 1.0 + jnp.tanh(xh)


def _twice_gelu(x):
    c = math.sqrt(2.0 / math.pi)
    return x * (1.0 + jnp.tanh(c * (x + 0.044715 * (x * x * x))))


def _rms_normalize(x, gain):
    ms = jnp.mean(x * x, axis=-1, keepdims=True)
    return x * lax.rsqrt(ms + EPS) * gain


def _slot_masks(rows):
    lane = lax.broadcasted_iota(jnp.int32, (rows, LANES), 1)
    return [(lane >= p * SSM_GROUP) & (lane < (p + 1) * SSM_GROUP)
            for p in range(SLOTS_PER_VREG)]


def _to_group_major(u_scr, ug_ref, kk, nb, t0, steps):
    masks = _slot_masks(nb)
    for v in range(D_SSM // LANES):
        for half in range(CHUNK // SLOTS_PER_VREG):
            rolled = []
            for t8 in range(SLOTS_PER_VREG):
                r = t0 + half * SLOTS_PER_VREG + t8
                piece = u_scr[v, pl.ds(r, nb, stride=steps), :]
                rolled.append(pltpu.roll(piece, t8 * SSM_GROUP, 1) if t8 else piece)
            for p0 in range(SLOTS_PER_VREG):
                acc = rolled[0]
                for t8 in range(1, SLOTS_PER_VREG):
                    acc = jnp.where(masks[(p0 + t8) % SLOTS_PER_VREG], rolled[t8], acc)
                g = v * SLOTS_PER_VREG + p0
                ug_ref[g, kk, :, half * LANES:(half + 1) * LANES] = acc.astype(_BF16)


def _from_group_major(yg_ref, y_scr, kk, nb, t0, steps):
    masks = _slot_masks(nb)
    for v in range(D_SSM // LANES):
        for half in range(CHUNK // SLOTS_PER_VREG):
            src = [yg_ref[v * SLOTS_PER_VREG + p0, kk, :,
                          half * LANES:(half + 1) * LANES].astype(_F32)
                   for p0 in range(SLOTS_PER_VREG)]
            for t8 in range(SLOTS_PER_VREG):
                acc = src[0]
                for p0 in range(1, SLOTS_PER_VREG):
                    acc = jnp.where(masks[(p0 + t8) % SLOTS_PER_VREG], src[p0], acc)
                if t8:
                    acc = pltpu.roll(acc, LANES - t8 * SSM_GROUP, 1)
                r = t0 + half * SLOTS_PER_VREG + t8
                y_scr[v, pl.ds(r, nb, stride=steps), :] = acc


def _inproj_kernel(x_ref, gain_ref, w_ref, q_ref, k_ref, v_ref, za_ref, ug_ref, zs_ref,
                   *u_scrs):
    nb, nt = x_ref.shape[0], x_ref.shape[1]
    rows = nb * SUB_STEPS
    for st in range(nt // SUB_STEPS):
        steps = slice(st * SUB_STEPS, (st + 1) * SUB_STEPS)
        x = x_ref[:, steps, :].reshape(rows, D_MODEL)
        h = _rms_normalize(x, gain_ref[...]).astype(_BF16)

        def put(ref, val):
            ref[:, steps, :] = val.astype(_BF16).reshape(nb, SUB_STEPS, val.shape[-1])

        proj = lambda lo, n: jnp.dot(h, w_ref[:, lo:lo + n], preferred_element_type=_F32)
        u_zs = proj(W_OFF_U, 2 * D_SSM)
        for v in range(D_SSM // LANES):
            u_scrs[st][v] = u_zs[:, v * LANES:(v + 1) * LANES]
        for c in range(SUB_STEPS // CHUNK):
            _to_group_major(u_scrs[st], ug_ref, st * (SUB_STEPS // CHUNK) + c, nb, c * CHUNK,
                            SUB_STEPS)
        put(zs_ref, _silu(u_zs[:, D_SSM:]))
        put(za_ref, _silu(proj(W_OFF_ZA, D_ATTN)))
        qkv = proj(0, W_OFF_ZA)
        put(q_ref, qkv[:, :D_ATTN] * Q_SCALE)
        with_swap = lambda a: jnp.concatenate([a, pltpu.roll(a, HEAD_DIM, 1)], axis=1)
        put(k_ref, with_swap(qkv[:, D_ATTN:D_ATTN + D_KV]))
        put(v_ref, with_swap(qkv[:, D_ATTN + D_KV:]))


def _inproj(x, gain, w_in):
    bsz, s, _ = x.shape
    nt = INPROJ_TOKEN_TILE // bsz
    tok = lambda i: (0, i, 0)
    fixed = lambda i: (0, 0)
    outs = [D_ATTN, D_KVX, D_KVX, D_ATTN, None, D_SSM]
    act = lambda n: (pl.BlockSpec((bsz, nt, n), tok), jax.ShapeDtypeStruct((bsz, s, n), _BF16))
    ug = (pl.BlockSpec((N_SSM_GROUPS, nt // CHUNK, bsz, CHUNK_W), lambda i: (0, i, 0, 0)),
          jax.ShapeDtypeStruct((N_SSM_GROUPS, s // CHUNK, bsz, CHUNK_W), _BF16))
    specs, shapes = zip(*[ug if n is None else act(n) for n in outs])
    return pl.pallas_call(
        _inproj_kernel,
        grid=(s // nt,),
        in_specs=[pl.BlockSpec((bsz, nt, D_MODEL), tok),
                  pl.BlockSpec((1, D_MODEL), fixed)]
                 + [pl.BlockSpec((D_MODEL, W_OFF_GATES), fixed)],
        out_specs=list(specs),
        out_shape=list(shapes),
        scratch_shapes=[pltpu.VMEM((D_SSM // LANES, bsz * SUB_STEPS, LANES), _F32)
                        for _ in range(nt // SUB_STEPS)],
        compiler_params=pltpu.CompilerParams(
            dimension_semantics=("arbitrary",), vmem_limit_bytes=VMEM_LIMIT_BYTES),
        name="inproj",
    )(x, gain, w_in)


def _ssm_kernel(n_chunks, rows_per_chunk, u_ref, t_ref, s_ref, c_ref, ar_ref, ai_ref,
                y_ref, s_scr, x_scr):
    rb = rows_per_chunk
    groups = range(u_ref.shape[0])
    for gi in groups:
        s_scr[gi] = jnp.dot(u_ref[gi], s_ref[gi], preferred_element_type=_F32)
    ar = [jnp.broadcast_to(ar_ref[gi], (rb, 2 * SSM_STATE)) for gi in groups]
    ai = [jnp.broadcast_to(ai_ref[gi], (rb, 2 * SSM_STATE)) for gi in groups]
    is_fwd = lax.broadcasted_iota(jnp.int32, (rb, 2 * SSM_STATE), 1) < SSM_STATE
    is_bwd = jnp.logical_not(is_fwd)
    re_cols = pl.ds(0, 2 * SSM_STATE)
    im_cols = pl.ds(2 * SSM_STATE, 2 * SSM_STATE)

    def step(i, carry):
        rows_f = pl.ds(pl.multiple_of(i * rb, rb), rb)
        rows_b = pl.ds(pl.multiple_of((n_chunks - 1 - i) * rb, rb), rb)
        new = []
        for gi in groups:
            st_re, st_im = carry[2 * gi], carry[2 * gi + 1]
            pltpu.store(x_scr.at[gi, rows_f, re_cols], st_re, mask=is_fwd)
            pltpu.store(x_scr.at[gi, rows_f, im_cols], st_im, mask=is_fwd)
            pltpu.store(x_scr.at[gi, rows_b, re_cols], st_re, mask=is_bwd)
            pltpu.store(x_scr.at[gi, rows_b, im_cols], st_im, mask=is_bwd)
            in_re = jnp.where(is_fwd, s_scr[gi, rows_f, re_cols], s_scr[gi, rows_b, re_cols])
            in_im = jnp.where(is_fwd, s_scr[gi, rows_f, im_cols], s_scr[gi, rows_b, im_cols])
            new.append(ar[gi] * st_re - ai[gi] * st_im + in_re)
            new.append(ar[gi] * st_im + ai[gi] * st_re + in_im)
        return tuple(new)

    zero = jnp.zeros((rb, 2 * SSM_STATE), _F32)
    lax.fori_loop(0, n_chunks, step, (zero,) * (2 * len(groups)), unroll=True)

    for gi in groups:
        y = jnp.dot(u_ref[gi], t_ref[gi], preferred_element_type=_F32)
        y = y + lax.dot_general(x_scr[gi].astype(_BF16), c_ref[gi], (((1,), (1,)), ((), ())),
                                preferred_element_type=_F32)
        y_ref[gi] = y.astype(y_ref.dtype)


def _ssm(ug, tmat, smat, cmat, ar, ai, n_chunks, rows_per_chunk):
    g, rows, _ = ug.shape
    gb = SSM_GROUPS_PER_STEP
    per_group = lambda i: (i, 0, 0)
    return pl.pallas_call(
        functools.partial(_ssm_kernel, n_chunks, rows_per_chunk),
        grid=(g // gb,),
        in_specs=[pl.BlockSpec((gb, rows, CHUNK_W), per_group),
                  pl.BlockSpec((gb, CHUNK_W, CHUNK_W), per_group),
                  pl.BlockSpec((gb, CHUNK_W, STATE_W), per_group),
                  pl.BlockSpec((gb, CHUNK_W, STATE_W), per_group),
                  pl.BlockSpec((gb, 1, 2 * SSM_STATE), per_group),
                  pl.BlockSpec((gb, 1, 2 * SSM_STATE), per_group)],
        out_specs=pl.BlockSpec((gb, rows, CHUNK_W), per_group),
        out_shape=jax.ShapeDtypeStruct((g, rows, CHUNK_W), _BF16),
        scratch_shapes=[pltpu.VMEM((gb, rows, STATE_W), _F32),
                        pltpu.VMEM((gb, rows, STATE_W), _F32)],
        compiler_params=pltpu.CompilerParams(
            dimension_semantics=("arbitrary",), vmem_limit_bytes=VMEM_LIMIT_BYTES),
        name="ssm",
    )(ug, tmat, smat, cmat, ar, ai)


def _ssm_matrices(a_re, a_im, log_dt, b_re, b_im, c_re, c_im, d_skip):
    L, G, P, C = CHUNK, N_SSM_GROUPS, SSM_STATE, SSM_GROUP
    both = lambda a: jnp.concatenate([a[0], a[1]], axis=-1)
    dt = jnp.broadcast_to(jnp.exp(log_dt)[..., None], a_re.shape)
    lam = jnp.stack([both(a_re), both(a_im), both(dt)], axis=1)
    bt = jnp.stack([b_re, b_im], axis=0).transpose(2, 0, 4, 1, 3).reshape(G, 2, C, 2 * P)
    ct = jnp.stack([c_re, c_im], axis=0).transpose(2, 0, 3, 1, 4).reshape(G, 2, C, 2 * P)
    dvec = jnp.tile(d_skip.reshape(G, 1, C), (1, 1, L))
    gb = SLOTS_PER_VREG
    per_group = lambda *blk: pl.BlockSpec((gb,) + blk, lambda i: (i,) + (0,) * len(blk))
    mat = jax.ShapeDtypeStruct((G, L * C, L * C), _BF16)
    vec = jax.ShapeDtypeStruct((G, 1, 2 * P), _F32)
    return pl.pallas_call(
        _ssm_prep_kernel,
        grid=(G // gb,),
        in_specs=[per_group(3, 2 * P), per_group(2, C, 2 * P), per_group(2, C, 2 * P),
                  per_group(1, L * C)],
        out_specs=[per_group(L * C, L * C), per_group(L * C, STATE_W),
                   per_group(L * C, STATE_W), per_group(1, 2 * P), per_group(1, 2 * P)],
        out_shape=[mat, mat, mat, vec, vec],
        compiler_params=pltpu.CompilerParams(dimension_semantics=("arbitrary",)),
        name="ssm_prep",
    )(lam, bt, ct, dvec)


def _ssm_prep_kernel(lam_ref, bt_ref, ct_ref, dvec_ref, t_ref, s_ref, c_ref, ar_ref, ai_ref):
    for gi in range(lam_ref.shape[0]):
        _ssm_prep_group(gi, lam_ref, bt_ref, ct_ref, dvec_ref, t_ref, s_ref, c_ref, ar_ref,
                        ai_ref)


def _ssm_prep_group(gi, lam_ref, bt_ref, ct_ref, dvec_ref, t_ref, s_ref, c_ref, ar_ref,
                    ai_ref):
    L, P, C = CHUNK, SSM_STATE, SSM_GROUP
    g8 = gi % SLOTS_PER_VREG

    def time_of_slot(slot):
        return (jnp.bitwise_and(slot, -SLOTS_PER_VREG)
                + jnp.bitwise_and(slot - g8, SLOTS_PER_VREG - 1))

    def cmul(x_re, x_im, y_re, y_im):
        return x_re * y_re - x_im * y_im, x_re * y_im + x_im * y_re

    a_re, a_im, dt = lam_ref[gi, 0:1], lam_ref[gi, 1:2], lam_ref[gi, 2:3]
    lr, li = a_re * dt, a_im * dt

    def cpow(tau):
        mag, ang = jnp.exp(lr * tau), li * tau
        return mag * jnp.cos(ang), mag * jnp.sin(ang)

    ab_re, ab_im = cpow(jnp.ones_like(lr))
    den = a_re * a_re + a_im * a_im
    co_re = ((ab_re - 1.0) * a_re + ab_im * a_im) / den
    co_im = (ab_im * a_re - (ab_re - 1.0) * a_im) / den
    bb_re, bb_im = cmul(bt_ref[gi, 0], bt_ref[gi, 1], co_re, co_im)
    cc_re, cc_im = ct_ref[gi, 0], ct_ref[gi, 1]

    def outer(p, m):
        (p_re, p_im), (m_re, m_im) = p, m
        blocks = [cmul(p_re[s:s + 1], p_im[s:s + 1], m_re, m_im) for s in range(L)]
        return (jnp.concatenate([b[0] for b in blocks], axis=0),
                jnp.concatenate([b[1] for b in blocks], axis=0))

    j = time_of_slot(lax.broadcasted_iota(jnp.int32, (L, 2 * P), 0)).astype(_F32)
    fwd = lax.broadcasted_iota(jnp.int32, (L, 2 * P), 1) < P
    s_re, s_im = outer(cpow(jnp.where(fwd, L - 1 - j, j)), (bb_re, bb_im))
    s_ref[gi] = jnp.concatenate([s_re, s_im], axis=1).astype(_BF16)
    w_re, w_im = outer(cpow(jnp.where(fwd, j + 1, L - j)), (cc_re, cc_im))
    wcat = jnp.concatenate([w_re, -w_im], axis=1)
    c_ref[gi] = wcat.astype(_BF16)
    l_re, l_im = outer(cpow(jnp.where(fwd, -1 - j, j - L)), (bb_re, bb_im))
    lcat = jnp.concatenate([l_re, l_im], axis=1)
    row = lax.broadcasted_iota(jnp.int32, (L * C, L * C), 0)
    col = lax.broadcasted_iota(jnp.int32, (L * C, L * C), 1)
    is_fwd_state = jnp.bitwise_and(col, P) == 0

    def split(a):
        hi = a.astype(_BF16).astype(_F32)
        return hi, a - hi

    def dot_state(a_parts, b_parts):
        (a_hi, a_lo), (b_hi, b_lo) = [[p.astype(_BF16) for p in parts]
                                      for parts in (a_parts, b_parts)]
        d = lambda a, b: lax.dot_general(a, b, (((1,), (1,)), ((), ())),
                                         preferred_element_type=_F32)
        return d(a_hi, b_hi) + d(a_hi, b_lo) + d(a_lo, b_hi)

    l_parts, w_parts = split(lcat), split(wcat)
    mf = dot_state([jnp.where(is_fwd_state, p, 0.0) for p in l_parts], w_parts)
    mb = dot_state([jnp.where(is_fwd_state, 0.0, p) for p in l_parts], w_parts)
    lane_to_slot = C.bit_length() - 1
    jj = time_of_slot(jnp.right_shift(row, lane_to_slot))
    tt = time_of_slot(jnp.right_shift(col, lane_to_slot))
    tm = (jnp.where(tt >= jj, mf, 0.0) + jnp.where(tt <= jj, mb, 0.0)
          + jnp.where(row == col, dvec_ref[gi], 0.0))
    t_ref[gi] = tm.astype(_BF16)
    al_re, al_im = cpow(jnp.full_like(lr, float(L)))
    ar_ref[gi] = al_re
    ai_ref[gi] = al_im


def _attn_kernel(sink_ref, q_ref, k_ref, v_ref, bias_ref, za_ref, o_ref):
    nq = q_ref.shape[1] // BLOCK
    rows = lambda ref, c: ref[0, c * BLOCK:(c + 1) * BLOCK]
    k_blocks = [rows(k_ref, c) for c in range(nq)]
    v_blocks = [rows(v_ref, c) for c in range(nq)]
    lane = lax.broadcasted_iota(jnp.int32, (BLOCK, LANES), 1)
    halves = [lane < HEAD_DIM, lane >= HEAD_DIM]
    keep = [h.astype(_F32).astype(_BF16) for h in halves]
    slab = lambda blk, j, e: blk[:, (j ^ e) * LANES:((j ^ e) + 1) * LANES]
    k_half = {(j, e): [slab(kb, j, e) * keep[e] for kb in k_blocks]
              for j in range(KV_HEADS) for e in range(HEAD_PAIR)}
    v_half = {(j, e): [jnp.concatenate([slab(vb, j, e) * keep[e], keep[e]], axis=1)
                       for vb in v_blocks]
              for j in range(KV_HEADS) for e in range(HEAD_PAIR)}
    for c in range(nq):
        blocks = [b for b in (c - 1, c, c + 1) if 0 <= b < nq]
        slot0 = blocks[0] - (c - 1)
        nk = len(blocks) * BLOCK
        q = rows(q_ref, c)
        slabs_out = []
        for j in range(KV_HEADS):
            window = lambda halves_of: jnp.concatenate(
                [halves_of[j, e][b] for e in range(HEAD_PAIR) for b in blocks], axis=0)
            kcat = window(k_half)
            vcat = window(v_half)
            qs = jnp.concatenate(
                [q[:, (j * PAIRS_PER_KV + i) * LANES:(j * PAIRS_PER_KV + i + 1) * LANES]
                 for i in range(PAIRS_PER_KV)], axis=0)
            s = lax.dot_general(qs, kcat, (((1,), (1,)), ((), ())),
                                preferred_element_type=_F32)
            p_rows, e_rows = [], []
            for i in range(PAIRS_PER_KV):
                p_lanes, e_sink = [], []
                for e in range(HEAD_PAIR):
                    head = j * Q_PER_KV + i * HEAD_PAIR + e
                    sg = (s[i * BLOCK:(i + 1) * BLOCK, e * nk:(e + 1) * nk]
                          + bias_ref[head, :, slot0 * BLOCK:slot0 * BLOCK + nk])
                    sk = sink_ref[head]
                    m = jnp.maximum(jnp.max(sg, axis=-1, keepdims=True), sk)
                    p_lanes.append(jnp.exp2(sg - m).astype(_BF16))
                    e_sink.append(jnp.broadcast_to(jnp.exp2(sk - m), (BLOCK, LANES)))
                p_rows.append(jnp.concatenate(p_lanes, axis=1))
                e_rows.append(jnp.where(halves[0], e_sink[0], e_sink[1]))
            p = jnp.concatenate(p_rows, axis=0)
            o = jnp.dot(p, vcat, preferred_element_type=_F32)
            den = o[:, LANES:] + jnp.concatenate(e_rows, axis=0)
            on = o[:, :LANES] / den
            slabs_out += [on[i * BLOCK:(i + 1) * BLOCK] for i in range(PAIRS_PER_KV)]
        o_all = jnp.concatenate(slabs_out, axis=1)
        o_ref[0, c * BLOCK:(c + 1) * BLOCK, :] = (
            o_all * rows(za_ref, c).astype(_F32)).astype(o_ref.dtype)


def _attention(sink, q, k, v, bias, za):
    b, s, _ = q.shape
    seq = lambda w: pl.BlockSpec((1, s, w), lambda i: (i, 0, 0))
    return pl.pallas_call(
        _attn_kernel,
        grid=(b,),
        in_specs=[pl.BlockSpec(memory_space=pltpu.SMEM), seq(D_ATTN), seq(D_KVX), seq(D_KVX),
                  pl.BlockSpec(bias.shape, lambda i: (0, 0, 0)), seq(D_ATTN)],
        out_specs=seq(D_ATTN),
        out_shape=jax.ShapeDtypeStruct((b, s, D_ATTN), _BF16),
        compiler_params=pltpu.CompilerParams(
            dimension_semantics=("arbitrary",), vmem_limit_bytes=VMEM_LIMIT_BYTES),
        name="attn",
    )(sink, q, k, v, bias, za)


def _t5_bucket_np(rel):
    half = NUM_BUCKETS // 2
    ret = (rel > 0).astype(np.int64) * half
    n = np.abs(rel)
    max_exact = half // 2
    nf = np.maximum(n, 1).astype(np.float64)
    large = max_exact + (np.log(nf / max_exact) / math.log(MAX_DISTANCE / max_exact)
                         * (half - max_exact)).astype(np.int64)
    large = np.minimum(large, half - 1)
    return ret + np.where(n < max_exact, n, large)


def _attn_bias(rel_table):
    rel = (np.arange(3 * BLOCK)[None, :] - BLOCK) - np.arange(BLOCK)[:, None]
    onehot = (_t5_bucket_np(rel)[None] == np.arange(NUM_BUCKETS)[:, None, None])
    bias = jnp.einsum('bh,bqk->hqk', LOG2_E * rel_table.astype(_F32),
                      jnp.asarray(onehot, _F32), precision=_HI)
    return jnp.where(jnp.asarray(np.abs(rel) <= WINDOW), bias, NEG_INF)


def _merge_kernel(x_ref, gain_ref, wg_ref, bg_ref, ao_ref, yg_ref, zs_ref, wglu_ref,
                  bglu_ref, wba_ref, wbs_ref, wout_ref, fgain_ref, o_ref, *y_scrs):
    nb, nt = x_ref.shape[0], x_ref.shape[1]
    rows = nb * SUB_STEPS
    for st in range(nt // SUB_STEPS):
        steps = slice(st * SUB_STEPS, (st + 1) * SUB_STEPS)
        x = x_ref[:, steps, :].reshape(rows, D_MODEL)
        h = _rms_normalize(x, gain_ref[...]).astype(_BF16)
        gates2 = _twice_sigmoid_of_double(
            jnp.dot(h, wg_ref[:, W_OFF_GATES - GATE_BLOCK:], preferred_element_type=_F32)
            + bg_ref[...])
        for c in range(SUB_STEPS // CHUNK):
            _from_group_major(yg_ref, y_scrs[st], st * (SUB_STEPS // CHUNK) + c, nb, c * CHUNK,
                              SUB_STEPS)
        y = jnp.concatenate([y_scrs[st][v] for v in range(D_SSM // LANES)], axis=1)
        y2 = _twice_gelu(y)
        glu_half = (jnp.dot(y2.astype(_BF16), wglu_ref[...], preferred_element_type=_F32)
                    + bglu_ref[...])
        ssm4 = (y2 * _twice_sigmoid_of_double(glu_half)
                * zs_ref[:, steps, :].reshape(rows, D_SSM).astype(_F32))
        pa = jnp.dot(ao_ref[:, steps, :].reshape(rows, D_ATTN), wba_ref[...],
                     preferred_element_type=_F32)
        ps = jnp.dot(ssm4.astype(_BF16), wbs_ref[...], preferred_element_type=_F32)
        merged2 = gates2[:, :D_MODEL] * pa + gates2[:, D_MODEL:] * ps
        xn = x + jnp.dot(merged2.astype(_BF16), wout_ref[...], preferred_element_type=_F32)
        o_ref[:, steps, :] = _rms_normalize(xn, fgain_ref[...]).reshape(nb, SUB_STEPS, D_MODEL)


def _merge(x, gain, wg, bg, ao, yg, zs, wglu, bglu, wba, wbs, wout, fgain):
    bsz, s, _ = x.shape
    nt = TOKEN_TILE // bsz
    tok = lambda i: (0, i, 0)
    fixed = lambda i: (0, 0)
    full = lambda a: pl.BlockSpec(a.shape, fixed)
    act = lambda n: pl.BlockSpec((bsz, nt, n), tok)
    return pl.pallas_call(
        _merge_kernel,
        grid=(s // nt,),
        in_specs=[act(D_MODEL), full(gain),
                  pl.BlockSpec((D_MODEL, GATE_BLOCK), lambda i: (0, 1)),
                  full(bg), act(D_ATTN),
                  pl.BlockSpec((N_SSM_GROUPS, nt // CHUNK, bsz, CHUNK_W),
                               lambda i: (0, i, 0, 0)),
                  act(D_SSM), full(wglu), full(bglu), full(wba), full(wbs), full(wout),
                  full(fgain)],
        out_specs=act(D_MODEL),
        out_shape=jax.ShapeDtypeStruct((bsz, s, D_MODEL), _F32),
        scratch_shapes=[pltpu.VMEM((D_SSM // LANES, bsz * SUB_STEPS, LANES), _F32)
                        for _ in range(nt // SUB_STEPS)],
        compiler_params=pltpu.CompilerParams(
            dimension_semantics=("arbitrary",), vmem_limit_bytes=VMEM_LIMIT_BYTES),
        name="merge",
    )(x, gain, wg, bg, ao, yg, zs, wglu, bglu, wba, wbs, wout, fgain)


def _layer(x, norm_gain, w_in, b_gate, attn_sink, a_re, a_im, log_dt, b_re, b_im, c_re, c_im,
           d_skip, w_glu, b_glu, w_ba, w_bs, w_out, out_gain, bias):
    bsz, s, d = x.shape
    assert d == D_MODEL and x.dtype == _F32 and w_in.shape == (D_MODEL, W_OFF_GATES + 2 * D_MODEL)
    assert bsz % 8 == 0 and s % BLOCK == 0
    assert INPROJ_TOKEN_TILE % (bsz * SUB_STEPS) == 0 and s % (INPROJ_TOKEN_TILE // bsz) == 0
    assert TOKEN_TILE % (bsz * SUB_STEPS) == 0 and s % (TOKEN_TILE // bsz) == 0
    n_chunks = s // CHUNK
    gain = norm_gain.reshape(1, D_MODEL).astype(_F32)
    w_bf = w_in.astype(_BF16)
    q, k, v, za, ug, zs = _inproj(x, gain, w_bf)

    tmat, smat, cmat, ar, ai = _ssm_matrices(a_re, a_im, log_dt, b_re, b_im, c_re, c_im,
                                             d_skip)
    yg = _ssm(ug.reshape(N_SSM_GROUPS, n_chunks * bsz, CHUNK_W), tmat, smat, cmat, ar, ai,
              n_chunks, bsz)
    yg = yg.reshape(N_SSM_GROUPS, n_chunks, bsz, CHUNK_W)

    ao = _attention(LOG2_E * attn_sink.astype(_F32), q, k, v, bias, za)

    return _merge(x, 0.5 * gain,
                  w_bf,
                  (0.5 * b_gate).reshape(1, -1).astype(_F32),
                  ao, yg, zs,
                  (0.25 * w_glu).astype(_BF16),
                  (0.5 * b_glu).reshape(1, -1).astype(_F32),
                  w_ba.astype(_BF16),
                  (0.25 * w_bs).astype(_BF16),
                  (0.5 * w_out).astype(_BF16),
                  out_gain.reshape(1, D_MODEL).astype(_F32))


def kernel(x, norm_gain, w_in, b_gate, attn_sink, rel_bias_table, ssm_a_re, ssm_a_im,
           ssm_log_dt, ssm_b_re, ssm_b_im, ssm_c_re, ssm_c_im, ssm_d, w_glu, b_glu,
           w_branch_attn, w_branch_ssm, w_out, final_norm_gain):
    depth = norm_gain.shape[0]
    assert depth == 1, "final norm is fused into the single layer's epilogue"
    bias = _attn_bias(rel_bias_table)
    l = 0
    return _layer(x, norm_gain[l], w_in[l], b_gate[l], attn_sink[l], ssm_a_re[l],
                  ssm_a_im[l], ssm_log_dt[l], ssm_b_re[l], ssm_b_im[l], ssm_c_re[l],
                  ssm_c_im[l], ssm_d[l], w_glu[l], b_glu[l], w_branch_attn[l],
                  w_branch_ssm[l], w_out[l], final_norm_gain, bias)
```

```python
import functools
import math

import jax
import jax.numpy as jnp
import numpy as np
from jax import lax
from jax.experimental import pallas as pl
from jax.experimental.pallas import tpu as pltpu

D_MODEL = 1024
ATTN_HEADS = 8
KV_HEADS = 2
Q_PER_KV = ATTN_HEADS // KV_HEADS
HEAD_DIM = 64
D_ATTN = ATTN_HEADS * HEAD_DIM
D_KV = KV_HEADS * HEAD_DIM
WINDOW = 128
BLOCK = 128
NUM_BUCKETS = 32
MAX_DISTANCE = 128
D_SSM = 512
SSM_GROUP = 16
N_SSM_GROUPS = D_SSM // SSM_GROUP
SSM_STATE = 64
EPS = 1e-6
NEG_INF = -1e30

LANES = 128
CHUNK = 16
CHUNK_W = CHUNK * SSM_GROUP
STATE_W = 4 * SSM_STATE
SLOTS_PER_VREG = LANES // SSM_GROUP
SSM_GROUPS_PER_STEP = 4

HEAD_PAIR = LANES // HEAD_DIM
PAIRS_PER_KV = Q_PER_KV // HEAD_PAIR
assert D_KV == LANES and KV_HEADS == HEAD_PAIR
D_KVX = 2 * D_KV
LOG2_E = math.log2(math.e)
Q_SCALE = HEAD_DIM ** -0.5 * LOG2_E

W_OFF_ZA = D_ATTN + 2 * D_KV
W_OFF_U = W_OFF_ZA + D_ATTN
W_OFF_GATES = W_OFF_U + 2 * D_SSM
GATE_BLOCK = (W_OFF_GATES + 2 * D_MODEL) // 2
assert GATE_BLOCK % LANES == 0 and GATE_BLOCK <= W_OFF_GATES

TOKEN_TILE = 1024
INPROJ_TOKEN_TILE = 2048
SUB_STEPS = CHUNK
VMEM_LIMIT_BYTES = 56 * 1024 * 1024

_F32 = jnp.float32
_BF16 = jnp.bfloat16
_HI = lax.Precision.HIGHEST


def _silu(x):
    return x * (0.5 * jnp.tanh(0.5 * x) + 0.5)


def _twice_sigmoid_of_double(xh):
    return 1.0 + jnp.tanh(xh)


def _twice_gelu(x):
    c = math.sqrt(2.0 / math.pi)
    return x * (1.0 + jnp.tanh(c * (x + 0.044715 * (x * x * x))))


def _rms_normalize(x, gain):
    ms = jnp.mean(x * x, axis=-1, keepdims=True)
    return x * lax.rsqrt(ms + EPS) * gain


def _slot_masks(rows):
    lane = lax.broadcasted_iota(jnp.int32, (rows, LANES), 1)
    return [(lane >= p * SSM_GROUP) & (lane < (p + 1) * SSM_GROUP)
            for p in range(SLOTS_PER_VREG)]


def _to_group_major(u_scr, ug_ref, kk, nb, t0, steps):
    masks = _slot_masks(nb)
    for v in range(D_SSM // LANES):
        for half in range(CHUNK // SLOTS_PER_VREG):
            rolled = []
            for t8 in range(SLOTS_PER_VREG):
                r = t0 + half * SLOTS_PER_VREG + t8
                piece = u_scr[v, pl.ds(r, nb, stride=steps), :]
                rolled.append(pltpu.roll(piece, t8 * SSM_GROUP, 1) if t8 else piece)
            for p0 in range(SLOTS_PER_VREG):
                acc = rolled[0]
                for t8 in range(1, SLOTS_PER_VREG):
                    acc = jnp.where(masks[(p0 + t8) % SLOTS_PER_VREG], rolled[t8], acc)
                g = v * SLOTS_PER_VREG + p0
                ug_ref[g, kk, :, half * LANES:(half + 1) * LANES] = acc.astype(_BF16)


def _from_group_major(yg_ref, y_scr, kk, nb, t0, steps):
    masks = _slot_masks(nb)
    for v in range(D_SSM // LANES):
        for half in range(CHUNK // SLOTS_PER_VREG):
            src = [yg_ref[v * SLOTS_PER_VREG + p0, kk, :,
                          half * LANES:(half + 1) * LANES].astype(_F32)
                   for p0 in range(SLOTS_PER_VREG)]
            for t8 in range(SLOTS_PER_VREG):
                acc = src[0]
                for p0 in range(1, SLOTS_PER_VREG):
                    acc = jnp.where(masks[(p0 + t8) % SLOTS_PER_VREG], src[p0], acc)
                if t8:
                    acc = pltpu.roll(acc, LANES - t8 * SSM_GROUP, 1)
                r = t0 + half * SLOTS_PER_VREG + t8
                y_scr[v, pl.ds(r, nb, stride=steps), :] = acc


def _inproj_kernel(x_ref, gain_ref, w_ref, q_ref, k_ref, v_ref, za_ref, ug_ref, zs_ref,
                   *u_scrs):
    nb, nt = x_ref.shape[0], x_ref.shape[1]
    rows = nb * SUB_STEPS
    for st in range(nt // SUB_STEPS):
        steps = slice(st * SUB_STEPS, (st + 1) * SUB_STEPS)
        x = x_ref[:, steps, :].reshape(rows, D_MODEL)
        h = _rms_normalize(x, gain_ref[...]).astype(_BF16)

        def put(ref, val):
            ref[:, steps, :] = val.astype(_BF16).reshape(nb, SUB_STEPS, val.shape[-1])

        proj = lambda lo, n: jnp.dot(h, w_ref[:, lo:lo + n], preferred_element_type=_F32)
        u_zs = proj(W_OFF_U, 2 * D_SSM)
        for v in range(D_SSM // LANES):
            u_scrs[st][v] = u_zs[:, v * LANES:(v + 1) * LANES]
        for c in range(SUB_STEPS // CHUNK):
            _to_group_major(u_scrs[st], ug_ref, st * (SUB_STEPS // CHUNK) + c, nb, c * CHUNK,
                            SUB_STEPS)
        put(zs_ref, _silu(u_zs[:, D_SSM:]))
        put(za_ref, _silu(proj(W_OFF_ZA, D_ATTN)))
        qkv = proj(0, W_OFF_ZA)
        put(q_ref, qkv[:, :D_ATTN] * Q_SCALE)
        with_swap = lambda a: jnp.concatenate([a, pltpu.roll(a, HEAD_DIM, 1)], axis=1)
        put(k_ref, with_swap(qkv[:, D_ATTN:D_ATTN + D_KV]))
        put(v_ref, with_swap(qkv[:, D_ATTN + D_KV:]))


def _inproj(x, gain, w_in):
    bsz, s, _ = x.shape
    nt = INPROJ_TOKEN_TILE // bsz
    tok = lambda i: (0, i, 0)
    fixed = lambda i: (0, 0)
    outs = [D_ATTN, D_KVX, D_KVX, D_ATTN, None, D_SSM]
    act = lambda n: (pl.BlockSpec((bsz, nt, n), tok), jax.ShapeDtypeStruct((bsz, s, n), _BF16))
    ug = (pl.BlockSpec((N_SSM_GROUPS, nt // CHUNK, bsz, CHUNK_W), lambda i: (0, i, 0, 0)),
          jax.ShapeDtypeStruct((N_SSM_GROUPS, s // CHUNK, bsz, CHUNK_W), _BF16))
    specs, shapes = zip(*[ug if n is None else act(n) for n in outs])
    return pl.pallas_call(
        _inproj_kernel,
        grid=(s // nt,),
        in_specs=[pl.BlockSpec((bsz, nt, D_MODEL), tok),
                  pl.BlockSpec((1, D_MODEL), fixed)]
                 + [pl.BlockSpec((D_MODEL, W_OFF_GATES), fixed)],
        out_specs=list(specs),
        out_shape=list(shapes),
        scratch_shapes=[pltpu.VMEM((D_SSM // LANES, bsz * SUB_STEPS, LANES), _F32)
                        for _ in range(nt // SUB_STEPS)],
        compiler_params=pltpu.CompilerParams(
            dimension_semantics=("arbitrary",), vmem_limit_bytes=VMEM_LIMIT_BYTES),
        name="inproj",
    )(x, gain, w_in)


def _ssm_kernel(n_chunks, rows_per_chunk, u_ref, t_ref, s_ref, c_ref, ar_ref, ai_ref,
                y_ref, s_scr, x_scr):
    rb = rows_per_chunk
    groups = range(u_ref.shape[0])
    for gi in groups:
        s_scr[gi] = jnp.dot(u_ref[gi], s_ref[gi], preferred_element_type=_F32)
    ar = [jnp.broadcast_to(ar_ref[gi], (rb, 2 * SSM_STATE)) for gi in groups]
    ai = [jnp.broadcast_to(ai_ref[gi], (rb, 2 * SSM_STATE)) for gi in groups]
    is_fwd = lax.broadcasted_iota(jnp.int32, (rb, 2 * SSM_STATE), 1) < SSM_STATE
    is_bwd = jnp.logical_not(is_fwd)
    re_cols = pl.ds(0, 2 * SSM_STATE)
    im_cols = pl.ds(2 * SSM_STATE, 2 * SSM_STATE)

    def step(i, carry):
        rows_f = pl.ds(pl.multiple_of(i * rb, rb), rb)
        rows_b = pl.ds(pl.multiple_of((n_chunks - 1 - i) * rb, rb), rb)
        new = []
        for gi in groups:
            st_re, st_im = carry[2 * gi], carry[2 * gi + 1]
            pltpu.store(x_scr.at[gi, rows_f, re_cols], st_re, mask=is_fwd)
            pltpu.store(x_scr.at[gi, rows_f, im_cols], st_im, mask=is_fwd)
            pltpu.store(x_scr.at[gi, rows_b, re_cols], st_re, mask=is_bwd)
            pltpu.store(x_scr.at[gi, rows_b, im_cols], st_im, mask=is_bwd)
            in_re = jnp.where(is_fwd, s_scr[gi, rows_f, re_cols], s_scr[gi, rows_b, re_cols])
            in_im = jnp.where(is_fwd, s_scr[gi, rows_f, im_cols], s_scr[gi, rows_b, im_cols])
            new.append(ar[gi] * st_re - ai[gi] * st_im + in_re)
            new.append(ar[gi] * st_im + ai[gi] * st_re + in_im)
        return tuple(new)

    zero = jnp.zeros((rb, 2 * SSM_STATE), _F32)
    lax.fori_loop(0, n_chunks, step, (zero,) * (2 * len(groups)), unroll=True)

    for gi in groups:
        y = jnp.dot(u_ref[gi], t_ref[gi], preferred_element_type=_F32)
        y = y + lax.dot_general(x_scr[gi].astype(_BF16), c_ref[gi], (((1,), (1,)), ((), ())),
                                preferred_element_type=_F32)
        y_ref[gi] = y.astype(y_ref.dtype)


def _ssm(ug, tmat, smat, cmat, ar, ai, n_chunks, rows_per_chunk):
    g, rows, _ = ug.shape
    gb = SSM_GROUPS_PER_STEP
    per_group = lambda i: (i, 0, 0)
    return pl.pallas_call(
        functools.partial(_ssm_kernel, n_chunks, rows_per_chunk),
        grid=(g // gb,),
        in_specs=[pl.BlockSpec((gb, rows, CHUNK_W), per_group),
                  pl.BlockSpec((gb, CHUNK_W, CHUNK_W), per_group),
                  pl.BlockSpec((gb, CHUNK_W, STATE_W), per_group),
                  pl.BlockSpec((gb, CHUNK_W, STATE_W), per_group),
                  pl.BlockSpec((gb, 1, 2 * SSM_STATE), per_group),
                  pl.BlockSpec((gb, 1, 2 * SSM_STATE), per_group)],
        out_specs=pl.BlockSpec((gb, rows, CHUNK_W), per_group),
        out_shape=jax.ShapeDtypeStruct((g, rows, CHUNK_W), _BF16),
        scratch_shapes=[pltpu.VMEM((gb, rows, STATE_W), _F32),
                        pltpu.VMEM((gb, rows, STATE_W), _F32)],
        compiler_params=pltpu.CompilerParams(
            dimension_semantics=("arbitrary",), vmem_limit_bytes=VMEM_LIMIT_BYTES),
        name="ssm",
    )(ug, tmat, smat, cmat, ar, ai)


def _ssm_matrices(a_re, a_im, log_dt, b_re, b_im, c_re, c_im, d_skip):
    L, G, P, C = CHUNK, N_SSM_GROUPS, SSM_STATE, SSM_GROUP
    both = lambda a: jnp.concatenate([a[0], a[1]], axis=-1)
    dt = jnp.broadcast_to(jnp.exp(log_dt)[..., None], a_re.shape)
    lam = jnp.stack([both(a_re), both(a_im), both(dt)], axis=1)
    bt = jnp.stack([b_re, b_im], axis=0).transpose(2, 0, 4, 1, 3).reshape(G, 2, C, 2 * P)
    ct = jnp.stack([c_re, c_im], axis=0).transpose(2, 0, 3, 1, 4).reshape(G, 2, C, 2 * P)
    dvec = jnp.tile(d_skip.reshape(G, 1, C), (1, 1, L))
    gb = SLOTS_PER_VREG
    per_group = lambda *blk: pl.BlockSpec((gb,) + blk, lambda i: (i,) + (0,) * len(blk))
    mat = jax.ShapeDtypeStruct((G, L * C, L * C), _BF16)
    vec = jax.ShapeDtypeStruct((G, 1, 2 * P), _F32)
    return pl.pallas_call(
        _ssm_prep_kernel,
        grid=(G // gb,),
        in_specs=[per_group(3, 2 * P), per_group(2, C, 2 * P), per_group(2, C, 2 * P),
                  per_group(1, L * C)],
        out_specs=[per_group(L * C, L * C), per_group(L * C, STATE_W),
                   per_group(L * C, STATE_W), per_group(1, 2 * P), per_group(1, 2 * P)],
        out_shape=[mat, mat, mat, vec, vec],
        compiler_params=pltpu.CompilerParams(dimension_semantics=("arbitrary",)),
        name="ssm_prep",
    )(lam, bt, ct, dvec)


def _ssm_prep_kernel(lam_ref, bt_ref, ct_ref, dvec_ref, t_ref, s_ref, c_ref, ar_ref, ai_ref):
    for gi in range(lam_ref.shape[0]):
        _ssm_prep_group(gi, lam_ref, bt_ref, ct_ref, dvec_ref, t_ref, s_ref, c_ref, ar_ref,
                        ai_ref)


def _ssm_prep_group(gi, lam_ref, bt_ref, ct_ref, dvec_ref, t_ref, s_ref, c_ref, ar_ref,
                    ai_ref):
    L, P, C = CHUNK, SSM_STATE, SSM_GROUP
    g8 = gi % SLOTS_PER_VREG

    def time_of_slot(slot):
        return (jnp.bitwise_and(slot, -SLOTS_PER_VREG)
                + jnp.bitwise_and(slot - g8, SLOTS_PER_VREG - 1))

    def cmul(x_re, x_im, y_re, y_im):
        return x_re * y_re - x_im * y_im, x_re * y_im + x_im * y_re

    a_re, a_im, dt = lam_ref[gi, 0:1], lam_ref[gi, 1:2], lam_ref[gi, 2:3]
    lr, li = a_re * dt, a_im * dt

    def cpow(tau):
        mag, ang = jnp.exp(lr * tau), li * tau
        return mag * jnp.cos(ang), mag * jnp.sin(ang)

    ab_re, ab_im = cpow(jnp.ones_like(lr))
    den = a_re * a_re + a_im * a_im
    co_re = ((ab_re - 1.0) * a_re + ab_im * a_im) / den
    co_im = (ab_im * a_re - (ab_re - 1.0) * a_im) / den
    bb_re, bb_im = cmul(bt_ref[gi, 0], bt_ref[gi, 1], co_re, co_im)
    cc_re, cc_im = ct_ref[gi, 0], ct_ref[gi, 1]

    def outer(p, m):
        (p_re, p_im), (m_re, m_im) = p, m
        blocks = [cmul(p_re[s:s + 1], p_im[s:s + 1], m_re, m_im) for s in range(L)]
        return (jnp.concatenate([b[0] for b in blocks], axis=0),
                jnp.concatenate([b[1] for b in blocks], axis=0))

    j = time_of_slot(lax.broadcasted_iota(jnp.int32, (L, 2 * P), 0)).astype(_F32)
    fwd = lax.broadcasted_iota(jnp.int32, (L, 2 * P), 1) < P
    s_re, s_im = outer(cpow(jnp.where(fwd, L - 1 - j, j)), (bb_re, bb_im))
    s_ref[gi] = jnp.concatenate([s_re, s_im], axis=1).astype(_BF16)
    w_re, w_im = outer(cpow(jnp.where(fwd, j + 1, L - j)), (cc_re, cc_im))
    wcat = jnp.concatenate([w_re, -w_im], axis=1)
    c_ref[gi] = wcat.astype(_BF16)
    pos = lax.broadcasted_iota(jnp.int32, (L, 2 * P), 0).astype(_F32)
    a_re2, a_im2 = outer(cpow(jnp.where(fwd, pos, L - 1 - pos)), (cc_re, cc_im))
    acat = jnp.concatenate([a_re2, -a_im2], axis=1)
    bcat = jnp.concatenate([bb_re, bb_im], axis=1)
    is_fwd_state = jnp.bitwise_and(lax.broadcasted_iota(jnp.int32, bcat.shape, 1), P) == 0

    def split(a):
        hi = a.astype(_BF16).astype(_F32)
        return hi, a - hi

    def dot_state(a_parts, b_parts):
        (a_hi, a_lo), (b_hi, b_lo) = [[p.astype(_BF16) for p in parts]
                                      for parts in (a_parts, b_parts)]
        d = lambda a, b: lax.dot_general(a, b, (((1,), (1,)), ((), ())),
                                         preferred_element_type=_F32)
        return d(a_hi, b_hi) + d(a_hi, b_lo) + d(a_lo, b_hi)

    b_parts, a_parts = split(bcat), split(acat)
    kf = dot_state([jnp.where(is_fwd_state, p, 0.0) for p in b_parts], a_parts)
    kb = dot_state([jnp.where(is_fwd_state, 0.0, p) for p in b_parts], a_parts)

    lane = lax.broadcasted_iota(jnp.int32, (C, LANES), 1)

    def shift_slots(k, n):
        halves = [k[:, :LANES], k[:, LANES:]]
        if n >= SLOTS_PER_VREG:
            halves, n = halves[::-1], n - SLOTS_PER_VREG
        if n == 0:
            return halves
        r = [pltpu.roll(h, n * C, 1) for h in halves]
        wrapped = lane < n * C
        return [jnp.where(wrapped, r[1], r[0]), jnp.where(wrapped, r[0], r[1])]

    col_t = lax.broadcasted_iota(jnp.int32, (C, L * C), 1)
    blocks = []
    for jt in range(L):
        fwd_part = jnp.concatenate(shift_slots(kf, jt), axis=1)
        bwd_part = jnp.concatenate(shift_slots(kb, (jt + 1) % L), axis=1)
        blk = (jnp.where(col_t >= jt * C, fwd_part, 0.0)
               + jnp.where(col_t < (jt + 1) * C, bwd_part, 0.0))
        blocks.append(jnp.concatenate(
            [pltpu.roll(blk[:, h * LANES:(h + 1) * LANES], g8 * C, 1) if g8 else
             blk[:, h * LANES:(h + 1) * LANES] for h in range(L * C // LANES)], axis=1))
    time_of = lambda s: (s // SLOTS_PER_VREG) * SLOTS_PER_VREG + (s - g8) % SLOTS_PER_VREG
    tm = jnp.concatenate([blocks[time_of(s)] for s in range(L)], axis=0)
    row = lax.broadcasted_iota(jnp.int32, (L * C, L * C), 0)
    col = lax.broadcasted_iota(jnp.int32, (L * C, L * C), 1)
    t_ref[gi] = (tm + jnp.where(row == col, dvec_ref[gi], 0.0)).astype(_BF16)
    al_re, al_im = cpow(jnp.full_like(lr, float(L)))
    ar_ref[gi] = al_re
    ai_ref[gi] = al_im


def _attn_kernel(sink_ref, q_ref, k_ref, v_ref, bias_ref, za_ref, o_ref):
    nq = q_ref.shape[1] // BLOCK
    rows = lambda ref, c: ref[0, c * BLOCK:(c + 1) * BLOCK]
    k_blocks = [rows(k_ref, c) for c in range(nq)]
    v_blocks = [rows(v_ref, c) for c in range(nq)]
    lane = lax.broadcasted_iota(jnp.int32, (BLOCK, LANES), 1)
    halves = [lane < HEAD_DIM, lane >= HEAD_DIM]
    keep = [h.astype(_F32).astype(_BF16) for h in halves]
    slab = lambda blk, j, e: blk[:, (j ^ e) * LANES:((j ^ e) + 1) * LANES]
    k_half = {(j, e): [slab(kb, j, e) * keep[e] for kb in k_blocks]
              for j in range(KV_HEADS) for e in range(HEAD_PAIR)}
    v_half = {(j, e): [jnp.concatenate([slab(vb, j, e) * keep[e], keep[e]], axis=1)
                       for vb in v_blocks]
              for j in range(KV_HEADS) for e in range(HEAD_PAIR)}
    for c in range(nq):
        blocks = [b for b in (c - 1, c, c + 1) if 0 <= b < nq]
        slot0 = blocks[0] - (c - 1)
        nk = len(blocks) * BLOCK
        q = rows(q_ref, c)
        slabs_out = []
        for j in range(KV_HEADS):
            window = lambda halves_of: jnp.concatenate(
                [halves_of[j, e][b] for e in range(HEAD_PAIR) for b in blocks], axis=0)
            kcat = window(k_half)
            vcat = window(v_half)
            qs = jnp.concatenate(
                [q[:, (j * PAIRS_PER_KV + i) * LANES:(j * PAIRS_PER_KV + i + 1) * LANES]
                 for i in range(PAIRS_PER_KV)], axis=0)
            s = lax.dot_general(qs, kcat, (((1,), (1,)), ((), ())),
                                preferred_element_type=_F32)
            p_rows, e_rows = [], []
            for i in range(PAIRS_PER_KV):
                p_lanes, e_sink = [], []
                for e in range(HEAD_PAIR):
                    head = j * Q_PER_KV + i * HEAD_PAIR + e
                    sg = (s[i * BLOCK:(i + 1) * BLOCK, e * nk:(e + 1) * nk]
                          + bias_ref[head, :, slot0 * BLOCK:slot0 * BLOCK + nk])
                    sk = sink_ref[head]
                    m = jnp.maximum(jnp.max(sg, axis=-1, keepdims=True), sk)
                    p_lanes.append(jnp.exp2(sg - m).astype(_BF16))
                    e_sink.append(jnp.broadcast_to(jnp.exp2(sk - m), (BLOCK, LANES)))
                p_rows.append(jnp.concatenate(p_lanes, axis=1))
                e_rows.append(jnp.where(halves[0], e_sink[0], e_sink[1]))
            p = jnp.concatenate(p_rows, axis=0)
            o = jnp.dot(p, vcat, preferred_element_type=_F32)
            den = o[:, LANES:] + jnp.concatenate(e_rows, axis=0)
            on = o[:, :LANES] / den
            slabs_out += [on[i * BLOCK:(i + 1) * BLOCK] for i in range(PAIRS_PER_KV)]
        o_all = jnp.concatenate(slabs_out, axis=1)
        o_ref[0, c * BLOCK:(c + 1) * BLOCK, :] = (
            o_all * rows(za_ref, c).astype(_F32)).astype(o_ref.dtype)


def _attention(sink, q, k, v, bias, za):
    b, s, _ = q.shape
    seq = lambda w: pl.BlockSpec((1, s, w), lambda i: (i, 0, 0))
    return pl.pallas_call(
        _attn_kernel,
        grid=(b,),
        in_specs=[pl.BlockSpec(memory_space=pltpu.SMEM), seq(D_ATTN), seq(D_KVX), seq(D_KVX),
                  pl.BlockSpec(bias.shape, lambda i: (0, 0, 0)), seq(D_ATTN)],
        out_specs=seq(D_ATTN),
        out_shape=jax.ShapeDtypeStruct((b, s, D_ATTN), _BF16),
        compiler_params=pltpu.CompilerParams(
            dimension_semantics=("arbitrary",), vmem_limit_bytes=VMEM_LIMIT_BYTES),
        name="attn",
    )(sink, q, k, v, bias, za)


def _t5_bucket_np(rel):
    half = NUM_BUCKETS // 2
    ret = (rel > 0).astype(np.int64) * half
    n = np.abs(rel)
    max_exact = half // 2
    nf = np.maximum(n, 1).astype(np.float64)
    large = max_exact + (np.log(nf / max_exact) / math.log(MAX_DISTANCE / max_exact)
                         * (half - max_exact)).astype(np.int64)
    large = np.minimum(large, half - 1)
    return ret + np.where(n < max_exact, n, large)


def _attn_bias(rel_table):
    rel = (np.arange(3 * BLOCK)[None, :] - BLOCK) - np.arange(BLOCK)[:, None]
    onehot = (_t5_bucket_np(rel)[None] == np.arange(NUM_BUCKETS)[:, None, None])
    bias = jnp.einsum('bh,bqk->hqk', LOG2_E * rel_table.astype(_F32),
                      jnp.asarray(onehot, _F32), precision=_HI)
    return jnp.where(jnp.asarray(np.abs(rel) <= WINDOW), bias, NEG_INF)


def _merge_kernel(x_ref, gain_ref, wg_ref, bg_ref, ao_ref, yg_ref, zs_ref, wglu_ref,
                  bglu_ref, wba_ref, wbs_ref, wout_ref, fgain_ref, o_ref, *y_scrs):
    nb, nt = x_ref.shape[0], x_ref.shape[1]
    rows = nb * SUB_STEPS
    for st in range(nt // SUB_STEPS):
        steps = slice(st * SUB_STEPS, (st + 1) * SUB_STEPS)
        x = x_ref[:, steps, :].reshape(rows, D_MODEL)
        h = _rms_normalize(x, gain_ref[...]).astype(_BF16)
        gates2 = _twice_sigmoid_of_double(
            jnp.dot(h, wg_ref[:, W_OFF_GATES - GATE_BLOCK:], preferred_element_type=_F32)
            + bg_ref[...])
        for c in range(SUB_STEPS // CHUNK):
            _from_group_major(yg_ref, y_scrs[st], st * (SUB_STEPS // CHUNK) + c, nb, c * CHUNK,
                              SUB_STEPS)
        y = jnp.concatenate([y_scrs[st][v] for v in range(D_SSM // LANES)], axis=1)
        y2 = _twice_gelu(y)
        glu_half = (jnp.dot(y2.astype(_BF16), wglu_ref[...], preferred_element_type=_F32)
                    + bglu_ref[...])
        ssm4 = (y2 * _twice_sigmoid_of_double(glu_half)
                * zs_ref[:, steps, :].reshape(rows, D_SSM).astype(_F32))
        pa = jnp.dot(ao_ref[:, steps, :].reshape(rows, D_ATTN), wba_ref[...],
                     preferred_element_type=_F32)
        ps = jnp.dot(ssm4.astype(_BF16), wbs_ref[...], preferred_element_type=_F32)
        merged2 = gates2[:, :D_MODEL] * pa + gates2[:, D_MODEL:] * ps
        xn = x + jnp.dot(merged2.astype(_BF16), wout_ref[...], preferred_element_type=_F32)
        o_ref[:, steps, :] = _rms_normalize(xn, fgain_ref[...]).reshape(nb, SUB_STEPS, D_MODEL)


def _merge(x, gain, wg, bg, ao, yg, zs, wglu, bglu, wba, wbs, wout, fgain):
    bsz, s, _ = x.shape
    nt = TOKEN_TILE // bsz
    tok = lambda i: (0, i, 0)
    fixed = lambda i: (0, 0)
    full = lambda a: pl.BlockSpec(a.shape, fixed)
    act = lambda n: pl.BlockSpec((bsz, nt, n), tok)
    return pl.pallas_call(
        _merge_kernel,
        grid=(s // nt,),
        in_specs=[act(D_MODEL), full(gain),
                  pl.BlockSpec((D_MODEL, GATE_BLOCK), lambda i: (0, 1)),
                  full(bg), act(D_ATTN),
                  pl.BlockSpec((N_SSM_GROUPS, nt // CHUNK, bsz, CHUNK_W),
                               lambda i: (0, i, 0, 0)),
                  act(D_SSM), full(wglu), full(bglu), full(wba), full(wbs), full(wout),
                  full(fgain)],
        out_specs=act(D_MODEL),
        out_shape=jax.ShapeDtypeStruct((bsz, s, D_MODEL), _F32),
        scratch_shapes=[pltpu.VMEM((D_SSM // LANES, bsz * SUB_STEPS, LANES), _F32)
                        for _ in range(nt // SUB_STEPS)],
        compiler_params=pltpu.CompilerParams(
            dimension_semantics=("arbitrary",), vmem_limit_bytes=VMEM_LIMIT_BYTES),
        name="merge",
    )(x, gain, wg, bg, ao, yg, zs, wglu, bglu, wba, wbs, wout, fgain)


def _layer(x, norm_gain, w_in, b_gate, attn_sink, a_re, a_im, log_dt, b_re, b_im, c_re, c_im,
           d_skip, w_glu, b_glu, w_ba, w_bs, w_out, out_gain, bias):
    bsz, s, d = x.shape
    assert d == D_MODEL and x.dtype == _F32 and w_in.shape == (D_MODEL, W_OFF_GATES + 2 * D_MODEL)
    assert bsz % 8 == 0 and s % BLOCK == 0
    assert INPROJ_TOKEN_TILE % (bsz * SUB_STEPS) == 0 and s % (INPROJ_TOKEN_TILE // bsz) == 0
    assert TOKEN_TILE % (bsz * SUB_STEPS) == 0 and s % (TOKEN_TILE // bsz) == 0
    n_chunks = s // CHUNK
    gain = norm_gain.reshape(1, D_MODEL).astype(_F32)
    w_bf = w_in.astype(_BF16)
    q, k, v, za, ug, zs = _inproj(x, gain, w_bf)

    tmat, smat, cmat, ar, ai = _ssm_matrices(a_re, a_im, log_dt, b_re, b_im, c_re, c_im,
                                             d_skip)
    yg = _ssm(ug.reshape(N_SSM_GROUPS, n_chunks * bsz, CHUNK_W), tmat, smat, cmat, ar, ai,
              n_chunks, bsz)
    yg = yg.reshape(N_SSM_GROUPS, n_chunks, bsz, CHUNK_W)

    ao = _attention(LOG2_E * attn_sink.astype(_F32), q, k, v, bias, za)

    return _merge(x, 0.5 * gain,
                  w_bf,
                  (0.5 * b_gate).reshape(1, -1).astype(_F32),
                  ao, yg, zs,
                  (0.25 * w_glu).astype(_BF16),
                  (0.5 * b_glu).reshape(1, -1).astype(_F32),
                  w_ba.astype(_BF16),
                  (0.25 * w_bs).astype(_BF16),
                  (0.5 * w_out).astype(_BF16),
                  out_gain.reshape(1, D_MODEL).astype(_F32))


def kernel(x, norm_gain, w_in, b_gate, attn_sink, rel_bias_table, ssm_a_re, ssm_a_im,
           ssm_log_dt, ssm_b_re, ssm_b_im, ssm_c_re, ssm_c_im, ssm_d, w_glu, b_glu,
           w_branch_attn, w_branch_ssm, w_out, final_norm_gain):
    depth = norm_gain.shape[0]
    assert depth == 1, "final norm is fused into the single layer's epilogue"
    bias = _attn_bias(rel_bias_table)
    l = 0
    return _layer(x, norm_gain[l], w_in[l], b_gate[l], attn_sink[l], ssm_a_re[l],
                  ssm_a_im[l], ssm_log_dt[l], ssm_b_re[l], ssm_b_im[l], ssm_c_re[l],
                  ssm_c_im[l], ssm_d[l], w_glu[l], b_glu[l], w_branch_attn[l],
                  w_branch_ssm[l], w_out[l], final_norm_gain, bias)
```

```python
import functools
import math

import jax
import jax.numpy as jnp
import numpy as np
from jax import lax
from jax.experimental import pallas as pl
from jax.experimental.pallas import tpu as pltpu

D_MODEL = 1024
ATTN_HEADS = 8
KV_HEADS = 2
Q_PER_KV = ATTN_HEADS // KV_HEADS
HEAD_DIM = 64
D_ATTN = ATTN_HEADS * HEAD_DIM
D_KV = KV_HEADS * HEAD_DIM
WINDOW = 128
BLOCK = 128
NUM_BUCKETS = 32
MAX_DISTANCE = 128
D_SSM = 512
SSM_GROUP = 16
N_SSM_GROUPS = D_SSM // SSM_GROUP
SSM_STATE = 64
EPS = 1e-6
NEG_INF = -1e30

LANES = 128
MXU_WIDTH = 256
CHUNK = MXU_WIDTH // SSM_GROUP
CHUNK_W = CHUNK * SSM_GROUP
STATE_W = 4 * SSM_STATE
SLOTS_PER_VREG = LANES // SSM_GROUP
SSM_GROUPS_PER_STEP = 4

HEAD_PAIR = LANES // HEAD_DIM
PAIRS_PER_KV = Q_PER_KV // HEAD_PAIR
assert D_KV == LANES and KV_HEADS == HEAD_PAIR
D_KVX = 2 * D_KV
LOG2_E = math.log2(math.e)
Q_SCALE = HEAD_DIM ** -0.5 * LOG2_E

W_OFF_ZA = D_ATTN + 2 * D_KV
W_OFF_U = W_OFF_ZA + D_ATTN
W_OFF_GATES = W_OFF_U + 2 * D_SSM
GATE_BLOCK = (W_OFF_GATES + 2 * D_MODEL) // 2
assert GATE_BLOCK % LANES == 0 and GATE_BLOCK <= W_OFF_GATES

TOKEN_TILE = 1024
INPROJ_TOKEN_TILE = 2048
SUB_STEPS = CHUNK
VMEM_LIMIT_BYTES = 56 * 1024 * 1024

_F32 = jnp.float32
_BF16 = jnp.bfloat16
_HI = lax.Precision.HIGHEST


def _silu(x):
    return x * (0.5 * jnp.tanh(0.5 * x) + 0.5)


def _twice_sigmoid_of_double(xh):
    return 1.0 + jnp.tanh(xh)


def _twice_gelu(x):
    c = math.sqrt(2.0 / math.pi)
    return x * (1.0 + jnp.tanh(c * (x + 0.044715 * (x * x * x))))


def _rms_normalize(x, gain):
    ms = jnp.mean(x * x, axis=-1, keepdims=True)
    return x * lax.rsqrt(ms + EPS) * gain


def _slot_masks(rows):
    lane = lax.broadcasted_iota(jnp.int32, (rows, LANES), 1)
    return [(lane >= p * SSM_GROUP) & (lane < (p + 1) * SSM_GROUP)
            for p in range(SLOTS_PER_VREG)]


def _to_group_major(u_scr, ug_ref, kk, nb, t0, steps):
    masks = _slot_masks(nb)
    for v in range(D_SSM // LANES):
        for half in range(CHUNK // SLOTS_PER_VREG):
            rolled = []
            for t8 in range(SLOTS_PER_VREG):
                r = t0 + half * SLOTS_PER_VREG + t8
                piece = u_scr[v, pl.ds(r, nb, stride=steps), :]
                rolled.append(pltpu.roll(piece, t8 * SSM_GROUP, 1) if t8 else piece)
            for p0 in range(SLOTS_PER_VREG):
                acc = rolled[0]
                for t8 in range(1, SLOTS_PER_VREG):
                    acc = jnp.where(masks[(p0 + t8) % SLOTS_PER_VREG], rolled[t8], acc)
                g = v * SLOTS_PER_VREG + p0
                ug_ref[g, kk, :, half * LANES:(half + 1) * LANES] = acc.astype(_BF16)


def _from_group_major(yg_ref, y_scr, kk, nb, t0, steps):
    masks = _slot_masks(nb)
    for v in range(D_SSM // LANES):
        for half in range(CHUNK // SLOTS_PER_VREG):
            src = [yg_ref[v * SLOTS_PER_VREG + p0, kk, :,
                          half * LANES:(half + 1) * LANES].astype(_F32)
                   for p0 in range(SLOTS_PER_VREG)]
            for t8 in range(SLOTS_PER_VREG):
                acc = src[0]
                for p0 in range(1, SLOTS_PER_VREG):
                    acc = jnp.where(masks[(p0 + t8) % SLOTS_PER_VREG], src[p0], acc)
                if t8:
                    acc = pltpu.roll(acc, LANES - t8 * SSM_GROUP, 1)
                r = t0 + half * SLOTS_PER_VREG + t8
                y_scr[v, pl.ds(r, nb, stride=steps), :] = acc


def _inproj_kernel(x_ref, gain_ref, w_ref, q_ref, k_ref, v_ref, za_ref, ug_ref, zs_ref,
                   *u_scrs):
    nb, nt = x_ref.shape[0], x_ref.shape[1]
    rows = nb * SUB_STEPS
    for st in range(nt // SUB_STEPS):
        steps = slice(st * SUB_STEPS, (st + 1) * SUB_STEPS)
        x = x_ref[:, steps, :].reshape(rows, D_MODEL)
        h = _rms_normalize(x, gain_ref[...]).astype(_BF16)

        def put(ref, val):
            ref[:, steps, :] = val.astype(_BF16).reshape(nb, SUB_STEPS, val.shape[-1])

        proj = lambda lo, n: jnp.dot(h, w_ref[:, lo:lo + n], preferred_element_type=_F32)
        u_zs = proj(W_OFF_U, 2 * D_SSM)
        for v in range(D_SSM // LANES):
            u_scrs[st][v] = u_zs[:, v * LANES:(v + 1) * LANES]
        for c in range(SUB_STEPS // CHUNK):
            _to_group_major(u_scrs[st], ug_ref, st * (SUB_STEPS // CHUNK) + c, nb, c * CHUNK,
                            SUB_STEPS)
        put(zs_ref, _silu(u_zs[:, D_SSM:]))
        put(za_ref, _silu(proj(W_OFF_ZA, D_ATTN)))
        qkv = proj(0, W_OFF_ZA)
        put(q_ref, qkv[:, :D_ATTN] * Q_SCALE)
        with_swap = lambda a: jnp.concatenate([a, pltpu.roll(a, HEAD_DIM, 1)], axis=1)
        put(k_ref, with_swap(qkv[:, D_ATTN:D_ATTN + D_KV]))
        put(v_ref, with_swap(qkv[:, D_ATTN + D_KV:]))


def _inproj(x, gain, w_in):
    bsz, s, _ = x.shape
    nt = INPROJ_TOKEN_TILE // bsz
    tok = lambda i: (0, i, 0)
    fixed = lambda i: (0, 0)
    outs = [D_ATTN, D_KVX, D_KVX, D_ATTN, None, D_SSM]
    act = lambda n: (pl.BlockSpec((bsz, nt, n), tok), jax.ShapeDtypeStruct((bsz, s, n), _BF16))
    ug = (pl.BlockSpec((N_SSM_GROUPS, nt // CHUNK, bsz, CHUNK_W), lambda i: (0, i, 0, 0)),
          jax.ShapeDtypeStruct((N_SSM_GROUPS, s // CHUNK, bsz, CHUNK_W), _BF16))
    specs, shapes = zip(*[ug if n is None else act(n) for n in outs])
    return pl.pallas_call(
        _inproj_kernel,
        grid=(s // nt,),
        in_specs=[pl.BlockSpec((bsz, nt, D_MODEL), tok),
                  pl.BlockSpec((1, D_MODEL), fixed)]
                 + [pl.BlockSpec((D_MODEL, W_OFF_GATES), fixed)],
        out_specs=list(specs),
        out_shape=list(shapes),
        scratch_shapes=[pltpu.VMEM((D_SSM // LANES, bsz * SUB_STEPS, LANES), _F32)
                        for _ in range(nt // SUB_STEPS)],
        compiler_params=pltpu.CompilerParams(
            dimension_semantics=("arbitrary",), vmem_limit_bytes=VMEM_LIMIT_BYTES),
        name="inproj",
    )(x, gain, w_in)


def _ssm_kernel(n_chunks, rows_per_chunk, u_ref, t_ref, s_ref, c_ref, ar_ref, ai_ref,
                y_ref, s_scr, x_scr):
    rb = rows_per_chunk
    groups = range(u_ref.shape[0])
    for gi in groups:
        s_scr[gi] = jnp.dot(u_ref[gi], s_ref[gi], preferred_element_type=_F32)
    ar = [jnp.broadcast_to(ar_ref[gi], (rb, 2 * SSM_STATE)) for gi in groups]
    ai = [jnp.broadcast_to(ai_ref[gi], (rb, 2 * SSM_STATE)) for gi in groups]
    is_fwd = lax.broadcasted_iota(jnp.int32, (rb, 2 * SSM_STATE), 1) < SSM_STATE
    is_bwd = jnp.logical_not(is_fwd)
    re_cols = pl.ds(0, 2 * SSM_STATE)
    im_cols = pl.ds(2 * SSM_STATE, 2 * SSM_STATE)

    def step(i, carry):
        rows_f = pl.ds(pl.multiple_of(i * rb, rb), rb)
        rows_b = pl.ds(pl.multiple_of((n_chunks - 1 - i) * rb, rb), rb)
        new = []
        for gi in groups:
            st_re, st_im = carry[2 * gi], carry[2 * gi + 1]
            pltpu.store(x_scr.at[gi, rows_f, re_cols], st_re, mask=is_fwd)
            pltpu.store(x_scr.at[gi, rows_f, im_cols], st_im, mask=is_fwd)
            pltpu.store(x_scr.at[gi, rows_b, re_cols], st_re, mask=is_bwd)
            pltpu.store(x_scr.at[gi, rows_b, im_cols], st_im, mask=is_bwd)
            in_re = jnp.where(is_fwd, s_scr[gi, rows_f, re_cols], s_scr[gi, rows_b, re_cols])
            in_im = jnp.where(is_fwd, s_scr[gi, rows_f, im_cols], s_scr[gi, rows_b, im_cols])
            new.append(ar[gi] * st_re - ai[gi] * st_im + in_re)
            new.append(ar[gi] * st_im + ai[gi] * st_re + in_im)
        return tuple(new)

    zero = jnp.zeros((rb, 2 * SSM_STATE), _F32)
    lax.fori_loop(0, n_chunks, step, (zero,) * (2 * len(groups)), unroll=True)

    for gi in groups:
        y = jnp.dot(u_ref[gi], t_ref[gi], preferred_element_type=_F32)
        y = y + lax.dot_general(x_scr[gi].astype(_BF16), c_ref[gi], (((1,), (1,)), ((), ())),
                                preferred_element_type=_F32)
        y_ref[gi] = y.astype(y_ref.dtype)


def _ssm(ug, tmat, smat, cmat, ar, ai, n_chunks, rows_per_chunk):
    g, rows, _ = ug.shape
    gb = SSM_GROUPS_PER_STEP
    per_group = lambda i: (i, 0, 0)
    return pl.pallas_call(
        functools.partial(_ssm_kernel, n_chunks, rows_per_chunk),
        grid=(g // gb,),
        in_specs=[pl.BlockSpec((gb, rows, CHUNK_W), per_group),
                  pl.BlockSpec((gb, CHUNK_W, CHUNK_W), per_group),
                  pl.BlockSpec((gb, CHUNK_W, STATE_W), per_group),
                  pl.BlockSpec((gb, CHUNK_W, STATE_W), per_group),
                  pl.BlockSpec((gb, 1, 2 * SSM_STATE), per_group),
                  pl.BlockSpec((gb, 1, 2 * SSM_STATE), per_group)],
        out_specs=pl.BlockSpec((gb, rows, CHUNK_W), per_group),
        out_shape=jax.ShapeDtypeStruct((g, rows, CHUNK_W), _BF16),
        scratch_shapes=[pltpu.VMEM((gb, rows, STATE_W), _F32),
                        pltpu.VMEM((gb, rows, STATE_W), _F32)],
        compiler_params=pltpu.CompilerParams(
            dimension_semantics=("arbitrary",), vmem_limit_bytes=VMEM_LIMIT_BYTES),
        name="ssm",
    )(ug, tmat, smat, cmat, ar, ai)


def _ssm_matrices(a_re, a_im, log_dt, b_re, b_im, c_re, c_im, d_skip):
    L, G, P, C = CHUNK, N_SSM_GROUPS, SSM_STATE, SSM_GROUP
    both = lambda a: jnp.concatenate([a[0], a[1]], axis=-1)
    dt = jnp.broadcast_to(jnp.exp(log_dt)[..., None], a_re.shape)
    lam = jnp.stack([both(a_re), both(a_im), both(dt)], axis=1)
    bt = jnp.stack([b_re, b_im], axis=0).transpose(2, 0, 4, 1, 3).reshape(G, 2, C, 2 * P)
    ct = jnp.stack([c_re, c_im], axis=0).transpose(2, 0, 3, 1, 4).reshape(G, 2, C, 2 * P)
    dvec = jnp.tile(d_skip.reshape(G, 1, C), (1, 1, L))
    gb = SLOTS_PER_VREG
    per_group = lambda *blk: pl.BlockSpec((gb,) + blk, lambda i: (i,) + (0,) * len(blk))
    mat = jax.ShapeDtypeStruct((G, L * C, L * C), _BF16)
    vec = jax.ShapeDtypeStruct((G, 1, 2 * P), _F32)
    return pl.pallas_call(
        _ssm_prep_kernel,
        grid=(G // gb,),
        in_specs=[per_group(3, 2 * P), per_group(2, C, 2 * P), per_group(2, C, 2 * P),
                  per_group(1, L * C)],
        out_specs=[per_group(L * C, L * C), per_group(L * C, STATE_W),
                   per_group(L * C, STATE_W), per_group(1, 2 * P), per_group(1, 2 * P)],
        out_shape=[mat, mat, mat, vec, vec],
        compiler_params=pltpu.CompilerParams(dimension_semantics=("arbitrary",)),
        name="ssm_prep",
    )(lam, bt, ct, dvec)


def _ssm_prep_kernel(lam_ref, bt_ref, ct_ref, dvec_ref, t_ref, s_ref, c_ref, ar_ref, ai_ref):
    for gi in range(lam_ref.shape[0]):
        _ssm_prep_group(gi, lam_ref, bt_ref, ct_ref, dvec_ref, t_ref, s_ref, c_ref, ar_ref,
                        ai_ref)


def _ssm_prep_group(gi, lam_ref, bt_ref, ct_ref, dvec_ref, t_ref, s_ref, c_ref, ar_ref,
                    ai_ref):
    L, P, C = CHUNK, SSM_STATE, SSM_GROUP
    g8 = gi % SLOTS_PER_VREG

    def time_of_slot(slot):
        return (jnp.bitwise_and(slot, -SLOTS_PER_VREG)
                + jnp.bitwise_and(slot - g8, SLOTS_PER_VREG - 1))

    def cmul(x_re, x_im, y_re, y_im):
        return x_re * y_re - x_im * y_im, x_re * y_im + x_im * y_re

    a_re, a_im, dt = lam_ref[gi, 0:1], lam_ref[gi, 1:2], lam_ref[gi, 2:3]
    lr, li = a_re * dt, a_im * dt

    def cpow(tau):
        mag, ang = jnp.exp(lr * tau), li * tau
        return mag * jnp.cos(ang), mag * jnp.sin(ang)

    ab_re, ab_im = cpow(jnp.ones_like(lr))
    den = a_re * a_re + a_im * a_im
    co_re = ((ab_re - 1.0) * a_re + ab_im * a_im) / den
    co_im = (ab_im * a_re - (ab_re - 1.0) * a_im) / den
    bb_re, bb_im = cmul(bt_ref[gi, 0], bt_ref[gi, 1], co_re, co_im)
    cc_re, cc_im = ct_ref[gi, 0], ct_ref[gi, 1]

    def outer(p, m):
        (p_re, p_im), (m_re, m_im) = p, m
        blocks = [cmul(p_re[s:s + 1], p_im[s:s + 1], m_re, m_im) for s in range(L)]
        return (jnp.concatenate([b[0] for b in blocks], axis=0),
                jnp.concatenate([b[1] for b in blocks], axis=0))

    j = time_of_slot(lax.broadcasted_iota(jnp.int32, (L, 2 * P), 0)).astype(_F32)
    fwd = lax.broadcasted_iota(jnp.int32, (L, 2 * P), 1) < P
    s_re, s_im = outer(cpow(jnp.where(fwd, L - 1 - j, j)), (bb_re, bb_im))
    s_ref[gi] = jnp.concatenate([s_re, s_im], axis=1).astype(_BF16)
    w_re, w_im = outer(cpow(jnp.where(fwd, j + 1, L - j)), (cc_re, cc_im))
    wcat = jnp.concatenate([w_re, -w_im], axis=1)
    c_ref[gi] = wcat.astype(_BF16)
    pos = lax.broadcasted_iota(jnp.int32, (L, 2 * P), 0).astype(_F32)
    a_re2, a_im2 = outer(cpow(jnp.where(fwd, pos, L - 1 - pos)), (cc_re, cc_im))
    acat = jnp.concatenate([a_re2, -a_im2], axis=1)
    bcat = jnp.concatenate([bb_re, bb_im], axis=1)
    is_fwd_state = jnp.bitwise_and(lax.broadcasted_iota(jnp.int32, bcat.shape, 1), P) == 0

    def split(a):
        hi = a.astype(_BF16).astype(_F32)
        return hi, a - hi

    def dot_state(a_parts, b_parts):
        (a_hi, a_lo), (b_hi, b_lo) = [[p.astype(_BF16) for p in parts]
                                      for parts in (a_parts, b_parts)]
        d = lambda a, b: lax.dot_general(a, b, (((1,), (1,)), ((), ())),
                                         preferred_element_type=_F32)
        return d(a_hi, b_hi) + d(a_hi, b_lo) + d(a_lo, b_hi)

    b_parts, a_parts = split(bcat), split(acat)
    kf = dot_state([jnp.where(is_fwd_state, p, 0.0) for p in b_parts], a_parts)
    kb = dot_state([jnp.where(is_fwd_state, 0.0, p) for p in b_parts], a_parts)

    lane = lax.broadcasted_iota(jnp.int32, (C, LANES), 1)

    def shift_slots(k, n):
        halves = [k[:, :LANES], k[:, LANES:]]
        if n >= SLOTS_PER_VREG:
            halves, n = halves[::-1], n - SLOTS_PER_VREG
        if n == 0:
            return halves
        r = [pltpu.roll(h, n * C, 1) for h in halves]
        wrapped = lane < n * C
        return [jnp.where(wrapped, r[1], r[0]), jnp.where(wrapped, r[0], r[1])]

    col_t = lax.broadcasted_iota(jnp.int32, (C, L * C), 1)
    blocks = []
    for jt in range(L):
        fwd_part = jnp.concatenate(shift_slots(kf, jt), axis=1)
        bwd_part = jnp.concatenate(shift_slots(kb, (jt + 1) % L), axis=1)
        blk = (jnp.where(col_t >= jt * C, fwd_part, 0.0)
               + jnp.where(col_t < (jt + 1) * C, bwd_part, 0.0))
        blocks.append(jnp.concatenate(
            [pltpu.roll(blk[:, h * LANES:(h + 1) * LANES], g8 * C, 1) if g8 else
             blk[:, h * LANES:(h + 1) * LANES] for h in range(L * C // LANES)], axis=1))
    time_of = lambda s: (s // SLOTS_PER_VREG) * SLOTS_PER_VREG + (s - g8) % SLOTS_PER_VREG
    tm = jnp.concatenate([blocks[time_of(s)] for s in range(L)], axis=0)
    row = lax.broadcasted_iota(jnp.int32, (L * C, L * C), 0)
    col = lax.broadcasted_iota(jnp.int32, (L * C, L * C), 1)
    t_ref[gi] = (tm + jnp.where(row == col, dvec_ref[gi], 0.0)).astype(_BF16)
    al_re, al_im = cpow(jnp.full_like(lr, float(L)))
    ar_ref[gi] = al_re
    ai_ref[gi] = al_im


def _attn_kernel(sink_ref, q_ref, k_ref, v_ref, bias_ref, za_ref, o_ref):
    nq = q_ref.shape[1] // BLOCK
    rows = lambda ref, c: ref[0, c * BLOCK:(c + 1) * BLOCK]
    k_blocks = [rows(k_ref, c) for c in range(nq)]
    v_blocks = [rows(v_ref, c) for c in range(nq)]
    lane = lax.broadcasted_iota(jnp.int32, (BLOCK, LANES), 1)
    halves = [lane < HEAD_DIM, lane >= HEAD_DIM]
    keep = [h.astype(_F32).astype(_BF16) for h in halves]
    slab = lambda blk, j, e: blk[:, (j ^ e) * LANES:((j ^ e) + 1) * LANES]
    k_half = {(j, e): [slab(kb, j, e) * keep[e] for kb in k_blocks]
              for j in range(KV_HEADS) for e in range(HEAD_PAIR)}
    v_half = {(j, e): [jnp.concatenate([slab(vb, j, e) * keep[e], keep[e]], axis=1)
                       for vb in v_blocks]
              for j in range(KV_HEADS) for e in range(HEAD_PAIR)}
    for c in range(nq):
        blocks = [b for b in (c - 1, c, c + 1) if 0 <= b < nq]
        slot0 = blocks[0] - (c - 1)
        nk = len(blocks) * BLOCK
        q = rows(q_ref, c)
        slabs_out = []
        for j in range(KV_HEADS):
            window = lambda halves_of: jnp.concatenate(
                [halves_of[j, e][b] for e in range(HEAD_PAIR) for b in blocks], axis=0)
            kcat = window(k_half)
            vcat = window(v_half)
            qs = jnp.concatenate(
                [q[:, (j * PAIRS_PER_KV + i) * LANES:(j * PAIRS_PER_KV + i + 1) * LANES]
                 for i in range(PAIRS_PER_KV)], axis=0)
            s = lax.dot_general(qs, kcat, (((1,), (1,)), ((), ())),
                                preferred_element_type=_F32)
            p_rows, e_rows = [], []
            for i in range(PAIRS_PER_KV):
                p_lanes, e_sink = [], []
                for e in range(HEAD_PAIR):
                    head = j * Q_PER_KV + i * HEAD_PAIR + e
                    sg = (s[i * BLOCK:(i + 1) * BLOCK, e * nk:(e + 1) * nk]
                          + bias_ref[head, :, slot0 * BLOCK:slot0 * BLOCK + nk])
                    sk = sink_ref[head]
                    m = jnp.maximum(jnp.max(sg, axis=-1, keepdims=True), sk)
                    p_lanes.append(jnp.exp2(sg - m).astype(_BF16))
                    e_sink.append(jnp.broadcast_to(jnp.exp2(sk - m), (BLOCK, LANES)))
                p_rows.append(jnp.concatenate(p_lanes, axis=1))
                e_rows.append(jnp.where(halves[0], e_sink[0], e_sink[1]))
            p = jnp.concatenate(p_rows, axis=0)
            o = jnp.dot(p, vcat, preferred_element_type=_F32)
            den = o[:, LANES:] + jnp.concatenate(e_rows, axis=0)
            on = o[:, :LANES] / den
            slabs_out += [on[i * BLOCK:(i + 1) * BLOCK] for i in range(PAIRS_PER_KV)]
        o_all = jnp.concatenate(slabs_out, axis=1)
        o_ref[0, c * BLOCK:(c + 1) * BLOCK, :] = (
            o_all * rows(za_ref, c).astype(_F32)).astype(o_ref.dtype)


def _attention(sink, q, k, v, bias, za):
    b, s, _ = q.shape
    seq = lambda w: pl.BlockSpec((1, s, w), lambda i: (i, 0, 0))
    return pl.pallas_call(
        _attn_kernel,
        grid=(b,),
        in_specs=[pl.BlockSpec(memory_space=pltpu.SMEM), seq(D_ATTN), seq(D_KVX), seq(D_KVX),
                  pl.BlockSpec(bias.shape, lambda i: (0, 0, 0)), seq(D_ATTN)],
        out_specs=seq(D_ATTN),
        out_shape=jax.ShapeDtypeStruct((b, s, D_ATTN), _BF16),
        compiler_params=pltpu.CompilerParams(
            dimension_semantics=("arbitrary",), vmem_limit_bytes=VMEM_LIMIT_BYTES),
        name="attn",
    )(sink, q, k, v, bias, za)


def _t5_bucket_np(rel):
    half = NUM_BUCKETS // 2
    ret = (rel > 0).astype(np.int64) * half
    n = np.abs(rel)
    max_exact = half // 2
    nf = np.maximum(n, 1).astype(np.float64)
    large = max_exact + (np.log(nf / max_exact) / math.log(MAX_DISTANCE / max_exact)
                         * (half - max_exact)).astype(np.int64)
    large = np.minimum(large, half - 1)
    return ret + np.where(n < max_exact, n, large)


def _attn_bias(rel_table):
    rel = (np.arange(3 * BLOCK)[None, :] - BLOCK) - np.arange(BLOCK)[:, None]
    onehot = (_t5_bucket_np(rel)[None] == np.arange(NUM_BUCKETS)[:, None, None])
    bias = jnp.einsum('bh,bqk->hqk', LOG2_E * rel_table.astype(_F32),
                      jnp.asarray(onehot, _F32), precision=_HI)
    return jnp.where(jnp.asarray(np.abs(rel) <= WINDOW), bias, NEG_INF)


def _merge_kernel(x_ref, gain_ref, wg_ref, bg_ref, ao_ref, yg_ref, zs_ref, wglu_ref,
                  bglu_ref, wba_ref, wbs_ref, wout_ref, fgain_ref, o_ref, *y_scrs):
    nb, nt = x_ref.shape[0], x_ref.shape[1]
    rows = nb * SUB_STEPS
    for st in range(nt // SUB_STEPS):
        steps = slice(st * SUB_STEPS, (st + 1) * SUB_STEPS)
        x = x_ref[:, steps, :].reshape(rows, D_MODEL)
        h = _rms_normalize(x, gain_ref[...]).astype(_BF16)
        gates2 = _twice_sigmoid_of_double(
            jnp.dot(h, wg_ref[:, W_OFF_GATES - GATE_BLOCK:], preferred_element_type=_F32)
            + bg_ref[...])
        for c in range(SUB_STEPS // CHUNK):
            _from_group_major(yg_ref, y_scrs[st], st * (SUB_STEPS // CHUNK) + c, nb, c * CHUNK,
                              SUB_STEPS)
        y = jnp.concatenate([y_scrs[st][v] for v in range(D_SSM // LANES)], axis=1)
        y2 = _twice_gelu(y)
        glu_half = (jnp.dot(y2.astype(_BF16), wglu_ref[...], preferred_element_type=_F32)
                    + bglu_ref[...])
        ssm4 = (y2 * _twice_sigmoid_of_double(glu_half)
                * zs_ref[:, steps, :].reshape(rows, D_SSM).astype(_F32))
        pa = jnp.dot(ao_ref[:, steps, :].reshape(rows, D_ATTN), wba_ref[...],
                     preferred_element_type=_F32)
        ps = jnp.dot(ssm4.astype(_BF16), wbs_ref[...], preferred_element_type=_F32)
        merged2 = gates2[:, :D_MODEL] * pa + gates2[:, D_MODEL:] * ps
        xn = x + jnp.dot(merged2.astype(_BF16), wout_ref[...], preferred_element_type=_F32)
        o_ref[:, steps, :] = _rms_normalize(xn, fgain_ref[...]).reshape(nb, SUB_STEPS, D_MODEL)


def _merge(x, gain, wg, bg, ao, yg, zs, wglu, bglu, wba, wbs, wout, fgain):
    bsz, s, _ = x.shape
    nt = TOKEN_TILE // bsz
    tok = lambda i: (0, i, 0)
    fixed = lambda i: (0, 0)
    full = lambda a: pl.BlockSpec(a.shape, fixed)
    act = lambda n: pl.BlockSpec((bsz, nt, n), tok)
    return pl.pallas_call(
        _merge_kernel,
        grid=(s // nt,),
        in_specs=[act(D_MODEL), full(gain),
                  pl.BlockSpec((D_MODEL, GATE_BLOCK), lambda i: (0, 1)),
                  full(bg), act(D_ATTN),
                  pl.BlockSpec((N_SSM_GROUPS, nt // CHUNK, bsz, CHUNK_W),
                               lambda i: (0, i, 0, 0)),
                  act(D_SSM), full(wglu), full(bglu), full(wba), full(wbs), full(wout),
                  full(fgain)],
        out_specs=act(D_MODEL),
        out_shape=jax.ShapeDtypeStruct((bsz, s, D_MODEL), _F32),
        scratch_shapes=[pltpu.VMEM((D_SSM // LANES, bsz * SUB_STEPS, LANES), _F32)
                        for _ in range(nt // SUB_STEPS)],
        compiler_params=pltpu.CompilerParams(
            dimension_semantics=("arbitrary",), vmem_limit_bytes=VMEM_LIMIT_BYTES),
        name="merge",
    )(x, gain, wg, bg, ao, yg, zs, wglu, bglu, wba, wbs, wout, fgain)


def _layer(x, norm_gain, w_in, b_gate, attn_sink, a_re, a_im, log_dt, b_re, b_im, c_re, c_im,
           d_skip, w_glu, b_glu, w_ba, w_bs, w_out, out_gain, bias):
    bsz, s, d = x.shape
    assert d == D_MODEL and x.dtype == _F32 and w_in.shape == (D_MODEL, W_OFF_GATES + 2 * D_MODEL)
    assert bsz % 8 == 0 and s % BLOCK == 0
    assert INPROJ_TOKEN_TILE % (bsz * SUB_STEPS) == 0 and s % (INPROJ_TOKEN_TILE // bsz) == 0
    assert TOKEN_TILE % (bsz * SUB_STEPS) == 0 and s % (TOKEN_TILE // bsz) == 0
    n_chunks = s // CHUNK
    gain = norm_gain.reshape(1, D_MODEL).astype(_F32)
    w_bf = w_in.astype(_BF16)
    q, k, v, za, ug, zs = _inproj(x, gain, w_bf)

    tmat, smat, cmat, ar, ai = _ssm_matrices(a_re, a_im, log_dt, b_re, b_im, c_re, c_im,
                                             d_skip)
    yg = _ssm(ug.reshape(N_SSM_GROUPS, n_chunks * bsz, CHUNK_W), tmat, smat, cmat, ar, ai,
              n_chunks, bsz)
    yg = yg.reshape(N_SSM_GROUPS, n_chunks, bsz, CHUNK_W)

    ao = _attention(LOG2_E * attn_sink.astype(_F32), q, k, v, bias, za)

    return _merge(x, 0.5 * gain,
                  w_bf,
                  (0.5 * b_gate).reshape(1, -1).astype(_F32),
                  ao, yg, zs,
                  (0.25 * w_glu).astype(_BF16),
                  (0.5 * b_glu).reshape(1, -1).astype(_F32),
                  w_ba.astype(_BF16),
                  (0.25 * w_bs).astype(_BF16),
                  (0.5 * w_out).astype(_BF16),
                  out_gain.reshape(1, D_MODEL).astype(_F32))


def kernel(x, norm_gain, w_in, b_gate, attn_sink, rel_bias_table, ssm_a_re, ssm_a_im,
           ssm_log_dt, ssm_b_re, ssm_b_im, ssm_c_re, ssm_c_im, ssm_d, w_glu, b_glu,
           w_branch_attn, w_branch_ssm, w_out, final_norm_gain):
    depth = norm_gain.shape[0]
    assert depth == 1, "final norm is fused into the single layer's epilogue"
    bias = _attn_bias(rel_bias_table)
    l = 0
    return _layer(x, norm_gain[l], w_in[l], b_gate[l], attn_sink[l], ssm_a_re[l],
                  ssm_a_im[l], ssm_log_dt[l], ssm_b_re[l], ssm_b_im[l], ssm_c_re[l],
                  ssm_c_im[l], ssm_d[l], w_glu[l], b_glu[l], w_branch_attn[l],
                  w_branch_ssm[l], w_out[l], final_norm_gain, bias)
```

```python
import functools
import math

import jax
import jax.numpy as jnp
import numpy as np
from jax import lax
from jax.experimental import pallas as pl
from jax.experimental.pallas import tpu as pltpu

D_MODEL = 1024
ATTN_HEADS = 8
KV_HEADS = 2
Q_PER_KV = ATTN_HEADS // KV_HEADS
HEAD_DIM = 64
D_ATTN = ATTN_HEADS * HEAD_DIM
D_KV = KV_HEADS * HEAD_DIM
WINDOW = 128
BLOCK = 128
NUM_BUCKETS = 32
MAX_DISTANCE = 128
D_SSM = 512
SSM_GROUP = 16
N_SSM_GROUPS = D_SSM // SSM_GROUP
SSM_STATE = 64
EPS = 1e-6
NEG_INF = -1e30

LANES = 128
SUBLANES = 8
MXU_WIDTH = 256
CHUNK = MXU_WIDTH // SSM_GROUP
CHUNK_W = CHUNK * SSM_GROUP
STATE_W = 4 * SSM_STATE
SLOTS_PER_VREG = LANES // SSM_GROUP
SSM_GROUPS_PER_STEP = 4

HEAD_PAIR = LANES // HEAD_DIM
PAIRS_PER_KV = Q_PER_KV // HEAD_PAIR
assert D_KV == LANES and KV_HEADS == HEAD_PAIR
D_KVX = 2 * D_KV
LOG2_E = math.log2(math.e)
Q_SCALE = HEAD_DIM ** -0.5 * LOG2_E

W_OFF_ZA = D_ATTN + 2 * D_KV
W_OFF_U = W_OFF_ZA + D_ATTN
W_OFF_GATES = W_OFF_U + 2 * D_SSM
GATE_BLOCK = (W_OFF_GATES + 2 * D_MODEL) // 2
assert GATE_BLOCK % LANES == 0 and GATE_BLOCK <= W_OFF_GATES

TOKEN_TILE = 1024
INPROJ_TOKEN_TILE = 2048
SUB_STEPS = CHUNK
VMEM_LIMIT_BYTES = 56 * 1024 * 1024

_F32 = jnp.float32
_BF16 = jnp.bfloat16
_HI = lax.Precision.HIGHEST


def _silu(x):
    return x * (0.5 * jnp.tanh(0.5 * x) + 0.5)


def _twice_sigmoid_of_double(xh):
    return 1.0 + jnp.tanh(xh)


def _twice_gelu(x):
    c = math.sqrt(2.0 / math.pi)
    return x * (1.0 + jnp.tanh(c * (x + 0.044715 * (x * x * x))))


def _rms_normalize(x, gain):
    ms = jnp.mean(x * x, axis=-1, keepdims=True)
    return x * lax.rsqrt(ms + EPS) * gain


def _slot_masks(rows):
    lane = lax.broadcasted_iota(jnp.int32, (rows, LANES), 1)
    return [(lane >= p * SSM_GROUP) & (lane < (p + 1) * SSM_GROUP)
            for p in range(SLOTS_PER_VREG)]


def _to_group_major(u_scr, ug_ref, kk, nb, t0, steps):
    masks = _slot_masks(nb)
    for v in range(D_SSM // LANES):
        for half in range(CHUNK // SLOTS_PER_VREG):
            rolled = []
            for t8 in range(SLOTS_PER_VREG):
                r = t0 + half * SLOTS_PER_VREG + t8
                piece = u_scr[v, pl.ds(r, nb, stride=steps), :]
                rolled.append(pltpu.roll(piece, t8 * SSM_GROUP, 1) if t8 else piece)
            for p0 in range(SLOTS_PER_VREG):
                acc = rolled[0]
                for t8 in range(1, SLOTS_PER_VREG):
                    acc = jnp.where(masks[(p0 + t8) % SLOTS_PER_VREG], rolled[t8], acc)
                g = v * SLOTS_PER_VREG + p0
                ug_ref[g, kk, :, half * LANES:(half + 1) * LANES] = acc.astype(_BF16)


def _from_group_major(yg_ref, y_scr, kk, nb, t0, steps):
    masks = _slot_masks(nb)
    for v in range(D_SSM // LANES):
        for half in range(CHUNK // SLOTS_PER_VREG):
            src = [yg_ref[v * SLOTS_PER_VREG + p0, kk, :,
                          half * LANES:(half + 1) * LANES].astype(_F32)
                   for p0 in range(SLOTS_PER_VREG)]
            for t8 in range(SLOTS_PER_VREG):
                acc = src[0]
                for p0 in range(1, SLOTS_PER_VREG):
                    acc = jnp.where(masks[(p0 + t8) % SLOTS_PER_VREG], src[p0], acc)
                if t8:
                    acc = pltpu.roll(acc, LANES - t8 * SSM_GROUP, 1)
                r = t0 + half * SLOTS_PER_VREG + t8
                y_scr[v, pl.ds(r, nb, stride=steps), :] = acc


def _inproj_kernel(x_ref, gain_ref, w_ref, q_ref, k_ref, v_ref, za_ref, ug_ref, zs_ref,
                   *u_scrs):
    nb, nt = x_ref.shape[0], x_ref.shape[1]
    rows = nb * SUB_STEPS
    for st in range(nt // SUB_STEPS):
        steps = slice(st * SUB_STEPS, (st + 1) * SUB_STEPS)
        x = x_ref[:, steps, :].reshape(rows, D_MODEL)
        h = _rms_normalize(x, gain_ref[...]).astype(_BF16)

        def put(ref, val):
            ref[:, steps, :] = val.astype(_BF16).reshape(nb, SUB_STEPS, val.shape[-1])

        proj = lambda lo, n: jnp.dot(h, w_ref[:, lo:lo + n], preferred_element_type=_F32)
        u_zs = proj(W_OFF_U, 2 * D_SSM)
        for v in range(D_SSM // LANES):
            u_scrs[st][v] = u_zs[:, v * LANES:(v + 1) * LANES]
        for c in range(SUB_STEPS // CHUNK):
            _to_group_major(u_scrs[st], ug_ref, st * (SUB_STEPS // CHUNK) + c, nb, c * CHUNK,
                            SUB_STEPS)
        put(zs_ref, _silu(u_zs[:, D_SSM:]))
        put(za_ref, _silu(proj(W_OFF_ZA, D_ATTN)))
        qkv = proj(0, W_OFF_ZA)
        put(q_ref, qkv[:, :D_ATTN] * Q_SCALE)
        with_swap = lambda a: jnp.concatenate([a, pltpu.roll(a, HEAD_DIM, 1)], axis=1)
        put(k_ref, with_swap(qkv[:, D_ATTN:D_ATTN + D_KV]))
        put(v_ref, with_swap(qkv[:, D_ATTN + D_KV:]))


def _inproj(x, gain, w_in):
    bsz, s, _ = x.shape
    nt = INPROJ_TOKEN_TILE // bsz
    tok = lambda i: (0, i, 0)
    fixed = lambda i: (0, 0)
    outs = [D_ATTN, D_KVX, D_KVX, D_ATTN, None, D_SSM]
    act = lambda n: (pl.BlockSpec((bsz, nt, n), tok), jax.ShapeDtypeStruct((bsz, s, n), _BF16))
    ug = (pl.BlockSpec((N_SSM_GROUPS, nt // CHUNK, bsz, CHUNK_W), lambda i: (0, i, 0, 0)),
          jax.ShapeDtypeStruct((N_SSM_GROUPS, s // CHUNK, bsz, CHUNK_W), _BF16))
    specs, shapes = zip(*[ug if n is None else act(n) for n in outs])
    return pl.pallas_call(
        _inproj_kernel,
        grid=(s // nt,),
        in_specs=[pl.BlockSpec((bsz, nt, D_MODEL), tok),
                  pl.BlockSpec((1, D_MODEL), fixed)]
                 + [pl.BlockSpec((D_MODEL, W_OFF_GATES), fixed)],
        out_specs=list(specs),
        out_shape=list(shapes),
        scratch_shapes=[pltpu.VMEM((D_SSM // LANES, bsz * SUB_STEPS, LANES), _F32)
                        for _ in range(nt // SUB_STEPS)],
        compiler_params=pltpu.CompilerParams(
            dimension_semantics=("arbitrary",), vmem_limit_bytes=VMEM_LIMIT_BYTES),
        name="inproj",
    )(x, gain, w_in)


def _ssm_kernel(n_chunks, rows_per_chunk, u_ref, t_ref, s_ref, c_ref, ar_ref, ai_ref,
                y_ref, s_scr, x_scr):
    rb = rows_per_chunk
    groups = range(u_ref.shape[0])
    for gi in groups:
        s_scr[gi] = jnp.dot(u_ref[gi], s_ref[gi], preferred_element_type=_F32)
    ar = [jnp.broadcast_to(ar_ref[gi], (rb, 2 * SSM_STATE)) for gi in groups]
    ai = [jnp.broadcast_to(ai_ref[gi], (rb, 2 * SSM_STATE)) for gi in groups]
    is_fwd = lax.broadcasted_iota(jnp.int32, (rb, 2 * SSM_STATE), 1) < SSM_STATE
    is_bwd = jnp.logical_not(is_fwd)
    re_cols = pl.ds(0, 2 * SSM_STATE)
    im_cols = pl.ds(2 * SSM_STATE, 2 * SSM_STATE)

    def step(i, carry):
        rows_f = pl.ds(pl.multiple_of(i * rb, rb), rb)
        rows_b = pl.ds(pl.multiple_of((n_chunks - 1 - i) * rb, rb), rb)
        new = []
        for gi in groups:
            st_re, st_im = carry[2 * gi], carry[2 * gi + 1]
            pltpu.store(x_scr.at[gi, rows_f, re_cols], st_re, mask=is_fwd)
            pltpu.store(x_scr.at[gi, rows_f, im_cols], st_im, mask=is_fwd)
            pltpu.store(x_scr.at[gi, rows_b, re_cols], st_re, mask=is_bwd)
            pltpu.store(x_scr.at[gi, rows_b, im_cols], st_im, mask=is_bwd)
            in_re = jnp.where(is_fwd, s_scr[gi, rows_f, re_cols], s_scr[gi, rows_b, re_cols])
            in_im = jnp.where(is_fwd, s_scr[gi, rows_f, im_cols], s_scr[gi, rows_b, im_cols])
            new.append(ar[gi] * st_re - ai[gi] * st_im + in_re)
            new.append(ar[gi] * st_im + ai[gi] * st_re + in_im)
        return tuple(new)

    zero = jnp.zeros((rb, 2 * SSM_STATE), _F32)
    lax.fori_loop(0, n_chunks, step, (zero,) * (2 * len(groups)), unroll=True)

    for gi in groups:
        y = jnp.dot(u_ref[gi], t_ref[gi], preferred_element_type=_F32)
        y = y + lax.dot_general(x_scr[gi].astype(_BF16), c_ref[gi], (((1,), (1,)), ((), ())),
                                preferred_element_type=_F32)
        y_ref[gi] = y.astype(y_ref.dtype)


def _ssm(ug, tmat, smat, cmat, ar, ai, n_chunks, rows_per_chunk):
    g, rows, _ = ug.shape
    gb = SSM_GROUPS_PER_STEP
    per_group = lambda i: (i, 0, 0)
    return pl.pallas_call(
        functools.partial(_ssm_kernel, n_chunks, rows_per_chunk),
        grid=(g // gb,),
        in_specs=[pl.BlockSpec((gb, rows, CHUNK_W), per_group),
                  pl.BlockSpec((gb, CHUNK_W, CHUNK_W), per_group),
                  pl.BlockSpec((gb, CHUNK_W, STATE_W), per_group),
                  pl.BlockSpec((gb, CHUNK_W, STATE_W), per_group),
                  pl.BlockSpec((gb, 1, 2 * SSM_STATE), per_group),
                  pl.BlockSpec((gb, 1, 2 * SSM_STATE), per_group)],
        out_specs=pl.BlockSpec((gb, rows, CHUNK_W), per_group),
        out_shape=jax.ShapeDtypeStruct((g, rows, CHUNK_W), _BF16),
        scratch_shapes=[pltpu.VMEM((gb, rows, STATE_W), _F32),
                        pltpu.VMEM((gb, rows, STATE_W), _F32)],
        compiler_params=pltpu.CompilerParams(
            dimension_semantics=("arbitrary",), vmem_limit_bytes=VMEM_LIMIT_BYTES),
        name="ssm",
    )(ug, tmat, smat, cmat, ar, ai)


def _ssm_matrices(a_re, a_im, log_dt, b_re, b_im, c_re, c_im, d_skip):
    L, G, P, C = CHUNK, N_SSM_GROUPS, SSM_STATE, SSM_GROUP
    both = lambda a: jnp.concatenate([a[0], a[1]], axis=-1)
    dt = jnp.broadcast_to(jnp.exp(log_dt)[..., None], a_re.shape)
    lam = jnp.stack([both(a_re), both(a_im), both(dt)], axis=1)
    bt = jnp.stack([b_re, b_im], axis=0).transpose(2, 0, 4, 1, 3).reshape(G, 2, C, 2 * P)
    ct = jnp.stack([c_re, c_im], axis=0).transpose(2, 0, 3, 1, 4).reshape(G, 2, C, 2 * P)
    dvec = jnp.tile(d_skip.reshape(G, 1, C), (1, 1, L))
    gb = SLOTS_PER_VREG
    per_group = lambda *blk: pl.BlockSpec((gb,) + blk, lambda i: (i,) + (0,) * len(blk))
    mat = jax.ShapeDtypeStruct((G, L * C, L * C), _BF16)
    vec = jax.ShapeDtypeStruct((G, 1, 2 * P), _F32)
    return pl.pallas_call(
        _ssm_prep_kernel,
        grid=(G // gb,),
        in_specs=[per_group(3, 2 * P), per_group(2, C, 2 * P), per_group(2, C, 2 * P),
                  per_group(1, L * C)],
        out_specs=[per_group(L * C, L * C), per_group(L * C, STATE_W),
                   per_group(L * C, STATE_W), per_group(1, 2 * P), per_group(1, 2 * P)],
        out_shape=[mat, mat, mat, vec, vec],
        compiler_params=pltpu.CompilerParams(dimension_semantics=("arbitrary",)),
        name="ssm_prep",
    )(lam, bt, ct, dvec)


def _ssm_prep_kernel(lam_ref, bt_ref, ct_ref, dvec_ref, t_ref, s_ref, c_ref, ar_ref, ai_ref):
    for gi in range(lam_ref.shape[0]):
        _ssm_prep_group(gi, lam_ref, bt_ref, ct_ref, dvec_ref, t_ref, s_ref, c_ref, ar_ref,
                        ai_ref)


def _ssm_prep_group(gi, lam_ref, bt_ref, ct_ref, dvec_ref, t_ref, s_ref, c_ref, ar_ref,
                    ai_ref):
    L, P, C = CHUNK, SSM_STATE, SSM_GROUP
    g8 = gi % SLOTS_PER_VREG

    def time_of_slot(slot):
        return (jnp.bitwise_and(slot, -SLOTS_PER_VREG)
                + jnp.bitwise_and(slot - g8, SLOTS_PER_VREG - 1))

    def cmul(x_re, x_im, y_re, y_im):
        return x_re * y_re - x_im * y_im, x_re * y_im + x_im * y_re

    a_re, a_im, dt = lam_ref[gi, 0:1], lam_ref[gi, 1:2], lam_ref[gi, 2:3]
    lr, li = a_re * dt, a_im * dt

    j = time_of_slot(lax.broadcasted_iota(jnp.int32, (L, 2 * P), 0)).astype(_F32)
    pos = lax.broadcasted_iota(jnp.int32, (L, 2 * P), 0).astype(_F32)
    fwd = lax.broadcasted_iota(jnp.int32, (L, 2 * P), 1) < P
    cj, sj = jnp.cos(li * j), jnp.sin(li * j)
    slot_of = lambda t: (t // SLOTS_PER_VREG) * SLOTS_PER_VREG + (t + g8) % SLOTS_PER_VREG
    in_time_order = lambda a: jnp.concatenate(
        [a[slot_of(t):slot_of(t) + 1] for t in range(L)], axis=0)
    cp, sp = in_time_order(cj), in_time_order(sj)
    krow = lax.broadcasted_iota(jnp.int32, (SUBLANES, 2 * P), 0)
    consts = jnp.where(krow == 0, 1.0, jnp.where(krow == 1, L - 1.0,
                                                 jnp.where(krow == 2, float(L), 0.0)))
    ck, sk = jnp.cos(li * consts), jnp.sin(li * consts)
    (c_1, s_1), (c_m, s_m), (c_l, s_l) = [(ck[r:r + 1], sk[r:r + 1]) for r in range(3)]

    def cpow(tau, trig):
        mag = jnp.exp(lr * tau)
        return mag * trig[0], mag * trig[1]

    minus = lambda c0, s0, c, s: (c0 * c + s0 * s, s0 * c - c0 * s)
    plus = lambda c0, s0, c, s: (c0 * c - s0 * s, s0 * c + c0 * s)
    per_dir = lambda f, b: (jnp.where(fwd, f[0], b[0]), jnp.where(fwd, f[1], b[1]))

    ab_re, ab_im = cpow(jnp.ones_like(lr), (c_1, s_1))
    den = a_re * a_re + a_im * a_im
    co_re = ((ab_re - 1.0) * a_re + ab_im * a_im) / den
    co_im = (ab_im * a_re - (ab_re - 1.0) * a_im) / den
    bb_re, bb_im = cmul(bt_ref[gi, 0], bt_ref[gi, 1], co_re, co_im)
    cc_re, cc_im = ct_ref[gi, 0], ct_ref[gi, 1]

    def outer(p, m):
        (p_re, p_im), (m_re, m_im) = p, m
        blocks = [cmul(p_re[s:s + 1], p_im[s:s + 1], m_re, m_im) for s in range(L)]
        return (jnp.concatenate([b[0] for b in blocks], axis=0),
                jnp.concatenate([b[1] for b in blocks], axis=0))

    s_re, s_im = outer(cpow(jnp.where(fwd, L - 1 - j, j),
                            per_dir(minus(c_m, s_m, cj, sj), (cj, sj))), (bb_re, bb_im))
    s_ref[gi] = jnp.concatenate([s_re, s_im], axis=1).astype(_BF16)
    w_re, w_im = outer(cpow(jnp.where(fwd, j + 1, L - j),
                            per_dir(plus(c_1, s_1, cj, sj), minus(c_l, s_l, cj, sj))),
                       (cc_re, cc_im))
    wcat = jnp.concatenate([w_re, -w_im], axis=1)
    c_ref[gi] = wcat.astype(_BF16)
    a_re2, a_im2 = outer(cpow(jnp.where(fwd, pos, L - 1 - pos),
                              per_dir((cp, sp), minus(c_m, s_m, cp, sp))), (cc_re, cc_im))
    acat = jnp.concatenate([a_re2, -a_im2], axis=1)
    bcat = jnp.concatenate([bb_re, bb_im], axis=1)
    is_fwd_state = jnp.bitwise_and(lax.broadcasted_iota(jnp.int32, bcat.shape, 1), P) == 0

    def split(a):
        hi = a.astype(_BF16).astype(_F32)
        return hi, a - hi

    def dot_state(a_parts, b_parts):
        (a_hi, a_lo), (b_hi, b_lo) = [[p.astype(_BF16) for p in parts]
                                      for parts in (a_parts, b_parts)]
        d = lambda a, b: lax.dot_general(a, b, (((1,), (1,)), ((), ())),
                                         preferred_element_type=_F32)
        return d(a_hi, b_hi) + d(a_hi, b_lo) + d(a_lo, b_hi)

    b_parts, a_parts = split(bcat), split(acat)
    kf = dot_state([jnp.where(is_fwd_state, p, 0.0) for p in b_parts], a_parts)
    kb = dot_state([jnp.where(is_fwd_state, 0.0, p) for p in b_parts], a_parts)

    lane = lax.broadcasted_iota(jnp.int32, (C, LANES), 1)

    def shift_slots(k, n):
        halves = [k[:, :LANES], k[:, LANES:]]
        if n >= SLOTS_PER_VREG:
            halves, n = halves[::-1], n - SLOTS_PER_VREG
        if n == 0:
            return halves
        r = [pltpu.roll(h, n * C, 1) for h in halves]
        wrapped = lane < n * C
        return [jnp.where(wrapped, r[1], r[0]), jnp.where(wrapped, r[0], r[1])]

    col_t = lax.broadcasted_iota(jnp.int32, (C, L * C), 1)
    blocks = []
    for jt in range(L):
        fwd_part = jnp.concatenate(shift_slots(kf, jt), axis=1)
        bwd_part = jnp.concatenate(shift_slots(kb, (jt + 1) % L), axis=1)
        blk = (jnp.where(col_t >= jt * C, fwd_part, 0.0)
               + jnp.where(col_t < (jt + 1) * C, bwd_part, 0.0))
        blocks.append(jnp.concatenate(
            [pltpu.roll(blk[:, h * LANES:(h + 1) * LANES], g8 * C, 1) if g8 else
             blk[:, h * LANES:(h + 1) * LANES] for h in range(L * C // LANES)], axis=1))
    time_of = lambda s: (s // SLOTS_PER_VREG) * SLOTS_PER_VREG + (s - g8) % SLOTS_PER_VREG
    tm = jnp.concatenate([blocks[time_of(s)] for s in range(L)], axis=0)
    row = lax.broadcasted_iota(jnp.int32, (L * C, L * C), 0)
    col = lax.broadcasted_iota(jnp.int32, (L * C, L * C), 1)
    t_ref[gi] = (tm + jnp.where(row == col, dvec_ref[gi], 0.0)).astype(_BF16)
    al_re, al_im = cpow(jnp.full_like(lr, float(L)), (c_l, s_l))
    ar_ref[gi] = al_re
    ai_ref[gi] = al_im


def _attn_kernel(sink_ref, q_ref, k_ref, v_ref, bias_ref, za_ref, o_ref):
    nq = q_ref.shape[1] // BLOCK
    rows = lambda ref, c: ref[0, c * BLOCK:(c + 1) * BLOCK]
    k_blocks = [rows(k_ref, c) for c in range(nq)]
    v_blocks = [rows(v_ref, c) for c in range(nq)]
    lane = lax.broadcasted_iota(jnp.int32, (BLOCK, LANES), 1)
    halves = [lane < HEAD_DIM, lane >= HEAD_DIM]
    keep = [h.astype(_F32).astype(_BF16) for h in halves]
    slab = lambda blk, j, e: blk[:, (j ^ e) * LANES:((j ^ e) + 1) * LANES]
    k_half = {(j, e): [slab(kb, j, e) * keep[e] for kb in k_blocks]
              for j in range(KV_HEADS) for e in range(HEAD_PAIR)}
    v_half = {(j, e): [jnp.concatenate([slab(vb, j, e) * keep[e], keep[e]], axis=1)
                       for vb in v_blocks]
              for j in range(KV_HEADS) for e in range(HEAD_PAIR)}
    for c in range(nq):
        blocks = [b for b in (c - 1, c, c + 1) if 0 <= b < nq]
        slot0 = blocks[0] - (c - 1)
        nk = len(blocks) * BLOCK
        q = rows(q_ref, c)
        slabs_out = []
        for j in range(KV_HEADS):
            window = lambda halves_of: jnp.concatenate(
                [halves_of[j, e][b] for e in range(HEAD_PAIR) for b in blocks], axis=0)
            kcat = window(k_half)
            vcat = window(v_half)
            qs = jnp.concatenate(
                [q[:, (j * PAIRS_PER_KV + i) * LANES:(j * PAIRS_PER_KV + i + 1) * LANES]
                 for i in range(PAIRS_PER_KV)], axis=0)
            s = lax.dot_general(qs, kcat, (((1,), (1,)), ((), ())),
                                preferred_element_type=_F32)
            p_rows, e_rows = [], []
            for i in range(PAIRS_PER_KV):
                p_lanes, e_sink = [], []
                for e in range(HEAD_PAIR):
                    head = j * Q_PER_KV + i * HEAD_PAIR + e
                    sg = (s[i * BLOCK:(i + 1) * BLOCK, e * nk:(e + 1) * nk]
                          + bias_ref[head, :, slot0 * BLOCK:slot0 * BLOCK + nk])
                    sk = sink_ref[head]
                    m = jnp.maximum(jnp.max(sg, axis=-1, keepdims=True), sk)
                    p_lanes.append(jnp.exp2(sg - m).astype(_BF16))
                    e_sink.append(jnp.broadcast_to(jnp.exp2(sk - m), (BLOCK, LANES)))
                p_rows.append(jnp.concatenate(p_lanes, axis=1))
                e_rows.append(jnp.where(halves[0], e_sink[0], e_sink[1]))
            p = jnp.concatenate(p_rows, axis=0)
            o = jnp.dot(p, vcat, preferred_element_type=_F32)
            den = o[:, LANES:] + jnp.concatenate(e_rows, axis=0)
            on = o[:, :LANES] / den
            slabs_out += [on[i * BLOCK:(i + 1) * BLOCK] for i in range(PAIRS_PER_KV)]
        o_all = jnp.concatenate(slabs_out, axis=1)
        o_ref[0, c * BLOCK:(c + 1) * BLOCK, :] = (
            o_all * rows(za_ref, c).astype(_F32)).astype(o_ref.dtype)


def _attention(sink, q, k, v, bias, za):
    b, s, _ = q.shape
    seq = lambda w: pl.BlockSpec((1, s, w), lambda i: (i, 0, 0))
    return pl.pallas_call(
        _attn_kernel,
        grid=(b,),
        in_specs=[pl.BlockSpec(memory_space=pltpu.SMEM), seq(D_ATTN), seq(D_KVX), seq(D_KVX),
                  pl.BlockSpec(bias.shape, lambda i: (0, 0, 0)), seq(D_ATTN)],
        out_specs=seq(D_ATTN),
        out_shape=jax.ShapeDtypeStruct((b, s, D_ATTN), _BF16),
        compiler_params=pltpu.CompilerParams(
            dimension_semantics=("arbitrary",), vmem_limit_bytes=VMEM_LIMIT_BYTES),
        name="attn",
    )(sink, q, k, v, bias, za)


def _t5_bucket_np(rel):
    half = NUM_BUCKETS // 2
    ret = (rel > 0).astype(np.int64) * half
    n = np.abs(rel)
    max_exact = half // 2
    nf = np.maximum(n, 1).astype(np.float64)
    large = max_exact + (np.log(nf / max_exact) / math.log(MAX_DISTANCE / max_exact)
                         * (half - max_exact)).astype(np.int64)
    large = np.minimum(large, half - 1)
    return ret + np.where(n < max_exact, n, large)


def _attn_bias(rel_table):
    rel = (np.arange(3 * BLOCK)[None, :] - BLOCK) - np.arange(BLOCK)[:, None]
    onehot = (_t5_bucket_np(rel)[None] == np.arange(NUM_BUCKETS)[:, None, None])
    bias = jnp.einsum('bh,bqk->hqk', LOG2_E * rel_table.astype(_F32),
                      jnp.asarray(onehot, _F32), precision=_HI)
    return jnp.where(jnp.asarray(np.abs(rel) <= WINDOW), bias, NEG_INF)


def _merge_kernel(x_ref, gain_ref, wg_ref, bg_ref, ao_ref, yg_ref, zs_ref, wglu_ref,
                  bglu_ref, wba_ref, wbs_ref, wout_ref, fgain_ref, o_ref, *y_scrs):
    nb, nt = x_ref.shape[0], x_ref.shape[1]
    rows = nb * SUB_STEPS
    for st in range(nt // SUB_STEPS):
        steps = slice(st * SUB_STEPS, (st + 1) * SUB_STEPS)
        x = x_ref[:, steps, :].reshape(rows, D_MODEL)
        h = _rms_normalize(x, gain_ref[...]).astype(_BF16)
        gates2 = _twice_sigmoid_of_double(
            jnp.dot(h, wg_ref[:, W_OFF_GATES - GATE_BLOCK:], preferred_element_type=_F32)
            + bg_ref[...])
        for c in range(SUB_STEPS // CHUNK):
            _from_group_major(yg_ref, y_scrs[st], st * (SUB_STEPS // CHUNK) + c, nb, c * CHUNK,
                              SUB_STEPS)
        y = jnp.concatenate([y_scrs[st][v] for v in range(D_SSM // LANES)], axis=1)
        y2 = _twice_gelu(y)
        glu_half = (jnp.dot(y2.astype(_BF16), wglu_ref[...], preferred_element_type=_F32)
                    + bglu_ref[...])
        ssm4 = (y2 * _twice_sigmoid_of_double(glu_half)
                * zs_ref[:, steps, :].reshape(rows, D_SSM).astype(_F32))
        pa = jnp.dot(ao_ref[:, steps, :].reshape(rows, D_ATTN), wba_ref[...],
                     preferred_element_type=_F32)
        ps = jnp.dot(ssm4.astype(_BF16), wbs_ref[...], preferred_element_type=_F32)
        merged2 = gates2[:, :D_MODEL] * pa + gates2[:, D_MODEL:] * ps
        xn = x + jnp.dot(merged2.astype(_BF16), wout_ref[...], preferred_element_type=_F32)
        o_ref[:, steps, :] = _rms_normalize(xn, fgain_ref[...]).reshape(nb, SUB_STEPS, D_MODEL)


def _merge(x, gain, wg, bg, ao, yg, zs, wglu, bglu, wba, wbs, wout, fgain):
    bsz, s, _ = x.shape
    nt = TOKEN_TILE // bsz
    tok = lambda i: (0, i, 0)
    fixed = lambda i: (0, 0)
    full = lambda a: pl.BlockSpec(a.shape, fixed)
    act = lambda n: pl.BlockSpec((bsz, nt, n), tok)
    return pl.pallas_call(
        _merge_kernel,
        grid=(s // nt,),
        in_specs=[act(D_MODEL), full(gain),
                  pl.BlockSpec((D_MODEL, GATE_BLOCK), lambda i: (0, 1)),
                  full(bg), act(D_ATTN),
                  pl.BlockSpec((N_SSM_GROUPS, nt // CHUNK, bsz, CHUNK_W),
                               lambda i: (0, i, 0, 0)),
                  act(D_SSM), full(wglu), full(bglu), full(wba), full(wbs), full(wout),
                  full(fgain)],
        out_specs=act(D_MODEL),
        out_shape=jax.ShapeDtypeStruct((bsz, s, D_MODEL), _F32),
        scratch_shapes=[pltpu.VMEM((D_SSM // LANES, bsz * SUB_STEPS, LANES), _F32)
                        for _ in range(nt // SUB_STEPS)],
        compiler_params=pltpu.CompilerParams(
            dimension_semantics=("arbitrary",), vmem_limit_bytes=VMEM_LIMIT_BYTES),
        name="merge",
    )(x, gain, wg, bg, ao, yg, zs, wglu, bglu, wba, wbs, wout, fgain)


def _layer(x, norm_gain, w_in, b_gate, attn_sink, a_re, a_im, log_dt, b_re, b_im, c_re, c_im,
           d_skip, w_glu, b_glu, w_ba, w_bs, w_out, out_gain, bias):
    bsz, s, d = x.shape
    assert d == D_MODEL and x.dtype == _F32 and w_in.shape == (D_MODEL, W_OFF_GATES + 2 * D_MODEL)
    assert bsz % 8 == 0 and s % BLOCK == 0
    assert INPROJ_TOKEN_TILE % (bsz * SUB_STEPS) == 0 and s % (INPROJ_TOKEN_TILE // bsz) == 0
    assert TOKEN_TILE % (bsz * SUB_STEPS) == 0 and s % (TOKEN_TILE // bsz) == 0
    n_chunks = s // CHUNK
    gain = norm_gain.reshape(1, D_MODEL).astype(_F32)
    w_bf = w_in.astype(_BF16)
    q, k, v, za, ug, zs = _inproj(x, gain, w_bf)

    tmat, smat, cmat, ar, ai = _ssm_matrices(a_re, a_im, log_dt, b_re, b_im, c_re, c_im,
                                             d_skip)
    yg = _ssm(ug.reshape(N_SSM_GROUPS, n_chunks * bsz, CHUNK_W), tmat, smat, cmat, ar, ai,
              n_chunks, bsz)
    yg = yg.reshape(N_SSM_GROUPS, n_chunks, bsz, CHUNK_W)

    ao = _attention(LOG2_E * attn_sink.astype(_F32), q, k, v, bias, za)

    return _merge(x, 0.5 * gain,
                  w_bf,
                  (0.5 * b_gate).reshape(1, -1).astype(_F32),
                  ao, yg, zs,
                  (0.25 * w_glu).astype(_BF16),
                  (0.5 * b_glu).reshape(1, -1).astype(_F32),
                  w_ba.astype(_BF16),
                  (0.25 * w_bs).astype(_BF16),
                  (0.5 * w_out).astype(_BF16),
                  out_gain.reshape(1, D_MODEL).astype(_F32))


def kernel(x, norm_gain, w_in, b_gate, attn_sink, rel_bias_table, ssm_a_re, ssm_a_im,
           ssm_log_dt, ssm_b_re, ssm_b_im, ssm_c_re, ssm_c_im, ssm_d, w_glu, b_glu,
           w_branch_attn, w_branch_ssm, w_out, final_norm_gain):
    depth = norm_gain.shape[0]
    assert depth == 1, "final norm is fused into the single layer's epilogue"
    bias = _attn_bias(rel_bias_table)
    l = 0
    return _layer(x, norm_gain[l], w_in[l], b_gate[l], attn_sink[l], ssm_a_re[l],
                  ssm_a_im[l], ssm_log_dt[l], ssm_b_re[l], ssm_b_im[l], ssm_c_re[l],
                  ssm_c_im[l], ssm_d[l], w_glu[l], b_glu[l], w_branch_attn[l],
                  w_branch_ssm[l], w_out[l], final_norm_gain, bias)
```

```python
import functools
import math

import jax
import jax.numpy as jnp
import numpy as np
from jax import lax
from jax.experimental import pallas as pl
from jax.experimental.pallas import tpu as pltpu

D_MODEL = 1024
ATTN_HEADS = 8
KV_HEADS = 2
Q_PER_KV = ATTN_HEADS // KV_HEADS
HEAD_DIM = 64
D_ATTN = ATTN_HEADS * HEAD_DIM
D_KV = KV_HEADS * HEAD_DIM
WINDOW = 128
BLOCK = 128
NUM_BUCKETS = 32
MAX_DISTANCE = 128
D_SSM = 512
SSM_GROUP = 16
N_SSM_GROUPS = D_SSM // SSM_GROUP
SSM_STATE = 64
EPS = 1e-6
NEG_INF = -1e30

LANES = 128
SUBLANES = 8
MXU_WIDTH = 256
CHUNK = MXU_WIDTH // SSM_GROUP
CHUNK_W = CHUNK * SSM_GROUP
STATE_W = 4 * SSM_STATE
SLOTS_PER_VREG = LANES // SSM_GROUP
SSM_GROUPS_PER_STEP = 4

HEAD_PAIR = LANES // HEAD_DIM
PAIRS_PER_KV = Q_PER_KV // HEAD_PAIR
assert D_KV == LANES and KV_HEADS == HEAD_PAIR
D_KVX = 2 * D_KV
LOG2_E = math.log2(math.e)
Q_SCALE = HEAD_DIM ** -0.5 * LOG2_E
BIAS_ROW = 4 * BLOCK

W_OFF_ZA = D_ATTN + 2 * D_KV
W_OFF_U = W_OFF_ZA + D_ATTN
W_OFF_GATES = W_OFF_U + 2 * D_SSM
GATE_BLOCK = (W_OFF_GATES + 2 * D_MODEL) // 2
assert GATE_BLOCK % LANES == 0 and GATE_BLOCK <= W_OFF_GATES

TOKEN_TILE = 1024
INPROJ_TOKEN_TILE = 2048
SUB_STEPS = CHUNK
VMEM_LIMIT_BYTES = 56 * 1024 * 1024

_F32 = jnp.float32
_BF16 = jnp.bfloat16
_HI = lax.Precision.HIGHEST


def _silu(x):
    return x * (0.5 * jnp.tanh(0.5 * x) + 0.5)


def _twice_sigmoid_of_double(xh):
    return 1.0 + jnp.tanh(xh)


def _twice_gelu(x):
    c = math.sqrt(2.0 / math.pi)
    return x * (1.0 + jnp.tanh(c * (x + 0.044715 * (x * x * x))))


def _rms_normalize(x, gain):
    ms = jnp.mean(x * x, axis=-1, keepdims=True)
    return x * lax.rsqrt(ms + EPS) * gain


def _slot_masks(rows):
    lane = lax.broadcasted_iota(jnp.int32, (rows, LANES), 1)
    return [(lane >= p * SSM_GROUP) & (lane < (p + 1) * SSM_GROUP)
            for p in range(SLOTS_PER_VREG)]


def _to_group_major(u_scr, ug_ref, kk, nb, t0, steps):
    masks = _slot_masks(nb)
    for v in range(D_SSM // LANES):
        for half in range(CHUNK // SLOTS_PER_VREG):
            rolled = []
            for t8 in range(SLOTS_PER_VREG):
                r = t0 + half * SLOTS_PER_VREG + t8
                piece = u_scr[v, pl.ds(r, nb, stride=steps), :]
                rolled.append(pltpu.roll(piece, t8 * SSM_GROUP, 1) if t8 else piece)
            for p0 in range(SLOTS_PER_VREG):
                acc = rolled[0]
                for t8 in range(1, SLOTS_PER_VREG):
                    acc = jnp.where(masks[(p0 + t8) % SLOTS_PER_VREG], rolled[t8], acc)
                g = v * SLOTS_PER_VREG + p0
                ug_ref[g, kk, :, half * LANES:(half + 1) * LANES] = acc.astype(_BF16)


def _from_group_major(yg_ref, y_scr, kk, nb, t0, steps):
    masks = _slot_masks(nb)
    for v in range(D_SSM // LANES):
        for half in range(CHUNK // SLOTS_PER_VREG):
            src = [yg_ref[v * SLOTS_PER_VREG + p0, kk, :,
                          half * LANES:(half + 1) * LANES].astype(_F32)
                   for p0 in range(SLOTS_PER_VREG)]
            for t8 in range(SLOTS_PER_VREG):
                acc = src[0]
                for p0 in range(1, SLOTS_PER_VREG):
                    acc = jnp.where(masks[(p0 + t8) % SLOTS_PER_VREG], src[p0], acc)
                if t8:
                    acc = pltpu.roll(acc, LANES - t8 * SSM_GROUP, 1)
                r = t0 + half * SLOTS_PER_VREG + t8
                y_scr[v, pl.ds(r, nb, stride=steps), :] = acc


def _inproj_kernel(x_ref, gain_ref, w_ref, q_ref, k_ref, v_ref, za_ref, ug_ref, zs_ref,
                   *u_scrs):
    nb, nt = x_ref.shape[0], x_ref.shape[1]
    rows = nb * SUB_STEPS
    for st in range(nt // SUB_STEPS):
        steps = slice(st * SUB_STEPS, (st + 1) * SUB_STEPS)
        x = x_ref[:, steps, :].reshape(rows, D_MODEL)
        h = _rms_normalize(x, gain_ref[...]).astype(_BF16)

        def put(ref, val):
            ref[:, steps, :] = val.astype(_BF16).reshape(nb, SUB_STEPS, val.shape[-1])

        proj = lambda lo, n: jnp.dot(h, w_ref[:, lo:lo + n], preferred_element_type=_F32)
        u_zs = proj(W_OFF_U, 2 * D_SSM)
        for v in range(D_SSM // LANES):
            u_scrs[st][v] = u_zs[:, v * LANES:(v + 1) * LANES]
        for c in range(SUB_STEPS // CHUNK):
            _to_group_major(u_scrs[st], ug_ref, st * (SUB_STEPS // CHUNK) + c, nb, c * CHUNK,
                            SUB_STEPS)
        put(zs_ref, _silu(u_zs[:, D_SSM:]))
        put(za_ref, _silu(proj(W_OFF_ZA, D_ATTN)))
        qkv = proj(0, W_OFF_ZA)
        put(q_ref, qkv[:, :D_ATTN] * Q_SCALE)
        with_swap = lambda a: jnp.concatenate([a, pltpu.roll(a, HEAD_DIM, 1)], axis=1)
        put(k_ref, with_swap(qkv[:, D_ATTN:D_ATTN + D_KV]))
        put(v_ref, with_swap(qkv[:, D_ATTN + D_KV:]))


def _inproj(x, gain, w_in):
    bsz, s, _ = x.shape
    nt = INPROJ_TOKEN_TILE // bsz
    tok = lambda i: (0, i, 0)
    fixed = lambda i: (0, 0)
    outs = [D_ATTN, D_KVX, D_KVX, D_ATTN, None, D_SSM]
    act = lambda n: (pl.BlockSpec((bsz, nt, n), tok), jax.ShapeDtypeStruct((bsz, s, n), _BF16))
    ug = (pl.BlockSpec((N_SSM_GROUPS, nt // CHUNK, bsz, CHUNK_W), lambda i: (0, i, 0, 0)),
          jax.ShapeDtypeStruct((N_SSM_GROUPS, s // CHUNK, bsz, CHUNK_W), _BF16))
    specs, shapes = zip(*[ug if n is None else act(n) for n in outs])
    return pl.pallas_call(
        _inproj_kernel,
        grid=(s // nt,),
        in_specs=[pl.BlockSpec((bsz, nt, D_MODEL), tok),
                  pl.BlockSpec((1, D_MODEL), fixed)]
                 + [pl.BlockSpec((D_MODEL, W_OFF_GATES), fixed)],
        out_specs=list(specs),
        out_shape=list(shapes),
        scratch_shapes=[pltpu.VMEM((D_SSM // LANES, bsz * SUB_STEPS, LANES), _F32)
                        for _ in range(nt // SUB_STEPS)],
        compiler_params=pltpu.CompilerParams(
            dimension_semantics=("arbitrary",), vmem_limit_bytes=VMEM_LIMIT_BYTES),
        name="inproj",
    )(x, gain, w_in)


def _ssm_kernel(n_chunks, rows_per_chunk, u_ref, t_ref, s_ref, c_ref, ar_ref, ai_ref,
                y_ref, s_scr, x_scr):
    rb = rows_per_chunk
    groups = range(u_ref.shape[0])
    for gi in groups:
        s_scr[gi] = jnp.dot(u_ref[gi], s_ref[gi], preferred_element_type=_F32)
    ar = [jnp.broadcast_to(ar_ref[gi], (rb, 2 * SSM_STATE)) for gi in groups]
    ai = [jnp.broadcast_to(ai_ref[gi], (rb, 2 * SSM_STATE)) for gi in groups]
    is_fwd = lax.broadcasted_iota(jnp.int32, (rb, 2 * SSM_STATE), 1) < SSM_STATE
    is_bwd = jnp.logical_not(is_fwd)
    re_cols = pl.ds(0, 2 * SSM_STATE)
    im_cols = pl.ds(2 * SSM_STATE, 2 * SSM_STATE)

    def step(i, carry):
        rows_f = pl.ds(pl.multiple_of(i * rb, rb), rb)
        rows_b = pl.ds(pl.multiple_of((n_chunks - 1 - i) * rb, rb), rb)
        new = []
        for gi in groups:
            st_re, st_im = carry[2 * gi], carry[2 * gi + 1]
            pltpu.store(x_scr.at[gi, rows_f, re_cols], st_re, mask=is_fwd)
            pltpu.store(x_scr.at[gi, rows_f, im_cols], st_im, mask=is_fwd)
            pltpu.store(x_scr.at[gi, rows_b, re_cols], st_re, mask=is_bwd)
            pltpu.store(x_scr.at[gi, rows_b, im_cols], st_im, mask=is_bwd)
            in_re = jnp.where(is_fwd, s_scr[gi, rows_f, re_cols], s_scr[gi, rows_b, re_cols])
            in_im = jnp.where(is_fwd, s_scr[gi, rows_f, im_cols], s_scr[gi, rows_b, im_cols])
            new.append(ar[gi] * st_re - ai[gi] * st_im + in_re)
            new.append(ar[gi] * st_im + ai[gi] * st_re + in_im)
        return tuple(new)

    zero = jnp.zeros((rb, 2 * SSM_STATE), _F32)
    lax.fori_loop(0, n_chunks, step, (zero,) * (2 * len(groups)), unroll=True)

    for gi in groups:
        y = jnp.dot(u_ref[gi], t_ref[gi], preferred_element_type=_F32)
        y = y + lax.dot_general(x_scr[gi].astype(_BF16), c_ref[gi], (((1,), (1,)), ((), ())),
                                preferred_element_type=_F32)
        y_ref[gi] = y.astype(y_ref.dtype)


def _ssm(ug, tmat, smat, cmat, ar, ai, n_chunks, rows_per_chunk):
    g, rows, _ = ug.shape
    gb = SSM_GROUPS_PER_STEP
    per_group = lambda i: (i, 0, 0)
    return pl.pallas_call(
        functools.partial(_ssm_kernel, n_chunks, rows_per_chunk),
        grid=(g // gb,),
        in_specs=[pl.BlockSpec((gb, rows, CHUNK_W), per_group),
                  pl.BlockSpec((gb, CHUNK_W, CHUNK_W), per_group),
                  pl.BlockSpec((gb, CHUNK_W, STATE_W), per_group),
                  pl.BlockSpec((gb, CHUNK_W, STATE_W), per_group),
                  pl.BlockSpec((gb, 1, 2 * SSM_STATE), per_group),
                  pl.BlockSpec((gb, 1, 2 * SSM_STATE), per_group)],
        out_specs=pl.BlockSpec((gb, rows, CHUNK_W), per_group),
        out_shape=jax.ShapeDtypeStruct((g, rows, CHUNK_W), _BF16),
        scratch_shapes=[pltpu.VMEM((gb, rows, STATE_W), _F32),
                        pltpu.VMEM((gb, rows, STATE_W), _F32)],
        compiler_params=pltpu.CompilerParams(
            dimension_semantics=("arbitrary",), vmem_limit_bytes=VMEM_LIMIT_BYTES),
        name="ssm",
    )(ug, tmat, smat, cmat, ar, ai)


def _ssm_matrices(a_re, a_im, log_dt, b_re, b_im, c_re, c_im, d_skip):
    L, G, P, C = CHUNK, N_SSM_GROUPS, SSM_STATE, SSM_GROUP
    both = lambda a: jnp.concatenate([a[0], a[1]], axis=-1)
    dt = jnp.broadcast_to(jnp.exp(log_dt)[..., None], a_re.shape)
    lam = jnp.stack([both(a_re), both(a_im), both(dt)], axis=1)
    bt = jnp.stack([b_re, b_im], axis=0).transpose(2, 0, 4, 1, 3).reshape(G, 2, C, 2 * P)
    ct = jnp.stack([c_re, c_im], axis=0).transpose(2, 0, 3, 1, 4).reshape(G, 2, C, 2 * P)
    dvec = jnp.tile(d_skip.reshape(G, 1, C), (1, 1, L))
    gb = SLOTS_PER_VREG
    per_group = lambda *blk: pl.BlockSpec((gb,) + blk, lambda i: (i,) + (0,) * len(blk))
    mat = jax.ShapeDtypeStruct((G, L * C, L * C), _BF16)
    vec = jax.ShapeDtypeStruct((G, 1, 2 * P), _F32)
    return pl.pallas_call(
        _ssm_prep_kernel,
        grid=(G // gb,),
        in_specs=[per_group(3, 2 * P), per_group(2, C, 2 * P), per_group(2, C, 2 * P),
                  per_group(1, L * C)],
        out_specs=[per_group(L * C, L * C), per_group(L * C, STATE_W),
                   per_group(L * C, STATE_W), per_group(1, 2 * P), per_group(1, 2 * P)],
        out_shape=[mat, mat, mat, vec, vec],
        compiler_params=pltpu.CompilerParams(dimension_semantics=("arbitrary",)),
        name="ssm_prep",
    )(lam, bt, ct, dvec)


def _ssm_prep_kernel(lam_ref, bt_ref, ct_ref, dvec_ref, t_ref, s_ref, c_ref, ar_ref, ai_ref):
    for gi in range(lam_ref.shape[0]):
        _ssm_prep_group(gi, lam_ref, bt_ref, ct_ref, dvec_ref, t_ref, s_ref, c_ref, ar_ref,
                        ai_ref)


def _ssm_prep_group(gi, lam_ref, bt_ref, ct_ref, dvec_ref, t_ref, s_ref, c_ref, ar_ref,
                    ai_ref):
    L, P, C = CHUNK, SSM_STATE, SSM_GROUP
    g8 = gi % SLOTS_PER_VREG

    def time_of_slot(slot):
        return (jnp.bitwise_and(slot, -SLOTS_PER_VREG)
                + jnp.bitwise_and(slot - g8, SLOTS_PER_VREG - 1))

    def cmul(x_re, x_im, y_re, y_im):
        return x_re * y_re - x_im * y_im, x_re * y_im + x_im * y_re

    a_re, a_im, dt = lam_ref[gi, 0:1], lam_ref[gi, 1:2], lam_ref[gi, 2:3]
    lr, li = a_re * dt, a_im * dt

    j = time_of_slot(lax.broadcasted_iota(jnp.int32, (L, 2 * P), 0)).astype(_F32)
    pos = lax.broadcasted_iota(jnp.int32, (L, 2 * P), 0).astype(_F32)
    fwd = lax.broadcasted_iota(jnp.int32, (L, 2 * P), 1) < P
    cj, sj = jnp.cos(li * j), jnp.sin(li * j)
    slot_of = lambda t: (t // SLOTS_PER_VREG) * SLOTS_PER_VREG + (t + g8) % SLOTS_PER_VREG
    in_time_order = lambda a: jnp.concatenate(
        [a[slot_of(t):slot_of(t) + 1] for t in range(L)], axis=0)
    cp, sp = in_time_order(cj), in_time_order(sj)
    krow = lax.broadcasted_iota(jnp.int32, (SUBLANES, 2 * P), 0)
    consts = jnp.where(krow == 0, 1.0, jnp.where(krow == 1, L - 1.0,
                                                 jnp.where(krow == 2, float(L), 0.0)))
    ck, sk = jnp.cos(li * consts), jnp.sin(li * consts)
    (c_1, s_1), (c_m, s_m), (c_l, s_l) = [(ck[r:r + 1], sk[r:r + 1]) for r in range(3)]

    def cpow(tau, trig):
        mag = jnp.exp(lr * tau)
        return mag * trig[0], mag * trig[1]

    minus = lambda c0, s0, c, s: (c0 * c + s0 * s, s0 * c - c0 * s)
    plus = lambda c0, s0, c, s: (c0 * c - s0 * s, s0 * c + c0 * s)
    per_dir = lambda f, b: (jnp.where(fwd, f[0], b[0]), jnp.where(fwd, f[1], b[1]))

    ab_re, ab_im = cpow(jnp.ones_like(lr), (c_1, s_1))
    den = a_re * a_re + a_im * a_im
    co_re = ((ab_re - 1.0) * a_re + ab_im * a_im) / den
    co_im = (ab_im * a_re - (ab_re - 1.0) * a_im) / den
    bb_re, bb_im = cmul(bt_ref[gi, 0], bt_ref[gi, 1], co_re, co_im)
    cc_re, cc_im = ct_ref[gi, 0], ct_ref[gi, 1]

    def outer(p, m):
        (p_re, p_im), (m_re, m_im) = p, m
        blocks = [cmul(p_re[s:s + 1], p_im[s:s + 1], m_re, m_im) for s in range(L)]
        return (jnp.concatenate([b[0] for b in blocks], axis=0),
                jnp.concatenate([b[1] for b in blocks], axis=0))

    s_re, s_im = outer(cpow(jnp.where(fwd, L - 1 - j, j),
                            per_dir(minus(c_m, s_m, cj, sj), (cj, sj))), (bb_re, bb_im))
    s_ref[gi] = jnp.concatenate([s_re, s_im], axis=1).astype(_BF16)
    w_re, w_im = outer(cpow(jnp.where(fwd, j + 1, L - j),
                            per_dir(plus(c_1, s_1, cj, sj), minus(c_l, s_l, cj, sj))),
                       (cc_re, cc_im))
    wcat = jnp.concatenate([w_re, -w_im], axis=1)
    c_ref[gi] = wcat.astype(_BF16)
    a_re2, a_im2 = outer(cpow(jnp.where(fwd, pos, L - 1 - pos),
                              per_dir((cp, sp), minus(c_m, s_m, cp, sp))), (cc_re, cc_im))
    acat = jnp.concatenate([a_re2, -a_im2], axis=1)
    bcat = jnp.concatenate([bb_re, bb_im], axis=1)
    is_fwd_state = jnp.bitwise_and(lax.broadcasted_iota(jnp.int32, bcat.shape, 1), P) == 0

    def split(a):
        hi = a.astype(_BF16).astype(_F32)
        return hi, a - hi

    def dot_state(a_parts, b_parts):
        (a_hi, a_lo), (b_hi, b_lo) = [[p.astype(_BF16) for p in parts]
                                      for parts in (a_parts, b_parts)]
        d = lambda a, b: lax.dot_general(a, b, (((1,), (1,)), ((), ())),
                                         preferred_element_type=_F32)
        return d(a_hi, b_hi) + d(a_hi, b_lo) + d(a_lo, b_hi)

    b_parts, a_parts = split(bcat), split(acat)
    kf = dot_state([jnp.where(is_fwd_state, p, 0.0) for p in b_parts], a_parts)
    kb = dot_state([jnp.where(is_fwd_state, 0.0, p) for p in b_parts], a_parts)

    lane = lax.broadcasted_iota(jnp.int32, (C, LANES), 1)

    def shift_slots(k, n):
        halves = [k[:, :LANES], k[:, LANES:]]
        if n >= SLOTS_PER_VREG:
            halves, n = halves[::-1], n - SLOTS_PER_VREG
        if n == 0:
            return halves
        r = [pltpu.roll(h, n * C, 1) for h in halves]
        wrapped = lane < n * C
        return [jnp.where(wrapped, r[1], r[0]), jnp.where(wrapped, r[0], r[1])]

    col_t = lax.broadcasted_iota(jnp.int32, (C, L * C), 1)
    blocks = []
    for jt in range(L):
        fwd_part = jnp.concatenate(shift_slots(kf, jt), axis=1)
        bwd_part = jnp.concatenate(shift_slots(kb, (jt + 1) % L), axis=1)
        blk = (jnp.where(col_t >= jt * C, fwd_part, 0.0)
               + jnp.where(col_t < (jt + 1) * C, bwd_part, 0.0))
        blocks.append(jnp.concatenate(
            [pltpu.roll(blk[:, h * LANES:(h + 1) * LANES], g8 * C, 1) if g8 else
             blk[:, h * LANES:(h + 1) * LANES] for h in range(L * C // LANES)], axis=1))
    time_of = lambda s: (s // SLOTS_PER_VREG) * SLOTS_PER_VREG + (s - g8) % SLOTS_PER_VREG
    tm = jnp.concatenate([blocks[time_of(s)] for s in range(L)], axis=0)
    row = lax.broadcasted_iota(jnp.int32, (L * C, L * C), 0)
    col = lax.broadcasted_iota(jnp.int32, (L * C, L * C), 1)
    t_ref[gi] = (tm + jnp.where(row == col, dvec_ref[gi], 0.0)).astype(_BF16)
    al_re, al_im = cpow(jnp.full_like(lr, float(L)), (c_l, s_l))
    ar_ref[gi] = al_re
    ai_ref[gi] = al_im


def _attn_kernel(sink_ref, q_ref, k_ref, v_ref, bias_rows_ref, za_ref, o_ref, bias_ref):
    nq = q_ref.shape[1] // BLOCK

    @pl.when(pl.program_id(0) == 0)
    def _():
        qi = lax.broadcasted_iota(jnp.int32, (BLOCK, 3 * BLOCK), 0)
        ki = lax.broadcasted_iota(jnp.int32, (BLOCK, 3 * BLOCK), 1)
        in_band = jnp.abs(ki - BLOCK - qi) <= WINDOW
        for h in range(ATTN_HEADS):
            row = jnp.broadcast_to(bias_rows_ref[h:h + 1, :], (BLOCK, BIAS_ROW))
            rel = pltpu.roll(row, 0, 1, stride=1, stride_axis=0)
            bias_ref[h] = jnp.where(in_band, rel[:, :3 * BLOCK], NEG_INF)

    rows = lambda ref, c: ref[0, c * BLOCK:(c + 1) * BLOCK]
    k_blocks = [rows(k_ref, c) for c in range(nq)]
    v_blocks = [rows(v_ref, c) for c in range(nq)]
    lane = lax.broadcasted_iota(jnp.int32, (BLOCK, LANES), 1)
    halves = [lane < HEAD_DIM, lane >= HEAD_DIM]
    keep = [h.astype(_F32).astype(_BF16) for h in halves]
    slab = lambda blk, j, e: blk[:, (j ^ e) * LANES:((j ^ e) + 1) * LANES]
    k_half = {(j, e): [slab(kb, j, e) * keep[e] for kb in k_blocks]
              for j in range(KV_HEADS) for e in range(HEAD_PAIR)}
    v_half = {(j, e): [jnp.concatenate([slab(vb, j, e) * keep[e], keep[e]], axis=1)
                       for vb in v_blocks]
              for j in range(KV_HEADS) for e in range(HEAD_PAIR)}
    for c in range(nq):
        blocks = [b for b in (c - 1, c, c + 1) if 0 <= b < nq]
        slot0 = blocks[0] - (c - 1)
        nk = len(blocks) * BLOCK
        q = rows(q_ref, c)
        slabs_out = []
        for j in range(KV_HEADS):
            window = lambda halves_of: jnp.concatenate(
                [halves_of[j, e][b] for e in range(HEAD_PAIR) for b in blocks], axis=0)
            kcat = window(k_half)
            vcat = window(v_half)
            qs = jnp.concatenate(
                [q[:, (j * PAIRS_PER_KV + i) * LANES:(j * PAIRS_PER_KV + i + 1) * LANES]
                 for i in range(PAIRS_PER_KV)], axis=0)
            s = lax.dot_general(qs, kcat, (((1,), (1,)), ((), ())),
                                preferred_element_type=_F32)
            p_rows, e_rows = [], []
            for i in range(PAIRS_PER_KV):
                p_lanes, e_sink = [], []
                for e in range(HEAD_PAIR):
                    head = j * Q_PER_KV + i * HEAD_PAIR + e
                    sg = (s[i * BLOCK:(i + 1) * BLOCK, e * nk:(e + 1) * nk]
                          + bias_ref[head, :, slot0 * BLOCK:slot0 * BLOCK + nk])
                    sk = sink_ref[head]
                    m = jnp.maximum(jnp.max(sg, axis=-1, keepdims=True), sk)
                    p_lanes.append(jnp.exp2(sg - m).astype(_BF16))
                    e_sink.append(jnp.broadcast_to(jnp.exp2(sk - m), (BLOCK, LANES)))
                p_rows.append(jnp.concatenate(p_lanes, axis=1))
                e_rows.append(jnp.where(halves[0], e_sink[0], e_sink[1]))
            p = jnp.concatenate(p_rows, axis=0)
            o = jnp.dot(p, vcat, preferred_element_type=_F32)
            den = o[:, LANES:] + jnp.concatenate(e_rows, axis=0)
            on = o[:, :LANES] / den
            slabs_out += [on[i * BLOCK:(i + 1) * BLOCK] for i in range(PAIRS_PER_KV)]
        o_all = jnp.concatenate(slabs_out, axis=1)
        o_ref[0, c * BLOCK:(c + 1) * BLOCK, :] = (
            o_all * rows(za_ref, c).astype(_F32)).astype(o_ref.dtype)


def _attention(sink, q, k, v, bias_rows, za):
    b, s, _ = q.shape
    seq = lambda w: pl.BlockSpec((1, s, w), lambda i: (i, 0, 0))
    return pl.pallas_call(
        _attn_kernel,
        grid=(b,),
        in_specs=[pl.BlockSpec(memory_space=pltpu.SMEM), seq(D_ATTN), seq(D_KVX), seq(D_KVX),
                  pl.BlockSpec(bias_rows.shape, lambda i: (0, 0)), seq(D_ATTN)],
        out_specs=seq(D_ATTN),
        out_shape=jax.ShapeDtypeStruct((b, s, D_ATTN), _BF16),
        scratch_shapes=[pltpu.VMEM((ATTN_HEADS, BLOCK, 3 * BLOCK), _F32)],
        compiler_params=pltpu.CompilerParams(
            dimension_semantics=("arbitrary",), vmem_limit_bytes=VMEM_LIMIT_BYTES),
        name="attn",
    )(sink, q, k, v, bias_rows, za)


def _t5_bucket_np(rel):
    half = NUM_BUCKETS // 2
    ret = (rel > 0).astype(np.int64) * half
    n = np.abs(rel)
    max_exact = half // 2
    nf = np.maximum(n, 1).astype(np.float64)
    large = max_exact + (np.log(nf / max_exact) / math.log(MAX_DISTANCE / max_exact)
                         * (half - max_exact)).astype(np.int64)
    large = np.minimum(large, half - 1)
    return ret + np.where(n < max_exact, n, large)


def _attn_bias(rel_table):
    rel = (np.arange(BIAS_ROW) + BLOCK) % BIAS_ROW - 2 * BLOCK
    onehot = (_t5_bucket_np(rel)[None] == np.arange(NUM_BUCKETS)[:, None])
    return jnp.einsum('bh,br->hr', LOG2_E * rel_table.astype(_F32),
                      jnp.asarray(onehot, _F32), precision=_HI)


def _merge_kernel(x_ref, gain_ref, wg_ref, bg_ref, ao_ref, yg_ref, zs_ref, wglu_ref,
                  bglu_ref, wba_ref, wbs_ref, wout_ref, fgain_ref, o_ref, *y_scrs):
    nb, nt = x_ref.shape[0], x_ref.shape[1]
    rows = nb * SUB_STEPS
    for st in range(nt // SUB_STEPS):
        steps = slice(st * SUB_STEPS, (st + 1) * SUB_STEPS)
        x = x_ref[:, steps, :].reshape(rows, D_MODEL)
        h = _rms_normalize(x, gain_ref[...]).astype(_BF16)
        gates2 = _twice_sigmoid_of_double(
            jnp.dot(h, wg_ref[:, W_OFF_GATES - GATE_BLOCK:], preferred_element_type=_F32)
            + bg_ref[...])
        for c in range(SUB_STEPS // CHUNK):
            _from_group_major(yg_ref, y_scrs[st], st * (SUB_STEPS // CHUNK) + c, nb, c * CHUNK,
                              SUB_STEPS)
        y = jnp.concatenate([y_scrs[st][v] for v in range(D_SSM // LANES)], axis=1)
        y2 = _twice_gelu(y)
        glu_half = (jnp.dot(y2.astype(_BF16), wglu_ref[...], preferred_element_type=_F32)
                    + bglu_ref[...])
        ssm4 = (y2 * _twice_sigmoid_of_double(glu_half)
                * zs_ref[:, steps, :].reshape(rows, D_SSM).astype(_F32))
        pa = jnp.dot(ao_ref[:, steps, :].reshape(rows, D_ATTN), wba_ref[...],
                     preferred_element_type=_F32)
        ps = jnp.dot(ssm4.astype(_BF16), wbs_ref[...], preferred_element_type=_F32)
        merged2 = gates2[:, :D_MODEL] * pa + gates2[:, D_MODEL:] * ps
        xn = x + jnp.dot(merged2.astype(_BF16), wout_ref[...], preferred_element_type=_F32)
        o_ref[:, steps, :] = _rms_normalize(xn, fgain_ref[...]).reshape(nb, SUB_STEPS, D_MODEL)


def _merge(x, gain, wg, bg, ao, yg, zs, wglu, bglu, wba, wbs, wout, fgain):
    bsz, s, _ = x.shape
    nt = TOKEN_TILE // bsz
    tok = lambda i: (0, i, 0)
    fixed = lambda i: (0, 0)
    full = lambda a: pl.BlockSpec(a.shape, fixed)
    act = lambda n: pl.BlockSpec((bsz, nt, n), tok)
    return pl.pallas_call(
        _merge_kernel,
        grid=(s // nt,),
        in_specs=[act(D_MODEL), full(gain),
                  pl.BlockSpec((D_MODEL, GATE_BLOCK), lambda i: (0, 1)),
                  full(bg), act(D_ATTN),
                  pl.BlockSpec((N_SSM_GROUPS, nt // CHUNK, bsz, CHUNK_W),
                               lambda i: (0, i, 0, 0)),
                  act(D_SSM), full(wglu), full(bglu), full(wba), full(wbs), full(wout),
                  full(fgain)],
        out_specs=act(D_MODEL),
        out_shape=jax.ShapeDtypeStruct((bsz, s, D_MODEL), _F32),
        scratch_shapes=[pltpu.VMEM((D_SSM // LANES, bsz * SUB_STEPS, LANES), _F32)
                        for _ in range(nt // SUB_STEPS)],
        compiler_params=pltpu.CompilerParams(
            dimension_semantics=("arbitrary",), vmem_limit_bytes=VMEM_LIMIT_BYTES),
        name="merge",
    )(x, gain, wg, bg, ao, yg, zs, wglu, bglu, wba, wbs, wout, fgain)


def _layer(x, norm_gain, w_in, b_gate, attn_sink, a_re, a_im, log_dt, b_re, b_im, c_re, c_im,
           d_skip, w_glu, b_glu, w_ba, w_bs, w_out, out_gain, bias):
    bsz, s, d = x.shape
    assert d == D_MODEL and x.dtype == _F32 and w_in.shape == (D_MODEL, W_OFF_GATES + 2 * D_MODEL)
    assert bsz % 8 == 0 and s % BLOCK == 0
    assert INPROJ_TOKEN_TILE % (bsz * SUB_STEPS) == 0 and s % (INPROJ_TOKEN_TILE // bsz) == 0
    assert TOKEN_TILE % (bsz * SUB_STEPS) == 0 and s % (TOKEN_TILE // bsz) == 0
    n_chunks = s // CHUNK
    gain = norm_gain.reshape(1, D_MODEL).astype(_F32)
    w_bf = w_in.astype(_BF16)
    q, k, v, za, ug, zs = _inproj(x, gain, w_bf)

    tmat, smat, cmat, ar, ai = _ssm_matrices(a_re, a_im, log_dt, b_re, b_im, c_re, c_im,
                                             d_skip)
    yg = _ssm(ug.reshape(N_SSM_GROUPS, n_chunks * bsz, CHUNK_W), tmat, smat, cmat, ar, ai,
              n_chunks, bsz)
    yg = yg.reshape(N_SSM_GROUPS, n_chunks, bsz, CHUNK_W)

    ao = _attention(LOG2_E * attn_sink.astype(_F32), q, k, v, bias, za)

    return _merge(x, 0.5 * gain,
                  w_bf,
                  (0.5 * b_gate).reshape(1, -1).astype(_F32),
                  ao, yg, zs,
                  (0.25 * w_glu).astype(_BF16),
                  (0.5 * b_glu).reshape(1, -1).astype(_F32),
                  w_ba.astype(_BF16),
                  (0.25 * w_bs).astype(_BF16),
                  (0.5 * w_out).astype(_BF16),
                  out_gain.reshape(1, D_MODEL).astype(_F32))


def kernel(x, norm_gain, w_in, b_gate, attn_sink, rel_bias_table, ssm_a_re, ssm_a_im,
           ssm_log_dt, ssm_b_re, ssm_b_im, ssm_c_re, ssm_c_im, ssm_d, w_glu, b_glu,
           w_branch_attn, w_branch_ssm, w_out, final_norm_gain):
    depth = norm_gain.shape[0]
    assert depth == 1, "final norm is fused into the single layer's epilogue"
    bias = _attn_bias(rel_bias_table)
    l = 0
    return _layer(x, norm_gain[l], w_in[l], b_gate[l], attn_sink[l], ssm_a_re[l],
                  ssm_a_im[l], ssm_log_dt[l], ssm_b_re[l], ssm_b_im[l], ssm_c_re[l],
                  ssm_c_im[l], ssm_d[l], w_glu[l], b_glu[l], w_branch_attn[l],
                  w_branch_ssm[l], w_out[l], final_norm_gain, bias)
```

```python
import functools
import math

import jax
import jax.numpy as jnp
import numpy as np
from jax import lax
from jax.experimental import pallas as pl
from jax.experimental.pallas import tpu as pltpu

D_MODEL = 1024
ATTN_HEADS = 8
KV_HEADS = 2
Q_PER_KV = ATTN_HEADS // KV_HEADS
HEAD_DIM = 64
D_ATTN = ATTN_HEADS * HEAD_DIM
D_KV = KV_HEADS * HEAD_DIM
WINDOW = 128
BLOCK = 128
NUM_BUCKETS = 32
MAX_DISTANCE = 128
D_SSM = 512
SSM_GROUP = 16
N_SSM_GROUPS = D_SSM // SSM_GROUP
SSM_STATE = 64
EPS = 1e-6
NEG_INF = -1e30

LANES = 128
SUBLANES = 8
MXU_WIDTH = 256
CHUNK = MXU_WIDTH // SSM_GROUP
CHUNK_W = CHUNK * SSM_GROUP
STATE_W = 4 * SSM_STATE
SLOTS_PER_VREG = LANES // SSM_GROUP
PREP_ROWS = 4 * SSM_GROUP + SUBLANES
SSM_GROUPS_PER_STEP = 4

HEAD_PAIR = LANES // HEAD_DIM
PAIRS_PER_KV = Q_PER_KV // HEAD_PAIR
assert D_KV == LANES and KV_HEADS == HEAD_PAIR
D_KVX = 2 * D_KV
LOG2_E = math.log2(math.e)
Q_SCALE = HEAD_DIM ** -0.5 * LOG2_E
BIAS_ROW = 4 * BLOCK

W_OFF_ZA = D_ATTN + 2 * D_KV
W_OFF_U = W_OFF_ZA + D_ATTN
W_OFF_GATES = W_OFF_U + 2 * D_SSM
GATE_BLOCK = (W_OFF_GATES + 2 * D_MODEL) // 2
assert GATE_BLOCK % LANES == 0 and GATE_BLOCK <= W_OFF_GATES

TOKEN_TILE = 1024
INPROJ_TOKEN_TILE = 2048
SUB_STEPS = CHUNK
VMEM_LIMIT_BYTES = 56 * 1024 * 1024

_F32 = jnp.float32
_BF16 = jnp.bfloat16
_HI = lax.Precision.HIGHEST


def _silu(x):
    return x * (0.5 * jnp.tanh(0.5 * x) + 0.5)


def _twice_sigmoid_of_double(xh):
    return 1.0 + jnp.tanh(xh)


def _twice_gelu(x):
    c = math.sqrt(2.0 / math.pi)
    return x * (1.0 + jnp.tanh(c * (x + 0.044715 * (x * x * x))))


def _rms_normalize(x, gain):
    ms = jnp.mean(x * x, axis=-1, keepdims=True)
    return x * lax.rsqrt(ms + EPS) * gain


def _slot_masks(rows):
    lane = lax.broadcasted_iota(jnp.int32, (rows, LANES), 1)
    return [(lane >= p * SSM_GROUP) & (lane < (p + 1) * SSM_GROUP)
            for p in range(SLOTS_PER_VREG)]


def _to_group_major(u_scr, ug_ref, kk, nb, t0, steps):
    masks = _slot_masks(nb)
    for v in range(D_SSM // LANES):
        for half in range(CHUNK // SLOTS_PER_VREG):
            rolled = []
            for t8 in range(SLOTS_PER_VREG):
                r = t0 + half * SLOTS_PER_VREG + t8
                piece = u_scr[v, pl.ds(r, nb, stride=steps), :]
                rolled.append(pltpu.roll(piece, t8 * SSM_GROUP, 1) if t8 else piece)
            for p0 in range(SLOTS_PER_VREG):
                acc = rolled[0]
                for t8 in range(1, SLOTS_PER_VREG):
                    acc = jnp.where(masks[(p0 + t8) % SLOTS_PER_VREG], rolled[t8], acc)
                g = v * SLOTS_PER_VREG + p0
                ug_ref[g, kk, :, half * LANES:(half + 1) * LANES] = acc.astype(_BF16)


def _from_group_major(yg_ref, y_scr, kk, nb, t0, steps):
    masks = _slot_masks(nb)
    for v in range(D_SSM // LANES):
        for half in range(CHUNK // SLOTS_PER_VREG):
            src = [yg_ref[v * SLOTS_PER_VREG + p0, kk, :,
                          half * LANES:(half + 1) * LANES].astype(_F32)
                   for p0 in range(SLOTS_PER_VREG)]
            for t8 in range(SLOTS_PER_VREG):
                acc = src[0]
                for p0 in range(1, SLOTS_PER_VREG):
                    acc = jnp.where(masks[(p0 + t8) % SLOTS_PER_VREG], src[p0], acc)
                if t8:
                    acc = pltpu.roll(acc, LANES - t8 * SSM_GROUP, 1)
                r = t0 + half * SLOTS_PER_VREG + t8
                y_scr[v, pl.ds(r, nb, stride=steps), :] = acc


def _inproj_kernel(x_ref, gain_ref, w_ref, q_ref, k_ref, v_ref, za_ref, ug_ref, zs_ref,
                   *u_scrs):
    nb, nt = x_ref.shape[0], x_ref.shape[1]
    rows = nb * SUB_STEPS
    for st in range(nt // SUB_STEPS):
        steps = slice(st * SUB_STEPS, (st + 1) * SUB_STEPS)
        x = x_ref[:, steps, :].reshape(rows, D_MODEL)
        h = _rms_normalize(x, gain_ref[...]).astype(_BF16)

        def put(ref, val):
            ref[:, steps, :] = val.astype(_BF16).reshape(nb, SUB_STEPS, val.shape[-1])

        proj = lambda lo, n: jnp.dot(h, w_ref[:, lo:lo + n], preferred_element_type=_F32)
        u_zs = proj(W_OFF_U, 2 * D_SSM)
        for v in range(D_SSM // LANES):
            u_scrs[st][v] = u_zs[:, v * LANES:(v + 1) * LANES]
        for c in range(SUB_STEPS // CHUNK):
            _to_group_major(u_scrs[st], ug_ref, st * (SUB_STEPS // CHUNK) + c, nb, c * CHUNK,
                            SUB_STEPS)
        put(zs_ref, _silu(u_zs[:, D_SSM:]))
        put(za_ref, _silu(proj(W_OFF_ZA, D_ATTN)))
        qkv = proj(0, W_OFF_ZA)
        put(q_ref, qkv[:, :D_ATTN] * Q_SCALE)
        with_swap = lambda a: jnp.concatenate([a, pltpu.roll(a, HEAD_DIM, 1)], axis=1)
        put(k_ref, with_swap(qkv[:, D_ATTN:D_ATTN + D_KV]))
        put(v_ref, with_swap(qkv[:, D_ATTN + D_KV:]))


def _inproj(x, gain, w_in):
    bsz, s, _ = x.shape
    nt = INPROJ_TOKEN_TILE // bsz
    tok = lambda i: (0, i, 0)
    fixed = lambda i: (0, 0)
    outs = [D_ATTN, D_KVX, D_KVX, D_ATTN, None, D_SSM]
    act = lambda n: (pl.BlockSpec((bsz, nt, n), tok), jax.ShapeDtypeStruct((bsz, s, n), _BF16))
    ug = (pl.BlockSpec((N_SSM_GROUPS, nt // CHUNK, bsz, CHUNK_W), lambda i: (0, i, 0, 0)),
          jax.ShapeDtypeStruct((N_SSM_GROUPS, s // CHUNK, bsz, CHUNK_W), _BF16))
    specs, shapes = zip(*[ug if n is None else act(n) for n in outs])
    return pl.pallas_call(
        _inproj_kernel,
        grid=(s // nt,),
        in_specs=[pl.BlockSpec((bsz, nt, D_MODEL), tok),
                  pl.BlockSpec((1, D_MODEL), fixed)]
                 + [pl.BlockSpec((D_MODEL, W_OFF_GATES), fixed)],
        out_specs=list(specs),
        out_shape=list(shapes),
        scratch_shapes=[pltpu.VMEM((D_SSM // LANES, bsz * SUB_STEPS, LANES), _F32)
                        for _ in range(nt // SUB_STEPS)],
        compiler_params=pltpu.CompilerParams(
            dimension_semantics=("arbitrary",), vmem_limit_bytes=VMEM_LIMIT_BYTES),
        name="inproj",
    )(x, gain, w_in)


def _ssm_kernel(n_chunks, rows_per_chunk, u_ref, t_ref, s_ref, c_ref, ar_ref, ai_ref,
                y_ref, s_scr, x_scr):
    rb = rows_per_chunk
    groups = range(u_ref.shape[0])
    for gi in groups:
        s_scr[gi] = jnp.dot(u_ref[gi], s_ref[gi], preferred_element_type=_F32)
    ar = [jnp.broadcast_to(ar_ref[gi], (rb, 2 * SSM_STATE)) for gi in groups]
    ai = [jnp.broadcast_to(ai_ref[gi], (rb, 2 * SSM_STATE)) for gi in groups]
    is_fwd = lax.broadcasted_iota(jnp.int32, (rb, 2 * SSM_STATE), 1) < SSM_STATE
    is_bwd = jnp.logical_not(is_fwd)
    re_cols = pl.ds(0, 2 * SSM_STATE)
    im_cols = pl.ds(2 * SSM_STATE, 2 * SSM_STATE)

    def step(i, carry):
        rows_f = pl.ds(pl.multiple_of(i * rb, rb), rb)
        rows_b = pl.ds(pl.multiple_of((n_chunks - 1 - i) * rb, rb), rb)
        new = []
        for gi in groups:
            st_re, st_im = carry[2 * gi], carry[2 * gi + 1]
            pltpu.store(x_scr.at[gi, rows_f, re_cols], st_re, mask=is_fwd)
            pltpu.store(x_scr.at[gi, rows_f, im_cols], st_im, mask=is_fwd)
            pltpu.store(x_scr.at[gi, rows_b, re_cols], st_re, mask=is_bwd)
            pltpu.store(x_scr.at[gi, rows_b, im_cols], st_im, mask=is_bwd)
            in_re = jnp.where(is_fwd, s_scr[gi, rows_f, re_cols], s_scr[gi, rows_b, re_cols])
            in_im = jnp.where(is_fwd, s_scr[gi, rows_f, im_cols], s_scr[gi, rows_b, im_cols])
            new.append(ar[gi] * st_re - ai[gi] * st_im + in_re)
            new.append(ar[gi] * st_im + ai[gi] * st_re + in_im)
        return tuple(new)

    zero = jnp.zeros((rb, 2 * SSM_STATE), _F32)
    lax.fori_loop(0, n_chunks, step, (zero,) * (2 * len(groups)), unroll=True)

    for gi in groups:
        y = jnp.dot(u_ref[gi], t_ref[gi], preferred_element_type=_F32)
        y = y + lax.dot_general(x_scr[gi].astype(_BF16), c_ref[gi], (((1,), (1,)), ((), ())),
                                preferred_element_type=_F32)
        y_ref[gi] = y.astype(y_ref.dtype)


def _ssm(ug, tmat, smat, cmat, ar, ai, n_chunks, rows_per_chunk):
    g, rows, _ = ug.shape
    gb = SSM_GROUPS_PER_STEP
    per_group = lambda i: (i, 0, 0)
    return pl.pallas_call(
        functools.partial(_ssm_kernel, n_chunks, rows_per_chunk),
        grid=(g // gb,),
        in_specs=[pl.BlockSpec((gb, rows, CHUNK_W), per_group),
                  pl.BlockSpec((gb, CHUNK_W, CHUNK_W), per_group),
                  pl.BlockSpec((gb, CHUNK_W, STATE_W), per_group),
                  pl.BlockSpec((gb, CHUNK_W, STATE_W), per_group),
                  pl.BlockSpec((gb, 1, 2 * SSM_STATE), per_group),
                  pl.BlockSpec((gb, 1, 2 * SSM_STATE), per_group)],
        out_specs=pl.BlockSpec((gb, rows, CHUNK_W), per_group),
        out_shape=jax.ShapeDtypeStruct((g, rows, CHUNK_W), _BF16),
        scratch_shapes=[pltpu.VMEM((gb, rows, STATE_W), _F32),
                        pltpu.VMEM((gb, rows, STATE_W), _F32)],
        compiler_params=pltpu.CompilerParams(
            dimension_semantics=("arbitrary",), vmem_limit_bytes=VMEM_LIMIT_BYTES),
        name="ssm",
    )(ug, tmat, smat, cmat, ar, ai)


def _ssm_matrices(a_re, a_im, log_dt, b_re, b_im, c_re, c_im, d_skip):
    L, G, P, C = CHUNK, N_SSM_GROUPS, SSM_STATE, SSM_GROUP
    both = lambda a: jnp.concatenate([a[0], a[1]], axis=-1)[:, None, :]
    to_rows = lambda a, perm: a.transpose(perm).reshape(G, C, 2 * P)
    dt = jnp.broadcast_to(jnp.exp(log_dt)[..., None], a_re.shape)
    d_row = jnp.tile(d_skip.reshape(G, 1, C), (1, 1, 2 * P // C))
    used = 4 * C + 4
    packed = jnp.concatenate(
        [to_rows(b_re, (1, 3, 0, 2)), to_rows(b_im, (1, 3, 0, 2)),
         to_rows(c_re, (1, 2, 0, 3)), to_rows(c_im, (1, 2, 0, 3)),
         both(a_re), both(a_im), both(dt), d_row,
         jnp.zeros((G, PREP_ROWS - used, 2 * P), _F32)], axis=1)
    gb = SLOTS_PER_VREG
    per_group = lambda *blk: pl.BlockSpec((gb,) + blk, lambda i: (i,) + (0,) * len(blk))
    mat = jax.ShapeDtypeStruct((G, L * C, L * C), _BF16)
    vec = jax.ShapeDtypeStruct((G, 1, 2 * P), _F32)
    return pl.pallas_call(
        _ssm_prep_kernel,
        grid=(G // gb,),
        in_specs=[per_group(PREP_ROWS, 2 * P)],
        out_specs=[per_group(L * C, L * C), per_group(L * C, STATE_W),
                   per_group(L * C, STATE_W), per_group(1, 2 * P), per_group(1, 2 * P)],
        out_shape=[mat, mat, mat, vec, vec],
        compiler_params=pltpu.CompilerParams(dimension_semantics=("arbitrary",)),
        name="ssm_prep",
    )(packed)


def _ssm_prep_kernel(pk_ref, t_ref, s_ref, c_ref, ar_ref, ai_ref):
    for gi in range(pk_ref.shape[0]):
        _ssm_prep_group(gi, pk_ref, t_ref, s_ref, c_ref, ar_ref, ai_ref)


def _ssm_prep_group(gi, pk_ref, t_ref, s_ref, c_ref, ar_ref, ai_ref):
    L, P, C = CHUNK, SSM_STATE, SSM_GROUP
    rows = lambda r0, n: pk_ref[gi, r0:r0 + n]
    g8 = gi % SLOTS_PER_VREG

    def time_of_slot(slot):
        return (jnp.bitwise_and(slot, -SLOTS_PER_VREG)
                + jnp.bitwise_and(slot - g8, SLOTS_PER_VREG - 1))

    def cmul(x_re, x_im, y_re, y_im):
        return x_re * y_re - x_im * y_im, x_re * y_im + x_im * y_re

    a_re, a_im, dt, d_row = [rows(4 * C + r, 1) for r in range(4)]
    lr, li = a_re * dt, a_im * dt

    j = time_of_slot(lax.broadcasted_iota(jnp.int32, (L, 2 * P), 0)).astype(_F32)
    pos = lax.broadcasted_iota(jnp.int32, (L, 2 * P), 0).astype(_F32)
    fwd = lax.broadcasted_iota(jnp.int32, (L, 2 * P), 1) < P
    cj, sj = jnp.cos(li * j), jnp.sin(li * j)
    slot_of = lambda t: (t // SLOTS_PER_VREG) * SLOTS_PER_VREG + (t + g8) % SLOTS_PER_VREG
    in_time_order = lambda a: jnp.concatenate(
        [a[slot_of(t):slot_of(t) + 1] for t in range(L)], axis=0)
    cp, sp = in_time_order(cj), in_time_order(sj)
    krow = lax.broadcasted_iota(jnp.int32, (SUBLANES, 2 * P), 0)
    consts = jnp.where(krow == 0, 1.0, jnp.where(krow == 1, L - 1.0,
                                                 jnp.where(krow == 2, float(L), 0.0)))
    ck, sk = jnp.cos(li * consts), jnp.sin(li * consts)
    (c_1, s_1), (c_m, s_m), (c_l, s_l) = [(ck[r:r + 1], sk[r:r + 1]) for r in range(3)]

    def cpow(tau, trig):
        mag = jnp.exp(lr * tau)
        return mag * trig[0], mag * trig[1]

    minus = lambda c0, s0, c, s: (c0 * c + s0 * s, s0 * c - c0 * s)
    plus = lambda c0, s0, c, s: (c0 * c - s0 * s, s0 * c + c0 * s)
    per_dir = lambda f, b: (jnp.where(fwd, f[0], b[0]), jnp.where(fwd, f[1], b[1]))

    ab_re, ab_im = cpow(jnp.ones_like(lr), (c_1, s_1))
    den = a_re * a_re + a_im * a_im
    co_re = ((ab_re - 1.0) * a_re + ab_im * a_im) / den
    co_im = (ab_im * a_re - (ab_re - 1.0) * a_im) / den
    bb_re, bb_im = cmul(rows(0, C), rows(C, C), co_re, co_im)
    cc_re, cc_im = rows(2 * C, C), rows(3 * C, C)

    def outer(p, m):
        (p_re, p_im), (m_re, m_im) = p, m
        blocks = [cmul(p_re[s:s + 1], p_im[s:s + 1], m_re, m_im) for s in range(L)]
        return (jnp.concatenate([b[0] for b in blocks], axis=0),
                jnp.concatenate([b[1] for b in blocks], axis=0))

    s_re, s_im = outer(cpow(jnp.where(fwd, L - 1 - j, j),
                            per_dir(minus(c_m, s_m, cj, sj), (cj, sj))), (bb_re, bb_im))
    s_ref[gi] = jnp.concatenate([s_re, s_im], axis=1).astype(_BF16)
    w_re, w_im = outer(cpow(jnp.where(fwd, j + 1, L - j),
                            per_dir(plus(c_1, s_1, cj, sj), minus(c_l, s_l, cj, sj))),
                       (cc_re, cc_im))
    wcat = jnp.concatenate([w_re, -w_im], axis=1)
    c_ref[gi] = wcat.astype(_BF16)
    a_re2, a_im2 = outer(cpow(jnp.where(fwd, pos, L - 1 - pos),
                              per_dir((cp, sp), minus(c_m, s_m, cp, sp))), (cc_re, cc_im))
    acat = jnp.concatenate([a_re2, -a_im2], axis=1)
    bcat = jnp.concatenate([bb_re, bb_im], axis=1)
    is_fwd_state = jnp.bitwise_and(lax.broadcasted_iota(jnp.int32, bcat.shape, 1), P) == 0

    def split(a):
        hi = a.astype(_BF16).astype(_F32)
        return hi, a - hi

    def dot_state(a_parts, b_parts):
        (a_hi, a_lo), (b_hi, b_lo) = [[p.astype(_BF16) for p in parts]
                                      for parts in (a_parts, b_parts)]
        d = lambda a, b: lax.dot_general(a, b, (((1,), (1,)), ((), ())),
                                         preferred_element_type=_F32)
        return d(a_hi, b_hi) + d(a_hi, b_lo) + d(a_lo, b_hi)

    b_parts, a_parts = split(bcat), split(acat)
    kf = dot_state([jnp.where(is_fwd_state, p, 0.0) for p in b_parts], a_parts)
    kb = dot_state([jnp.where(is_fwd_state, 0.0, p) for p in b_parts], a_parts)

    lane = lax.broadcasted_iota(jnp.int32, (C, LANES), 1)

    def shift_slots(k, n):
        halves = [k[:, :LANES], k[:, LANES:]]
        if n >= SLOTS_PER_VREG:
            halves, n = halves[::-1], n - SLOTS_PER_VREG
        if n == 0:
            return halves
        r = [pltpu.roll(h, n * C, 1) for h in halves]
        wrapped = lane < n * C
        return [jnp.where(wrapped, r[1], r[0]), jnp.where(wrapped, r[0], r[1])]

    col_t = lax.broadcasted_iota(jnp.int32, (C, L * C), 1)
    blocks = []
    for jt in range(L):
        fwd_part = jnp.concatenate(shift_slots(kf, jt), axis=1)
        bwd_part = jnp.concatenate(shift_slots(kb, (jt + 1) % L), axis=1)
        blk = (jnp.where(col_t >= jt * C, fwd_part, 0.0)
               + jnp.where(col_t < (jt + 1) * C, bwd_part, 0.0))
        blocks.append(jnp.concatenate(
            [pltpu.roll(blk[:, h * LANES:(h + 1) * LANES], g8 * C, 1) if g8 else
             blk[:, h * LANES:(h + 1) * LANES] for h in range(L * C // LANES)], axis=1))
    time_of = lambda s: (s // SLOTS_PER_VREG) * SLOTS_PER_VREG + (s - g8) % SLOTS_PER_VREG
    tm = jnp.concatenate([blocks[time_of(s)] for s in range(L)], axis=0)
    row = lax.broadcasted_iota(jnp.int32, (L * C, L * C), 0)
    col = lax.broadcasted_iota(jnp.int32, (L * C, L * C), 1)
    d_lanes = jnp.concatenate([d_row] * (L * C // (2 * P)), axis=1)
    t_ref[gi] = (tm + jnp.where(row == col, d_lanes, 0.0)).astype(_BF16)
    al_re, al_im = cpow(jnp.full_like(lr, float(L)), (c_l, s_l))
    ar_ref[gi] = al_re
    ai_ref[gi] = al_im


def _attn_kernel(sink_ref, q_ref, k_ref, v_ref, bias_rows_ref, za_ref, o_ref, bias_ref):
    nq = q_ref.shape[1] // BLOCK

    @pl.when(pl.program_id(0) == 0)
    def _():
        qi = lax.broadcasted_iota(jnp.int32, (BLOCK, 3 * BLOCK), 0)
        ki = lax.broadcasted_iota(jnp.int32, (BLOCK, 3 * BLOCK), 1)
        in_band = jnp.abs(ki - BLOCK - qi) <= WINDOW
        for h in range(ATTN_HEADS):
            row = jnp.broadcast_to(bias_rows_ref[h:h + 1, :], (BLOCK, BIAS_ROW))
            rel = pltpu.roll(row, 0, 1, stride=1, stride_axis=0)
            bias_ref[h] = jnp.where(in_band, rel[:, :3 * BLOCK], NEG_INF)

    rows = lambda ref, c: ref[0, c * BLOCK:(c + 1) * BLOCK]
    k_blocks = [rows(k_ref, c) for c in range(nq)]
    v_blocks = [rows(v_ref, c) for c in range(nq)]
    lane = lax.broadcasted_iota(jnp.int32, (BLOCK, LANES), 1)
    halves = [lane < HEAD_DIM, lane >= HEAD_DIM]
    keep = [h.astype(_F32).astype(_BF16) for h in halves]
    slab = lambda blk, j, e: blk[:, (j ^ e) * LANES:((j ^ e) + 1) * LANES]
    k_half = {(j, e): [slab(kb, j, e) * keep[e] for kb in k_blocks]
              for j in range(KV_HEADS) for e in range(HEAD_PAIR)}
    v_half = {(j, e): [jnp.concatenate([slab(vb, j, e) * keep[e], keep[e]], axis=1)
                       for vb in v_blocks]
              for j in range(KV_HEADS) for e in range(HEAD_PAIR)}
    for c in range(nq):
        blocks = [b for b in (c - 1, c, c + 1) if 0 <= b < nq]
        slot0 = blocks[0] - (c - 1)
        nk = len(blocks) * BLOCK
        q = rows(q_ref, c)
        slabs_out = []
        for j in range(KV_HEADS):
            window = lambda halves_of: jnp.concatenate(
                [halves_of[j, e][b] for e in range(HEAD_PAIR) for b in blocks], axis=0)
            kcat = window(k_half)
            vcat = window(v_half)
            qs = jnp.concatenate(
                [q[:, (j * PAIRS_PER_KV + i) * LANES:(j * PAIRS_PER_KV + i + 1) * LANES]
                 for i in range(PAIRS_PER_KV)], axis=0)
            s = lax.dot_general(qs, kcat, (((1,), (1,)), ((), ())),
                                preferred_element_type=_F32)
            p_rows, e_rows = [], []
            for i in range(PAIRS_PER_KV):
                p_lanes, e_sink = [], []
                for e in range(HEAD_PAIR):
                    head = j * Q_PER_KV + i * HEAD_PAIR + e
                    sg = (s[i * BLOCK:(i + 1) * BLOCK, e * nk:(e + 1) * nk]
                          + bias_ref[head, :, slot0 * BLOCK:slot0 * BLOCK + nk])
                    sk = sink_ref[head]
                    m = jnp.maximum(jnp.max(sg, axis=-1, keepdims=True), sk)
                    p_lanes.append(jnp.exp2(sg - m).astype(_BF16))
                    e_sink.append(jnp.broadcast_to(jnp.exp2(sk - m), (BLOCK, LANES)))
                p_rows.append(jnp.concatenate(p_lanes, axis=1))
                e_rows.append(jnp.where(halves[0], e_sink[0], e_sink[1]))
            p = jnp.concatenate(p_rows, axis=0)
            o = jnp.dot(p, vcat, preferred_element_type=_F32)
            den = o[:, LANES:] + jnp.concatenate(e_rows, axis=0)
            on = o[:, :LANES] / den
            slabs_out += [on[i * BLOCK:(i + 1) * BLOCK] for i in range(PAIRS_PER_KV)]
        o_all = jnp.concatenate(slabs_out, axis=1)
        o_ref[0, c * BLOCK:(c + 1) * BLOCK, :] = (
            o_all * rows(za_ref, c).astype(_F32)).astype(o_ref.dtype)


def _attention(sink, q, k, v, bias_rows, za):
    b, s, _ = q.shape
    seq = lambda w: pl.BlockSpec((1, s, w), lambda i: (i, 0, 0))
    return pl.pallas_call(
        _attn_kernel,
        grid=(b,),
        in_specs=[pl.BlockSpec(memory_space=pltpu.SMEM), seq(D_ATTN), seq(D_KVX), seq(D_KVX),
                  pl.BlockSpec(bias_rows.shape, lambda i: (0, 0)), seq(D_ATTN)],
        out_specs=seq(D_ATTN),
        out_shape=jax.ShapeDtypeStruct((b, s, D_ATTN), _BF16),
        scratch_shapes=[pltpu.VMEM((ATTN_HEADS, BLOCK, 3 * BLOCK), _F32)],
        compiler_params=pltpu.CompilerParams(
            dimension_semantics=("arbitrary",), vmem_limit_bytes=VMEM_LIMIT_BYTES),
        name="attn",
    )(sink, q, k, v, bias_rows, za)


def _t5_bucket_np(rel):
    half = NUM_BUCKETS // 2
    ret = (rel > 0).astype(np.int64) * half
    n = np.abs(rel)
    max_exact = half // 2
    nf = np.maximum(n, 1).astype(np.float64)
    large = max_exact + (np.log(nf / max_exact) / math.log(MAX_DISTANCE / max_exact)
                         * (half - max_exact)).astype(np.int64)
    large = np.minimum(large, half - 1)
    return ret + np.where(n < max_exact, n, large)


def _attn_bias(rel_table):
    rel = (np.arange(BIAS_ROW) + BLOCK) % BIAS_ROW - 2 * BLOCK
    onehot = (_t5_bucket_np(rel)[None] == np.arange(NUM_BUCKETS)[:, None])
    return jnp.einsum('bh,br->hr', LOG2_E * rel_table.astype(_F32),
                      jnp.asarray(onehot, _F32), precision=_HI)


def _merge_kernel(x_ref, gain_ref, wg_ref, bg_ref, ao_ref, yg_ref, zs_ref, wglu_ref,
                  bglu_ref, wba_ref, wbs_ref, wout_ref, fgain_ref, o_ref, *y_scrs):
    nb, nt = x_ref.shape[0], x_ref.shape[1]
    rows = nb * SUB_STEPS
    for st in range(nt // SUB_STEPS):
        steps = slice(st * SUB_STEPS, (st + 1) * SUB_STEPS)
        x = x_ref[:, steps, :].reshape(rows, D_MODEL)
        h = _rms_normalize(x, gain_ref[...]).astype(_BF16)
        gates2 = _twice_sigmoid_of_double(
            jnp.dot(h, wg_ref[:, W_OFF_GATES - GATE_BLOCK:], preferred_element_type=_F32)
            + bg_ref[...])
        for c in range(SUB_STEPS // CHUNK):
            _from_group_major(yg_ref, y_scrs[st], st * (SUB_STEPS // CHUNK) + c, nb, c * CHUNK,
                              SUB_STEPS)
        y = jnp.concatenate([y_scrs[st][v] for v in range(D_SSM // LANES)], axis=1)
        y2 = _twice_gelu(y)
        glu_half = (jnp.dot(y2.astype(_BF16), wglu_ref[...], preferred_element_type=_F32)
                    + bglu_ref[...])
        ssm4 = (y2 * _twice_sigmoid_of_double(glu_half)
                * zs_ref[:, steps, :].reshape(rows, D_SSM).astype(_F32))
        pa = jnp.dot(ao_ref[:, steps, :].reshape(rows, D_ATTN), wba_ref[...],
                     preferred_element_type=_F32)
        ps = jnp.dot(ssm4.astype(_BF16), wbs_ref[...], preferred_element_type=_F32)
        merged2 = gates2[:, :D_MODEL] * pa + gates2[:, D_MODEL:] * ps
        xn = x + jnp.dot(merged2.astype(_BF16), wout_ref[...], preferred_element_type=_F32)
        o_ref[:, steps, :] = _rms_normalize(xn, fgain_ref[...]).reshape(nb, SUB_STEPS, D_MODEL)


def _merge(x, gain, wg, bg, ao, yg, zs, wglu, bglu, wba, wbs, wout, fgain):
    bsz, s, _ = x.shape
    nt = TOKEN_TILE // bsz
    tok = lambda i: (0, i, 0)
    fixed = lambda i: (0, 0)
    full = lambda a: pl.BlockSpec(a.shape, fixed)
    act = lambda n: pl.BlockSpec((bsz, nt, n), tok)
    return pl.pallas_call(
        _merge_kernel,
        grid=(s // nt,),
        in_specs=[act(D_MODEL), full(gain),
                  pl.BlockSpec((D_MODEL, GATE_BLOCK), lambda i: (0, 1)),
                  full(bg), act(D_ATTN),
                  pl.BlockSpec((N_SSM_GROUPS, nt // CHUNK, bsz, CHUNK_W),
                               lambda i: (0, i, 0, 0)),
                  act(D_SSM), full(wglu), full(bglu), full(wba), full(wbs), full(wout),
                  full(fgain)],
        out_specs=act(D_MODEL),
        out_shape=jax.ShapeDtypeStruct((bsz, s, D_MODEL), _F32),
        scratch_shapes=[pltpu.VMEM((D_SSM // LANES, bsz * SUB_STEPS, LANES), _F32)
                        for _ in range(nt // SUB_STEPS)],
        compiler_params=pltpu.CompilerParams(
            dimension_semantics=("arbitrary",), vmem_limit_bytes=VMEM_LIMIT_BYTES),
        name="merge",
    )(x, gain, wg, bg, ao, yg, zs, wglu, bglu, wba, wbs, wout, fgain)


def _layer(x, norm_gain, w_in, b_gate, attn_sink, a_re, a_im, log_dt, b_re, b_im, c_re, c_im,
           d_skip, w_glu, b_glu, w_ba, w_bs, w_out, out_gain, bias):
    bsz, s, d = x.shape
    assert d == D_MODEL and x.dtype == _F32 and w_in.shape == (D_MODEL, W_OFF_GATES + 2 * D_MODEL)
    assert bsz % 8 == 0 and s % BLOCK == 0
    assert INPROJ_TOKEN_TILE % (bsz * SUB_STEPS) == 0 and s % (INPROJ_TOKEN_TILE // bsz) == 0
    assert TOKEN_TILE % (bsz * SUB_STEPS) == 0 and s % (TOKEN_TILE // bsz) == 0
    n_chunks = s // CHUNK
    gain = norm_gain.reshape(1, D_MODEL).astype(_F32)
    w_bf = w_in.astype(_BF16)
    q, k, v, za, ug, zs = _inproj(x, gain, w_bf)

    tmat, smat, cmat, ar, ai = _ssm_matrices(a_re, a_im, log_dt, b_re, b_im, c_re, c_im,
                                             d_skip)
    yg = _ssm(ug.reshape(N_SSM_GROUPS, n_chunks * bsz, CHUNK_W), tmat, smat, cmat, ar, ai,
              n_chunks, bsz)
    yg = yg.reshape(N_SSM_GROUPS, n_chunks, bsz, CHUNK_W)

    ao = _attention(LOG2_E * attn_sink.astype(_F32), q, k, v, bias, za)

    return _merge(x, 0.5 * gain,
                  w_bf,
                  (0.5 * b_gate).reshape(1, -1).astype(_F32),
                  ao, yg, zs,
                  (0.25 * w_glu).astype(_BF16),
                  (0.5 * b_glu).reshape(1, -1).astype(_F32),
                  w_ba.astype(_BF16),
                  (0.25 * w_bs).astype(_BF16),
                  (0.5 * w_out).astype(_BF16),
                  out_gain.reshape(1, D_MODEL).astype(_F32))


def kernel(x, norm_gain, w_in, b_gate, attn_sink, rel_bias_table, ssm_a_re, ssm_a_im,
           ssm_log_dt, ssm_b_re, ssm_b_im, ssm_c_re, ssm_c_im, ssm_d, w_glu, b_glu,
           w_branch_attn, w_branch_ssm, w_out, final_norm_gain):
    depth = norm_gain.shape[0]
    assert depth == 1, "final norm is fused into the single layer's epilogue"
    bias = _attn_bias(rel_bias_table)
    l = 0
    return _layer(x, norm_gain[l], w_in[l], b_gate[l], attn_sink[l], ssm_a_re[l],
                  ssm_a_im[l], ssm_log_dt[l], ssm_b_re[l], ssm_b_im[l], ssm_c_re[l],
                  ssm_c_im[l], ssm_d[l], w_glu[l], b_glu[l], w_branch_attn[l],
                  w_branch_ssm[l], w_out[l], final_norm_gain, bias)
```

```python
import functools
import math

import jax
import jax.numpy as jnp
import numpy as np
from jax import lax
from jax.experimental import pallas as pl
from jax.experimental.pallas import tpu as pltpu

D_MODEL = 1024
ATTN_HEADS = 8
KV_HEADS = 2
Q_PER_KV = ATTN_HEADS // KV_HEADS
HEAD_DIM = 64
D_ATTN = ATTN_HEADS * HEAD_DIM
D_KV = KV_HEADS * HEAD_DIM
WINDOW = 128
BLOCK = 128
NUM_BUCKETS = 32
MAX_DISTANCE = 128
D_SSM = 512
SSM_GROUP = 16
N_SSM_GROUPS = D_SSM // SSM_GROUP
SSM_STATE = 64
EPS = 1e-6
NEG_INF = -1e30

LANES = 128
SUBLANES = 8
MXU_WIDTH = 256
CHUNK = MXU_WIDTH // SSM_GROUP
CHUNK_W = CHUNK * SSM_GROUP
STATE_W = 4 * SSM_STATE
SLOTS_PER_VREG = LANES // SSM_GROUP

HEAD_PAIR = LANES // HEAD_DIM
PAIRS_PER_KV = Q_PER_KV // HEAD_PAIR
assert D_KV == LANES and KV_HEADS == HEAD_PAIR
D_KVX = 2 * D_KV
LOG2_E = math.log2(math.e)
Q_SCALE = HEAD_DIM ** -0.5 * LOG2_E
BIAS_ROW = 4 * BLOCK

W_OFF_ZA = D_ATTN + 2 * D_KV
W_OFF_U = W_OFF_ZA + D_ATTN
W_OFF_GATES = W_OFF_U + 2 * D_SSM
GATE_BLOCK = (W_OFF_GATES + 2 * D_MODEL) // 2
assert GATE_BLOCK % LANES == 0 and GATE_BLOCK <= W_OFF_GATES

TOKEN_TILE = 1024
INPROJ_TOKEN_TILE = 2048
SUB_STEPS = CHUNK
VMEM_LIMIT_BYTES = 56 * 1024 * 1024

_F32 = jnp.float32
_BF16 = jnp.bfloat16
_HI = lax.Precision.HIGHEST


def _silu(x):
    return x * (0.5 * jnp.tanh(0.5 * x) + 0.5)


def _twice_sigmoid_of_double(xh):
    return 1.0 + jnp.tanh(xh)


def _twice_gelu(x):
    c = math.sqrt(2.0 / math.pi)
    return x * (1.0 + jnp.tanh(c * (x + 0.044715 * (x * x * x))))


def _rms_normalize(x, gain):
    ms = jnp.mean(x * x, axis=-1, keepdims=True)
    return x * lax.rsqrt(ms + EPS) * gain


def _slot_masks(rows):
    lane = lax.broadcasted_iota(jnp.int32, (rows, LANES), 1)
    return [(lane >= p * SSM_GROUP) & (lane < (p + 1) * SSM_GROUP)
            for p in range(SLOTS_PER_VREG)]


def _to_group_major(u_scr, ug_ref, kk, nb, t0, steps):
    masks = _slot_masks(nb)
    for v in range(D_SSM // LANES):
        for half in range(CHUNK // SLOTS_PER_VREG):
            rolled = []
            for t8 in range(SLOTS_PER_VREG):
                r = t0 + half * SLOTS_PER_VREG + t8
                piece = u_scr[v, pl.ds(r, nb, stride=steps), :]
                rolled.append(pltpu.roll(piece, t8 * SSM_GROUP, 1) if t8 else piece)
            for p0 in range(SLOTS_PER_VREG):
                acc = rolled[0]
                for t8 in range(1, SLOTS_PER_VREG):
                    acc = jnp.where(masks[(p0 + t8) % SLOTS_PER_VREG], rolled[t8], acc)
                g = v * SLOTS_PER_VREG + p0
                ug_ref[g, kk, :, half * LANES:(half + 1) * LANES] = acc.astype(_BF16)


def _from_group_major(yg_ref, y_scr, kk, nb, t0, steps):
    masks = _slot_masks(nb)
    for v in range(D_SSM // LANES):
        for half in range(CHUNK // SLOTS_PER_VREG):
            src = [yg_ref[v * SLOTS_PER_VREG + p0, kk, :,
                          half * LANES:(half + 1) * LANES].astype(_F32)
                   for p0 in range(SLOTS_PER_VREG)]
            for t8 in range(SLOTS_PER_VREG):
                acc = src[0]
                for p0 in range(1, SLOTS_PER_VREG):
                    acc = jnp.where(masks[(p0 + t8) % SLOTS_PER_VREG], src[p0], acc)
                if t8:
                    acc = pltpu.roll(acc, LANES - t8 * SSM_GROUP, 1)
                r = t0 + half * SLOTS_PER_VREG + t8
                y_scr[v, pl.ds(r, nb, stride=steps), :] = acc


def _inproj_kernel(x_ref, gain_ref, w_ref, q_ref, k_ref, v_ref, za_ref, ug_ref, zs_ref,
                   *u_scrs):
    nb, nt = x_ref.shape[0], x_ref.shape[1]
    rows = nb * SUB_STEPS
    for st in range(nt // SUB_STEPS):
        steps = slice(st * SUB_STEPS, (st + 1) * SUB_STEPS)
        x = x_ref[:, steps, :].reshape(rows, D_MODEL)
        h = _rms_normalize(x, gain_ref[...]).astype(_BF16)

        def put(ref, val):
            ref[:, steps, :] = val.astype(_BF16).reshape(nb, SUB_STEPS, val.shape[-1])

        proj = lambda lo, n: jnp.dot(h, w_ref[:, lo:lo + n], preferred_element_type=_F32)
        u_zs = proj(W_OFF_U, 2 * D_SSM)
        for v in range(D_SSM // LANES):
            u_scrs[st][v] = u_zs[:, v * LANES:(v + 1) * LANES]
        for c in range(SUB_STEPS // CHUNK):
            _to_group_major(u_scrs[st], ug_ref, st * (SUB_STEPS // CHUNK) + c, nb, c * CHUNK,
                            SUB_STEPS)
        put(zs_ref, _silu(u_zs[:, D_SSM:]))
        put(za_ref, _silu(proj(W_OFF_ZA, D_ATTN)))
        qkv = proj(0, W_OFF_ZA)
        put(q_ref, qkv[:, :D_ATTN] * Q_SCALE)
        with_swap = lambda a: jnp.concatenate([a, pltpu.roll(a, HEAD_DIM, 1)], axis=1)
        put(k_ref, with_swap(qkv[:, D_ATTN:D_ATTN + D_KV]))
        put(v_ref, with_swap(qkv[:, D_ATTN + D_KV:]))


def _inproj(x, gain, w_in):
    bsz, s, _ = x.shape
    nt = INPROJ_TOKEN_TILE // bsz
    tok = lambda i: (0, i, 0)
    fixed = lambda i: (0, 0)
    outs = [D_ATTN, D_KVX, D_KVX, D_ATTN, None, D_SSM]
    act = lambda n: (pl.BlockSpec((bsz, nt, n), tok), jax.ShapeDtypeStruct((bsz, s, n), _BF16))
    ug = (pl.BlockSpec((N_SSM_GROUPS, nt // CHUNK, bsz, CHUNK_W), lambda i: (0, i, 0, 0)),
          jax.ShapeDtypeStruct((N_SSM_GROUPS, s // CHUNK, bsz, CHUNK_W), _BF16))
    specs, shapes = zip(*[ug if n is None else act(n) for n in outs])
    return pl.pallas_call(
        _inproj_kernel,
        grid=(s // nt,),
        in_specs=[pl.BlockSpec((bsz, nt, D_MODEL), tok),
                  pl.BlockSpec((1, D_MODEL), fixed)]
                 + [pl.BlockSpec((D_MODEL, W_OFF_GATES), fixed)],
        out_specs=list(specs),
        out_shape=list(shapes),
        scratch_shapes=[pltpu.VMEM((D_SSM // LANES, bsz * SUB_STEPS, LANES), _F32)
                        for _ in range(nt // SUB_STEPS)],
        compiler_params=pltpu.CompilerParams(
            dimension_semantics=("arbitrary",), vmem_limit_bytes=VMEM_LIMIT_BYTES),
        name="inproj",
    )(x, gain, w_in)


def _ssm_phases(n_chunks, rows_per_chunk, u_ref, t_ref, s_ref, c_ref, ar_ref, ai_ref,
                y_ref, s_scr, x_scr):
    rb = rows_per_chunk
    groups = range(u_ref.shape[0])
    is_fwd = lax.broadcasted_iota(jnp.int32, (rb, 2 * SSM_STATE), 1) < SSM_STATE
    is_bwd = jnp.logical_not(is_fwd)
    re_cols = pl.ds(0, 2 * SSM_STATE)
    im_cols = pl.ds(2 * SSM_STATE, 2 * SSM_STATE)
    zero = jnp.zeros((rb, 2 * SSM_STATE), _F32)
    carry = [zero] * (2 * len(groups))
    a_re = [jnp.broadcast_to(ar_ref[gi], (rb, 2 * SSM_STATE)) for gi in groups]
    a_im = [jnp.broadcast_to(ai_ref[gi], (rb, 2 * SSM_STATE)) for gi in groups]

    def begin():
        for gi in groups:
            s_scr[gi] = jnp.dot(u_ref[gi], s_ref[gi], preferred_element_type=_F32)

    def scan(i0, i1):
        for i in range(i0, i1):
            rows_f = pl.ds(i * rb, rb)
            rows_b = pl.ds((n_chunks - 1 - i) * rb, rb)
            for gi in groups:
                st_re, st_im = carry[2 * gi], carry[2 * gi + 1]
                pltpu.store(x_scr.at[gi, rows_f, re_cols], st_re, mask=is_fwd)
                pltpu.store(x_scr.at[gi, rows_f, im_cols], st_im, mask=is_fwd)
                pltpu.store(x_scr.at[gi, rows_b, re_cols], st_re, mask=is_bwd)
                pltpu.store(x_scr.at[gi, rows_b, im_cols], st_im, mask=is_bwd)
                in_re = jnp.where(is_fwd, s_scr[gi, rows_f, re_cols], s_scr[gi, rows_b, re_cols])
                in_im = jnp.where(is_fwd, s_scr[gi, rows_f, im_cols], s_scr[gi, rows_b, im_cols])
                carry[2 * gi] = a_re[gi] * st_re - a_im[gi] * st_im + in_re
                carry[2 * gi + 1] = a_re[gi] * st_im + a_im[gi] * st_re + in_im

    def finish(gi):
        y = jnp.dot(u_ref[gi], t_ref[gi], preferred_element_type=_F32)
        y = y + lax.dot_general(x_scr[gi].astype(_BF16), c_ref[gi], (((1,), (1,)), ((), ())),
                                preferred_element_type=_F32)
        y_ref[gi] = y.astype(y_ref.dtype)

    return begin, scan, finish


def _attn_ssm_kernel(n_chunks, rows_per_chunk, sink_ref, q_ref, k_ref, v_ref, bias_rows_ref,
                     za_ref, u_ref, t_ref, s_ref, c_ref, ar_ref, ai_ref, o_ref, y_ref,
                     bias_scr, s_scr, x_scr):
    begin, scan, finish = _ssm_phases(n_chunks, rows_per_chunk, u_ref, t_ref, s_ref, c_ref,
                                      ar_ref, ai_ref, y_ref, s_scr, x_scr)
    n_groups = u_ref.shape[0]
    nq = q_ref.shape[1] // BLOCK
    scan_blocks = nq // 2
    per_block = n_chunks // scan_blocks
    assert per_block * scan_blocks == n_chunks and nq - scan_blocks >= n_groups

    def after_block(c):
        if c < scan_blocks:
            scan(c * per_block, (c + 1) * per_block)
        elif c - scan_blocks < n_groups:
            finish(c - scan_blocks)

    begin()
    _attn_kernel(sink_ref, q_ref, k_ref, v_ref, bias_rows_ref, za_ref, o_ref, bias_scr,
                 after_block=after_block)


def _attn_ssm(sink, q, k, v, bias_rows, za, ug, tmat, smat, cmat, ar, ai, n_chunks,
              rows_per_chunk):
    b, s, _ = q.shape
    g, rows, _ = ug.shape
    assert g % b == 0
    gb = g // b
    seq = lambda w: pl.BlockSpec((1, s, w), lambda i: (i, 0, 0))
    per_group = lambda *blk: pl.BlockSpec((gb,) + blk, lambda i: (i, 0, 0))
    return pl.pallas_call(
        functools.partial(_attn_ssm_kernel, n_chunks, rows_per_chunk),
        grid=(b,),
        in_specs=[pl.BlockSpec(memory_space=pltpu.SMEM), seq(D_ATTN), seq(D_KVX), seq(D_KVX),
                  pl.BlockSpec(bias_rows.shape, lambda i: (0, 0)), seq(D_ATTN),
                  per_group(rows, CHUNK_W), per_group(CHUNK_W, CHUNK_W),
                  per_group(CHUNK_W, STATE_W), per_group(CHUNK_W, STATE_W),
                  per_group(1, 2 * SSM_STATE), per_group(1, 2 * SSM_STATE)],
        out_specs=[seq(D_ATTN), per_group(rows, CHUNK_W)],
        out_shape=[jax.ShapeDtypeStruct((b, s, D_ATTN), _BF16),
                   jax.ShapeDtypeStruct((g, rows, CHUNK_W), _BF16)],
        scratch_shapes=[pltpu.VMEM((ATTN_HEADS, BLOCK, 3 * BLOCK), _F32),
                        pltpu.VMEM((gb, rows, STATE_W), _F32),
                        pltpu.VMEM((gb, rows, STATE_W), _F32)],
        compiler_params=pltpu.CompilerParams(
            dimension_semantics=("arbitrary",), vmem_limit_bytes=VMEM_LIMIT_BYTES),
        name="attn_ssm",
    )(sink, q, k, v, bias_rows, za, ug, tmat, smat, cmat, ar, ai)


def _ssm_matrices(a_re, a_im, log_dt, b_re, b_im, c_re, c_im, d_skip):
    L, G, P, C = CHUNK, N_SSM_GROUPS, SSM_STATE, SSM_GROUP
    both = lambda a: jnp.concatenate([a[0], a[1]], axis=-1)
    dt = jnp.broadcast_to(jnp.exp(log_dt)[..., None], a_re.shape)
    lam = jnp.stack([both(a_re), both(a_im), both(dt)], axis=1)
    bt = jnp.stack([b_re, b_im], axis=0).transpose(2, 0, 4, 1, 3).reshape(G, 2, C, 2 * P)
    ct = jnp.stack([c_re, c_im], axis=0).transpose(2, 0, 3, 1, 4).reshape(G, 2, C, 2 * P)
    dvec = jnp.tile(d_skip.reshape(G, 1, C), (1, 1, L))
    gb = SLOTS_PER_VREG
    per_group = lambda *blk: pl.BlockSpec((gb,) + blk, lambda i: (i,) + (0,) * len(blk))
    mat = jax.ShapeDtypeStruct((G, L * C, L * C), _BF16)
    vec = jax.ShapeDtypeStruct((G, 1, 2 * P), _F32)
    return pl.pallas_call(
        _ssm_prep_kernel,
        grid=(G // gb,),
        in_specs=[per_group(3, 2 * P), per_group(2, C, 2 * P), per_group(2, C, 2 * P),
                  per_group(1, L * C)],
        out_specs=[per_group(L * C, L * C), per_group(L * C, STATE_W),
                   per_group(L * C, STATE_W), per_group(1, 2 * P), per_group(1, 2 * P)],
        out_shape=[mat, mat, mat, vec, vec],
        compiler_params=pltpu.CompilerParams(dimension_semantics=("arbitrary",)),
        name="ssm_prep",
    )(lam, bt, ct, dvec)


def _ssm_prep_kernel(lam_ref, bt_ref, ct_ref, dvec_ref, t_ref, s_ref, c_ref, ar_ref, ai_ref):
    for gi in range(lam_ref.shape[0]):
        _ssm_prep_group(gi, lam_ref, bt_ref, ct_ref, dvec_ref, t_ref, s_ref, c_ref, ar_ref,
                        ai_ref)


def _ssm_prep_group(gi, lam_ref, bt_ref, ct_ref, dvec_ref, t_ref, s_ref, c_ref, ar_ref,
                    ai_ref):
    L, P, C = CHUNK, SSM_STATE, SSM_GROUP
    g8 = gi % SLOTS_PER_VREG

    def time_of_slot(slot):
        return (jnp.bitwise_and(slot, -SLOTS_PER_VREG)
                + jnp.bitwise_and(slot - g8, SLOTS_PER_VREG - 1))

    def cmul(x_re, x_im, y_re, y_im):
        return x_re * y_re - x_im * y_im, x_re * y_im + x_im * y_re

    a_re, a_im, dt = lam_ref[gi, 0:1], lam_ref[gi, 1:2], lam_ref[gi, 2:3]
    lr, li = a_re * dt, a_im * dt

    j = time_of_slot(lax.broadcasted_iota(jnp.int32, (L, 2 * P), 0)).astype(_F32)
    pos = lax.broadcasted_iota(jnp.int32, (L, 2 * P), 0).astype(_F32)
    fwd = lax.broadcasted_iota(jnp.int32, (L, 2 * P), 1) < P
    cj, sj = jnp.cos(li * j), jnp.sin(li * j)
    slot_of = lambda t: (t // SLOTS_PER_VREG) * SLOTS_PER_VREG + (t + g8) % SLOTS_PER_VREG
    in_time_order = lambda a: jnp.concatenate(
        [a[slot_of(t):slot_of(t) + 1] for t in range(L)], axis=0)
    cp, sp = in_time_order(cj), in_time_order(sj)
    krow = lax.broadcasted_iota(jnp.int32, (SUBLANES, 2 * P), 0)
    consts = jnp.where(krow == 0, 1.0, jnp.where(krow == 1, L - 1.0,
                                                 jnp.where(krow == 2, float(L), 0.0)))
    ck, sk = jnp.cos(li * consts), jnp.sin(li * consts)
    (c_1, s_1), (c_m, s_m), (c_l, s_l) = [(ck[r:r + 1], sk[r:r + 1]) for r in range(3)]

    def cpow(tau, trig):
        mag = jnp.exp(lr * tau)
        return mag * trig[0], mag * trig[1]

    minus = lambda c0, s0, c, s: (c0 * c + s0 * s, s0 * c - c0 * s)
    plus = lambda c0, s0, c, s: (c0 * c - s0 * s, s0 * c + c0 * s)
    per_dir = lambda f, b: (jnp.where(fwd, f[0], b[0]), jnp.where(fwd, f[1], b[1]))

    ab_re, ab_im = cpow(jnp.ones_like(lr), (c_1, s_1))
    den = a_re * a_re + a_im * a_im
    co_re = ((ab_re - 1.0) * a_re + ab_im * a_im) / den
    co_im = (ab_im * a_re - (ab_re - 1.0) * a_im) / den
    bb_re, bb_im = cmul(bt_ref[gi, 0], bt_ref[gi, 1], co_re, co_im)
    cc_re, cc_im = ct_ref[gi, 0], ct_ref[gi, 1]

    def outer(p, m):
        (p_re, p_im), (m_re, m_im) = p, m
        blocks = [cmul(p_re[s:s + 1], p_im[s:s + 1], m_re, m_im) for s in range(L)]
        return (jnp.concatenate([b[0] for b in blocks], axis=0),
                jnp.concatenate([b[1] for b in blocks], axis=0))

    s_re, s_im = outer(cpow(jnp.where(fwd, L - 1 - j, j),
                            per_dir(minus(c_m, s_m, cj, sj), (cj, sj))), (bb_re, bb_im))
    s_ref[gi] = jnp.concatenate([s_re, s_im], axis=1).astype(_BF16)
    w_re, w_im = outer(cpow(jnp.where(fwd, j + 1, L - j),
                            per_dir(plus(c_1, s_1, cj, sj), minus(c_l, s_l, cj, sj))),
                       (cc_re, cc_im))
    wcat = jnp.concatenate([w_re, -w_im], axis=1)
    c_ref[gi] = wcat.astype(_BF16)
    a_re2, a_im2 = outer(cpow(jnp.where(fwd, pos, L - 1 - pos),
                              per_dir((cp, sp), minus(c_m, s_m, cp, sp))), (cc_re, cc_im))
    acat = jnp.concatenate([a_re2, -a_im2], axis=1)
    bcat = jnp.concatenate([bb_re, bb_im], axis=1)
    is_fwd_state = jnp.bitwise_and(lax.broadcasted_iota(jnp.int32, bcat.shape, 1), P) == 0

    def split(a):
        hi = a.astype(_BF16).astype(_F32)
        return hi, a - hi

    def dot_state(a_parts, b_parts):
        (a_hi, a_lo), (b_hi, b_lo) = [[p.astype(_BF16) for p in parts]
                                      for parts in (a_parts, b_parts)]
        d = lambda a, b: lax.dot_general(a, b, (((1,), (1,)), ((), ())),
                                         preferred_element_type=_F32)
        return d(a_hi, b_hi) + d(a_hi, b_lo) + d(a_lo, b_hi)

    b_parts, a_parts = split(bcat), split(acat)
    kf = dot_state([jnp.where(is_fwd_state, p, 0.0) for p in b_parts], a_parts)
    kb = dot_state([jnp.where(is_fwd_state, 0.0, p) for p in b_parts], a_parts)

    lane = lax.broadcasted_iota(jnp.int32, (C, LANES), 1)

    def shift_slots(k, n):
        halves = [k[:, :LANES], k[:, LANES:]]
        if n >= SLOTS_PER_VREG:
            halves, n = halves[::-1], n - SLOTS_PER_VREG
        if n == 0:
            return halves
        r = [pltpu.roll(h, n * C, 1) for h in halves]
        wrapped = lane < n * C
        return [jnp.where(wrapped, r[1], r[0]), jnp.where(wrapped, r[0], r[1])]

    col_t = lax.broadcasted_iota(jnp.int32, (C, L * C), 1)
    blocks = []
    for jt in range(L):
        fwd_part = jnp.concatenate(shift_slots(kf, jt), axis=1)
        bwd_part = jnp.concatenate(shift_slots(kb, (jt + 1) % L), axis=1)
        blk = (jnp.where(col_t >= jt * C, fwd_part, 0.0)
               + jnp.where(col_t < (jt + 1) * C, bwd_part, 0.0))
        blocks.append(jnp.concatenate(
            [pltpu.roll(blk[:, h * LANES:(h + 1) * LANES], g8 * C, 1) if g8 else
             blk[:, h * LANES:(h + 1) * LANES] for h in range(L * C // LANES)], axis=1))
    time_of = lambda s: (s // SLOTS_PER_VREG) * SLOTS_PER_VREG + (s - g8) % SLOTS_PER_VREG
    tm = jnp.concatenate([blocks[time_of(s)] for s in range(L)], axis=0)
    row = lax.broadcasted_iota(jnp.int32, (L * C, L * C), 0)
    col = lax.broadcasted_iota(jnp.int32, (L * C, L * C), 1)
    t_ref[gi] = (tm + jnp.where(row == col, dvec_ref[gi], 0.0)).astype(_BF16)
    al_re, al_im = cpow(jnp.full_like(lr, float(L)), (c_l, s_l))
    ar_ref[gi] = al_re
    ai_ref[gi] = al_im


def _attn_kernel(sink_ref, q_ref, k_ref, v_ref, bias_rows_ref, za_ref, o_ref, bias_ref,
                 after_block=lambda c: None):
    nq = q_ref.shape[1] // BLOCK

    @pl.when(pl.program_id(0) == 0)
    def _():
        qi = lax.broadcasted_iota(jnp.int32, (BLOCK, 3 * BLOCK), 0)
        ki = lax.broadcasted_iota(jnp.int32, (BLOCK, 3 * BLOCK), 1)
        in_band = jnp.abs(ki - BLOCK - qi) <= WINDOW
        for h in range(ATTN_HEADS):
            row = jnp.broadcast_to(bias_rows_ref[h:h + 1, :], (BLOCK, BIAS_ROW))
            rel = pltpu.roll(row, 0, 1, stride=1, stride_axis=0)
            bias_ref[h] = jnp.where(in_band, rel[:, :3 * BLOCK], NEG_INF)

    rows = lambda ref, c: ref[0, c * BLOCK:(c + 1) * BLOCK]
    k_blocks = [rows(k_ref, c) for c in range(nq)]
    v_blocks = [rows(v_ref, c) for c in range(nq)]
    lane = lax.broadcasted_iota(jnp.int32, (BLOCK, LANES), 1)
    halves = [lane < HEAD_DIM, lane >= HEAD_DIM]
    keep = [h.astype(_F32).astype(_BF16) for h in halves]
    slab = lambda blk, j, e: blk[:, (j ^ e) * LANES:((j ^ e) + 1) * LANES]
    k_half = {(j, e): [slab(kb, j, e) * keep[e] for kb in k_blocks]
              for j in range(KV_HEADS) for e in range(HEAD_PAIR)}
    v_half = {(j, e): [jnp.concatenate([slab(vb, j, e) * keep[e], keep[e]], axis=1)
                       for vb in v_blocks]
              for j in range(KV_HEADS) for e in range(HEAD_PAIR)}
    for c in range(nq):
        blocks = [b for b in (c - 1, c, c + 1) if 0 <= b < nq]
        slot0 = blocks[0] - (c - 1)
        nk = len(blocks) * BLOCK
        q = rows(q_ref, c)
        slabs_out = []
        for j in range(KV_HEADS):
            window = lambda halves_of: jnp.concatenate(
                [halves_of[j, e][b] for e in range(HEAD_PAIR) for b in blocks], axis=0)
            kcat = window(k_half)
            vcat = window(v_half)
            qs = jnp.concatenate(
                [q[:, (j * PAIRS_PER_KV + i) * LANES:(j * PAIRS_PER_KV + i + 1) * LANES]
                 for i in range(PAIRS_PER_KV)], axis=0)
            s = lax.dot_general(qs, kcat, (((1,), (1,)), ((), ())),
                                preferred_element_type=_F32)
            p_rows, e_rows = [], []
            for i in range(PAIRS_PER_KV):
                p_lanes, e_sink = [], []
                for e in range(HEAD_PAIR):
                    head = j * Q_PER_KV + i * HEAD_PAIR + e
                    sg = (s[i * BLOCK:(i + 1) * BLOCK, e * nk:(e + 1) * nk]
                          + bias_ref[head, :, slot0 * BLOCK:slot0 * BLOCK + nk])
                    sk = sink_ref[head]
                    m = jnp.maximum(jnp.max(sg, axis=-1, keepdims=True), sk)
                    p_lanes.append(jnp.exp2(sg - m).astype(_BF16))
                    e_sink.append(jnp.broadcast_to(jnp.exp2(sk - m), (BLOCK, LANES)))
                p_rows.append(jnp.concatenate(p_lanes, axis=1))
                e_rows.append(jnp.where(halves[0], e_sink[0], e_sink[1]))
            p = jnp.concatenate(p_rows, axis=0)
            o = jnp.dot(p, vcat, preferred_element_type=_F32)
            den = o[:, LANES:] + jnp.concatenate(e_rows, axis=0)
            on = o[:, :LANES] / den
            slabs_out += [on[i * BLOCK:(i + 1) * BLOCK] for i in range(PAIRS_PER_KV)]
        o_all = jnp.concatenate(slabs_out, axis=1)
        o_ref[0, c * BLOCK:(c + 1) * BLOCK, :] = (
            o_all * rows(za_ref, c).astype(_F32)).astype(o_ref.dtype)
        after_block(c)


def _t5_bucket_np(rel):
    half = NUM_BUCKETS // 2
    ret = (rel > 0).astype(np.int64) * half
    n = np.abs(rel)
    max_exact = half // 2
    nf = np.maximum(n, 1).astype(np.float64)
    large = max_exact + (np.log(nf / max_exact) / math.log(MAX_DISTANCE / max_exact)
                         * (half - max_exact)).astype(np.int64)
    large = np.minimum(large, half - 1)
    return ret + np.where(n < max_exact, n, large)


def _attn_bias(rel_table):
    rel = (np.arange(BIAS_ROW) + BLOCK) % BIAS_ROW - 2 * BLOCK
    onehot = (_t5_bucket_np(rel)[None] == np.arange(NUM_BUCKETS)[:, None])
    return jnp.einsum('bh,br->hr', LOG2_E * rel_table.astype(_F32),
                      jnp.asarray(onehot, _F32), precision=_HI)


def _merge_kernel(x_ref, gain_ref, wg_ref, bg_ref, ao_ref, yg_ref, zs_ref, wglu_ref,
                  bglu_ref, wba_ref, wbs_ref, wout_ref, fgain_ref, o_ref, *y_scrs):
    nb, nt = x_ref.shape[0], x_ref.shape[1]
    rows = nb * SUB_STEPS
    for st in range(nt // SUB_STEPS):
        steps = slice(st * SUB_STEPS, (st + 1) * SUB_STEPS)
        x = x_ref[:, steps, :].reshape(rows, D_MODEL)
        h = _rms_normalize(x, gain_ref[...]).astype(_BF16)
        gates2 = _twice_sigmoid_of_double(
            jnp.dot(h, wg_ref[:, W_OFF_GATES - GATE_BLOCK:], preferred_element_type=_F32)
            + bg_ref[...])
        for c in range(SUB_STEPS // CHUNK):
            _from_group_major(yg_ref, y_scrs[st], st * (SUB_STEPS // CHUNK) + c, nb, c * CHUNK,
                              SUB_STEPS)
        y = jnp.concatenate([y_scrs[st][v] for v in range(D_SSM // LANES)], axis=1)
        y2 = _twice_gelu(y)
        glu_half = (jnp.dot(y2.astype(_BF16), wglu_ref[...], preferred_element_type=_F32)
                    + bglu_ref[...])
        ssm4 = (y2 * _twice_sigmoid_of_double(glu_half)
                * zs_ref[:, steps, :].reshape(rows, D_SSM).astype(_F32))
        pa = jnp.dot(ao_ref[:, steps, :].reshape(rows, D_ATTN), wba_ref[...],
                     preferred_element_type=_F32)
        ps = jnp.dot(ssm4.astype(_BF16), wbs_ref[...], preferred_element_type=_F32)
        merged2 = gates2[:, :D_MODEL] * pa + gates2[:, D_MODEL:] * ps
        xn = x + jnp.dot(merged2.astype(_BF16), wout_ref[...], preferred_element_type=_F32)
        o_ref[:, steps, :] = _rms_normalize(xn, fgain_ref[...]).reshape(nb, SUB_STEPS, D_MODEL)


def _merge(x, gain, wg, bg, ao, yg, zs, wglu, bglu, wba, wbs, wout, fgain):
    bsz, s, _ = x.shape
    nt = TOKEN_TILE // bsz
    tok = lambda i: (0, i, 0)
    fixed = lambda i: (0, 0)
    full = lambda a: pl.BlockSpec(a.shape, fixed)
    act = lambda n: pl.BlockSpec((bsz, nt, n), tok)
    return pl.pallas_call(
        _merge_kernel,
        grid=(s // nt,),
        in_specs=[act(D_MODEL), full(gain),
                  pl.BlockSpec((D_MODEL, GATE_BLOCK), lambda i: (0, 1)),
                  full(bg), act(D_ATTN),
                  pl.BlockSpec((N_SSM_GROUPS, nt // CHUNK, bsz, CHUNK_W),
                               lambda i: (0, i, 0, 0)),
                  act(D_SSM), full(wglu), full(bglu), full(wba), full(wbs), full(wout),
                  full(fgain)],
        out_specs=act(D_MODEL),
        out_shape=jax.ShapeDtypeStruct((bsz, s, D_MODEL), _F32),
        scratch_shapes=[pltpu.VMEM((D_SSM // LANES, bsz * SUB_STEPS, LANES), _F32)
                        for _ in range(nt // SUB_STEPS)],
        compiler_params=pltpu.CompilerParams(
            dimension_semantics=("arbitrary",), vmem_limit_bytes=VMEM_LIMIT_BYTES),
        name="merge",
    )(x, gain, wg, bg, ao, yg, zs, wglu, bglu, wba, wbs, wout, fgain)


def _layer(x, norm_gain, w_in, b_gate, attn_sink, a_re, a_im, log_dt, b_re, b_im, c_re, c_im,
           d_skip, w_glu, b_glu, w_ba, w_bs, w_out, out_gain, bias):
    bsz, s, d = x.shape
    assert d == D_MODEL and x.dtype == _F32 and w_in.shape == (D_MODEL, W_OFF_GATES + 2 * D_MODEL)
    assert bsz % 8 == 0 and s % BLOCK == 0
    assert INPROJ_TOKEN_TILE % (bsz * SUB_STEPS) == 0 and s % (INPROJ_TOKEN_TILE // bsz) == 0
    assert TOKEN_TILE % (bsz * SUB_STEPS) == 0 and s % (TOKEN_TILE // bsz) == 0
    n_chunks = s // CHUNK
    gain = norm_gain.reshape(1, D_MODEL).astype(_F32)
    w_bf = w_in.astype(_BF16)
    q, k, v, za, ug, zs = _inproj(x, gain, w_bf)

    tmat, smat, cmat, ar, ai = _ssm_matrices(a_re, a_im, log_dt, b_re, b_im, c_re, c_im,
                                             d_skip)
    ao, yg = _attn_ssm(LOG2_E * attn_sink.astype(_F32), q, k, v, bias, za,
                       ug.reshape(N_SSM_GROUPS, n_chunks * bsz, CHUNK_W), tmat, smat, cmat,
                       ar, ai, n_chunks, bsz)
    yg = yg.reshape(N_SSM_GROUPS, n_chunks, bsz, CHUNK_W)

    return _merge(x, 0.5 * gain,
                  w_bf,
                  (0.5 * b_gate).reshape(1, -1).astype(_F32),
                  ao, yg, zs,
                  (0.25 * w_glu).astype(_BF16),
                  (0.5 * b_glu).reshape(1, -1).astype(_F32),
                  w_ba.astype(_BF16),
                  (0.25 * w_bs).astype(_BF16),
                  (0.5 * w_out).astype(_BF16),
                  out_gain.reshape(1, D_MODEL).astype(_F32))


def kernel(x, norm_gain, w_in, b_gate, attn_sink, rel_bias_table, ssm_a_re, ssm_a_im,
           ssm_log_dt, ssm_b_re, ssm_b_im, ssm_c_re, ssm_c_im, ssm_d, w_glu, b_glu,
           w_branch_attn, w_branch_ssm, w_out, final_norm_gain):
    depth = norm_gain.shape[0]
    assert depth == 1, "final norm is fused into the single layer's epilogue"
    bias = _attn_bias(rel_bias_table)
    l = 0
    return _layer(x, norm_gain[l], w_in[l], b_gate[l], attn_sink[l], ssm_a_re[l],
                  ssm_a_im[l], ssm_log_dt[l], ssm_b_re[l], ssm_b_im[l], ssm_c_re[l],
                  ssm_c_im[l], ssm_d[l], w_glu[l], b_glu[l], w_branch_attn[l],
                  w_branch_ssm[l], w_out[l], final_norm_gain, bias)
```

```python
import functools
import math

import jax
import jax.numpy as jnp
import numpy as np
from jax import lax
from jax.experimental import pallas as pl
from jax.experimental.pallas import tpu as pltpu

D_MODEL = 1024
ATTN_HEADS = 8
KV_HEADS = 2
Q_PER_KV = ATTN_HEADS // KV_HEADS
HEAD_DIM = 64
D_ATTN = ATTN_HEADS * HEAD_DIM
D_KV = KV_HEADS * HEAD_DIM
WINDOW = 128
BLOCK = 128
NUM_BUCKETS = 32
MAX_DISTANCE = 128
D_SSM = 512
SSM_GROUP = 16
N_SSM_GROUPS = D_SSM // SSM_GROUP
SSM_STATE = 64
EPS = 1e-6
NEG_INF = -1e30

LANES = 128
SUBLANES = 8
MXU_WIDTH = 256
CHUNK = MXU_WIDTH // SSM_GROUP
CHUNK_W = CHUNK * SSM_GROUP
STATE_W = 4 * SSM_STATE
SLOTS_PER_VREG = LANES // SSM_GROUP
SSM_GROUPS_PER_STEP = 4

HEAD_PAIR = LANES // HEAD_DIM
PAIRS_PER_KV = Q_PER_KV // HEAD_PAIR
assert D_KV == LANES and KV_HEADS == HEAD_PAIR
D_KVX = 2 * D_KV
LOG2_E = math.log2(math.e)
Q_SCALE = HEAD_DIM ** -0.5 * LOG2_E
BIAS_ROW = 4 * BLOCK

W_OFF_ZA = D_ATTN + 2 * D_KV
W_OFF_U = W_OFF_ZA + D_ATTN
W_OFF_GATES = W_OFF_U + 2 * D_SSM
GATE_BLOCK = (W_OFF_GATES + 2 * D_MODEL) // 2
assert GATE_BLOCK % LANES == 0 and GATE_BLOCK <= W_OFF_GATES

TOKEN_TILE = 1024
INPROJ_TOKEN_TILE = 2048
SUB_STEPS = CHUNK
VMEM_LIMIT_BYTES = 56 * 1024 * 1024

_F32 = jnp.float32
_BF16 = jnp.bfloat16
_HI = lax.Precision.HIGHEST


def _silu(x):
    return x * (0.5 * jnp.tanh(0.5 * x) + 0.5)


def _twice_sigmoid_of_double(xh):
    return 1.0 + jnp.tanh(xh)


def _twice_gelu(x):
    c = math.sqrt(2.0 / math.pi)
    return x * (1.0 + jnp.tanh(c * (x + 0.044715 * (x * x * x))))


def _rms_normalize(x, gain):
    ms = jnp.mean(x * x, axis=-1, keepdims=True)
    return x * lax.rsqrt(ms + EPS) * gain


def _slot_masks(rows):
    lane = lax.broadcasted_iota(jnp.int32, (rows, LANES), 1)
    return [(lane >= p * SSM_GROUP) & (lane < (p + 1) * SSM_GROUP)
            for p in range(SLOTS_PER_VREG)]


def _to_group_major(u_scr, ug_ref, kk, nb, t0, steps):
    masks = _slot_masks(nb)
    for v in range(D_SSM // LANES):
        for half in range(CHUNK // SLOTS_PER_VREG):
            rolled = []
            for t8 in range(SLOTS_PER_VREG):
                r = t0 + half * SLOTS_PER_VREG + t8
                piece = u_scr[v, pl.ds(r, nb, stride=steps), :]
                rolled.append(pltpu.roll(piece, t8 * SSM_GROUP, 1) if t8 else piece)
            for p0 in range(SLOTS_PER_VREG):
                acc = rolled[0]
                for t8 in range(1, SLOTS_PER_VREG):
                    acc = jnp.where(masks[(p0 + t8) % SLOTS_PER_VREG], rolled[t8], acc)
                g = v * SLOTS_PER_VREG + p0
                ug_ref[g, kk, :, half * LANES:(half + 1) * LANES] = acc.astype(_BF16)


def _from_group_major(yg_ref, y_scr, kk, nb, t0, steps):
    masks = _slot_masks(nb)
    for v in range(D_SSM // LANES):
        for half in range(CHUNK // SLOTS_PER_VREG):
            src = [yg_ref[v * SLOTS_PER_VREG + p0, kk, :,
                          half * LANES:(half + 1) * LANES].astype(_F32)
                   for p0 in range(SLOTS_PER_VREG)]
            for t8 in range(SLOTS_PER_VREG):
                acc = src[0]
                for p0 in range(1, SLOTS_PER_VREG):
                    acc = jnp.where(masks[(p0 + t8) % SLOTS_PER_VREG], src[p0], acc)
                if t8:
                    acc = pltpu.roll(acc, LANES - t8 * SSM_GROUP, 1)
                r = t0 + half * SLOTS_PER_VREG + t8
                y_scr[v, pl.ds(r, nb, stride=steps), :] = acc


def _inproj_kernel(x_ref, gain_ref, w_ref, q_ref, k_ref, v_ref, za_ref, ug_ref, zs_ref,
                   *u_scrs):
    nb, nt = x_ref.shape[0], x_ref.shape[1]
    rows = nb * SUB_STEPS
    for st in range(nt // SUB_STEPS):
        steps = slice(st * SUB_STEPS, (st + 1) * SUB_STEPS)
        x = x_ref[:, steps, :].reshape(rows, D_MODEL)
        h = _rms_normalize(x, gain_ref[...]).astype(_BF16)

        def put(ref, val):
            ref[:, steps, :] = val.astype(_BF16).reshape(nb, SUB_STEPS, val.shape[-1])

        proj = lambda lo, n: jnp.dot(h, w_ref[:, lo:lo + n], preferred_element_type=_F32)
        u_zs = proj(W_OFF_U, 2 * D_SSM)
        for v in range(D_SSM // LANES):
            u_scrs[st][v] = u_zs[:, v * LANES:(v + 1) * LANES]
        for c in range(SUB_STEPS // CHUNK):
            _to_group_major(u_scrs[st], ug_ref, st * (SUB_STEPS // CHUNK) + c, nb, c * CHUNK,
                            SUB_STEPS)
        put(zs_ref, _silu(u_zs[:, D_SSM:]))
        put(za_ref, _silu(proj(W_OFF_ZA, D_ATTN)))
        qkv = proj(0, W_OFF_ZA)
        put(q_ref, qkv[:, :D_ATTN] * Q_SCALE)
        with_swap = lambda a: jnp.concatenate([a, pltpu.roll(a, HEAD_DIM, 1)], axis=1)
        put(k_ref, with_swap(qkv[:, D_ATTN:D_ATTN + D_KV]))
        put(v_ref, with_swap(qkv[:, D_ATTN + D_KV:]))


def _inproj(x, gain, w_in):
    bsz, s, _ = x.shape
    nt = INPROJ_TOKEN_TILE // bsz
    tok = lambda i: (0, i, 0)
    fixed = lambda i: (0, 0)
    outs = [D_ATTN, D_KVX, D_KVX, D_ATTN, None, D_SSM]
    act = lambda n: (pl.BlockSpec((bsz, nt, n), tok), jax.ShapeDtypeStruct((bsz, s, n), _BF16))
    ug = (pl.BlockSpec((N_SSM_GROUPS, nt // CHUNK, bsz, CHUNK_W), lambda i: (0, i, 0, 0)),
          jax.ShapeDtypeStruct((N_SSM_GROUPS, s // CHUNK, bsz, CHUNK_W), _BF16))
    specs, shapes = zip(*[ug if n is None else act(n) for n in outs])
    return pl.pallas_call(
        _inproj_kernel,
        grid=(s // nt,),
        in_specs=[pl.BlockSpec((bsz, nt, D_MODEL), tok),
                  pl.BlockSpec((1, D_MODEL), fixed)]
                 + [pl.BlockSpec((D_MODEL, W_OFF_GATES), fixed)],
        out_specs=list(specs),
        out_shape=list(shapes),
        scratch_shapes=[pltpu.VMEM((D_SSM // LANES, bsz * SUB_STEPS, LANES), _F32)
                        for _ in range(nt // SUB_STEPS)],
        compiler_params=pltpu.CompilerParams(
            dimension_semantics=("arbitrary",), vmem_limit_bytes=VMEM_LIMIT_BYTES),
        name="inproj",
    )(x, gain, w_in)


def _ssm_kernel(n_chunks, rows_per_chunk, u_ref, t_ref, s_ref, c_ref, ar_ref, ai_ref,
                y_ref, s_scr, x_scr):
    rb = rows_per_chunk
    groups = range(u_ref.shape[0])
    for gi in groups:
        s_scr[gi] = jnp.dot(u_ref[gi], s_ref[gi], preferred_element_type=_F32)
    ar = [jnp.broadcast_to(ar_ref[gi], (rb, 2 * SSM_STATE)) for gi in groups]
    ai = [jnp.broadcast_to(ai_ref[gi], (rb, 2 * SSM_STATE)) for gi in groups]
    is_fwd = lax.broadcasted_iota(jnp.int32, (rb, 2 * SSM_STATE), 1) < SSM_STATE
    is_bwd = jnp.logical_not(is_fwd)
    re_cols = pl.ds(0, 2 * SSM_STATE)
    im_cols = pl.ds(2 * SSM_STATE, 2 * SSM_STATE)

    def step(i, carry):
        rows_f = pl.ds(pl.multiple_of(i * rb, rb), rb)
        rows_b = pl.ds(pl.multiple_of((n_chunks - 1 - i) * rb, rb), rb)
        new = []
        for gi in groups:
            st_re, st_im = carry[2 * gi], carry[2 * gi + 1]
            pltpu.store(x_scr.at[gi, rows_f, re_cols], st_re, mask=is_fwd)
            pltpu.store(x_scr.at[gi, rows_f, im_cols], st_im, mask=is_fwd)
            pltpu.store(x_scr.at[gi, rows_b, re_cols], st_re, mask=is_bwd)
            pltpu.store(x_scr.at[gi, rows_b, im_cols], st_im, mask=is_bwd)
            in_re = jnp.where(is_fwd, s_scr[gi, rows_f, re_cols], s_scr[gi, rows_b, re_cols])
            in_im = jnp.where(is_fwd, s_scr[gi, rows_f, im_cols], s_scr[gi, rows_b, im_cols])
            new.append(ar[gi] * st_re - ai[gi] * st_im + in_re)
            new.append(ar[gi] * st_im + ai[gi] * st_re + in_im)
        return tuple(new)

    zero = jnp.zeros((rb, 2 * SSM_STATE), _F32)
    lax.fori_loop(0, n_chunks, step, (zero,) * (2 * len(groups)), unroll=True)

    for gi in groups:
        y = jnp.dot(u_ref[gi], t_ref[gi], preferred_element_type=_F32)
        y = y + lax.dot_general(x_scr[gi].astype(_BF16), c_ref[gi], (((1,), (1,)), ((), ())),
                                preferred_element_type=_F32)
        y_ref[gi] = y.astype(y_ref.dtype)


def _ssm(ug, tmat, smat, cmat, ar, ai, n_chunks, rows_per_chunk):
    g, rows, _ = ug.shape
    gb = SSM_GROUPS_PER_STEP
    per_group = lambda i: (i, 0, 0)
    return pl.pallas_call(
        functools.partial(_ssm_kernel, n_chunks, rows_per_chunk),
        grid=(g // gb,),
        in_specs=[pl.BlockSpec((gb, rows, CHUNK_W), per_group),
                  pl.BlockSpec((gb, CHUNK_W, CHUNK_W), per_group),
                  pl.BlockSpec((gb, CHUNK_W, STATE_W), per_group),
                  pl.BlockSpec((gb, CHUNK_W, STATE_W), per_group),
                  pl.BlockSpec((gb, 1, 2 * SSM_STATE), per_group),
                  pl.BlockSpec((gb, 1, 2 * SSM_STATE), per_group)],
        out_specs=pl.BlockSpec((gb, rows, CHUNK_W), per_group),
        out_shape=jax.ShapeDtypeStruct((g, rows, CHUNK_W), _BF16),
        scratch_shapes=[pltpu.VMEM((gb, rows, STATE_W), _F32),
                        pltpu.VMEM((gb, rows, STATE_W), _F32)],
        compiler_params=pltpu.CompilerParams(
            dimension_semantics=("arbitrary",), vmem_limit_bytes=VMEM_LIMIT_BYTES),
        name="ssm",
    )(ug, tmat, smat, cmat, ar, ai)


def _ssm_matrices(a_re, a_im, log_dt, b_re, b_im, c_re, c_im, d_skip):
    L, G, P, C = CHUNK, N_SSM_GROUPS, SSM_STATE, SSM_GROUP
    both = lambda a: jnp.concatenate([a[0], a[1]], axis=-1)
    dt = jnp.broadcast_to(jnp.exp(log_dt)[..., None], a_re.shape)
    lam = jnp.stack([both(a_re), both(a_im), both(dt)], axis=1)
    bt = jnp.stack([b_re, b_im], axis=0).transpose(2, 0, 4, 1, 3).reshape(G, 2, C, 2 * P)
    ct = jnp.stack([c_re, c_im], axis=0).transpose(2, 0, 3, 1, 4).reshape(G, 2, C, 2 * P)
    dvec = jnp.tile(d_skip.reshape(G, 1, C), (1, 1, L))
    gb = SLOTS_PER_VREG
    per_group = lambda *blk: pl.BlockSpec((gb,) + blk, lambda i: (i,) + (0,) * len(blk))
    mat = jax.ShapeDtypeStruct((G, L * C, L * C), _BF16)
    vec = jax.ShapeDtypeStruct((G, 1, 2 * P), _F32)
    return pl.pallas_call(
        _ssm_prep_kernel,
        grid=(G // gb,),
        in_specs=[per_group(3, 2 * P), per_group(2, C, 2 * P), per_group(2, C, 2 * P),
                  per_group(1, L * C)],
        out_specs=[per_group(L * C, L * C), per_group(L * C, STATE_W),
                   per_group(L * C, STATE_W), per_group(1, 2 * P), per_group(1, 2 * P)],
        out_shape=[mat, mat, mat, vec, vec],
        compiler_params=pltpu.CompilerParams(dimension_semantics=("arbitrary",)),
        name="ssm_prep",
    )(lam, bt, ct, dvec)


def _ssm_prep_kernel(lam_ref, bt_ref, ct_ref, dvec_ref, t_ref, s_ref, c_ref, ar_ref, ai_ref):
    for gi in range(lam_ref.shape[0]):
        _ssm_prep_group(gi, lam_ref, bt_ref, ct_ref, dvec_ref, t_ref, s_ref, c_ref, ar_ref,
                        ai_ref)


def _ssm_prep_group(gi, lam_ref, bt_ref, ct_ref, dvec_ref, t_ref, s_ref, c_ref, ar_ref,
                    ai_ref):
    L, P, C = CHUNK, SSM_STATE, SSM_GROUP
    g8 = gi % SLOTS_PER_VREG

    def time_of_slot(slot):
        return (jnp.bitwise_and(slot, -SLOTS_PER_VREG)
                + jnp.bitwise_and(slot - g8, SLOTS_PER_VREG - 1))

    def cmul(x_re, x_im, y_re, y_im):
        return x_re * y_re - x_im * y_im, x_re * y_im + x_im * y_re

    a_re, a_im, dt = lam_ref[gi, 0:1], lam_ref[gi, 1:2], lam_ref[gi, 2:3]
    lr, li = a_re * dt, a_im * dt

    j = time_of_slot(lax.broadcasted_iota(jnp.int32, (L, 2 * P), 0)).astype(_F32)
    pos = lax.broadcasted_iota(jnp.int32, (L, 2 * P), 0).astype(_F32)
    fwd = lax.broadcasted_iota(jnp.int32, (L, 2 * P), 1) < P
    cj, sj = jnp.cos(li * j), jnp.sin(li * j)
    slot_of = lambda t: (t // SLOTS_PER_VREG) * SLOTS_PER_VREG + (t + g8) % SLOTS_PER_VREG
    in_time_order = lambda a: jnp.concatenate(
        [a[slot_of(t):slot_of(t) + 1] for t in range(L)], axis=0)
    cp, sp = in_time_order(cj), in_time_order(sj)
    krow = lax.broadcasted_iota(jnp.int32, (SUBLANES, 2 * P), 0)
    consts = jnp.where(krow == 0, 1.0, jnp.where(krow == 1, L - 1.0,
                                                 jnp.where(krow == 2, float(L), 0.0)))
    ck, sk = jnp.cos(li * consts), jnp.sin(li * consts)
    (c_1, s_1), (c_m, s_m), (c_l, s_l) = [(ck[r:r + 1], sk[r:r + 1]) for r in range(3)]

    def cpow(tau, trig):
        mag = jnp.exp(lr * tau)
        return mag * trig[0], mag * trig[1]

    minus = lambda c0, s0, c, s: (c0 * c + s0 * s, s0 * c - c0 * s)
    plus = lambda c0, s0, c, s: (c0 * c - s0 * s, s0 * c + c0 * s)
    per_dir = lambda f, b: (jnp.where(fwd, f[0], b[0]), jnp.where(fwd, f[1], b[1]))

    ab_re, ab_im = cpow(jnp.ones_like(lr), (c_1, s_1))
    den = a_re * a_re + a_im * a_im
    co_re = ((ab_re - 1.0) * a_re + ab_im * a_im) / den
    co_im = (ab_im * a_re - (ab_re - 1.0) * a_im) / den
    bb_re, bb_im = cmul(bt_ref[gi, 0], bt_ref[gi, 1], co_re, co_im)
    cc_re, cc_im = ct_ref[gi, 0], ct_ref[gi, 1]

    def outer(p, m):
        (p_re, p_im), (m_re, m_im) = p, m
        blocks = [cmul(p_re[s:s + 1], p_im[s:s + 1], m_re, m_im) for s in range(L)]
        return (jnp.concatenate([b[0] for b in blocks], axis=0),
                jnp.concatenate([b[1] for b in blocks], axis=0))

    s_re, s_im = outer(cpow(jnp.where(fwd, L - 1 - j, j),
                            per_dir(minus(c_m, s_m, cj, sj), (cj, sj))), (bb_re, bb_im))
    s_ref[gi] = jnp.concatenate([s_re, s_im], axis=1).astype(_BF16)
    w_re, w_im = outer(cpow(jnp.where(fwd, j + 1, L - j),
                            per_dir(plus(c_1, s_1, cj, sj), minus(c_l, s_l, cj, sj))),
                       (cc_re, cc_im))
    wcat = jnp.concatenate([w_re, -w_im], axis=1)
    c_ref[gi] = wcat.astype(_BF16)
    a_re2, a_im2 = outer(cpow(jnp.where(fwd, pos, L - 1 - pos),
                              per_dir((cp, sp), minus(c_m, s_m, cp, sp))), (cc_re, cc_im))
    acat = jnp.concatenate([a_re2, -a_im2], axis=1)
    bcat = jnp.concatenate([bb_re, bb_im], axis=1)
    is_fwd_state = jnp.bitwise_and(lax.broadcasted_iota(jnp.int32, bcat.shape, 1), P) == 0

    def split(a):
        hi = a.astype(_BF16).astype(_F32)
        return hi, a - hi

    def dot_state(a_parts, b_parts):
        (a_hi, a_lo), (b_hi, b_lo) = [[p.astype(_BF16) for p in parts]
                                      for parts in (a_parts, b_parts)]
        d = lambda a, b: lax.dot_general(a, b, (((1,), (1,)), ((), ())),
                                         preferred_element_type=_F32)
        return d(a_hi, b_hi) + d(a_hi, b_lo) + d(a_lo, b_hi)

    b_parts, a_parts = split(bcat), split(acat)
    kf = dot_state([jnp.where(is_fwd_state, p, 0.0) for p in b_parts], a_parts)
    kb = dot_state([jnp.where(is_fwd_state, 0.0, p) for p in b_parts], a_parts)

    lane = lax.broadcasted_iota(jnp.int32, (C, LANES), 1)

    def shift_slots(k, n):
        halves = [k[:, :LANES], k[:, LANES:]]
        if n >= SLOTS_PER_VREG:
            halves, n = halves[::-1], n - SLOTS_PER_VREG
        if n == 0:
            return halves
        r = [pltpu.roll(h, n * C, 1) for h in halves]
        wrapped = lane < n * C
        return [jnp.where(wrapped, r[1], r[0]), jnp.where(wrapped, r[0], r[1])]

    col_t = lax.broadcasted_iota(jnp.int32, (C, L * C), 1)
    blocks = []
    for jt in range(L):
        fwd_part = jnp.concatenate(shift_slots(kf, jt), axis=1)
        bwd_part = jnp.concatenate(shift_slots(kb, (jt + 1) % L), axis=1)
        blk = (jnp.where(col_t >= jt * C, fwd_part, 0.0)
               + jnp.where(col_t < (jt + 1) * C, bwd_part, 0.0))
        blocks.append(jnp.concatenate(
            [pltpu.roll(blk[:, h * LANES:(h + 1) * LANES], g8 * C, 1) if g8 else
             blk[:, h * LANES:(h + 1) * LANES] for h in range(L * C // LANES)], axis=1))
    time_of = lambda s: (s // SLOTS_PER_VREG) * SLOTS_PER_VREG + (s - g8) % SLOTS_PER_VREG
    tm = jnp.concatenate([blocks[time_of(s)] for s in range(L)], axis=0)
    row = lax.broadcasted_iota(jnp.int32, (L * C, L * C), 0)
    col = lax.broadcasted_iota(jnp.int32, (L * C, L * C), 1)
    t_ref[gi] = (tm + jnp.where(row == col, dvec_ref[gi], 0.0)).astype(_BF16)
    al_re, al_im = cpow(jnp.full_like(lr, float(L)), (c_l, s_l))
    ar_ref[gi] = al_re
    ai_ref[gi] = al_im


def _attn_kernel(sink_ref, q_ref, k_ref, v_ref, bias_rows_ref, za_ref, o_ref, bias_ref):
    nq = q_ref.shape[1] // BLOCK

    @pl.when(pl.program_id(0) == 0)
    def _():
        qi = lax.broadcasted_iota(jnp.int32, (BLOCK, 3 * BLOCK), 0)
        ki = lax.broadcasted_iota(jnp.int32, (BLOCK, 3 * BLOCK), 1)
        in_band = jnp.abs(ki - BLOCK - qi) <= WINDOW
        for h in range(ATTN_HEADS):
            row = jnp.broadcast_to(bias_rows_ref[h:h + 1, :], (BLOCK, BIAS_ROW))
            rel = pltpu.roll(row, 0, 1, stride=1, stride_axis=0)
            bias_ref[h] = jnp.where(in_band, rel[:, :3 * BLOCK], NEG_INF)

    rows = lambda ref, c: ref[0, c * BLOCK:(c + 1) * BLOCK]
    k_blocks = [rows(k_ref, c) for c in range(nq)]
    v_blocks = [rows(v_ref, c) for c in range(nq)]
    lane = lax.broadcasted_iota(jnp.int32, (BLOCK, LANES), 1)
    halves = [lane < HEAD_DIM, lane >= HEAD_DIM]
    keep = [h.astype(_F32).astype(_BF16) for h in halves]
    slab = lambda blk, j, e: blk[:, (j ^ e) * LANES:((j ^ e) + 1) * LANES]
    k_half = {(j, e): [slab(kb, j, e) * keep[e] for kb in k_blocks]
              for j in range(KV_HEADS) for e in range(HEAD_PAIR)}
    v_half = {(j, e): [jnp.concatenate([slab(vb, j, e) * keep[e], keep[e]], axis=1)
                       for vb in v_blocks]
              for j in range(KV_HEADS) for e in range(HEAD_PAIR)}
    for c in range(nq):
        blocks = [b for b in (c - 1, c, c + 1) if 0 <= b < nq]
        slot0 = blocks[0] - (c - 1)
        nk = len(blocks) * BLOCK
        q = rows(q_ref, c)
        slabs_out = []
        for j in range(KV_HEADS):
            window = lambda halves_of: jnp.concatenate(
                [halves_of[j, e][b] for e in range(HEAD_PAIR) for b in blocks], axis=0)
            kcat = window(k_half)
            vcat = window(v_half)
            qs = jnp.concatenate(
                [q[:, (j * PAIRS_PER_KV + i) * LANES:(j * PAIRS_PER_KV + i + 1) * LANES]
                 for i in range(PAIRS_PER_KV)], axis=0)
            s = lax.dot_general(qs, kcat, (((1,), (1,)), ((), ())),
                                preferred_element_type=_F32)
            p_rows, e_rows = [], []
            for i in range(PAIRS_PER_KV):
                p_lanes, e_sink = [], []
                for e in range(HEAD_PAIR):
                    head = j * Q_PER_KV + i * HEAD_PAIR + e
                    sg = (s[i * BLOCK:(i + 1) * BLOCK, e * nk:(e + 1) * nk]
                          + bias_ref[head, :, slot0 * BLOCK:slot0 * BLOCK + nk])
                    sk = sink_ref[head]
                    m = jnp.maximum(jnp.max(sg, axis=-1, keepdims=True), sk)
                    p_lanes.append(jnp.exp2(sg - m).astype(_BF16))
                    e_sink.append(jnp.broadcast_to(jnp.exp2(sk - m), (BLOCK, LANES)))
                p_rows.append(jnp.concatenate(p_lanes, axis=1))
                e_rows.append(jnp.where(halves[0], e_sink[0], e_sink[1]))
            p = jnp.concatenate(p_rows, axis=0)
            o = jnp.dot(p, vcat, preferred_element_type=_F32)
            den = o[:, LANES:] + jnp.concatenate(e_rows, axis=0)
            on = o[:, :LANES] / den
            slabs_out += [on[i * BLOCK:(i + 1) * BLOCK] for i in range(PAIRS_PER_KV)]
        o_all = jnp.concatenate(slabs_out, axis=1)
        o_ref[0, c * BLOCK:(c + 1) * BLOCK, :] = (
            o_all * rows(za_ref, c).astype(_F32)).astype(o_ref.dtype)


def _attention(sink, q, k, v, bias_rows, za):
    b, s, _ = q.shape
    seq = lambda w: pl.BlockSpec((1, s, w), lambda i: (i, 0, 0))
    return pl.pallas_call(
        _attn_kernel,
        grid=(b,),
        in_specs=[pl.BlockSpec(memory_space=pltpu.SMEM), seq(D_ATTN), seq(D_KVX), seq(D_KVX),
                  pl.BlockSpec(bias_rows.shape, lambda i: (0, 0)), seq(D_ATTN)],
        out_specs=seq(D_ATTN),
        out_shape=jax.ShapeDtypeStruct((b, s, D_ATTN), _BF16),
        scratch_shapes=[pltpu.VMEM((ATTN_HEADS, BLOCK, 3 * BLOCK), _F32)],
        compiler_params=pltpu.CompilerParams(
            dimension_semantics=("arbitrary",), vmem_limit_bytes=VMEM_LIMIT_BYTES),
        name="attn",
    )(sink, q, k, v, bias_rows, za)


def _t5_bucket_np(rel):
    half = NUM_BUCKETS // 2
    ret = (rel > 0).astype(np.int64) * half
    n = np.abs(rel)
    max_exact = half // 2
    nf = np.maximum(n, 1).astype(np.float64)
    large = max_exact + (np.log(nf / max_exact) / math.log(MAX_DISTANCE / max_exact)
                         * (half - max_exact)).astype(np.int64)
    large = np.minimum(large, half - 1)
    return ret + np.where(n < max_exact, n, large)


def _attn_bias(rel_table):
    rel = (np.arange(BIAS_ROW) + BLOCK) % BIAS_ROW - 2 * BLOCK
    onehot = (_t5_bucket_np(rel)[None] == np.arange(NUM_BUCKETS)[:, None])
    return jnp.einsum('bh,br->hr', LOG2_E * rel_table.astype(_F32),
                      jnp.asarray(onehot, _F32), precision=_HI)


def _merge_kernel(x_ref, gain_ref, wg_ref, bg_ref, ao_ref, yg_ref, zs_ref, wglu_f32_ref,
                  bglu_ref, wba_f32_ref, wbs_f32_ref, wout_f32_ref, fgain_ref, o_ref,
                  wglu_ref, wba_ref, wbs_ref, wout_ref, *y_scrs):
    nb, nt = x_ref.shape[0], x_ref.shape[1]
    rows = nb * SUB_STEPS

    @pl.when(pl.program_id(0) == 0)
    def _():
        for dst, src, scale in ((wglu_ref, wglu_f32_ref, 0.25), (wba_ref, wba_f32_ref, 1.0),
                                (wbs_ref, wbs_f32_ref, 0.25), (wout_ref, wout_f32_ref, 0.5)):
            dst[...] = (scale * src[...]).astype(_BF16)

    for st in range(nt // SUB_STEPS):
        steps = slice(st * SUB_STEPS, (st + 1) * SUB_STEPS)
        x = x_ref[:, steps, :].reshape(rows, D_MODEL)
        h = _rms_normalize(x, gain_ref[...]).astype(_BF16)
        gates2 = _twice_sigmoid_of_double(
            jnp.dot(h, wg_ref[:, W_OFF_GATES - GATE_BLOCK:], preferred_element_type=_F32)
            + bg_ref[...])
        for c in range(SUB_STEPS // CHUNK):
            _from_group_major(yg_ref, y_scrs[st], st * (SUB_STEPS // CHUNK) + c, nb, c * CHUNK,
                              SUB_STEPS)
        y = jnp.concatenate([y_scrs[st][v] for v in range(D_SSM // LANES)], axis=1)
        y2 = _twice_gelu(y)
        glu_half = (jnp.dot(y2.astype(_BF16), wglu_ref[...], preferred_element_type=_F32)
                    + bglu_ref[...])
        ssm4 = (y2 * _twice_sigmoid_of_double(glu_half)
                * zs_ref[:, steps, :].reshape(rows, D_SSM).astype(_F32))
        pa = jnp.dot(ao_ref[:, steps, :].reshape(rows, D_ATTN), wba_ref[...],
                     preferred_element_type=_F32)
        ps = jnp.dot(ssm4.astype(_BF16), wbs_ref[...], preferred_element_type=_F32)
        merged2 = gates2[:, :D_MODEL] * pa + gates2[:, D_MODEL:] * ps
        xn = x + jnp.dot(merged2.astype(_BF16), wout_ref[...], preferred_element_type=_F32)
        o_ref[:, steps, :] = _rms_normalize(xn, fgain_ref[...]).reshape(nb, SUB_STEPS, D_MODEL)


def _merge(x, gain, wg, bg, ao, yg, zs, wglu, bglu, wba, wbs, wout, fgain):
    bsz, s, _ = x.shape
    nt = TOKEN_TILE // bsz
    tok = lambda i: (0, i, 0)
    fixed = lambda i: (0, 0)
    full = lambda a: pl.BlockSpec(a.shape, fixed)
    once = lambda a: pl.BlockSpec(a.shape, fixed, pipeline_mode=pl.Buffered(1))
    act = lambda n: pl.BlockSpec((bsz, nt, n), tok)
    return pl.pallas_call(
        _merge_kernel,
        grid=(s // nt,),
        in_specs=[act(D_MODEL), full(gain),
                  pl.BlockSpec((D_MODEL, GATE_BLOCK), lambda i: (0, 1)),
                  full(bg), act(D_ATTN),
                  pl.BlockSpec((N_SSM_GROUPS, nt // CHUNK, bsz, CHUNK_W),
                               lambda i: (0, i, 0, 0)),
                  act(D_SSM), once(wglu), full(bglu), once(wba), once(wbs), once(wout),
                  full(fgain)],
        out_specs=act(D_MODEL),
        out_shape=jax.ShapeDtypeStruct((bsz, s, D_MODEL), _F32),
        scratch_shapes=[pltpu.VMEM(w.shape, _BF16) for w in (wglu, wba, wbs, wout)]
                       + [pltpu.VMEM((D_SSM // LANES, bsz * SUB_STEPS, LANES), _F32)
                          for _ in range(nt // SUB_STEPS)],
        compiler_params=pltpu.CompilerParams(
            dimension_semantics=("arbitrary",), vmem_limit_bytes=VMEM_LIMIT_BYTES),
        name="merge",
    )(x, gain, wg, bg, ao, yg, zs, wglu, bglu, wba, wbs, wout, fgain)


def _layer(x, norm_gain, w_in, b_gate, attn_sink, a_re, a_im, log_dt, b_re, b_im, c_re, c_im,
           d_skip, w_glu, b_glu, w_ba, w_bs, w_out, out_gain, bias):
    bsz, s, d = x.shape
    assert d == D_MODEL and x.dtype == _F32 and w_in.shape == (D_MODEL, W_OFF_GATES + 2 * D_MODEL)
    assert bsz % 8 == 0 and s % BLOCK == 0
    assert INPROJ_TOKEN_TILE % (bsz * SUB_STEPS) == 0 and s % (INPROJ_TOKEN_TILE // bsz) == 0
    assert TOKEN_TILE % (bsz * SUB_STEPS) == 0 and s % (TOKEN_TILE // bsz) == 0
    n_chunks = s // CHUNK
    gain = norm_gain.reshape(1, D_MODEL).astype(_F32)
    w_bf = w_in.astype(_BF16)
    q, k, v, za, ug, zs = _inproj(x, gain, w_bf)

    tmat, smat, cmat, ar, ai = _ssm_matrices(a_re, a_im, log_dt, b_re, b_im, c_re, c_im,
                                             d_skip)
    yg = _ssm(ug.reshape(N_SSM_GROUPS, n_chunks * bsz, CHUNK_W), tmat, smat, cmat, ar, ai,
              n_chunks, bsz)
    yg = yg.reshape(N_SSM_GROUPS, n_chunks, bsz, CHUNK_W)

    ao = _attention(LOG2_E * attn_sink.astype(_F32), q, k, v, bias, za)

    return _merge(x, 0.5 * gain,
                  w_bf,
                  (0.5 * b_gate).reshape(1, -1).astype(_F32),
                  ao, yg, zs,
                  w_glu,
                  (0.5 * b_glu).reshape(1, -1).astype(_F32),
                  w_ba,
                  w_bs,
                  w_out,
                  out_gain.reshape(1, D_MODEL).astype(_F32))


def kernel(x, norm_gain, w_in, b_gate, attn_sink, rel_bias_table, ssm_a_re, ssm_a_im,
           ssm_log_dt, ssm_b_re, ssm_b_im, ssm_c_re, ssm_c_im, ssm_d, w_glu, b_glu,
           w_branch_attn, w_branch_ssm, w_out, final_norm_gain):
    depth = norm_gain.shape[0]
    assert depth == 1, "final norm is fused into the single layer's epilogue"
    bias = _attn_bias(rel_bias_table)
    l = 0
    return _layer(x, norm_gain[l], w_in[l], b_gate[l], attn_sink[l], ssm_a_re[l],
                  ssm_a_im[l], ssm_log_dt[l], ssm_b_re[l], ssm_b_im[l], ssm_c_re[l],
                  ssm_c_im[l], ssm_d[l], w_glu[l], b_glu[l], w_branch_attn[l],
                  w_branch_ssm[l], w_out[l], final_norm_gain, bias)
```

```python
import functools
import math

import jax
import jax.numpy as jnp
import numpy as np
from jax import lax
from jax.experimental import pallas as pl
from jax.experimental.pallas import tpu as pltpu

D_MODEL = 1024
ATTN_HEADS = 8
KV_HEADS = 2
Q_PER_KV = ATTN_HEADS // KV_HEADS
HEAD_DIM = 64
D_ATTN = ATTN_HEADS * HEAD_DIM
D_KV = KV_HEADS * HEAD_DIM
WINDOW = 128
BLOCK = 128
NUM_BUCKETS = 32
MAX_DISTANCE = 128
D_SSM = 512
SSM_GROUP = 16
N_SSM_GROUPS = D_SSM // SSM_GROUP
SSM_STATE = 64
EPS = 1e-6
NEG_INF = -1e30

LANES = 128
SUBLANES = 8
MXU_WIDTH = 256
CHUNK = MXU_WIDTH // SSM_GROUP
CHUNK_W = CHUNK * SSM_GROUP
STATE_W = 4 * SSM_STATE
SLOTS_PER_VREG = LANES // SSM_GROUP
SSM_GROUPS_PER_STEP = 4

HEAD_PAIR = LANES // HEAD_DIM
PAIRS_PER_KV = Q_PER_KV // HEAD_PAIR
assert D_KV == LANES and KV_HEADS == HEAD_PAIR
D_KVX = 2 * D_KV
LOG2_E = math.log2(math.e)
Q_SCALE = HEAD_DIM ** -0.5 * LOG2_E
BIAS_ROW = 4 * BLOCK

W_OFF_ZA = D_ATTN + 2 * D_KV
W_OFF_U = W_OFF_ZA + D_ATTN
W_OFF_GATES = W_OFF_U + 2 * D_SSM
GATE_BLOCK = (W_OFF_GATES + 2 * D_MODEL) // 2
assert GATE_BLOCK % LANES == 0 and GATE_BLOCK <= W_OFF_GATES

TOKEN_TILE = 1024
INPROJ_TOKEN_TILE = 2048
SUB_STEPS = CHUNK
VMEM_LIMIT_BYTES = 56 * 1024 * 1024

_F32 = jnp.float32
_BF16 = jnp.bfloat16
_HI = lax.Precision.HIGHEST


def _silu(x):
    return x * (0.5 * jnp.tanh(0.5 * x) + 0.5)


def _twice_sigmoid_of_double(xh):
    return 1.0 + jnp.tanh(xh)


def _twice_gelu(x):
    c = math.sqrt(2.0 / math.pi)
    return x * (1.0 + jnp.tanh(c * (x + 0.044715 * (x * x * x))))


def _rms_normalize(x, gain):
    ms = jnp.mean(x * x, axis=-1, keepdims=True)
    return x * lax.rsqrt(ms + EPS) * gain


def _slot_masks(rows):
    lane = lax.broadcasted_iota(jnp.int32, (rows, LANES), 1)
    return [(lane >= p * SSM_GROUP) & (lane < (p + 1) * SSM_GROUP)
            for p in range(SLOTS_PER_VREG)]


def _to_group_major(u_scr, ug_ref, kk, nb, t0, steps):
    masks = _slot_masks(nb)
    for v in range(D_SSM // LANES):
        for half in range(CHUNK // SLOTS_PER_VREG):
            rolled = []
            for t8 in range(SLOTS_PER_VREG):
                r = t0 + half * SLOTS_PER_VREG + t8
                piece = u_scr[v, pl.ds(r, nb, stride=steps), :]
                rolled.append(pltpu.roll(piece, t8 * SSM_GROUP, 1) if t8 else piece)
            for p0 in range(SLOTS_PER_VREG):
                acc = rolled[0]
                for t8 in range(1, SLOTS_PER_VREG):
                    acc = jnp.where(masks[(p0 + t8) % SLOTS_PER_VREG], rolled[t8], acc)
                g = v * SLOTS_PER_VREG + p0
                ug_ref[g, kk, :, half * LANES:(half + 1) * LANES] = acc.astype(_BF16)


def _from_group_major(yg_ref, y_scr, kk, nb, t0, steps):
    masks = _slot_masks(nb)
    for v in range(D_SSM // LANES):
        for half in range(CHUNK // SLOTS_PER_VREG):
            src = [yg_ref[v * SLOTS_PER_VREG + p0, kk, :,
                          half * LANES:(half + 1) * LANES].astype(_F32)
                   for p0 in range(SLOTS_PER_VREG)]
            for t8 in range(SLOTS_PER_VREG):
                acc = src[0]
                for p0 in range(1, SLOTS_PER_VREG):
                    acc = jnp.where(masks[(p0 + t8) % SLOTS_PER_VREG], src[p0], acc)
                if t8:
                    acc = pltpu.roll(acc, LANES - t8 * SSM_GROUP, 1)
                r = t0 + half * SLOTS_PER_VREG + t8
                y_scr[v, pl.ds(r, nb, stride=steps), :] = acc


def _inproj_kernel(x_ref, gain_ref, w_ref, q_ref, k_ref, v_ref, za_ref, ug_ref, zs_ref,
                   *u_scrs):
    nb, nt = x_ref.shape[0], x_ref.shape[1]
    rows = nb * SUB_STEPS
    for st in range(nt // SUB_STEPS):
        steps = slice(st * SUB_STEPS, (st + 1) * SUB_STEPS)
        x = x_ref[:, steps, :].reshape(rows, D_MODEL)
        h = _rms_normalize(x, gain_ref[...]).astype(_BF16)

        def put(ref, val):
            ref[:, steps, :] = val.astype(_BF16).reshape(nb, SUB_STEPS, val.shape[-1])

        proj = lambda lo, n: jnp.dot(h, w_ref[:, lo:lo + n], preferred_element_type=_F32)
        u_zs = proj(W_OFF_U, 2 * D_SSM)
        for v in range(D_SSM // LANES):
            u_scrs[st][v] = u_zs[:, v * LANES:(v + 1) * LANES]
        for c in range(SUB_STEPS // CHUNK):
            _to_group_major(u_scrs[st], ug_ref, st * (SUB_STEPS // CHUNK) + c, nb, c * CHUNK,
                            SUB_STEPS)
        put(zs_ref, _silu(u_zs[:, D_SSM:]))
        put(za_ref, _silu(proj(W_OFF_ZA, D_ATTN)))
        qkv = proj(0, W_OFF_ZA)
        put(q_ref, qkv[:, :D_ATTN] * Q_SCALE)
        with_swap = lambda a: jnp.concatenate([a, pltpu.roll(a, HEAD_DIM, 1)], axis=1)
        put(k_ref, with_swap(qkv[:, D_ATTN:D_ATTN + D_KV]))
        put(v_ref, with_swap(qkv[:, D_ATTN + D_KV:]))


def _inproj(x, gain, w_in):
    bsz, s, _ = x.shape
    nt = INPROJ_TOKEN_TILE // bsz
    tok = lambda i: (0, i, 0)
    fixed = lambda i: (0, 0)
    outs = [D_ATTN, D_KVX, D_KVX, D_ATTN, None, D_SSM]
    act = lambda n: (pl.BlockSpec((bsz, nt, n), tok), jax.ShapeDtypeStruct((bsz, s, n), _BF16))
    ug = (pl.BlockSpec((N_SSM_GROUPS, nt // CHUNK, bsz, CHUNK_W), lambda i: (0, i, 0, 0)),
          jax.ShapeDtypeStruct((N_SSM_GROUPS, s // CHUNK, bsz, CHUNK_W), _BF16))
    specs, shapes = zip(*[ug if n is None else act(n) for n in outs])
    return pl.pallas_call(
        _inproj_kernel,
        grid=(s // nt,),
        in_specs=[pl.BlockSpec((bsz, nt, D_MODEL), tok),
                  pl.BlockSpec((1, D_MODEL), fixed)]
                 + [pl.BlockSpec((D_MODEL, W_OFF_GATES), fixed)],
        out_specs=list(specs),
        out_shape=list(shapes),
        scratch_shapes=[pltpu.VMEM((D_SSM // LANES, bsz * SUB_STEPS, LANES), _F32)
                        for _ in range(nt // SUB_STEPS)],
        compiler_params=pltpu.CompilerParams(
            dimension_semantics=("arbitrary",), vmem_limit_bytes=VMEM_LIMIT_BYTES),
        name="inproj",
    )(x, gain, w_in)


def _ssm_kernel(n_chunks, rows_per_chunk, u_ref, t_ref, s_ref, c_ref, ar_ref, ai_ref,
                y_ref, s_scr, x_scr):
    rb = rows_per_chunk
    groups = range(u_ref.shape[0])
    for gi in groups:
        s_scr[gi] = jnp.dot(u_ref[gi], s_ref[gi], preferred_element_type=_F32)
    ar = [jnp.broadcast_to(ar_ref[gi], (rb, 2 * SSM_STATE)) for gi in groups]
    ai = [jnp.broadcast_to(ai_ref[gi], (rb, 2 * SSM_STATE)) for gi in groups]
    is_fwd = lax.broadcasted_iota(jnp.int32, (rb, 2 * SSM_STATE), 1) < SSM_STATE
    is_bwd = jnp.logical_not(is_fwd)
    re_cols = pl.ds(0, 2 * SSM_STATE)
    im_cols = pl.ds(2 * SSM_STATE, 2 * SSM_STATE)

    def step(i, carry):
        rows_f = pl.ds(pl.multiple_of(i * rb, rb), rb)
        rows_b = pl.ds(pl.multiple_of((n_chunks - 1 - i) * rb, rb), rb)
        new = []
        for gi in groups:
            st_re, st_im = carry[2 * gi], carry[2 * gi + 1]
            pltpu.store(x_scr.at[gi, rows_f, re_cols], st_re, mask=is_fwd)
            pltpu.store(x_scr.at[gi, rows_f, im_cols], st_im, mask=is_fwd)
            pltpu.store(x_scr.at[gi, rows_b, re_cols], st_re, mask=is_bwd)
            pltpu.store(x_scr.at[gi, rows_b, im_cols], st_im, mask=is_bwd)
            in_re = jnp.where(is_fwd, s_scr[gi, rows_f, re_cols], s_scr[gi, rows_b, re_cols])
            in_im = jnp.where(is_fwd, s_scr[gi, rows_f, im_cols], s_scr[gi, rows_b, im_cols])
            new.append(ar[gi] * st_re - ai[gi] * st_im + in_re)
            new.append(ar[gi] * st_im + ai[gi] * st_re + in_im)
        return tuple(new)

    zero = jnp.zeros((rb, 2 * SSM_STATE), _F32)
    lax.fori_loop(0, n_chunks, step, (zero,) * (2 * len(groups)), unroll=True)

    for gi in groups:
        y = jnp.dot(u_ref[gi], t_ref[gi], preferred_element_type=_F32)
        y = y + lax.dot_general(x_scr[gi].astype(_BF16), c_ref[gi], (((1,), (1,)), ((), ())),
                                preferred_element_type=_F32)
        y_ref[gi] = y.astype(y_ref.dtype)


def _ssm(ug, tmat, smat, cmat, ar, ai, n_chunks, rows_per_chunk):
    g, rows, _ = ug.shape
    gb = SSM_GROUPS_PER_STEP
    per_group = lambda i: (i, 0, 0)
    return pl.pallas_call(
        functools.partial(_ssm_kernel, n_chunks, rows_per_chunk),
        grid=(g // gb,),
        in_specs=[pl.BlockSpec((gb, rows, CHUNK_W), per_group),
                  pl.BlockSpec((gb, CHUNK_W, CHUNK_W), per_group),
                  pl.BlockSpec((gb, CHUNK_W, STATE_W), per_group),
                  pl.BlockSpec((gb, CHUNK_W, STATE_W), per_group),
                  pl.BlockSpec((gb, 1, 2 * SSM_STATE), per_group),
                  pl.BlockSpec((gb, 1, 2 * SSM_STATE), per_group)],
        out_specs=pl.BlockSpec((gb, rows, CHUNK_W), per_group),
        out_shape=jax.ShapeDtypeStruct((g, rows, CHUNK_W), _BF16),
        scratch_shapes=[pltpu.VMEM((gb, rows, STATE_W), _F32),
                        pltpu.VMEM((gb, rows, STATE_W), _F32)],
        compiler_params=pltpu.CompilerParams(
            dimension_semantics=("arbitrary",), vmem_limit_bytes=VMEM_LIMIT_BYTES),
        name="ssm",
    )(ug, tmat, smat, cmat, ar, ai)


def _ssm_matrices(a_re, a_im, log_dt, b_re, b_im, c_re, c_im, d_skip):
    L, G, P, C = CHUNK, N_SSM_GROUPS, SSM_STATE, SSM_GROUP
    both = lambda a: jnp.concatenate([a[0], a[1]], axis=-1)
    dt = jnp.broadcast_to(jnp.exp(log_dt)[..., None], a_re.shape)
    lam = jnp.stack([both(a_re), both(a_im), both(dt)], axis=1)
    bt = jnp.stack([b_re, b_im], axis=0).transpose(2, 0, 4, 1, 3).reshape(G, 2, C, 2 * P)
    ct = jnp.stack([c_re, c_im], axis=0).transpose(2, 0, 3, 1, 4).reshape(G, 2, C, 2 * P)
    dvec = jnp.tile(d_skip.reshape(G, 1, C), (1, 1, L))
    gb = SLOTS_PER_VREG
    per_group = lambda *blk: pl.BlockSpec((gb,) + blk, lambda i: (i,) + (0,) * len(blk))
    mat = jax.ShapeDtypeStruct((G, L * C, L * C), _BF16)
    vec = jax.ShapeDtypeStruct((G, 1, 2 * P), _F32)
    return pl.pallas_call(
        _ssm_prep_kernel,
        grid=(G // gb,),
        in_specs=[per_group(3, 2 * P), per_group(2, C, 2 * P), per_group(2, C, 2 * P),
                  per_group(1, L * C)],
        out_specs=[per_group(L * C, L * C), per_group(L * C, STATE_W),
                   per_group(L * C, STATE_W), per_group(1, 2 * P), per_group(1, 2 * P)],
        out_shape=[mat, mat, mat, vec, vec],
        compiler_params=pltpu.CompilerParams(dimension_semantics=("arbitrary",)),
        name="ssm_prep",
    )(lam, bt, ct, dvec)


def _ssm_prep_kernel(lam_ref, bt_ref, ct_ref, dvec_ref, t_ref, s_ref, c_ref, ar_ref, ai_ref):
    for gi in range(lam_ref.shape[0]):
        _ssm_prep_group(gi, lam_ref, bt_ref, ct_ref, dvec_ref, t_ref, s_ref, c_ref, ar_ref,
                        ai_ref)


def _ssm_prep_group(gi, lam_ref, bt_ref, ct_ref, dvec_ref, t_ref, s_ref, c_ref, ar_ref,
                    ai_ref):
    L, P, C = CHUNK, SSM_STATE, SSM_GROUP
    g8 = gi % SLOTS_PER_VREG

    def time_of_slot(slot):
        return (jnp.bitwise_and(slot, -SLOTS_PER_VREG)
                + jnp.bitwise_and(slot - g8, SLOTS_PER_VREG - 1))

    def cmul(x_re, x_im, y_re, y_im):
        return x_re * y_re - x_im * y_im, x_re * y_im + x_im * y_re

    a_re, a_im, dt = lam_ref[gi, 0:1], lam_ref[gi, 1:2], lam_ref[gi, 2:3]
    lr, li = a_re * dt, a_im * dt

    j = time_of_slot(lax.broadcasted_iota(jnp.int32, (L, 2 * P), 0)).astype(_F32)
    pos = lax.broadcasted_iota(jnp.int32, (L, 2 * P), 0).astype(_F32)
    fwd = lax.broadcasted_iota(jnp.int32, (L, 2 * P), 1) < P
    cj, sj = jnp.cos(li * j), jnp.sin(li * j)
    slot_of = lambda t: (t // SLOTS_PER_VREG) * SLOTS_PER_VREG + (t + g8) % SLOTS_PER_VREG
    in_time_order = lambda a: jnp.concatenate(
        [a[slot_of(t):slot_of(t) + 1] for t in range(L)], axis=0)
    cp, sp = in_time_order(cj), in_time_order(sj)
    krow = lax.broadcasted_iota(jnp.int32, (SUBLANES, 2 * P), 0)
    consts = jnp.where(krow == 0, 1.0, jnp.where(krow == 1, L - 1.0,
                                                 jnp.where(krow == 2, float(L), 0.0)))
    ck, sk = jnp.cos(li * consts), jnp.sin(li * consts)
    (c_1, s_1), (c_m, s_m), (c_l, s_l) = [(ck[r:r + 1], sk[r:r + 1]) for r in range(3)]

    def cpow(tau, trig):
        mag = jnp.exp(lr * tau)
        return mag * trig[0], mag * trig[1]

    minus = lambda c0, s0, c, s: (c0 * c + s0 * s, s0 * c - c0 * s)
    plus = lambda c0, s0, c, s: (c0 * c - s0 * s, s0 * c + c0 * s)
    per_dir = lambda f, b: (jnp.where(fwd, f[0], b[0]), jnp.where(fwd, f[1], b[1]))

    ab_re, ab_im = cpow(jnp.ones_like(lr), (c_1, s_1))
    den = a_re * a_re + a_im * a_im
    co_re = ((ab_re - 1.0) * a_re + ab_im * a_im) / den
    co_im = (ab_im * a_re - (ab_re - 1.0) * a_im) / den
    bb_re, bb_im = cmul(bt_ref[gi, 0], bt_ref[gi, 1], co_re, co_im)
    cc_re, cc_im = ct_ref[gi, 0], ct_ref[gi, 1]

    def outer(p, m):
        (p_re, p_im), (m_re, m_im) = p, m
        blocks = [cmul(p_re[s:s + 1], p_im[s:s + 1], m_re, m_im) for s in range(L)]
        return (jnp.concatenate([b[0] for b in blocks], axis=0),
                jnp.concatenate([b[1] for b in blocks], axis=0))

    s_re, s_im = outer(cpow(jnp.where(fwd, L - 1 - j, j),
                            per_dir(minus(c_m, s_m, cj, sj), (cj, sj))), (bb_re, bb_im))
    s_ref[gi] = jnp.concatenate([s_re, s_im], axis=1).astype(_BF16)
    w_re, w_im = outer(cpow(jnp.where(fwd, j + 1, L - j),
                            per_dir(plus(c_1, s_1, cj, sj), minus(c_l, s_l, cj, sj))),
                       (cc_re, cc_im))
    wcat = jnp.concatenate([w_re, -w_im], axis=1)
    c_ref[gi] = wcat.astype(_BF16)
    a_re2, a_im2 = outer(cpow(jnp.where(fwd, pos, L - 1 - pos),
                              per_dir((cp, sp), minus(c_m, s_m, cp, sp))), (cc_re, cc_im))
    acat = jnp.concatenate([a_re2, -a_im2], axis=1)
    bcat = jnp.concatenate([bb_re, bb_im], axis=1)
    is_fwd_state = jnp.bitwise_and(lax.broadcasted_iota(jnp.int32, bcat.shape, 1), P) == 0

    def split(a):
        hi = a.astype(_BF16).astype(_F32)
        return hi, a - hi

    def dot_state(a_parts, b_parts):
        (a_hi, a_lo), (b_hi, b_lo) = [[p.astype(_BF16) for p in parts]
                                      for parts in (a_parts, b_parts)]
        d = lambda a, b: lax.dot_general(a, b, (((1,), (1,)), ((), ())),
                                         preferred_element_type=_F32)
        return d(a_hi, b_hi) + d(a_hi, b_lo) + d(a_lo, b_hi)

    b_parts, a_parts = split(bcat), split(acat)
    kf = dot_state([jnp.where(is_fwd_state, p, 0.0) for p in b_parts], a_parts)
    kb = dot_state([jnp.where(is_fwd_state, 0.0, p) for p in b_parts], a_parts)

    lane = lax.broadcasted_iota(jnp.int32, (C, LANES), 1)

    def shift_slots(k, n):
        halves = [k[:, :LANES], k[:, LANES:]]
        if n >= SLOTS_PER_VREG:
            halves, n = halves[::-1], n - SLOTS_PER_VREG
        if n == 0:
            return halves
        r = [pltpu.roll(h, n * C, 1) for h in halves]
        wrapped = lane < n * C
        return [jnp.where(wrapped, r[1], r[0]), jnp.where(wrapped, r[0], r[1])]

    col_t = lax.broadcasted_iota(jnp.int32, (C, L * C), 1)
    blocks = []
    for jt in range(L):
        fwd_part = jnp.concatenate(shift_slots(kf, jt), axis=1)
        bwd_part = jnp.concatenate(shift_slots(kb, (jt + 1) % L), axis=1)
        blk = (jnp.where(col_t >= jt * C, fwd_part, 0.0)
               + jnp.where(col_t < (jt + 1) * C, bwd_part, 0.0))
        blocks.append(jnp.concatenate(
            [pltpu.roll(blk[:, h * LANES:(h + 1) * LANES], g8 * C, 1) if g8 else
             blk[:, h * LANES:(h + 1) * LANES] for h in range(L * C // LANES)], axis=1))
    time_of = lambda s: (s // SLOTS_PER_VREG) * SLOTS_PER_VREG + (s - g8) % SLOTS_PER_VREG
    tm = jnp.concatenate([blocks[time_of(s)] for s in range(L)], axis=0)
    row = lax.broadcasted_iota(jnp.int32, (L * C, L * C), 0)
    col = lax.broadcasted_iota(jnp.int32, (L * C, L * C), 1)
    t_ref[gi] = (tm + jnp.where(row == col, dvec_ref[gi], 0.0)).astype(_BF16)
    al_re, al_im = cpow(jnp.full_like(lr, float(L)), (c_l, s_l))
    ar_ref[gi] = al_re
    ai_ref[gi] = al_im


def _attn_kernel(sink_ref, q_ref, k_ref, v_ref, bias_rows_ref, za_ref, o_ref, bias_ref):
    nq = q_ref.shape[1] // BLOCK

    @pl.when(pl.program_id(0) == 0)
    def _():
        qi = lax.broadcasted_iota(jnp.int32, (BLOCK, 3 * BLOCK), 0)
        ki = lax.broadcasted_iota(jnp.int32, (BLOCK, 3 * BLOCK), 1)
        in_band = jnp.abs(ki - BLOCK - qi) <= WINDOW
        for h in range(ATTN_HEADS):
            row = jnp.broadcast_to(bias_rows_ref[h:h + 1, :], (BLOCK, BIAS_ROW))
            rel = pltpu.roll(row, 0, 1, stride=1, stride_axis=0)
            bias_ref[h] = jnp.where(in_band, rel[:, :3 * BLOCK], NEG_INF)

    rows = lambda ref, c: ref[0, c * BLOCK:(c + 1) * BLOCK]
    k_blocks = [rows(k_ref, c) for c in range(nq)]
    v_blocks = [rows(v_ref, c) for c in range(nq)]
    lane = lax.broadcasted_iota(jnp.int32, (BLOCK, LANES), 1)
    halves = [lane < HEAD_DIM, lane >= HEAD_DIM]
    keep = [h.astype(_F32).astype(_BF16) for h in halves]
    slab = lambda blk, j, e: blk[:, (j ^ e) * LANES:((j ^ e) + 1) * LANES]
    k_half = {(j, e): [slab(kb, j, e) * keep[e] for kb in k_blocks]
              for j in range(KV_HEADS) for e in range(HEAD_PAIR)}
    v_half = {(j, e): [jnp.concatenate([slab(vb, j, e) * keep[e], keep[e]], axis=1)
                       for vb in v_blocks]
              for j in range(KV_HEADS) for e in range(HEAD_PAIR)}
    for c in range(nq):
        blocks = [b for b in (c - 1, c, c + 1) if 0 <= b < nq]
        slot0 = blocks[0] - (c - 1)
        nk = len(blocks) * BLOCK
        q = rows(q_ref, c)
        slabs_out = []
        for j in range(KV_HEADS):
            window = lambda halves_of: jnp.concatenate(
                [halves_of[j, e][b] for e in range(HEAD_PAIR) for b in blocks], axis=0)
            kcat = window(k_half)
            vcat = window(v_half)
            qs = jnp.concatenate(
                [q[:, (j * PAIRS_PER_KV + i) * LANES:(j * PAIRS_PER_KV + i + 1) * LANES]
                 for i in range(PAIRS_PER_KV)], axis=0)
            s = lax.dot_general(qs, kcat, (((1,), (1,)), ((), ())),
                                preferred_element_type=_F32)
            p_rows, e_rows = [], []
            for i in range(PAIRS_PER_KV):
                p_lanes, e_sink = [], []
                for e in range(HEAD_PAIR):
                    head = j * Q_PER_KV + i * HEAD_PAIR + e
                    sg = (s[i * BLOCK:(i + 1) * BLOCK, e * nk:(e + 1) * nk]
                          + bias_ref[head, :, slot0 * BLOCK:slot0 * BLOCK + nk])
                    sk = LOG2_E * sink_ref[head]
                    m = jnp.maximum(jnp.max(sg, axis=-1, keepdims=True), sk)
                    p_lanes.append(jnp.exp2(sg - m).astype(_BF16))
                    e_sink.append(jnp.broadcast_to(jnp.exp2(sk - m), (BLOCK, LANES)))
                p_rows.append(jnp.concatenate(p_lanes, axis=1))
                e_rows.append(jnp.where(halves[0], e_sink[0], e_sink[1]))
            p = jnp.concatenate(p_rows, axis=0)
            o = jnp.dot(p, vcat, preferred_element_type=_F32)
            den = o[:, LANES:] + jnp.concatenate(e_rows, axis=0)
            on = o[:, :LANES] / den
            slabs_out += [on[i * BLOCK:(i + 1) * BLOCK] for i in range(PAIRS_PER_KV)]
        o_all = jnp.concatenate(slabs_out, axis=1)
        o_ref[0, c * BLOCK:(c + 1) * BLOCK, :] = (
            o_all * rows(za_ref, c).astype(_F32)).astype(o_ref.dtype)


def _attention(sink, q, k, v, bias_rows, za):
    b, s, _ = q.shape
    seq = lambda w: pl.BlockSpec((1, s, w), lambda i: (i, 0, 0))
    return pl.pallas_call(
        _attn_kernel,
        grid=(b,),
        in_specs=[pl.BlockSpec(memory_space=pltpu.SMEM), seq(D_ATTN), seq(D_KVX), seq(D_KVX),
                  pl.BlockSpec(bias_rows.shape, lambda i: (0, 0)), seq(D_ATTN)],
        out_specs=seq(D_ATTN),
        out_shape=jax.ShapeDtypeStruct((b, s, D_ATTN), _BF16),
        scratch_shapes=[pltpu.VMEM((ATTN_HEADS, BLOCK, 3 * BLOCK), _F32)],
        compiler_params=pltpu.CompilerParams(
            dimension_semantics=("arbitrary",), vmem_limit_bytes=VMEM_LIMIT_BYTES),
        name="attn",
    )(sink, q, k, v, bias_rows, za)


def _t5_bucket_np(rel):
    half = NUM_BUCKETS // 2
    ret = (rel > 0).astype(np.int64) * half
    n = np.abs(rel)
    max_exact = half // 2
    nf = np.maximum(n, 1).astype(np.float64)
    large = max_exact + (np.log(nf / max_exact) / math.log(MAX_DISTANCE / max_exact)
                         * (half - max_exact)).astype(np.int64)
    large = np.minimum(large, half - 1)
    return ret + np.where(n < max_exact, n, large)


def _attn_bias(rel_table):
    rel = (np.arange(BIAS_ROW) + BLOCK) % BIAS_ROW - 2 * BLOCK
    onehot = (_t5_bucket_np(rel)[None] == np.arange(NUM_BUCKETS)[:, None])
    return jnp.einsum('bh,br->hr', LOG2_E * rel_table.astype(_F32),
                      jnp.asarray(onehot, _F32), precision=_HI)


def _merge_kernel(x_ref, gain_ref, wg_ref, bg_ref, ao_ref, yg_ref, zs_ref, wglu_f32_ref,
                  bglu_ref, wba_f32_ref, wbs_f32_ref, wout_f32_ref, fgain_ref, o_ref,
                  wglu_ref, wba_ref, wbs_ref, wout_ref, *y_scrs):
    nb, nt = x_ref.shape[0], x_ref.shape[1]
    rows = nb * SUB_STEPS

    @pl.when(pl.program_id(0) == 0)
    def _():
        for dst, src, scale in ((wglu_ref, wglu_f32_ref, 0.25), (wba_ref, wba_f32_ref, 1.0),
                                (wbs_ref, wbs_f32_ref, 0.25), (wout_ref, wout_f32_ref, 0.5)):
            dst[...] = (scale * src[...]).astype(_BF16)

    for st in range(nt // SUB_STEPS):
        steps = slice(st * SUB_STEPS, (st + 1) * SUB_STEPS)
        x = x_ref[:, steps, :].reshape(rows, D_MODEL)
        h = _rms_normalize(x, 0.5 * gain_ref[...]).astype(_BF16)
        gates2 = _twice_sigmoid_of_double(
            jnp.dot(h, wg_ref[:, W_OFF_GATES - GATE_BLOCK:], preferred_element_type=_F32)
            + 0.5 * bg_ref[...])
        for c in range(SUB_STEPS // CHUNK):
            _from_group_major(yg_ref, y_scrs[st], st * (SUB_STEPS // CHUNK) + c, nb, c * CHUNK,
                              SUB_STEPS)
        y = jnp.concatenate([y_scrs[st][v] for v in range(D_SSM // LANES)], axis=1)
        y2 = _twice_gelu(y)
        glu_half = (jnp.dot(y2.astype(_BF16), wglu_ref[...], preferred_element_type=_F32)
                    + 0.5 * bglu_ref[...])
        ssm4 = (y2 * _twice_sigmoid_of_double(glu_half)
                * zs_ref[:, steps, :].reshape(rows, D_SSM).astype(_F32))
        pa = jnp.dot(ao_ref[:, steps, :].reshape(rows, D_ATTN), wba_ref[...],
                     preferred_element_type=_F32)
        ps = jnp.dot(ssm4.astype(_BF16), wbs_ref[...], preferred_element_type=_F32)
        merged2 = gates2[:, :D_MODEL] * pa + gates2[:, D_MODEL:] * ps
        xn = x + jnp.dot(merged2.astype(_BF16), wout_ref[...], preferred_element_type=_F32)
        o_ref[:, steps, :] = _rms_normalize(xn, fgain_ref[...]).reshape(nb, SUB_STEPS, D_MODEL)


def _merge(x, gain, wg, bg, ao, yg, zs, wglu, bglu, wba, wbs, wout, fgain):
    bsz, s, _ = x.shape
    nt = TOKEN_TILE // bsz
    tok = lambda i: (0, i, 0)
    fixed = lambda i: (0, 0)
    full = lambda a: pl.BlockSpec(a.shape, fixed)
    once = lambda a: pl.BlockSpec(a.shape, fixed, pipeline_mode=pl.Buffered(1))
    act = lambda n: pl.BlockSpec((bsz, nt, n), tok)
    return pl.pallas_call(
        _merge_kernel,
        grid=(s // nt,),
        in_specs=[act(D_MODEL), full(gain),
                  pl.BlockSpec((D_MODEL, GATE_BLOCK), lambda i: (0, 1)),
                  full(bg), act(D_ATTN),
                  pl.BlockSpec((N_SSM_GROUPS, nt // CHUNK, bsz, CHUNK_W),
                               lambda i: (0, i, 0, 0)),
                  act(D_SSM), once(wglu), full(bglu), once(wba), once(wbs), once(wout),
                  full(fgain)],
        out_specs=act(D_MODEL),
        out_shape=jax.ShapeDtypeStruct((bsz, s, D_MODEL), _F32),
        scratch_shapes=[pltpu.VMEM(w.shape, _BF16) for w in (wglu, wba, wbs, wout)]
                       + [pltpu.VMEM((D_SSM // LANES, bsz * SUB_STEPS, LANES), _F32)
                          for _ in range(nt // SUB_STEPS)],
        compiler_params=pltpu.CompilerParams(
            dimension_semantics=("arbitrary",), vmem_limit_bytes=VMEM_LIMIT_BYTES),
        name="merge",
    )(x, gain, wg, bg, ao, yg, zs, wglu, bglu, wba, wbs, wout, fgain)


def _layer(x, norm_gain, w_in, b_gate, attn_sink, a_re, a_im, log_dt, b_re, b_im, c_re, c_im,
           d_skip, w_glu, b_glu, w_ba, w_bs, w_out, out_gain, bias):
    bsz, s, d = x.shape
    assert d == D_MODEL and x.dtype == _F32 and w_in.shape == (D_MODEL, W_OFF_GATES + 2 * D_MODEL)
    assert bsz % 8 == 0 and s % BLOCK == 0
    assert INPROJ_TOKEN_TILE % (bsz * SUB_STEPS) == 0 and s % (INPROJ_TOKEN_TILE // bsz) == 0
    assert TOKEN_TILE % (bsz * SUB_STEPS) == 0 and s % (TOKEN_TILE // bsz) == 0
    n_chunks = s // CHUNK
    gain = norm_gain.reshape(1, D_MODEL).astype(_F32)
    w_bf = w_in.astype(_BF16)
    q, k, v, za, ug, zs = _inproj(x, gain, w_bf)

    tmat, smat, cmat, ar, ai = _ssm_matrices(a_re, a_im, log_dt, b_re, b_im, c_re, c_im,
                                             d_skip)
    yg = _ssm(ug.reshape(N_SSM_GROUPS, n_chunks * bsz, CHUNK_W), tmat, smat, cmat, ar, ai,
              n_chunks, bsz)
    yg = yg.reshape(N_SSM_GROUPS, n_chunks, bsz, CHUNK_W)

    ao = _attention(attn_sink.astype(_F32), q, k, v, bias, za)

    return _merge(x, gain, w_bf, b_gate.reshape(1, -1).astype(_F32), ao, yg, zs, w_glu,
                  b_glu.reshape(1, -1).astype(_F32), w_ba, w_bs, w_out,
                  out_gain.reshape(1, D_MODEL).astype(_F32))


def kernel(x, norm_gain, w_in, b_gate, attn_sink, rel_bias_table, ssm_a_re, ssm_a_im,
           ssm_log_dt, ssm_b_re, ssm_b_im, ssm_c_re, ssm_c_im, ssm_d, w_glu, b_glu,
           w_branch_attn, w_branch_ssm, w_out, final_norm_gain):
    depth = norm_gain.shape[0]
    assert depth == 1, "final norm is fused into the single layer's epilogue"
    bias = _attn_bias(rel_bias_table)
    l = 0
    return _layer(x, norm_gain[l], w_in[l], b_gate[l], attn_sink[l], ssm_a_re[l],
                  ssm_a_im[l], ssm_log_dt[l], ssm_b_re[l], ssm_b_im[l], ssm_c_re[l],
                  ssm_c_im[l], ssm_d[l], w_glu[l], b_glu[l], w_branch_attn[l],
                  w_branch_ssm[l], w_out[l], final_norm_gain, bias)
```

```python
import functools
import math

import jax
import jax.numpy as jnp
import numpy as np
from jax import lax
from jax.experimental import pallas as pl
from jax.experimental.pallas import tpu as pltpu

D_MODEL = 1024
ATTN_HEADS = 8
KV_HEADS = 2
Q_PER_KV = ATTN_HEADS // KV_HEADS
HEAD_DIM = 64
D_ATTN = ATTN_HEADS * HEAD_DIM
D_KV = KV_HEADS * HEAD_DIM
WINDOW = 128
BLOCK = 128
NUM_BUCKETS = 32
MAX_DISTANCE = 128
D_SSM = 512
SSM_GROUP = 16
N_SSM_GROUPS = D_SSM // SSM_GROUP
SSM_STATE = 64
EPS = 1e-6
NEG_INF = -1e30

LANES = 128
SUBLANES = 8
MXU_WIDTH = 256
CHUNK = MXU_WIDTH // SSM_GROUP
CHUNK_W = CHUNK * SSM_GROUP
STATE_W = 4 * SSM_STATE
SLOTS_PER_VREG = LANES // SSM_GROUP
SSM_GROUPS_PER_STEP = 4

HEAD_PAIR = LANES // HEAD_DIM
PAIRS_PER_KV = Q_PER_KV // HEAD_PAIR
assert D_KV == LANES and KV_HEADS == HEAD_PAIR
D_KVX = 2 * D_KV
LOG2_E = math.log2(math.e)
Q_SCALE = HEAD_DIM ** -0.5 * LOG2_E
BIAS_ROW = 4 * BLOCK

W_OFF_ZA = D_ATTN + 2 * D_KV
W_OFF_U = W_OFF_ZA + D_ATTN
W_OFF_GATES = W_OFF_U + 2 * D_SSM
GATE_BLOCK = (W_OFF_GATES + 2 * D_MODEL) // 2
assert GATE_BLOCK % LANES == 0 and GATE_BLOCK <= W_OFF_GATES

TOKEN_TILE = 1024
INPROJ_TOKEN_TILE = 2048
SUB_STEPS = CHUNK
VMEM_LIMIT_BYTES = 56 * 1024 * 1024

_F32 = jnp.float32
_BF16 = jnp.bfloat16
_HI = lax.Precision.HIGHEST


def _silu(x):
    return x * (0.5 * jnp.tanh(0.5 * x) + 0.5)


def _twice_sigmoid_of_double(xh):
    return 1.0 + jnp.tanh(xh)


def _twice_gelu(x):
    c = math.sqrt(2.0 / math.pi)
    return x * (1.0 + jnp.tanh(c * (x + 0.044715 * (x * x * x))))


def _rms_normalize(x, gain):
    ms = jnp.mean(x * x, axis=-1, keepdims=True)
    return x * lax.rsqrt(ms + EPS) * gain


def _slot_masks(rows):
    lane = lax.broadcasted_iota(jnp.int32, (rows, LANES), 1)
    return [(lane >= p * SSM_GROUP) & (lane < (p + 1) * SSM_GROUP)
            for p in range(SLOTS_PER_VREG)]


def _to_group_major(u_scr, ug_ref, kk, nb, t0, steps):
    masks = _slot_masks(nb)
    for v in range(D_SSM // LANES):
        for half in range(CHUNK // SLOTS_PER_VREG):
            rolled = []
            for t8 in range(SLOTS_PER_VREG):
                r = t0 + half * SLOTS_PER_VREG + t8
                piece = u_scr[v, pl.ds(r, nb, stride=steps), :]
                rolled.append(pltpu.roll(piece, t8 * SSM_GROUP, 1) if t8 else piece)
            for p0 in range(SLOTS_PER_VREG):
                acc = rolled[0]
                for t8 in range(1, SLOTS_PER_VREG):
                    acc = jnp.where(masks[(p0 + t8) % SLOTS_PER_VREG], rolled[t8], acc)
                g = v * SLOTS_PER_VREG + p0
                ug_ref[g, kk, :, half * LANES:(half + 1) * LANES] = acc.astype(_BF16)


def _from_group_major(yg_ref, y_scr, kk, nb, t0, steps):
    masks = _slot_masks(nb)
    for v in range(D_SSM // LANES):
        for half in range(CHUNK // SLOTS_PER_VREG):
            src = [yg_ref[v * SLOTS_PER_VREG + p0, kk, :,
                          half * LANES:(half + 1) * LANES].astype(_F32)
                   for p0 in range(SLOTS_PER_VREG)]
            for t8 in range(SLOTS_PER_VREG):
                acc = src[0]
                for p0 in range(1, SLOTS_PER_VREG):
                    acc = jnp.where(masks[(p0 + t8) % SLOTS_PER_VREG], src[p0], acc)
                if t8:
                    acc = pltpu.roll(acc, LANES - t8 * SSM_GROUP, 1)
                r = t0 + half * SLOTS_PER_VREG + t8
                y_scr[v, pl.ds(r, nb, stride=steps), :] = acc


def _inproj_kernel(x_ref, gain_ref, w_ref, q_ref, k_ref, v_ref, za_ref, ug_ref, zs_ref,
                   *u_scrs):
    nb, nt = x_ref.shape[0], x_ref.shape[1]
    rows = nb * SUB_STEPS
    for st in range(nt // SUB_STEPS):
        steps = slice(st * SUB_STEPS, (st + 1) * SUB_STEPS)
        x = x_ref[:, steps, :].reshape(rows, D_MODEL)
        h = _rms_normalize(x, gain_ref[...]).astype(_BF16)

        def put(ref, val):
            ref[:, steps, :] = val.astype(_BF16).reshape(nb, SUB_STEPS, val.shape[-1])

        proj = lambda lo, n: jnp.dot(h, w_ref[:, lo:lo + n], preferred_element_type=_F32)
        u_zs = proj(W_OFF_U, 2 * D_SSM)
        for v in range(D_SSM // LANES):
            u_scrs[st][v] = u_zs[:, v * LANES:(v + 1) * LANES]
        for c in range(SUB_STEPS // CHUNK):
            _to_group_major(u_scrs[st], ug_ref, st * (SUB_STEPS // CHUNK) + c, nb, c * CHUNK,
                            SUB_STEPS)
        put(zs_ref, _silu(u_zs[:, D_SSM:]))
        put(za_ref, _silu(proj(W_OFF_ZA, D_ATTN)))
        qkv = proj(0, W_OFF_ZA)
        put(q_ref, qkv[:, :D_ATTN] * Q_SCALE)
        with_swap = lambda a: jnp.concatenate([a, pltpu.roll(a, HEAD_DIM, 1)], axis=1)
        put(k_ref, with_swap(qkv[:, D_ATTN:D_ATTN + D_KV]))
        put(v_ref, with_swap(qkv[:, D_ATTN + D_KV:]))


def _inproj(x, gain, w_in):
    bsz, s, _ = x.shape
    nt = INPROJ_TOKEN_TILE // bsz
    tok = lambda i: (0, i, 0)
    fixed = lambda i: (0, 0)
    outs = [D_ATTN, D_KVX, D_KVX, D_ATTN, None, D_SSM]
    act = lambda n: (pl.BlockSpec((bsz, nt, n), tok), jax.ShapeDtypeStruct((bsz, s, n), _BF16))
    ug = (pl.BlockSpec((N_SSM_GROUPS, nt // CHUNK, bsz, CHUNK_W), lambda i: (0, i, 0, 0)),
          jax.ShapeDtypeStruct((N_SSM_GROUPS, s // CHUNK, bsz, CHUNK_W), _BF16))
    specs, shapes = zip(*[ug if n is None else act(n) for n in outs])
    return pl.pallas_call(
        _inproj_kernel,
        grid=(s // nt,),
        in_specs=[pl.BlockSpec((bsz, nt, D_MODEL), tok),
                  pl.BlockSpec((1, D_MODEL), fixed)]
                 + [pl.BlockSpec((D_MODEL, W_OFF_GATES), fixed)],
        out_specs=list(specs),
        out_shape=list(shapes),
        scratch_shapes=[pltpu.VMEM((D_SSM // LANES, bsz * SUB_STEPS, LANES), _F32)
                        for _ in range(nt // SUB_STEPS)],
        compiler_params=pltpu.CompilerParams(
            dimension_semantics=("arbitrary",), vmem_limit_bytes=VMEM_LIMIT_BYTES),
        name="inproj",
    )(x, gain, w_in)


def _ssm_kernel(n_chunks, rows_per_chunk, u_ref, t_ref, s_ref, c_ref, ar_ref, ai_ref,
                y_ref, s_scr, x_scr):
    rb = rows_per_chunk
    groups = range(u_ref.shape[0])
    for gi in groups:
        s_scr[gi] = jnp.dot(u_ref[gi], s_ref[gi], preferred_element_type=_F32)
    ar = [jnp.broadcast_to(ar_ref[gi], (rb, 2 * SSM_STATE)) for gi in groups]
    ai = [jnp.broadcast_to(ai_ref[gi], (rb, 2 * SSM_STATE)) for gi in groups]
    is_fwd = lax.broadcasted_iota(jnp.int32, (rb, 2 * SSM_STATE), 1) < SSM_STATE
    is_bwd = jnp.logical_not(is_fwd)
    re_cols = pl.ds(0, 2 * SSM_STATE)
    im_cols = pl.ds(2 * SSM_STATE, 2 * SSM_STATE)

    def step(i, carry):
        rows_f = pl.ds(pl.multiple_of(i * rb, rb), rb)
        rows_b = pl.ds(pl.multiple_of((n_chunks - 1 - i) * rb, rb), rb)
        new = []
        for gi in groups:
            st_re, st_im = carry[2 * gi], carry[2 * gi + 1]
            pltpu.store(x_scr.at[gi, rows_f, re_cols], st_re, mask=is_fwd)
            pltpu.store(x_scr.at[gi, rows_f, im_cols], st_im, mask=is_fwd)
            pltpu.store(x_scr.at[gi, rows_b, re_cols], st_re, mask=is_bwd)
            pltpu.store(x_scr.at[gi, rows_b, im_cols], st_im, mask=is_bwd)
            in_re = jnp.where(is_fwd, s_scr[gi, rows_f, re_cols], s_scr[gi, rows_b, re_cols])
            in_im = jnp.where(is_fwd, s_scr[gi, rows_f, im_cols], s_scr[gi, rows_b, im_cols])
            new.append(ar[gi] * st_re - ai[gi] * st_im + in_re)
            new.append(ar[gi] * st_im + ai[gi] * st_re + in_im)
        return tuple(new)

    zero = jnp.zeros((rb, 2 * SSM_STATE), _F32)
    lax.fori_loop(0, n_chunks, step, (zero,) * (2 * len(groups)), unroll=True)

    for gi in groups:
        y = jnp.dot(u_ref[gi], t_ref[gi], preferred_element_type=_F32)
        y = y + lax.dot_general(x_scr[gi].astype(_BF16), c_ref[gi], (((1,), (1,)), ((), ())),
                                preferred_element_type=_F32)
        y_ref[gi] = y.astype(y_ref.dtype)


def _ssm(ug, tmat, smat, cmat, ar, ai, n_chunks, rows_per_chunk):
    g, rows, _ = ug.shape
    gb = SSM_GROUPS_PER_STEP
    per_group = lambda i: (i, 0, 0)
    return pl.pallas_call(
        functools.partial(_ssm_kernel, n_chunks, rows_per_chunk),
        grid=(g // gb,),
        in_specs=[pl.BlockSpec((gb, rows, CHUNK_W), per_group),
                  pl.BlockSpec((gb, CHUNK_W, CHUNK_W), per_group),
                  pl.BlockSpec((gb, CHUNK_W, STATE_W), per_group),
                  pl.BlockSpec((gb, CHUNK_W, STATE_W), per_group),
                  pl.BlockSpec((gb, 1, 2 * SSM_STATE), per_group),
                  pl.BlockSpec((gb, 1, 2 * SSM_STATE), per_group)],
        out_specs=pl.BlockSpec((gb, rows, CHUNK_W), per_group),
        out_shape=jax.ShapeDtypeStruct((g, rows, CHUNK_W), _BF16),
        scratch_shapes=[pltpu.VMEM((gb, rows, STATE_W), _F32),
                        pltpu.VMEM((gb, rows, STATE_W), _F32)],
        compiler_params=pltpu.CompilerParams(
            dimension_semantics=("arbitrary",), vmem_limit_bytes=VMEM_LIMIT_BYTES),
        name="ssm",
    )(ug, tmat, smat, cmat, ar, ai)


def _ssm_matrices(a_re, a_im, log_dt, b_re, b_im, c_re, c_im, d_skip):
    L, G, P, C = CHUNK, N_SSM_GROUPS, SSM_STATE, SSM_GROUP
    bt = jnp.stack([b_re, b_im], axis=0).transpose(2, 0, 4, 1, 3).reshape(G, 2, C, 2 * P)
    dvec = jnp.tile(d_skip.reshape(G, 1, C), (1, 1, L))
    gb = SLOTS_PER_VREG
    per_group = lambda *blk: pl.BlockSpec((gb,) + blk, lambda i: (i,) + (0,) * len(blk))
    per_dir = lambda *blk: pl.BlockSpec((2, gb) + blk, lambda i: (0, i) + (0,) * len(blk))
    mat = jax.ShapeDtypeStruct((G, L * C, L * C), _BF16)
    vec = jax.ShapeDtypeStruct((G, 1, 2 * P), _F32)
    return pl.pallas_call(
        _ssm_prep_kernel,
        grid=(G // gb,),
        in_specs=[pl.BlockSpec(memory_space=pltpu.SMEM), per_dir(P), per_dir(P),
                  per_group(2, C, 2 * P), per_dir(C, P), per_dir(C, P), per_group(1, L * C)],
        out_specs=[per_group(L * C, L * C), per_group(L * C, STATE_W),
                   per_group(L * C, STATE_W), per_group(1, 2 * P), per_group(1, 2 * P)],
        out_shape=[mat, mat, mat, vec, vec],
        compiler_params=pltpu.CompilerParams(dimension_semantics=("arbitrary",)),
        name="ssm_prep",
    )(log_dt, a_re, a_im, bt, c_re, c_im, dvec)


def _ssm_prep_kernel(ldt_ref, are_ref, aim_ref, bt_ref, cre_ref, cim_ref, dvec_ref,
                     t_ref, s_ref, c_ref, ar_ref, ai_ref):
    for gi in range(bt_ref.shape[0]):
        _ssm_prep_group(gi, ldt_ref, are_ref, aim_ref, bt_ref, cre_ref, cim_ref, dvec_ref,
                        t_ref, s_ref, c_ref, ar_ref, ai_ref)


def _ssm_prep_group(gi, ldt_ref, are_ref, aim_ref, bt_ref, cre_ref, cim_ref, dvec_ref,
                    t_ref, s_ref, c_ref, ar_ref, ai_ref):
    L, P, C = CHUNK, SSM_STATE, SSM_GROUP
    g8 = gi % SLOTS_PER_VREG
    g = pl.program_id(0) * bt_ref.shape[0] + gi
    both = lambda ref, rows: jnp.concatenate([ref[0, rows], ref[1, rows]], axis=-1)

    def time_of_slot(slot):
        return (jnp.bitwise_and(slot, -SLOTS_PER_VREG)
                + jnp.bitwise_and(slot - g8, SLOTS_PER_VREG - 1))

    def cmul(x_re, x_im, y_re, y_im):
        return x_re * y_re - x_im * y_im, x_re * y_im + x_im * y_re

    a_re, a_im = both(are_ref, pl.ds(gi, 1)), both(aim_ref, pl.ds(gi, 1))
    dt = jnp.exp(jnp.where(lax.broadcasted_iota(jnp.int32, (1, 2 * P), 1) < P,
                           ldt_ref[0, g], ldt_ref[1, g]))
    lr, li = a_re * dt, a_im * dt

    j = time_of_slot(lax.broadcasted_iota(jnp.int32, (L, 2 * P), 0)).astype(_F32)
    pos = lax.broadcasted_iota(jnp.int32, (L, 2 * P), 0).astype(_F32)
    fwd = lax.broadcasted_iota(jnp.int32, (L, 2 * P), 1) < P
    cj, sj = jnp.cos(li * j), jnp.sin(li * j)
    slot_of = lambda t: (t // SLOTS_PER_VREG) * SLOTS_PER_VREG + (t + g8) % SLOTS_PER_VREG
    in_time_order = lambda a: jnp.concatenate(
        [a[slot_of(t):slot_of(t) + 1] for t in range(L)], axis=0)
    cp, sp = in_time_order(cj), in_time_order(sj)
    krow = lax.broadcasted_iota(jnp.int32, (SUBLANES, 2 * P), 0)
    consts = jnp.where(krow == 0, 1.0, jnp.where(krow == 1, L - 1.0,
                                                 jnp.where(krow == 2, float(L), 0.0)))
    ck, sk = jnp.cos(li * consts), jnp.sin(li * consts)
    (c_1, s_1), (c_m, s_m), (c_l, s_l) = [(ck[r:r + 1], sk[r:r + 1]) for r in range(3)]

    def cpow(tau, trig):
        mag = jnp.exp(lr * tau)
        return mag * trig[0], mag * trig[1]

    minus = lambda c0, s0, c, s: (c0 * c + s0 * s, s0 * c - c0 * s)
    plus = lambda c0, s0, c, s: (c0 * c - s0 * s, s0 * c + c0 * s)
    per_dir = lambda f, b: (jnp.where(fwd, f[0], b[0]), jnp.where(fwd, f[1], b[1]))

    ab_re, ab_im = cpow(jnp.ones_like(lr), (c_1, s_1))
    den = a_re * a_re + a_im * a_im
    co_re = ((ab_re - 1.0) * a_re + ab_im * a_im) / den
    co_im = (ab_im * a_re - (ab_re - 1.0) * a_im) / den
    bb_re, bb_im = cmul(bt_ref[gi, 0], bt_ref[gi, 1], co_re, co_im)
    cc_re, cc_im = both(cre_ref, gi), both(cim_ref, gi)

    def outer(p, m):
        (p_re, p_im), (m_re, m_im) = p, m
        blocks = [cmul(p_re[s:s + 1], p_im[s:s + 1], m_re, m_im) for s in range(L)]
        return (jnp.concatenate([b[0] for b in blocks], axis=0),
                jnp.concatenate([b[1] for b in blocks], axis=0))

    s_re, s_im = outer(cpow(jnp.where(fwd, L - 1 - j, j),
                            per_dir(minus(c_m, s_m, cj, sj), (cj, sj))), (bb_re, bb_im))
    s_ref[gi] = jnp.concatenate([s_re, s_im], axis=1).astype(_BF16)
    w_re, w_im = outer(cpow(jnp.where(fwd, j + 1, L - j),
                            per_dir(plus(c_1, s_1, cj, sj), minus(c_l, s_l, cj, sj))),
                       (cc_re, cc_im))
    wcat = jnp.concatenate([w_re, -w_im], axis=1)
    c_ref[gi] = wcat.astype(_BF16)
    a_re2, a_im2 = outer(cpow(jnp.where(fwd, pos, L - 1 - pos),
                              per_dir((cp, sp), minus(c_m, s_m, cp, sp))), (cc_re, cc_im))
    acat = jnp.concatenate([a_re2, -a_im2], axis=1)
    bcat = jnp.concatenate([bb_re, bb_im], axis=1)
    is_fwd_state = jnp.bitwise_and(lax.broadcasted_iota(jnp.int32, bcat.shape, 1), P) == 0

    def split(a):
        hi = a.astype(_BF16).astype(_F32)
        return hi, a - hi

    def dot_state(a_parts, b_parts):
        (a_hi, a_lo), (b_hi, b_lo) = [[p.astype(_BF16) for p in parts]
                                      for parts in (a_parts, b_parts)]
        d = lambda a, b: lax.dot_general(a, b, (((1,), (1,)), ((), ())),
                                         preferred_element_type=_F32)
        return d(a_hi, b_hi) + d(a_hi, b_lo) + d(a_lo, b_hi)

    b_parts, a_parts = split(bcat), split(acat)
    kf = dot_state([jnp.where(is_fwd_state, p, 0.0) for p in b_parts], a_parts)
    kb = dot_state([jnp.where(is_fwd_state, 0.0, p) for p in b_parts], a_parts)

    lane = lax.broadcasted_iota(jnp.int32, (C, LANES), 1)

    def shift_slots(k, n):
        halves = [k[:, :LANES], k[:, LANES:]]
        if n >= SLOTS_PER_VREG:
            halves, n = halves[::-1], n - SLOTS_PER_VREG
        if n == 0:
            return halves
        r = [pltpu.roll(h, n * C, 1) for h in halves]
        wrapped = lane < n * C
        return [jnp.where(wrapped, r[1], r[0]), jnp.where(wrapped, r[0], r[1])]

    col_t = lax.broadcasted_iota(jnp.int32, (C, L * C), 1)
    blocks = []
    for jt in range(L):
        fwd_part = jnp.concatenate(shift_slots(kf, jt), axis=1)
        bwd_part = jnp.concatenate(shift_slots(kb, (jt + 1) % L), axis=1)
        blk = (jnp.where(col_t >= jt * C, fwd_part, 0.0)
               + jnp.where(col_t < (jt + 1) * C, bwd_part, 0.0))
        blocks.append(jnp.concatenate(
            [pltpu.roll(blk[:, h * LANES:(h + 1) * LANES], g8 * C, 1) if g8 else
             blk[:, h * LANES:(h + 1) * LANES] for h in range(L * C // LANES)], axis=1))
    time_of = lambda s: (s // SLOTS_PER_VREG) * SLOTS_PER_VREG + (s - g8) % SLOTS_PER_VREG
    tm = jnp.concatenate([blocks[time_of(s)] for s in range(L)], axis=0)
    row = lax.broadcasted_iota(jnp.int32, (L * C, L * C), 0)
    col = lax.broadcasted_iota(jnp.int32, (L * C, L * C), 1)
    t_ref[gi] = (tm + jnp.where(row == col, dvec_ref[gi], 0.0)).astype(_BF16)
    al_re, al_im = cpow(jnp.full_like(lr, float(L)), (c_l, s_l))
    ar_ref[gi] = al_re
    ai_ref[gi] = al_im


def _attn_kernel(sink_ref, q_ref, k_ref, v_ref, bias_rows_ref, za_ref, o_ref, bias_ref):
    nq = q_ref.shape[1] // BLOCK

    @pl.when(pl.program_id(0) == 0)
    def _():
        qi = lax.broadcasted_iota(jnp.int32, (BLOCK, 3 * BLOCK), 0)
        ki = lax.broadcasted_iota(jnp.int32, (BLOCK, 3 * BLOCK), 1)
        in_band = jnp.abs(ki - BLOCK - qi) <= WINDOW
        for h in range(ATTN_HEADS):
            row = jnp.broadcast_to(bias_rows_ref[h:h + 1, :], (BLOCK, BIAS_ROW))
            rel = pltpu.roll(row, 0, 1, stride=1, stride_axis=0)
            bias_ref[h] = jnp.where(in_band, rel[:, :3 * BLOCK], NEG_INF)

    rows = lambda ref, c: ref[0, c * BLOCK:(c + 1) * BLOCK]
    k_blocks = [rows(k_ref, c) for c in range(nq)]
    v_blocks = [rows(v_ref, c) for c in range(nq)]
    lane = lax.broadcasted_iota(jnp.int32, (BLOCK, LANES), 1)
    halves = [lane < HEAD_DIM, lane >= HEAD_DIM]
    keep = [h.astype(_F32).astype(_BF16) for h in halves]
    slab = lambda blk, j, e: blk[:, (j ^ e) * LANES:((j ^ e) + 1) * LANES]
    k_half = {(j, e): [slab(kb, j, e) * keep[e] for kb in k_blocks]
              for j in range(KV_HEADS) for e in range(HEAD_PAIR)}
    v_half = {(j, e): [jnp.concatenate([slab(vb, j, e) * keep[e], keep[e]], axis=1)
                       for vb in v_blocks]
              for j in range(KV_HEADS) for e in range(HEAD_PAIR)}
    for c in range(nq):
        blocks = [b for b in (c - 1, c, c + 1) if 0 <= b < nq]
        slot0 = blocks[0] - (c - 1)
        nk = len(blocks) * BLOCK
        q = rows(q_ref, c)
        slabs_out = []
        for j in range(KV_HEADS):
            window = lambda halves_of: jnp.concatenate(
                [halves_of[j, e][b] for e in range(HEAD_PAIR) for b in blocks], axis=0)
            kcat = window(k_half)
            vcat = window(v_half)
            qs = jnp.concatenate(
                [q[:, (j * PAIRS_PER_KV + i) * LANES:(j * PAIRS_PER_KV + i + 1) * LANES]
                 for i in range(PAIRS_PER_KV)], axis=0)
            s = lax.dot_general(qs, kcat, (((1,), (1,)), ((), ())),
                                preferred_element_type=_F32)
            p_rows, e_rows = [], []
            for i in range(PAIRS_PER_KV):
                p_lanes, e_sink = [], []
                for e in range(HEAD_PAIR):
                    head = j * Q_PER_KV + i * HEAD_PAIR + e
                    sg = (s[i * BLOCK:(i + 1) * BLOCK, e * nk:(e + 1) * nk]
                          + bias_ref[head, :, slot0 * BLOCK:slot0 * BLOCK + nk])
                    sk = LOG2_E * sink_ref[head]
                    m = jnp.maximum(jnp.max(sg, axis=-1, keepdims=True), sk)
                    p_lanes.append(jnp.exp2(sg - m).astype(_BF16))
                    e_sink.append(jnp.broadcast_to(jnp.exp2(sk - m), (BLOCK, LANES)))
                p_rows.append(jnp.concatenate(p_lanes, axis=1))
                e_rows.append(jnp.where(halves[0], e_sink[0], e_sink[1]))
            p = jnp.concatenate(p_rows, axis=0)
            o = jnp.dot(p, vcat, preferred_element_type=_F32)
            den = o[:, LANES:] + jnp.concatenate(e_rows, axis=0)
            on = o[:, :LANES] / den
            slabs_out += [on[i * BLOCK:(i + 1) * BLOCK] for i in range(PAIRS_PER_KV)]
        o_all = jnp.concatenate(slabs_out, axis=1)
        o_ref[0, c * BLOCK:(c + 1) * BLOCK, :] = (
            o_all * rows(za_ref, c).astype(_F32)).astype(o_ref.dtype)


def _attention(sink, q, k, v, bias_rows, za):
    b, s, _ = q.shape
    seq = lambda w: pl.BlockSpec((1, s, w), lambda i: (i, 0, 0))
    return pl.pallas_call(
        _attn_kernel,
        grid=(b,),
        in_specs=[pl.BlockSpec(memory_space=pltpu.SMEM), seq(D_ATTN), seq(D_KVX), seq(D_KVX),
                  pl.BlockSpec(bias_rows.shape, lambda i: (0, 0)), seq(D_ATTN)],
        out_specs=seq(D_ATTN),
        out_shape=jax.ShapeDtypeStruct((b, s, D_ATTN), _BF16),
        scratch_shapes=[pltpu.VMEM((ATTN_HEADS, BLOCK, 3 * BLOCK), _F32)],
        compiler_params=pltpu.CompilerParams(
            dimension_semantics=("arbitrary",), vmem_limit_bytes=VMEM_LIMIT_BYTES),
        name="attn",
    )(sink, q, k, v, bias_rows, za)


def _t5_bucket_np(rel):
    half = NUM_BUCKETS // 2
    ret = (rel > 0).astype(np.int64) * half
    n = np.abs(rel)
    max_exact = half // 2
    nf = np.maximum(n, 1).astype(np.float64)
    large = max_exact + (np.log(nf / max_exact) / math.log(MAX_DISTANCE / max_exact)
                         * (half - max_exact)).astype(np.int64)
    large = np.minimum(large, half - 1)
    return ret + np.where(n < max_exact, n, large)


def _attn_bias(rel_table):
    rel = (np.arange(BIAS_ROW) + BLOCK) % BIAS_ROW - 2 * BLOCK
    onehot = (_t5_bucket_np(rel)[None] == np.arange(NUM_BUCKETS)[:, None])
    return jnp.einsum('bh,br->hr', LOG2_E * rel_table.astype(_F32),
                      jnp.asarray(onehot, _F32), precision=_HI)


def _merge_kernel(x_ref, gain_ref, wg_ref, bg_ref, ao_ref, yg_ref, zs_ref, wglu_f32_ref,
                  bglu_ref, wba_f32_ref, wbs_f32_ref, wout_f32_ref, fgain_ref, o_ref,
                  wglu_ref, wba_ref, wbs_ref, wout_ref, *y_scrs):
    nb, nt = x_ref.shape[0], x_ref.shape[1]
    rows = nb * SUB_STEPS

    @pl.when(pl.program_id(0) == 0)
    def _():
        for dst, src, scale in ((wglu_ref, wglu_f32_ref, 0.25), (wba_ref, wba_f32_ref, 1.0),
                                (wbs_ref, wbs_f32_ref, 0.25), (wout_ref, wout_f32_ref, 0.5)):
            dst[...] = (scale * src[...]).astype(_BF16)

    for st in range(nt // SUB_STEPS):
        steps = slice(st * SUB_STEPS, (st + 1) * SUB_STEPS)
        x = x_ref[:, steps, :].reshape(rows, D_MODEL)
        h = _rms_normalize(x, 0.5 * gain_ref[...]).astype(_BF16)
        gates2 = _twice_sigmoid_of_double(
            jnp.dot(h, wg_ref[:, W_OFF_GATES - GATE_BLOCK:], preferred_element_type=_F32)
            + 0.5 * bg_ref[...])
        for c in range(SUB_STEPS // CHUNK):
            _from_group_major(yg_ref, y_scrs[st], st * (SUB_STEPS // CHUNK) + c, nb, c * CHUNK,
                              SUB_STEPS)
        y = jnp.concatenate([y_scrs[st][v] for v in range(D_SSM // LANES)], axis=1)
        y2 = _twice_gelu(y)
        glu_half = (jnp.dot(y2.astype(_BF16), wglu_ref[...], preferred_element_type=_F32)
                    + 0.5 * bglu_ref[...])
        ssm4 = (y2 * _twice_sigmoid_of_double(glu_half)
                * zs_ref[:, steps, :].reshape(rows, D_SSM).astype(_F32))
        pa = jnp.dot(ao_ref[:, steps, :].reshape(rows, D_ATTN), wba_ref[...],
                     preferred_element_type=_F32)
        ps = jnp.dot(ssm4.astype(_BF16), wbs_ref[...], preferred_element_type=_F32)
        merged2 = gates2[:, :D_MODEL] * pa + gates2[:, D_MODEL:] * ps
        xn = x + jnp.dot(merged2.astype(_BF16), wout_ref[...], preferred_element_type=_F32)
        o_ref[:, steps, :] = _rms_normalize(xn, fgain_ref[...]).reshape(nb, SUB_STEPS, D_MODEL)


def _merge(x, gain, wg, bg, ao, yg, zs, wglu, bglu, wba, wbs, wout, fgain):
    bsz, s, _ = x.shape
    nt = TOKEN_TILE // bsz
    tok = lambda i: (0, i, 0)
    fixed = lambda i: (0, 0)
    full = lambda a: pl.BlockSpec(a.shape, fixed)
    once = lambda a: pl.BlockSpec(a.shape, fixed, pipeline_mode=pl.Buffered(1))
    act = lambda n: pl.BlockSpec((bsz, nt, n), tok)
    return pl.pallas_call(
        _merge_kernel,
        grid=(s // nt,),
        in_specs=[act(D_MODEL), full(gain),
                  pl.BlockSpec((D_MODEL, GATE_BLOCK), lambda i: (0, 1)),
                  full(bg), act(D_ATTN),
                  pl.BlockSpec((N_SSM_GROUPS, nt // CHUNK, bsz, CHUNK_W),
                               lambda i: (0, i, 0, 0)),
                  act(D_SSM), once(wglu), full(bglu), once(wba), once(wbs), once(wout),
                  full(fgain)],
        out_specs=act(D_MODEL),
        out_shape=jax.ShapeDtypeStruct((bsz, s, D_MODEL), _F32),
        scratch_shapes=[pltpu.VMEM(w.shape, _BF16) for w in (wglu, wba, wbs, wout)]
                       + [pltpu.VMEM((D_SSM // LANES, bsz * SUB_STEPS, LANES), _F32)
                          for _ in range(nt // SUB_STEPS)],
        compiler_params=pltpu.CompilerParams(
            dimension_semantics=("arbitrary",), vmem_limit_bytes=VMEM_LIMIT_BYTES),
        name="merge",
    )(x, gain, wg, bg, ao, yg, zs, wglu, bglu, wba, wbs, wout, fgain)


def _layer(x, norm_gain, w_in, b_gate, attn_sink, a_re, a_im, log_dt, b_re, b_im, c_re, c_im,
           d_skip, w_glu, b_glu, w_ba, w_bs, w_out, out_gain, bias):
    bsz, s, d = x.shape
    assert d == D_MODEL and x.dtype == _F32 and w_in.shape == (D_MODEL, W_OFF_GATES + 2 * D_MODEL)
    assert bsz % 8 == 0 and s % BLOCK == 0
    assert INPROJ_TOKEN_TILE % (bsz * SUB_STEPS) == 0 and s % (INPROJ_TOKEN_TILE // bsz) == 0
    assert TOKEN_TILE % (bsz * SUB_STEPS) == 0 and s % (TOKEN_TILE // bsz) == 0
    n_chunks = s // CHUNK
    gain = norm_gain.reshape(1, D_MODEL).astype(_F32)
    w_bf = w_in.astype(_BF16)
    q, k, v, za, ug, zs = _inproj(x, gain, w_bf)

    tmat, smat, cmat, ar, ai = _ssm_matrices(a_re, a_im, log_dt, b_re, b_im, c_re, c_im,
                                             d_skip)
    yg = _ssm(ug.reshape(N_SSM_GROUPS, n_chunks * bsz, CHUNK_W), tmat, smat, cmat, ar, ai,
              n_chunks, bsz)
    yg = yg.reshape(N_SSM_GROUPS, n_chunks, bsz, CHUNK_W)

    ao = _attention(attn_sink.astype(_F32), q, k, v, bias, za)

    return _merge(x, gain, w_bf, b_gate.reshape(1, -1).astype(_F32), ao, yg, zs, w_glu,
                  b_glu.reshape(1, -1).astype(_F32), w_ba, w_bs, w_out,
                  out_gain.reshape(1, D_MODEL).astype(_F32))


def kernel(x, norm_gain, w_in, b_gate, attn_sink, rel_bias_table, ssm_a_re, ssm_a_im,
           ssm_log_dt, ssm_b_re, ssm_b_im, ssm_c_re, ssm_c_im, ssm_d, w_glu, b_glu,
           w_branch_attn, w_branch_ssm, w_out, final_norm_gain):
    depth = norm_gain.shape[0]
    assert depth == 1, "final norm is fused into the single layer's epilogue"
    bias = _attn_bias(rel_bias_table)
    l = 0
    return _layer(x, norm_gain[l], w_in[l], b_gate[l], attn_sink[l], ssm_a_re[l],
                  ssm_a_im[l], ssm_log_dt[l], ssm_b_re[l], ssm_b_im[l], ssm_c_re[l],
                  ssm_c_im[l], ssm_d[l], w_glu[l], b_glu[l], w_branch_attn[l],
                  w_branch_ssm[l], w_out[l], final_norm_gain, bias)
```

```python
import functools
import math

import jax
import jax.numpy as jnp
import numpy as np
from jax import lax
from jax.experimental import pallas as pl
from jax.experimental.pallas import tpu as pltpu

D_MODEL = 1024
ATTN_HEADS = 8
KV_HEADS = 2
Q_PER_KV = ATTN_HEADS // KV_HEADS
HEAD_DIM = 64
D_ATTN = ATTN_HEADS * HEAD_DIM
D_KV = KV_HEADS * HEAD_DIM
WINDOW = 128
BLOCK = 128
NUM_BUCKETS = 32
MAX_DISTANCE = 128
D_SSM = 512
SSM_GROUP = 16
N_SSM_GROUPS = D_SSM // SSM_GROUP
SSM_STATE = 64
EPS = 1e-6
NEG_INF = -1e30

LANES = 128
SUBLANES = 8
MXU_WIDTH = 256
CHUNK = MXU_WIDTH // SSM_GROUP
CHUNK_W = CHUNK * SSM_GROUP
STATE_W = 4 * SSM_STATE
SLOTS_PER_VREG = LANES // SSM_GROUP
SSM_GROUPS_PER_STEP = 4

HEAD_PAIR = LANES // HEAD_DIM
PAIRS_PER_KV = Q_PER_KV // HEAD_PAIR
assert D_KV == LANES and KV_HEADS == HEAD_PAIR
D_KVX = 2 * D_KV
LOG2_E = math.log2(math.e)
Q_SCALE = HEAD_DIM ** -0.5 * LOG2_E
BIAS_ROW = 4 * BLOCK

W_OFF_ZA = D_ATTN + 2 * D_KV
W_OFF_U = W_OFF_ZA + D_ATTN
W_OFF_GATES = W_OFF_U + 2 * D_SSM
GATE_BLOCK = (W_OFF_GATES + 2 * D_MODEL) // 2
assert GATE_BLOCK % LANES == 0 and GATE_BLOCK <= W_OFF_GATES

TOKEN_TILE = 1024
INPROJ_TOKEN_TILE = 2048
SUB_STEPS = CHUNK
VMEM_LIMIT_BYTES = 56 * 1024 * 1024

_F32 = jnp.float32
_BF16 = jnp.bfloat16
_HI = lax.Precision.HIGHEST


def _silu(x):
    return x * (0.5 * jnp.tanh(0.5 * x) + 0.5)


def _twice_sigmoid_of_double(xh):
    return 1.0 + jnp.tanh(xh)


def _twice_gelu(x):
    c = math.sqrt(2.0 / math.pi)
    return x * (1.0 + jnp.tanh(c * (x + 0.044715 * (x * x * x))))


def _rms_normalize(x, gain):
    ms = jnp.mean(x * x, axis=-1, keepdims=True)
    return x * lax.rsqrt(ms + EPS) * gain


def _slot_masks(rows):
    lane = lax.broadcasted_iota(jnp.int32, (rows, LANES), 1)
    return [(lane >= p * SSM_GROUP) & (lane < (p + 1) * SSM_GROUP)
            for p in range(SLOTS_PER_VREG)]


def _to_group_major(u_scr, ug_ref, kk, nb, t0, steps):
    masks = _slot_masks(nb)
    for v in range(D_SSM // LANES):
        for half in range(CHUNK // SLOTS_PER_VREG):
            rolled = []
            for t8 in range(SLOTS_PER_VREG):
                r = t0 + half * SLOTS_PER_VREG + t8
                piece = u_scr[v, pl.ds(r, nb, stride=steps), :]
                rolled.append(pltpu.roll(piece, t8 * SSM_GROUP, 1) if t8 else piece)
            for p0 in range(SLOTS_PER_VREG):
                acc = rolled[0]
                for t8 in range(1, SLOTS_PER_VREG):
                    acc = jnp.where(masks[(p0 + t8) % SLOTS_PER_VREG], rolled[t8], acc)
                g = v * SLOTS_PER_VREG + p0
                ug_ref[g, kk, :, half * LANES:(half + 1) * LANES] = acc.astype(_BF16)


def _from_group_major(yg_ref, y_scr, kk, nb, t0, steps):
    masks = _slot_masks(nb)
    for v in range(D_SSM // LANES):
        for half in range(CHUNK // SLOTS_PER_VREG):
            src = [yg_ref[v * SLOTS_PER_VREG + p0, kk, :,
                          half * LANES:(half + 1) * LANES].astype(_F32)
                   for p0 in range(SLOTS_PER_VREG)]
            for t8 in range(SLOTS_PER_VREG):
                acc = src[0]
                for p0 in range(1, SLOTS_PER_VREG):
                    acc = jnp.where(masks[(p0 + t8) % SLOTS_PER_VREG], src[p0], acc)
                if t8:
                    acc = pltpu.roll(acc, LANES - t8 * SSM_GROUP, 1)
                r = t0 + half * SLOTS_PER_VREG + t8
                y_scr[v, pl.ds(r, nb, stride=steps), :] = acc


def _inproj_kernel(x_ref, gain_ref, w_f32_ref, q_ref, k_ref, v_ref, za_ref, ug_ref, zs_ref,
                   w_ref, *u_scrs):
    nb, nt = x_ref.shape[0], x_ref.shape[1]
    rows = nb * SUB_STEPS

    @pl.when(pl.program_id(0) == 0)
    def _():
        w_ref[...] = w_f32_ref[...].astype(_BF16)

    for st in range(nt // SUB_STEPS):
        steps = slice(st * SUB_STEPS, (st + 1) * SUB_STEPS)
        x = x_ref[:, steps, :].reshape(rows, D_MODEL)
        h = _rms_normalize(x, gain_ref[...]).astype(_BF16)

        def put(ref, val):
            ref[:, steps, :] = val.astype(_BF16).reshape(nb, SUB_STEPS, val.shape[-1])

        proj = lambda lo, n: jnp.dot(h, w_ref[:, lo:lo + n], preferred_element_type=_F32)
        u_zs = proj(W_OFF_U, 2 * D_SSM)
        for v in range(D_SSM // LANES):
            u_scrs[st][v] = u_zs[:, v * LANES:(v + 1) * LANES]
        for c in range(SUB_STEPS // CHUNK):
            _to_group_major(u_scrs[st], ug_ref, st * (SUB_STEPS // CHUNK) + c, nb, c * CHUNK,
                            SUB_STEPS)
        put(zs_ref, _silu(u_zs[:, D_SSM:]))
        put(za_ref, _silu(proj(W_OFF_ZA, D_ATTN)))
        qkv = proj(0, W_OFF_ZA)
        put(q_ref, qkv[:, :D_ATTN] * Q_SCALE)
        with_swap = lambda a: jnp.concatenate([a, pltpu.roll(a, HEAD_DIM, 1)], axis=1)
        put(k_ref, with_swap(qkv[:, D_ATTN:D_ATTN + D_KV]))
        put(v_ref, with_swap(qkv[:, D_ATTN + D_KV:]))


def _inproj(x, gain, w_in):
    bsz, s, _ = x.shape
    nt = INPROJ_TOKEN_TILE // bsz
    tok = lambda i: (0, i, 0)
    fixed = lambda i: (0, 0)
    outs = [D_ATTN, D_KVX, D_KVX, D_ATTN, None, D_SSM]
    act = lambda n: (pl.BlockSpec((bsz, nt, n), tok), jax.ShapeDtypeStruct((bsz, s, n), _BF16))
    ug = (pl.BlockSpec((N_SSM_GROUPS, nt // CHUNK, bsz, CHUNK_W), lambda i: (0, i, 0, 0)),
          jax.ShapeDtypeStruct((N_SSM_GROUPS, s // CHUNK, bsz, CHUNK_W), _BF16))
    specs, shapes = zip(*[ug if n is None else act(n) for n in outs])
    return pl.pallas_call(
        _inproj_kernel,
        grid=(s // nt,),
        in_specs=[pl.BlockSpec((bsz, nt, D_MODEL), tok),
                  pl.BlockSpec((1, D_MODEL), fixed)]
                 + [pl.BlockSpec((D_MODEL, W_OFF_GATES), fixed,
                                 pipeline_mode=pl.Buffered(1))],
        out_specs=list(specs),
        out_shape=list(shapes),
        scratch_shapes=[pltpu.VMEM((D_MODEL, W_OFF_GATES), _BF16)]
                       + [pltpu.VMEM((D_SSM // LANES, bsz * SUB_STEPS, LANES), _F32)
                          for _ in range(nt // SUB_STEPS)],
        compiler_params=pltpu.CompilerParams(
            dimension_semantics=("arbitrary",), vmem_limit_bytes=VMEM_LIMIT_BYTES),
        name="inproj",
    )(x, gain, w_in)


def _ssm_kernel(n_chunks, rows_per_chunk, u_ref, t_ref, s_ref, c_ref, ar_ref, ai_ref,
                y_ref, s_scr, x_scr):
    rb = rows_per_chunk
    groups = range(u_ref.shape[0])
    for gi in groups:
        s_scr[gi] = jnp.dot(u_ref[gi], s_ref[gi], preferred_element_type=_F32)
    ar = [jnp.broadcast_to(ar_ref[gi], (rb, 2 * SSM_STATE)) for gi in groups]
    ai = [jnp.broadcast_to(ai_ref[gi], (rb, 2 * SSM_STATE)) for gi in groups]
    is_fwd = lax.broadcasted_iota(jnp.int32, (rb, 2 * SSM_STATE), 1) < SSM_STATE
    is_bwd = jnp.logical_not(is_fwd)
    re_cols = pl.ds(0, 2 * SSM_STATE)
    im_cols = pl.ds(2 * SSM_STATE, 2 * SSM_STATE)

    def step(i, carry):
        rows_f = pl.ds(pl.multiple_of(i * rb, rb), rb)
        rows_b = pl.ds(pl.multiple_of((n_chunks - 1 - i) * rb, rb), rb)
        new = []
        for gi in groups:
            st_re, st_im = carry[2 * gi], carry[2 * gi + 1]
            pltpu.store(x_scr.at[gi, rows_f, re_cols], st_re, mask=is_fwd)
            pltpu.store(x_scr.at[gi, rows_f, im_cols], st_im, mask=is_fwd)
            pltpu.store(x_scr.at[gi, rows_b, re_cols], st_re, mask=is_bwd)
            pltpu.store(x_scr.at[gi, rows_b, im_cols], st_im, mask=is_bwd)
            in_re = jnp.where(is_fwd, s_scr[gi, rows_f, re_cols], s_scr[gi, rows_b, re_cols])
            in_im = jnp.where(is_fwd, s_scr[gi, rows_f, im_cols], s_scr[gi, rows_b, im_cols])
            new.append(ar[gi] * st_re - ai[gi] * st_im + in_re)
            new.append(ar[gi] * st_im + ai[gi] * st_re + in_im)
        return tuple(new)

    zero = jnp.zeros((rb, 2 * SSM_STATE), _F32)
    lax.fori_loop(0, n_chunks, step, (zero,) * (2 * len(groups)), unroll=True)

    for gi in groups:
        y = jnp.dot(u_ref[gi], t_ref[gi], preferred_element_type=_F32)
        y = y + lax.dot_general(x_scr[gi].astype(_BF16), c_ref[gi], (((1,), (1,)), ((), ())),
                                preferred_element_type=_F32)
        y_ref[gi] = y.astype(y_ref.dtype)


def _ssm(ug, tmat, smat, cmat, ar, ai, n_chunks, rows_per_chunk):
    g, rows, _ = ug.shape
    gb = SSM_GROUPS_PER_STEP
    per_group = lambda i: (i, 0, 0)
    return pl.pallas_call(
        functools.partial(_ssm_kernel, n_chunks, rows_per_chunk),
        grid=(g // gb,),
        in_specs=[pl.BlockSpec((gb, rows, CHUNK_W), per_group),
                  pl.BlockSpec((gb, CHUNK_W, CHUNK_W), per_group),
                  pl.BlockSpec((gb, CHUNK_W, STATE_W), per_group),
                  pl.BlockSpec((gb, CHUNK_W, STATE_W), per_group),
                  pl.BlockSpec((gb, 1, 2 * SSM_STATE), per_group),
                  pl.BlockSpec((gb, 1, 2 * SSM_STATE), per_group)],
        out_specs=pl.BlockSpec((gb, rows, CHUNK_W), per_group),
        out_shape=jax.ShapeDtypeStruct((g, rows, CHUNK_W), _BF16),
        scratch_shapes=[pltpu.VMEM((gb, rows, STATE_W), _F32),
                        pltpu.VMEM((gb, rows, STATE_W), _F32)],
        compiler_params=pltpu.CompilerParams(
            dimension_semantics=("arbitrary",), vmem_limit_bytes=VMEM_LIMIT_BYTES),
        name="ssm",
    )(ug, tmat, smat, cmat, ar, ai)


def _ssm_matrices(a_re, a_im, log_dt, b_re, b_im, c_re, c_im, d_skip):
    L, G, P, C = CHUNK, N_SSM_GROUPS, SSM_STATE, SSM_GROUP
    bt = jnp.stack([b_re, b_im], axis=0).transpose(2, 0, 4, 1, 3).reshape(G, 2, C, 2 * P)
    dvec = jnp.tile(d_skip.reshape(G, 1, C), (1, 1, L))
    gb = SLOTS_PER_VREG
    per_group = lambda *blk: pl.BlockSpec((gb,) + blk, lambda i: (i,) + (0,) * len(blk))
    per_dir = lambda *blk: pl.BlockSpec((2, gb) + blk, lambda i: (0, i) + (0,) * len(blk))
    mat = jax.ShapeDtypeStruct((G, L * C, L * C), _BF16)
    vec = jax.ShapeDtypeStruct((G, 1, 2 * P), _F32)
    return pl.pallas_call(
        _ssm_prep_kernel,
        grid=(G // gb,),
        in_specs=[pl.BlockSpec(memory_space=pltpu.SMEM), per_dir(P), per_dir(P),
                  per_group(2, C, 2 * P), per_dir(C, P), per_dir(C, P), per_group(1, L * C)],
        out_specs=[per_group(L * C, L * C), per_group(L * C, STATE_W),
                   per_group(L * C, STATE_W), per_group(1, 2 * P), per_group(1, 2 * P)],
        out_shape=[mat, mat, mat, vec, vec],
        compiler_params=pltpu.CompilerParams(dimension_semantics=("arbitrary",)),
        name="ssm_prep",
    )(log_dt, a_re, a_im, bt, c_re, c_im, dvec)


def _ssm_prep_kernel(ldt_ref, are_ref, aim_ref, bt_ref, cre_ref, cim_ref, dvec_ref,
                     t_ref, s_ref, c_ref, ar_ref, ai_ref):
    for gi in range(bt_ref.shape[0]):
        _ssm_prep_group(gi, ldt_ref, are_ref, aim_ref, bt_ref, cre_ref, cim_ref, dvec_ref,
                        t_ref, s_ref, c_ref, ar_ref, ai_ref)


def _ssm_prep_group(gi, ldt_ref, are_ref, aim_ref, bt_ref, cre_ref, cim_ref, dvec_ref,
                    t_ref, s_ref, c_ref, ar_ref, ai_ref):
    L, P, C = CHUNK, SSM_STATE, SSM_GROUP
    g8 = gi % SLOTS_PER_VREG
    g = pl.program_id(0) * bt_ref.shape[0] + gi
    both = lambda ref, rows: jnp.concatenate([ref[0, rows], ref[1, rows]], axis=-1)

    def time_of_slot(slot):
        return (jnp.bitwise_and(slot, -SLOTS_PER_VREG)
                + jnp.bitwise_and(slot - g8, SLOTS_PER_VREG - 1))

    def cmul(x_re, x_im, y_re, y_im):
        return x_re * y_re - x_im * y_im, x_re * y_im + x_im * y_re

    a_re, a_im = both(are_ref, pl.ds(gi, 1)), both(aim_ref, pl.ds(gi, 1))
    dt = jnp.exp(jnp.where(lax.broadcasted_iota(jnp.int32, (1, 2 * P), 1) < P,
                           ldt_ref[0, g], ldt_ref[1, g]))
    lr, li = a_re * dt, a_im * dt

    j = time_of_slot(lax.broadcasted_iota(jnp.int32, (L, 2 * P), 0)).astype(_F32)
    pos = lax.broadcasted_iota(jnp.int32, (L, 2 * P), 0).astype(_F32)
    fwd = lax.broadcasted_iota(jnp.int32, (L, 2 * P), 1) < P
    cj, sj = jnp.cos(li * j), jnp.sin(li * j)
    slot_of = lambda t: (t // SLOTS_PER_VREG) * SLOTS_PER_VREG + (t + g8) % SLOTS_PER_VREG
    in_time_order = lambda a: jnp.concatenate(
        [a[slot_of(t):slot_of(t) + 1] for t in range(L)], axis=0)
    cp, sp = in_time_order(cj), in_time_order(sj)
    krow = lax.broadcasted_iota(jnp.int32, (SUBLANES, 2 * P), 0)
    consts = jnp.where(krow == 0, 1.0, jnp.where(krow == 1, L - 1.0,
                                                 jnp.where(krow == 2, float(L), 0.0)))
    ck, sk = jnp.cos(li * consts), jnp.sin(li * consts)
    (c_1, s_1), (c_m, s_m), (c_l, s_l) = [(ck[r:r + 1], sk[r:r + 1]) for r in range(3)]

    def cpow(tau, trig):
        mag = jnp.exp(lr * tau)
        return mag * trig[0], mag * trig[1]

    minus = lambda c0, s0, c, s: (c0 * c + s0 * s, s0 * c - c0 * s)
    plus = lambda c0, s0, c, s: (c0 * c - s0 * s, s0 * c + c0 * s)
    per_dir = lambda f, b: (jnp.where(fwd, f[0], b[0]), jnp.where(fwd, f[1], b[1]))

    ab_re, ab_im = cpow(jnp.ones_like(lr), (c_1, s_1))
    den = a_re * a_re + a_im * a_im
    co_re = ((ab_re - 1.0) * a_re + ab_im * a_im) / den
    co_im = (ab_im * a_re - (ab_re - 1.0) * a_im) / den
    bb_re, bb_im = cmul(bt_ref[gi, 0], bt_ref[gi, 1], co_re, co_im)
    cc_re, cc_im = both(cre_ref, gi), both(cim_ref, gi)

    def outer(p, m):
        (p_re, p_im), (m_re, m_im) = p, m
        blocks = [cmul(p_re[s:s + 1], p_im[s:s + 1], m_re, m_im) for s in range(L)]
        return (jnp.concatenate([b[0] for b in blocks], axis=0),
                jnp.concatenate([b[1] for b in blocks], axis=0))

    s_re, s_im = outer(cpow(jnp.where(fwd, L - 1 - j, j),
                            per_dir(minus(c_m, s_m, cj, sj), (cj, sj))), (bb_re, bb_im))
    s_ref[gi] = jnp.concatenate([s_re, s_im], axis=1).astype(_BF16)
    w_re, w_im = outer(cpow(jnp.where(fwd, j + 1, L - j),
                            per_dir(plus(c_1, s_1, cj, sj), minus(c_l, s_l, cj, sj))),
                       (cc_re, cc_im))
    wcat = jnp.concatenate([w_re, -w_im], axis=1)
    c_ref[gi] = wcat.astype(_BF16)
    a_re2, a_im2 = outer(cpow(jnp.where(fwd, pos, L - 1 - pos),
                              per_dir((cp, sp), minus(c_m, s_m, cp, sp))), (cc_re, cc_im))
    acat = jnp.concatenate([a_re2, -a_im2], axis=1)
    bcat = jnp.concatenate([bb_re, bb_im], axis=1)
    is_fwd_state = jnp.bitwise_and(lax.broadcasted_iota(jnp.int32, bcat.shape, 1), P) == 0

    def split(a):
        hi = a.astype(_BF16).astype(_F32)
        return hi, a - hi

    def dot_state(a_parts, b_parts):
        (a_hi, a_lo), (b_hi, b_lo) = [[p.astype(_BF16) for p in parts]
                                      for parts in (a_parts, b_parts)]
        d = lambda a, b: lax.dot_general(a, b, (((1,), (1,)), ((), ())),
                                         preferred_element_type=_F32)
        return d(a_hi, b_hi) + d(a_hi, b_lo) + d(a_lo, b_hi)

    b_parts, a_parts = split(bcat), split(acat)
    kf = dot_state([jnp.where(is_fwd_state, p, 0.0) for p in b_parts], a_parts)
    kb = dot_state([jnp.where(is_fwd_state, 0.0, p) for p in b_parts], a_parts)

    lane = lax.broadcasted_iota(jnp.int32, (C, LANES), 1)

    def shift_slots(k, n):
        halves = [k[:, :LANES], k[:, LANES:]]
        if n >= SLOTS_PER_VREG:
            halves, n = halves[::-1], n - SLOTS_PER_VREG
        if n == 0:
            return halves
        r = [pltpu.roll(h, n * C, 1) for h in halves]
        wrapped = lane < n * C
        return [jnp.where(wrapped, r[1], r[0]), jnp.where(wrapped, r[0], r[1])]

    col_t = lax.broadcasted_iota(jnp.int32, (C, L * C), 1)
    blocks = []
    for jt in range(L):
        fwd_part = jnp.concatenate(shift_slots(kf, jt), axis=1)
        bwd_part = jnp.concatenate(shift_slots(kb, (jt + 1) % L), axis=1)
        blk = (jnp.where(col_t >= jt * C, fwd_part, 0.0)
               + jnp.where(col_t < (jt + 1) * C, bwd_part, 0.0))
        blocks.append(jnp.concatenate(
            [pltpu.roll(blk[:, h * LANES:(h + 1) * LANES], g8 * C, 1) if g8 else
             blk[:, h * LANES:(h + 1) * LANES] for h in range(L * C // LANES)], axis=1))
    time_of = lambda s: (s // SLOTS_PER_VREG) * SLOTS_PER_VREG + (s - g8) % SLOTS_PER_VREG
    tm = jnp.concatenate([blocks[time_of(s)] for s in range(L)], axis=0)
    row = lax.broadcasted_iota(jnp.int32, (L * C, L * C), 0)
    col = lax.broadcasted_iota(jnp.int32, (L * C, L * C), 1)
    t_ref[gi] = (tm + jnp.where(row == col, dvec_ref[gi], 0.0)).astype(_BF16)
    al_re, al_im = cpow(jnp.full_like(lr, float(L)), (c_l, s_l))
    ar_ref[gi] = al_re
    ai_ref[gi] = al_im


def _attn_kernel(sink_ref, q_ref, k_ref, v_ref, bias_rows_ref, za_ref, o_ref, bias_ref):
    nq = q_ref.shape[1] // BLOCK

    @pl.when(pl.program_id(0) == 0)
    def _():
        qi = lax.broadcasted_iota(jnp.int32, (BLOCK, 3 * BLOCK), 0)
        ki = lax.broadcasted_iota(jnp.int32, (BLOCK, 3 * BLOCK), 1)
        in_band = jnp.abs(ki - BLOCK - qi) <= WINDOW
        for h in range(ATTN_HEADS):
            row = jnp.broadcast_to(bias_rows_ref[h:h + 1, :], (BLOCK, BIAS_ROW))
            rel = pltpu.roll(row, 0, 1, stride=1, stride_axis=0)
            bias_ref[h] = jnp.where(in_band, rel[:, :3 * BLOCK], NEG_INF)

    rows = lambda ref, c: ref[0, c * BLOCK:(c + 1) * BLOCK]
    k_blocks = [rows(k_ref, c) for c in range(nq)]
    v_blocks = [rows(v_ref, c) for c in range(nq)]
    lane = lax.broadcasted_iota(jnp.int32, (BLOCK, LANES), 1)
    halves = [lane < HEAD_DIM, lane >= HEAD_DIM]
    keep = [h.astype(_F32).astype(_BF16) for h in halves]
    slab = lambda blk, j, e: blk[:, (j ^ e) * LANES:((j ^ e) + 1) * LANES]
    k_half = {(j, e): [slab(kb, j, e) * keep[e] for kb in k_blocks]
              for j in range(KV_HEADS) for e in range(HEAD_PAIR)}
    v_half = {(j, e): [jnp.concatenate([slab(vb, j, e) * keep[e], keep[e]], axis=1)
                       for vb in v_blocks]
              for j in range(KV_HEADS) for e in range(HEAD_PAIR)}
    for c in range(nq):
        blocks = [b for b in (c - 1, c, c + 1) if 0 <= b < nq]
        slot0 = blocks[0] - (c - 1)
        nk = len(blocks) * BLOCK
        q = rows(q_ref, c)
        slabs_out = []
        for j in range(KV_HEADS):
            window = lambda halves_of: jnp.concatenate(
                [halves_of[j, e][b] for e in range(HEAD_PAIR) for b in blocks], axis=0)
            kcat = window(k_half)
            vcat = window(v_half)
            qs = jnp.concatenate(
                [q[:, (j * PAIRS_PER_KV + i) * LANES:(j * PAIRS_PER_KV + i + 1) * LANES]
                 for i in range(PAIRS_PER_KV)], axis=0)
            s = lax.dot_general(qs, kcat, (((1,), (1,)), ((), ())),
                                preferred_element_type=_F32)
            p_rows, e_rows = [], []
            for i in range(PAIRS_PER_KV):
                p_lanes, e_sink = [], []
                for e in range(HEAD_PAIR):
                    head = j * Q_PER_KV + i * HEAD_PAIR + e
                    sg = (s[i * BLOCK:(i + 1) * BLOCK, e * nk:(e + 1) * nk]
                          + bias_ref[head, :, slot0 * BLOCK:slot0 * BLOCK + nk])
                    sk = LOG2_E * sink_ref[head]
                    m = jnp.maximum(jnp.max(sg, axis=-1, keepdims=True), sk)
                    p_lanes.append(jnp.exp2(sg - m).astype(_BF16))
                    e_sink.append(jnp.broadcast_to(jnp.exp2(sk - m), (BLOCK, LANES)))
                p_rows.append(jnp.concatenate(p_lanes, axis=1))
                e_rows.append(jnp.where(halves[0], e_sink[0], e_sink[1]))
            p = jnp.concatenate(p_rows, axis=0)
            o = jnp.dot(p, vcat, preferred_element_type=_F32)
            den = o[:, LANES:] + jnp.concatenate(e_rows, axis=0)
            on = o[:, :LANES] / den
            slabs_out += [on[i * BLOCK:(i + 1) * BLOCK] for i in range(PAIRS_PER_KV)]
        o_all = jnp.concatenate(slabs_out, axis=1)
        o_ref[0, c * BLOCK:(c + 1) * BLOCK, :] = (
            o_all * rows(za_ref, c).astype(_F32)).astype(o_ref.dtype)


def _attention(sink, q, k, v, bias_rows, za):
    b, s, _ = q.shape
    seq = lambda w: pl.BlockSpec((1, s, w), lambda i: (i, 0, 0))
    return pl.pallas_call(
        _attn_kernel,
        grid=(b,),
        in_specs=[pl.BlockSpec(memory_space=pltpu.SMEM), seq(D_ATTN), seq(D_KVX), seq(D_KVX),
                  pl.BlockSpec(bias_rows.shape, lambda i: (0, 0)), seq(D_ATTN)],
        out_specs=seq(D_ATTN),
        out_shape=jax.ShapeDtypeStruct((b, s, D_ATTN), _BF16),
        scratch_shapes=[pltpu.VMEM((ATTN_HEADS, BLOCK, 3 * BLOCK), _F32)],
        compiler_params=pltpu.CompilerParams(
            dimension_semantics=("arbitrary",), vmem_limit_bytes=VMEM_LIMIT_BYTES),
        name="attn",
    )(sink, q, k, v, bias_rows, za)


def _t5_bucket_np(rel):
    half = NUM_BUCKETS // 2
    ret = (rel > 0).astype(np.int64) * half
    n = np.abs(rel)
    max_exact = half // 2
    nf = np.maximum(n, 1).astype(np.float64)
    large = max_exact + (np.log(nf / max_exact) / math.log(MAX_DISTANCE / max_exact)
                         * (half - max_exact)).astype(np.int64)
    large = np.minimum(large, half - 1)
    return ret + np.where(n < max_exact, n, large)


def _attn_bias(rel_table):
    rel = (np.arange(BIAS_ROW) + BLOCK) % BIAS_ROW - 2 * BLOCK
    onehot = (_t5_bucket_np(rel)[None] == np.arange(NUM_BUCKETS)[:, None])
    return jnp.einsum('bh,br->hr', LOG2_E * rel_table.astype(_F32),
                      jnp.asarray(onehot, _F32), precision=_HI)


def _merge_kernel(x_ref, gain_ref, wg_f32_ref, bg_ref, ao_ref, yg_ref, zs_ref, wglu_f32_ref,
                  bglu_ref, wba_f32_ref, wbs_f32_ref, wout_f32_ref, fgain_ref, o_ref,
                  wg_ref, wglu_ref, wba_ref, wbs_ref, wout_ref, *y_scrs):
    nb, nt = x_ref.shape[0], x_ref.shape[1]
    rows = nb * SUB_STEPS

    @pl.when(pl.program_id(0) == 0)
    def _():
        wg_ref[...] = wg_f32_ref[:, W_OFF_GATES - GATE_BLOCK:].astype(_BF16)
        for dst, src, scale in ((wglu_ref, wglu_f32_ref, 0.25), (wba_ref, wba_f32_ref, 1.0),
                                (wbs_ref, wbs_f32_ref, 0.25), (wout_ref, wout_f32_ref, 0.5)):
            dst[...] = (scale * src[...]).astype(_BF16)

    for st in range(nt // SUB_STEPS):
        steps = slice(st * SUB_STEPS, (st + 1) * SUB_STEPS)
        x = x_ref[:, steps, :].reshape(rows, D_MODEL)
        h = _rms_normalize(x, 0.5 * gain_ref[...]).astype(_BF16)
        gates2 = _twice_sigmoid_of_double(
            jnp.dot(h, wg_ref[...], preferred_element_type=_F32)
            + 0.5 * bg_ref[...])
        for c in range(SUB_STEPS // CHUNK):
            _from_group_major(yg_ref, y_scrs[st], st * (SUB_STEPS // CHUNK) + c, nb, c * CHUNK,
                              SUB_STEPS)
        y = jnp.concatenate([y_scrs[st][v] for v in range(D_SSM // LANES)], axis=1)
        y2 = _twice_gelu(y)
        glu_half = (jnp.dot(y2.astype(_BF16), wglu_ref[...], preferred_element_type=_F32)
                    + 0.5 * bglu_ref[...])
        ssm4 = (y2 * _twice_sigmoid_of_double(glu_half)
                * zs_ref[:, steps, :].reshape(rows, D_SSM).astype(_F32))
        pa = jnp.dot(ao_ref[:, steps, :].reshape(rows, D_ATTN), wba_ref[...],
                     preferred_element_type=_F32)
        ps = jnp.dot(ssm4.astype(_BF16), wbs_ref[...], preferred_element_type=_F32)
        merged2 = gates2[:, :D_MODEL] * pa + gates2[:, D_MODEL:] * ps
        xn = x + jnp.dot(merged2.astype(_BF16), wout_ref[...], preferred_element_type=_F32)
        o_ref[:, steps, :] = _rms_normalize(xn, fgain_ref[...]).reshape(nb, SUB_STEPS, D_MODEL)


def _merge(x, gain, wg, bg, ao, yg, zs, wglu, bglu, wba, wbs, wout, fgain):
    bsz, s, _ = x.shape
    nt = TOKEN_TILE // bsz
    tok = lambda i: (0, i, 0)
    fixed = lambda i: (0, 0)
    full = lambda a: pl.BlockSpec(a.shape, fixed)
    once = lambda a: pl.BlockSpec(a.shape, fixed, pipeline_mode=pl.Buffered(1))
    act = lambda n: pl.BlockSpec((bsz, nt, n), tok)
    return pl.pallas_call(
        _merge_kernel,
        grid=(s // nt,),
        in_specs=[act(D_MODEL), full(gain),
                  pl.BlockSpec((D_MODEL, GATE_BLOCK), lambda i: (0, 1),
                               pipeline_mode=pl.Buffered(1)),
                  full(bg), act(D_ATTN),
                  pl.BlockSpec((N_SSM_GROUPS, nt // CHUNK, bsz, CHUNK_W),
                               lambda i: (0, i, 0, 0)),
                  act(D_SSM), once(wglu), full(bglu), once(wba), once(wbs), once(wout),
                  full(fgain)],
        out_specs=act(D_MODEL),
        out_shape=jax.ShapeDtypeStruct((bsz, s, D_MODEL), _F32),
        scratch_shapes=[pltpu.VMEM((D_MODEL, 2 * D_MODEL), _BF16)]
                       + [pltpu.VMEM(w.shape, _BF16) for w in (wglu, wba, wbs, wout)]
                       + [pltpu.VMEM((D_SSM // LANES, bsz * SUB_STEPS, LANES), _F32)
                          for _ in range(nt // SUB_STEPS)],
        compiler_params=pltpu.CompilerParams(
            dimension_semantics=("arbitrary",), vmem_limit_bytes=VMEM_LIMIT_BYTES),
        name="merge",
    )(x, gain, wg, bg, ao, yg, zs, wglu, bglu, wba, wbs, wout, fgain)


def _layer(x, norm_gain, w_in, b_gate, attn_sink, a_re, a_im, log_dt, b_re, b_im, c_re, c_im,
           d_skip, w_glu, b_glu, w_ba, w_bs, w_out, out_gain, bias):
    bsz, s, d = x.shape
    assert d == D_MODEL and x.dtype == _F32 and w_in.shape == (D_MODEL, W_OFF_GATES + 2 * D_MODEL)
    assert bsz % 8 == 0 and s % BLOCK == 0
    assert INPROJ_TOKEN_TILE % (bsz * SUB_STEPS) == 0 and s % (INPROJ_TOKEN_TILE // bsz) == 0
    assert TOKEN_TILE % (bsz * SUB_STEPS) == 0 and s % (TOKEN_TILE // bsz) == 0
    n_chunks = s // CHUNK
    gain = norm_gain.reshape(1, D_MODEL).astype(_F32)
    q, k, v, za, ug, zs = _inproj(x, gain, w_in)

    tmat, smat, cmat, ar, ai = _ssm_matrices(a_re, a_im, log_dt, b_re, b_im, c_re, c_im,
                                             d_skip)
    yg = _ssm(ug.reshape(N_SSM_GROUPS, n_chunks * bsz, CHUNK_W), tmat, smat, cmat, ar, ai,
              n_chunks, bsz)
    yg = yg.reshape(N_SSM_GROUPS, n_chunks, bsz, CHUNK_W)

    ao = _attention(attn_sink.astype(_F32), q, k, v, bias, za)

    return _merge(x, gain, w_in, b_gate.reshape(1, -1).astype(_F32), ao, yg, zs, w_glu,
                  b_glu.reshape(1, -1).astype(_F32), w_ba, w_bs, w_out,
                  out_gain.reshape(1, D_MODEL).astype(_F32))


def kernel(x, norm_gain, w_in, b_gate, attn_sink, rel_bias_table, ssm_a_re, ssm_a_im,
           ssm_log_dt, ssm_b_re, ssm_b_im, ssm_c_re, ssm_c_im, ssm_d, w_glu, b_glu,
           w_branch_attn, w_branch_ssm, w_out, final_norm_gain):
    depth = norm_gain.shape[0]
    assert depth == 1, "final norm is fused into the single layer's epilogue"
    bias = _attn_bias(rel_bias_table)
    l = 0
    return _layer(x, norm_gain[l], w_in[l], b_gate[l], attn_sink[l], ssm_a_re[l],
                  ssm_a_im[l], ssm_log_dt[l], ssm_b_re[l], ssm_b_im[l], ssm_c_re[l],
                  ssm_c_im[l], ssm_d[l], w_glu[l], b_glu[l], w_branch_attn[l],
                  w_branch_ssm[l], w_out[l], final_norm_gain, bias)
```

```python
import functools
import math

import jax
import jax.numpy as jnp
import numpy as np
from jax import lax
from jax.experimental import pallas as pl
from jax.experimental.pallas import tpu as pltpu

D_MODEL = 1024
ATTN_HEADS = 8
KV_HEADS = 2
Q_PER_KV = ATTN_HEADS // KV_HEADS
HEAD_DIM = 64
D_ATTN = ATTN_HEADS * HEAD_DIM
D_KV = KV_HEADS * HEAD_DIM
WINDOW = 128
BLOCK = 128
NUM_BUCKETS = 32
MAX_DISTANCE = 128
D_SSM = 512
SSM_GROUP = 16
N_SSM_GROUPS = D_SSM // SSM_GROUP
SSM_STATE = 64
EPS = 1e-6
NEG_INF = -1e30

LANES = 128
SUBLANES = 8
MXU_WIDTH = 256
CHUNK = MXU_WIDTH // SSM_GROUP
CHUNK_W = CHUNK * SSM_GROUP
STATE_W = 4 * SSM_STATE
SLOTS_PER_VREG = LANES // SSM_GROUP
SSM_GROUPS_PER_STEP = 4

HEAD_PAIR = LANES // HEAD_DIM
PAIRS_PER_KV = Q_PER_KV // HEAD_PAIR
assert D_KV == LANES and KV_HEADS == HEAD_PAIR
D_KVX = 2 * D_KV
LOG2_E = math.log2(math.e)
Q_SCALE = HEAD_DIM ** -0.5 * LOG2_E
BIAS_ROW = 4 * BLOCK

W_OFF_ZA = D_ATTN + 2 * D_KV
W_OFF_U = W_OFF_ZA + D_ATTN
W_OFF_GATES = W_OFF_U + 2 * D_SSM
GATE_BLOCK = (W_OFF_GATES + 2 * D_MODEL) // 2
assert GATE_BLOCK % LANES == 0 and GATE_BLOCK <= W_OFF_GATES

TOKEN_TILE = 1024
INPROJ_TOKEN_TILE = 2048
SUB_STEPS = CHUNK
VMEM_LIMIT_BYTES = 56 * 1024 * 1024

_F32 = jnp.float32
_BF16 = jnp.bfloat16
_HI = lax.Precision.HIGHEST


def _silu(x):
    return x * (0.5 * jnp.tanh(0.5 * x) + 0.5)


def _twice_sigmoid_of_double(xh):
    return 1.0 + jnp.tanh(xh)


def _twice_gelu(x):
    c = math.sqrt(2.0 / math.pi)
    return x * (1.0 + jnp.tanh(c * (x + 0.044715 * (x * x * x))))


def _rms_normalize(x, gain):
    ms = jnp.mean(x * x, axis=-1, keepdims=True)
    return x * lax.rsqrt(ms + EPS) * gain


def _slot_masks(rows):
    lane = lax.broadcasted_iota(jnp.int32, (rows, LANES), 1)
    return [(lane >= p * SSM_GROUP) & (lane < (p + 1) * SSM_GROUP)
            for p in range(SLOTS_PER_VREG)]


def _to_group_major(u_scr, ug_ref, kk, nb, t0, steps):
    masks = _slot_masks(nb)
    for v in range(D_SSM // LANES):
        for half in range(CHUNK // SLOTS_PER_VREG):
            rolled = []
            for t8 in range(SLOTS_PER_VREG):
                r = t0 + half * SLOTS_PER_VREG + t8
                piece = u_scr[v, pl.ds(r, nb, stride=steps), :]
                rolled.append(pltpu.roll(piece, t8 * SSM_GROUP, 1) if t8 else piece)
            for p0 in range(SLOTS_PER_VREG):
                acc = rolled[0]
                for t8 in range(1, SLOTS_PER_VREG):
                    acc = jnp.where(masks[(p0 + t8) % SLOTS_PER_VREG], rolled[t8], acc)
                g = v * SLOTS_PER_VREG + p0
                ug_ref[g, kk, :, half * LANES:(half + 1) * LANES] = acc.astype(_BF16)


def _from_group_major(yg_ref, y_scr, kk, nb, t0, steps):
    masks = _slot_masks(nb)
    for v in range(D_SSM // LANES):
        for half in range(CHUNK // SLOTS_PER_VREG):
            src = [yg_ref[v * SLOTS_PER_VREG + p0, kk, :,
                          half * LANES:(half + 1) * LANES].astype(_F32)
                   for p0 in range(SLOTS_PER_VREG)]
            for t8 in range(SLOTS_PER_VREG):
                acc = src[0]
                for p0 in range(1, SLOTS_PER_VREG):
                    acc = jnp.where(masks[(p0 + t8) % SLOTS_PER_VREG], src[p0], acc)
                if t8:
                    acc = pltpu.roll(acc, LANES - t8 * SSM_GROUP, 1)
                r = t0 + half * SLOTS_PER_VREG + t8
                y_scr[v, pl.ds(r, nb, stride=steps), :] = acc


def _inproj_kernel(x_ref, gain_ref, w_ref, q_ref, k_ref, v_ref, za_ref, ug_ref, zs_ref,
                   *u_scrs):
    nb, nt = x_ref.shape[0], x_ref.shape[1]
    rows = nb * SUB_STEPS
    for st in range(nt // SUB_STEPS):
        steps = slice(st * SUB_STEPS, (st + 1) * SUB_STEPS)
        x = x_ref[:, steps, :].reshape(rows, D_MODEL)
        h = _rms_normalize(x, gain_ref[...]).astype(_BF16)

        def put(ref, val):
            ref[:, steps, :] = val.astype(_BF16).reshape(nb, SUB_STEPS, val.shape[-1])

        proj = lambda lo, n: jnp.dot(h, w_ref[:, lo:lo + n], preferred_element_type=_F32)
        u_zs = proj(W_OFF_U, 2 * D_SSM)
        for v in range(D_SSM // LANES):
            u_scrs[st][v] = u_zs[:, v * LANES:(v + 1) * LANES]
        for c in range(SUB_STEPS // CHUNK):
            _to_group_major(u_scrs[st], ug_ref, st * (SUB_STEPS // CHUNK) + c, nb, c * CHUNK,
                            SUB_STEPS)
        put(zs_ref, _silu(u_zs[:, D_SSM:]))
        put(za_ref, _silu(proj(W_OFF_ZA, D_ATTN)))
        qkv = proj(0, W_OFF_ZA)
        put(q_ref, qkv[:, :D_ATTN] * Q_SCALE)
        with_swap = lambda a: jnp.concatenate([a, pltpu.roll(a, HEAD_DIM, 1)], axis=1)
        put(k_ref, with_swap(qkv[:, D_ATTN:D_ATTN + D_KV]))
        put(v_ref, with_swap(qkv[:, D_ATTN + D_KV:]))


def _inproj(x, gain, w_in):
    bsz, s, _ = x.shape
    nt = INPROJ_TOKEN_TILE // bsz
    tok = lambda i: (0, i, 0)
    fixed = lambda i: (0, 0)
    outs = [D_ATTN, D_KVX, D_KVX, D_ATTN, None, D_SSM]
    act = lambda n: (pl.BlockSpec((bsz, nt, n), tok), jax.ShapeDtypeStruct((bsz, s, n), _BF16))
    ug = (pl.BlockSpec((N_SSM_GROUPS, nt // CHUNK, bsz, CHUNK_W), lambda i: (0, i, 0, 0)),
          jax.ShapeDtypeStruct((N_SSM_GROUPS, s // CHUNK, bsz, CHUNK_W), _BF16))
    specs, shapes = zip(*[ug if n is None else act(n) for n in outs])
    return pl.pallas_call(
        _inproj_kernel,
        grid=(s // nt,),
        in_specs=[pl.BlockSpec((bsz, nt, D_MODEL), tok),
                  pl.BlockSpec((1, D_MODEL), fixed)]
                 + [pl.BlockSpec((D_MODEL, W_OFF_GATES), fixed)],
        out_specs=list(specs),
        out_shape=list(shapes),
        scratch_shapes=[pltpu.VMEM((D_SSM // LANES, bsz * SUB_STEPS, LANES), _F32)
                        for _ in range(nt // SUB_STEPS)],
        compiler_params=pltpu.CompilerParams(
            dimension_semantics=("arbitrary",), vmem_limit_bytes=VMEM_LIMIT_BYTES),
        name="inproj",
    )(x, gain, w_in)


def _ssm_kernel(n_chunks, rows_per_chunk, u_ref, t_ref, s_ref, c_ref, ar_ref, ai_ref,
                y_ref, s_scr, x_scr):
    rb = rows_per_chunk
    groups = range(u_ref.shape[0])
    for gi in groups:
        s_scr[gi] = jnp.dot(u_ref[gi], s_ref[gi], preferred_element_type=_F32)
    ar = [jnp.broadcast_to(ar_ref[gi], (rb, 2 * SSM_STATE)) for gi in groups]
    ai = [jnp.broadcast_to(ai_ref[gi], (rb, 2 * SSM_STATE)) for gi in groups]
    is_fwd = lax.broadcasted_iota(jnp.int32, (rb, 2 * SSM_STATE), 1) < SSM_STATE
    is_bwd = jnp.logical_not(is_fwd)
    re_cols = pl.ds(0, 2 * SSM_STATE)
    im_cols = pl.ds(2 * SSM_STATE, 2 * SSM_STATE)

    def step(i, carry):
        rows_f = pl.ds(pl.multiple_of(i * rb, rb), rb)
        rows_b = pl.ds(pl.multiple_of((n_chunks - 1 - i) * rb, rb), rb)
        new = []
        for gi in groups:
            st_re, st_im = carry[2 * gi], carry[2 * gi + 1]
            pltpu.store(x_scr.at[gi, rows_f, re_cols], st_re, mask=is_fwd)
            pltpu.store(x_scr.at[gi, rows_f, im_cols], st_im, mask=is_fwd)
            pltpu.store(x_scr.at[gi, rows_b, re_cols], st_re, mask=is_bwd)
            pltpu.store(x_scr.at[gi, rows_b, im_cols], st_im, mask=is_bwd)
            in_re = jnp.where(is_fwd, s_scr[gi, rows_f, re_cols], s_scr[gi, rows_b, re_cols])
            in_im = jnp.where(is_fwd, s_scr[gi, rows_f, im_cols], s_scr[gi, rows_b, im_cols])
            new.append(ar[gi] * st_re - ai[gi] * st_im + in_re)
            new.append(ar[gi] * st_im + ai[gi] * st_re + in_im)
        return tuple(new)

    zero = jnp.zeros((rb, 2 * SSM_STATE), _F32)
    lax.fori_loop(0, n_chunks, step, (zero,) * (2 * len(groups)), unroll=True)

    for gi in groups:
        y = jnp.dot(u_ref[gi], t_ref[gi], preferred_element_type=_F32)
        y = y + lax.dot_general(x_scr[gi].astype(_BF16), c_ref[gi], (((1,), (1,)), ((), ())),
                                preferred_element_type=_F32)
        y_ref[gi] = y.astype(y_ref.dtype)


def _ssm(ug, tmat, smat, cmat, ar, ai, n_chunks, rows_per_chunk):
    g, rows, _ = ug.shape
    gb = SSM_GROUPS_PER_STEP
    per_group = lambda i: (i, 0, 0)
    return pl.pallas_call(
        functools.partial(_ssm_kernel, n_chunks, rows_per_chunk),
        grid=(g // gb,),
        in_specs=[pl.BlockSpec((gb, rows, CHUNK_W), per_group),
                  pl.BlockSpec((gb, CHUNK_W, CHUNK_W), per_group),
                  pl.BlockSpec((gb, CHUNK_W, STATE_W), per_group),
                  pl.BlockSpec((gb, CHUNK_W, STATE_W), per_group),
                  pl.BlockSpec((gb, 1, 2 * SSM_STATE), per_group),
                  pl.BlockSpec((gb, 1, 2 * SSM_STATE), per_group)],
        out_specs=pl.BlockSpec((gb, rows, CHUNK_W), per_group),
        out_shape=jax.ShapeDtypeStruct((g, rows, CHUNK_W), _BF16),
        scratch_shapes=[pltpu.VMEM((gb, rows, STATE_W), _F32),
                        pltpu.VMEM((gb, rows, STATE_W), _F32)],
        compiler_params=pltpu.CompilerParams(
            dimension_semantics=("arbitrary",), vmem_limit_bytes=VMEM_LIMIT_BYTES),
        name="ssm",
    )(ug, tmat, smat, cmat, ar, ai)


def _ssm_matrices(a_re, a_im, log_dt, b_re, b_im, c_re, c_im, d_skip):
    L, G, P, C = CHUNK, N_SSM_GROUPS, SSM_STATE, SSM_GROUP
    bt = jnp.stack([b_re, b_im], axis=0).transpose(2, 0, 4, 1, 3).reshape(G, 2, C, 2 * P)
    dvec = jnp.tile(d_skip.reshape(G, 1, C), (1, 1, L))
    gb = SLOTS_PER_VREG
    per_group = lambda *blk: pl.BlockSpec((gb,) + blk, lambda i: (i,) + (0,) * len(blk))
    per_dir = lambda *blk: pl.BlockSpec((2, gb) + blk, lambda i: (0, i) + (0,) * len(blk))
    mat = jax.ShapeDtypeStruct((G, L * C, L * C), _BF16)
    vec = jax.ShapeDtypeStruct((G, 1, 2 * P), _F32)
    return pl.pallas_call(
        _ssm_prep_kernel,
        grid=(G // gb,),
        in_specs=[pl.BlockSpec(memory_space=pltpu.SMEM), per_dir(P), per_dir(P),
                  per_group(2, C, 2 * P), per_dir(C, P), per_dir(C, P), per_group(1, L * C)],
        out_specs=[per_group(L * C, L * C), per_group(L * C, STATE_W),
                   per_group(L * C, STATE_W), per_group(1, 2 * P), per_group(1, 2 * P)],
        out_shape=[mat, mat, mat, vec, vec],
        compiler_params=pltpu.CompilerParams(dimension_semantics=("arbitrary",)),
        name="ssm_prep",
    )(log_dt, a_re, a_im, bt, c_re, c_im, dvec)


def _ssm_prep_kernel(ldt_ref, are_ref, aim_ref, bt_ref, cre_ref, cim_ref, dvec_ref,
                     t_ref, s_ref, c_ref, ar_ref, ai_ref):
    for gi in range(bt_ref.shape[0]):
        _ssm_prep_group(gi, ldt_ref, are_ref, aim_ref, bt_ref, cre_ref, cim_ref, dvec_ref,
                        t_ref, s_ref, c_ref, ar_ref, ai_ref)


def _ssm_prep_group(gi, ldt_ref, are_ref, aim_ref, bt_ref, cre_ref, cim_ref, dvec_ref,
                    t_ref, s_ref, c_ref, ar_ref, ai_ref):
    L, P, C = CHUNK, SSM_STATE, SSM_GROUP
    g8 = gi % SLOTS_PER_VREG
    g = pl.program_id(0) * bt_ref.shape[0] + gi
    both = lambda ref, rows: jnp.concatenate([ref[0, rows], ref[1, rows]], axis=-1)

    def time_of_slot(slot):
        return (jnp.bitwise_and(slot, -SLOTS_PER_VREG)
                + jnp.bitwise_and(slot - g8, SLOTS_PER_VREG - 1))

    def cmul(x_re, x_im, y_re, y_im):
        return x_re * y_re - x_im * y_im, x_re * y_im + x_im * y_re

    a_re, a_im = both(are_ref, pl.ds(gi, 1)), both(aim_ref, pl.ds(gi, 1))
    dt = jnp.exp(jnp.where(lax.broadcasted_iota(jnp.int32, (1, 2 * P), 1) < P,
                           ldt_ref[0, g], ldt_ref[1, g]))
    lr, li = a_re * dt, a_im * dt

    j = time_of_slot(lax.broadcasted_iota(jnp.int32, (L, 2 * P), 0)).astype(_F32)
    pos = lax.broadcasted_iota(jnp.int32, (L, 2 * P), 0).astype(_F32)
    fwd = lax.broadcasted_iota(jnp.int32, (L, 2 * P), 1) < P
    cj, sj = jnp.cos(li * j), jnp.sin(li * j)
    slot_of = lambda t: (t // SLOTS_PER_VREG) * SLOTS_PER_VREG + (t + g8) % SLOTS_PER_VREG
    in_time_order = lambda a: jnp.concatenate(
        [a[slot_of(t):slot_of(t) + 1] for t in range(L)], axis=0)
    cp, sp = in_time_order(cj), in_time_order(sj)
    krow = lax.broadcasted_iota(jnp.int32, (SUBLANES, 2 * P), 0)
    consts = jnp.where(krow == 0, 1.0, jnp.where(krow == 1, L - 1.0,
                                                 jnp.where(krow == 2, float(L), 0.0)))
    ck, sk = jnp.cos(li * consts), jnp.sin(li * consts)
    (c_1, s_1), (c_m, s_m), (c_l, s_l) = [(ck[r:r + 1], sk[r:r + 1]) for r in range(3)]

    def cpow(tau, trig):
        mag = jnp.exp(lr * tau)
        return mag * trig[0], mag * trig[1]

    minus = lambda c0, s0, c, s: (c0 * c + s0 * s, s0 * c - c0 * s)
    plus = lambda c0, s0, c, s: (c0 * c - s0 * s, s0 * c + c0 * s)
    per_dir = lambda f, b: (jnp.where(fwd, f[0], b[0]), jnp.where(fwd, f[1], b[1]))

    ab_re, ab_im = cpow(jnp.ones_like(lr), (c_1, s_1))
    den = a_re * a_re + a_im * a_im
    co_re = ((ab_re - 1.0) * a_re + ab_im * a_im) / den
    co_im = (ab_im * a_re - (ab_re - 1.0) * a_im) / den
    bb_re, bb_im = cmul(bt_ref[gi, 0], bt_ref[gi, 1], co_re, co_im)
    cc_re, cc_im = both(cre_ref, gi), both(cim_ref, gi)

    def outer(p, m):
        (p_re, p_im), (m_re, m_im) = p, m
        blocks = [cmul(p_re[s:s + 1], p_im[s:s + 1], m_re, m_im) for s in range(L)]
        return (jnp.concatenate([b[0] for b in blocks], axis=0),
                jnp.concatenate([b[1] for b in blocks], axis=0))

    s_re, s_im = outer(cpow(jnp.where(fwd, L - 1 - j, j),
                            per_dir(minus(c_m, s_m, cj, sj), (cj, sj))), (bb_re, bb_im))
    s_ref[gi] = jnp.concatenate([s_re, s_im], axis=1).astype(_BF16)
    w_re, w_im = outer(cpow(jnp.where(fwd, j + 1, L - j),
                            per_dir(plus(c_1, s_1, cj, sj), minus(c_l, s_l, cj, sj))),
                       (cc_re, cc_im))
    wcat = jnp.concatenate([w_re, -w_im], axis=1)
    c_ref[gi] = wcat.astype(_BF16)
    a_re2, a_im2 = outer(cpow(jnp.where(fwd, pos, L - 1 - pos),
                              per_dir((cp, sp), minus(c_m, s_m, cp, sp))), (cc_re, cc_im))
    acat = jnp.concatenate([a_re2, -a_im2], axis=1)
    bcat = jnp.concatenate([bb_re, bb_im], axis=1)
    is_fwd_state = jnp.bitwise_and(lax.broadcasted_iota(jnp.int32, bcat.shape, 1), P) == 0

    def split(a):
        hi = a.astype(_BF16).astype(_F32)
        return hi, a - hi

    def dot_state(a_parts, b_parts):
        (a_hi, a_lo), (b_hi, b_lo) = [[p.astype(_BF16) for p in parts]
                                      for parts in (a_parts, b_parts)]
        d = lambda a, b: lax.dot_general(a, b, (((1,), (1,)), ((), ())),
                                         preferred_element_type=_F32)
        return d(a_hi, b_hi) + d(a_hi, b_lo) + d(a_lo, b_hi)

    b_parts, a_parts = split(bcat), split(acat)
    kf = dot_state([jnp.where(is_fwd_state, p, 0.0) for p in b_parts], a_parts)
    kb = dot_state([jnp.where(is_fwd_state, 0.0, p) for p in b_parts], a_parts)

    lane = lax.broadcasted_iota(jnp.int32, (C, LANES), 1)

    def shift_slots(k, n):
        halves = [k[:, :LANES], k[:, LANES:]]
        if n >= SLOTS_PER_VREG:
            halves, n = halves[::-1], n - SLOTS_PER_VREG
        if n == 0:
            return halves
        r = [pltpu.roll(h, n * C, 1) for h in halves]
        wrapped = lane < n * C
        return [jnp.where(wrapped, r[1], r[0]), jnp.where(wrapped, r[0], r[1])]

    col_t = lax.broadcasted_iota(jnp.int32, (C, L * C), 1)
    blocks = []
    for jt in range(L):
        fwd_part = jnp.concatenate(shift_slots(kf, jt), axis=1)
        bwd_part = jnp.concatenate(shift_slots(kb, (jt + 1) % L), axis=1)
        blk = (jnp.where(col_t >= jt * C, fwd_part, 0.0)
               + jnp.where(col_t < (jt + 1) * C, bwd_part, 0.0))
        blocks.append(jnp.concatenate(
            [pltpu.roll(blk[:, h * LANES:(h + 1) * LANES], g8 * C, 1) if g8 else
             blk[:, h * LANES:(h + 1) * LANES] for h in range(L * C // LANES)], axis=1))
    time_of = lambda s: (s // SLOTS_PER_VREG) * SLOTS_PER_VREG + (s - g8) % SLOTS_PER_VREG
    tm = jnp.concatenate([blocks[time_of(s)] for s in range(L)], axis=0)
    row = lax.broadcasted_iota(jnp.int32, (L * C, L * C), 0)
    col = lax.broadcasted_iota(jnp.int32, (L * C, L * C), 1)
    t_ref[gi] = (tm + jnp.where(row == col, dvec_ref[gi], 0.0)).astype(_BF16)
    al_re, al_im = cpow(jnp.full_like(lr, float(L)), (c_l, s_l))
    ar_ref[gi] = al_re
    ai_ref[gi] = al_im


def _attn_kernel(sink_ref, q_ref, k_ref, v_ref, bias_rows_ref, za_ref, o_ref, bias_ref):
    nq = q_ref.shape[1] // BLOCK

    @pl.when(pl.program_id(0) == 0)
    def _():
        qi = lax.broadcasted_iota(jnp.int32, (BLOCK, 3 * BLOCK), 0)
        ki = lax.broadcasted_iota(jnp.int32, (BLOCK, 3 * BLOCK), 1)
        in_band = jnp.abs(ki - BLOCK - qi) <= WINDOW
        for h in range(ATTN_HEADS):
            row = jnp.broadcast_to(bias_rows_ref[h:h + 1, :], (BLOCK, BIAS_ROW))
            rel = pltpu.roll(row, 0, 1, stride=1, stride_axis=0)
            bias_ref[h] = jnp.where(in_band, rel[:, :3 * BLOCK], NEG_INF)

    rows = lambda ref, c: ref[0, c * BLOCK:(c + 1) * BLOCK]
    k_blocks = [rows(k_ref, c) for c in range(nq)]
    v_blocks = [rows(v_ref, c) for c in range(nq)]
    lane = lax.broadcasted_iota(jnp.int32, (BLOCK, LANES), 1)
    halves = [lane < HEAD_DIM, lane >= HEAD_DIM]
    keep = [h.astype(_F32).astype(_BF16) for h in halves]
    slab = lambda blk, j, e: blk[:, (j ^ e) * LANES:((j ^ e) + 1) * LANES]
    k_half = {(j, e): [slab(kb, j, e) * keep[e] for kb in k_blocks]
              for j in range(KV_HEADS) for e in range(HEAD_PAIR)}
    v_half = {(j, e): [jnp.concatenate([slab(vb, j, e) * keep[e], keep[e]], axis=1)
                       for vb in v_blocks]
              for j in range(KV_HEADS) for e in range(HEAD_PAIR)}
    for c in range(nq):
        blocks = [b for b in (c - 1, c, c + 1) if 0 <= b < nq]
        slot0 = blocks[0] - (c - 1)
        nk = len(blocks) * BLOCK
        q = rows(q_ref, c)
        slabs_out = []
        for j in range(KV_HEADS):
            window = lambda halves_of: jnp.concatenate(
                [halves_of[j, e][b] for e in range(HEAD_PAIR) for b in blocks], axis=0)
            kcat = window(k_half)
            vcat = window(v_half)
            qs = jnp.concatenate(
                [q[:, (j * PAIRS_PER_KV + i) * LANES:(j * PAIRS_PER_KV + i + 1) * LANES]
                 for i in range(PAIRS_PER_KV)], axis=0)
            s = lax.dot_general(qs, kcat, (((1,), (1,)), ((), ())),
                                preferred_element_type=_F32)
            p_rows, e_rows = [], []
            for i in range(PAIRS_PER_KV):
                p_lanes, e_sink = [], []
                for e in range(HEAD_PAIR):
                    head = j * Q_PER_KV + i * HEAD_PAIR + e
                    sg = (s[i * BLOCK:(i + 1) * BLOCK, e * nk:(e + 1) * nk]
                          + bias_ref[head, :, slot0 * BLOCK:slot0 * BLOCK + nk])
                    sk = LOG2_E * sink_ref[head]
                    m = jnp.maximum(jnp.max(sg, axis=-1, keepdims=True), sk)
                    p_lanes.append(jnp.exp2(sg - m).astype(_BF16))
                    e_sink.append(jnp.broadcast_to(jnp.exp2(sk - m), (BLOCK, LANES)))
                p_rows.append(jnp.concatenate(p_lanes, axis=1))
                e_rows.append(jnp.where(halves[0], e_sink[0], e_sink[1]))
            p = jnp.concatenate(p_rows, axis=0)
            o = jnp.dot(p, vcat, preferred_element_type=_F32)
            den = o[:, LANES:] + jnp.concatenate(e_rows, axis=0)
            on = o[:, :LANES] / den
            slabs_out += [on[i * BLOCK:(i + 1) * BLOCK] for i in range(PAIRS_PER_KV)]
        o_all = jnp.concatenate(slabs_out, axis=1)
        o_ref[0, c * BLOCK:(c + 1) * BLOCK, :] = (
            o_all * rows(za_ref, c).astype(_F32)).astype(o_ref.dtype)


def _attention(sink, q, k, v, bias_rows, za):
    b, s, _ = q.shape
    seq = lambda w: pl.BlockSpec((1, s, w), lambda i: (i, 0, 0))
    return pl.pallas_call(
        _attn_kernel,
        grid=(b,),
        in_specs=[pl.BlockSpec(memory_space=pltpu.SMEM), seq(D_ATTN), seq(D_KVX), seq(D_KVX),
                  pl.BlockSpec(bias_rows.shape, lambda i: (0, 0)), seq(D_ATTN)],
        out_specs=seq(D_ATTN),
        out_shape=jax.ShapeDtypeStruct((b, s, D_ATTN), _BF16),
        scratch_shapes=[pltpu.VMEM((ATTN_HEADS, BLOCK, 3 * BLOCK), _F32)],
        compiler_params=pltpu.CompilerParams(
            dimension_semantics=("arbitrary",), vmem_limit_bytes=VMEM_LIMIT_BYTES),
        name="attn",
    )(sink, q, k, v, bias_rows, za)


def _t5_bucket_np(rel):
    half = NUM_BUCKETS // 2
    ret = (rel > 0).astype(np.int64) * half
    n = np.abs(rel)
    max_exact = half // 2
    nf = np.maximum(n, 1).astype(np.float64)
    large = max_exact + (np.log(nf / max_exact) / math.log(MAX_DISTANCE / max_exact)
                         * (half - max_exact)).astype(np.int64)
    large = np.minimum(large, half - 1)
    return ret + np.where(n < max_exact, n, large)


def _attn_bias(rel_table):
    rel = (np.arange(BIAS_ROW) + BLOCK) % BIAS_ROW - 2 * BLOCK
    onehot = (_t5_bucket_np(rel)[None] == np.arange(NUM_BUCKETS)[:, None])
    return jnp.einsum('bh,br->hr', LOG2_E * rel_table.astype(_F32),
                      jnp.asarray(onehot, _F32), precision=_HI)


def _merge_kernel(x_ref, gain_ref, wg_f32_ref, bg_ref, ao_ref, yg_ref, zs_ref, wglu_f32_ref,
                  bglu_ref, wba_f32_ref, wbs_f32_ref, wout_f32_ref, fgain_ref, o_ref,
                  wg_ref, wglu_ref, wba_ref, wbs_ref, wout_ref, *y_scrs):
    nb, nt = x_ref.shape[0], x_ref.shape[1]
    rows = nb * SUB_STEPS

    @pl.when(pl.program_id(0) == 0)
    def _():
        wg_ref[...] = wg_f32_ref[:, W_OFF_GATES - GATE_BLOCK:].astype(_BF16)
        for dst, src, scale in ((wglu_ref, wglu_f32_ref, 0.25), (wba_ref, wba_f32_ref, 1.0),
                                (wbs_ref, wbs_f32_ref, 0.25), (wout_ref, wout_f32_ref, 0.5)):
            dst[...] = (scale * src[...]).astype(_BF16)

    for st in range(nt // SUB_STEPS):
        steps = slice(st * SUB_STEPS, (st + 1) * SUB_STEPS)
        x = x_ref[:, steps, :].reshape(rows, D_MODEL)
        h = _rms_normalize(x, 0.5 * gain_ref[...]).astype(_BF16)
        gates2 = _twice_sigmoid_of_double(
            jnp.dot(h, wg_ref[...], preferred_element_type=_F32)
            + 0.5 * bg_ref[...])
        for c in range(SUB_STEPS // CHUNK):
            _from_group_major(yg_ref, y_scrs[st], st * (SUB_STEPS // CHUNK) + c, nb, c * CHUNK,
                              SUB_STEPS)
        y = jnp.concatenate([y_scrs[st][v] for v in range(D_SSM // LANES)], axis=1)
        y2 = _twice_gelu(y)
        glu_half = (jnp.dot(y2.astype(_BF16), wglu_ref[...], preferred_element_type=_F32)
                    + 0.5 * bglu_ref[...])
        ssm4 = (y2 * _twice_sigmoid_of_double(glu_half)
                * zs_ref[:, steps, :].reshape(rows, D_SSM).astype(_F32))
        pa = jnp.dot(ao_ref[:, steps, :].reshape(rows, D_ATTN), wba_ref[...],
                     preferred_element_type=_F32)
        ps = jnp.dot(ssm4.astype(_BF16), wbs_ref[...], preferred_element_type=_F32)
        merged2 = gates2[:, :D_MODEL] * pa + gates2[:, D_MODEL:] * ps
        xn = x + jnp.dot(merged2.astype(_BF16), wout_ref[...], preferred_element_type=_F32)
        o_ref[:, steps, :] = _rms_normalize(xn, fgain_ref[...]).reshape(nb, SUB_STEPS, D_MODEL)


def _merge(x, gain, wg, bg, ao, yg, zs, wglu, bglu, wba, wbs, wout, fgain):
    bsz, s, _ = x.shape
    nt = TOKEN_TILE // bsz
    tok = lambda i: (0, i, 0)
    fixed = lambda i: (0, 0)
    full = lambda a: pl.BlockSpec(a.shape, fixed)
    once = lambda a: pl.BlockSpec(a.shape, fixed, pipeline_mode=pl.Buffered(1))
    act = lambda n: pl.BlockSpec((bsz, nt, n), tok)
    return pl.pallas_call(
        _merge_kernel,
        grid=(s // nt,),
        in_specs=[act(D_MODEL), full(gain),
                  pl.BlockSpec((D_MODEL, GATE_BLOCK), lambda i: (0, 1),
                               pipeline_mode=pl.Buffered(1)),
                  full(bg), act(D_ATTN),
                  pl.BlockSpec((N_SSM_GROUPS, nt // CHUNK, bsz, CHUNK_W),
                               lambda i: (0, i, 0, 0)),
                  act(D_SSM), once(wglu), full(bglu), once(wba), once(wbs), once(wout),
                  full(fgain)],
        out_specs=act(D_MODEL),
        out_shape=jax.ShapeDtypeStruct((bsz, s, D_MODEL), _F32),
        scratch_shapes=[pltpu.VMEM((D_MODEL, 2 * D_MODEL), _BF16)]
                       + [pltpu.VMEM(w.shape, _BF16) for w in (wglu, wba, wbs, wout)]
                       + [pltpu.VMEM((D_SSM // LANES, bsz * SUB_STEPS, LANES), _F32)
                          for _ in range(nt // SUB_STEPS)],
        compiler_params=pltpu.CompilerParams(
            dimension_semantics=("arbitrary",), vmem_limit_bytes=VMEM_LIMIT_BYTES),
        name="merge",
    )(x, gain, wg, bg, ao, yg, zs, wglu, bglu, wba, wbs, wout, fgain)


def _layer(x, norm_gain, w_in, b_gate, attn_sink, a_re, a_im, log_dt, b_re, b_im, c_re, c_im,
           d_skip, w_glu, b_glu, w_ba, w_bs, w_out, out_gain, bias):
    bsz, s, d = x.shape
    assert d == D_MODEL and x.dtype == _F32 and w_in.shape == (D_MODEL, W_OFF_GATES + 2 * D_MODEL)
    assert bsz % 8 == 0 and s % BLOCK == 0
    assert INPROJ_TOKEN_TILE % (bsz * SUB_STEPS) == 0 and s % (INPROJ_TOKEN_TILE // bsz) == 0
    assert TOKEN_TILE % (bsz * SUB_STEPS) == 0 and s % (TOKEN_TILE // bsz) == 0
    n_chunks = s // CHUNK
    gain = norm_gain.reshape(1, D_MODEL).astype(_F32)
    q, k, v, za, ug, zs = _inproj(x, gain, w_in[:, :W_OFF_GATES].astype(_BF16))

    tmat, smat, cmat, ar, ai = _ssm_matrices(a_re, a_im, log_dt, b_re, b_im, c_re, c_im,
                                             d_skip)
    yg = _ssm(ug.reshape(N_SSM_GROUPS, n_chunks * bsz, CHUNK_W), tmat, smat, cmat, ar, ai,
              n_chunks, bsz)
    yg = yg.reshape(N_SSM_GROUPS, n_chunks, bsz, CHUNK_W)

    ao = _attention(attn_sink.astype(_F32), q, k, v, bias, za)

    return _merge(x, gain, w_in, b_gate.reshape(1, -1).astype(_F32), ao, yg, zs, w_glu,
                  b_glu.reshape(1, -1).astype(_F32), w_ba, w_bs, w_out,
                  out_gain.reshape(1, D_MODEL).astype(_F32))


def kernel(x, norm_gain, w_in, b_gate, attn_sink, rel_bias_table, ssm_a_re, ssm_a_im,
           ssm_log_dt, ssm_b_re, ssm_b_im, ssm_c_re, ssm_c_im, ssm_d, w_glu, b_glu,
           w_branch_attn, w_branch_ssm, w_out, final_norm_gain):
    depth = norm_gain.shape[0]
    assert depth == 1, "final norm is fused into the single layer's epilogue"
    bias = _attn_bias(rel_bias_table)
    l = 0
    return _layer(x, norm_gain[l], w_in[l], b_gate[l], attn_sink[l], ssm_a_re[l],
                  ssm_a_im[l], ssm_log_dt[l], ssm_b_re[l], ssm_b_im[l], ssm_c_re[l],
                  ssm_c_im[l], ssm_d[l], w_glu[l], b_glu[l], w_branch_attn[l],
                  w_branch_ssm[l], w_out[l], final_norm_gain, bias)
```

```python
import functools
import math

import jax
import jax.numpy as jnp
import numpy as np
from jax import lax
from jax.experimental import pallas as pl
from jax.experimental.pallas import tpu as pltpu

D_MODEL = 1024
ATTN_HEADS = 8
KV_HEADS = 2
Q_PER_KV = ATTN_HEADS // KV_HEADS
HEAD_DIM = 64
D_ATTN = ATTN_HEADS * HEAD_DIM
D_KV = KV_HEADS * HEAD_DIM
WINDOW = 128
BLOCK = 128
NUM_BUCKETS = 32
MAX_DISTANCE = 128
D_SSM = 512
SSM_GROUP = 16
N_SSM_GROUPS = D_SSM // SSM_GROUP
SSM_STATE = 64
EPS = 1e-6
NEG_INF = -1e30

LANES = 128
SUBLANES = 8
MXU_WIDTH = 256
CHUNK = MXU_WIDTH // SSM_GROUP
CHUNK_W = CHUNK * SSM_GROUP
STATE_W = 4 * SSM_STATE
SLOTS_PER_VREG = LANES // SSM_GROUP
SSM_GROUPS_PER_STEP = 4

HEAD_PAIR = LANES // HEAD_DIM
PAIRS_PER_KV = Q_PER_KV // HEAD_PAIR
assert D_KV == LANES and KV_HEADS == HEAD_PAIR
D_KVX = 2 * D_KV
LOG2_E = math.log2(math.e)
Q_SCALE = HEAD_DIM ** -0.5 * LOG2_E
BIAS_ROW = 4 * BLOCK

W_OFF_ZA = D_ATTN + 2 * D_KV
W_OFF_U = W_OFF_ZA + D_ATTN
W_OFF_GATES = W_OFF_U + 2 * D_SSM
GATE_BLOCK = (W_OFF_GATES + 2 * D_MODEL) // 2
assert GATE_BLOCK % LANES == 0 and GATE_BLOCK <= W_OFF_GATES

TOKEN_TILE = 1024
INPROJ_TOKEN_TILE = 2048
SUB_STEPS = CHUNK
VMEM_LIMIT_BYTES = 56 * 1024 * 1024

_F32 = jnp.float32
_BF16 = jnp.bfloat16
_HI = lax.Precision.HIGHEST


def _silu(x):
    return x * (0.5 * jnp.tanh(0.5 * x) + 0.5)


def _twice_sigmoid_of_double(xh):
    return 1.0 + jnp.tanh(xh)


def _twice_gelu(x):
    c = math.sqrt(2.0 / math.pi)
    return x * (1.0 + jnp.tanh(c * (x + 0.044715 * (x * x * x))))


def _rms_normalize(x, gain):
    ms = jnp.mean(x * x, axis=-1, keepdims=True)
    return x * lax.rsqrt(ms + EPS) * gain


def _slot_masks(rows):
    lane = lax.broadcasted_iota(jnp.int32, (rows, LANES), 1)
    return [(lane >= p * SSM_GROUP) & (lane < (p + 1) * SSM_GROUP)
            for p in range(SLOTS_PER_VREG)]


def _to_group_major(u_scr, ug_ref, kk, nb, t0, steps):
    masks = _slot_masks(nb)
    for v in range(D_SSM // LANES):
        for half in range(CHUNK // SLOTS_PER_VREG):
            rolled = []
            for t8 in range(SLOTS_PER_VREG):
                r = t0 + half * SLOTS_PER_VREG + t8
                piece = u_scr[v, pl.ds(r, nb, stride=steps), :]
                rolled.append(pltpu.roll(piece, t8 * SSM_GROUP, 1) if t8 else piece)
            for p0 in range(SLOTS_PER_VREG):
                acc = rolled[0]
                for t8 in range(1, SLOTS_PER_VREG):
                    acc = jnp.where(masks[(p0 + t8) % SLOTS_PER_VREG], rolled[t8], acc)
                g = v * SLOTS_PER_VREG + p0
                ug_ref[g, kk, :, half * LANES:(half + 1) * LANES] = acc.astype(_BF16)


def _from_group_major(yg_ref, y_scr, kk, nb, t0, steps):
    masks = _slot_masks(nb)
    for v in range(D_SSM // LANES):
        for half in range(CHUNK // SLOTS_PER_VREG):
            src = [yg_ref[v * SLOTS_PER_VREG + p0, kk, :,
                          half * LANES:(half + 1) * LANES].astype(_F32)
                   for p0 in range(SLOTS_PER_VREG)]
            for t8 in range(SLOTS_PER_VREG):
                acc = src[0]
                for p0 in range(1, SLOTS_PER_VREG):
                    acc = jnp.where(masks[(p0 + t8) % SLOTS_PER_VREG], src[p0], acc)
                if t8:
                    acc = pltpu.roll(acc, LANES - t8 * SSM_GROUP, 1)
                r = t0 + half * SLOTS_PER_VREG + t8
                y_scr[v, pl.ds(r, nb, stride=steps), :] = acc


def _inproj_kernel(x_ref, gain_ref, w_ref, q_ref, k_ref, v_ref, za_ref, ug_ref, zs_ref,
                   *u_scrs):
    nb, nt = x_ref.shape[0], x_ref.shape[1]
    rows = nb * SUB_STEPS
    for st in range(nt // SUB_STEPS):
        steps = slice(st * SUB_STEPS, (st + 1) * SUB_STEPS)
        x = x_ref[:, steps, :].reshape(rows, D_MODEL)
        h = _rms_normalize(x, gain_ref[...]).astype(_BF16)

        def put(ref, val):
            ref[:, steps, :] = val.astype(_BF16).reshape(nb, SUB_STEPS, val.shape[-1])

        proj = lambda lo, n: jnp.dot(h, w_ref[:, lo:lo + n], preferred_element_type=_F32)
        u_zs = proj(W_OFF_U, 2 * D_SSM)
        for v in range(D_SSM // LANES):
            u_scrs[st][v] = u_zs[:, v * LANES:(v + 1) * LANES]
        for c in range(SUB_STEPS // CHUNK):
            _to_group_major(u_scrs[st], ug_ref, st * (SUB_STEPS // CHUNK) + c, nb, c * CHUNK,
                            SUB_STEPS)
        put(zs_ref, _silu(u_zs[:, D_SSM:]))
        put(za_ref, _silu(proj(W_OFF_ZA, D_ATTN)))
        qkv = proj(0, W_OFF_ZA)
        put(q_ref, qkv[:, :D_ATTN] * Q_SCALE)
        with_swap = lambda a: jnp.concatenate([a, pltpu.roll(a, HEAD_DIM, 1)], axis=1)
        put(k_ref, with_swap(qkv[:, D_ATTN:D_ATTN + D_KV]))
        put(v_ref, with_swap(qkv[:, D_ATTN + D_KV:]))


def _inproj(x, gain, w_in):
    bsz, s, _ = x.shape
    nt = INPROJ_TOKEN_TILE // bsz
    tok = lambda i: (0, i, 0)
    fixed = lambda i: (0, 0)
    outs = [D_ATTN, D_KVX, D_KVX, D_ATTN, None, D_SSM]
    act = lambda n: (pl.BlockSpec((bsz, nt, n), tok), jax.ShapeDtypeStruct((bsz, s, n), _BF16))
    ug = (pl.BlockSpec((N_SSM_GROUPS, nt // CHUNK, bsz, CHUNK_W), lambda i: (0, i, 0, 0)),
          jax.ShapeDtypeStruct((N_SSM_GROUPS, s // CHUNK, bsz, CHUNK_W), _BF16))
    specs, shapes = zip(*[ug if n is None else act(n) for n in outs])
    return pl.pallas_call(
        _inproj_kernel,
        grid=(s // nt,),
        in_specs=[pl.BlockSpec((bsz, nt, D_MODEL), tok),
                  pl.BlockSpec((1, D_MODEL), fixed)]
                 + [pl.BlockSpec((D_MODEL, W_OFF_GATES), fixed)],
        out_specs=list(specs),
        out_shape=list(shapes),
        scratch_shapes=[pltpu.VMEM((D_SSM // LANES, bsz * SUB_STEPS, LANES), _F32)
                        for _ in range(nt // SUB_STEPS)],
        compiler_params=pltpu.CompilerParams(
            dimension_semantics=("arbitrary",), vmem_limit_bytes=VMEM_LIMIT_BYTES),
        name="inproj",
    )(x, gain, w_in)


def _ssm_kernel(n_chunks, rows_per_chunk, u_ref, t_ref, s_ref, c_ref, ar_ref, ai_ref,
                y_ref, s_scr, x_scr):
    rb = rows_per_chunk
    groups = range(u_ref.shape[0])
    for gi in groups:
        s_scr[gi] = jnp.dot(u_ref[gi], s_ref[gi], preferred_element_type=_F32)
    ar = [jnp.broadcast_to(ar_ref[gi], (rb, 2 * SSM_STATE)) for gi in groups]
    ai = [jnp.broadcast_to(ai_ref[gi], (rb, 2 * SSM_STATE)) for gi in groups]
    is_fwd = lax.broadcasted_iota(jnp.int32, (rb, 2 * SSM_STATE), 1) < SSM_STATE
    is_bwd = jnp.logical_not(is_fwd)
    re_cols = pl.ds(0, 2 * SSM_STATE)
    im_cols = pl.ds(2 * SSM_STATE, 2 * SSM_STATE)

    def step(i, carry):
        rows_f = pl.ds(pl.multiple_of(i * rb, rb), rb)
        rows_b = pl.ds(pl.multiple_of((n_chunks - 1 - i) * rb, rb), rb)
        new = []
        for gi in groups:
            st_re, st_im = carry[2 * gi], carry[2 * gi + 1]
            pltpu.store(x_scr.at[gi, rows_f, re_cols], st_re, mask=is_fwd)
            pltpu.store(x_scr.at[gi, rows_f, im_cols], st_im, mask=is_fwd)
            pltpu.store(x_scr.at[gi, rows_b, re_cols], st_re, mask=is_bwd)
            pltpu.store(x_scr.at[gi, rows_b, im_cols], st_im, mask=is_bwd)
            in_re = jnp.where(is_fwd, s_scr[gi, rows_f, re_cols], s_scr[gi, rows_b, re_cols])
            in_im = jnp.where(is_fwd, s_scr[gi, rows_f, im_cols], s_scr[gi, rows_b, im_cols])
            new.append(ar[gi] * st_re - ai[gi] * st_im + in_re)
            new.append(ar[gi] * st_im + ai[gi] * st_re + in_im)
        return tuple(new)

    zero = jnp.zeros((rb, 2 * SSM_STATE), _F32)
    lax.fori_loop(0, n_chunks, step, (zero,) * (2 * len(groups)), unroll=True)

    for gi in groups:
        y = jnp.dot(u_ref[gi], t_ref[gi], preferred_element_type=_F32)
        y = y + lax.dot_general(x_scr[gi].astype(_BF16), c_ref[gi], (((1,), (1,)), ((), ())),
                                preferred_element_type=_F32)
        y_ref[gi] = y.astype(y_ref.dtype)


def _ssm(ug, tmat, smat, cmat, ar, ai, n_chunks, rows_per_chunk):
    g, rows, _ = ug.shape
    gb = SSM_GROUPS_PER_STEP
    per_group = lambda i: (i, 0, 0)
    return pl.pallas_call(
        functools.partial(_ssm_kernel, n_chunks, rows_per_chunk),
        grid=(g // gb,),
        in_specs=[pl.BlockSpec((gb, rows, CHUNK_W), per_group),
                  pl.BlockSpec((gb, CHUNK_W, CHUNK_W), per_group),
                  pl.BlockSpec((gb, CHUNK_W, STATE_W), per_group),
                  pl.BlockSpec((gb, CHUNK_W, STATE_W), per_group),
                  pl.BlockSpec((gb, 1, 2 * SSM_STATE), per_group),
                  pl.BlockSpec((gb, 1, 2 * SSM_STATE), per_group)],
        out_specs=pl.BlockSpec((gb, rows, CHUNK_W), per_group),
        out_shape=jax.ShapeDtypeStruct((g, rows, CHUNK_W), _BF16),
        scratch_shapes=[pltpu.VMEM((gb, rows, STATE_W), _F32),
                        pltpu.VMEM((gb, rows, STATE_W), _F32)],
        compiler_params=pltpu.CompilerParams(
            dimension_semantics=("arbitrary",), vmem_limit_bytes=VMEM_LIMIT_BYTES),
        name="ssm",
    )(ug, tmat, smat, cmat, ar, ai)


def _ssm_matrices(a_re, a_im, log_dt, b_re, b_im, c_re, c_im, d_skip):
    L, G, P, C = CHUNK, N_SSM_GROUPS, SSM_STATE, SSM_GROUP
    bt = jnp.stack([b_re, b_im], axis=0).transpose(2, 0, 4, 1, 3).reshape(G, 2, C, 2 * P)
    dvec = d_skip.reshape(G, C, 1)
    gb = SLOTS_PER_VREG
    per_group = lambda *blk: pl.BlockSpec((gb,) + blk, lambda i: (i,) + (0,) * len(blk))
    per_dir = lambda *blk: pl.BlockSpec((2, gb) + blk, lambda i: (0, i) + (0,) * len(blk))
    mat = jax.ShapeDtypeStruct((G, L * C, L * C), _BF16)
    vec = jax.ShapeDtypeStruct((G, 1, 2 * P), _F32)
    return pl.pallas_call(
        _ssm_prep_kernel,
        grid=(G // gb,),
        in_specs=[pl.BlockSpec(memory_space=pltpu.SMEM), per_dir(P), per_dir(P),
                  per_group(2, C, 2 * P), per_dir(C, P), per_dir(C, P), per_group(C, 1)],
        out_specs=[per_group(L * C, L * C), per_group(L * C, STATE_W),
                   per_group(L * C, STATE_W), per_group(1, 2 * P), per_group(1, 2 * P)],
        out_shape=[mat, mat, mat, vec, vec],
        compiler_params=pltpu.CompilerParams(dimension_semantics=("arbitrary",)),
        name="ssm_prep",
    )(log_dt, a_re, a_im, bt, c_re, c_im, dvec)


def _ssm_prep_kernel(ldt_ref, are_ref, aim_ref, bt_ref, cre_ref, cim_ref, dvec_ref,
                     t_ref, s_ref, c_ref, ar_ref, ai_ref):
    for gi in range(bt_ref.shape[0]):
        _ssm_prep_group(gi, ldt_ref, are_ref, aim_ref, bt_ref, cre_ref, cim_ref, dvec_ref,
                        t_ref, s_ref, c_ref, ar_ref, ai_ref)


def _ssm_prep_group(gi, ldt_ref, are_ref, aim_ref, bt_ref, cre_ref, cim_ref, dvec_ref,
                    t_ref, s_ref, c_ref, ar_ref, ai_ref):
    L, P, C = CHUNK, SSM_STATE, SSM_GROUP
    g8 = gi % SLOTS_PER_VREG
    g = pl.program_id(0) * bt_ref.shape[0] + gi
    both = lambda ref, rows: jnp.concatenate([ref[0, rows], ref[1, rows]], axis=-1)

    def time_of_slot(slot):
        return (jnp.bitwise_and(slot, -SLOTS_PER_VREG)
                + jnp.bitwise_and(slot - g8, SLOTS_PER_VREG - 1))

    def cmul(x_re, x_im, y_re, y_im):
        return x_re * y_re - x_im * y_im, x_re * y_im + x_im * y_re

    a_re, a_im = both(are_ref, pl.ds(gi, 1)), both(aim_ref, pl.ds(gi, 1))
    dt = jnp.exp(jnp.where(lax.broadcasted_iota(jnp.int32, (1, 2 * P), 1) < P,
                           ldt_ref[0, g], ldt_ref[1, g]))
    lr, li = a_re * dt, a_im * dt

    j = time_of_slot(lax.broadcasted_iota(jnp.int32, (L, 2 * P), 0)).astype(_F32)
    pos = lax.broadcasted_iota(jnp.int32, (L, 2 * P), 0).astype(_F32)
    fwd = lax.broadcasted_iota(jnp.int32, (L, 2 * P), 1) < P
    cj, sj = jnp.cos(li * j), jnp.sin(li * j)
    slot_of = lambda t: (t // SLOTS_PER_VREG) * SLOTS_PER_VREG + (t + g8) % SLOTS_PER_VREG
    in_time_order = lambda a: jnp.concatenate(
        [a[slot_of(t):slot_of(t) + 1] for t in range(L)], axis=0)
    cp, sp = in_time_order(cj), in_time_order(sj)
    krow = lax.broadcasted_iota(jnp.int32, (SUBLANES, 2 * P), 0)
    consts = jnp.where(krow == 0, 1.0, jnp.where(krow == 1, L - 1.0,
                                                 jnp.where(krow == 2, float(L), 0.0)))
    ck, sk = jnp.cos(li * consts), jnp.sin(li * consts)
    (c_1, s_1), (c_m, s_m), (c_l, s_l) = [(ck[r:r + 1], sk[r:r + 1]) for r in range(3)]

    def cpow(tau, trig):
        mag = jnp.exp(lr * tau)
        return mag * trig[0], mag * trig[1]

    minus = lambda c0, s0, c, s: (c0 * c + s0 * s, s0 * c - c0 * s)
    plus = lambda c0, s0, c, s: (c0 * c - s0 * s, s0 * c + c0 * s)
    per_dir = lambda f, b: (jnp.where(fwd, f[0], b[0]), jnp.where(fwd, f[1], b[1]))

    ab_re, ab_im = cpow(jnp.ones_like(lr), (c_1, s_1))
    den = a_re * a_re + a_im * a_im
    co_re = ((ab_re - 1.0) * a_re + ab_im * a_im) / den
    co_im = (ab_im * a_re - (ab_re - 1.0) * a_im) / den
    bb_re, bb_im = cmul(bt_ref[gi, 0], bt_ref[gi, 1], co_re, co_im)
    cc_re, cc_im = both(cre_ref, gi), both(cim_ref, gi)

    def outer(p, m):
        (p_re, p_im), (m_re, m_im) = p, m
        blocks = [cmul(p_re[s:s + 1], p_im[s:s + 1], m_re, m_im) for s in range(L)]
        return (jnp.concatenate([b[0] for b in blocks], axis=0),
                jnp.concatenate([b[1] for b in blocks], axis=0))

    s_re, s_im = outer(cpow(jnp.where(fwd, L - 1 - j, j),
                            per_dir(minus(c_m, s_m, cj, sj), (cj, sj))), (bb_re, bb_im))
    s_ref[gi] = jnp.concatenate([s_re, s_im], axis=1).astype(_BF16)
    w_re, w_im = outer(cpow(jnp.where(fwd, j + 1, L - j),
                            per_dir(plus(c_1, s_1, cj, sj), minus(c_l, s_l, cj, sj))),
                       (cc_re, cc_im))
    wcat = jnp.concatenate([w_re, -w_im], axis=1)
    c_ref[gi] = wcat.astype(_BF16)
    a_re2, a_im2 = outer(cpow(jnp.where(fwd, pos, L - 1 - pos),
                              per_dir((cp, sp), minus(c_m, s_m, cp, sp))), (cc_re, cc_im))
    acat = jnp.concatenate([a_re2, -a_im2], axis=1)
    bcat = jnp.concatenate([bb_re, bb_im], axis=1)
    is_fwd_state = jnp.bitwise_and(lax.broadcasted_iota(jnp.int32, bcat.shape, 1), P) == 0

    def split(a):
        hi = a.astype(_BF16).astype(_F32)
        return hi, a - hi

    def dot_state(a_parts, b_parts):
        (a_hi, a_lo), (b_hi, b_lo) = [[p.astype(_BF16) for p in parts]
                                      for parts in (a_parts, b_parts)]
        d = lambda a, b: lax.dot_general(a, b, (((1,), (1,)), ((), ())),
                                         preferred_element_type=_F32)
        return d(a_hi, b_hi) + d(a_hi, b_lo) + d(a_lo, b_hi)

    b_parts, a_parts = split(bcat), split(acat)
    kf = dot_state([jnp.where(is_fwd_state, p, 0.0) for p in b_parts], a_parts)
    kb = dot_state([jnp.where(is_fwd_state, 0.0, p) for p in b_parts], a_parts)

    lane = lax.broadcasted_iota(jnp.int32, (C, LANES), 1)

    def shift_slots(k, n):
        halves = [k[:, :LANES], k[:, LANES:]]
        if n >= SLOTS_PER_VREG:
            halves, n = halves[::-1], n - SLOTS_PER_VREG
        if n == 0:
            return halves
        r = [pltpu.roll(h, n * C, 1) for h in halves]
        wrapped = lane < n * C
        return [jnp.where(wrapped, r[1], r[0]), jnp.where(wrapped, r[0], r[1])]

    col_t = lax.broadcasted_iota(jnp.int32, (C, L * C), 1)
    blocks = []
    for jt in range(L):
        fwd_part = jnp.concatenate(shift_slots(kf, jt), axis=1)
        bwd_part = jnp.concatenate(shift_slots(kb, (jt + 1) % L), axis=1)
        blk = (jnp.where(col_t >= jt * C, fwd_part, 0.0)
               + jnp.where(col_t < (jt + 1) * C, bwd_part, 0.0))
        blocks.append(jnp.concatenate(
            [pltpu.roll(blk[:, h * LANES:(h + 1) * LANES], g8 * C, 1) if g8 else
             blk[:, h * LANES:(h + 1) * LANES] for h in range(L * C // LANES)], axis=1))
    time_of = lambda s: (s // SLOTS_PER_VREG) * SLOTS_PER_VREG + (s - g8) % SLOTS_PER_VREG
    tm = jnp.concatenate([blocks[time_of(s)] for s in range(L)], axis=0)
    row = lax.broadcasted_iota(jnp.int32, (L * C, L * C), 0)
    col = lax.broadcasted_iota(jnp.int32, (L * C, L * C), 1)
    chan_of_lane = jnp.bitwise_and(lax.broadcasted_iota(jnp.int32, (C, L * C), 1), C - 1)
    dvec = jnp.sum(jnp.where(chan_of_lane == lax.broadcasted_iota(jnp.int32, (C, L * C), 0),
                             dvec_ref[gi], 0.0), axis=0, keepdims=True)
    t_ref[gi] = (tm + jnp.where(row == col, dvec, 0.0)).astype(_BF16)
    al_re, al_im = cpow(jnp.full_like(lr, float(L)), (c_l, s_l))
    ar_ref[gi] = al_re
    ai_ref[gi] = al_im


def _attn_kernel(sink_ref, q_ref, k_ref, v_ref, bias_rows_ref, za_ref, o_ref, bias_ref):
    nq = q_ref.shape[1] // BLOCK

    @pl.when(pl.program_id(0) == 0)
    def _():
        qi = lax.broadcasted_iota(jnp.int32, (BLOCK, 3 * BLOCK), 0)
        ki = lax.broadcasted_iota(jnp.int32, (BLOCK, 3 * BLOCK), 1)
        in_band = jnp.abs(ki - BLOCK - qi) <= WINDOW
        for h in range(ATTN_HEADS):
            row = jnp.broadcast_to(bias_rows_ref[h:h + 1, :], (BLOCK, BIAS_ROW))
            rel = pltpu.roll(row, 0, 1, stride=1, stride_axis=0)
            bias_ref[h] = jnp.where(in_band, rel[:, :3 * BLOCK], NEG_INF)

    rows = lambda ref, c: ref[0, c * BLOCK:(c + 1) * BLOCK]
    k_blocks = [rows(k_ref, c) for c in range(nq)]
    v_blocks = [rows(v_ref, c) for c in range(nq)]
    lane = lax.broadcasted_iota(jnp.int32, (BLOCK, LANES), 1)
    halves = [lane < HEAD_DIM, lane >= HEAD_DIM]
    keep = [h.astype(_F32).astype(_BF16) for h in halves]
    slab = lambda blk, j, e: blk[:, (j ^ e) * LANES:((j ^ e) + 1) * LANES]
    k_half = {(j, e): [slab(kb, j, e) * keep[e] for kb in k_blocks]
              for j in range(KV_HEADS) for e in range(HEAD_PAIR)}
    v_half = {(j, e): [jnp.concatenate([slab(vb, j, e) * keep[e], keep[e]], axis=1)
                       for vb in v_blocks]
              for j in range(KV_HEADS) for e in range(HEAD_PAIR)}
    for c in range(nq):
        blocks = [b for b in (c - 1, c, c + 1) if 0 <= b < nq]
        slot0 = blocks[0] - (c - 1)
        nk = len(blocks) * BLOCK
        q = rows(q_ref, c)
        slabs_out = []
        for j in range(KV_HEADS):
            window = lambda halves_of: jnp.concatenate(
                [halves_of[j, e][b] for e in range(HEAD_PAIR) for b in blocks], axis=0)
            kcat = window(k_half)
            vcat = window(v_half)
            qs = jnp.concatenate(
                [q[:, (j * PAIRS_PER_KV + i) * LANES:(j * PAIRS_PER_KV + i + 1) * LANES]
                 for i in range(PAIRS_PER_KV)], axis=0)
            s = lax.dot_general(qs, kcat, (((1,), (1,)), ((), ())),
                                preferred_element_type=_F32)
            p_rows, e_rows = [], []
            for i in range(PAIRS_PER_KV):
                p_lanes, e_sink = [], []
                for e in range(HEAD_PAIR):
                    head = j * Q_PER_KV + i * HEAD_PAIR + e
                    sg = (s[i * BLOCK:(i + 1) * BLOCK, e * nk:(e + 1) * nk]
                          + bias_ref[head, :, slot0 * BLOCK:slot0 * BLOCK + nk])
                    sk = LOG2_E * sink_ref[head]
                    m = jnp.maximum(jnp.max(sg, axis=-1, keepdims=True), sk)
                    p_lanes.append(jnp.exp2(sg - m).astype(_BF16))
                    e_sink.append(jnp.broadcast_to(jnp.exp2(sk - m), (BLOCK, LANES)))
                p_rows.append(jnp.concatenate(p_lanes, axis=1))
                e_rows.append(jnp.where(halves[0], e_sink[0], e_sink[1]))
            p = jnp.concatenate(p_rows, axis=0)
            o = jnp.dot(p, vcat, preferred_element_type=_F32)
            den = o[:, LANES:] + jnp.concatenate(e_rows, axis=0)
            on = o[:, :LANES] / den
            slabs_out += [on[i * BLOCK:(i + 1) * BLOCK] for i in range(PAIRS_PER_KV)]
        o_all = jnp.concatenate(slabs_out, axis=1)
        o_ref[0, c * BLOCK:(c + 1) * BLOCK, :] = (
            o_all * rows(za_ref, c).astype(_F32)).astype(o_ref.dtype)


def _attention(sink, q, k, v, bias_rows, za):
    b, s, _ = q.shape
    seq = lambda w: pl.BlockSpec((1, s, w), lambda i: (i, 0, 0))
    return pl.pallas_call(
        _attn_kernel,
        grid=(b,),
        in_specs=[pl.BlockSpec(memory_space=pltpu.SMEM), seq(D_ATTN), seq(D_KVX), seq(D_KVX),
                  pl.BlockSpec(bias_rows.shape, lambda i: (0, 0)), seq(D_ATTN)],
        out_specs=seq(D_ATTN),
        out_shape=jax.ShapeDtypeStruct((b, s, D_ATTN), _BF16),
        scratch_shapes=[pltpu.VMEM((ATTN_HEADS, BLOCK, 3 * BLOCK), _F32)],
        compiler_params=pltpu.CompilerParams(
            dimension_semantics=("arbitrary",), vmem_limit_bytes=VMEM_LIMIT_BYTES),
        name="attn",
    )(sink, q, k, v, bias_rows, za)


def _t5_bucket_np(rel):
    half = NUM_BUCKETS // 2
    ret = (rel > 0).astype(np.int64) * half
    n = np.abs(rel)
    max_exact = half // 2
    nf = np.maximum(n, 1).astype(np.float64)
    large = max_exact + (np.log(nf / max_exact) / math.log(MAX_DISTANCE / max_exact)
                         * (half - max_exact)).astype(np.int64)
    large = np.minimum(large, half - 1)
    return ret + np.where(n < max_exact, n, large)


def _attn_bias(rel_table):
    rel = (np.arange(BIAS_ROW) + BLOCK) % BIAS_ROW - 2 * BLOCK
    onehot = (_t5_bucket_np(rel)[None] == np.arange(NUM_BUCKETS)[:, None])
    return jnp.einsum('bh,br->hr', LOG2_E * rel_table.astype(_F32),
                      jnp.asarray(onehot, _F32), precision=_HI)


def _merge_kernel(x_ref, gain_ref, wg_ref, bg_ref, ao_ref, yg_ref, zs_ref, wglu_f32_ref,
                  bglu_ref, wba_f32_ref, wbs_f32_ref, wout_f32_ref, fgain_ref, o_ref,
                  wglu_ref, wba_ref, wbs_ref, wout_ref, *y_scrs):
    nb, nt = x_ref.shape[0], x_ref.shape[1]
    rows = nb * SUB_STEPS

    @pl.when(pl.program_id(0) == 0)
    def _():
        for dst, src, scale in ((wglu_ref, wglu_f32_ref, 0.25), (wba_ref, wba_f32_ref, 1.0),
                                (wbs_ref, wbs_f32_ref, 0.25), (wout_ref, wout_f32_ref, 0.5)):
            dst[...] = (scale * src[...]).astype(_BF16)

    for st in range(nt // SUB_STEPS):
        steps = slice(st * SUB_STEPS, (st + 1) * SUB_STEPS)
        x = x_ref[:, steps, :].reshape(rows, D_MODEL)
        h = _rms_normalize(x, 0.5 * gain_ref[...]).astype(_BF16)
        gates2 = _twice_sigmoid_of_double(
            jnp.dot(h, wg_ref[:, W_OFF_GATES - GATE_BLOCK:], preferred_element_type=_F32)
            + 0.5 * bg_ref[...])
        for c in range(SUB_STEPS // CHUNK):
            _from_group_major(yg_ref, y_scrs[st], st * (SUB_STEPS // CHUNK) + c, nb, c * CHUNK,
                              SUB_STEPS)
        y = jnp.concatenate([y_scrs[st][v] for v in range(D_SSM // LANES)], axis=1)
        y2 = _twice_gelu(y)
        glu_half = (jnp.dot(y2.astype(_BF16), wglu_ref[...], preferred_element_type=_F32)
                    + 0.5 * bglu_ref[...])
        ssm4 = (y2 * _twice_sigmoid_of_double(glu_half)
                * zs_ref[:, steps, :].reshape(rows, D_SSM).astype(_F32))
        pa = jnp.dot(ao_ref[:, steps, :].reshape(rows, D_ATTN), wba_ref[...],
                     preferred_element_type=_F32)
        ps = jnp.dot(ssm4.astype(_BF16), wbs_ref[...], preferred_element_type=_F32)
        merged2 = gates2[:, :D_MODEL] * pa + gates2[:, D_MODEL:] * ps
        xn = x + jnp.dot(merged2.astype(_BF16), wout_ref[...], preferred_element_type=_F32)
        o_ref[:, steps, :] = _rms_normalize(xn, fgain_ref[...]).reshape(nb, SUB_STEPS, D_MODEL)


def _merge(x, gain, wg, bg, ao, yg, zs, wglu, bglu, wba, wbs, wout, fgain):
    bsz, s, _ = x.shape
    nt = TOKEN_TILE // bsz
    tok = lambda i: (0, i, 0)
    fixed = lambda i: (0, 0)
    full = lambda a: pl.BlockSpec(a.shape, fixed)
    once = lambda a: pl.BlockSpec(a.shape, fixed, pipeline_mode=pl.Buffered(1))
    act = lambda n: pl.BlockSpec((bsz, nt, n), tok)
    return pl.pallas_call(
        _merge_kernel,
        grid=(s // nt,),
        in_specs=[act(D_MODEL), full(gain),
                  pl.BlockSpec((D_MODEL, GATE_BLOCK), lambda i: (0, 1)),
                  full(bg), act(D_ATTN),
                  pl.BlockSpec((N_SSM_GROUPS, nt // CHUNK, bsz, CHUNK_W),
                               lambda i: (0, i, 0, 0)),
                  act(D_SSM), once(wglu), full(bglu), once(wba), once(wbs), once(wout),
                  full(fgain)],
        out_specs=act(D_MODEL),
        out_shape=jax.ShapeDtypeStruct((bsz, s, D_MODEL), _F32),
        scratch_shapes=[pltpu.VMEM(w.shape, _BF16) for w in (wglu, wba, wbs, wout)]
                       + [pltpu.VMEM((D_SSM // LANES, bsz * SUB_STEPS, LANES), _F32)
                          for _ in range(nt // SUB_STEPS)],
        compiler_params=pltpu.CompilerParams(
            dimension_semantics=("arbitrary",), vmem_limit_bytes=VMEM_LIMIT_BYTES),
        name="merge",
    )(x, gain, wg, bg, ao, yg, zs, wglu, bglu, wba, wbs, wout, fgain)


def _layer(x, norm_gain, w_in, b_gate, attn_sink, a_re, a_im, log_dt, b_re, b_im, c_re, c_im,
           d_skip, w_glu, b_glu, w_ba, w_bs, w_out, out_gain, bias):
    bsz, s, d = x.shape
    assert d == D_MODEL and x.dtype == _F32 and w_in.shape == (D_MODEL, W_OFF_GATES + 2 * D_MODEL)
    assert bsz % 8 == 0 and s % BLOCK == 0
    assert INPROJ_TOKEN_TILE % (bsz * SUB_STEPS) == 0 and s % (INPROJ_TOKEN_TILE // bsz) == 0
    assert TOKEN_TILE % (bsz * SUB_STEPS) == 0 and s % (TOKEN_TILE // bsz) == 0
    n_chunks = s // CHUNK
    gain = norm_gain.reshape(1, D_MODEL).astype(_F32)
    w_bf = w_in.astype(_BF16)
    q, k, v, za, ug, zs = _inproj(x, gain, w_bf)

    tmat, smat, cmat, ar, ai = _ssm_matrices(a_re, a_im, log_dt, b_re, b_im, c_re, c_im,
                                             d_skip)
    yg = _ssm(ug.reshape(N_SSM_GROUPS, n_chunks * bsz, CHUNK_W), tmat, smat, cmat, ar, ai,
              n_chunks, bsz)
    yg = yg.reshape(N_SSM_GROUPS, n_chunks, bsz, CHUNK_W)

    ao = _attention(attn_sink.astype(_F32), q, k, v, bias, za)

    return _merge(x, gain, w_bf, b_gate.reshape(1, -1).astype(_F32), ao, yg, zs, w_glu,
                  b_glu.reshape(1, -1).astype(_F32), w_ba, w_bs, w_out,
                  out_gain.reshape(1, D_MODEL).astype(_F32))


def kernel(x, norm_gain, w_in, b_gate, attn_sink, rel_bias_table, ssm_a_re, ssm_a_im,
           ssm_log_dt, ssm_b_re, ssm_b_im, ssm_c_re, ssm_c_im, ssm_d, w_glu, b_glu,
           w_branch_attn, w_branch_ssm, w_out, final_norm_gain):
    depth = norm_gain.shape[0]
    assert depth == 1, "final norm is fused into the single layer's epilogue"
    bias = _attn_bias(rel_bias_table)
    l = 0
    return _layer(x, norm_gain[l], w_in[l], b_gate[l], attn_sink[l], ssm_a_re[l],
                  ssm_a_im[l], ssm_log_dt[l], ssm_b_re[l], ssm_b_im[l], ssm_c_re[l],
                  ssm_c_im[l], ssm_d[l], w_glu[l], b_glu[l], w_branch_attn[l],
                  w_branch_ssm[l], w_out[l], final_norm_gain, bias)
```
